```python
import math
import jax, jax.numpy as jnp
from jax import lax
import numpy as np

D_MODEL = 1024
BATCH = 8
SEQ = 4096
DEPTH = 1

SSM_D_INNER = D_MODEL
SSM_HEAD_DIM = 64
SSM_HEADS = SSM_D_INNER // SSM_HEAD_DIM
SSM_GROUPS = 2
SSM_STATE = 128
SSM_CHUNK = 128
LRU_WIDTH = D_MODEL
LRU_HEADS = 16
LRU_BLOCK = LRU_WIDTH // LRU_HEADS
LRU_C = 8.0
CONV_WIDTH = 4
D_FF = 4 * D_MODEL
NORM_EPS = 1e-6

SSM_XBC = SSM_D_INNER + 2 * SSM_GROUPS * SSM_STATE
IN_SPLIT_SIZES = [SSM_D_INNER, SSM_XBC, SSM_HEADS, LRU_WIDTH, LRU_WIDTH, D_MODEL, D_MODEL]
IN_SPLIT_IDX = [int(v) for v in np.cumsum(IN_SPLIT_SIZES)[:-1]]
D_IN_PROJ = int(sum(IN_SPLIT_SIZES))

kernel_name = "hybrid_ssd_rglru_gated_block"


def rms_norm(x, g):
    xf = x.astype(jnp.float32)
    var = jnp.mean(xf * xf, axis=-1, keepdims=True)
    return (xf * lax.rsqrt(var + NORM_EPS) * g.astype(jnp.float32)).astype(x.dtype)


def causal_depthwise_conv(x, w, b):
    k, c = w.shape
    out = lax.conv_general_dilated(
        x, w[:, None, :].astype(x.dtype), window_strides=(1,), padding=[(k - 1, 0)],
        dimension_numbers=("NWC", "WIO", "NWC"), feature_group_count=c)
    return out + b.astype(x.dtype)


def ssd_chunked(xs, dt, a, bm, cm):
    b, l, h, p = xs.shape
    g, n = bm.shape[2], bm.shape[3]
    r = h // g
    c = l // SSM_CHUNK
    L = SSM_CHUNK
    x = xs.reshape(b, c, L, g, r, p)
    dtc = dt.reshape(b, c, L, g, r)
    bc = bm.reshape(b, c, L, g, n)
    cc = cm.reshape(b, c, L, g, n)
    a_dt = dtc * a.reshape(g, r)
    a_cum = jnp.cumsum(a_dt, axis=2)
    seg = a_cum[:, :, :, None] - a_cum[:, :, None]
    mask = jnp.tril(jnp.ones((L, L), dtype=bool))[:, :, None, None]
    decay = jnp.exp(jnp.where(mask, seg, -jnp.inf))
    cb = jnp.einsum("bclgn,bcsgn->bclsg", cc, bc)
    wts = cb[..., None] * decay * dtc[:, :, None]
    y_diag = jnp.einsum("bclsgr,bcsgrp->bclgrp", wts, x)
    decay_states = jnp.exp(a_cum[:, :, -1:] - a_cum)
    states = jnp.einsum("bcsgn,bcsgr,bcsgrp->bcgrpn", bc, decay_states * dtc, x)
    chunk_decay = jnp.exp(a_cum[:, :, -1])

    def step(h_prev, inp):
        dec, st = inp
        h_new = dec[..., None, None] * h_prev + st
        return h_new, h_prev

    h0 = jnp.zeros((b, g, r, p, n), dtype=xs.dtype)
    _, prev = lax.scan(step, h0, (jnp.moveaxis(chunk_decay, 1, 0), jnp.moveaxis(states, 1, 0)))
    prev = jnp.moveaxis(prev, 0, 1)
    y_off = jnp.einsum("bclgn,bcgrpn,bclgr->bclgrp", cc, prev, jnp.exp(a_cum))
    return (y_diag + y_off).reshape(b, l, h, p)


def mamba2_branch(z, xbc, dt_raw, conv_w, conv_b, dt_bias, a_log, d_skip, norm_g):
    b, l, _ = z.shape
    xbc = jax.nn.silu(causal_depthwise_conv(xbc, conv_w, conv_b))
    xs, bm, cm = jnp.split(xbc, [SSM_D_INNER, SSM_D_INNER + SSM_GROUPS * SSM_STATE], axis=-1)
    xs = xs.reshape(b, l, SSM_HEADS, SSM_HEAD_DIM)
    bm = bm.reshape(b, l, SSM_GROUPS, SSM_STATE)
    cm = cm.reshape(b, l, SSM_GROUPS, SSM_STATE)
    dt = jax.nn.softplus(dt_raw + dt_bias)
    a = -jnp.exp(a_log)
    y = ssd_chunked(xs, dt, a, bm, cm) + d_skip[:, None] * xs
    y = y.reshape(b, l, SSM_D_INNER) * jax.nn.silu(z)
    yg = y.reshape(b, l, SSM_GROUPS, SSM_D_INNER // SSM_GROUPS)
    yg = yg * lax.rsqrt(jnp.mean(yg * yg, axis=-1, keepdims=True) + NORM_EPS)
    return yg.reshape(b, l, SSM_D_INNER) * norm_g


def rglru_branch(g_in, x_in, conv_w, conv_b, wa, ba, wx, bx, lam):
    b, l, _ = x_in.shape
    xr = causal_depthwise_conv(x_in, conv_w, conv_b)
    xb = xr.reshape(b, l, LRU_HEADS, LRU_BLOCK)
    gate_r = jax.nn.sigmoid(jnp.einsum("blhi,hij->blhj", xb, wa) + ba).reshape(b, l, LRU_WIDTH)
    gate_i = jax.nn.sigmoid(jnp.einsum("blhi,hij->blhj", xb, wx) + bx).reshape(b, l, LRU_WIDTH)
    log_a = -LRU_C * gate_r * jax.nn.softplus(-lam)
    a = jnp.exp(log_a)
    mult = jnp.sqrt(-jnp.expm1(2.0 * log_a))
    u = mult * (gate_i * xr)

    def combine(e1, e2):
        a1, b1 = e1
        a2, b2 = e2
        return a1 * a2, a2 * b1 + b2

    _, h = lax.associative_scan(combine, (a, u), axis=1)
    return h * jax.nn.gelu(g_in, approximate=True)


def _fwd_setup_inputs(seed: int = 0) -> dict:
    key = jax.random.key(seed)
    ks = jax.random.split(key, 24)
    f32 = jnp.float32
    nrm = lambda k, shape, s: jax.random.normal(k, shape, f32) * s
    gain = lambda k, shape: 1.0 + 0.02 * jax.random.normal(k, shape, f32)
    dt0 = jnp.exp(jax.random.uniform(ks[5], (DEPTH, SSM_HEADS), f32, math.log(1e-3), math.log(1e-1)))
    a0 = jax.random.uniform(ks[15], (DEPTH, LRU_WIDTH), f32, 0.9, 0.999)
    s0 = a0 ** (1.0 / LRU_C)
    return {
        "x": jax.random.normal(ks[0], (BATCH, SEQ, D_MODEL), f32),
        "norm_mix_pre": gain(ks[1], (DEPTH, D_MODEL)),
        "w_in": nrm(ks[2], (DEPTH, D_MODEL, D_IN_PROJ), D_MODEL ** -0.5),
        "conv_ssm_w": nrm(ks[3], (DEPTH, CONV_WIDTH, SSM_XBC), CONV_WIDTH ** -0.5),
        "conv_ssm_b": nrm(ks[4], (DEPTH, SSM_XBC), 0.01),
        "dt_bias": dt0 + jnp.log(-jnp.expm1(-dt0)),
        "a_log": jnp.log(jax.random.uniform(ks[6], (DEPTH, SSM_HEADS), f32, 1.0, 16.0)),
        "d_skip": gain(ks[7], (DEPTH, SSM_HEADS)),
        "ssm_norm": gain(ks[8], (DEPTH, SSM_D_INNER)),
        "conv_lru_w": nrm(ks[9], (DEPTH, CONV_WIDTH, LRU_WIDTH), CONV_WIDTH ** -0.5),
        "conv_lru_b": nrm(ks[10], (DEPTH, LRU_WIDTH), 0.01),
        "lru_wa": nrm(ks[11], (DEPTH, LRU_HEADS, LRU_BLOCK, LRU_BLOCK), LRU_BLOCK ** -0.5),
        "lru_ba": nrm(ks[12], (DEPTH, LRU_WIDTH // LRU_BLOCK, LRU_BLOCK), 0.01),
        "lru_wx": nrm(ks[13], (DEPTH, LRU_HEADS, LRU_BLOCK, LRU_BLOCK), LRU_BLOCK ** -0.5),
        "lru_bx": nrm(ks[14], (DEPTH, LRU_WIDTH // LRU_BLOCK, LRU_BLOCK), 0.01),
        "lru_lambda": jnp.log(s0) - jnp.log1p(-s0),
        "w_out": nrm(ks[16], (DEPTH, D_MODEL, D_MODEL), D_MODEL ** -0.5),
        "norm_mix_post": gain(ks[17], (DEPTH, D_MODEL)),
        "norm_mlp_pre": gain(ks[18], (DEPTH, D_MODEL)),
        "w_up": nrm(ks[19], (DEPTH, D_MODEL, D_FF), D_MODEL ** -0.5),
        "w_down": nrm(ks[20], (DEPTH, D_FF, D_MODEL), D_FF ** -0.5),
        "norm_mlp_post": gain(ks[21], (DEPTH, D_MODEL)),
    }


def _fwd_reference(x, norm_mix_pre, w_in, conv_ssm_w, conv_ssm_b, dt_bias, a_log, d_skip,
              ssm_norm, conv_lru_w, conv_lru_b, lru_wa, lru_ba, lru_wx, lru_bx,
              lru_lambda, w_out, norm_mix_post, norm_mlp_pre, w_up, w_down, norm_mlp_post):
    f32 = jnp.float32
    h = x
    for i in range(DEPTH):
        u = rms_norm(h, norm_mix_pre[i])
        proj = jnp.einsum("bsd,de->bse", u, w_in[i]).astype(f32)
        z, xbc, dt_raw, g_lru, x_lru, gate_a, gate_b = jnp.split(proj, IN_SPLIT_IDX, axis=-1)
        y_a = mamba2_branch(z, xbc, dt_raw, conv_ssm_w[i].astype(f32), conv_ssm_b[i].astype(f32),
                            dt_bias[i].astype(f32), a_log[i].astype(f32), d_skip[i].astype(f32),
                            ssm_norm[i].astype(f32))
        y_b = rglru_branch(g_lru, x_lru, conv_lru_w[i].astype(f32), conv_lru_b[i].astype(f32),
                           lru_wa[i].astype(f32), lru_ba[i].astype(f32),
                           lru_wx[i].astype(f32), lru_bx[i].astype(f32), lru_lambda[i].astype(f32))
        merged = (jax.nn.sigmoid(gate_a) * y_a + jax.nn.sigmoid(gate_b) * y_b).astype(h.dtype)
        mix = jnp.einsum("bsd,de->bse", merged, w_out[i])
        h = h + rms_norm(mix, norm_mix_post[i])
        v = rms_norm(h, norm_mlp_pre[i])
        hid = jnp.square(jax.nn.relu(jnp.einsum("bsd,df->bsf", v, w_up[i])))
        ff = jnp.einsum("bsf,fd->bsd", hid, w_down[i])
        h = h + rms_norm(ff, norm_mlp_post[i])
    return h


import jax as _jax
import jax.numpy as _jnp

TWIN_FORMAT = 'train_step'
FWD_PARAMS = ['x', 'norm_mix_pre', 'w_in', 'conv_ssm_w', 'conv_ssm_b', 'dt_bias', 'a_log', 'd_skip', 'ssm_norm', 'conv_lru_w', 'conv_lru_b', 'lru_wa', 'lru_ba', 'lru_wx', 'lru_bx', 'lru_lambda', 'w_out', 'norm_mix_post', 'norm_mlp_pre', 'w_up', 'w_down', 'norm_mlp_post']
TWIN_WEIGHTS = ['norm_mix_pre', 'w_in', 'conv_ssm_w', 'conv_ssm_b', 'dt_bias', 'a_log', 'd_skip', 'ssm_norm', 'conv_lru_w', 'conv_lru_b', 'lru_wa', 'lru_ba', 'lru_wx', 'lru_bx', 'lru_lambda', 'w_out', 'norm_mix_post', 'norm_mlp_pre', 'w_up', 'w_down', 'norm_mlp_post']
TWIN_DIFF_INPUT = 'x'
TWIN_INPUTS = ['x', 'norm_mix_pre', 'w_in', 'conv_ssm_w', 'conv_ssm_b', 'dt_bias', 'a_log', 'd_skip', 'ssm_norm', 'conv_lru_w', 'conv_lru_b', 'lru_wa', 'lru_ba', 'lru_wx', 'lru_bx', 'lru_lambda', 'w_out', 'norm_mix_post', 'norm_mlp_pre', 'w_up', 'w_down', 'norm_mlp_post', 'loss_target', 'm_norm_mix_pre', 'm_w_in', 'm_conv_ssm_w', 'm_conv_ssm_b', 'm_dt_bias', 'm_a_log', 'm_d_skip', 'm_ssm_norm', 'm_conv_lru_w', 'm_conv_lru_b', 'm_lru_wa', 'm_lru_ba', 'm_lru_wx', 'm_lru_bx', 'm_lru_lambda', 'm_w_out', 'm_norm_mix_post', 'm_norm_mlp_pre', 'm_w_up', 'm_w_down', 'm_norm_mlp_post', 'v_norm_mix_pre', 'v_w_in', 'v_conv_ssm_w', 'v_conv_ssm_b', 'v_dt_bias', 'v_a_log', 'v_d_skip', 'v_ssm_norm', 'v_conv_lru_w', 'v_conv_lru_b', 'v_lru_wa', 'v_lru_ba', 'v_lru_wx', 'v_lru_bx', 'v_lru_lambda', 'v_w_out', 'v_norm_mix_post', 'v_norm_mlp_pre', 'v_w_up', 'v_w_down', 'v_norm_mlp_post']
TWIN_OUTPUTS = ['loss', 'grad_x', 'grad_norm_mix_pre', 'grad_w_in', 'grad_conv_ssm_w', 'grad_conv_ssm_b', 'grad_dt_bias', 'grad_a_log', 'grad_d_skip', 'grad_ssm_norm', 'grad_conv_lru_w', 'grad_conv_lru_b', 'grad_lru_wa', 'grad_lru_ba', 'grad_lru_wx', 'grad_lru_bx', 'grad_lru_lambda', 'grad_w_out', 'grad_norm_mix_post', 'grad_norm_mlp_pre', 'grad_w_up', 'grad_w_down', 'grad_norm_mlp_post', 'delta_norm_mix_pre', 'delta_w_in', 'delta_conv_ssm_w', 'delta_conv_ssm_b', 'delta_dt_bias', 'delta_a_log', 'delta_d_skip', 'delta_ssm_norm', 'delta_conv_lru_w', 'delta_conv_lru_b', 'delta_lru_wa', 'delta_lru_ba', 'delta_lru_wx', 'delta_lru_bx', 'delta_lru_lambda', 'delta_w_out', 'delta_norm_mix_post', 'delta_norm_mlp_pre', 'delta_w_up', 'delta_w_down', 'delta_norm_mlp_post', 'new_m_norm_mix_pre', 'new_m_w_in', 'new_m_conv_ssm_w', 'new_m_conv_ssm_b', 'new_m_dt_bias', 'new_m_a_log', 'new_m_d_skip', 'new_m_ssm_norm', 'new_m_conv_lru_w', 'new_m_conv_lru_b', 'new_m_lru_wa', 'new_m_lru_ba', 'new_m_lru_wx', 'new_m_lru_bx', 'new_m_lru_lambda', 'new_m_w_out', 'new_m_norm_mix_post', 'new_m_norm_mlp_pre', 'new_m_w_up', 'new_m_w_down', 'new_m_norm_mlp_post', 'new_v_norm_mix_pre', 'new_v_w_in', 'new_v_conv_ssm_w', 'new_v_conv_ssm_b', 'new_v_dt_bias', 'new_v_a_log', 'new_v_d_skip', 'new_v_ssm_norm', 'new_v_conv_lru_w', 'new_v_conv_lru_b', 'new_v_lru_wa', 'new_v_lru_ba', 'new_v_lru_wx', 'new_v_lru_bx', 'new_v_lru_lambda', 'new_v_w_out', 'new_v_norm_mix_post', 'new_v_norm_mlp_pre', 'new_v_w_up', 'new_v_w_down', 'new_v_norm_mlp_post']
TWIN_LEAF_KINDS = {'loss': 'loss', 'grad_x': 'grad_x', 'grad_norm_mix_pre': 'grad_w', 'grad_w_in': 'grad_w', 'grad_conv_ssm_w': 'grad_w', 'grad_conv_ssm_b': 'grad_w', 'grad_dt_bias': 'grad_w', 'grad_a_log': 'grad_w', 'grad_d_skip': 'grad_w', 'grad_ssm_norm': 'grad_w', 'grad_conv_lru_w': 'grad_w', 'grad_conv_lru_b': 'grad_w', 'grad_lru_wa': 'grad_w', 'grad_lru_ba': 'grad_w', 'grad_lru_wx': 'grad_w', 'grad_lru_bx': 'grad_w', 'grad_lru_lambda': 'grad_w', 'grad_w_out': 'grad_w', 'grad_norm_mix_post': 'grad_w', 'grad_norm_mlp_pre': 'grad_w', 'grad_w_up': 'grad_w', 'grad_w_down': 'grad_w', 'grad_norm_mlp_post': 'grad_w', 'delta_norm_mix_pre': 'delta_w', 'delta_w_in': 'delta_w', 'delta_conv_ssm_w': 'delta_w', 'delta_conv_ssm_b': 'delta_w', 'delta_dt_bias': 'delta_w', 'delta_a_log': 'delta_w', 'delta_d_skip': 'delta_w', 'delta_ssm_norm': 'delta_w', 'delta_conv_lru_w': 'delta_w', 'delta_conv_lru_b': 'delta_w', 'delta_lru_wa': 'delta_w', 'delta_lru_ba': 'delta_w', 'delta_lru_wx': 'delta_w', 'delta_lru_bx': 'delta_w', 'delta_lru_lambda': 'delta_w', 'delta_w_out': 'delta_w', 'delta_norm_mix_post': 'delta_w', 'delta_norm_mlp_pre': 'delta_w', 'delta_w_up': 'delta_w', 'delta_w_down': 'delta_w', 'delta_norm_mlp_post': 'delta_w', 'new_m_norm_mix_pre': 'new_m', 'new_m_w_in': 'new_m', 'new_m_conv_ssm_w': 'new_m', 'new_m_conv_ssm_b': 'new_m', 'new_m_dt_bias': 'new_m', 'new_m_a_log': 'new_m', 'new_m_d_skip': 'new_m', 'new_m_ssm_norm': 'new_m', 'new_m_conv_lru_w': 'new_m', 'new_m_conv_lru_b': 'new_m', 'new_m_lru_wa': 'new_m', 'new_m_lru_ba': 'new_m', 'new_m_lru_wx': 'new_m', 'new_m_lru_bx': 'new_m', 'new_m_lru_lambda': 'new_m', 'new_m_w_out': 'new_m', 'new_m_norm_mix_post': 'new_m', 'new_m_norm_mlp_pre': 'new_m', 'new_m_w_up': 'new_m', 'new_m_w_down': 'new_m', 'new_m_norm_mlp_post': 'new_m', 'new_v_norm_mix_pre': 'new_v', 'new_v_w_in': 'new_v', 'new_v_conv_ssm_w': 'new_v', 'new_v_conv_ssm_b': 'new_v', 'new_v_dt_bias': 'new_v', 'new_v_a_log': 'new_v', 'new_v_d_skip': 'new_v', 'new_v_ssm_norm': 'new_v', 'new_v_conv_lru_w': 'new_v', 'new_v_conv_lru_b': 'new_v', 'new_v_lru_wa': 'new_v', 'new_v_lru_ba': 'new_v', 'new_v_lru_wx': 'new_v', 'new_v_lru_bx': 'new_v', 'new_v_lru_lambda': 'new_v', 'new_v_w_out': 'new_v', 'new_v_norm_mix_post': 'new_v', 'new_v_norm_mlp_pre': 'new_v', 'new_v_w_up': 'new_v', 'new_v_w_down': 'new_v', 'new_v_norm_mlp_post': 'new_v'}


def _forward(args):
    return _fwd_reference(*[args[k] for k in FWD_PARAMS])


def _output_shape():
    def fwd():
        inp = _fwd_setup_inputs(0)
        return _fwd_reference(*[inp[k] for k in FWD_PARAMS])
    out = _jax.eval_shape(fwd)
    return out.shape, out.dtype

N_MICROBATCH = 1
ADAM_LR = 0.001
ADAM_B1 = 0.9
ADAM_B2 = 0.999
ADAM_EPS = 1e-08
ADAM_WD = 0.01
ADAM_STEP = 10
PER_EXAMPLE_BATCH_AXIS = {'x': 0, 'loss_target': 0}
SHARED_INPUTS = []
_WEIGHT_DTYPES = {'norm_mix_pre': _jnp.float32, 'w_in': _jnp.float32, 'conv_ssm_w': _jnp.float32, 'conv_ssm_b': _jnp.float32, 'dt_bias': _jnp.float32, 'a_log': _jnp.float32, 'd_skip': _jnp.float32, 'ssm_norm': _jnp.float32, 'conv_lru_w': _jnp.float32, 'conv_lru_b': _jnp.float32, 'lru_wa': _jnp.float32, 'lru_ba': _jnp.float32, 'lru_wx': _jnp.float32, 'lru_bx': _jnp.float32, 'lru_lambda': _jnp.float32, 'w_out': _jnp.float32, 'norm_mix_post': _jnp.float32, 'norm_mlp_pre': _jnp.float32, 'w_up': _jnp.float32, 'w_down': _jnp.float32, 'norm_mlp_post': _jnp.float32}
MOMENT_SCALE = {'norm_mix_pre': 7.753716e-01, 'w_in': 2.949269e-01, 'conv_ssm_w': 1.025455e+00, 'conv_ssm_b': 3.532028e+00, 'dt_bias': 1.233869e+00, 'a_log': 3.129552e+00, 'd_skip': 5.081582e+00, 'ssm_norm': 1.838033e+00, 'conv_lru_w': 9.787169e-01, 'conv_lru_b': 1.215767e+01, 'lru_wa': 4.493688e-01, 'lru_ba': 2.387567e-01, 'lru_wx': 8.218097e-01, 'lru_bx': 2.636970e-01, 'lru_lambda': 3.235678e-01, 'w_out': 2.406239e+00, 'norm_mix_post': 3.214430e+01, 'norm_mlp_pre': 1.131044e+00, 'w_up': 5.407922e-01, 'w_down': 2.524260e+00, 'norm_mlp_post': 3.292118e+01}


def _to_microbatches(a, axis):
    t = _jnp.moveaxis(a, axis, 0)
    t = t.reshape((N_MICROBATCH, t.shape[0] // N_MICROBATCH) + t.shape[1:])
    return _jnp.moveaxis(t, 1, axis + 1)


def setup_inputs(seed: int = 0) -> dict:
    inp = _fwd_setup_inputs(seed)
    key = _jax.random.fold_in(_jax.random.key(seed), 7919)
    shape, _ = _output_shape()
    out = dict(inp)
    out["loss_target"] = _jax.random.normal(_jax.random.fold_in(key, 0), shape, _jnp.float32)
    for i, name in enumerate(TWIN_WEIGHTS):
        w = inp[name].astype(_jnp.float32)
        if MOMENT_SCALE is None:
            s = _jnp.sqrt(_jnp.mean(_jnp.square(w)) + 1e-30)
        else:
            s = MOMENT_SCALE[name]
        km, kv = _jax.random.split(_jax.random.fold_in(key, i + 1))
        out[name] = w
        out["m_" + name] = s * _jax.random.normal(km, w.shape, _jnp.float32)
        out["v_" + name] = (s * s) * _jax.random.uniform(kv, w.shape, _jnp.float32, 0.5, 1.5)
    if N_MICROBATCH > 1:
        for name, axis in PER_EXAMPLE_BATCH_AXIS.items():
            out[name] = _to_microbatches(out[name], axis)
    return {'x': out['x'], 'norm_mix_pre': out['norm_mix_pre'], 'w_in': out['w_in'], 'conv_ssm_w': out['conv_ssm_w'], 'conv_ssm_b': out['conv_ssm_b'], 'dt_bias': out['dt_bias'], 'a_log': out['a_log'], 'd_skip': out['d_skip'], 'ssm_norm': out['ssm_norm'], 'conv_lru_w': out['conv_lru_w'], 'conv_lru_b': out['conv_lru_b'], 'lru_wa': out['lru_wa'], 'lru_ba': out['lru_ba'], 'lru_wx': out['lru_wx'], 'lru_bx': out['lru_bx'], 'lru_lambda': out['lru_lambda'], 'w_out': out['w_out'], 'norm_mix_post': out['norm_mix_post'], 'norm_mlp_pre': out['norm_mlp_pre'], 'w_up': out['w_up'], 'w_down': out['w_down'], 'norm_mlp_post': out['norm_mlp_post'], 'loss_target': out['loss_target'], 'm_norm_mix_pre': out['m_norm_mix_pre'], 'm_w_in': out['m_w_in'], 'm_conv_ssm_w': out['m_conv_ssm_w'], 'm_conv_ssm_b': out['m_conv_ssm_b'], 'm_dt_bias': out['m_dt_bias'], 'm_a_log': out['m_a_log'], 'm_d_skip': out['m_d_skip'], 'm_ssm_norm': out['m_ssm_norm'], 'm_conv_lru_w': out['m_conv_lru_w'], 'm_conv_lru_b': out['m_conv_lru_b'], 'm_lru_wa': out['m_lru_wa'], 'm_lru_ba': out['m_lru_ba'], 'm_lru_wx': out['m_lru_wx'], 'm_lru_bx': out['m_lru_bx'], 'm_lru_lambda': out['m_lru_lambda'], 'm_w_out': out['m_w_out'], 'm_norm_mix_post': out['m_norm_mix_post'], 'm_norm_mlp_pre': out['m_norm_mlp_pre'], 'm_w_up': out['m_w_up'], 'm_w_down': out['m_w_down'], 'm_norm_mlp_post': out['m_norm_mlp_post'], 'v_norm_mix_pre': out['v_norm_mix_pre'], 'v_w_in': out['v_w_in'], 'v_conv_ssm_w': out['v_conv_ssm_w'], 'v_conv_ssm_b': out['v_conv_ssm_b'], 'v_dt_bias': out['v_dt_bias'], 'v_a_log': out['v_a_log'], 'v_d_skip': out['v_d_skip'], 'v_ssm_norm': out['v_ssm_norm'], 'v_conv_lru_w': out['v_conv_lru_w'], 'v_conv_lru_b': out['v_conv_lru_b'], 'v_lru_wa': out['v_lru_wa'], 'v_lru_ba': out['v_lru_ba'], 'v_lru_wx': out['v_lru_wx'], 'v_lru_bx': out['v_lru_bx'], 'v_lru_lambda': out['v_lru_lambda'], 'v_w_out': out['v_w_out'], 'v_norm_mix_post': out['v_norm_mix_post'], 'v_norm_mlp_pre': out['v_norm_mlp_pre'], 'v_w_up': out['v_w_up'], 'v_w_down': out['v_w_down'], 'v_norm_mlp_post': out['v_norm_mlp_post']}


def _loss(weights, diff, rest, loss_target):
    with _jax.named_scope("forward"):
        args = {**rest, TWIN_DIFF_INPUT: diff, **{k: w.astype(_WEIGHT_DTYPES[k]) for k, w in weights.items()}}
        y = _forward(args)
    with _jax.named_scope("loss_head"):
        err = _jnp.square(y.astype(_jnp.float32) - loss_target)
        return 0.5 * _jnp.sum(_jnp.mean(err, axis=-1)) if err.ndim else 0.5 * err


def _adamw(w, g, m, v):
    m = ADAM_B1 * m + (1.0 - ADAM_B1) * g
    v = ADAM_B2 * v + (1.0 - ADAM_B2) * _jnp.square(g)
    m_hat = m / (1.0 - ADAM_B1 ** ADAM_STEP)
    v_hat = v / (1.0 - ADAM_B2 ** ADAM_STEP)
    delta = -ADAM_LR * (m_hat / (_jnp.sqrt(v_hat) + ADAM_EPS) + ADAM_WD * w)
    return delta, m, v


def reference(x, norm_mix_pre, w_in, conv_ssm_w, conv_ssm_b, dt_bias, a_log, d_skip, ssm_norm, conv_lru_w, conv_lru_b, lru_wa, lru_ba, lru_wx, lru_bx, lru_lambda, w_out, norm_mix_post, norm_mlp_pre, w_up, w_down, norm_mlp_post, loss_target, m_norm_mix_pre, m_w_in, m_conv_ssm_w, m_conv_ssm_b, m_dt_bias, m_a_log, m_d_skip, m_ssm_norm, m_conv_lru_w, m_conv_lru_b, m_lru_wa, m_lru_ba, m_lru_wx, m_lru_bx, m_lru_lambda, m_w_out, m_norm_mix_post, m_norm_mlp_pre, m_w_up, m_w_down, m_norm_mlp_post, v_norm_mix_pre, v_w_in, v_conv_ssm_w, v_conv_ssm_b, v_dt_bias, v_a_log, v_d_skip, v_ssm_norm, v_conv_lru_w, v_conv_lru_b, v_lru_wa, v_lru_ba, v_lru_wx, v_lru_bx, v_lru_lambda, v_w_out, v_norm_mix_post, v_norm_mlp_pre, v_w_up, v_w_down, v_norm_mlp_post):
    given = dict(x=x, norm_mix_pre=norm_mix_pre, w_in=w_in, conv_ssm_w=conv_ssm_w, conv_ssm_b=conv_ssm_b, dt_bias=dt_bias, a_log=a_log, d_skip=d_skip, ssm_norm=ssm_norm, conv_lru_w=conv_lru_w, conv_lru_b=conv_lru_b, lru_wa=lru_wa, lru_ba=lru_ba, lru_wx=lru_wx, lru_bx=lru_bx, lru_lambda=lru_lambda, w_out=w_out, norm_mix_post=norm_mix_post, norm_mlp_pre=norm_mlp_pre, w_up=w_up, w_down=w_down, norm_mlp_post=norm_mlp_post, loss_target=loss_target, m_norm_mix_pre=m_norm_mix_pre, m_w_in=m_w_in, m_conv_ssm_w=m_conv_ssm_w, m_conv_ssm_b=m_conv_ssm_b, m_dt_bias=m_dt_bias, m_a_log=m_a_log, m_d_skip=m_d_skip, m_ssm_norm=m_ssm_norm, m_conv_lru_w=m_conv_lru_w, m_conv_lru_b=m_conv_lru_b, m_lru_wa=m_lru_wa, m_lru_ba=m_lru_ba, m_lru_wx=m_lru_wx, m_lru_bx=m_lru_bx, m_lru_lambda=m_lru_lambda, m_w_out=m_w_out, m_norm_mix_post=m_norm_mix_post, m_norm_mlp_pre=m_norm_mlp_pre, m_w_up=m_w_up, m_w_down=m_w_down, m_norm_mlp_post=m_norm_mlp_post, v_norm_mix_pre=v_norm_mix_pre, v_w_in=v_w_in, v_conv_ssm_w=v_conv_ssm_w, v_conv_ssm_b=v_conv_ssm_b, v_dt_bias=v_dt_bias, v_a_log=v_a_log, v_d_skip=v_d_skip, v_ssm_norm=v_ssm_norm, v_conv_lru_w=v_conv_lru_w, v_conv_lru_b=v_conv_lru_b, v_lru_wa=v_lru_wa, v_lru_ba=v_lru_ba, v_lru_wx=v_lru_wx, v_lru_bx=v_lru_bx, v_lru_lambda=v_lru_lambda, v_w_out=v_w_out, v_norm_mix_post=v_norm_mix_post, v_norm_mlp_pre=v_norm_mlp_pre, v_w_up=v_w_up, v_w_down=v_w_down, v_norm_mlp_post=v_norm_mlp_post)
    weights = {n: given[n] for n in TWIN_WEIGHTS}
    shared = {n: given[n] for n in SHARED_INPUTS}
    per_example = {n: given[n] for n in ['x']}
    grad_fn = _jax.value_and_grad(_loss, argnums=(0, 1))

    def one_microbatch(ex, loss_target):
        ex = dict(ex)
        diff = ex.pop(TWIN_DIFF_INPUT)
        return grad_fn(weights, diff, {**shared, **ex}, loss_target)

    if N_MICROBATCH == 1:
        loss, (grad_w, grad_x) = one_microbatch(per_example, given["loss_target"])
    else:
        def body(carry, xs):
            loss_sum, grad_sum = carry
            l_k, (gw_k, gx_k) = one_microbatch(xs[0], xs[1])
            with _jax.named_scope("update"):
                return (loss_sum + l_k, _jax.tree.map(_jnp.add, grad_sum, gw_k)), gx_k

        init = (_jnp.zeros((), _jnp.float32), _jax.tree.map(_jnp.zeros_like, weights))
        (loss, grad_w), grad_x = _jax.lax.scan(body, init, (per_example, given["loss_target"]))
    with _jax.named_scope("update"):
        delta_w, new_m, new_v = {}, {}, {}
        for n in TWIN_WEIGHTS:
            delta_w[n], new_m[n], new_v[n] = _adamw(weights[n], grad_w[n], given["m_" + n], given["v_" + n])
    return (loss, grad_x, *[grad_w[n] for n in TWIN_WEIGHTS], *[delta_w[n] for n in TWIN_WEIGHTS],
            *[new_m[n] for n in TWIN_WEIGHTS], *[new_v[n] for n in TWIN_WEIGHTS])
```

```python
import functools
import math

import jax
import jax.numpy as jnp
from jax import lax
from jax.experimental import pallas as pl
from jax.experimental.pallas import tpu as pltpu

F32 = jnp.float32
BF16 = jnp.bfloat16
MXU = BF16
HIGH = lax.Precision.HIGHEST

D = 1024
DFF = 4096
NH = 16
HP = 64
NG = 2
NS = 128
CH = 128
XBC = D + 2 * NG * NS
GW = D // NG
LRU_C = 8.0
EPS = 1e-6
NCHIP = 4
W_IN_COLS = 6672
W_IN_SHARD = W_IN_COLS // NCHIP

ADAM_LR = 0.001
ADAM_B1 = 0.9
ADAM_B2 = 0.999
ADAM_EPS = 1e-08
ADAM_WD = 0.01
ADAM_STEP = 10

VMEM_LIMIT = 56 * 1024 * 1024
MESH = pl.DeviceIdType.MESH


def _cp(*sem):
    return pltpu.CompilerParams(dimension_semantics=sem, vmem_limit_bytes=VMEM_LIMIT)


def _dot(a, b, ca=1, cb=0, prec=None):
    return lax.dot_general(a, b, (((ca,), (cb,)), ((), ())), precision=prec, preferred_element_type=F32)


def _mdot(a, b, ca=1, cb=0):
    return _dot(a.astype(MXU), b.astype(MXU), ca, cb)


def _hdot(a, b, ca=1, cb=0):
    return _dot(a, b, ca, cb, HIGH)


def _sig(x):
    return 1.0 / (1.0 + jnp.exp(-x))


def _silu(x):
    return x * _sig(x)


def _dsilu(x):
    s = _sig(x)
    return s * (1.0 + x * (1.0 - s))


def _softplus(x):
    e = jnp.exp(-jnp.abs(x))
    return jnp.maximum(x, 0.0) + jnp.where(e < 1e-4, e * (1.0 - 0.5 * e), jnp.log(1.0 + e))


_GELU_C = math.sqrt(2.0 / math.pi)


def _gelu(x):
    t = jnp.tanh(_GELU_C * (x + 0.044715 * x * x * x))
    return 0.5 * x * (1.0 + t)


def _dgelu(x):
    x2 = x * x
    t = jnp.tanh(_GELU_C * (x + 0.044715 * x * x2))
    return 0.5 * (1.0 + t) + 0.5 * x * (1.0 - t * t) * _GELU_C * (1.0 + 3.0 * 0.044715 * x2)


def _expm1(x):
    small = x * (1.0 + x * (0.5 + x * (1.0 / 6.0 + x * (1.0 / 24.0 + x * (1.0 / 120.0)))))
    return jnp.where(jnp.abs(x) < 0.03, small, jnp.exp(x) - 1.0)


def _rms(x):
    return lax.rsqrt(jnp.mean(x * x, axis=-1, keepdims=True) + EPS)


def _rms_bwd(x, r, g, dy):
    xn = x * r
    dxh = dy * g
    m = jnp.mean(dxh * xn, axis=-1, keepdims=True)
    return r * (dxh - xn * m), jnp.sum(dy * xn, axis=0, keepdims=True)


def _row_spec(t, c, col=0):
    return pl.BlockSpec((t, c), lambda i: (i, col))


def _rev_spec(t, c, n, col=0):
    return pl.BlockSpec((t, c), lambda i: (n - 1 - i, col))


def _full_spec(shape):
    nd = len(shape)
    return pl.BlockSpec(shape, lambda *_: (0,) * nd)


def _sds(shape, dtype=F32):
    return jax.ShapeDtypeStruct(shape, dtype)


def _matmul(a, b, *, name, ta=False, tb=False, tm=512, tn=1024, tk=1024, out_dtype=F32, a_fn=None, epi=None,
            epi_args=()):
    m, k = (a.shape[1], a.shape[0]) if ta else a.shape
    n = b.shape[0] if tb else b.shape[1]
    tm, tn, tk = min(tm, m), min(tn, n), min(tk, k)
    nk = k // tk
    a_spec = pl.BlockSpec((tk, tm), lambda i, j, kk: (kk, i)) if ta else pl.BlockSpec((tm, tk), lambda i, j, kk: (i, kk))
    b_spec = pl.BlockSpec((tn, tk), lambda i, j, kk: (j, kk)) if tb else pl.BlockSpec((tk, tn), lambda i, j, kk: (kk, j))
    e_specs = [pl.BlockSpec((tm, tn), lambda i, j, kk: (i, j)) for _ in epi_args]
    ne = len(epi_args)

    def body(a_ref, b_ref, *rest):
        e_refs, o_ref, acc_ref = rest[:ne], rest[ne], rest[ne + 1]
        kk = pl.program_id(2)

        @pl.when(kk == 0)
        def _():
            acc_ref[...] = jnp.zeros_like(acc_ref)

        av = a_ref[...]
        if a_fn is not None:
            av = a_fn(av)
        acc_ref[...] += _mdot(av, b_ref[...], 0 if ta else 1, 1 if tb else 0)

        @pl.when(kk == nk - 1)
        def _():
            r = acc_ref[...]
            if epi is not None:
                r = epi(r, *[e[...] for e in e_refs])
            o_ref[...] = r.astype(o_ref.dtype)

    return pl.pallas_call(
        body, name=name, grid=(m // tm, n // tn, nk),
        in_specs=[a_spec, b_spec] + e_specs,
        out_specs=pl.BlockSpec((tm, tn), lambda i, j, kk: (i, j)),
        out_shape=_sds((m, n), out_dtype),
        scratch_shapes=[pltpu.VMEM((tm, tn), F32)],
        compiler_params=_cp("parallel", "parallel", "arbitrary"),
    )(a, b, *epi_args)


def _relu2(p):
    p = jnp.maximum(p, 0.0)
    return p * p


def _norm_cast(x, g, name):
    s = x.shape[0]
    t = min(512, s)

    def body(x_ref, g_ref, o_ref):
        xv = x_ref[...]
        o_ref[...] = (xv * _rms(xv) * g_ref[...]).astype(o_ref.dtype)

    return pl.pallas_call(
        body, name=name, grid=(s // t,), in_specs=[_row_spec(t, D), _full_spec((1, D))],
        out_specs=_row_spec(t, D), out_shape=_sds((s, D), MXU), compiler_params=_cp("parallel"),
    )(x, g)


def _conv_fwd(xbc_raw, proj5, dt_raw, cw_s, cb_s, cw_l, cb_l, dt_bias):
    s = xbc_raw.shape[0]
    t = min(256, s)

    def body(xs_ref, xl_ref, dtr_ref, cws_ref, cbs_ref, cwl_ref, cbl_ref, dtb_ref, xc_ref, xr_ref, dt_ref, bs_ref,
             bl_ref):
        @pl.when(pl.program_id(0) == 0)
        def _():
            bs_ref[0:8, :] = jnp.zeros((8, XBC), F32)
            bl_ref[0:8, :] = jnp.zeros((8, D), F32)

        bs_ref[8:t + 8, :] = xs_ref[...]
        bl_ref[8:t + 8, :] = xl_ref[...]

        def conv(buf, w_ref, b_ref):
            acc = b_ref[...] + w_ref[3:4, :] * buf[8:t + 8, :]
            for k in (1, 2, 3):
                acc = acc + w_ref[3 - k:4 - k, :] * buf[8 - k:t + 8 - k, :]
            return acc

        xc_ref[...] = _silu(conv(bs_ref, cws_ref, cbs_ref))
        xr_ref[...] = conv(bl_ref, cwl_ref, cbl_ref)
        dt_ref[...] = _softplus(dtr_ref[...] + dtb_ref[...])
        bs_ref[0:8, :] = bs_ref[t:t + 8, :]
        bl_ref[0:8, :] = bl_ref[t:t + 8, :]

    return pl.pallas_call(
        body, name="conv_fwd", grid=(s // t,),
        in_specs=[_row_spec(t, XBC), _row_spec(t, D, 2), _row_spec(t, 128), _full_spec((4, XBC)),
                  _full_spec((1, XBC)), _full_spec((4, D)), _full_spec((1, D)), _full_spec((1, 128))],
        out_specs=[_row_spec(t, XBC), _row_spec(t, D), _row_spec(t, 128)],
        out_shape=[_sds((s, XBC)), _sds((s, D)), _sds((s, 128))],
        scratch_shapes=[pltpu.VMEM((t + 8, XBC), F32), pltpu.VMEM((t + 8, D), F32)],
        compiler_params=_cp("arbitrary"),
    )(xbc_raw, proj5, dt_raw, cw_s, cb_s, cw_l, cb_l, dt_bias)


def _ssd_chunk_setup(dt_ref, alog_ref, e_ref, at_ref, dtt_ref):
    lane = lax.broadcasted_iota(jnp.int32, (CH, 128), 1)
    row = lax.broadcasted_iota(jnp.int32, (CH, 128), 0)
    lane1 = lax.broadcasted_iota(jnp.int32, (1, 128), 1)
    a = jnp.where(lane1 < NH, -jnp.exp(alog_ref[...]), 0.0)
    dtv = dt_ref[...]
    adt = dtv * a
    tril = row >= lane
    acum = _hdot(tril.astype(F32), adt)
    alast = jnp.sum(adt, axis=0, keepdims=True)
    at_ref[...] = acum.T
    dtt_ref[...] = dtv.T
    e = e_ref[...]
    ea_x = _hdot(jnp.exp(acum), e)
    ws = jnp.exp(alast - acum) * dtv
    ws_x = _hdot(ws, e)
    eal = jnp.exp(alast)
    eal_x = jnp.max(_hdot(jnp.broadcast_to(eal, (8, 128)), e), axis=0, keepdims=True)
    return dict(lane=lane, row=row, tril=tril, a=a, dtv=dtv, acum=acum, alast=alast, ea_x=ea_x, ws=ws, ws_x=ws_x,
                eal=eal, eal_x=eal_x)


def _head_decay(cs, at_ref, dtt_ref, h):
    col = jnp.sum(jnp.where(cs["lane"] == h, cs["acum"], 0.0), axis=1, keepdims=True)
    ld = jnp.where(cs["tril"], jnp.exp(jnp.minimum(col - at_ref[h:h + 1, :], 0.0)), 0.0)
    return ld, dtt_ref[h:h + 1, :]


def _ssd_fwd(xbc_c, dt, proj5, a_log, dskip_x, ssm_norm, expand):
    s = xbc_c.shape[0]
    nc = s // CH

    def body(xc_ref, dt_ref, z_ref, alog_ref, dsk_ref, ng_ref, e_ref, y_ref, ya_ref, st_ref, h_ref, at_ref, dtt_ref,
             yd_ref):
        @pl.when(pl.program_id(0) == 0)
        def _():
            h_ref[...] = jnp.zeros_like(h_ref)

        cs = _ssd_chunk_setup(dt_ref, alog_ref, e_ref, at_ref, dtt_ref)
        lane = cs["lane"]
        for g in range(NG):
            gs = slice(GW * g, GW * (g + 1))
            bg = xc_ref[:, D + NS * g:D + NS * (g + 1)]
            cg = xc_ref[:, D + NG * NS + NS * g:D + NG * NS + NS * (g + 1)]
            cb = _mdot(cg, bg, 1, 1)
            for j in range(4 * g, 4 * g + 4):
                ps = slice(128 * j, 128 * (j + 1))
                xp = xc_ref[:, ps]
                acc = jnp.zeros((CH, 128), F32)
                for hf in range(2):
                    ld, rowdt = _head_decay(cs, at_ref, dtt_ref, 2 * j + hf)
                    hm = (lane >= HP) if hf else (lane < HP)
                    acc = acc + _mdot(cb * ld * rowdt, jnp.where(hm, xp, 0.0))
                yd_ref[:, ps] = acc
            hg = h_ref[:, gs]
            yd_ref[:, gs] += _mdot(cg, hg) * cs["ea_x"][:, gs]
            st = _mdot(bg, xc_ref[:, gs] * cs["ws_x"][:, gs], 0, 0)
            st_ref[0, :, gs] = hg
            h_ref[:, gs] = cs["eal_x"][:, gs] * hg + st
        y = yd_ref[...] + dsk_ref[...] * xc_ref[:, 0:D]
        y_ref[...] = y
        yg = y * _silu(z_ref[...])
        for g in range(NG):
            gs = slice(GW * g, GW * (g + 1))
            seg = yg[:, gs]
            ya_ref[:, gs] = seg * _rms(seg) * ng_ref[:, gs]

    return pl.pallas_call(
        body, name="ssd_fwd", grid=(nc,),
        in_specs=[_row_spec(CH, XBC), _row_spec(CH, 128), _row_spec(CH, D, 0), _full_spec((1, 128)),
                  _full_spec((1, D)), _full_spec((1, D)), _full_spec((128, D))],
        out_specs=[_row_spec(CH, D), _row_spec(CH, D), pl.BlockSpec((1, NS, D), lambda i: (i, 0, 0))],
        out_shape=[_sds((s, D)), _sds((s, D)), _sds((nc, NS, D))],
        scratch_shapes=[pltpu.VMEM((NS, D), F32), pltpu.VMEM((128, 128), F32), pltpu.VMEM((128, 128), F32),
                        pltpu.VMEM((CH, D), F32)],
        compiler_params=_cp("arbitrary"),
    )(xbc_c, dt, proj5, a_log, dskip_x, ssm_norm, expand)


def _lru_gates(xr, wa_ref, wx_ref, ba_ref, bx_ref, lam_ref):
    gr = _sig(_mdot(xr, wa_ref[...]) + ba_ref[...])
    gi = _sig(_mdot(xr, wx_ref[...]) + bx_ref[...])
    sp = _softplus(-lam_ref[...])
    la = -LRU_C * gr * sp
    a = jnp.exp(la)
    mult = jnp.sqrt(-_expm1(2.0 * la))
    return gr, gi, sp, a, mult


def _lru_fwd(xr, proj5, ya, wa_bd, wx_bd, ba, bx, lam):
    s = xr.shape[0]
    t = min(256, s)

    def body(xr_ref, g_ref, ga_ref, gb_ref, ya_ref, wa_ref, wx_ref, ba_ref, bx_ref, lam_ref, h_ref, mg_ref, hc_ref):
        @pl.when(pl.program_id(0) == 0)
        def _():
            hc_ref[...] = jnp.zeros_like(hc_ref)

        xrv = xr_ref[...]
        _, gi, _, a, mult = _lru_gates(xrv, wa_ref, wx_ref, ba_ref, bx_ref, lam_ref)
        u = mult * gi * xrv
        row = lax.broadcasted_iota(jnp.int32, (t, D), 0)
        sh = 1
        while sh < t:
            m = row >= sh
            u = jnp.where(m, a * pltpu.roll(u, sh, 0) + u, u)
            a = jnp.where(m, a * pltpu.roll(a, sh, 0), a)
            sh *= 2
        h = u + a * hc_ref[0:1, :]
        h_ref[...] = h
        hc_ref[0:1, :] = h_ref[t - 1:t, :]
        yb = h * _gelu(g_ref[...])
        mg_ref[...] = (_sig(ga_ref[...]) * ya_ref[...] + _sig(gb_ref[...]) * yb).astype(mg_ref.dtype)

    return pl.pallas_call(
        body, name="lru_fwd", grid=(s // t,),
        in_specs=[_row_spec(t, D), _row_spec(t, D, 1), _row_spec(t, D, 3), _row_spec(t, D, 4), _row_spec(t, D),
                  _full_spec((D, D)), _full_spec((D, D)), _full_spec((1, D)), _full_spec((1, D)), _full_spec((1, D))],
        out_specs=[_row_spec(t, D), _row_spec(t, D)],
        out_shape=[_sds((s, D)), _sds((s, D), MXU)],
        scratch_shapes=[pltpu.VMEM((8, D), F32)],
        compiler_params=_cp("arbitrary"),
    )(xr, proj5, proj5, proj5, ya, wa_bd, wx_bd, ba, bx, lam)


def _out_proj(merged, w_out, x, g2, g3):
    s = x.shape[0]
    t = min(256, s)

    def body(mg_ref, w_ref, x_ref, g2_ref, g3_ref, mix_ref, h1_ref, v_ref):
        mix = _mdot(mg_ref[...], w_ref[...])
        mix_ref[...] = mix
        h1 = x_ref[...] + mix * _rms(mix) * g2_ref[...]
        h1_ref[...] = h1
        v_ref[...] = (h1 * _rms(h1) * g3_ref[...]).astype(v_ref.dtype)

    return pl.pallas_call(
        body, name="out_proj", grid=(s // t,),
        in_specs=[_row_spec(t, D), _full_spec((D, D)), _row_spec(t, D), _full_spec((1, D)), _full_spec((1, D))],
        out_specs=[_row_spec(t, D), _row_spec(t, D), _row_spec(t, D)],
        out_shape=[_sds((s, D)), _sds((s, D)), _sds((s, D), MXU)],
        compiler_params=_cp("parallel"),
    )(merged, w_out, x, g2, g3)


def _down_loss(pre, w_down, h1, target, g4):
    s = pre.shape[0]
    t = min(256, s)

    def body(pre_ref, w_ref, h1_ref, tg_ref, g4_ref, dout_ref, dff_ref, loss_ref, dg4_ref):
        @pl.when(pl.program_id(0) == 0)
        def _():
            loss_ref[...] = jnp.zeros_like(loss_ref)
            dg4_ref[...] = jnp.zeros_like(dg4_ref)

        ff = _mdot(_relu2(pre_ref[...]), w_ref[...])
        r4 = _rms(ff)
        g4v = g4_ref[...]
        diff = h1_ref[...] + ff * r4 * g4v - tg_ref[...]
        sq = jnp.sum(jnp.sum(diff * diff, axis=1, keepdims=True), axis=0, keepdims=True)
        loss_ref[...] += (0.5 / D) * sq
        dout = diff * (1.0 / D)
        dout_ref[...] = dout
        dff, dg = _rms_bwd(ff, r4, g4v, dout)
        dff_ref[...] = dff.astype(dff_ref.dtype)
        dg4_ref[...] += dg

    return pl.pallas_call(
        body, name="down_loss", grid=(s // t,),
        in_specs=[_row_spec(t, DFF), _full_spec((DFF, D)), _row_spec(t, D), _row_spec(t, D), _full_spec((1, D))],
        out_specs=[_row_spec(t, D), _row_spec(t, D), _full_spec((1, 128)), _full_spec((1, D))],
        out_shape=[_sds((s, D)), _sds((s, D), MXU), _sds((1, 128)), _sds((1, D))],
        compiler_params=_cp("arbitrary"),
    )(pre, w_down, h1, target, g4)


def _dv_norms(dpre, w_up, h1, mix, dout, g3, g2):
    s = h1.shape[0]
    t = min(256, s)

    def body(dp_ref, w_ref, h1_ref, mix_ref, dout_ref, g3_ref, g2_ref, dh1_ref, dmix_ref, dg3_ref, dg2_ref):
        @pl.when(pl.program_id(0) == 0)
        def _():
            dg3_ref[...] = jnp.zeros_like(dg3_ref)
            dg2_ref[...] = jnp.zeros_like(dg2_ref)

        dv = _mdot(dp_ref[...], w_ref[...], 1, 1)
        h1 = h1_ref[...]
        dh1n, dg3 = _rms_bwd(h1, _rms(h1), g3_ref[...], dv)
        dh1 = dout_ref[...] + dh1n
        dh1_ref[...] = dh1
        mix = mix_ref[...]
        dmix, dg2 = _rms_bwd(mix, _rms(mix), g2_ref[...], dh1)
        dmix_ref[...] = dmix.astype(dmix_ref.dtype)
        dg3_ref[...] += dg3
        dg2_ref[...] += dg2

    return pl.pallas_call(
        body, name="dv_norms", grid=(s // t,),
        in_specs=[_row_spec(t, DFF), _full_spec((D, DFF)), _row_spec(t, D), _row_spec(t, D), _row_spec(t, D),
                  _full_spec((1, D)), _full_spec((1, D))],
        out_specs=[_row_spec(t, D), _row_spec(t, D), _full_spec((1, D)), _full_spec((1, D))],
        out_shape=[_sds((s, D)), _sds((s, D), MXU), _sds((1, D)), _sds((1, D))],
        compiler_params=_cp("arbitrary"),
    )(dpre, w_up, h1, mix, dout, g3, g2)


def _lru_bwd(dmerged, ya, xr, h, proj5, wa_bd, wx_bd, ba, bx, lam):
    s = xr.shape[0]
    t = min(128, s)
    n = s // t
    rs = functools.partial(_rev_spec, t, D, n)

    def body(dm_ref, ya_ref, xr_ref, h_ref, hp_ref, g_ref, ga_ref, gb_ref, wa_ref, wx_ref, ba_ref, bx_ref, lam_ref,
             dya_ref, dga_ref, dgb_ref, dg_ref, dxr_ref, dpr_ref, dpi_ref, dlam_ref, dba_ref, dbx_ref, gc_ref,
             af_ref):
        i = pl.program_id(0)

        @pl.when(i == 0)
        def _():
            gc_ref[...] = jnp.zeros_like(gc_ref)
            af_ref[...] = jnp.zeros_like(af_ref)
            dlam_ref[...] = jnp.zeros_like(dlam_ref)
            dba_ref[...] = jnp.zeros_like(dba_ref)
            dbx_ref[...] = jnp.zeros_like(dbx_ref)

        xrv = xr_ref[...]
        gr, gi, sp, a, mult = _lru_gates(xrv, wa_ref, wx_ref, ba_ref, bx_ref, lam_ref)
        hv = h_ref[...]
        gv = g_ref[...]
        dm = dm_ref[...]
        yav = ya_ref[...]
        sa = _sig(ga_ref[...])
        sb = _sig(gb_ref[...])
        gel = _gelu(gv)
        dya_ref[...] = dm * sa
        dga_ref[...] = (dm * yav * sa * (1.0 - sa)).astype(dga_ref.dtype)
        dyb = dm * sb
        dgb_ref[...] = (dyb * hv * gel * (1.0 - sb)).astype(dgb_ref.dtype)
        dg_ref[...] = (dyb * hv * _dgelu(gv)).astype(dg_ref.dtype)
        dh = dyb * gel
        row = lax.broadcasted_iota(jnp.int32, (t, D), 0)
        an = jnp.where(row == t - 1, af_ref[0:1, :], pltpu.roll(a, t - 1, 0))
        gacc = dh
        sh = 1
        while sh < t:
            m = row < t - sh
            gacc = jnp.where(m, gacc + an * pltpu.roll(gacc, t - sh, 0), gacc)
            an = jnp.where(m, an * pltpu.roll(an, t - sh, 0), an)
            sh *= 2
        gfull = gacc + an * gc_ref[0:1, :]
        gc_ref[0:1, :] = jnp.sum(jnp.where(row == 0, gfull, 0.0), axis=0, keepdims=True)
        af_ref[0:1, :] = jnp.sum(jnp.where(row == 0, a, 0.0), axis=0, keepdims=True)
        hlast = jnp.where(i == n - 1, 0.0, hp_ref[7:8, :])
        hprev = jnp.where(row == 0, hlast, pltpu.roll(hv, 1, 0))
        da = gfull * hprev
        dmult = gfull * gi * xrv
        dgi = gfull * mult * xrv
        dla = da * a - dmult * a * a / mult
        dgr = dla * (-LRU_C * sp)
        dsp = jnp.sum(dla * (-LRU_C * gr), axis=0, keepdims=True)
        dlam_ref[...] += dsp * (-_sig(-lam_ref[...]))
        dpr = dgr * gr * (1.0 - gr)
        dpi = dgi * gi * (1.0 - gi)
        dpr_ref[...] = dpr.astype(dpr_ref.dtype)
        dpi_ref[...] = dpi.astype(dpi_ref.dtype)
        dba_ref[...] += jnp.sum(dpr, axis=0, keepdims=True)
        dbx_ref[...] += jnp.sum(dpi, axis=0, keepdims=True)
        dxr_ref[...] = gfull * mult * gi + _mdot(dpr, wa_ref[...], 1, 1) + _mdot(dpi, wx_ref[...], 1, 1)

    hp_spec = pl.BlockSpec((8, D), lambda i: (jnp.maximum((n - 1 - i) * (t // 8) - 1, 0), 0))
    return pl.pallas_call(
        body, name="lru_bwd", grid=(n,),
        in_specs=[rs(), rs(), rs(), rs(), hp_spec, rs(1), rs(3), rs(4), _full_spec((D, D)), _full_spec((D, D)),
                  _full_spec((1, D)), _full_spec((1, D)), _full_spec((1, D))],
        out_specs=[rs(), rs(), rs(), rs(), rs(), rs(), rs(), _full_spec((1, D)), _full_spec((1, D)),
                   _full_spec((1, D))],
        out_shape=[_sds((s, D)), _sds((s, D), MXU), _sds((s, D), MXU), _sds((s, D), MXU), _sds((s, D)),
                   _sds((s, D), MXU), _sds((s, D), MXU), _sds((1, D)), _sds((1, D)), _sds((1, D))],
        scratch_shapes=[pltpu.VMEM((8, D), F32), pltpu.VMEM((8, D), F32)],
        compiler_params=_cp("arbitrary"),
    )(dmerged, ya, xr, h, h, proj5, proj5, proj5, wa_bd, wx_bd, ba, bx, lam)


def _ssd_bwd(dya, y, proj5, xbc_c, dt, states, a_log, dskip_x, ssm_norm, expand, reduce_):
    s = xbc_c.shape[0]
    nc = s // CH
    rv = functools.partial(_rev_spec, CH)

    def body(dya_ref, y_ref, z_ref, xc_ref, dt_ref, st_ref, alog_ref, dsk_ref, ng_ref, e_ref, et_ref, dz_ref,
             dxc_ref, ddt_ref, dng_ref, ddsk_ref, dalog_ref, dh_ref, at_ref, dtt_ref, dat_ref, ddtt_ref, dy_ref,
             yoffdy_ref, xbds_ref):
        @pl.when(pl.program_id(0) == 0)
        def _():
            dh_ref[...] = jnp.zeros_like(dh_ref)
            dng_ref[...] = jnp.zeros_like(dng_ref)
            ddsk_ref[...] = jnp.zeros_like(ddsk_ref)
            dalog_ref[...] = jnp.zeros_like(dalog_ref)

        cs = _ssd_chunk_setup(dt_ref, alog_ref, e_ref, at_ref, dtt_ref)
        lane, row = cs["lane"], cs["row"]
        et = et_ref[...]
        for g in range(NG):
            gs = slice(GW * g, GW * (g + 1))
            yv = y_ref[:, gs]
            zv = z_ref[:, gs]
            sz = _silu(zv)
            yg = yv * sz
            dyav = dya_ref[:, gs]
            dyg, dng = _rms_bwd(yg, _rms(yg), ng_ref[:, gs], dyav)
            dng_ref[:, gs] += dng
            dy_ref[:, gs] = dyg * sz
            dz_ref[:, gs] = (dyg * yv * _dsilu(zv)).astype(dz_ref.dtype)
        dyv = dy_ref[...]
        xs = xc_ref[:, 0:D]
        ddsk_ref[...] += jnp.sum(dyv * xs, axis=0, keepdims=True)
        dxc_ref[:, 0:D] = dyv * dsk_ref[...]
        dat_ref[...] = jnp.zeros_like(dat_ref)
        ddtt_ref[...] = jnp.zeros_like(ddtt_ref)
        hh = jnp.sum(dh_ref[...] * st_ref[0], axis=0, keepdims=True)
        deal = jnp.max(_hdot(jnp.broadcast_to(hh, (8, D)), et), axis=0, keepdims=True)
        d_acum = jnp.zeros((CH, 128), F32)
        for g in range(NG):
            gs = slice(GW * g, GW * (g + 1))
            bs_ = slice(D + NS * g, D + NS * (g + 1))
            cs_ = slice(D + NG * NS + NS * g, D + NG * NS + NS * (g + 1))
            bg = xc_ref[:, bs_]
            cg = xc_ref[:, cs_]
            cb = _mdot(cg, bg, 1, 1)
            hg = st_ref[0, :, gs]
            dhg = dh_ref[:, gs]
            dyg_ = dy_ref[:, gs]
            xsg = xc_ref[:, gs]
            ea = cs["ea_x"][:, gs]
            wsx = cs["ws_x"][:, gs]
            dp = dyg_ * ea
            yoffdy_ref[:, gs] = dp * _mdot(cg, hg)
            dc = _mdot(dp, hg, 1, 1)
            dhprev = _mdot(cg, dp, 0, 0)
            bds = _mdot(bg, dhg)
            dxc_ref[:, gs] += wsx * bds
            xbds_ref[:, gs] = xsg * bds
            db = _mdot(xsg * wsx, dhg, 1, 1)
            dh_ref[:, gs] = dhprev + cs["eal_x"][:, gs] * dhg
            dcbs = jnp.zeros((CH, CH), F32)
            for j in range(4 * g, 4 * g + 4):
                ps = slice(128 * j, 128 * (j + 1))
                xp = xc_ref[:, ps]
                dyp = dy_ref[:, ps]
                dxacc = jnp.zeros((CH, 128), F32)
                for hf in range(2):
                    hd = 2 * j + hf
                    ld, rowdt = _head_decay(cs, at_ref, dtt_ref, hd)
                    hm = (lane >= HP) if hf else (lane < HP)
                    dym = jnp.where(hm, dyp, 0.0)
                    w = cb * ld * rowdt
                    dw = _mdot(dym, jnp.where(hm, xp, 0.0), 1, 1)
                    dxacc = dxacc + _mdot(w, dym, 0, 0)
                    nm = dw * w
                    ddtt_ref[hd:hd + 1, :] += jnp.sum(dw * cb * ld, axis=0, keepdims=True)
                    d_acum = d_acum + jnp.where(lane == hd, jnp.sum(nm, axis=1, keepdims=True), 0.0)
                    dat_ref[hd:hd + 1, :] -= jnp.sum(nm, axis=0, keepdims=True)
                    dcbs = dcbs + dw * ld * rowdt
                dxc_ref[:, ps] += dxacc
            dxc_ref[:, bs_] = db + _mdot(dcbs, cg, 0, 0)
            dxc_ref[:, cs_] = dc + _mdot(dcbs, bg)
        dws = _hdot(xbds_ref[...], et)
        ws = cs["ws"]
        d_acum = d_acum - dws * ws + _hdot(yoffdy_ref[...], et) + dat_ref[...].T
        d_alast = jnp.sum(dws * ws, axis=0, keepdims=True) + deal * cs["eal"]
        d_acum = d_acum + jnp.where(row == CH - 1, d_alast, 0.0)
        triu = row <= lane
        d_adt = _hdot(triu.astype(F32), d_acum)
        ddt_ref[...] = dws * jnp.exp(cs["alast"] - cs["acum"]) + ddtt_ref[...].T + d_adt * cs["a"]
        dalog_ref[...] += jnp.sum(d_adt * cs["dtv"], axis=0, keepdims=True) * cs["a"]

    return pl.pallas_call(
        body, name="ssd_bwd", grid=(nc,),
        in_specs=[rv(D, nc), rv(D, nc), rv(D, nc, 0), rv(XBC, nc), rv(128, nc),
                  pl.BlockSpec((1, NS, D), lambda i: (nc - 1 - i, 0, 0)), _full_spec((1, 128)), _full_spec((1, D)),
                  _full_spec((1, D)), _full_spec((128, D)), _full_spec((D, 128))],
        out_specs=[rv(D, nc), rv(XBC, nc), rv(128, nc), _full_spec((1, D)), _full_spec((1, D)),
                   _full_spec((1, 128))],
        out_shape=[_sds((s, D), MXU), _sds((s, XBC)), _sds((s, 128)), _sds((1, D)), _sds((1, D)), _sds((1, 128))],
        scratch_shapes=[pltpu.VMEM((NS, D), F32), pltpu.VMEM((128, 128), F32), pltpu.VMEM((128, 128), F32),
                        pltpu.VMEM((128, 128), F32), pltpu.VMEM((128, 128), F32), pltpu.VMEM((CH, D), F32),
                        pltpu.VMEM((CH, D), F32), pltpu.VMEM((CH, D), F32)],
        compiler_params=_cp("arbitrary"),
    )(dya, y, proj5, xbc_c, dt, states, a_log, dskip_x, ssm_norm, expand, reduce_)


def _conv_bwd(dxbc_c, dxr, ddt, xbc_raw, proj5, dt_raw, cw_s, cb_s, cw_l, dt_bias):
    s = xbc_raw.shape[0]
    t = min(256, s)
    n = s // t

    def prev_spec(c, col=0):
        return pl.BlockSpec((8, c), lambda i: (jnp.maximum((n - 1 - i) * (t // 8) - 1, 0), col))

    def body(dxc_ref, dxr_ref, ddt_ref, xs_ref, xsp_ref, xl_ref, xlp_ref, dtr_ref, cws_ref, cbs_ref, cwl_ref,
             dtb_ref, dxs_ref, dxl_ref, ddtr_ref, dcws_ref, dcbs_ref, dcwl_ref, dcbl_ref, ddtb_ref, ds_ref, dl_ref,
             xsb_ref, xlb_ref):
        i = pl.program_id(0)

        @pl.when(i == 0)
        def _():
            ds_ref[t:t + 8, :] = jnp.zeros((8, XBC), F32)
            dl_ref[t:t + 8, :] = jnp.zeros((8, D), F32)
            for r in (dcws_ref, dcbs_ref, dcwl_ref, dcbl_ref, ddtb_ref):
                r[...] = jnp.zeros_like(r)

        first = i == n - 1
        xsb_ref[0:8, :] = jnp.where(first, 0.0, xsp_ref[...])
        xsb_ref[8:t + 8, :] = xs_ref[...]
        xlb_ref[0:8, :] = jnp.where(first, 0.0, xlp_ref[...])
        xlb_ref[8:t + 8, :] = xl_ref[...]
        pre = cbs_ref[...] + cws_ref[3:4, :] * xsb_ref[8:t + 8, :]
        for k in (1, 2, 3):
            pre = pre + cws_ref[3 - k:4 - k, :] * xsb_ref[8 - k:t + 8 - k, :]
        ds_ref[0:t, :] = dxc_ref[...] * _dsilu(pre)
        dl_ref[0:t, :] = dxr_ref[...]

        def back(dbuf, xbuf, w_ref, dx_ref, dw_ref, db_ref):
            dpre = dbuf[0:t, :]
            dx = w_ref[3:4, :] * dpre
            for k in (1, 2, 3):
                dx = dx + w_ref[3 - k:4 - k, :] * dbuf[k:t + k, :]
            dx_ref[...] = dx.astype(dx_ref.dtype)
            for k in range(4):
                dw_ref[3 - k:4 - k, :] += jnp.sum(dpre * xbuf[8 - k:t + 8 - k, :], axis=0, keepdims=True)
            db_ref[...] += jnp.sum(dpre, axis=0, keepdims=True)
            dbuf[t:t + 8, :] = dbuf[0:8, :]

        back(ds_ref, xsb_ref, cws_ref, dxs_ref, dcws_ref, dcbs_ref)
        back(dl_ref, xlb_ref, cwl_ref, dxl_ref, dcwl_ref, dcbl_ref)
        ddtr = ddt_ref[...] * _sig(dtr_ref[...] + dtb_ref[...])
        ddtr_ref[...] = ddtr.astype(ddtr_ref.dtype)
        ddtb_ref[...] += jnp.sum(ddtr, axis=0, keepdims=True)

    rv = functools.partial(_rev_spec, t)
    return pl.pallas_call(
        body, name="conv_bwd", grid=(n,),
        in_specs=[rv(XBC, n), rv(D, n), rv(128, n), rv(XBC, n), prev_spec(XBC), rv(D, n, 2), prev_spec(D, 2),
                  rv(128, n), _full_spec((4, XBC)), _full_spec((1, XBC)), _full_spec((4, D)), _full_spec((1, 128))],
        out_specs=[rv(XBC, n), rv(D, n), rv(128, n), _full_spec((4, XBC)), _full_spec((1, XBC)), _full_spec((4, D)),
                   _full_spec((1, D)), _full_spec((1, 128))],
        out_shape=[_sds((s, XBC), MXU), _sds((s, D), MXU), _sds((s, 128), MXU), _sds((4, XBC)), _sds((1, XBC)),
                   _sds((4, D)), _sds((1, D)), _sds((1, 128))],
        scratch_shapes=[pltpu.VMEM((t + 8, XBC), F32), pltpu.VMEM((t + 8, D), F32), pltpu.VMEM((t + 8, XBC), F32),
                        pltpu.VMEM((t + 8, D), F32)],
        compiler_params=_cp("arbitrary"),
    )(dxbc_c, dxr, ddt, xbc_raw, xbc_raw, proj5, proj5, dt_raw, cw_s, cb_s, cw_l, dt_bias)


def _du_norm(pieces5, dxbc, ddtr, w5, wxbc, wdt, x, dh1, g1):
    s = x.shape[0]
    t = min(256, s)

    def body(p0, p1, p2, p3, p4, dxbc_ref, ddtr_ref, w5_ref, wx_ref, wd_ref, x_ref, dh1_ref, g1_ref, dx_ref, dg1_ref):
        @pl.when(pl.program_id(0) == 0)
        def _():
            dg1_ref[...] = jnp.zeros_like(dg1_ref)

        du = _mdot(dxbc_ref[...], wx_ref[...], 1, 1) + _mdot(ddtr_ref[...], wd_ref[...], 1, 1)
        for b, p in enumerate((p0, p1, p2, p3, p4)):
            du = du + _mdot(p[...], w5_ref[:, D * b:D * (b + 1)], 1, 1)
        xv = x_ref[...]
        dxn, dg1 = _rms_bwd(xv, _rms(xv), g1_ref[...], du)
        dx_ref[...] = dh1_ref[...] + dxn
        dg1_ref[...] += dg1

    return pl.pallas_call(
        body, name="du_norm", grid=(s // t,),
        in_specs=[_row_spec(t, D)] * 5 + [_row_spec(t, XBC), _row_spec(t, 128), _full_spec((D, 5 * D)),
                                          _full_spec((D, XBC)), _full_spec((D, 128)), _row_spec(t, D),
                                          _row_spec(t, D), _full_spec((1, D))],
        out_specs=[_row_spec(t, D), _full_spec((1, D))],
        out_shape=[_sds((s, D)), _sds((1, D))],
        compiler_params=_cp("arbitrary"),
    )(*pieces5, dxbc, ddtr, w5, wxbc, wdt, x, dh1, g1)


def _adamw(w, g, m, v, name):
    r, c = w.shape
    t = r
    for cand in (512, 256, 128, 64, 32, 16, 8):
        if r % cand == 0 and cand * c <= 512 * 1024:
            t = cand
            break
    bc1 = 1.0 - ADAM_B1 ** ADAM_STEP
    bc2 = 1.0 - ADAM_B2 ** ADAM_STEP

    def body(w_ref, g_ref, m_ref, v_ref, d_ref, nm_ref, nv_ref):
        gv = g_ref[...]
        nm = ADAM_B1 * m_ref[...] + (1.0 - ADAM_B1) * gv
        nv = ADAM_B2 * v_ref[...] + (1.0 - ADAM_B2) * (gv * gv)
        nm_ref[...] = nm
        nv_ref[...] = nv
        d_ref[...] = -ADAM_LR * ((nm / bc1) / (jnp.sqrt(nv / bc2) + ADAM_EPS) + ADAM_WD * w_ref[...])

    spec = pl.BlockSpec((t, c), lambda i: (i, 0))
    return pl.pallas_call(
        body, name=name, grid=(r // t,), in_specs=[spec] * 4, out_specs=[spec] * 3,
        out_shape=[_sds((r, c))] * 3, compiler_params=_cp("parallel"),
    )(w, g, m, v)


def _block_diag(w):
    eye = jnp.eye(NH, dtype=w.dtype)
    return (w[:, :, None, :] * eye[:, None, :, None]).reshape(D, D)


def _diag_blocks(full):
    return jnp.stack([full[HP * h:HP * (h + 1), HP * h:HP * (h + 1)] for h in range(NH)])


def _pad_lanes(v, n=128):
    return jnp.pad(v, ((0, 0), (0, n - v.shape[1])))


def _local_step(x, target, p):
    heads = jnp.arange(D, dtype=jnp.int32) // HP
    expand = (jnp.arange(128, dtype=jnp.int32)[:, None] == heads[None, :]).astype(F32)
    reduce_ = expand.T
    dskip_x = jnp.repeat(p["d_skip"], HP, axis=1)
    a_log = _pad_lanes(p["a_log"])
    dt_bias = _pad_lanes(p["dt_bias"])
    w5, wxbc, wdt = p["w5"], p["wxbc"], p["wdt"]
    wa_bd = _block_diag(p["lru_wa"]).astype(MXU)
    wx_bd = _block_diag(p["lru_wx"]).astype(MXU)
    ba = p["lru_ba"].reshape(1, D)
    bx = p["lru_bx"].reshape(1, D)

    u = _norm_cast(x, p["norm_mix_pre"], "norm_u")
    proj5 = _matmul(u, w5, name="proj5")
    xbc_raw = _matmul(u, wxbc, name="proj_xbc", tn=XBC)
    dt_raw = _matmul(u, wdt, name="proj_dt")
    xbc_c, xr, dt = _conv_fwd(xbc_raw, proj5, dt_raw, p["conv_ssm_w"], p["conv_ssm_b"], p["conv_lru_w"],
                              p["conv_lru_b"], dt_bias)
    y, ya, states = _ssd_fwd(xbc_c, dt, proj5, a_log, dskip_x, p["ssm_norm"], expand)
    h, merged = _lru_fwd(xr, proj5, ya, wa_bd, wx_bd, ba, bx, p["lru_lambda"])
    mix, h1, v = _out_proj(merged, p["w_out"], x, p["norm_mix_post"], p["norm_mlp_pre"])
    pre = _matmul(v, p["w_up"], name="up_proj")
    dout, dff, loss, dg4 = _down_loss(pre, p["w_down"], h1, target, p["norm_mlp_post"])

    dpre = _matmul(dff, p["w_down"], name="d_pre", tb=True, out_dtype=MXU,
                   epi=lambda r, pr: r * (2.0 * jnp.maximum(pr, 0.0)), epi_args=(pre,))
    g_w_down = _matmul(pre, dff, name="dw_down", ta=True, tm=1024, tn=1024, tk=512, a_fn=_relu2)
    dh1, dmix, dg3, dg2 = _dv_norms(dpre, p["w_up"], h1, mix, dout, p["norm_mlp_pre"], p["norm_mix_post"])
    g_w_up = _matmul(v, dpre, name="dw_up", ta=True, tm=1024, tn=1024, tk=512)
    dmerged = _matmul(dmix, p["w_out"], name="d_merged", tb=True)
    g_w_out = _matmul(merged, dmix, name="dw_out", ta=True, tm=1024, tn=1024, tk=512)
    (dya, dga, dgb, dg, dxr, dpr, dpi, dlam, dba, dbx) = _lru_bwd(dmerged, ya, xr, h, proj5, wa_bd, wx_bd, ba, bx,
                                                                  p["lru_lambda"])
    g_wa = _diag_blocks(_matmul(xr, dpr, name="dw_lru_a", ta=True, tm=1024, tn=1024, tk=512))
    g_wx = _diag_blocks(_matmul(xr, dpi, name="dw_lru_x", ta=True, tm=1024, tn=1024, tk=512))
    dz, dxbc_c, ddt, dng, ddsk, dalog = _ssd_bwd(dya, y, proj5, xbc_c, dt, states, a_log, dskip_x, p["ssm_norm"],
                                                 expand, reduce_)
    (dxbc, dxl, ddtr, dcws, dcbs, dcwl, dcbl, ddtb) = _conv_bwd(dxbc_c, dxr, ddt, xbc_raw, proj5, dt_raw,
                                                                p["conv_ssm_w"], p["conv_ssm_b"], p["conv_lru_w"],
                                                                dt_bias)
    pieces5 = (dz, dg, dxl, dga, dgb)
    grad_x, dg1 = _du_norm(pieces5, dxbc, ddtr, w5, wxbc, wdt, x, dh1, p["norm_mix_pre"])
    gw5 = [_matmul(u, pc, name=f"dw_in_{i}", ta=True, tm=1024, tn=1024, tk=512) for i, pc in enumerate(pieces5)]
    gwxbc = _matmul(u, dxbc, name="dw_in_xbc", ta=True, tm=1024, tn=XBC, tk=512)
    gwdt = _matmul(u, ddtr, name="dw_in_dt", ta=True, tm=1024, tn=128, tk=512)
    g_w_in = jnp.concatenate([gw5[0], gwxbc, gwdt[:, :NH], gw5[1], gw5[2], gw5[3], gw5[4]], axis=1)
    grads = {
        "norm_mix_pre": dg1, "w_in": g_w_in, "conv_ssm_w": dcws, "conv_ssm_b": dcbs, "dt_bias": ddtb[:, :NH],
        "a_log": dalog[:, :NH], "d_skip": ddsk.reshape(NH, HP).sum(axis=1)[None, :], "ssm_norm": dng,
        "conv_lru_w": dcwl, "conv_lru_b": dcbl, "lru_wa": g_wa, "lru_ba": dba.reshape(NH, HP), "lru_wx": g_wx,
        "lru_bx": dbx.reshape(NH, HP), "lru_lambda": dlam, "w_out": g_w_out, "norm_mix_post": dg2,
        "norm_mlp_pre": dg3, "w_up": g_w_up, "w_down": g_w_down, "norm_mlp_post": dg4,
    }
    return loss[0, 0], grad_x, grads


def _split_w_in(w_in_full):
    z, xbc, dtc, g, xl, ga, gb = jnp.split(w_in_full, [D, D + XBC, D + XBC + NH, 2 * D + XBC + NH,
                                                        3 * D + XBC + NH, 4 * D + XBC + NH], axis=1)
    return jnp.concatenate([z, g, xl, ga, gb], axis=1), xbc, _pad_lanes(dtc)


ANY = pl.BlockSpec(memory_space=pl.ANY)
COMM = BF16


def _place():
    x, y, c = lax.axis_index("x"), lax.axis_index("y"), lax.axis_index("c")
    chips = [(1 - x, y), (x, 1 - y), (1 - x, 1 - y)]
    return x, y, c, chips


def _remote(src, dst, send_sem, recv_sem, to):
    return pltpu.make_async_remote_copy(src_ref=src, dst_ref=dst, send_sem=send_sem, recv_sem=recv_sem, device_id=to,
                                        device_id_type=MESH)


def _gather_weights(big, small):
    nb, ns = len(big), len(small)
    na = nb + ns

    def body(*refs):
        ins, outs = refs[:na], refs[na:2 * na]
        send_sems, recv_sems, local_sems = refs[2 * na:]
        x, y, c, chips = _place()
        k = 2 * x + y
        local = [pltpu.make_async_copy(ins[a], outs[a].at[k], local_sems.at[a]) for a in range(na)]
        for cp in local:
            cp.start()
        sends = []
        for a in range(na):
            if a < nb:
                hr = ins[a].shape[0] // 2
                src = ins[a].at[pl.ds(c * hr, hr)]
                dst = outs[a].at[k, pl.ds(c * hr, hr)]
            else:
                src, dst = ins[a], outs[a].at[k]
            for j, (cx, cy) in enumerate(chips):
                sends.append(_remote(src, dst, send_sems.at[a, j], recv_sems.at[a, j], (cx, cy, c)))
        for cp in sends:
            cp.start()
        for j, (cx, cy) in enumerate(chips):
            kj = 2 * cx + cy
            for a in range(na):
                if a < nb:
                    hr = ins[a].shape[0] // 2
                    got = outs[a].at[kj, pl.ds(c * hr, hr)]
                    _remote(got, got, send_sems.at[a, j], recv_sems.at[a, j], (cx, cy, c)).wait_recv()
                    fwd = _remote(got, got, send_sems.at[a, 3 + j], recv_sems.at[a, 3 + j], (x, y, 1 - c))
                    fwd.start()
                    sends.append(fwd)
                else:
                    got = outs[a].at[kj]
                    _remote(got, got, send_sems.at[a, j], recv_sems.at[a, j], (cx, cy, c)).wait_recv()
        for j, (cx, cy) in enumerate(chips):
            kj = 2 * cx + cy
            for a in range(nb):
                hr = ins[a].shape[0] // 2
                got = outs[a].at[kj, pl.ds((1 - c) * hr, hr)]
                _remote(got, got, send_sems.at[a, 3 + j], recv_sems.at[a, 3 + j], (x, y, 1 - c)).wait_recv()
        for cp in sends:
            cp.wait_send()
        for cp in local:
            cp.wait()

    arrs = list(big) + list(small)
    return pl.pallas_call(
        body, name="gather_weights", in_specs=[ANY] * na, out_specs=[ANY] * na,
        out_shape=[_sds((NCHIP,) + a.shape, a.dtype) for a in arrs],
        scratch_shapes=[pltpu.SemaphoreType.DMA((na, 6)), pltpu.SemaphoreType.DMA((na, 6)),
                        pltpu.SemaphoreType.DMA((na,))],
    )(*arrs)


def _pair_exchange(gs):
    na = len(gs)

    def body(*refs):
        ins, outs = refs[:na], refs[na:2 * na]
        send_sems, recv_sems = refs[2 * na:]
        x, y, c, _ = _place()
        cps = []
        for a in range(na):
            hr = ins[a].shape[1] // 2
            cps.append(_remote(ins[a].at[:, pl.ds((1 - c) * hr, hr)], outs[a], send_sems.at[a], recv_sems.at[a],
                               (x, y, 1 - c)))
        for cp in cps:
            cp.start()
        for cp in cps:
            cp.wait()

    return pl.pallas_call(
        body, name="grad_pair_exchange", in_specs=[ANY] * na, out_specs=[ANY] * na,
        out_shape=[_sds((NCHIP, g.shape[1] // 2, g.shape[2]), g.dtype) for g in gs],
        scratch_shapes=[pltpu.SemaphoreType.DMA((na,)), pltpu.SemaphoreType.DMA((na,))],
    )(*gs)


def _pair_add(g, got, cidx, name):
    _, r, cdim = g.shape
    hr = r // 2
    t = 256 if hr % 256 == 0 else 128
    nt = hr // t

    def body(c_ref, g_ref, o_ref, p_ref, pc_ref):
        sm = g_ref[...] + o_ref[...]
        p_ref[...] = sm
        pc_ref[...] = sm.astype(pc_ref.dtype)

    spec = pl.BlockSpec((1, t, cdim), lambda k, i, c_ref: (k, i, 0))
    return pl.pallas_call(
        body, name=name,
        grid_spec=pltpu.PrefetchScalarGridSpec(
            num_scalar_prefetch=1, grid=(NCHIP, nt),
            in_specs=[pl.BlockSpec((1, t, cdim), lambda k, i, c_ref: (k, c_ref[0] * nt + i, 0)), spec],
            out_specs=[spec, spec]),
        out_shape=[_sds((NCHIP, hr, cdim)), _sds((NCHIP, hr, cdim), COMM)],
        compiler_params=_cp("parallel", "parallel"),
    )(cidx, g, got)


def _chip_exchange(ps):
    na = len(ps)

    def body(*refs):
        ins, outs = refs[:na], refs[na:2 * na]
        send_sems, recv_sems = refs[2 * na:]
        _, _, c, chips = _place()
        cps = []
        for a in range(na):
            for j, (cx, cy) in enumerate(chips):
                cps.append(_remote(ins[a].at[2 * cx + cy], outs[a].at[j], send_sems.at[a, j], recv_sems.at[a, j],
                                   (cx, cy, c)))
        for cp in cps:
            cp.start()
        for cp in cps:
            cp.wait()

    return pl.pallas_call(
        body, name="grad_chip_exchange", in_specs=[ANY] * na, out_specs=[ANY] * na,
        out_shape=[_sds((NCHIP - 1,) + p.shape[1:], p.dtype) for p in ps],
        scratch_shapes=[pltpu.SemaphoreType.DMA((na, 3)), pltpu.SemaphoreType.DMA((na, 3))],
    )(*ps)


def _shard_sum(p, got, kidx, name):
    _, hr, cdim = p.shape
    t = 256 if hr % 256 == 0 else 128

    def body(k_ref, p_ref, g_ref, o_ref):
        sm = p_ref[0]
        for j in range(NCHIP - 1):
            sm = sm + g_ref[j].astype(F32)
        o_ref[...] = sm

    return pl.pallas_call(
        body, name=name,
        grid_spec=pltpu.PrefetchScalarGridSpec(
            num_scalar_prefetch=1, grid=(hr // t,),
            in_specs=[pl.BlockSpec((1, t, cdim), lambda i, k_ref: (k_ref[0], i, 0)),
                      pl.BlockSpec((NCHIP - 1, t, cdim), lambda i, k_ref: (0, i, 0))],
            out_specs=pl.BlockSpec((t, cdim), lambda i, k_ref: (i, 0))),
        out_shape=_sds((hr, cdim)),
        compiler_params=_cp("parallel"),
    )(kidx, p, got)


def _pair_gather(rs):
    na = len(rs)

    def body(*refs):
        ins, outs = refs[:na], refs[na:2 * na]
        send_sems, recv_sems, local_sems = refs[2 * na:]
        x, y, c, _ = _place()
        cps, local = [], []
        for a in range(na):
            local.append(pltpu.make_async_copy(ins[a], outs[a].at[c], local_sems.at[a]))
            cps.append(_remote(ins[a], outs[a].at[c], send_sems.at[a], recv_sems.at[a], (x, y, 1 - c)))
        for cp in local + cps:
            cp.start()
        for a in range(na):
            cps[a].wait_send()
            _remote(ins[a], outs[a].at[1 - c], send_sems.at[a], recv_sems.at[a], (x, y, 1 - c)).wait_recv()
            local[a].wait()

    return pl.pallas_call(
        body, name="grad_pair_gather", in_specs=[ANY] * na, out_specs=[ANY] * na,
        out_shape=[_sds((2,) + r.shape, r.dtype) for r in rs],
        scratch_shapes=[pltpu.SemaphoreType.DMA((na,)), pltpu.SemaphoreType.DMA((na,)),
                        pltpu.SemaphoreType.DMA((na,))],
    )(*rs)


def _small_allgather(v):
    r = v.shape[0]

    def body(v_ref, out_ref, send_sems, recv_sems, local_sem):
        x, y, c, chips = _place()
        me, sibling = (x, y, c), (x, y, 1 - c)

        def slot(px, py, pc):
            return out_ref.at[4 * px + 2 * py + pc]

        def copy(k, block, to, src=None):
            return _remote(slot(*block) if src is None else src, slot(*block), send_sems.at[k], recv_sems.at[k], to)

        mine = pltpu.make_async_copy(v_ref, slot(*me), local_sem)
        mine.start()
        first = [copy(0, me, sibling, src=v_ref)]
        first += [copy(1 + j, me, (*chip, c), src=v_ref) for j, chip in enumerate(chips)]
        for cp in first:
            cp.start()
        passed = [copy(4 + j, (*chip, c), sibling) for j, chip in enumerate(chips)]
        for j, chip in enumerate(chips):
            copy(1 + j, (*chip, c), me).wait_recv()
            passed[j].start()
        copy(0, sibling, me).wait_recv()
        for j, chip in enumerate(chips):
            copy(4 + j, (*chip, 1 - c), me).wait_recv()
        for cp in first + passed:
            cp.wait_send()
        mine.wait()

    vm = pl.BlockSpec(memory_space=pltpu.VMEM)
    return pl.pallas_call(
        body, name="small_allgather", in_specs=[vm], out_specs=vm, out_shape=_sds((8, r, 128)),
        scratch_shapes=[pltpu.SemaphoreType.DMA((7,)), pltpu.SemaphoreType.DMA((7,)), pltpu.SemaphoreType.DMA],
        compiler_params=pltpu.CompilerParams(vmem_limit_bytes=VMEM_LIMIT),
    )(v)


def _sum_devices(allv):
    _, r, _ = allv.shape

    def body(a_ref, o_ref):
        sm = a_ref[0]
        for d in range(1, 8):
            sm = sm + a_ref[d]
        o_ref[...] = sm

    return pl.pallas_call(
        body, name="small_sum", grid=(1,), in_specs=[_full_spec((8, r, 128))], out_specs=_full_spec((r, 128)),
        out_shape=_sds((r, 128)), compiler_params=_cp("arbitrary"),
    )(allv)


def _pack(arrs):
    rows = []
    for a in arrs:
        f = a.reshape(-1)
        f = jnp.pad(f, (0, (-f.shape[0]) % 128))
        rows.append(f.reshape(-1, 128))
    out = jnp.concatenate(rows, axis=0)
    return jnp.pad(out, ((0, (-out.shape[0]) % 8), (0, 0)))


def _unpack(packed, shapes):
    outs, r0 = [], 0
    for shp in shapes:
        n = math.prod(shp)
        nr = -(-n // 128)
        outs.append(packed[r0:r0 + nr].reshape(-1)[:n].reshape(shp))
        r0 += nr
    return outs


BIG = ("w_in", "w_out", "w_up", "w_down")
CONV = ("conv_ssm_w", "conv_lru_w")
WEIGHTS = ("norm_mix_pre", "w_in", "conv_ssm_w", "conv_ssm_b", "dt_bias", "a_log", "d_skip", "ssm_norm", "conv_lru_w",
           "conv_lru_b", "lru_wa", "lru_ba", "lru_wx", "lru_bx", "lru_lambda", "w_out", "norm_mix_post",
           "norm_mlp_pre", "w_up", "w_down", "norm_mlp_post")
SMALL = tuple(n for n in WEIGHTS if n not in BIG and n not in CONV)


def kernel(x, norm_mix_pre, w_in, conv_ssm_w, conv_ssm_b, dt_bias, a_log, d_skip, ssm_norm, conv_lru_w, conv_lru_b, lru_wa, lru_ba, lru_wx, lru_bx, lru_lambda, w_out, norm_mix_post, norm_mlp_pre, w_up, w_down, norm_mlp_post, loss_target, m_norm_mix_pre, m_w_in, m_conv_ssm_w, m_conv_ssm_b, m_dt_bias, m_a_log, m_d_skip, m_ssm_norm, m_conv_lru_w, m_conv_lru_b, m_lru_wa, m_lru_ba, m_lru_wx, m_lru_bx, m_lru_lambda, m_w_out, m_norm_mix_post, m_norm_mlp_pre, m_w_up, m_w_down, m_norm_mlp_post, v_norm_mix_pre, v_w_in, v_conv_ssm_w, v_conv_ssm_b, v_dt_bias, v_a_log, v_d_skip, v_ssm_norm, v_conv_lru_w, v_conv_lru_b, v_lru_wa, v_lru_ba, v_lru_wx, v_lru_bx, v_lru_lambda, v_w_out, v_norm_mix_post, v_norm_mlp_pre, v_w_up, v_w_down, v_norm_mlp_post):
    args = locals()
    w = {n: args[n][0] for n in WEIGHTS}
    m = {n: args["m_" + n][0] for n in WEIGHTS}
    v = {n: args["v_" + n][0] for n in WEIGHTS}
    cidx = lax.axis_index("c").astype(jnp.int32).reshape(1)
    kchip = 2 * lax.axis_index("x") + lax.axis_index("y")
    kidx = kchip.astype(jnp.int32).reshape(1)

    gathered = _gather_weights([w[n].astype(MXU) for n in BIG], [w[n] for n in CONV])
    cat = lambda g: jnp.concatenate([g[k] for k in range(NCHIP)], axis=1)
    w5, wxbc, wdt = _split_w_in(cat(gathered[0]))
    p = {n: (w[n].reshape(1, -1) if w[n].ndim == 1 else w[n]) for n in SMALL}
    p.update(w5=w5, wxbc=wxbc, wdt=wdt, w_out=gathered[1].reshape(D, D), w_up=cat(gathered[2]),
             w_down=gathered[3].reshape(DFF, D), conv_ssm_w=cat(gathered[4]), conv_lru_w=cat(gathered[5]))

    loss, grad_x, g = _local_step(x[0], loss_target[0], p)
    loss = lax.psum(loss, ("x", "y", "c"))

    shard_major = [
        jnp.stack([g["w_in"][:, W_IN_SHARD * k:W_IN_SHARD * (k + 1)] for k in range(NCHIP)]),
        g["w_out"].reshape(NCHIP, D // NCHIP, D),
        jnp.stack([g["w_up"][:, D * k:D * (k + 1)] for k in range(NCHIP)]),
        g["w_down"].reshape(NCHIP, D, D),
    ]
    from_sibling = _pair_exchange(shard_major)
    pair = [_pair_add(gs, got, cidx, f"grad_pair_add_{n}") for gs, got, n in zip(shard_major, from_sibling, BIG)]
    from_chips = _chip_exchange([pc for _, pc in pair])
    halves = [_shard_sum(pf, got, kidx, f"grad_shard_sum_{n}") for (pf, _), got, n in zip(pair, from_chips, BIG)]
    reduced = {n: r.reshape(w[n].shape) for n, r in zip(BIG, _pair_gather(halves))}

    small_shapes = [g[n].shape for n in SMALL + CONV]
    summed = _unpack(_sum_devices(_small_allgather(_pack([g[n] for n in SMALL + CONV]))), small_shapes)
    for n, s in zip(SMALL + CONV, summed):
        if n in CONV:
            width = w[n].shape[1]
            reduced[n] = lax.dynamic_slice_in_dim(s, kchip * width, width, axis=1)
        else:
            reduced[n] = s.reshape(w[n].shape)

    delta, new_m, new_v = {}, {}, {}
    for n in BIG + CONV:
        delta[n], new_m[n], new_v[n] = _adamw(w[n], reduced[n], m[n], v[n], f"adamw_{n}")
    shapes = [w[n].shape for n in SMALL]
    packed = [_pack([d[n] for n in SMALL]) for d in (w, reduced, m, v)]
    for d, out in zip((delta, new_m, new_v), _adamw(*packed, "adamw_small")):
        d.update(zip(SMALL, _unpack(out, shapes)))

    lead = lambda d: [d[n][None] for n in WEIGHTS]
    return (loss, grad_x[None], *lead(reduced), *lead(delta), *lead(new_m), *lead(new_v))
```

```python
import functools
import math

import jax
import jax.numpy as jnp
from jax import lax
from jax.experimental import pallas as pl
from jax.experimental.pallas import tpu as pltpu

F32 = jnp.float32
BF16 = jnp.bfloat16
MXU = BF16
HIGH = lax.Precision.HIGHEST

D = 1024
DFF = 4096
NH = 16
HP = 64
NG = 2
NS = 128
CH = 128
XBC = D + 2 * NG * NS
GW = D // NG
LRU_C = 8.0
EPS = 1e-6
NCHIP = 4
W_IN_COLS = 6672
W_IN_SHARD = W_IN_COLS // NCHIP

ADAM_LR = 0.001
ADAM_B1 = 0.9
ADAM_B2 = 0.999
ADAM_EPS = 1e-08
ADAM_WD = 0.01
ADAM_STEP = 10

VMEM_LIMIT = 56 * 1024 * 1024
MESH = pl.DeviceIdType.MESH


def _cp(*sem):
    return pltpu.CompilerParams(dimension_semantics=sem, vmem_limit_bytes=VMEM_LIMIT)


def _dot(a, b, ca=1, cb=0, prec=None):
    return lax.dot_general(a, b, (((ca,), (cb,)), ((), ())), precision=prec, preferred_element_type=F32)


def _mdot(a, b, ca=1, cb=0):
    return _dot(a.astype(MXU), b.astype(MXU), ca, cb)


def _hdot(a, b, ca=1, cb=0):
    return _dot(a, b, ca, cb, HIGH)


def _sig(x):
    return 1.0 / (1.0 + jnp.exp(-x))


def _silu(x):
    return x * _sig(x)


def _dsilu(x):
    s = _sig(x)
    return s * (1.0 + x * (1.0 - s))


def _softplus(x):
    e = jnp.exp(-jnp.abs(x))
    return jnp.maximum(x, 0.0) + jnp.where(e < 1e-4, e * (1.0 - 0.5 * e), jnp.log(1.0 + e))


_GELU_C = math.sqrt(2.0 / math.pi)


def _gelu(x):
    t = jnp.tanh(_GELU_C * (x + 0.044715 * x * x * x))
    return 0.5 * x * (1.0 + t)


def _dgelu(x):
    x2 = x * x
    t = jnp.tanh(_GELU_C * (x + 0.044715 * x * x2))
    return 0.5 * (1.0 + t) + 0.5 * x * (1.0 - t * t) * _GELU_C * (1.0 + 3.0 * 0.044715 * x2)


def _expm1(x):
    small = x * (1.0 + x * (0.5 + x * (1.0 / 6.0 + x * (1.0 / 24.0 + x * (1.0 / 120.0)))))
    return jnp.where(jnp.abs(x) < 0.03, small, jnp.exp(x) - 1.0)


def _rms(x):
    return lax.rsqrt(jnp.mean(x * x, axis=-1, keepdims=True) + EPS)


def _rms_bwd(x, r, g, dy):
    xn = x * r
    dxh = dy * g
    m = jnp.mean(dxh * xn, axis=-1, keepdims=True)
    return r * (dxh - xn * m), jnp.sum(dy * xn, axis=0, keepdims=True)


def _row_spec(t, c, col=0):
    return pl.BlockSpec((t, c), lambda i: (i, col))


def _rev_spec(t, c, n, col=0):
    return pl.BlockSpec((t, c), lambda i: (n - 1 - i, col))


def _full_spec(shape):
    nd = len(shape)
    return pl.BlockSpec(shape, lambda *_: (0,) * nd)


def _sds(shape, dtype=F32):
    return jax.ShapeDtypeStruct(shape, dtype)


def _matmul(a, b, *, name, ta=False, tb=False, tm=512, tn=1024, tk=1024, out_dtype=F32, a_fn=None, epi=None,
            epi_args=()):
    m, k = (a.shape[1], a.shape[0]) if ta else a.shape
    n = b.shape[0] if tb else b.shape[1]
    tm, tn, tk = min(tm, m), min(tn, n), min(tk, k)
    nk = k // tk
    a_spec = pl.BlockSpec((tk, tm), lambda i, j, kk: (kk, i)) if ta else pl.BlockSpec((tm, tk), lambda i, j, kk: (i, kk))
    b_spec = pl.BlockSpec((tn, tk), lambda i, j, kk: (j, kk)) if tb else pl.BlockSpec((tk, tn), lambda i, j, kk: (kk, j))
    e_specs = [pl.BlockSpec((tm, tn), lambda i, j, kk: (i, j)) for _ in epi_args]
    ne = len(epi_args)

    def body(a_ref, b_ref, *rest):
        e_refs, o_ref, acc_ref = rest[:ne], rest[ne], rest[ne + 1]
        kk = pl.program_id(2)

        @pl.when(kk == 0)
        def _():
            acc_ref[...] = jnp.zeros_like(acc_ref)

        av = a_ref[...]
        if a_fn is not None:
            av = a_fn(av)
        acc_ref[...] += _mdot(av, b_ref[...], 0 if ta else 1, 1 if tb else 0)

        @pl.when(kk == nk - 1)
        def _():
            r = acc_ref[...]
            if epi is not None:
                r = epi(r, *[e[...] for e in e_refs])
            o_ref[...] = r.astype(o_ref.dtype)

    return pl.pallas_call(
        body, name=name, grid=(m // tm, n // tn, nk),
        in_specs=[a_spec, b_spec] + e_specs,
        out_specs=pl.BlockSpec((tm, tn), lambda i, j, kk: (i, j)),
        out_shape=_sds((m, n), out_dtype),
        scratch_shapes=[pltpu.VMEM((tm, tn), F32)],
        compiler_params=_cp("parallel", "parallel", "arbitrary"),
    )(a, b, *epi_args)


def _relu2(p):
    p = jnp.maximum(p, 0.0)
    return p * p


def _norm_cast(x, g, name):
    s = x.shape[0]
    t = min(512, s)

    def body(x_ref, g_ref, o_ref):
        xv = x_ref[...]
        o_ref[...] = (xv * _rms(xv) * g_ref[...]).astype(o_ref.dtype)

    return pl.pallas_call(
        body, name=name, grid=(s // t,), in_specs=[_row_spec(t, D), _full_spec((1, D))],
        out_specs=_row_spec(t, D), out_shape=_sds((s, D), MXU), compiler_params=_cp("parallel"),
    )(x, g)


def _conv_fwd(xbc_raw, proj5, dt_raw, cw_s, cb_s, cw_l, cb_l, dt_bias):
    s = xbc_raw.shape[0]
    t = min(256, s)

    def body(xs_ref, xl_ref, dtr_ref, cws_ref, cbs_ref, cwl_ref, cbl_ref, dtb_ref, xc_ref, xr_ref, dt_ref, bs_ref,
             bl_ref):
        @pl.when(pl.program_id(0) == 0)
        def _():
            bs_ref[0:8, :] = jnp.zeros((8, XBC), F32)
            bl_ref[0:8, :] = jnp.zeros((8, D), F32)

        bs_ref[8:t + 8, :] = xs_ref[...]
        bl_ref[8:t + 8, :] = xl_ref[...]

        def conv(buf, w_ref, b_ref):
            acc = b_ref[...] + w_ref[3:4, :] * buf[8:t + 8, :]
            for k in (1, 2, 3):
                acc = acc + w_ref[3 - k:4 - k, :] * buf[8 - k:t + 8 - k, :]
            return acc

        xc_ref[...] = _silu(conv(bs_ref, cws_ref, cbs_ref))
        xr_ref[...] = conv(bl_ref, cwl_ref, cbl_ref)
        dt_ref[...] = _softplus(dtr_ref[...] + dtb_ref[...])
        bs_ref[0:8, :] = bs_ref[t:t + 8, :]
        bl_ref[0:8, :] = bl_ref[t:t + 8, :]

    return pl.pallas_call(
        body, name="conv_fwd", grid=(s // t,),
        in_specs=[_row_spec(t, XBC), _row_spec(t, D, 2), _row_spec(t, 128), _full_spec((4, XBC)),
                  _full_spec((1, XBC)), _full_spec((4, D)), _full_spec((1, D)), _full_spec((1, 128))],
        out_specs=[_row_spec(t, XBC), _row_spec(t, D), _row_spec(t, 128)],
        out_shape=[_sds((s, XBC)), _sds((s, D)), _sds((s, 128))],
        scratch_shapes=[pltpu.VMEM((t + 8, XBC), F32), pltpu.VMEM((t + 8, D), F32)],
        compiler_params=_cp("arbitrary"),
    )(xbc_raw, proj5, dt_raw, cw_s, cb_s, cw_l, cb_l, dt_bias)


def _ssd_chunk_setup(dt_ref, alog_ref, e_ref, at_ref, dtt_ref):
    lane = lax.broadcasted_iota(jnp.int32, (CH, 128), 1)
    row = lax.broadcasted_iota(jnp.int32, (CH, 128), 0)
    lane1 = lax.broadcasted_iota(jnp.int32, (1, 128), 1)
    a = jnp.where(lane1 < NH, -jnp.exp(alog_ref[...]), 0.0)
    dtv = dt_ref[...]
    adt = dtv * a
    tril = row >= lane
    acum = _hdot(tril.astype(F32), adt)
    alast = jnp.sum(adt, axis=0, keepdims=True)
    at_ref[...] = acum.T
    dtt_ref[...] = dtv.T
    e = e_ref[...]
    ea_x = _hdot(jnp.exp(acum), e)
    ws = jnp.exp(alast - acum) * dtv
    ws_x = _hdot(ws, e)
    eal = jnp.exp(alast)
    eal_x = jnp.max(_hdot(jnp.broadcast_to(eal, (8, 128)), e), axis=0, keepdims=True)
    return dict(lane=lane, row=row, tril=tril, a=a, dtv=dtv, acum=acum, alast=alast, ea_x=ea_x, ws=ws, ws_x=ws_x,
                eal=eal, eal_x=eal_x)


def _head_decay(cs, at_ref, dtt_ref, h):
    col = jnp.sum(jnp.where(cs["lane"] == h, cs["acum"], 0.0), axis=1, keepdims=True)
    ld = jnp.where(cs["tril"], jnp.exp(jnp.minimum(col - at_ref[h:h + 1, :], 0.0)), 0.0)
    return ld, dtt_ref[h:h + 1, :]


def _ssd_fwd(xbc_c, dt, proj5, a_log, dskip_x, ssm_norm, expand):
    s = xbc_c.shape[0]
    nc = s // CH

    def body(xc_ref, dt_ref, z_ref, alog_ref, dsk_ref, ng_ref, e_ref, y_ref, ya_ref, st_ref, h_ref, at_ref, dtt_ref,
             yd_ref):
        @pl.when(pl.program_id(0) == 0)
        def _():
            h_ref[...] = jnp.zeros_like(h_ref)

        cs = _ssd_chunk_setup(dt_ref, alog_ref, e_ref, at_ref, dtt_ref)
        lane = cs["lane"]
        for g in range(NG):
            gs = slice(GW * g, GW * (g + 1))
            bg = xc_ref[:, D + NS * g:D + NS * (g + 1)]
            cg = xc_ref[:, D + NG * NS + NS * g:D + NG * NS + NS * (g + 1)]
            cb = _mdot(cg, bg, 1, 1)
            for j in range(4 * g, 4 * g + 4):
                ps = slice(128 * j, 128 * (j + 1))
                xp = xc_ref[:, ps]
                acc = jnp.zeros((CH, 128), F32)
                for hf in range(2):
                    ld, rowdt = _head_decay(cs, at_ref, dtt_ref, 2 * j + hf)
                    hm = (lane >= HP) if hf else (lane < HP)
                    acc = acc + _mdot(cb * ld * rowdt, jnp.where(hm, xp, 0.0))
                yd_ref[:, ps] = acc
            hg = h_ref[:, gs]
            yd_ref[:, gs] += _mdot(cg, hg) * cs["ea_x"][:, gs]
            st = _mdot(bg, xc_ref[:, gs] * cs["ws_x"][:, gs], 0, 0)
            st_ref[0, :, gs] = hg
            h_ref[:, gs] = cs["eal_x"][:, gs] * hg + st
        y = yd_ref[...] + dsk_ref[...] * xc_ref[:, 0:D]
        y_ref[...] = y
        yg = y * _silu(z_ref[...])
        for g in range(NG):
            gs = slice(GW * g, GW * (g + 1))
            seg = yg[:, gs]
            ya_ref[:, gs] = seg * _rms(seg) * ng_ref[:, gs]

    return pl.pallas_call(
        body, name="ssd_fwd", grid=(nc,),
        in_specs=[_row_spec(CH, XBC), _row_spec(CH, 128), _row_spec(CH, D, 0), _full_spec((1, 128)),
                  _full_spec((1, D)), _full_spec((1, D)), _full_spec((128, D))],
        out_specs=[_row_spec(CH, D), _row_spec(CH, D), pl.BlockSpec((1, NS, D), lambda i: (i, 0, 0))],
        out_shape=[_sds((s, D)), _sds((s, D)), _sds((nc, NS, D))],
        scratch_shapes=[pltpu.VMEM((NS, D), F32), pltpu.VMEM((128, 128), F32), pltpu.VMEM((128, 128), F32),
                        pltpu.VMEM((CH, D), F32)],
        compiler_params=_cp("arbitrary"),
    )(xbc_c, dt, proj5, a_log, dskip_x, ssm_norm, expand)


def _lru_gates(xr, wa_ref, wx_ref, ba_ref, bx_ref, lam_ref):
    gr = _sig(_mdot(xr, wa_ref[...]) + ba_ref[...])
    gi = _sig(_mdot(xr, wx_ref[...]) + bx_ref[...])
    sp = _softplus(-lam_ref[...])
    la = -LRU_C * gr * sp
    a = jnp.exp(la)
    mult = jnp.sqrt(-_expm1(2.0 * la))
    return gr, gi, sp, a, mult


def _lru_fwd(xr, proj5, ya, wa_bd, wx_bd, ba, bx, lam):
    s = xr.shape[0]
    t = min(256, s)

    def body(xr_ref, g_ref, ga_ref, gb_ref, ya_ref, wa_ref, wx_ref, ba_ref, bx_ref, lam_ref, h_ref, mg_ref, hc_ref):
        @pl.when(pl.program_id(0) == 0)
        def _():
            hc_ref[...] = jnp.zeros_like(hc_ref)

        xrv = xr_ref[...]
        _, gi, _, a, mult = _lru_gates(xrv, wa_ref, wx_ref, ba_ref, bx_ref, lam_ref)
        u = mult * gi * xrv
        row = lax.broadcasted_iota(jnp.int32, (t, D), 0)
        sh = 1
        while sh < t:
            m = row >= sh
            u = jnp.where(m, a * pltpu.roll(u, sh, 0) + u, u)
            a = jnp.where(m, a * pltpu.roll(a, sh, 0), a)
            sh *= 2
        h = u + a * hc_ref[0:1, :]
        h_ref[...] = h
        hc_ref[0:1, :] = h_ref[t - 1:t, :]
        yb = h * _gelu(g_ref[...])
        mg_ref[...] = (_sig(ga_ref[...]) * ya_ref[...] + _sig(gb_ref[...]) * yb).astype(mg_ref.dtype)

    return pl.pallas_call(
        body, name="lru_fwd", grid=(s // t,),
        in_specs=[_row_spec(t, D), _row_spec(t, D, 1), _row_spec(t, D, 3), _row_spec(t, D, 4), _row_spec(t, D),
                  _full_spec((D, D)), _full_spec((D, D)), _full_spec((1, D)), _full_spec((1, D)), _full_spec((1, D))],
        out_specs=[_row_spec(t, D), _row_spec(t, D)],
        out_shape=[_sds((s, D)), _sds((s, D), MXU)],
        scratch_shapes=[pltpu.VMEM((8, D), F32)],
        compiler_params=_cp("arbitrary"),
    )(xr, proj5, proj5, proj5, ya, wa_bd, wx_bd, ba, bx, lam)


def _out_proj(merged, w_out, x, g2, g3):
    s = x.shape[0]
    t = min(256, s)

    def body(mg_ref, w_ref, x_ref, g2_ref, g3_ref, mix_ref, h1_ref, v_ref):
        mix = _mdot(mg_ref[...], w_ref[...])
        mix_ref[...] = mix
        h1 = x_ref[...] + mix * _rms(mix) * g2_ref[...]
        h1_ref[...] = h1
        v_ref[...] = (h1 * _rms(h1) * g3_ref[...]).astype(v_ref.dtype)

    return pl.pallas_call(
        body, name="out_proj", grid=(s // t,),
        in_specs=[_row_spec(t, D), _full_spec((D, D)), _row_spec(t, D), _full_spec((1, D)), _full_spec((1, D))],
        out_specs=[_row_spec(t, D), _row_spec(t, D), _row_spec(t, D)],
        out_shape=[_sds((s, D)), _sds((s, D)), _sds((s, D), MXU)],
        compiler_params=_cp("parallel"),
    )(merged, w_out, x, g2, g3)


def _down_loss(pre, w_down, h1, target, g4):
    s = pre.shape[0]
    t = min(256, s)

    def body(pre_ref, w_ref, h1_ref, tg_ref, g4_ref, dout_ref, dff_ref, loss_ref, dg4_ref):
        @pl.when(pl.program_id(0) == 0)
        def _():
            loss_ref[...] = jnp.zeros_like(loss_ref)
            dg4_ref[...] = jnp.zeros_like(dg4_ref)

        ff = _mdot(_relu2(pre_ref[...]), w_ref[...])
        r4 = _rms(ff)
        g4v = g4_ref[...]
        diff = h1_ref[...] + ff * r4 * g4v - tg_ref[...]
        sq = jnp.sum(jnp.sum(diff * diff, axis=1, keepdims=True), axis=0, keepdims=True)
        loss_ref[...] += (0.5 / D) * sq
        dout = diff * (1.0 / D)
        dout_ref[...] = dout
        dff, dg = _rms_bwd(ff, r4, g4v, dout)
        dff_ref[...] = dff.astype(dff_ref.dtype)
        dg4_ref[...] += dg

    return pl.pallas_call(
        body, name="down_loss", grid=(s // t,),
        in_specs=[_row_spec(t, DFF), _full_spec((DFF, D)), _row_spec(t, D), _row_spec(t, D), _full_spec((1, D))],
        out_specs=[_row_spec(t, D), _row_spec(t, D), _full_spec((1, 128)), _full_spec((1, D))],
        out_shape=[_sds((s, D)), _sds((s, D), MXU), _sds((1, 128)), _sds((1, D))],
        compiler_params=_cp("arbitrary"),
    )(pre, w_down, h1, target, g4)


def _dv_norms(dpre, w_up, h1, mix, dout, g3, g2):
    s = h1.shape[0]
    t = min(256, s)

    def body(dp_ref, w_ref, h1_ref, mix_ref, dout_ref, g3_ref, g2_ref, dh1_ref, dmix_ref, dg3_ref, dg2_ref):
        @pl.when(pl.program_id(0) == 0)
        def _():
            dg3_ref[...] = jnp.zeros_like(dg3_ref)
            dg2_ref[...] = jnp.zeros_like(dg2_ref)

        dv = _mdot(dp_ref[...], w_ref[...], 1, 1)
        h1 = h1_ref[...]
        dh1n, dg3 = _rms_bwd(h1, _rms(h1), g3_ref[...], dv)
        dh1 = dout_ref[...] + dh1n
        dh1_ref[...] = dh1
        mix = mix_ref[...]
        dmix, dg2 = _rms_bwd(mix, _rms(mix), g2_ref[...], dh1)
        dmix_ref[...] = dmix.astype(dmix_ref.dtype)
        dg3_ref[...] += dg3
        dg2_ref[...] += dg2

    return pl.pallas_call(
        body, name="dv_norms", grid=(s // t,),
        in_specs=[_row_spec(t, DFF), _full_spec((D, DFF)), _row_spec(t, D), _row_spec(t, D), _row_spec(t, D),
                  _full_spec((1, D)), _full_spec((1, D))],
        out_specs=[_row_spec(t, D), _row_spec(t, D), _full_spec((1, D)), _full_spec((1, D))],
        out_shape=[_sds((s, D)), _sds((s, D), MXU), _sds((1, D)), _sds((1, D))],
        compiler_params=_cp("arbitrary"),
    )(dpre, w_up, h1, mix, dout, g3, g2)


def _lru_bwd(dmerged, ya, xr, h, proj5, wa_bd, wx_bd, ba, bx, lam):
    s = xr.shape[0]
    t = min(128, s)
    n = s // t
    rs = functools.partial(_rev_spec, t, D, n)

    def body(dm_ref, ya_ref, xr_ref, h_ref, hp_ref, g_ref, ga_ref, gb_ref, wa_ref, wx_ref, ba_ref, bx_ref, lam_ref,
             dya_ref, dga_ref, dgb_ref, dg_ref, dxr_ref, dpr_ref, dpi_ref, dlam_ref, dba_ref, dbx_ref, gc_ref,
             af_ref):
        i = pl.program_id(0)

        @pl.when(i == 0)
        def _():
            gc_ref[...] = jnp.zeros_like(gc_ref)
            af_ref[...] = jnp.zeros_like(af_ref)
            dlam_ref[...] = jnp.zeros_like(dlam_ref)
            dba_ref[...] = jnp.zeros_like(dba_ref)
            dbx_ref[...] = jnp.zeros_like(dbx_ref)

        xrv = xr_ref[...]
        gr, gi, sp, a, mult = _lru_gates(xrv, wa_ref, wx_ref, ba_ref, bx_ref, lam_ref)
        hv = h_ref[...]
        gv = g_ref[...]
        dm = dm_ref[...]
        yav = ya_ref[...]
        sa = _sig(ga_ref[...])
        sb = _sig(gb_ref[...])
        gel = _gelu(gv)
        dya_ref[...] = dm * sa
        dga_ref[...] = (dm * yav * sa * (1.0 - sa)).astype(dga_ref.dtype)
        dyb = dm * sb
        dgb_ref[...] = (dyb * hv * gel * (1.0 - sb)).astype(dgb_ref.dtype)
        dg_ref[...] = (dyb * hv * _dgelu(gv)).astype(dg_ref.dtype)
        dh = dyb * gel
        row = lax.broadcasted_iota(jnp.int32, (t, D), 0)
        an = jnp.where(row == t - 1, af_ref[0:1, :], pltpu.roll(a, t - 1, 0))
        gacc = dh
        sh = 1
        while sh < t:
            m = row < t - sh
            gacc = jnp.where(m, gacc + an * pltpu.roll(gacc, t - sh, 0), gacc)
            an = jnp.where(m, an * pltpu.roll(an, t - sh, 0), an)
            sh *= 2
        gfull = gacc + an * gc_ref[0:1, :]
        gc_ref[0:1, :] = jnp.sum(jnp.where(row == 0, gfull, 0.0), axis=0, keepdims=True)
        af_ref[0:1, :] = jnp.sum(jnp.where(row == 0, a, 0.0), axis=0, keepdims=True)
        hlast = jnp.where(i == n - 1, 0.0, hp_ref[7:8, :])
        hprev = jnp.where(row == 0, hlast, pltpu.roll(hv, 1, 0))
        da = gfull * hprev
        dmult = gfull * gi * xrv
        dgi = gfull * mult * xrv
        dla = da * a - dmult * a * a / mult
        dgr = dla * (-LRU_C * sp)
        dsp = jnp.sum(dla * (-LRU_C * gr), axis=0, keepdims=True)
        dlam_ref[...] += dsp * (-_sig(-lam_ref[...]))
        dpr = dgr * gr * (1.0 - gr)
        dpi = dgi * gi * (1.0 - gi)
        dpr_ref[...] = dpr.astype(dpr_ref.dtype)
        dpi_ref[...] = dpi.astype(dpi_ref.dtype)
        dba_ref[...] += jnp.sum(dpr, axis=0, keepdims=True)
        dbx_ref[...] += jnp.sum(dpi, axis=0, keepdims=True)
        dxr_ref[...] = gfull * mult * gi + _mdot(dpr, wa_ref[...], 1, 1) + _mdot(dpi, wx_ref[...], 1, 1)

    hp_spec = pl.BlockSpec((8, D), lambda i: (jnp.maximum((n - 1 - i) * (t // 8) - 1, 0), 0))
    return pl.pallas_call(
        body, name="lru_bwd", grid=(n,),
        in_specs=[rs(), rs(), rs(), rs(), hp_spec, rs(1), rs(3), rs(4), _full_spec((D, D)), _full_spec((D, D)),
                  _full_spec((1, D)), _full_spec((1, D)), _full_spec((1, D))],
        out_specs=[rs(), rs(), rs(), rs(), rs(), rs(), rs(), _full_spec((1, D)), _full_spec((1, D)),
                   _full_spec((1, D))],
        out_shape=[_sds((s, D)), _sds((s, D), MXU), _sds((s, D), MXU), _sds((s, D), MXU), _sds((s, D)),
                   _sds((s, D), MXU), _sds((s, D), MXU), _sds((1, D)), _sds((1, D)), _sds((1, D))],
        scratch_shapes=[pltpu.VMEM((8, D), F32), pltpu.VMEM((8, D), F32)],
        compiler_params=_cp("arbitrary"),
    )(dmerged, ya, xr, h, h, proj5, proj5, proj5, wa_bd, wx_bd, ba, bx, lam)


def _ssd_bwd(dya, y, proj5, xbc_c, dt, states, a_log, dskip_x, ssm_norm, expand, reduce_):
    s = xbc_c.shape[0]
    nc = s // CH
    rv = functools.partial(_rev_spec, CH)

    def body(dya_ref, y_ref, z_ref, xc_ref, dt_ref, st_ref, alog_ref, dsk_ref, ng_ref, e_ref, et_ref, dz_ref,
             dxc_ref, ddt_ref, dng_ref, ddsk_ref, dalog_ref, dh_ref, at_ref, dtt_ref, dat_ref, ddtt_ref, dy_ref,
             yoffdy_ref, xbds_ref):
        @pl.when(pl.program_id(0) == 0)
        def _():
            dh_ref[...] = jnp.zeros_like(dh_ref)
            dng_ref[...] = jnp.zeros_like(dng_ref)
            ddsk_ref[...] = jnp.zeros_like(ddsk_ref)
            dalog_ref[...] = jnp.zeros_like(dalog_ref)

        cs = _ssd_chunk_setup(dt_ref, alog_ref, e_ref, at_ref, dtt_ref)
        lane, row = cs["lane"], cs["row"]
        et = et_ref[...]
        for g in range(NG):
            gs = slice(GW * g, GW * (g + 1))
            yv = y_ref[:, gs]
            zv = z_ref[:, gs]
            sz = _silu(zv)
            yg = yv * sz
            dyav = dya_ref[:, gs]
            dyg, dng = _rms_bwd(yg, _rms(yg), ng_ref[:, gs], dyav)
            dng_ref[:, gs] += dng
            dy_ref[:, gs] = dyg * sz
            dz_ref[:, gs] = (dyg * yv * _dsilu(zv)).astype(dz_ref.dtype)
        dyv = dy_ref[...]
        xs = xc_ref[:, 0:D]
        ddsk_ref[...] += jnp.sum(dyv * xs, axis=0, keepdims=True)
        dxc_ref[:, 0:D] = dyv * dsk_ref[...]
        dat_ref[...] = jnp.zeros_like(dat_ref)
        ddtt_ref[...] = jnp.zeros_like(ddtt_ref)
        hh = jnp.sum(dh_ref[...] * st_ref[0], axis=0, keepdims=True)
        deal = jnp.max(_hdot(jnp.broadcast_to(hh, (8, D)), et), axis=0, keepdims=True)
        d_acum = jnp.zeros((CH, 128), F32)
        for g in range(NG):
            gs = slice(GW * g, GW * (g + 1))
            bs_ = slice(D + NS * g, D + NS * (g + 1))
            cs_ = slice(D + NG * NS + NS * g, D + NG * NS + NS * (g + 1))
            bg = xc_ref[:, bs_]
            cg = xc_ref[:, cs_]
            cb = _mdot(cg, bg, 1, 1)
            hg = st_ref[0, :, gs]
            dhg = dh_ref[:, gs]
            dyg_ = dy_ref[:, gs]
            xsg = xc_ref[:, gs]
            ea = cs["ea_x"][:, gs]
            wsx = cs["ws_x"][:, gs]
            dp = dyg_ * ea
            yoffdy_ref[:, gs] = dp * _mdot(cg, hg)
            dc = _mdot(dp, hg, 1, 1)
            dhprev = _mdot(cg, dp, 0, 0)
            bds = _mdot(bg, dhg)
            dxc_ref[:, gs] += wsx * bds
            xbds_ref[:, gs] = xsg * bds
            db = _mdot(xsg * wsx, dhg, 1, 1)
            dh_ref[:, gs] = dhprev + cs["eal_x"][:, gs] * dhg
            dcbs = jnp.zeros((CH, CH), F32)
            for j in range(4 * g, 4 * g + 4):
                ps = slice(128 * j, 128 * (j + 1))
                xp = xc_ref[:, ps]
                dyp = dy_ref[:, ps]
                dxacc = jnp.zeros((CH, 128), F32)
                for hf in range(2):
                    hd = 2 * j + hf
                    ld, rowdt = _head_decay(cs, at_ref, dtt_ref, hd)
                    hm = (lane >= HP) if hf else (lane < HP)
                    dym = jnp.where(hm, dyp, 0.0)
                    w = cb * ld * rowdt
                    dw = _mdot(dym, jnp.where(hm, xp, 0.0), 1, 1)
                    dxacc = dxacc + _mdot(w, dym, 0, 0)
                    nm = dw * w
                    ddtt_ref[hd:hd + 1, :] += jnp.sum(dw * cb * ld, axis=0, keepdims=True)
                    d_acum = d_acum + jnp.where(lane == hd, jnp.sum(nm, axis=1, keepdims=True), 0.0)
                    dat_ref[hd:hd + 1, :] -= jnp.sum(nm, axis=0, keepdims=True)
                    dcbs = dcbs + dw * ld * rowdt
                dxc_ref[:, ps] += dxacc
            dxc_ref[:, bs_] = db + _mdot(dcbs, cg, 0, 0)
            dxc_ref[:, cs_] = dc + _mdot(dcbs, bg)
        dws = _hdot(xbds_ref[...], et)
        ws = cs["ws"]
        d_acum = d_acum - dws * ws + _hdot(yoffdy_ref[...], et) + dat_ref[...].T
        d_alast = jnp.sum(dws * ws, axis=0, keepdims=True) + deal * cs["eal"]
        d_acum = d_acum + jnp.where(row == CH - 1, d_alast, 0.0)
        triu = row <= lane
        d_adt = _hdot(triu.astype(F32), d_acum)
        ddt_ref[...] = dws * jnp.exp(cs["alast"] - cs["acum"]) + ddtt_ref[...].T + d_adt * cs["a"]
        dalog_ref[...] += jnp.sum(d_adt * cs["dtv"], axis=0, keepdims=True) * cs["a"]

    return pl.pallas_call(
        body, name="ssd_bwd", grid=(nc,),
        in_specs=[rv(D, nc), rv(D, nc), rv(D, nc, 0), rv(XBC, nc), rv(128, nc),
                  pl.BlockSpec((1, NS, D), lambda i: (nc - 1 - i, 0, 0)), _full_spec((1, 128)), _full_spec((1, D)),
                  _full_spec((1, D)), _full_spec((128, D)), _full_spec((D, 128))],
        out_specs=[rv(D, nc), rv(XBC, nc), rv(128, nc), _full_spec((1, D)), _full_spec((1, D)),
                   _full_spec((1, 128))],
        out_shape=[_sds((s, D), MXU), _sds((s, XBC)), _sds((s, 128)), _sds((1, D)), _sds((1, D)), _sds((1, 128))],
        scratch_shapes=[pltpu.VMEM((NS, D), F32), pltpu.VMEM((128, 128), F32), pltpu.VMEM((128, 128), F32),
                        pltpu.VMEM((128, 128), F32), pltpu.VMEM((128, 128), F32), pltpu.VMEM((CH, D), F32),
                        pltpu.VMEM((CH, D), F32), pltpu.VMEM((CH, D), F32)],
        compiler_params=_cp("arbitrary"),
    )(dya, y, proj5, xbc_c, dt, states, a_log, dskip_x, ssm_norm, expand, reduce_)


def _conv_bwd(dxbc_c, dxr, ddt, xbc_raw, proj5, dt_raw, cw_s, cb_s, cw_l, dt_bias):
    s = xbc_raw.shape[0]
    t = min(256, s)
    n = s // t

    def prev_spec(c, col=0):
        return pl.BlockSpec((8, c), lambda i: (jnp.maximum((n - 1 - i) * (t // 8) - 1, 0), col))

    def body(dxc_ref, dxr_ref, ddt_ref, xs_ref, xsp_ref, xl_ref, xlp_ref, dtr_ref, cws_ref, cbs_ref, cwl_ref,
             dtb_ref, dxs_ref, dxl_ref, ddtr_ref, dcws_ref, dcbs_ref, dcwl_ref, dcbl_ref, ddtb_ref, ds_ref, dl_ref,
             xsb_ref, xlb_ref):
        i = pl.program_id(0)

        @pl.when(i == 0)
        def _():
            ds_ref[t:t + 8, :] = jnp.zeros((8, XBC), F32)
            dl_ref[t:t + 8, :] = jnp.zeros((8, D), F32)
            for r in (dcws_ref, dcbs_ref, dcwl_ref, dcbl_ref, ddtb_ref):
                r[...] = jnp.zeros_like(r)

        first = i == n - 1
        xsb_ref[0:8, :] = jnp.where(first, 0.0, xsp_ref[...])
        xsb_ref[8:t + 8, :] = xs_ref[...]
        xlb_ref[0:8, :] = jnp.where(first, 0.0, xlp_ref[...])
        xlb_ref[8:t + 8, :] = xl_ref[...]
        pre = cbs_ref[...] + cws_ref[3:4, :] * xsb_ref[8:t + 8, :]
        for k in (1, 2, 3):
            pre = pre + cws_ref[3 - k:4 - k, :] * xsb_ref[8 - k:t + 8 - k, :]
        ds_ref[0:t, :] = dxc_ref[...] * _dsilu(pre)
        dl_ref[0:t, :] = dxr_ref[...]

        def back(dbuf, xbuf, w_ref, dx_ref, dw_ref, db_ref):
            dpre = dbuf[0:t, :]
            dx = w_ref[3:4, :] * dpre
            for k in (1, 2, 3):
                dx = dx + w_ref[3 - k:4 - k, :] * dbuf[k:t + k, :]
            dx_ref[...] = dx.astype(dx_ref.dtype)
            for k in range(4):
                dw_ref[3 - k:4 - k, :] += jnp.sum(dpre * xbuf[8 - k:t + 8 - k, :], axis=0, keepdims=True)
            db_ref[...] += jnp.sum(dpre, axis=0, keepdims=True)
            dbuf[t:t + 8, :] = dbuf[0:8, :]

        back(ds_ref, xsb_ref, cws_ref, dxs_ref, dcws_ref, dcbs_ref)
        back(dl_ref, xlb_ref, cwl_ref, dxl_ref, dcwl_ref, dcbl_ref)
        ddtr = ddt_ref[...] * _sig(dtr_ref[...] + dtb_ref[...])
        ddtr_ref[...] = ddtr.astype(ddtr_ref.dtype)
        ddtb_ref[...] += jnp.sum(ddtr, axis=0, keepdims=True)

    rv = functools.partial(_rev_spec, t)
    return pl.pallas_call(
        body, name="conv_bwd", grid=(n,),
        in_specs=[rv(XBC, n), rv(D, n), rv(128, n), rv(XBC, n), prev_spec(XBC), rv(D, n, 2), prev_spec(D, 2),
                  rv(128, n), _full_spec((4, XBC)), _full_spec((1, XBC)), _full_spec((4, D)), _full_spec((1, 128))],
        out_specs=[rv(XBC, n), rv(D, n), rv(128, n), _full_spec((4, XBC)), _full_spec((1, XBC)), _full_spec((4, D)),
                   _full_spec((1, D)), _full_spec((1, 128))],
        out_shape=[_sds((s, XBC), MXU), _sds((s, D), MXU), _sds((s, 128), MXU), _sds((4, XBC)), _sds((1, XBC)),
                   _sds((4, D)), _sds((1, D)), _sds((1, 128))],
        scratch_shapes=[pltpu.VMEM((t + 8, XBC), F32), pltpu.VMEM((t + 8, D), F32), pltpu.VMEM((t + 8, XBC), F32),
                        pltpu.VMEM((t + 8, D), F32)],
        compiler_params=_cp("arbitrary"),
    )(dxbc_c, dxr, ddt, xbc_raw, xbc_raw, proj5, proj5, dt_raw, cw_s, cb_s, cw_l, dt_bias)


def _du_norm(pieces5, dxbc, ddtr, w5, wxbc, wdt, x, dh1, g1):
    s = x.shape[0]
    t = min(256, s)

    def body(p0, p1, p2, p3, p4, dxbc_ref, ddtr_ref, w5_ref, wx_ref, wd_ref, x_ref, dh1_ref, g1_ref, dx_ref, dg1_ref):
        @pl.when(pl.program_id(0) == 0)
        def _():
            dg1_ref[...] = jnp.zeros_like(dg1_ref)

        du = _mdot(dxbc_ref[...], wx_ref[...], 1, 1) + _mdot(ddtr_ref[...], wd_ref[...], 1, 1)
        for b, p in enumerate((p0, p1, p2, p3, p4)):
            du = du + _mdot(p[...], w5_ref[:, D * b:D * (b + 1)], 1, 1)
        xv = x_ref[...]
        dxn, dg1 = _rms_bwd(xv, _rms(xv), g1_ref[...], du)
        dx_ref[...] = dh1_ref[...] + dxn
        dg1_ref[...] += dg1

    return pl.pallas_call(
        body, name="du_norm", grid=(s // t,),
        in_specs=[_row_spec(t, D)] * 5 + [_row_spec(t, XBC), _row_spec(t, 128), _full_spec((D, 5 * D)),
                                          _full_spec((D, XBC)), _full_spec((D, 128)), _row_spec(t, D),
                                          _row_spec(t, D), _full_spec((1, D))],
        out_specs=[_row_spec(t, D), _full_spec((1, D))],
        out_shape=[_sds((s, D)), _sds((1, D))],
        compiler_params=_cp("arbitrary"),
    )(*pieces5, dxbc, ddtr, w5, wxbc, wdt, x, dh1, g1)


def _adamw(w, g, m, v, name):
    r, c = w.shape
    t = r
    if r * c > 256 * 1024:
        t = next(cand for cand in (512, 256, 128, 64, 32, 16, 8) if r % cand == 0 and cand * c <= 512 * 1024)
    bc1 = 1.0 - ADAM_B1 ** ADAM_STEP
    bc2 = 1.0 - ADAM_B2 ** ADAM_STEP

    def body(w_ref, g_ref, m_ref, v_ref, d_ref, nm_ref, nv_ref):
        gv = g_ref[...]
        nm = ADAM_B1 * m_ref[...] + (1.0 - ADAM_B1) * gv
        nv = ADAM_B2 * v_ref[...] + (1.0 - ADAM_B2) * (gv * gv)
        nm_ref[...] = nm
        nv_ref[...] = nv
        d_ref[...] = -ADAM_LR * ((nm / bc1) / (jnp.sqrt(nv / bc2) + ADAM_EPS) + ADAM_WD * w_ref[...])

    spec = pl.BlockSpec((t, c), lambda i: (i, 0))
    return pl.pallas_call(
        body, name=name, grid=(r // t,), in_specs=[spec] * 4, out_specs=[spec] * 3,
        out_shape=[_sds((r, c))] * 3, compiler_params=_cp("parallel"),
    )(w, g, m, v)


def _adamw_halves(w, g_mine, g_other, m, v, cidx, name):
    r, c = w.shape
    hr = r // 2
    t = 256 if hr % 256 == 0 else 128
    nb = hr // t
    bc1 = 1.0 - ADAM_B1 ** ADAM_STEP
    bc2 = 1.0 - ADAM_B2 ** ADAM_STEP

    def body(c_ref, w_ref, gm_ref, go_ref, m_ref, v_ref, g_ref, d_ref, nm_ref, nv_ref):
        mine = (pl.program_id(0) // nb) == c_ref[0]
        gv = jnp.where(mine, gm_ref[...], go_ref[...])
        g_ref[...] = gv
        nm = ADAM_B1 * m_ref[...] + (1.0 - ADAM_B1) * gv
        nv = ADAM_B2 * v_ref[...] + (1.0 - ADAM_B2) * (gv * gv)
        nm_ref[...] = nm
        nv_ref[...] = nv
        d_ref[...] = -ADAM_LR * ((nm / bc1) / (jnp.sqrt(nv / bc2) + ADAM_EPS) + ADAM_WD * w_ref[...])

    spec = pl.BlockSpec((t, c), lambda i, c_ref: (i, 0))
    half = pl.BlockSpec((t, c), lambda i, c_ref: (i % nb, 0))
    return pl.pallas_call(
        body, name=name,
        grid_spec=pltpu.PrefetchScalarGridSpec(num_scalar_prefetch=1, grid=(2 * nb,),
                                               in_specs=[spec, half, half, spec, spec], out_specs=[spec] * 4),
        out_shape=[_sds((r, c))] * 4, compiler_params=_cp("parallel"),
    )(cidx, w, g_mine, g_other, m, v)


def _block_diag(w):
    eye = jnp.eye(NH, dtype=w.dtype)
    return (w[:, :, None, :] * eye[:, None, :, None]).reshape(D, D)


def _diag_blocks(full):
    return jnp.stack([full[HP * h:HP * (h + 1), HP * h:HP * (h + 1)] for h in range(NH)])


def _pad_lanes(v, n=128):
    return jnp.pad(v, ((0, 0), (0, n - v.shape[1])))


def _local_step(x, target, p):
    heads = jnp.arange(D, dtype=jnp.int32) // HP
    expand = (jnp.arange(128, dtype=jnp.int32)[:, None] == heads[None, :]).astype(F32)
    reduce_ = expand.T
    dskip_x = jnp.repeat(p["d_skip"], HP, axis=1)
    a_log = _pad_lanes(p["a_log"])
    dt_bias = _pad_lanes(p["dt_bias"])
    w5, wxbc, wdt = p["w5"], p["wxbc"], p["wdt"]
    wa_bd = _block_diag(p["lru_wa"]).astype(MXU)
    wx_bd = _block_diag(p["lru_wx"]).astype(MXU)
    ba = p["lru_ba"].reshape(1, D)
    bx = p["lru_bx"].reshape(1, D)

    u = _norm_cast(x, p["norm_mix_pre"], "norm_u")
    proj5 = _matmul(u, w5, name="proj5")
    xbc_raw = _matmul(u, wxbc, name="proj_xbc", tn=XBC)
    dt_raw = _matmul(u, wdt, name="proj_dt")
    xbc_c, xr, dt = _conv_fwd(xbc_raw, proj5, dt_raw, p["conv_ssm_w"], p["conv_ssm_b"], p["conv_lru_w"],
                              p["conv_lru_b"], dt_bias)
    y, ya, states = _ssd_fwd(xbc_c, dt, proj5, a_log, dskip_x, p["ssm_norm"], expand)
    h, merged = _lru_fwd(xr, proj5, ya, wa_bd, wx_bd, ba, bx, p["lru_lambda"])
    mix, h1, v = _out_proj(merged, p["w_out"], x, p["norm_mix_post"], p["norm_mlp_pre"])
    pre = _matmul(v, p["w_up"], name="up_proj")
    dout, dff, loss, dg4 = _down_loss(pre, p["w_down"], h1, target, p["norm_mlp_post"])

    dpre = _matmul(dff, p["w_down"], name="d_pre", tb=True, out_dtype=MXU,
                   epi=lambda r, pr: r * (2.0 * jnp.maximum(pr, 0.0)), epi_args=(pre,))
    g_w_down = _matmul(pre, dff, name="dw_down", ta=True, tm=1024, tn=1024, tk=512, a_fn=_relu2)
    dh1, dmix, dg3, dg2 = _dv_norms(dpre, p["w_up"], h1, mix, dout, p["norm_mlp_pre"], p["norm_mix_post"])
    g_w_up = _matmul(v, dpre, name="dw_up", ta=True, tm=1024, tn=1024, tk=512)
    dmerged = _matmul(dmix, p["w_out"], name="d_merged", tb=True)
    g_w_out = _matmul(merged, dmix, name="dw_out", ta=True, tm=1024, tn=1024, tk=512)
    (dya, dga, dgb, dg, dxr, dpr, dpi, dlam, dba, dbx) = _lru_bwd(dmerged, ya, xr, h, proj5, wa_bd, wx_bd, ba, bx,
                                                                  p["lru_lambda"])
    g_wa = _diag_blocks(_matmul(xr, dpr, name="dw_lru_a", ta=True, tm=1024, tn=1024, tk=512))
    g_wx = _diag_blocks(_matmul(xr, dpi, name="dw_lru_x", ta=True, tm=1024, tn=1024, tk=512))
    dz, dxbc_c, ddt, dng, ddsk, dalog = _ssd_bwd(dya, y, proj5, xbc_c, dt, states, a_log, dskip_x, p["ssm_norm"],
                                                 expand, reduce_)
    (dxbc, dxl, ddtr, dcws, dcbs, dcwl, dcbl, ddtb) = _conv_bwd(dxbc_c, dxr, ddt, xbc_raw, proj5, dt_raw,
                                                                p["conv_ssm_w"], p["conv_ssm_b"], p["conv_lru_w"],
                                                                dt_bias)
    pieces5 = (dz, dg, dxl, dga, dgb)
    grad_x, dg1 = _du_norm(pieces5, dxbc, ddtr, w5, wxbc, wdt, x, dh1, p["norm_mix_pre"])
    gw5 = [_matmul(u, pc, name=f"dw_in_{i}", ta=True, tm=1024, tn=1024, tk=512) for i, pc in enumerate(pieces5)]
    gwxbc = _matmul(u, dxbc, name="dw_in_xbc", ta=True, tm=1024, tn=XBC, tk=512)
    gwdt = _matmul(u, ddtr, name="dw_in_dt", ta=True, tm=1024, tn=128, tk=512)
    g_w_in = jnp.concatenate([gw5[0], gwxbc, gwdt[:, :NH], gw5[1], gw5[2], gw5[3], gw5[4]], axis=1)
    grads = {
        "norm_mix_pre": dg1, "w_in": g_w_in, "conv_ssm_w": dcws, "conv_ssm_b": dcbs, "dt_bias": ddtb[:, :NH],
        "a_log": dalog[:, :NH], "d_skip": ddsk.reshape(NH, HP).sum(axis=1)[None, :], "ssm_norm": dng,
        "conv_lru_w": dcwl, "conv_lru_b": dcbl, "lru_wa": g_wa, "lru_ba": dba.reshape(NH, HP), "lru_wx": g_wx,
        "lru_bx": dbx.reshape(NH, HP), "lru_lambda": dlam, "w_out": g_w_out, "norm_mix_post": dg2,
        "norm_mlp_pre": dg3, "w_up": g_w_up, "w_down": g_w_down, "norm_mlp_post": dg4,
    }
    return loss[0, 0], grad_x, grads


def _split_w_in(w_in_full):
    z, xbc, dtc, g, xl, ga, gb = jnp.split(w_in_full, [D, D + XBC, D + XBC + NH, 2 * D + XBC + NH,
                                                        3 * D + XBC + NH, 4 * D + XBC + NH], axis=1)
    return jnp.concatenate([z, g, xl, ga, gb], axis=1), xbc, _pad_lanes(dtc)


ANY = pl.BlockSpec(memory_space=pl.ANY)
COMM = BF16


def _place():
    x, y, c = lax.axis_index("x"), lax.axis_index("y"), lax.axis_index("c")
    chips = [(1 - x, y), (x, 1 - y), (1 - x, 1 - y)]
    return x, y, c, chips


def _remote(src, dst, send_sem, recv_sem, to):
    return pltpu.make_async_remote_copy(src_ref=src, dst_ref=dst, send_sem=send_sem, recv_sem=recv_sem, device_id=to,
                                        device_id_type=MESH)


def _gather_weights(big, small):
    nb, ns = len(big), len(small)
    na = nb + ns

    def body(*refs):
        ins, outs = refs[:na], refs[na:2 * na]
        send_sems, recv_sems = refs[2 * na:]
        x, y, c, chips = _place()
        k = 2 * x + y
        sends = []
        for a in range(na):
            if a < nb:
                hr = ins[a].shape[0] // 2
                src = ins[a].at[pl.ds(c * hr, hr)]
                dst = outs[a].at[k, pl.ds(c * hr, hr)]
            else:
                src, dst = ins[a], outs[a].at[k]
            for j, (cx, cy) in enumerate(chips):
                sends.append(_remote(src, dst, send_sems.at[a, j], recv_sems.at[a, j], (cx, cy, c)))
        for cp in sends:
            cp.start()
        for j, (cx, cy) in enumerate(chips):
            kj = 2 * cx + cy
            for a in range(na):
                if a < nb:
                    hr = ins[a].shape[0] // 2
                    got = outs[a].at[kj, pl.ds(c * hr, hr)]
                    _remote(got, got, send_sems.at[a, j], recv_sems.at[a, j], (cx, cy, c)).wait_recv()
                    fwd = _remote(got, got, send_sems.at[a, 3 + j], recv_sems.at[a, 3 + j], (x, y, 1 - c))
                    fwd.start()
                    sends.append(fwd)
                else:
                    got = outs[a].at[kj]
                    _remote(got, got, send_sems.at[a, j], recv_sems.at[a, j], (cx, cy, c)).wait_recv()
        for j, (cx, cy) in enumerate(chips):
            kj = 2 * cx + cy
            for a in range(nb):
                hr = ins[a].shape[0] // 2
                got = outs[a].at[kj, pl.ds((1 - c) * hr, hr)]
                _remote(got, got, send_sems.at[a, 3 + j], recv_sems.at[a, 3 + j], (x, y, 1 - c)).wait_recv()
        for cp in sends:
            cp.wait_send()

    arrs = list(big) + list(small)
    outs = pl.pallas_call(
        body, name="gather_weights", in_specs=[ANY] * na, out_specs=[ANY] * na,
        out_shape=[_sds((NCHIP,) + a.shape, a.dtype) for a in arrs],
        scratch_shapes=[pltpu.SemaphoreType.DMA((na, 6)), pltpu.SemaphoreType.DMA((na, 6))],
    )(*arrs)
    kchip = 2 * lax.axis_index("x") + lax.axis_index("y")
    return [lax.dynamic_update_index_in_dim(o, a, kchip, 0) for o, a in zip(outs, arrs)]


def _pair_exchange(gs):
    na = len(gs)

    def body(*refs):
        ins, outs = refs[:na], refs[na:2 * na]
        send_sems, recv_sems = refs[2 * na:]
        x, y, c, _ = _place()
        cps = []
        for a in range(na):
            hr = ins[a].shape[1] // 2
            cps.append(_remote(ins[a].at[:, pl.ds((1 - c) * hr, hr)], outs[a], send_sems.at[a], recv_sems.at[a],
                               (x, y, 1 - c)))
        for cp in cps:
            cp.start()
        for cp in cps:
            cp.wait()

    return pl.pallas_call(
        body, name="grad_pair_exchange", in_specs=[ANY] * na, out_specs=[ANY] * na,
        out_shape=[_sds((NCHIP, g.shape[1] // 2, g.shape[2]), g.dtype) for g in gs],
        scratch_shapes=[pltpu.SemaphoreType.DMA((na,)), pltpu.SemaphoreType.DMA((na,))],
    )(*gs)


def _pair_add(g, got, cidx, name):
    _, r, cdim = g.shape
    hr = r // 2
    t = 256 if hr % 256 == 0 else 128
    nt = hr // t

    def body(c_ref, g_ref, o_ref, p_ref, pc_ref):
        sm = g_ref[...] + o_ref[...]
        p_ref[...] = sm
        pc_ref[...] = sm.astype(pc_ref.dtype)

    spec = pl.BlockSpec((1, t, cdim), lambda k, i, c_ref: (k, i, 0))
    return pl.pallas_call(
        body, name=name,
        grid_spec=pltpu.PrefetchScalarGridSpec(
            num_scalar_prefetch=1, grid=(NCHIP, nt),
            in_specs=[pl.BlockSpec((1, t, cdim), lambda k, i, c_ref: (k, c_ref[0] * nt + i, 0)), spec],
            out_specs=[spec, spec]),
        out_shape=[_sds((NCHIP, hr, cdim)), _sds((NCHIP, hr, cdim), COMM)],
        compiler_params=_cp("parallel", "parallel"),
    )(cidx, g, got)


def _chip_exchange(ps):
    na = len(ps)

    def body(*refs):
        ins, outs = refs[:na], refs[na:2 * na]
        send_sems, recv_sems = refs[2 * na:]
        _, _, c, chips = _place()
        cps = []
        for a in range(na):
            for j, (cx, cy) in enumerate(chips):
                cps.append(_remote(ins[a].at[2 * cx + cy], outs[a].at[j], send_sems.at[a, j], recv_sems.at[a, j],
                                   (cx, cy, c)))
        for cp in cps:
            cp.start()
        for cp in cps:
            cp.wait()

    return pl.pallas_call(
        body, name="grad_chip_exchange", in_specs=[ANY] * na, out_specs=[ANY] * na,
        out_shape=[_sds((NCHIP - 1,) + p.shape[1:], p.dtype) for p in ps],
        scratch_shapes=[pltpu.SemaphoreType.DMA((na, 3)), pltpu.SemaphoreType.DMA((na, 3))],
    )(*ps)


def _shard_sum(p, got, kidx, name):
    _, hr, cdim = p.shape
    t = 256 if hr % 256 == 0 else 128

    def body(k_ref, p_ref, g_ref, o_ref):
        sm = p_ref[0]
        for j in range(NCHIP - 1):
            sm = sm + g_ref[j].astype(F32)
        o_ref[...] = sm

    return pl.pallas_call(
        body, name=name,
        grid_spec=pltpu.PrefetchScalarGridSpec(
            num_scalar_prefetch=1, grid=(hr // t,),
            in_specs=[pl.BlockSpec((1, t, cdim), lambda i, k_ref: (k_ref[0], i, 0)),
                      pl.BlockSpec((NCHIP - 1, t, cdim), lambda i, k_ref: (0, i, 0))],
            out_specs=pl.BlockSpec((t, cdim), lambda i, k_ref: (i, 0))),
        out_shape=_sds((hr, cdim)),
        compiler_params=_cp("parallel"),
    )(kidx, p, got)


def _pair_swap(rs):
    na = len(rs)

    def body(*refs):
        ins, outs = refs[:na], refs[na:2 * na]
        send_sems, recv_sems = refs[2 * na:]
        x, y, c, _ = _place()
        cps = [_remote(ins[a], outs[a], send_sems.at[a], recv_sems.at[a], (x, y, 1 - c)) for a in range(na)]
        for cp in cps:
            cp.start()
        for cp in cps:
            cp.wait()

    return pl.pallas_call(
        body, name="grad_pair_swap", in_specs=[ANY] * na, out_specs=[ANY] * na,
        out_shape=[_sds(r.shape, r.dtype) for r in rs],
        scratch_shapes=[pltpu.SemaphoreType.DMA((na,)), pltpu.SemaphoreType.DMA((na,))],
    )(*rs)


def _small_allgather(v):
    r = v.shape[0]

    def body(v_ref, out_ref, send_sems, recv_sems, local_sem):
        x, y, c, chips = _place()
        me, sibling = (x, y, c), (x, y, 1 - c)

        def slot(px, py, pc):
            return out_ref.at[4 * px + 2 * py + pc]

        def copy(k, block, to, src=None):
            return _remote(slot(*block) if src is None else src, slot(*block), send_sems.at[k], recv_sems.at[k], to)

        mine = pltpu.make_async_copy(v_ref, slot(*me), local_sem)
        mine.start()
        first = [copy(0, me, sibling, src=v_ref)]
        first += [copy(1 + j, me, (*chip, c), src=v_ref) for j, chip in enumerate(chips)]
        for cp in first:
            cp.start()
        passed = [copy(4 + j, (*chip, c), sibling) for j, chip in enumerate(chips)]
        for j, chip in enumerate(chips):
            copy(1 + j, (*chip, c), me).wait_recv()
            passed[j].start()
        copy(0, sibling, me).wait_recv()
        for j, chip in enumerate(chips):
            copy(4 + j, (*chip, 1 - c), me).wait_recv()
        for cp in first + passed:
            cp.wait_send()
        mine.wait()

    vm = pl.BlockSpec(memory_space=pltpu.VMEM)
    return pl.pallas_call(
        body, name="small_allgather", in_specs=[vm], out_specs=vm, out_shape=_sds((8, r, 128)),
        scratch_shapes=[pltpu.SemaphoreType.DMA((7,)), pltpu.SemaphoreType.DMA((7,)), pltpu.SemaphoreType.DMA],
        compiler_params=pltpu.CompilerParams(vmem_limit_bytes=VMEM_LIMIT),
    )(v)


def _sum_devices(allv):
    _, r, _ = allv.shape

    def body(a_ref, o_ref):
        sm = a_ref[0]
        for d in range(1, 8):
            sm = sm + a_ref[d]
        o_ref[...] = sm

    return pl.pallas_call(
        body, name="small_sum", grid=(1,), in_specs=[_full_spec((8, r, 128))], out_specs=_full_spec((r, 128)),
        out_shape=_sds((r, 128)), compiler_params=_cp("arbitrary"),
    )(allv)


def _pack(arrs):
    rows = []
    for a in arrs:
        f = a.reshape(-1)
        f = jnp.pad(f, (0, (-f.shape[0]) % 128))
        rows.append(f.reshape(-1, 128))
    out = jnp.concatenate(rows, axis=0)
    return jnp.pad(out, ((0, (-out.shape[0]) % 8), (0, 0)))


def _unpack(packed, shapes):
    outs, r0 = [], 0
    for shp in shapes:
        n = math.prod(shp)
        nr = -(-n // 128)
        outs.append(packed[r0:r0 + nr].reshape(-1)[:n].reshape(shp))
        r0 += nr
    return outs


BIG = ("w_in", "w_out", "w_up", "w_down")
CONV = ("conv_ssm_w", "conv_lru_w")
WEIGHTS = ("norm_mix_pre", "w_in", "conv_ssm_w", "conv_ssm_b", "dt_bias", "a_log", "d_skip", "ssm_norm", "conv_lru_w",
           "conv_lru_b", "lru_wa", "lru_ba", "lru_wx", "lru_bx", "lru_lambda", "w_out", "norm_mix_post",
           "norm_mlp_pre", "w_up", "w_down", "norm_mlp_post")
SMALL = tuple(n for n in WEIGHTS if n not in BIG and n not in CONV)


def kernel(x, norm_mix_pre, w_in, conv_ssm_w, conv_ssm_b, dt_bias, a_log, d_skip, ssm_norm, conv_lru_w, conv_lru_b, lru_wa, lru_ba, lru_wx, lru_bx, lru_lambda, w_out, norm_mix_post, norm_mlp_pre, w_up, w_down, norm_mlp_post, loss_target, m_norm_mix_pre, m_w_in, m_conv_ssm_w, m_conv_ssm_b, m_dt_bias, m_a_log, m_d_skip, m_ssm_norm, m_conv_lru_w, m_conv_lru_b, m_lru_wa, m_lru_ba, m_lru_wx, m_lru_bx, m_lru_lambda, m_w_out, m_norm_mix_post, m_norm_mlp_pre, m_w_up, m_w_down, m_norm_mlp_post, v_norm_mix_pre, v_w_in, v_conv_ssm_w, v_conv_ssm_b, v_dt_bias, v_a_log, v_d_skip, v_ssm_norm, v_conv_lru_w, v_conv_lru_b, v_lru_wa, v_lru_ba, v_lru_wx, v_lru_bx, v_lru_lambda, v_w_out, v_norm_mix_post, v_norm_mlp_pre, v_w_up, v_w_down, v_norm_mlp_post):
    args = locals()
    w = {n: args[n][0] for n in WEIGHTS}
    m = {n: args["m_" + n][0] for n in WEIGHTS}
    v = {n: args["v_" + n][0] for n in WEIGHTS}
    cidx = lax.axis_index("c").astype(jnp.int32).reshape(1)
    kchip = 2 * lax.axis_index("x") + lax.axis_index("y")
    kidx = kchip.astype(jnp.int32).reshape(1)

    gathered = _gather_weights([w[n].astype(MXU) for n in BIG], [w[n] for n in CONV])
    cat = lambda g: jnp.concatenate([g[k] for k in range(NCHIP)], axis=1)
    w5, wxbc, wdt = _split_w_in(cat(gathered[0]))
    p = {n: (w[n].reshape(1, -1) if w[n].ndim == 1 else w[n]) for n in SMALL}
    p.update(w5=w5, wxbc=wxbc, wdt=wdt, w_out=gathered[1].reshape(D, D), w_up=cat(gathered[2]),
             w_down=gathered[3].reshape(DFF, D), conv_ssm_w=cat(gathered[4]), conv_lru_w=cat(gathered[5]))

    loss, grad_x, g = _local_step(x[0], loss_target[0], p)
    loss = lax.psum(loss, ("x", "y", "c"))

    shard_major = [
        jnp.stack([g["w_in"][:, W_IN_SHARD * k:W_IN_SHARD * (k + 1)] for k in range(NCHIP)]),
        g["w_out"].reshape(NCHIP, D // NCHIP, D),
        jnp.stack([g["w_up"][:, D * k:D * (k + 1)] for k in range(NCHIP)]),
        g["w_down"].reshape(NCHIP, D, D),
    ]
    from_sibling = _pair_exchange(shard_major)
    pair = [_pair_add(gs, got, cidx, f"grad_pair_add_{n}") for gs, got, n in zip(shard_major, from_sibling, BIG)]
    from_chips = _chip_exchange([pc for _, pc in pair])
    halves = [_shard_sum(pf, got, kidx, f"grad_shard_sum_{n}") for (pf, _), got, n in zip(pair, from_chips, BIG)]
    others = _pair_swap(halves)

    reduced = {}
    small_shapes = [g[n].shape for n in SMALL + CONV]
    summed = _unpack(_sum_devices(_small_allgather(_pack([g[n] for n in SMALL + CONV]))), small_shapes)
    for n, s in zip(SMALL + CONV, summed):
        if n in CONV:
            width = w[n].shape[1]
            reduced[n] = lax.dynamic_slice_in_dim(s, kchip * width, width, axis=1)
        else:
            reduced[n] = s.reshape(w[n].shape)

    delta, new_m, new_v = {}, {}, {}
    for n, mine, other in zip(BIG, halves, others):
        reduced[n], delta[n], new_m[n], new_v[n] = _adamw_halves(w[n], mine, other, m[n], v[n], cidx, f"adamw_{n}")
    for n in CONV:
        delta[n], new_m[n], new_v[n] = _adamw(w[n], reduced[n], m[n], v[n], f"adamw_{n}")
    shapes = [w[n].shape for n in SMALL]
    packed = [_pack([d[n] for n in SMALL]) for d in (w, reduced, m, v)]
    for d, out in zip((delta, new_m, new_v), _adamw(*packed, "adamw_small")):
        d.update(zip(SMALL, _unpack(out, shapes)))

    lead = lambda d: [d[n][None] for n in WEIGHTS]
    return (loss, grad_x[None], *lead(reduced), *lead(delta), *lead(new_m), *lead(new_v))
```

```python
import functools
import math

import jax
import jax.numpy as jnp
from jax import lax
from jax.experimental import pallas as pl
from jax.experimental.pallas import tpu as pltpu

F32 = jnp.float32
BF16 = jnp.bfloat16
MXU = BF16

D = 1024
DFF = 4096
NH = 16
HP = 64
NG = 2
NS = 128
CH = 128
XBC = D + 2 * NG * NS
GW = D // NG
LRU_C = 8.0
EPS = 1e-6
NCHIP = 4
W_IN_COLS = 6672
W_IN_SHARD = W_IN_COLS // NCHIP

ADAM_LR = 0.001
ADAM_B1 = 0.9
ADAM_B2 = 0.999
ADAM_EPS = 1e-08
ADAM_WD = 0.01
ADAM_STEP = 10

VMEM_LIMIT = 56 * 1024 * 1024
MESH = pl.DeviceIdType.MESH


def _cp(*sem):
    return pltpu.CompilerParams(dimension_semantics=sem, vmem_limit_bytes=VMEM_LIMIT)


def _dot(a, b, ca=1, cb=0, prec=None):
    return lax.dot_general(a, b, (((ca,), (cb,)), ((), ())), precision=prec, preferred_element_type=F32)


def _mdot(a, b, ca=1, cb=0):
    return _dot(a.astype(MXU), b.astype(MXU), ca, cb)


def _bf16_parts(v, n):
    parts = []
    for i in range(n):
        p = v.astype(BF16)
        parts.append(p)
        if i < n - 1:
            v = v - p.astype(F32)
    return parts


def _xdot(a, b, passes, split_b=False):
    if split_b:
        a16 = a.astype(BF16)
        terms = [_dot(a16, p) for p in _bf16_parts(b, passes)]
    else:
        b16 = b.astype(BF16)
        terms = [_dot(p, b16) for p in _bf16_parts(a, passes)]
    return functools.reduce(lambda u, v: u + v, terms)


def _sig(x):
    return 0.5 * jnp.tanh(0.5 * x) + 0.5


def _silu(x):
    return x * _sig(x)


def _dsilu(x):
    s = _sig(x)
    return s * (1.0 + x * (1.0 - s))


def _softplus(x):
    e = jnp.exp(-jnp.abs(x))
    return jnp.maximum(x, 0.0) + jnp.where(e < 1e-4, e * (1.0 - 0.5 * e), jnp.log(1.0 + e))


_GELU_C = math.sqrt(2.0 / math.pi)


def _gelu(x):
    t = jnp.tanh(_GELU_C * (x + 0.044715 * x * x * x))
    return 0.5 * x * (1.0 + t)


def _dgelu(x):
    x2 = x * x
    t = jnp.tanh(_GELU_C * (x + 0.044715 * x * x2))
    return 0.5 * (1.0 + t) + 0.5 * x * (1.0 - t * t) * _GELU_C * (1.0 + 3.0 * 0.044715 * x2)


def _expm1(x):
    small = x * (1.0 + x * (0.5 + x * (1.0 / 6.0 + x * (1.0 / 24.0 + x * (1.0 / 120.0)))))
    return jnp.where(jnp.abs(x) < 0.03, small, jnp.exp(x) - 1.0)


def _rms(x):
    return lax.rsqrt(jnp.mean(x * x, axis=-1, keepdims=True) + EPS)


def _rms_bwd(x, r, g, dy):
    xn = x * r
    dxh = dy * g
    m = jnp.mean(dxh * xn, axis=-1, keepdims=True)
    return r * (dxh - xn * m), jnp.sum(dy * xn, axis=0, keepdims=True)


def _row_spec(t, c, col=0):
    return pl.BlockSpec((t, c), lambda i: (i, col))


def _rev_spec(t, c, n, col=0):
    return pl.BlockSpec((t, c), lambda i: (n - 1 - i, col))


def _full_spec(shape):
    nd = len(shape)
    return pl.BlockSpec(shape, lambda *_: (0,) * nd)


def _sds(shape, dtype=F32):
    return jax.ShapeDtypeStruct(shape, dtype)


def _matmul(a, b, *, name, ta=False, tb=False, tm=512, tn=1024, tk=1024, out_dtype=F32, a_fn=None, epi=None,
            epi_args=()):
    m, k = (a.shape[1], a.shape[0]) if ta else a.shape
    n = b.shape[0] if tb else b.shape[1]
    tm, tn, tk = min(tm, m), min(tn, n), min(tk, k)
    nk = k // tk
    a_spec = pl.BlockSpec((tk, tm), lambda i, j, kk: (kk, i)) if ta else pl.BlockSpec((tm, tk), lambda i, j, kk: (i, kk))
    b_spec = pl.BlockSpec((tn, tk), lambda i, j, kk: (j, kk)) if tb else pl.BlockSpec((tk, tn), lambda i, j, kk: (kk, j))
    e_specs = [pl.BlockSpec((tm, tn), lambda i, j, kk: (i, j)) for _ in epi_args]
    ne = len(epi_args)

    def body(a_ref, b_ref, *rest):
        e_refs, o_ref, acc_ref = rest[:ne], rest[ne], rest[ne + 1]
        kk = pl.program_id(2)

        @pl.when(kk == 0)
        def _():
            acc_ref[...] = jnp.zeros_like(acc_ref)

        av = a_ref[...]
        if a_fn is not None:
            av = a_fn(av)
        acc_ref[...] += _mdot(av, b_ref[...], 0 if ta else 1, 1 if tb else 0)

        @pl.when(kk == nk - 1)
        def _():
            r = acc_ref[...]
            if epi is not None:
                r = epi(r, *[e[...] for e in e_refs])
            o_ref[...] = r.astype(o_ref.dtype)

    return pl.pallas_call(
        body, name=name, grid=(m // tm, n // tn, nk),
        in_specs=[a_spec, b_spec] + e_specs,
        out_specs=pl.BlockSpec((tm, tn), lambda i, j, kk: (i, j)),
        out_shape=_sds((m, n), out_dtype),
        scratch_shapes=[pltpu.VMEM((tm, tn), F32)],
        compiler_params=_cp("parallel", "parallel", "arbitrary"),
    )(a, b, *epi_args)


def _relu2(p):
    p = jnp.maximum(p, 0.0)
    return p * p


def _norm_cast(x, g, name):
    s = x.shape[0]
    t = min(512, s)

    def body(x_ref, g_ref, o_ref):
        xv = x_ref[...]
        o_ref[...] = (xv * _rms(xv) * g_ref[...]).astype(o_ref.dtype)

    return pl.pallas_call(
        body, name=name, grid=(s // t,), in_specs=[_row_spec(t, D), _full_spec((1, D))],
        out_specs=_row_spec(t, D), out_shape=_sds((s, D), MXU), compiler_params=_cp("parallel"),
    )(x, g)


def _conv_fwd(xbc_raw, proj5, dt_raw, cw_s, cb_s, cw_l, cb_l, dt_bias):
    s = xbc_raw.shape[0]
    t = min(256, s)

    def body(xs_ref, xl_ref, dtr_ref, cws_ref, cbs_ref, cwl_ref, cbl_ref, dtb_ref, xc_ref, dsl_ref, xr_ref, dt_ref,
             bs_ref, bl_ref):
        @pl.when(pl.program_id(0) == 0)
        def _():
            bs_ref[0:8, :] = jnp.zeros((8, XBC), F32)
            bl_ref[0:8, :] = jnp.zeros((8, D), F32)

        bs_ref[8:t + 8, :] = xs_ref[...]
        bl_ref[8:t + 8, :] = xl_ref[...]

        def conv(buf, w_ref, b_ref):
            acc = b_ref[...] + w_ref[3:4, :] * buf[8:t + 8, :]
            for k in (1, 2, 3):
                acc = acc + w_ref[3 - k:4 - k, :] * buf[8 - k:t + 8 - k, :]
            return acc

        pre = conv(bs_ref, cws_ref, cbs_ref)
        sg = _sig(pre)
        xc_ref[...] = pre * sg
        dsl_ref[...] = (sg * (1.0 + pre * (1.0 - sg))).astype(dsl_ref.dtype)
        xr_ref[...] = conv(bl_ref, cwl_ref, cbl_ref)
        dt_ref[...] = _softplus(dtr_ref[...] + dtb_ref[...])
        bs_ref[0:8, :] = bs_ref[t:t + 8, :]
        bl_ref[0:8, :] = bl_ref[t:t + 8, :]

    return pl.pallas_call(
        body, name="conv_fwd", grid=(s // t,),
        in_specs=[_row_spec(t, XBC), _row_spec(t, D, 2), _row_spec(t, 128), _full_spec((4, XBC)),
                  _full_spec((1, XBC)), _full_spec((4, D)), _full_spec((1, D)), _full_spec((1, 128))],
        out_specs=[_row_spec(t, XBC), _row_spec(t, XBC), _row_spec(t, D), _row_spec(t, 128)],
        out_shape=[_sds((s, XBC)), _sds((s, XBC), BF16), _sds((s, D)), _sds((s, 128))],
        scratch_shapes=[pltpu.VMEM((t + 8, XBC), F32), pltpu.VMEM((t + 8, D), F32)],
        compiler_params=_cp("arbitrary"),
    )(xbc_raw, proj5, dt_raw, cw_s, cb_s, cw_l, cb_l, dt_bias)


def _ssd_chunk_setup(dt_ref, alog_ref, e_ref, at_ref, dtt_ref):
    lane = lax.broadcasted_iota(jnp.int32, (CH, 128), 1)
    row = lax.broadcasted_iota(jnp.int32, (CH, 128), 0)
    lane1 = lax.broadcasted_iota(jnp.int32, (1, 128), 1)
    a = jnp.where(lane1 < NH, -jnp.exp(alog_ref[...]), 0.0)
    dtv = dt_ref[...]
    adt = dtv * a
    tril = row >= lane
    acum = _xdot(tril.astype(F32), adt, 3, split_b=True)
    alast = jnp.sum(adt, axis=0, keepdims=True)
    at_ref[...] = acum.T
    dtt_ref[...] = dtv.T
    e = e_ref[...]
    ea_x = _xdot(jnp.exp(acum), e, 2)
    ws = jnp.exp(alast - acum) * dtv
    ws_x = _xdot(ws, e, 2)
    eal = jnp.exp(alast)
    eal_x = jnp.max(_xdot(jnp.broadcast_to(eal, (8, 128)), e, 3), axis=0, keepdims=True)
    return dict(lane=lane, row=row, tril=tril, a=a, dtv=dtv, acum=acum, alast=alast, ea_x=ea_x, ws=ws, ws_x=ws_x,
                eal=eal, eal_x=eal_x)


def _head_decay(cs, at_ref, dtt_ref, h):
    col = jnp.sum(jnp.where(cs["lane"] == h, cs["acum"], 0.0), axis=1, keepdims=True)
    ld = jnp.where(cs["tril"], jnp.exp(jnp.minimum(col - at_ref[h:h + 1, :], 0.0)), 0.0)
    return ld, dtt_ref[h:h + 1, :]


def _ssd_fwd(xbc_c, dt, proj5, a_log, dskip_x, ssm_norm, expand):
    s = xbc_c.shape[0]
    nc = s // CH

    def body(xc_ref, dt_ref, z_ref, alog_ref, dsk_ref, ng_ref, e_ref, y_ref, ya_ref, st_ref, h_ref, at_ref, dtt_ref,
             yd_ref):
        @pl.when(pl.program_id(0) == 0)
        def _():
            h_ref[...] = jnp.zeros_like(h_ref)

        cs = _ssd_chunk_setup(dt_ref, alog_ref, e_ref, at_ref, dtt_ref)
        lane = cs["lane"]
        for g in range(NG):
            gs = slice(GW * g, GW * (g + 1))
            bg = xc_ref[:, D + NS * g:D + NS * (g + 1)]
            cg = xc_ref[:, D + NG * NS + NS * g:D + NG * NS + NS * (g + 1)]
            cb = _mdot(cg, bg, 1, 1)
            for j in range(4 * g, 4 * g + 4):
                ps = slice(128 * j, 128 * (j + 1))
                xp = xc_ref[:, ps]
                acc = jnp.zeros((CH, 128), F32)
                for hf in range(2):
                    ld, rowdt = _head_decay(cs, at_ref, dtt_ref, 2 * j + hf)
                    hm = (lane >= HP) if hf else (lane < HP)
                    acc = acc + _mdot(cb * ld * rowdt, jnp.where(hm, xp, 0.0))
                yd_ref[:, ps] = acc
            hg = h_ref[:, gs]
            yd_ref[:, gs] += _mdot(cg, hg) * cs["ea_x"][:, gs]
            st = _mdot(bg, xc_ref[:, gs] * cs["ws_x"][:, gs], 0, 0)
            st_ref[0, :, gs] = hg
            h_ref[:, gs] = cs["eal_x"][:, gs] * hg + st
        y = yd_ref[...] + dsk_ref[...] * xc_ref[:, 0:D]
        y_ref[...] = y
        yg = y * _silu(z_ref[...])
        for g in range(NG):
            gs = slice(GW * g, GW * (g + 1))
            seg = yg[:, gs]
            ya_ref[:, gs] = seg * _rms(seg) * ng_ref[:, gs]

    return pl.pallas_call(
        body, name="ssd_fwd", grid=(nc,),
        in_specs=[_row_spec(CH, XBC), _row_spec(CH, 128), _row_spec(CH, D, 0), _full_spec((1, 128)),
                  _full_spec((1, D)), _full_spec((1, D)), _full_spec((128, D))],
        out_specs=[_row_spec(CH, D), _row_spec(CH, D), pl.BlockSpec((1, NS, D), lambda i: (i, 0, 0))],
        out_shape=[_sds((s, D)), _sds((s, D)), _sds((nc, NS, D))],
        scratch_shapes=[pltpu.VMEM((NS, D), F32), pltpu.VMEM((128, 128), F32), pltpu.VMEM((128, 128), F32),
                        pltpu.VMEM((CH, D), F32)],
        compiler_params=_cp("arbitrary"),
    )(xbc_c, dt, proj5, a_log, dskip_x, ssm_norm, expand)


def _lru_gates(xr, wa_ref, wx_ref, ba_ref, bx_ref, lam_ref):
    gr = _sig(_mdot(xr, wa_ref[...]) + ba_ref[...])
    gi = _sig(_mdot(xr, wx_ref[...]) + bx_ref[...])
    sp = _softplus(-lam_ref[...])
    la = -LRU_C * gr * sp
    a = jnp.exp(la)
    mult = jnp.sqrt(-_expm1(2.0 * la))
    return gr, gi, sp, a, mult


def _lru_fwd(xr, proj5, ya, wa_bd, wx_bd, ba, bx, lam):
    s = xr.shape[0]
    t = min(256, s)

    def body(xr_ref, g_ref, ga_ref, gb_ref, ya_ref, wa_ref, wx_ref, ba_ref, bx_ref, lam_ref, h_ref, mg_ref, hc_ref,
             a_ref, u_ref):
        @pl.when(pl.program_id(0) == 0)
        def _():
            hc_ref[...] = jnp.zeros_like(hc_ref)

        xrv = xr_ref[...]
        _, gi, _, a, mult = _lru_gates(xrv, wa_ref, wx_ref, ba_ref, bx_ref, lam_ref)
        a_ref[...] = a
        u_ref[...] = mult * gi * xrv
        row = lax.broadcasted_iota(jnp.int32, (8, D), 0)

        def slab(i, carry):
            r0 = pl.multiple_of(i * 8, 8)
            a8 = a_ref[pl.ds(r0, 8), :]
            u8 = u_ref[pl.ds(r0, 8), :]
            for sh in (1, 2, 4):
                m = row >= sh
                u8 = jnp.where(m, a8 * pltpu.roll(u8, sh, 0) + u8, u8)
                a8 = jnp.where(m, a8 * pltpu.roll(a8, sh, 0), a8)
            h8 = u8 + a8 * hc_ref[7:8, :]
            h_ref[pl.ds(r0, 8), :] = h8
            hc_ref[...] = h8
            return carry

        lax.fori_loop(0, t // 8, slab, 0, unroll=4)
        yb = h_ref[...] * _gelu(g_ref[...])
        mg_ref[...] = (_sig(ga_ref[...]) * ya_ref[...] + _sig(gb_ref[...]) * yb).astype(mg_ref.dtype)

    return pl.pallas_call(
        body, name="lru_fwd", grid=(s // t,),
        in_specs=[_row_spec(t, D), _row_spec(t, D, 1), _row_spec(t, D, 3), _row_spec(t, D, 4), _row_spec(t, D),
                  _full_spec((D, D)), _full_spec((D, D)), _full_spec((1, D)), _full_spec((1, D)), _full_spec((1, D))],
        out_specs=[_row_spec(t, D), _row_spec(t, D)],
        out_shape=[_sds((s, D)), _sds((s, D), MXU)],
        scratch_shapes=[pltpu.VMEM((8, D), F32), pltpu.VMEM((t, D), F32), pltpu.VMEM((t, D), F32)],
        compiler_params=_cp("arbitrary"),
    )(xr, proj5, proj5, proj5, ya, wa_bd, wx_bd, ba, bx, lam)


def _out_proj(merged, w_out, x, g2, g3):
    s = x.shape[0]
    t = min(256, s)

    def body(mg_ref, w_ref, x_ref, g2_ref, g3_ref, mix_ref, h1_ref, v_ref):
        mix = _mdot(mg_ref[...], w_ref[...])
        mix_ref[...] = mix
        h1 = x_ref[...] + mix * _rms(mix) * g2_ref[...]
        h1_ref[...] = h1
        v_ref[...] = (h1 * _rms(h1) * g3_ref[...]).astype(v_ref.dtype)

    return pl.pallas_call(
        body, name="out_proj", grid=(s // t,),
        in_specs=[_row_spec(t, D), _full_spec((D, D)), _row_spec(t, D), _full_spec((1, D)), _full_spec((1, D))],
        out_specs=[_row_spec(t, D), _row_spec(t, D), _row_spec(t, D)],
        out_shape=[_sds((s, D)), _sds((s, D)), _sds((s, D), MXU)],
        compiler_params=_cp("parallel"),
    )(merged, w_out, x, g2, g3)


def _down_loss(pre, w_down, h1, target, g4):
    s = pre.shape[0]
    t = min(256, s)

    def body(pre_ref, w_ref, h1_ref, tg_ref, g4_ref, dout_ref, dff_ref, loss_ref, dg4_ref):
        @pl.when(pl.program_id(0) == 0)
        def _():
            loss_ref[...] = jnp.zeros_like(loss_ref)
            dg4_ref[...] = jnp.zeros_like(dg4_ref)

        ff = _mdot(_relu2(pre_ref[...]), w_ref[...])
        r4 = _rms(ff)
        g4v = g4_ref[...]
        diff = h1_ref[...] + ff * r4 * g4v - tg_ref[...]
        sq = jnp.sum(jnp.sum(diff * diff, axis=1, keepdims=True), axis=0, keepdims=True)
        loss_ref[...] += (0.5 / D) * sq
        dout = diff * (1.0 / D)
        dout_ref[...] = dout
        dff, dg = _rms_bwd(ff, r4, g4v, dout)
        dff_ref[...] = dff.astype(dff_ref.dtype)
        dg4_ref[...] += dg

    return pl.pallas_call(
        body, name="down_loss", grid=(s // t,),
        in_specs=[_row_spec(t, DFF), _full_spec((DFF, D)), _row_spec(t, D), _row_spec(t, D), _full_spec((1, D))],
        out_specs=[_row_spec(t, D), _row_spec(t, D), _full_spec((1, 128)), _full_spec((1, D))],
        out_shape=[_sds((s, D)), _sds((s, D), MXU), _sds((1, 128)), _sds((1, D))],
        compiler_params=_cp("arbitrary"),
    )(pre, w_down, h1, target, g4)


def _dv_norms(dpre, w_up, h1, mix, dout, g3, g2):
    s = h1.shape[0]
    t = min(256, s)

    def body(dp_ref, w_ref, h1_ref, mix_ref, dout_ref, g3_ref, g2_ref, dh1_ref, dmix_ref, dg3_ref, dg2_ref):
        @pl.when(pl.program_id(0) == 0)
        def _():
            dg3_ref[...] = jnp.zeros_like(dg3_ref)
            dg2_ref[...] = jnp.zeros_like(dg2_ref)

        dv = _mdot(dp_ref[...], w_ref[...], 1, 1)
        h1 = h1_ref[...]
        dh1n, dg3 = _rms_bwd(h1, _rms(h1), g3_ref[...], dv)
        dh1 = dout_ref[...] + dh1n
        dh1_ref[...] = dh1
        mix = mix_ref[...]
        dmix, dg2 = _rms_bwd(mix, _rms(mix), g2_ref[...], dh1)
        dmix_ref[...] = dmix.astype(dmix_ref.dtype)
        dg3_ref[...] += dg3
        dg2_ref[...] += dg2

    return pl.pallas_call(
        body, name="dv_norms", grid=(s // t,),
        in_specs=[_row_spec(t, DFF), _full_spec((D, DFF)), _row_spec(t, D), _row_spec(t, D), _row_spec(t, D),
                  _full_spec((1, D)), _full_spec((1, D))],
        out_specs=[_row_spec(t, D), _row_spec(t, D), _full_spec((1, D)), _full_spec((1, D))],
        out_shape=[_sds((s, D)), _sds((s, D), MXU), _sds((1, D)), _sds((1, D))],
        compiler_params=_cp("arbitrary"),
    )(dpre, w_up, h1, mix, dout, g3, g2)


def _lru_bwd(dmerged, ya, xr, h, proj5, wa_bd, wx_bd, ba, bx, lam):
    s = xr.shape[0]
    t = min(128, s)
    n = s // t
    rs = functools.partial(_rev_spec, t, D, n)

    def body(dm_ref, ya_ref, xr_ref, h_ref, hp_ref, g_ref, ga_ref, gb_ref, wa_ref, wx_ref, ba_ref, bx_ref, lam_ref,
             dya_ref, dga_ref, dgb_ref, dg_ref, dxr_ref, dpr_ref, dpi_ref, dlam_ref, dba_ref, dbx_ref, gc_ref,
             af_ref, an_ref, gs_ref):
        i = pl.program_id(0)

        @pl.when(i == 0)
        def _():
            gc_ref[...] = jnp.zeros_like(gc_ref)
            af_ref[...] = jnp.zeros_like(af_ref)
            dlam_ref[...] = jnp.zeros_like(dlam_ref)
            dba_ref[...] = jnp.zeros_like(dba_ref)
            dbx_ref[...] = jnp.zeros_like(dbx_ref)

        xrv = xr_ref[...]
        gr, gi, sp, a, mult = _lru_gates(xrv, wa_ref, wx_ref, ba_ref, bx_ref, lam_ref)
        hv = h_ref[...]
        gv = g_ref[...]
        dm = dm_ref[...]
        yav = ya_ref[...]
        sa = _sig(ga_ref[...])
        sb = _sig(gb_ref[...])
        gel = _gelu(gv)
        dya_ref[...] = dm * sa
        dga_ref[...] = (dm * yav * sa * (1.0 - sa)).astype(dga_ref.dtype)
        dyb = dm * sb
        dgb_ref[...] = (dyb * hv * gel * (1.0 - sb)).astype(dgb_ref.dtype)
        dg_ref[...] = (dyb * hv * _dgelu(gv)).astype(dg_ref.dtype)
        row = lax.broadcasted_iota(jnp.int32, (t, D), 0)
        an_ref[...] = jnp.where(row == t - 1, af_ref[0:1, :], pltpu.roll(a, t - 1, 0))
        gs_ref[...] = dyb * gel
        row8 = lax.broadcasted_iota(jnp.int32, (8, D), 0)

        def slab(k, carry):
            r0 = pl.multiple_of((t // 8 - 1 - k) * 8, 8)
            a8 = an_ref[pl.ds(r0, 8), :]
            u8 = gs_ref[pl.ds(r0, 8), :]
            for sh in (1, 2, 4):
                m = row8 < 8 - sh
                u8 = jnp.where(m, u8 + a8 * pltpu.roll(u8, 8 - sh, 0), u8)
                a8 = jnp.where(m, a8 * pltpu.roll(a8, 8 - sh, 0), a8)
            g8 = u8 + a8 * gc_ref[0:1, :]
            gs_ref[pl.ds(r0, 8), :] = g8
            gc_ref[...] = g8
            return carry

        lax.fori_loop(0, t // 8, slab, 0, unroll=4)
        gfull = gs_ref[...]
        af_ref[0:1, :] = jnp.sum(jnp.where(row == 0, a, 0.0), axis=0, keepdims=True)
        hlast = jnp.where(i == n - 1, 0.0, hp_ref[7:8, :])
        hprev = jnp.where(row == 0, hlast, pltpu.roll(hv, 1, 0))
        da = gfull * hprev
        dmult = gfull * gi * xrv
        dgi = gfull * mult * xrv
        dla = da * a - dmult * a * a / mult
        dgr = dla * (-LRU_C * sp)
        dsp = jnp.sum(dla * (-LRU_C * gr), axis=0, keepdims=True)
        dlam_ref[...] += dsp * (-_sig(-lam_ref[...]))
        dpr = dgr * gr * (1.0 - gr)
        dpi = dgi * gi * (1.0 - gi)
        dpr_ref[...] = dpr.astype(dpr_ref.dtype)
        dpi_ref[...] = dpi.astype(dpi_ref.dtype)
        dba_ref[...] += jnp.sum(dpr, axis=0, keepdims=True)
        dbx_ref[...] += jnp.sum(dpi, axis=0, keepdims=True)
        dxr_ref[...] = gfull * mult * gi + _mdot(dpr, wa_ref[...], 1, 1) + _mdot(dpi, wx_ref[...], 1, 1)

    hp_spec = pl.BlockSpec((8, D), lambda i: (jnp.maximum((n - 1 - i) * (t // 8) - 1, 0), 0))
    return pl.pallas_call(
        body, name="lru_bwd", grid=(n,),
        in_specs=[rs(), rs(), rs(), rs(), hp_spec, rs(1), rs(3), rs(4), _full_spec((D, D)), _full_spec((D, D)),
                  _full_spec((1, D)), _full_spec((1, D)), _full_spec((1, D))],
        out_specs=[rs(), rs(), rs(), rs(), rs(), rs(), rs(), _full_spec((1, D)), _full_spec((1, D)),
                   _full_spec((1, D))],
        out_shape=[_sds((s, D)), _sds((s, D), MXU), _sds((s, D), MXU), _sds((s, D), MXU), _sds((s, D)),
                   _sds((s, D), MXU), _sds((s, D), MXU), _sds((1, D)), _sds((1, D)), _sds((1, D))],
        scratch_shapes=[pltpu.VMEM((8, D), F32), pltpu.VMEM((8, D), F32), pltpu.VMEM((t, D), F32),
                        pltpu.VMEM((t, D), F32)],
        compiler_params=_cp("arbitrary"),
    )(dmerged, ya, xr, h, h, proj5, proj5, proj5, wa_bd, wx_bd, ba, bx, lam)


def _ssd_bwd(dya, y, proj5, xbc_c, dt, states, a_log, dskip_x, ssm_norm, expand, reduce_):
    s = xbc_c.shape[0]
    nc = s // CH
    rv = functools.partial(_rev_spec, CH)

    def body(dya_ref, y_ref, z_ref, xc_ref, dt_ref, st_ref, alog_ref, dsk_ref, ng_ref, e_ref, et_ref, dz_ref,
             dxc_ref, ddt_ref, dng_ref, ddsk_ref, dalog_ref, dh_ref, at_ref, dtt_ref, dat_ref, ddtt_ref, dy_ref,
             yoffdy_ref, xbds_ref):
        @pl.when(pl.program_id(0) == 0)
        def _():
            dh_ref[...] = jnp.zeros_like(dh_ref)
            dng_ref[...] = jnp.zeros_like(dng_ref)
            ddsk_ref[...] = jnp.zeros_like(ddsk_ref)
            dalog_ref[...] = jnp.zeros_like(dalog_ref)

        cs = _ssd_chunk_setup(dt_ref, alog_ref, e_ref, at_ref, dtt_ref)
        lane, row = cs["lane"], cs["row"]
        et = et_ref[...]
        for g in range(NG):
            gs = slice(GW * g, GW * (g + 1))
            yv = y_ref[:, gs]
            zv = z_ref[:, gs]
            sz = _silu(zv)
            yg = yv * sz
            dyav = dya_ref[:, gs]
            dyg, dng = _rms_bwd(yg, _rms(yg), ng_ref[:, gs], dyav)
            dng_ref[:, gs] += dng
            dy_ref[:, gs] = dyg * sz
            dz_ref[:, gs] = (dyg * yv * _dsilu(zv)).astype(dz_ref.dtype)
        dyv = dy_ref[...]
        xs = xc_ref[:, 0:D]
        ddsk_ref[...] += jnp.sum(dyv * xs, axis=0, keepdims=True)
        dxc_ref[:, 0:D] = dyv * dsk_ref[...]
        dat_ref[...] = jnp.zeros_like(dat_ref)
        ddtt_ref[...] = jnp.zeros_like(ddtt_ref)
        hh = jnp.sum(dh_ref[...] * st_ref[0], axis=0, keepdims=True)
        deal = jnp.max(_xdot(jnp.broadcast_to(hh, (8, D)), et, 3), axis=0, keepdims=True)
        d_acum = jnp.zeros((CH, 128), F32)
        for g in range(NG):
            gs = slice(GW * g, GW * (g + 1))
            bs_ = slice(D + NS * g, D + NS * (g + 1))
            cs_ = slice(D + NG * NS + NS * g, D + NG * NS + NS * (g + 1))
            bg = xc_ref[:, bs_]
            cg = xc_ref[:, cs_]
            cb = _mdot(cg, bg, 1, 1)
            hg = st_ref[0, :, gs]
            dhg = dh_ref[:, gs]
            dyg_ = dy_ref[:, gs]
            xsg = xc_ref[:, gs]
            ea = cs["ea_x"][:, gs]
            wsx = cs["ws_x"][:, gs]
            dp = dyg_ * ea
            yoffdy_ref[:, gs] = dp * _mdot(cg, hg)
            dc = _mdot(dp, hg, 1, 1)
            dhprev = _mdot(cg, dp, 0, 0)
            bds = _mdot(bg, dhg)
            dxc_ref[:, gs] += wsx * bds
            xbds_ref[:, gs] = xsg * bds
            db = _mdot(xsg * wsx, dhg, 1, 1)
            dh_ref[:, gs] = dhprev + cs["eal_x"][:, gs] * dhg
            dcbs = jnp.zeros((CH, CH), F32)
            for j in range(4 * g, 4 * g + 4):
                ps = slice(128 * j, 128 * (j + 1))
                xp = xc_ref[:, ps]
                dyp = dy_ref[:, ps]
                dxacc = jnp.zeros((CH, 128), F32)
                for hf in range(2):
                    hd = 2 * j + hf
                    ld, rowdt = _head_decay(cs, at_ref, dtt_ref, hd)
                    hm = (lane >= HP) if hf else (lane < HP)
                    dym = jnp.where(hm, dyp, 0.0)
                    w = cb * ld * rowdt
                    dw = _mdot(dym, jnp.where(hm, xp, 0.0), 1, 1)
                    dxacc = dxacc + _mdot(w, dym, 0, 0)
                    nm = dw * w
                    ddtt_ref[hd:hd + 1, :] += jnp.sum(dw * cb * ld, axis=0, keepdims=True)
                    d_acum = d_acum + jnp.where(lane == hd, jnp.sum(nm, axis=1, keepdims=True), 0.0)
                    dat_ref[hd:hd + 1, :] -= jnp.sum(nm, axis=0, keepdims=True)
                    dcbs = dcbs + dw * ld * rowdt
                dxc_ref[:, ps] += dxacc
            dxc_ref[:, bs_] = db + _mdot(dcbs, cg, 0, 0)
            dxc_ref[:, cs_] = dc + _mdot(dcbs, bg)
        dws = _xdot(xbds_ref[...], et, 2)
        ws = cs["ws"]
        d_acum = d_acum - dws * ws + _xdot(yoffdy_ref[...], et, 2) + dat_ref[...].T
        d_alast = jnp.sum(dws * ws, axis=0, keepdims=True) + deal * cs["eal"]
        d_acum = d_acum + jnp.where(row == CH - 1, d_alast, 0.0)
        triu = row <= lane
        d_adt = _xdot(triu.astype(F32), d_acum, 3, split_b=True)
        ddt_ref[...] = dws * jnp.exp(cs["alast"] - cs["acum"]) + ddtt_ref[...].T + d_adt * cs["a"]
        dalog_ref[...] += jnp.sum(d_adt * cs["dtv"], axis=0, keepdims=True) * cs["a"]

    return pl.pallas_call(
        body, name="ssd_bwd", grid=(nc,),
        in_specs=[rv(D, nc), rv(D, nc), rv(D, nc, 0), rv(XBC, nc), rv(128, nc),
                  pl.BlockSpec((1, NS, D), lambda i: (nc - 1 - i, 0, 0)), _full_spec((1, 128)), _full_spec((1, D)),
                  _full_spec((1, D)), _full_spec((128, D)), _full_spec((D, 128))],
        out_specs=[rv(D, nc), rv(XBC, nc), rv(128, nc), _full_spec((1, D)), _full_spec((1, D)),
                   _full_spec((1, 128))],
        out_shape=[_sds((s, D), MXU), _sds((s, XBC)), _sds((s, 128)), _sds((1, D)), _sds((1, D)), _sds((1, 128))],
        scratch_shapes=[pltpu.VMEM((NS, D), F32), pltpu.VMEM((128, 128), F32), pltpu.VMEM((128, 128), F32),
                        pltpu.VMEM((128, 128), F32), pltpu.VMEM((128, 128), F32), pltpu.VMEM((CH, D), F32),
                        pltpu.VMEM((CH, D), F32), pltpu.VMEM((CH, D), F32)],
        compiler_params=_cp("arbitrary"),
    )(dya, y, proj5, xbc_c, dt, states, a_log, dskip_x, ssm_norm, expand, reduce_)


def _conv_bwd(dxbc_c, dsilu, dxr, ddt, xbc_raw, proj5, dt_raw, cw_s, cw_l, dt_bias):
    s = xbc_raw.shape[0]
    t = min(256, s)
    n = s // t

    def body(dxc_ref, dsl_ref, dxr_ref, ddt_ref, xs_ref, xl_ref, dtr_ref, cws_ref, cwl_ref, dtb_ref, dxs_ref,
             dxl_ref, ddtr_ref, dcws_ref, dcbs_ref, dcwl_ref, dcbl_ref, ddtb_ref, ds_ref, dl_ref):
        @pl.when(pl.program_id(0) == 0)
        def _():
            ds_ref[t:t + 8, :] = jnp.zeros((8, XBC), F32)
            dl_ref[t:t + 8, :] = jnp.zeros((8, D), F32)
            for r in (dcws_ref, dcbs_ref, dcwl_ref, dcbl_ref, ddtb_ref):
                r[...] = jnp.zeros_like(r)

        ds_ref[0:t, :] = dxc_ref[...] * dsl_ref[...].astype(F32)
        dl_ref[0:t, :] = dxr_ref[...]

        def back(dbuf, x_ref, w_ref, dx_ref, dw_ref, db_ref):
            xv = x_ref[...]
            dpre = dbuf[0:t, :]
            dx = w_ref[3:4, :] * dpre
            dw_ref[3:4, :] += jnp.sum(dpre * xv, axis=0, keepdims=True)
            db_ref[...] += jnp.sum(dpre, axis=0, keepdims=True)
            for k in (1, 2, 3):
                ahead = dbuf[k:t + k, :]
                dx = dx + w_ref[3 - k:4 - k, :] * ahead
                dw_ref[3 - k:4 - k, :] += jnp.sum(ahead * xv, axis=0, keepdims=True)
            dx_ref[...] = dx.astype(dx_ref.dtype)
            dbuf[t:t + 8, :] = dbuf[0:8, :]

        back(ds_ref, xs_ref, cws_ref, dxs_ref, dcws_ref, dcbs_ref)
        back(dl_ref, xl_ref, cwl_ref, dxl_ref, dcwl_ref, dcbl_ref)
        ddtr = ddt_ref[...] * _sig(dtr_ref[...] + dtb_ref[...])
        ddtr_ref[...] = ddtr.astype(ddtr_ref.dtype)
        ddtb_ref[...] += jnp.sum(ddtr, axis=0, keepdims=True)

    rv = functools.partial(_rev_spec, t)
    return pl.pallas_call(
        body, name="conv_bwd", grid=(n,),
        in_specs=[rv(XBC, n), rv(XBC, n), rv(D, n), rv(128, n), rv(XBC, n), rv(D, n, 2), rv(128, n),
                  _full_spec((4, XBC)), _full_spec((4, D)), _full_spec((1, 128))],
        out_specs=[rv(XBC, n), rv(D, n), rv(128, n), _full_spec((4, XBC)), _full_spec((1, XBC)), _full_spec((4, D)),
                   _full_spec((1, D)), _full_spec((1, 128))],
        out_shape=[_sds((s, XBC), MXU), _sds((s, D), MXU), _sds((s, 128), MXU), _sds((4, XBC)), _sds((1, XBC)),
                   _sds((4, D)), _sds((1, D)), _sds((1, 128))],
        scratch_shapes=[pltpu.VMEM((t + 8, XBC), F32), pltpu.VMEM((t + 8, D), F32)],
        compiler_params=_cp("arbitrary"),
    )(dxbc_c, dsilu, dxr, ddt, xbc_raw, proj5, dt_raw, cw_s, cw_l, dt_bias)


def _du_norm(pieces5, dxbc, ddtr, w5, wxbc, wdt, x, dh1, g1):
    s = x.shape[0]
    t = min(256, s)

    def body(p0, p1, p2, p3, p4, dxbc_ref, ddtr_ref, w5_ref, wx_ref, wd_ref, x_ref, dh1_ref, g1_ref, dx_ref, dg1_ref):
        @pl.when(pl.program_id(0) == 0)
        def _():
            dg1_ref[...] = jnp.zeros_like(dg1_ref)

        du = _mdot(dxbc_ref[...], wx_ref[...], 1, 1) + _mdot(ddtr_ref[...], wd_ref[...], 1, 1)
        for b, p in enumerate((p0, p1, p2, p3, p4)):
            du = du + _mdot(p[...], w5_ref[:, D * b:D * (b + 1)], 1, 1)
        xv = x_ref[...]
        dxn, dg1 = _rms_bwd(xv, _rms(xv), g1_ref[...], du)
        dx_ref[...] = dh1_ref[...] + dxn
        dg1_ref[...] += dg1

    return pl.pallas_call(
        body, name="du_norm", grid=(s // t,),
        in_specs=[_row_spec(t, D)] * 5 + [_row_spec(t, XBC), _row_spec(t, 128), _full_spec((D, 5 * D)),
                                          _full_spec((D, XBC)), _full_spec((D, 128)), _row_spec(t, D),
                                          _row_spec(t, D), _full_spec((1, D))],
        out_specs=[_row_spec(t, D), _full_spec((1, D))],
        out_shape=[_sds((s, D)), _sds((1, D))],
        compiler_params=_cp("arbitrary"),
    )(*pieces5, dxbc, ddtr, w5, wxbc, wdt, x, dh1, g1)


def _adamw(w, g, m, v, name):
    r, c = w.shape
    t = r
    if r * c > 256 * 1024:
        t = next(cand for cand in (512, 256, 128, 64, 32, 16, 8) if r % cand == 0 and cand * c <= 512 * 1024)
    bc1 = 1.0 - ADAM_B1 ** ADAM_STEP
    bc2 = 1.0 - ADAM_B2 ** ADAM_STEP

    def body(w_ref, g_ref, m_ref, v_ref, d_ref, nm_ref, nv_ref):
        gv = g_ref[...]
        nm = ADAM_B1 * m_ref[...] + (1.0 - ADAM_B1) * gv
        nv = ADAM_B2 * v_ref[...] + (1.0 - ADAM_B2) * (gv * gv)
        nm_ref[...] = nm
        nv_ref[...] = nv
        d_ref[...] = -ADAM_LR * ((nm / bc1) / (jnp.sqrt(nv / bc2) + ADAM_EPS) + ADAM_WD * w_ref[...])

    spec = pl.BlockSpec((t, c), lambda i: (i, 0))
    return pl.pallas_call(
        body, name=name, grid=(r // t,), in_specs=[spec] * 4, out_specs=[spec] * 3,
        out_shape=[_sds((r, c))] * 3, compiler_params=_cp("parallel"),
    )(w, g, m, v)


def _adamw_halves(w, g_mine, g_other, m, v, cidx, name):
    r, c = w.shape
    hr = r // 2
    t = 256 if hr % 256 == 0 else 128
    nb = hr // t
    bc1 = 1.0 - ADAM_B1 ** ADAM_STEP
    bc2 = 1.0 - ADAM_B2 ** ADAM_STEP

    def body(c_ref, w_ref, gm_ref, go_ref, m_ref, v_ref, g_ref, d_ref, nm_ref, nv_ref):
        mine = (pl.program_id(0) // nb) == c_ref[0]
        gv = jnp.where(mine, gm_ref[...], go_ref[...])
        g_ref[...] = gv
        nm = ADAM_B1 * m_ref[...] + (1.0 - ADAM_B1) * gv
        nv = ADAM_B2 * v_ref[...] + (1.0 - ADAM_B2) * (gv * gv)
        nm_ref[...] = nm
        nv_ref[...] = nv
        d_ref[...] = -ADAM_LR * ((nm / bc1) / (jnp.sqrt(nv / bc2) + ADAM_EPS) + ADAM_WD * w_ref[...])

    spec = pl.BlockSpec((t, c), lambda i, c_ref: (i, 0))
    half = pl.BlockSpec((t, c), lambda i, c_ref: (i % nb, 0))
    return pl.pallas_call(
        body, name=name,
        grid_spec=pltpu.PrefetchScalarGridSpec(num_scalar_prefetch=1, grid=(2 * nb,),
                                               in_specs=[spec, half, half, spec, spec], out_specs=[spec] * 4),
        out_shape=[_sds((r, c))] * 4, compiler_params=_cp("parallel"),
    )(cidx, w, g_mine, g_other, m, v)


def _block_diag(w):
    eye = jnp.eye(NH, dtype=w.dtype)
    return (w[:, :, None, :] * eye[:, None, :, None]).reshape(D, D)


def _diag_blocks(full):
    return jnp.stack([full[HP * h:HP * (h + 1), HP * h:HP * (h + 1)] for h in range(NH)])


def _pad_lanes(v, n=128):
    return jnp.pad(v, ((0, 0), (0, n - v.shape[1])))


def _local_step(x, target, p):
    heads = jnp.arange(D, dtype=jnp.int32) // HP
    expand = (jnp.arange(128, dtype=jnp.int32)[:, None] == heads[None, :]).astype(F32)
    reduce_ = expand.T
    dskip_x = jnp.repeat(p["d_skip"], HP, axis=1)
    a_log = _pad_lanes(p["a_log"])
    dt_bias = _pad_lanes(p["dt_bias"])
    w5, wxbc, wdt = p["w5"], p["wxbc"], p["wdt"]
    wa_bd = _block_diag(p["lru_wa"]).astype(MXU)
    wx_bd = _block_diag(p["lru_wx"]).astype(MXU)
    ba = p["lru_ba"].reshape(1, D)
    bx = p["lru_bx"].reshape(1, D)

    u = _norm_cast(x, p["norm_mix_pre"], "norm_u")
    proj5 = _matmul(u, w5, name="proj5")
    xbc_raw = _matmul(u, wxbc, name="proj_xbc", tn=XBC)
    dt_raw = _matmul(u, wdt, name="proj_dt")
    xbc_c, dsilu, xr, dt = _conv_fwd(xbc_raw, proj5, dt_raw, p["conv_ssm_w"], p["conv_ssm_b"], p["conv_lru_w"],
                                     p["conv_lru_b"], dt_bias)
    y, ya, states = _ssd_fwd(xbc_c, dt, proj5, a_log, dskip_x, p["ssm_norm"], expand)
    h, merged = _lru_fwd(xr, proj5, ya, wa_bd, wx_bd, ba, bx, p["lru_lambda"])
    mix, h1, v = _out_proj(merged, p["w_out"], x, p["norm_mix_post"], p["norm_mlp_pre"])
    pre = _matmul(v, p["w_up"], name="up_proj")
    dout, dff, loss, dg4 = _down_loss(pre, p["w_down"], h1, target, p["norm_mlp_post"])

    dpre = _matmul(dff, p["w_down"], name="d_pre", tb=True, out_dtype=MXU,
                   epi=lambda r, pr: r * (2.0 * jnp.maximum(pr, 0.0)), epi_args=(pre,))
    g_w_down = _matmul(pre, dff, name="dw_down", ta=True, tm=1024, tn=1024, tk=512, a_fn=_relu2)
    dh1, dmix, dg3, dg2 = _dv_norms(dpre, p["w_up"], h1, mix, dout, p["norm_mlp_pre"], p["norm_mix_post"])
    g_w_up = _matmul(v, dpre, name="dw_up", ta=True, tm=1024, tn=1024, tk=512)
    dmerged = _matmul(dmix, p["w_out"], name="d_merged", tb=True)
    g_w_out = _matmul(merged, dmix, name="dw_out", ta=True, tm=1024, tn=1024, tk=512)
    (dya, dga, dgb, dg, dxr, dpr, dpi, dlam, dba, dbx) = _lru_bwd(dmerged, ya, xr, h, proj5, wa_bd, wx_bd, ba, bx,
                                                                  p["lru_lambda"])
    g_wa = _diag_blocks(_matmul(xr, dpr, name="dw_lru_a", ta=True, tm=1024, tn=1024, tk=512))
    g_wx = _diag_blocks(_matmul(xr, dpi, name="dw_lru_x", ta=True, tm=1024, tn=1024, tk=512))
    dz, dxbc_c, ddt, dng, ddsk, dalog = _ssd_bwd(dya, y, proj5, xbc_c, dt, states, a_log, dskip_x, p["ssm_norm"],
                                                 expand, reduce_)
    (dxbc, dxl, ddtr, dcws, dcbs, dcwl, dcbl, ddtb) = _conv_bwd(dxbc_c, dsilu, dxr, ddt, xbc_raw, proj5, dt_raw,
                                                                p["conv_ssm_w"], p["conv_lru_w"], dt_bias)
    pieces5 = (dz, dg, dxl, dga, dgb)
    grad_x, dg1 = _du_norm(pieces5, dxbc, ddtr, w5, wxbc, wdt, x, dh1, p["norm_mix_pre"])
    gw5 = [_matmul(u, pc, name=f"dw_in_{i}", ta=True, tm=1024, tn=1024, tk=512) for i, pc in enumerate(pieces5)]
    gwxbc = _matmul(u, dxbc, name="dw_in_xbc", ta=True, tm=1024, tn=XBC, tk=512)
    gwdt = _matmul(u, ddtr, name="dw_in_dt", ta=True, tm=1024, tn=128, tk=512)
    g_w_in = jnp.concatenate([gw5[0], gwxbc, gwdt[:, :NH], gw5[1], gw5[2], gw5[3], gw5[4]], axis=1)
    grads = {
        "norm_mix_pre": dg1, "w_in": g_w_in, "conv_ssm_w": dcws, "conv_ssm_b": dcbs, "dt_bias": ddtb[:, :NH],
        "a_log": dalog[:, :NH], "d_skip": ddsk.reshape(NH, HP).sum(axis=1)[None, :], "ssm_norm": dng,
        "conv_lru_w": dcwl, "conv_lru_b": dcbl, "lru_wa": g_wa, "lru_ba": dba.reshape(NH, HP), "lru_wx": g_wx,
        "lru_bx": dbx.reshape(NH, HP), "lru_lambda": dlam, "w_out": g_w_out, "norm_mix_post": dg2,
        "norm_mlp_pre": dg3, "w_up": g_w_up, "w_down": g_w_down, "norm_mlp_post": dg4,
    }
    return loss[0, 0], grad_x, grads


def _split_w_in(w_in_full):
    z, xbc, dtc, g, xl, ga, gb = jnp.split(w_in_full, [D, D + XBC, D + XBC + NH, 2 * D + XBC + NH,
                                                        3 * D + XBC + NH, 4 * D + XBC + NH], axis=1)
    return jnp.concatenate([z, g, xl, ga, gb], axis=1), xbc, _pad_lanes(dtc)


ANY = pl.BlockSpec(memory_space=pl.ANY)
COMM = BF16


def _place():
    x, y, c = lax.axis_index("x"), lax.axis_index("y"), lax.axis_index("c")
    chips = [(1 - x, y), (x, 1 - y), (1 - x, 1 - y)]
    return x, y, c, chips


def _remote(src, dst, send_sem, recv_sem, to):
    return pltpu.make_async_remote_copy(src_ref=src, dst_ref=dst, send_sem=send_sem, recv_sem=recv_sem, device_id=to,
                                        device_id_type=MESH)


def _gather_weights(big, small):
    nb, ns = len(big), len(small)
    na = nb + ns

    def body(*refs):
        ins, outs = refs[:na], refs[na:2 * na]
        send_sems, recv_sems = refs[2 * na:]
        x, y, c, chips = _place()
        k = 2 * x + y
        sends = []
        for a in range(na):
            if a < nb:
                hr = ins[a].shape[0] // 2
                src = ins[a].at[pl.ds(c * hr, hr)]
                dst = outs[a].at[k, pl.ds(c * hr, hr)]
            else:
                src, dst = ins[a], outs[a].at[k]
            for j, (cx, cy) in enumerate(chips):
                sends.append(_remote(src, dst, send_sems.at[a, j], recv_sems.at[a, j], (cx, cy, c)))
        for cp in sends:
            cp.start()
        for j, (cx, cy) in enumerate(chips):
            kj = 2 * cx + cy
            for a in range(na):
                if a < nb:
                    hr = ins[a].shape[0] // 2
                    got = outs[a].at[kj, pl.ds(c * hr, hr)]
                    _remote(got, got, send_sems.at[a, j], recv_sems.at[a, j], (cx, cy, c)).wait_recv()
                    fwd = _remote(got, got, send_sems.at[a, 3 + j], recv_sems.at[a, 3 + j], (x, y, 1 - c))
                    fwd.start()
                    sends.append(fwd)
                else:
                    got = outs[a].at[kj]
                    _remote(got, got, send_sems.at[a, j], recv_sems.at[a, j], (cx, cy, c)).wait_recv()
        for j, (cx, cy) in enumerate(chips):
            kj = 2 * cx + cy
            for a in range(nb):
                hr = ins[a].shape[0] // 2
                got = outs[a].at[kj, pl.ds((1 - c) * hr, hr)]
                _remote(got, got, send_sems.at[a, 3 + j], recv_sems.at[a, 3 + j], (x, y, 1 - c)).wait_recv()
        for cp in sends:
            cp.wait_send()

    arrs = list(big) + list(small)
    outs = pl.pallas_call(
        body, name="gather_weights", in_specs=[ANY] * na, out_specs=[ANY] * na,
        out_shape=[_sds((NCHIP,) + a.shape, a.dtype) for a in arrs],
        scratch_shapes=[pltpu.SemaphoreType.DMA((na, 6)), pltpu.SemaphoreType.DMA((na, 6))],
    )(*arrs)
    kchip = 2 * lax.axis_index("x") + lax.axis_index("y")
    return [lax.dynamic_update_index_in_dim(o, a, kchip, 0) for o, a in zip(outs, arrs)]


def _pair_exchange(gs):
    na = len(gs)

    def body(*refs):
        ins, outs = refs[:na], refs[na:2 * na]
        send_sems, recv_sems = refs[2 * na:]
        x, y, c, _ = _place()
        cps = []
        for a in range(na):
            hr = ins[a].shape[1] // 2
            cps.append(_remote(ins[a].at[:, pl.ds((1 - c) * hr, hr)], outs[a], send_sems.at[a], recv_sems.at[a],
                               (x, y, 1 - c)))
        for cp in cps:
            cp.start()
        for cp in cps:
            cp.wait()

    return pl.pallas_call(
        body, name="grad_pair_exchange", in_specs=[ANY] * na, out_specs=[ANY] * na,
        out_shape=[_sds((NCHIP, g.shape[1] // 2, g.shape[2]), g.dtype) for g in gs],
        scratch_shapes=[pltpu.SemaphoreType.DMA((na,)), pltpu.SemaphoreType.DMA((na,))],
    )(*gs)


def _pair_add(g, got, cidx, name):
    _, r, cdim = g.shape
    hr = r // 2
    t = 256 if hr % 256 == 0 else 128
    nt = hr // t

    def body(c_ref, g_ref, o_ref, p_ref, pc_ref):
        sm = g_ref[...] + o_ref[...]
        p_ref[...] = sm
        pc_ref[...] = sm.astype(pc_ref.dtype)

    spec = pl.BlockSpec((1, t, cdim), lambda k, i, c_ref: (k, i, 0))
    return pl.pallas_call(
        body, name=name,
        grid_spec=pltpu.PrefetchScalarGridSpec(
            num_scalar_prefetch=1, grid=(NCHIP, nt),
            in_specs=[pl.BlockSpec((1, t, cdim), lambda k, i, c_ref: (k, c_ref[0] * nt + i, 0)), spec],
            out_specs=[spec, spec]),
        out_shape=[_sds((NCHIP, hr, cdim)), _sds((NCHIP, hr, cdim), COMM)],
        compiler_params=_cp("parallel", "parallel"),
    )(cidx, g, got)


def _chip_exchange(ps):
    na = len(ps)

    def body(*refs):
        ins, outs = refs[:na], refs[na:2 * na]
        send_sems, recv_sems = refs[2 * na:]
        _, _, c, chips = _place()
        cps = []
        for a in range(na):
            for j, (cx, cy) in enumerate(chips):
                cps.append(_remote(ins[a].at[2 * cx + cy], outs[a].at[j], send_sems.at[a, j], recv_sems.at[a, j],
                                   (cx, cy, c)))
        for cp in cps:
            cp.start()
        for cp in cps:
            cp.wait()

    return pl.pallas_call(
        body, name="grad_chip_exchange", in_specs=[ANY] * na, out_specs=[ANY] * na,
        out_shape=[_sds((NCHIP - 1,) + p.shape[1:], p.dtype) for p in ps],
        scratch_shapes=[pltpu.SemaphoreType.DMA((na, 3)), pltpu.SemaphoreType.DMA((na, 3))],
    )(*ps)


def _shard_sum(p, got, kidx, name):
    _, hr, cdim = p.shape
    t = 256 if hr % 256 == 0 else 128

    def body(k_ref, p_ref, g_ref, o_ref):
        sm = p_ref[0]
        for j in range(NCHIP - 1):
            sm = sm + g_ref[j].astype(F32)
        o_ref[...] = sm

    return pl.pallas_call(
        body, name=name,
        grid_spec=pltpu.PrefetchScalarGridSpec(
            num_scalar_prefetch=1, grid=(hr // t,),
            in_specs=[pl.BlockSpec((1, t, cdim), lambda i, k_ref: (k_ref[0], i, 0)),
                      pl.BlockSpec((NCHIP - 1, t, cdim), lambda i, k_ref: (0, i, 0))],
            out_specs=pl.BlockSpec((t, cdim), lambda i, k_ref: (i, 0))),
        out_shape=_sds((hr, cdim)),
        compiler_params=_cp("parallel"),
    )(kidx, p, got)


def _pair_swap(rs):
    na = len(rs)

    def body(*refs):
        ins, outs = refs[:na], refs[na:2 * na]
        send_sems, recv_sems = refs[2 * na:]
        x, y, c, _ = _place()
        cps = [_remote(ins[a], outs[a], send_sems.at[a], recv_sems.at[a], (x, y, 1 - c)) for a in range(na)]
        for cp in cps:
            cp.start()
        for cp in cps:
            cp.wait()

    return pl.pallas_call(
        body, name="grad_pair_swap", in_specs=[ANY] * na, out_specs=[ANY] * na,
        out_shape=[_sds(r.shape, r.dtype) for r in rs],
        scratch_shapes=[pltpu.SemaphoreType.DMA((na,)), pltpu.SemaphoreType.DMA((na,))],
    )(*rs)


def _small_allgather(v):
    r = v.shape[0]

    def body(v_ref, out_ref, send_sems, recv_sems, local_sem):
        x, y, c, chips = _place()
        me, sibling = (x, y, c), (x, y, 1 - c)

        def slot(px, py, pc):
            return out_ref.at[4 * px + 2 * py + pc]

        def copy(k, block, to, src=None):
            return _remote(slot(*block) if src is None else src, slot(*block), send_sems.at[k], recv_sems.at[k], to)

        mine = pltpu.make_async_copy(v_ref, slot(*me), local_sem)
        mine.start()
        first = [copy(0, me, sibling, src=v_ref)]
        first += [copy(1 + j, me, (*chip, c), src=v_ref) for j, chip in enumerate(chips)]
        for cp in first:
            cp.start()
        passed = [copy(4 + j, (*chip, c), sibling) for j, chip in enumerate(chips)]
        for j, chip in enumerate(chips):
            copy(1 + j, (*chip, c), me).wait_recv()
            passed[j].start()
        copy(0, sibling, me).wait_recv()
        for j, chip in enumerate(chips):
            copy(4 + j, (*chip, 1 - c), me).wait_recv()
        for cp in first + passed:
            cp.wait_send()
        mine.wait()

    vm = pl.BlockSpec(memory_space=pltpu.VMEM)
    return pl.pallas_call(
        body, name="small_allgather", in_specs=[vm], out_specs=vm, out_shape=_sds((8, r, 128)),
        scratch_shapes=[pltpu.SemaphoreType.DMA((7,)), pltpu.SemaphoreType.DMA((7,)), pltpu.SemaphoreType.DMA],
        compiler_params=pltpu.CompilerParams(vmem_limit_bytes=VMEM_LIMIT),
    )(v)


def _sum_devices(allv):
    _, r, _ = allv.shape

    def body(a_ref, o_ref):
        sm = a_ref[0]
        for d in range(1, 8):
            sm = sm + a_ref[d]
        o_ref[...] = sm

    return pl.pallas_call(
        body, name="small_sum", grid=(1,), in_specs=[_full_spec((8, r, 128))], out_specs=_full_spec((r, 128)),
        out_shape=_sds((r, 128)), compiler_params=_cp("arbitrary"),
    )(allv)


def _pack(arrs):
    rows = []
    for a in arrs:
        f = a.reshape(-1)
        f = jnp.pad(f, (0, (-f.shape[0]) % 128))
        rows.append(f.reshape(-1, 128))
    out = jnp.concatenate(rows, axis=0)
    return jnp.pad(out, ((0, (-out.shape[0]) % 8), (0, 0)))


def _unpack(packed, shapes):
    outs, r0 = [], 0
    for shp in shapes:
        n = math.prod(shp)
        nr = -(-n // 128)
        outs.append(packed[r0:r0 + nr].reshape(-1)[:n].reshape(shp))
        r0 += nr
    return outs


BIG = ("w_in", "w_out", "w_up", "w_down")
CONV = ("conv_ssm_w", "conv_lru_w")
WEIGHTS = ("norm_mix_pre", "w_in", "conv_ssm_w", "conv_ssm_b", "dt_bias", "a_log", "d_skip", "ssm_norm", "conv_lru_w",
           "conv_lru_b", "lru_wa", "lru_ba", "lru_wx", "lru_bx", "lru_lambda", "w_out", "norm_mix_post",
           "norm_mlp_pre", "w_up", "w_down", "norm_mlp_post")
SMALL = tuple(n for n in WEIGHTS if n not in BIG and n not in CONV)


def kernel(x, norm_mix_pre, w_in, conv_ssm_w, conv_ssm_b, dt_bias, a_log, d_skip, ssm_norm, conv_lru_w, conv_lru_b, lru_wa, lru_ba, lru_wx, lru_bx, lru_lambda, w_out, norm_mix_post, norm_mlp_pre, w_up, w_down, norm_mlp_post, loss_target, m_norm_mix_pre, m_w_in, m_conv_ssm_w, m_conv_ssm_b, m_dt_bias, m_a_log, m_d_skip, m_ssm_norm, m_conv_lru_w, m_conv_lru_b, m_lru_wa, m_lru_ba, m_lru_wx, m_lru_bx, m_lru_lambda, m_w_out, m_norm_mix_post, m_norm_mlp_pre, m_w_up, m_w_down, m_norm_mlp_post, v_norm_mix_pre, v_w_in, v_conv_ssm_w, v_conv_ssm_b, v_dt_bias, v_a_log, v_d_skip, v_ssm_norm, v_conv_lru_w, v_conv_lru_b, v_lru_wa, v_lru_ba, v_lru_wx, v_lru_bx, v_lru_lambda, v_w_out, v_norm_mix_post, v_norm_mlp_pre, v_w_up, v_w_down, v_norm_mlp_post):
    args = locals()
    w = {n: args[n][0] for n in WEIGHTS}
    m = {n: args["m_" + n][0] for n in WEIGHTS}
    v = {n: args["v_" + n][0] for n in WEIGHTS}
    cidx = lax.axis_index("c").astype(jnp.int32).reshape(1)
    kchip = 2 * lax.axis_index("x") + lax.axis_index("y")
    kidx = kchip.astype(jnp.int32).reshape(1)

    gathered = _gather_weights([w[n].astype(MXU) for n in BIG], [w[n] for n in CONV])
    cat = lambda g: jnp.concatenate([g[k] for k in range(NCHIP)], axis=1)
    w5, wxbc, wdt = _split_w_in(cat(gathered[0]))
    p = {n: (w[n].reshape(1, -1) if w[n].ndim == 1 else w[n]) for n in SMALL}
    p.update(w5=w5, wxbc=wxbc, wdt=wdt, w_out=gathered[1].reshape(D, D), w_up=cat(gathered[2]),
             w_down=gathered[3].reshape(DFF, D), conv_ssm_w=cat(gathered[4]), conv_lru_w=cat(gathered[5]))

    loss, grad_x, g = _local_step(x[0], loss_target[0], p)
    loss = lax.psum(loss, ("x", "y", "c"))

    shard_major = [
        jnp.stack([g["w_in"][:, W_IN_SHARD * k:W_IN_SHARD * (k + 1)] for k in range(NCHIP)]),
        g["w_out"].reshape(NCHIP, D // NCHIP, D),
        jnp.stack([g["w_up"][:, D * k:D * (k + 1)] for k in range(NCHIP)]),
        g["w_down"].reshape(NCHIP, D, D),
    ]
    from_sibling = _pair_exchange(shard_major)
    pair = [_pair_add(gs, got, cidx, f"grad_pair_add_{n}") for gs, got, n in zip(shard_major, from_sibling, BIG)]
    from_chips = _chip_exchange([pc for _, pc in pair])
    halves = [_shard_sum(pf, got, kidx, f"grad_shard_sum_{n}") for (pf, _), got, n in zip(pair, from_chips, BIG)]
    others = _pair_swap(halves)

    reduced = {}
    small_shapes = [g[n].shape for n in SMALL + CONV]
    summed = _unpack(_sum_devices(_small_allgather(_pack([g[n] for n in SMALL + CONV]))), small_shapes)
    for n, s in zip(SMALL + CONV, summed):
        if n in CONV:
            width = w[n].shape[1]
            reduced[n] = lax.dynamic_slice_in_dim(s, kchip * width, width, axis=1)
        else:
            reduced[n] = s.reshape(w[n].shape)

    delta, new_m, new_v = {}, {}, {}
    for n, mine, other in zip(BIG, halves, others):
        reduced[n], delta[n], new_m[n], new_v[n] = _adamw_halves(w[n], mine, other, m[n], v[n], cidx, f"adamw_{n}")
    for n in CONV:
        delta[n], new_m[n], new_v[n] = _adamw(w[n], reduced[n], m[n], v[n], f"adamw_{n}")
    shapes = [w[n].shape for n in SMALL]
    packed = [_pack([d[n] for n in SMALL]) for d in (w, reduced, m, v)]
    for d, out in zip((delta, new_m, new_v), _adamw(*packed, "adamw_small")):
        d.update(zip(SMALL, _unpack(out, shapes)))

    lead = lambda d: [d[n][None] for n in WEIGHTS]
    return (loss, grad_x[None], *lead(reduced), *lead(delta), *lead(new_m), *lead(new_v))
```

```python
import functools
import math

import jax
import jax.numpy as jnp
from jax import lax
from jax.experimental import pallas as pl
from jax.experimental.pallas import tpu as pltpu

F32 = jnp.float32
BF16 = jnp.bfloat16
MXU = BF16

D = 1024
DFF = 4096
NH = 16
HP = 64
NG = 2
NS = 128
CH = 128
XBC = D + 2 * NG * NS
GW = D // NG
LRU_C = 8.0
EPS = 1e-6
NCHIP = 4
W_IN_COLS = 6672
W_IN_SHARD = W_IN_COLS // NCHIP

ADAM_LR = 0.001
ADAM_B1 = 0.9
ADAM_B2 = 0.999
ADAM_EPS = 1e-08
ADAM_WD = 0.01
ADAM_STEP = 10

VMEM_LIMIT = 56 * 1024 * 1024
MESH = pl.DeviceIdType.MESH


def _cp(*sem):
    return pltpu.CompilerParams(dimension_semantics=sem, vmem_limit_bytes=VMEM_LIMIT)


def _dot(a, b, ca=1, cb=0, prec=None):
    return lax.dot_general(a, b, (((ca,), (cb,)), ((), ())), precision=prec, preferred_element_type=F32)


def _mdot(a, b, ca=1, cb=0):
    return _dot(a.astype(MXU), b.astype(MXU), ca, cb)


def _bf16_parts(v, n):
    parts = []
    for i in range(n):
        p = v.astype(BF16)
        parts.append(p)
        if i < n - 1:
            v = v - p.astype(F32)
    return parts


def _xdot(a, b, passes, split_b=False):
    if split_b:
        a16 = a.astype(BF16)
        terms = [_dot(a16, p) for p in _bf16_parts(b, passes)]
    else:
        b16 = b.astype(BF16)
        terms = [_dot(p, b16) for p in _bf16_parts(a, passes)]
    return functools.reduce(lambda u, v: u + v, terms)


def _sig(x):
    return 0.5 * jnp.tanh(0.5 * x) + 0.5


def _silu(x):
    return x * _sig(x)


def _dsilu(x):
    s = _sig(x)
    return s * (1.0 + x * (1.0 - s))


def _softplus(x):
    e = jnp.exp(-jnp.abs(x))
    return jnp.maximum(x, 0.0) + jnp.where(e < 1e-4, e * (1.0 - 0.5 * e), jnp.log(1.0 + e))


_GELU_C = math.sqrt(2.0 / math.pi)


def _gelu(x):
    t = jnp.tanh(_GELU_C * (x + 0.044715 * x * x * x))
    return 0.5 * x * (1.0 + t)


def _dgelu(x):
    x2 = x * x
    t = jnp.tanh(_GELU_C * (x + 0.044715 * x * x2))
    return 0.5 * (1.0 + t) + 0.5 * x * (1.0 - t * t) * _GELU_C * (1.0 + 3.0 * 0.044715 * x2)


def _one_minus_sq(a, la):
    x = 2.0 * la
    series = -x * (1.0 + x * (0.5 + x * (1.0 / 6.0)))
    return jnp.where(x > -0.01, series, 1.0 - a * a)


def _rms(x):
    return lax.rsqrt(jnp.mean(x * x, axis=-1, keepdims=True) + EPS)


def _rms_bwd(x, r, g, dy):
    xn = x * r
    dxh = dy * g
    m = jnp.mean(dxh * xn, axis=-1, keepdims=True)
    return r * (dxh - xn * m), jnp.sum(dy * xn, axis=0, keepdims=True)


def _row_spec(t, c, col=0):
    return pl.BlockSpec((t, c), lambda i: (i, col))


def _rev_spec(t, c, n, col=0):
    return pl.BlockSpec((t, c), lambda i: (n - 1 - i, col))


def _full_spec(shape):
    nd = len(shape)
    return pl.BlockSpec(shape, lambda *_: (0,) * nd)


def _sds(shape, dtype=F32):
    return jax.ShapeDtypeStruct(shape, dtype)


ANY = pl.BlockSpec(memory_space=pl.ANY)


class _Plan:
    def __init__(self, ins, outs, sems, start, finish):
        self.ins, self.outs, self.sems, self.start, self.finish = list(ins), list(outs), list(sems), start, finish


def _merge_plans(*plans):
    def each(fn_name, ins, outs, sems):
        i = o = s = 0
        for p in plans:
            getattr(p, fn_name)(ins[i:i + len(p.ins)], outs[o:o + len(p.outs)], sems[s:s + len(p.sems)])
            i, o, s = i + len(p.ins), o + len(p.outs), s + len(p.sems)

    return _Plan([a for p in plans for a in p.ins], [a for p in plans for a in p.outs],
                 [a for p in plans for a in p.sems], functools.partial(each, "start"), functools.partial(each, "finish"))


def _pcall(body, args, *, name, grid, in_specs, out_specs, out_shape, sem, scratch_shapes=(), plan=None):
    single = not isinstance(out_shape, (list, tuple))
    out_specs = [out_specs] if single else list(out_specs)
    out_shape = [out_shape] if single else list(out_shape)
    if plan is None:
        outs = pl.pallas_call(body, name=name, grid=grid, in_specs=list(in_specs), out_specs=out_specs,
                              out_shape=out_shape, scratch_shapes=list(scratch_shapes),
                              compiler_params=_cp(*sem))(*args)
        return outs[0] if single else outs
    n_in, n_out, n_sc, ni, no = len(in_specs), len(out_shape), len(scratch_shapes), len(plan.ins), len(plan.outs)

    def hosted(*refs):
        b0 = n_in + ni
        b1 = b0 + n_out + no
        sem_refs = refs[b1 + n_sc:]
        sems = [(sem_refs[2 * q], sem_refs[2 * q + 1]) for q in range(len(plan.sems))]
        ids = [pl.program_id(d) for d in range(len(grid))]
        first = functools.reduce(jnp.logical_and, [i == 0 for i in ids])
        last = functools.reduce(jnp.logical_and, [i == g - 1 for i, g in zip(ids, grid)])

        @pl.when(first)
        def _():
            plan.start(refs[n_in:b0], refs[b0 + n_out:b1], sems)

        body(*refs[:n_in], *refs[b0:b0 + n_out], *refs[b1:b1 + n_sc])

        @pl.when(last)
        def _():
            plan.finish(refs[n_in:b0], refs[b0 + n_out:b1], sems)

    dma = [pltpu.SemaphoreType.DMA(shape) for shape in plan.sems for _ in range(2)]
    outs = pl.pallas_call(hosted, name=name, grid=grid, in_specs=list(in_specs) + [ANY] * ni,
                          out_specs=out_specs + [ANY] * no, out_shape=out_shape + plan.outs,
                          scratch_shapes=list(scratch_shapes) + dma,
                          compiler_params=_cp(*("arbitrary",) * len(grid)))(*args, *plan.ins)
    return (outs[0] if single else outs[:n_out]), outs[n_out:]


def _run_plan(plan, name):
    ni, no = len(plan.ins), len(plan.outs)

    def body(*refs):
        sem_refs = refs[ni + no:]
        sems = [(sem_refs[2 * q], sem_refs[2 * q + 1]) for q in range(len(plan.sems))]
        plan.start(refs[:ni], refs[ni:ni + no], sems)
        plan.finish(refs[:ni], refs[ni:ni + no], sems)

    return pl.pallas_call(
        body, name=name, in_specs=[ANY] * ni, out_specs=[ANY] * no, out_shape=plan.outs,
        scratch_shapes=[pltpu.SemaphoreType.DMA(shape) for shape in plan.sems for _ in range(2)],
    )(*plan.ins)


def _matmul(a, b, *, name, ta=False, tb=False, tm=512, tn=1024, tk=1024, out_dtype=F32, a_fn=None, epi=None,
            epi_args=(), plan=None):
    m, k = (a.shape[1], a.shape[0]) if ta else a.shape
    n = b.shape[0] if tb else b.shape[1]
    tm, tn, tk = min(tm, m), min(tn, n), min(tk, k)
    nk = k // tk
    a_spec = pl.BlockSpec((tk, tm), lambda i, j, kk: (kk, i)) if ta else pl.BlockSpec((tm, tk), lambda i, j, kk: (i, kk))
    b_spec = pl.BlockSpec((tn, tk), lambda i, j, kk: (j, kk)) if tb else pl.BlockSpec((tk, tn), lambda i, j, kk: (kk, j))
    e_specs = [pl.BlockSpec((tm, tn), lambda i, j, kk: (i, j)) for _ in epi_args]
    ne = len(epi_args)

    def body(a_ref, b_ref, *rest):
        e_refs, o_ref, acc_ref = rest[:ne], rest[ne], rest[ne + 1]
        kk = pl.program_id(2)

        @pl.when(kk == 0)
        def _():
            acc_ref[...] = jnp.zeros_like(acc_ref)

        av = a_ref[...]
        if a_fn is not None:
            av = a_fn(av)
        acc_ref[...] += _mdot(av, b_ref[...], 0 if ta else 1, 1 if tb else 0)

        @pl.when(kk == nk - 1)
        def _():
            r = acc_ref[...]
            if epi is not None:
                r = epi(r, *[e[...] for e in e_refs])
            o_ref[...] = r.astype(o_ref.dtype)

    return _pcall(
        body, (a, b, *epi_args), name=name, grid=(m // tm, n // tn, nk),
        in_specs=[a_spec, b_spec] + e_specs,
        out_specs=pl.BlockSpec((tm, tn), lambda i, j, kk: (i, j)),
        out_shape=_sds((m, n), out_dtype),
        scratch_shapes=[pltpu.VMEM((tm, tn), F32)],
        sem=("parallel", "parallel", "arbitrary"), plan=plan)


def _relu2(p):
    p = jnp.maximum(p, 0.0)
    return p * p


def _norm_cast(x, g, name):
    s = x.shape[0]
    t = min(512, s)

    def body(x_ref, g_ref, o_ref):
        xv = x_ref[...]
        o_ref[...] = (xv * _rms(xv) * g_ref[...]).astype(o_ref.dtype)

    return pl.pallas_call(
        body, name=name, grid=(s // t,), in_specs=[_row_spec(t, D), _full_spec((1, D))],
        out_specs=_row_spec(t, D), out_shape=_sds((s, D), MXU), compiler_params=_cp("parallel"),
    )(x, g)


def _conv_fwd(xbc_raw, proj5, dt_raw, cw_s, cb_s, cw_l, cb_l, dt_bias):
    s = xbc_raw.shape[0]
    t = min(256, s)

    def body(xs_ref, xl_ref, dtr_ref, cws_ref, cbs_ref, cwl_ref, cbl_ref, dtb_ref, xc_ref, dsl_ref, xr_ref, dt_ref,
             bs_ref, bl_ref):
        @pl.when(pl.program_id(0) == 0)
        def _():
            bs_ref[0:8, :] = jnp.zeros((8, XBC), F32)
            bl_ref[0:8, :] = jnp.zeros((8, D), F32)

        bs_ref[8:t + 8, :] = xs_ref[...]
        bl_ref[8:t + 8, :] = xl_ref[...]

        def conv(buf, w_ref, b_ref):
            acc = b_ref[...] + w_ref[3:4, :] * buf[8:t + 8, :]
            for k in (1, 2, 3):
                acc = acc + w_ref[3 - k:4 - k, :] * buf[8 - k:t + 8 - k, :]
            return acc

        pre = conv(bs_ref, cws_ref, cbs_ref)
        sg = _sig(pre)
        xc_ref[...] = pre * sg
        dsl_ref[...] = (sg * (1.0 + pre * (1.0 - sg))).astype(dsl_ref.dtype)
        xr_ref[...] = conv(bl_ref, cwl_ref, cbl_ref)
        dt_ref[...] = _softplus(dtr_ref[...] + dtb_ref[...])
        bs_ref[0:8, :] = bs_ref[t:t + 8, :]
        bl_ref[0:8, :] = bl_ref[t:t + 8, :]

    return pl.pallas_call(
        body, name="conv_fwd", grid=(s // t,),
        in_specs=[_row_spec(t, XBC), _row_spec(t, D, 2), _row_spec(t, 128), _full_spec((4, XBC)),
                  _full_spec((1, XBC)), _full_spec((4, D)), _full_spec((1, D)), _full_spec((1, 128))],
        out_specs=[_row_spec(t, XBC), _row_spec(t, XBC), _row_spec(t, D), _row_spec(t, 128)],
        out_shape=[_sds((s, XBC)), _sds((s, XBC), BF16), _sds((s, D)), _sds((s, 128))],
        scratch_shapes=[pltpu.VMEM((t + 8, XBC), F32), pltpu.VMEM((t + 8, D), F32)],
        compiler_params=_cp("arbitrary"),
    )(xbc_raw, proj5, dt_raw, cw_s, cb_s, cw_l, cb_l, dt_bias)


def _ssd_chunk_setup(dt_ref, alog_ref, e_ref, at_ref, dtt_ref):
    lane = lax.broadcasted_iota(jnp.int32, (CH, 128), 1)
    row = lax.broadcasted_iota(jnp.int32, (CH, 128), 0)
    lane1 = lax.broadcasted_iota(jnp.int32, (1, 128), 1)
    a = jnp.where(lane1 < NH, -jnp.exp(alog_ref[...]), 0.0)
    dtv = dt_ref[...]
    adt = dtv * a
    tril = row >= lane
    acum = _xdot(tril.astype(F32), adt, 3, split_b=True)
    alast = jnp.sum(adt, axis=0, keepdims=True)
    at_ref[...] = acum.T
    dtt_ref[...] = dtv.T
    e = e_ref[...]
    ea_x = _xdot(jnp.exp(acum), e, 2)
    ws = jnp.exp(alast - acum) * dtv
    ws_x = _xdot(ws, e, 2)
    eal = jnp.exp(alast)
    eal_x = jnp.max(_xdot(jnp.broadcast_to(eal, (8, 128)), e, 3), axis=0, keepdims=True)
    return dict(lane=lane, row=row, tril=tril, a=a, dtv=dtv, acum=acum, alast=alast, ea_x=ea_x, ws=ws, ws_x=ws_x,
                eal=eal, eal_x=eal_x)


def _head_decay(cs, at_ref, dtt_ref, h):
    col = jnp.sum(jnp.where(cs["lane"] == h, cs["acum"], 0.0), axis=1, keepdims=True)
    ld = jnp.where(cs["tril"], jnp.exp(jnp.minimum(col - at_ref[h:h + 1, :], 0.0)), 0.0)
    return ld, dtt_ref[h:h + 1, :]


def _ssd_fwd(xbc_c, dt, proj5, a_log, dskip_x, ssm_norm, expand):
    s = xbc_c.shape[0]
    nc = s // CH

    def body(xc_ref, dt_ref, z_ref, alog_ref, dsk_ref, ng_ref, e_ref, y_ref, ya_ref, st_ref, h_ref, at_ref, dtt_ref,
             yd_ref):
        @pl.when(pl.program_id(0) == 0)
        def _():
            h_ref[...] = jnp.zeros_like(h_ref)

        cs = _ssd_chunk_setup(dt_ref, alog_ref, e_ref, at_ref, dtt_ref)
        lane = cs["lane"]
        for g in range(NG):
            gs = slice(GW * g, GW * (g + 1))
            bg = xc_ref[:, D + NS * g:D + NS * (g + 1)]
            cg = xc_ref[:, D + NG * NS + NS * g:D + NG * NS + NS * (g + 1)]
            cb = _mdot(cg, bg, 1, 1)
            for j in range(4 * g, 4 * g + 4):
                ps = slice(128 * j, 128 * (j + 1))
                xp = xc_ref[:, ps]
                acc = jnp.zeros((CH, 128), F32)
                for hf in range(2):
                    ld, rowdt = _head_decay(cs, at_ref, dtt_ref, 2 * j + hf)
                    hm = (lane >= HP) if hf else (lane < HP)
                    acc = acc + _mdot(cb * ld * rowdt, jnp.where(hm, xp, 0.0))
                yd_ref[:, ps] = acc
            hg = h_ref[:, gs]
            yd_ref[:, gs] += _mdot(cg, hg) * cs["ea_x"][:, gs]
            st = _mdot(bg, xc_ref[:, gs] * cs["ws_x"][:, gs], 0, 0)
            st_ref[0, :, gs] = hg
            h_ref[:, gs] = cs["eal_x"][:, gs] * hg + st
        y = yd_ref[...] + dsk_ref[...] * xc_ref[:, 0:D]
        y_ref[...] = y
        yg = y * _silu(z_ref[...])
        for g in range(NG):
            gs = slice(GW * g, GW * (g + 1))
            seg = yg[:, gs]
            ya_ref[:, gs] = seg * _rms(seg) * ng_ref[:, gs]

    return pl.pallas_call(
        body, name="ssd_fwd", grid=(nc,),
        in_specs=[_row_spec(CH, XBC), _row_spec(CH, 128), _row_spec(CH, D, 0), _full_spec((1, 128)),
                  _full_spec((1, D)), _full_spec((1, D)), _full_spec((128, D))],
        out_specs=[_row_spec(CH, D), _row_spec(CH, D), pl.BlockSpec((1, NS, D), lambda i: (i, 0, 0))],
        out_shape=[_sds((s, D)), _sds((s, D)), _sds((nc, NS, D))],
        scratch_shapes=[pltpu.VMEM((NS, D), F32), pltpu.VMEM((128, 128), F32), pltpu.VMEM((128, 128), F32),
                        pltpu.VMEM((CH, D), F32)],
        compiler_params=_cp("arbitrary"),
    )(xbc_c, dt, proj5, a_log, dskip_x, ssm_norm, expand)


def _lru_gates(xr, wa_ref, wx_ref, ba_ref, bx_ref, lam_ref):
    gr = _sig(_mdot(xr, wa_ref[...]) + ba_ref[...])
    gi = _sig(_mdot(xr, wx_ref[...]) + bx_ref[...])
    sp = _softplus(-lam_ref[...])
    la = -LRU_C * gr * sp
    a = jnp.exp(la)
    mult = jnp.sqrt(_one_minus_sq(a, la))
    return gr, gi, sp, a, mult


def _blocked_scan(a, u, carry_ref, a_ref, u_ref, c_ref, out_ref, reverse):
    t = a.shape[0]
    ns = t // 8

    def combine(av, uv, idx, n, sh):
        m = (idx < n - sh) if reverse else (idx >= sh)
        by = n - sh if reverse else sh
        return jnp.where(m, av * pltpu.roll(av, by, 0), av), jnp.where(m, uv + av * pltpu.roll(uv, by, 0), uv)

    row = lax.broadcasted_iota(jnp.int32, (t, D), 0)
    rin = jnp.bitwise_and(row, 7)
    for sh in (1, 2, 4):
        m = (rin < 8 - sh) if reverse else (rin >= sh)
        by = t - sh if reverse else sh
        a, u = jnp.where(m, a * pltpu.roll(a, by, 0), a), jnp.where(m, u + a * pltpu.roll(u, by, 0), u)
    a_ref[...] = a
    u_ref[...] = u
    edge = 0 if reverse else 7
    for j in range(ns):
        c_ref[j:j + 1, :] = a_ref[8 * j + edge:8 * j + edge + 1, :]
    at = c_ref[...]
    for j in range(ns):
        c_ref[j:j + 1, :] = u_ref[8 * j + edge:8 * j + edge + 1, :]
    ut = c_ref[...]
    srow = lax.broadcasted_iota(jnp.int32, (ns, D), 0)
    sh = 1
    while sh < ns:
        at, ut = combine(at, ut, srow, ns, sh)
        sh *= 2
    cv = carry_ref[0:1, :]
    ends = ut + at * cv
    last = 0 if reverse else ns - 1
    first = ns - 1 if reverse else 0
    c_ref[...] = jnp.where(srow == first, cv, pltpu.roll(ends, first if reverse else 1, 0))
    carry_ref[0:1, :] = jnp.sum(jnp.where(srow == last, ends, 0.0), axis=0, keepdims=True)
    for j in range(ns):
        sl = slice(8 * j, 8 * j + 8)
        out_ref[sl, :] = u_ref[sl, :] + a_ref[sl, :] * c_ref[j:j + 1, :]


def _lru_fwd(xr, proj5, ya, wa_bd, wx_bd, ba, bx, lam, plan=None):
    s = xr.shape[0]
    t = min(256, s)

    def body(xr_ref, g_ref, ga_ref, gb_ref, ya_ref, wa_ref, wx_ref, ba_ref, bx_ref, lam_ref, h_ref, mg_ref, hc_ref,
             a_ref, u_ref, c_ref):
        @pl.when(pl.program_id(0) == 0)
        def _():
            hc_ref[...] = jnp.zeros_like(hc_ref)

        xrv = xr_ref[...]
        _, gi, _, a, mult = _lru_gates(xrv, wa_ref, wx_ref, ba_ref, bx_ref, lam_ref)
        _blocked_scan(a, mult * gi * xrv, hc_ref, a_ref, u_ref, c_ref, h_ref, reverse=False)
        yb = h_ref[...] * _gelu(g_ref[...])
        mg_ref[...] = (_sig(ga_ref[...]) * ya_ref[...] + _sig(gb_ref[...]) * yb).astype(mg_ref.dtype)

    return _pcall(
        body, (xr, proj5, proj5, proj5, ya, wa_bd, wx_bd, ba, bx, lam), name="lru_fwd", grid=(s // t,),
        in_specs=[_row_spec(t, D), _row_spec(t, D, 1), _row_spec(t, D, 3), _row_spec(t, D, 4), _row_spec(t, D),
                  _full_spec((D, D)), _full_spec((D, D)), _full_spec((1, D)), _full_spec((1, D)), _full_spec((1, D))],
        out_specs=[_row_spec(t, D), _row_spec(t, D)],
        out_shape=[_sds((s, D)), _sds((s, D), MXU)],
        scratch_shapes=[pltpu.VMEM((8, D), F32), pltpu.VMEM((t, D), F32), pltpu.VMEM((t, D), F32),
                        pltpu.VMEM((t // 8, D), F32)],
        sem=("arbitrary",), plan=plan)


def _out_proj(merged, w_out, x, g2, g3):
    s = x.shape[0]
    t = min(256, s)

    def body(mg_ref, w_ref, x_ref, g2_ref, g3_ref, mix_ref, h1_ref, v_ref):
        mix = _mdot(mg_ref[...], w_ref[...])
        mix_ref[...] = mix
        h1 = x_ref[...] + mix * _rms(mix) * g2_ref[...]
        h1_ref[...] = h1
        v_ref[...] = (h1 * _rms(h1) * g3_ref[...]).astype(v_ref.dtype)

    return pl.pallas_call(
        body, name="out_proj", grid=(s // t,),
        in_specs=[_row_spec(t, D), _full_spec((D, D)), _row_spec(t, D), _full_spec((1, D)), _full_spec((1, D))],
        out_specs=[_row_spec(t, D), _row_spec(t, D), _row_spec(t, D)],
        out_shape=[_sds((s, D)), _sds((s, D)), _sds((s, D), MXU)],
        compiler_params=_cp("parallel"),
    )(merged, w_out, x, g2, g3)


def _down_loss(pre, w_down, h1, target, g4):
    s = pre.shape[0]
    t = min(256, s)

    def body(pre_ref, w_ref, h1_ref, tg_ref, g4_ref, dout_ref, dff_ref, loss_ref, dg4_ref):
        @pl.when(pl.program_id(0) == 0)
        def _():
            loss_ref[...] = jnp.zeros_like(loss_ref)
            dg4_ref[...] = jnp.zeros_like(dg4_ref)

        ff = _mdot(_relu2(pre_ref[...]), w_ref[...])
        r4 = _rms(ff)
        g4v = g4_ref[...]
        diff = h1_ref[...] + ff * r4 * g4v - tg_ref[...]
        sq = jnp.sum(jnp.sum(diff * diff, axis=1, keepdims=True), axis=0, keepdims=True)
        loss_ref[...] += (0.5 / D) * sq
        dout = diff * (1.0 / D)
        dout_ref[...] = dout
        dff, dg = _rms_bwd(ff, r4, g4v, dout)
        dff_ref[...] = dff.astype(dff_ref.dtype)
        dg4_ref[...] += dg

    return pl.pallas_call(
        body, name="down_loss", grid=(s // t,),
        in_specs=[_row_spec(t, DFF), _full_spec((DFF, D)), _row_spec(t, D), _row_spec(t, D), _full_spec((1, D))],
        out_specs=[_row_spec(t, D), _row_spec(t, D), _full_spec((1, 128)), _full_spec((1, D))],
        out_shape=[_sds((s, D)), _sds((s, D), MXU), _sds((1, 128)), _sds((1, D))],
        compiler_params=_cp("arbitrary"),
    )(pre, w_down, h1, target, g4)


def _dv_norms(dpre, w_up, h1, mix, dout, g3, g2):
    s = h1.shape[0]
    t = min(256, s)

    def body(dp_ref, w_ref, h1_ref, mix_ref, dout_ref, g3_ref, g2_ref, dh1_ref, dmix_ref, dg3_ref, dg2_ref):
        @pl.when(pl.program_id(0) == 0)
        def _():
            dg3_ref[...] = jnp.zeros_like(dg3_ref)
            dg2_ref[...] = jnp.zeros_like(dg2_ref)

        dv = _mdot(dp_ref[...], w_ref[...], 1, 1)
        h1 = h1_ref[...]
        dh1n, dg3 = _rms_bwd(h1, _rms(h1), g3_ref[...], dv)
        dh1 = dout_ref[...] + dh1n
        dh1_ref[...] = dh1
        mix = mix_ref[...]
        dmix, dg2 = _rms_bwd(mix, _rms(mix), g2_ref[...], dh1)
        dmix_ref[...] = dmix.astype(dmix_ref.dtype)
        dg3_ref[...] += dg3
        dg2_ref[...] += dg2

    return pl.pallas_call(
        body, name="dv_norms", grid=(s // t,),
        in_specs=[_row_spec(t, DFF), _full_spec((D, DFF)), _row_spec(t, D), _row_spec(t, D), _row_spec(t, D),
                  _full_spec((1, D)), _full_spec((1, D))],
        out_specs=[_row_spec(t, D), _row_spec(t, D), _full_spec((1, D)), _full_spec((1, D))],
        out_shape=[_sds((s, D)), _sds((s, D), MXU), _sds((1, D)), _sds((1, D))],
        compiler_params=_cp("arbitrary"),
    )(dpre, w_up, h1, mix, dout, g3, g2)


def _lru_bwd(dmerged, ya, xr, h, proj5, wa_bd, wx_bd, ba, bx, lam, plan=None):
    s = xr.shape[0]
    t = min(128, s)
    n = s // t
    rs = functools.partial(_rev_spec, t, D, n)

    def body(dm_ref, ya_ref, xr_ref, h_ref, hp_ref, g_ref, ga_ref, gb_ref, wa_ref, wx_ref, ba_ref, bx_ref, lam_ref,
             dya_ref, dga_ref, dgb_ref, dg_ref, dxr_ref, dpr_ref, dpi_ref, dlam_ref, dba_ref, dbx_ref, gc_ref,
             af_ref, an_ref, us_ref, c_ref, gs_ref):
        i = pl.program_id(0)

        @pl.when(i == 0)
        def _():
            gc_ref[...] = jnp.zeros_like(gc_ref)
            af_ref[...] = jnp.zeros_like(af_ref)
            dlam_ref[...] = jnp.zeros_like(dlam_ref)
            dba_ref[...] = jnp.zeros_like(dba_ref)
            dbx_ref[...] = jnp.zeros_like(dbx_ref)

        xrv = xr_ref[...]
        gr, gi, sp, a, mult = _lru_gates(xrv, wa_ref, wx_ref, ba_ref, bx_ref, lam_ref)
        hv = h_ref[...]
        gv = g_ref[...]
        dm = dm_ref[...]
        yav = ya_ref[...]
        sa = _sig(ga_ref[...])
        sb = _sig(gb_ref[...])
        gel = _gelu(gv)
        dya_ref[...] = dm * sa
        dga_ref[...] = (dm * yav * sa * (1.0 - sa)).astype(dga_ref.dtype)
        dyb = dm * sb
        dgb_ref[...] = (dyb * hv * gel * (1.0 - sb)).astype(dgb_ref.dtype)
        dg_ref[...] = (dyb * hv * _dgelu(gv)).astype(dg_ref.dtype)
        row = lax.broadcasted_iota(jnp.int32, (t, D), 0)
        an = jnp.where(row == t - 1, af_ref[0:1, :], pltpu.roll(a, t - 1, 0))
        _blocked_scan(an, dyb * gel, gc_ref, an_ref, us_ref, c_ref, gs_ref, reverse=True)
        gfull = gs_ref[...]
        af_ref[0:1, :] = jnp.sum(jnp.where(row == 0, a, 0.0), axis=0, keepdims=True)
        hlast = jnp.where(i == n - 1, 0.0, hp_ref[7:8, :])
        hprev = jnp.where(row == 0, hlast, pltpu.roll(hv, 1, 0))
        da = gfull * hprev
        dmult = gfull * gi * xrv
        dgi = gfull * mult * xrv
        dla = da * a - dmult * a * a / mult
        dgr = dla * (-LRU_C * sp)
        dsp = jnp.sum(dla * (-LRU_C * gr), axis=0, keepdims=True)
        dlam_ref[...] += dsp * (-_sig(-lam_ref[...]))
        dpr = dgr * gr * (1.0 - gr)
        dpi = dgi * gi * (1.0 - gi)
        dpr_ref[...] = dpr.astype(dpr_ref.dtype)
        dpi_ref[...] = dpi.astype(dpi_ref.dtype)
        dba_ref[...] += jnp.sum(dpr, axis=0, keepdims=True)
        dbx_ref[...] += jnp.sum(dpi, axis=0, keepdims=True)
        dxr_ref[...] = gfull * mult * gi + _mdot(dpr, wa_ref[...], 1, 1) + _mdot(dpi, wx_ref[...], 1, 1)

    hp_spec = pl.BlockSpec((8, D), lambda i: (jnp.maximum((n - 1 - i) * (t // 8) - 1, 0), 0))
    return _pcall(
        body, (dmerged, ya, xr, h, h, proj5, proj5, proj5, wa_bd, wx_bd, ba, bx, lam), name="lru_bwd", grid=(n,),
        in_specs=[rs(), rs(), rs(), rs(), hp_spec, rs(1), rs(3), rs(4), _full_spec((D, D)), _full_spec((D, D)),
                  _full_spec((1, D)), _full_spec((1, D)), _full_spec((1, D))],
        out_specs=[rs(), rs(), rs(), rs(), rs(), rs(), rs(), _full_spec((1, D)), _full_spec((1, D)),
                   _full_spec((1, D))],
        out_shape=[_sds((s, D)), _sds((s, D), MXU), _sds((s, D), MXU), _sds((s, D), MXU), _sds((s, D)),
                   _sds((s, D), MXU), _sds((s, D), MXU), _sds((1, D)), _sds((1, D)), _sds((1, D))],
        scratch_shapes=[pltpu.VMEM((8, D), F32), pltpu.VMEM((8, D), F32), pltpu.VMEM((t, D), F32),
                        pltpu.VMEM((t, D), F32), pltpu.VMEM((t // 8, D), F32), pltpu.VMEM((t, D), F32)],
        sem=("arbitrary",), plan=plan)


def _ssd_bwd(dya, y, proj5, xbc_c, dt, states, a_log, dskip_x, ssm_norm, expand, reduce_, plan=None):
    s = xbc_c.shape[0]
    nc = s // CH
    rv = functools.partial(_rev_spec, CH)

    def body(dya_ref, y_ref, z_ref, xc_ref, dt_ref, st_ref, alog_ref, dsk_ref, ng_ref, e_ref, et_ref, dz_ref,
             dxc_ref, ddt_ref, dng_ref, ddsk_ref, dalog_ref, dh_ref, at_ref, dtt_ref, dat_ref, ddtt_ref, dy_ref,
             yoffdy_ref, xbds_ref):
        @pl.when(pl.program_id(0) == 0)
        def _():
            dh_ref[...] = jnp.zeros_like(dh_ref)
            dng_ref[...] = jnp.zeros_like(dng_ref)
            ddsk_ref[...] = jnp.zeros_like(ddsk_ref)
            dalog_ref[...] = jnp.zeros_like(dalog_ref)

        cs = _ssd_chunk_setup(dt_ref, alog_ref, e_ref, at_ref, dtt_ref)
        lane, row = cs["lane"], cs["row"]
        et = et_ref[...]
        for g in range(NG):
            gs = slice(GW * g, GW * (g + 1))
            yv = y_ref[:, gs]
            zv = z_ref[:, gs]
            sz = _silu(zv)
            yg = yv * sz
            dyav = dya_ref[:, gs]
            dyg, dng = _rms_bwd(yg, _rms(yg), ng_ref[:, gs], dyav)
            dng_ref[:, gs] += dng
            dy_ref[:, gs] = dyg * sz
            dz_ref[:, gs] = (dyg * yv * _dsilu(zv)).astype(dz_ref.dtype)
        dyv = dy_ref[...]
        xs = xc_ref[:, 0:D]
        ddsk_ref[...] += jnp.sum(dyv * xs, axis=0, keepdims=True)
        dxc_ref[:, 0:D] = dyv * dsk_ref[...]
        dat_ref[...] = jnp.zeros_like(dat_ref)
        ddtt_ref[...] = jnp.zeros_like(ddtt_ref)
        hh = jnp.sum(dh_ref[...] * st_ref[0], axis=0, keepdims=True)
        deal = jnp.max(_xdot(jnp.broadcast_to(hh, (8, D)), et, 3), axis=0, keepdims=True)
        d_acum = jnp.zeros((CH, 128), F32)
        for g in range(NG):
            gs = slice(GW * g, GW * (g + 1))
            bs_ = slice(D + NS * g, D + NS * (g + 1))
            cs_ = slice(D + NG * NS + NS * g, D + NG * NS + NS * (g + 1))
            bg = xc_ref[:, bs_]
            cg = xc_ref[:, cs_]
            cb = _mdot(cg, bg, 1, 1)
            hg = st_ref[0, :, gs]
            dhg = dh_ref[:, gs]
            dyg_ = dy_ref[:, gs]
            xsg = xc_ref[:, gs]
            ea = cs["ea_x"][:, gs]
            wsx = cs["ws_x"][:, gs]
            dp = dyg_ * ea
            yoffdy_ref[:, gs] = dp * _mdot(cg, hg)
            dc = _mdot(dp, hg, 1, 1)
            dhprev = _mdot(cg, dp, 0, 0)
            bds = _mdot(bg, dhg)
            dxc_ref[:, gs] += wsx * bds
            xbds_ref[:, gs] = xsg * bds
            db = _mdot(xsg * wsx, dhg, 1, 1)
            dh_ref[:, gs] = dhprev + cs["eal_x"][:, gs] * dhg
            dcbs = jnp.zeros((CH, CH), F32)
            for j in range(4 * g, 4 * g + 4):
                ps = slice(128 * j, 128 * (j + 1))
                xp = xc_ref[:, ps]
                dyp = dy_ref[:, ps]
                dxacc = jnp.zeros((CH, 128), F32)
                for hf in range(2):
                    hd = 2 * j + hf
                    ld, rowdt = _head_decay(cs, at_ref, dtt_ref, hd)
                    hm = (lane >= HP) if hf else (lane < HP)
                    dym = jnp.where(hm, dyp, 0.0)
                    w = cb * ld * rowdt
                    dw = _mdot(dym, jnp.where(hm, xp, 0.0), 1, 1)
                    dxacc = dxacc + _mdot(w, dym, 0, 0)
                    nm = dw * w
                    ddtt_ref[hd:hd + 1, :] += jnp.sum(dw * cb * ld, axis=0, keepdims=True)
                    d_acum = d_acum + jnp.where(lane == hd, jnp.sum(nm, axis=1, keepdims=True), 0.0)
                    dat_ref[hd:hd + 1, :] -= jnp.sum(nm, axis=0, keepdims=True)
                    dcbs = dcbs + dw * ld * rowdt
                dxc_ref[:, ps] += dxacc
            dxc_ref[:, bs_] = db + _mdot(dcbs, cg, 0, 0)
            dxc_ref[:, cs_] = dc + _mdot(dcbs, bg)
        dws = _xdot(xbds_ref[...], et, 2)
        ws = cs["ws"]
        d_acum = d_acum - dws * ws + _xdot(yoffdy_ref[...], et, 2) + dat_ref[...].T
        d_alast = jnp.sum(dws * ws, axis=0, keepdims=True) + deal * cs["eal"]
        d_acum = d_acum + jnp.where(row == CH - 1, d_alast, 0.0)
        triu = row <= lane
        d_adt = _xdot(triu.astype(F32), d_acum, 3, split_b=True)
        ddt_ref[...] = dws * jnp.exp(cs["alast"] - cs["acum"]) + ddtt_ref[...].T + d_adt * cs["a"]
        dalog_ref[...] += jnp.sum(d_adt * cs["dtv"], axis=0, keepdims=True) * cs["a"]

    return _pcall(
        body, (dya, y, proj5, xbc_c, dt, states, a_log, dskip_x, ssm_norm, expand, reduce_), name="ssd_bwd",
        grid=(nc,),
        in_specs=[rv(D, nc), rv(D, nc), rv(D, nc, 0), rv(XBC, nc), rv(128, nc),
                  pl.BlockSpec((1, NS, D), lambda i: (nc - 1 - i, 0, 0)), _full_spec((1, 128)), _full_spec((1, D)),
                  _full_spec((1, D)), _full_spec((128, D)), _full_spec((D, 128))],
        out_specs=[rv(D, nc), rv(XBC, nc), rv(128, nc), _full_spec((1, D)), _full_spec((1, D)),
                   _full_spec((1, 128))],
        out_shape=[_sds((s, D), MXU), _sds((s, XBC)), _sds((s, 128)), _sds((1, D)), _sds((1, D)), _sds((1, 128))],
        scratch_shapes=[pltpu.VMEM((NS, D), F32), pltpu.VMEM((128, 128), F32), pltpu.VMEM((128, 128), F32),
                        pltpu.VMEM((128, 128), F32), pltpu.VMEM((128, 128), F32), pltpu.VMEM((CH, D), F32),
                        pltpu.VMEM((CH, D), F32), pltpu.VMEM((CH, D), F32)],
        sem=("arbitrary",), plan=plan)


def _conv_bwd(dxbc_c, dsilu, dxr, ddt, xbc_raw, proj5, dt_raw, cw_s, cw_l, dt_bias, plan=None):
    s = xbc_raw.shape[0]
    t = min(256, s)
    n = s // t

    def body(dxc_ref, dsl_ref, dxr_ref, ddt_ref, xs_ref, xl_ref, dtr_ref, cws_ref, cwl_ref, dtb_ref, dxs_ref,
             dxl_ref, ddtr_ref, dcws_ref, dcbs_ref, dcwl_ref, dcbl_ref, ddtb_ref, ds_ref, dl_ref):
        @pl.when(pl.program_id(0) == 0)
        def _():
            ds_ref[t:t + 8, :] = jnp.zeros((8, XBC), F32)
            dl_ref[t:t + 8, :] = jnp.zeros((8, D), F32)
            for r in (dcws_ref, dcbs_ref, dcwl_ref, dcbl_ref, ddtb_ref):
                r[...] = jnp.zeros_like(r)

        ds_ref[0:t, :] = dxc_ref[...] * dsl_ref[...].astype(F32)
        dl_ref[0:t, :] = dxr_ref[...]

        def back(dbuf, x_ref, w_ref, dx_ref, dw_ref, db_ref):
            xv = x_ref[...]
            dpre = dbuf[0:t, :]
            dx = w_ref[3:4, :] * dpre
            dw_ref[3:4, :] += jnp.sum(dpre * xv, axis=0, keepdims=True)
            db_ref[...] += jnp.sum(dpre, axis=0, keepdims=True)
            for k in (1, 2, 3):
                ahead = dbuf[k:t + k, :]
                dx = dx + w_ref[3 - k:4 - k, :] * ahead
                dw_ref[3 - k:4 - k, :] += jnp.sum(ahead * xv, axis=0, keepdims=True)
            dx_ref[...] = dx.astype(dx_ref.dtype)
            dbuf[t:t + 8, :] = dbuf[0:8, :]

        back(ds_ref, xs_ref, cws_ref, dxs_ref, dcws_ref, dcbs_ref)
        back(dl_ref, xl_ref, cwl_ref, dxl_ref, dcwl_ref, dcbl_ref)
        ddtr = ddt_ref[...] * _sig(dtr_ref[...] + dtb_ref[...])
        ddtr_ref[...] = ddtr.astype(ddtr_ref.dtype)
        ddtb_ref[...] += jnp.sum(ddtr, axis=0, keepdims=True)

    rv = functools.partial(_rev_spec, t)
    return _pcall(
        body, (dxbc_c, dsilu, dxr, ddt, xbc_raw, proj5, dt_raw, cw_s, cw_l, dt_bias), name="conv_bwd", grid=(n,),
        in_specs=[rv(XBC, n), rv(XBC, n), rv(D, n), rv(128, n), rv(XBC, n), rv(D, n, 2), rv(128, n),
                  _full_spec((4, XBC)), _full_spec((4, D)), _full_spec((1, 128))],
        out_specs=[rv(XBC, n), rv(D, n), rv(128, n), _full_spec((4, XBC)), _full_spec((1, XBC)), _full_spec((4, D)),
                   _full_spec((1, D)), _full_spec((1, 128))],
        out_shape=[_sds((s, XBC), MXU), _sds((s, D), MXU), _sds((s, 128), MXU), _sds((4, XBC)), _sds((1, XBC)),
                   _sds((4, D)), _sds((1, D)), _sds((1, 128))],
        scratch_shapes=[pltpu.VMEM((t + 8, XBC), F32), pltpu.VMEM((t + 8, D), F32)],
        sem=("arbitrary",), plan=plan)


def _du_norm(pieces5, dxbc, ddtr, w5, wxbc, wdt, x, dh1, g1):
    s = x.shape[0]
    t = min(256, s)

    def body(p0, p1, p2, p3, p4, dxbc_ref, ddtr_ref, w5_ref, wx_ref, wd_ref, x_ref, dh1_ref, g1_ref, dx_ref, dg1_ref):
        @pl.when(pl.program_id(0) == 0)
        def _():
            dg1_ref[...] = jnp.zeros_like(dg1_ref)

        du = _mdot(dxbc_ref[...], wx_ref[...], 1, 1) + _mdot(ddtr_ref[...], wd_ref[...], 1, 1)
        for b, p in enumerate((p0, p1, p2, p3, p4)):
            du = du + _mdot(p[...], w5_ref[:, D * b:D * (b + 1)], 1, 1)
        xv = x_ref[...]
        dxn, dg1 = _rms_bwd(xv, _rms(xv), g1_ref[...], du)
        dx_ref[...] = dh1_ref[...] + dxn
        dg1_ref[...] += dg1

    return pl.pallas_call(
        body, name="du_norm", grid=(s // t,),
        in_specs=[_row_spec(t, D)] * 5 + [_row_spec(t, XBC), _row_spec(t, 128), _full_spec((D, 5 * D)),
                                          _full_spec((D, XBC)), _full_spec((D, 128)), _row_spec(t, D),
                                          _row_spec(t, D), _full_spec((1, D))],
        out_specs=[_row_spec(t, D), _full_spec((1, D))],
        out_shape=[_sds((s, D)), _sds((1, D))],
        compiler_params=_cp("arbitrary"),
    )(*pieces5, dxbc, ddtr, w5, wxbc, wdt, x, dh1, g1)


def _adamw(w, g, m, v, name):
    r, c = w.shape
    t = r
    if r * c > 256 * 1024:
        t = next(cand for cand in (512, 256, 128, 64, 32, 16, 8) if r % cand == 0 and cand * c <= 512 * 1024)
    bc1 = 1.0 - ADAM_B1 ** ADAM_STEP
    bc2 = 1.0 - ADAM_B2 ** ADAM_STEP

    def body(w_ref, g_ref, m_ref, v_ref, d_ref, nm_ref, nv_ref):
        gv = g_ref[...]
        nm = ADAM_B1 * m_ref[...] + (1.0 - ADAM_B1) * gv
        nv = ADAM_B2 * v_ref[...] + (1.0 - ADAM_B2) * (gv * gv)
        nm_ref[...] = nm
        nv_ref[...] = nv
        d_ref[...] = -ADAM_LR * ((nm / bc1) / (jnp.sqrt(nv / bc2) + ADAM_EPS) + ADAM_WD * w_ref[...])

    spec = pl.BlockSpec((t, c), lambda i: (i, 0))
    return pl.pallas_call(
        body, name=name, grid=(r // t,), in_specs=[spec] * 4, out_specs=[spec] * 3,
        out_shape=[_sds((r, c))] * 3, compiler_params=_cp("parallel"),
    )(w, g, m, v)


def _adamw_halves(w, g_mine, g_other, m, v, cidx, name):
    r, c = w.shape
    hr = r // 2
    t = 256 if hr % 256 == 0 else 128
    nb = hr // t
    bc1 = 1.0 - ADAM_B1 ** ADAM_STEP
    bc2 = 1.0 - ADAM_B2 ** ADAM_STEP

    def body(c_ref, w_ref, gm_ref, go_ref, m_ref, v_ref, g_ref, d_ref, nm_ref, nv_ref):
        mine = (pl.program_id(0) // nb) == c_ref[0]
        gv = jnp.where(mine, gm_ref[...], go_ref[...])
        g_ref[...] = gv
        nm = ADAM_B1 * m_ref[...] + (1.0 - ADAM_B1) * gv
        nv = ADAM_B2 * v_ref[...] + (1.0 - ADAM_B2) * (gv * gv)
        nm_ref[...] = nm
        nv_ref[...] = nv
        d_ref[...] = -ADAM_LR * ((nm / bc1) / (jnp.sqrt(nv / bc2) + ADAM_EPS) + ADAM_WD * w_ref[...])

    spec = pl.BlockSpec((t, c), lambda i, c_ref: (i, 0))
    half = pl.BlockSpec((t, c), lambda i, c_ref: (i % nb, 0))
    return pl.pallas_call(
        body, name=name,
        grid_spec=pltpu.PrefetchScalarGridSpec(num_scalar_prefetch=1, grid=(2 * nb,),
                                               in_specs=[spec, half, half, spec, spec], out_specs=[spec] * 4),
        out_shape=[_sds((r, c))] * 4, compiler_params=_cp("parallel"),
    )(cidx, w, g_mine, g_other, m, v)


def _block_diag(w):
    eye = jnp.eye(NH, dtype=w.dtype)
    return (w[:, :, None, :] * eye[:, None, :, None]).reshape(D, D)


def _diag_blocks(full):
    return jnp.stack([full[HP * h:HP * (h + 1), HP * h:HP * (h + 1)] for h in range(NH)])


def _pad_lanes(v, n=128):
    return jnp.pad(v, ((0, 0), (0, n - v.shape[1])))


def _local_step(x, target, p, dist=None):
    heads = jnp.arange(D, dtype=jnp.int32) // HP
    expand = (jnp.arange(128, dtype=jnp.int32)[:, None] == heads[None, :]).astype(F32)
    reduce_ = expand.T
    dskip_x = jnp.repeat(p["d_skip"], HP, axis=1)
    a_log = _pad_lanes(p["a_log"])
    dt_bias = _pad_lanes(p["dt_bias"])
    w5, wxbc, wdt = p["w5"], p["wxbc"], p["wdt"]
    wa_bd = _block_diag(p["lru_wa"]).astype(MXU)
    wx_bd = _block_diag(p["lru_wx"]).astype(MXU)
    ba = p["lru_ba"].reshape(1, D)
    bx = p["lru_bx"].reshape(1, D)

    def hosted(key, fn):
        plan = dist.plan(key) if dist is not None else None
        if plan is None:
            return fn(plan=None)
        outs, got = fn(plan=plan)
        dist.done(key, got, p)
        return outs

    u = _norm_cast(x, p["norm_mix_pre"], "norm_u")
    proj5 = hosted("proj5", functools.partial(_matmul, u, w5, name="proj5"))
    xbc_raw = _matmul(u, wxbc, name="proj_xbc", tn=XBC)
    dt_raw = _matmul(u, wdt, name="proj_dt")
    xbc_c, dsilu, xr, dt = _conv_fwd(xbc_raw, proj5, dt_raw, p["conv_ssm_w"], p["conv_ssm_b"], p["conv_lru_w"],
                                     p["conv_lru_b"], dt_bias)
    y, ya, states = _ssd_fwd(xbc_c, dt, proj5, a_log, dskip_x, p["ssm_norm"], expand)
    h, merged = hosted("lru_fwd", functools.partial(_lru_fwd, xr, proj5, ya, wa_bd, wx_bd, ba, bx, p["lru_lambda"]))
    mix, h1, v = _out_proj(merged, p["w_out"], x, p["norm_mix_post"], p["norm_mlp_pre"])
    pre = _matmul(v, p["w_up"], name="up_proj")
    dout, dff, loss, dg4 = _down_loss(pre, p["w_down"], h1, target, p["norm_mlp_post"])

    dpre = _matmul(dff, p["w_down"], name="d_pre", tb=True, out_dtype=MXU,
                   epi=lambda r, pr: r * (2.0 * jnp.maximum(pr, 0.0)), epi_args=(pre,))
    g_w_down = _matmul(pre, dff, name="dw_down", ta=True, tm=1024, tn=1024, tk=512, a_fn=_relu2)
    dh1, dmix, dg3, dg2 = _dv_norms(dpre, p["w_up"], h1, mix, dout, p["norm_mlp_pre"], p["norm_mix_post"])
    g_w_up = _matmul(v, dpre, name="dw_up", ta=True, tm=1024, tn=1024, tk=512)
    dmerged = _matmul(dmix, p["w_out"], name="d_merged", tb=True)
    g_w_out = _matmul(merged, dmix, name="dw_out", ta=True, tm=1024, tn=1024, tk=512)
    if dist is not None:
        dist.early_grads(w_down=g_w_down, w_up=g_w_up, w_out=g_w_out)
    (dya, dga, dgb, dg, dxr, dpr, dpi, dlam, dba, dbx) = hosted("lru_bwd", functools.partial(
        _lru_bwd, dmerged, ya, xr, h, proj5, wa_bd, wx_bd, ba, bx, p["lru_lambda"]))
    g_wa = _diag_blocks(_matmul(xr, dpr, name="dw_lru_a", ta=True, tm=1024, tn=1024, tk=512))
    g_wx = _diag_blocks(_matmul(xr, dpi, name="dw_lru_x", ta=True, tm=1024, tn=1024, tk=512))
    dz, dxbc_c, ddt, dng, ddsk, dalog = hosted("ssd_bwd", functools.partial(
        _ssd_bwd, dya, y, proj5, xbc_c, dt, states, a_log, dskip_x, p["ssm_norm"], expand, reduce_))
    (dxbc, dxl, ddtr, dcws, dcbs, dcwl, dcbl, ddtb) = hosted("conv_bwd", functools.partial(
        _conv_bwd, dxbc_c, dsilu, dxr, ddt, xbc_raw, proj5, dt_raw, p["conv_ssm_w"], p["conv_lru_w"], dt_bias))
    pieces5 = (dz, dg, dxl, dga, dgb)
    grad_x, dg1 = _du_norm(pieces5, dxbc, ddtr, w5, wxbc, wdt, x, dh1, p["norm_mix_pre"])
    gw5 = [_matmul(u, pc, name=f"dw_in_{i}", ta=True, tm=1024, tn=1024, tk=512) for i, pc in enumerate(pieces5)]
    gwxbc = _matmul(u, dxbc, name="dw_in_xbc", ta=True, tm=1024, tn=XBC, tk=512)
    gwdt = _matmul(u, ddtr, name="dw_in_dt", ta=True, tm=1024, tn=128, tk=512)
    g_w_in = jnp.concatenate([gw5[0], gwxbc, gwdt[:, :NH], gw5[1], gw5[2], gw5[3], gw5[4]], axis=1)
    grads = {
        "norm_mix_pre": dg1, "w_in": g_w_in, "conv_ssm_w": dcws, "conv_ssm_b": dcbs, "dt_bias": ddtb[:, :NH],
        "a_log": dalog[:, :NH], "d_skip": ddsk.reshape(NH, HP).sum(axis=1)[None, :], "ssm_norm": dng,
        "conv_lru_w": dcwl, "conv_lru_b": dcbl, "lru_wa": g_wa, "lru_ba": dba.reshape(NH, HP), "lru_wx": g_wx,
        "lru_bx": dbx.reshape(NH, HP), "lru_lambda": dlam, "w_out": g_w_out, "norm_mix_post": dg2,
        "norm_mlp_pre": dg3, "w_up": g_w_up, "w_down": g_w_down, "norm_mlp_post": dg4,
    }
    return loss[0, 0], grad_x, grads


def _split_w_in(w_in_full):
    z, xbc, dtc, g, xl, ga, gb = jnp.split(w_in_full, [D, D + XBC, D + XBC + NH, 2 * D + XBC + NH,
                                                        3 * D + XBC + NH, 4 * D + XBC + NH], axis=1)
    return jnp.concatenate([z, g, xl, ga, gb], axis=1), xbc, _pad_lanes(dtc)


COMM = BF16


def _place():
    x, y, c = lax.axis_index("x"), lax.axis_index("y"), lax.axis_index("c")
    chips = [(1 - x, y), (x, 1 - y), (1 - x, 1 - y)]
    return x, y, c, chips


def _remote(src, dst, send_sem, recv_sem, to):
    return pltpu.make_async_remote_copy(src_ref=src, dst_ref=dst, send_sem=send_sem, recv_sem=recv_sem, device_id=to,
                                        device_id_type=MESH)


def _gather_plan(big, small=()):
    nb = len(big)
    arrs = list(big) + list(small)
    na = len(arrs)

    def direct(ins, outs, send, recv):
        x, y, c, chips = _place()
        k = 2 * x + y
        cps = []
        for a in range(na):
            if a < nb:
                hr = ins[a].shape[0] // 2
                src, dst = ins[a].at[pl.ds(c * hr, hr)], outs[a].at[k, pl.ds(c * hr, hr)]
            else:
                src, dst = ins[a], outs[a].at[k]
            cps += [_remote(src, dst, send.at[a, j], recv.at[a, j], (cx, cy, c)) for j, (cx, cy) in enumerate(chips)]
        return cps

    def start(ins, outs, sems):
        for cp in direct(ins, outs, *sems[0]):
            cp.start()

    def finish(ins, outs, sems):
        send, recv = sems[0]
        x, y, c, chips = _place()
        passed = []
        for j, (cx, cy) in enumerate(chips):
            kj = 2 * cx + cy
            for a in range(na):
                if a < nb:
                    hr = ins[a].shape[0] // 2
                    got = outs[a].at[kj, pl.ds(c * hr, hr)]
                    _remote(got, got, send.at[a, j], recv.at[a, j], (cx, cy, c)).wait_recv()
                    passed.append(_remote(got, got, send.at[a, 3 + j], recv.at[a, 3 + j], (x, y, 1 - c)))
                    passed[-1].start()
                else:
                    got = outs[a].at[kj]
                    _remote(got, got, send.at[a, j], recv.at[a, j], (cx, cy, c)).wait_recv()
        for j, (cx, cy) in enumerate(chips):
            for a in range(nb):
                hr = ins[a].shape[0] // 2
                got = outs[a].at[2 * cx + cy, pl.ds((1 - c) * hr, hr)]
                _remote(got, got, send.at[a, 3 + j], recv.at[a, 3 + j], (x, y, 1 - c)).wait_recv()
        for cp in direct(ins, outs, send, recv) + passed:
            cp.wait_send()

    return _Plan(arrs, [_sds((NCHIP,) + a.shape, a.dtype) for a in arrs], [(na, 6)], start, finish)


def _own_shards(gathered, shards):
    kchip = 2 * lax.axis_index("x") + lax.axis_index("y")
    return [lax.dynamic_update_index_in_dim(o, a, kchip, 0) for o, a in zip(gathered, shards)]


def _swap_plan(ins, outs, sems, copies):
    def start(i, o, s):
        for cp in copies(i, o, *s[0]):
            cp.start()

    def finish(i, o, s):
        for cp in copies(i, o, *s[0]):
            cp.wait()

    return _Plan(ins, outs, [sems], start, finish)


def _pair_exchange_plan(gs):
    def copies(ins, outs, send, recv):
        x, y, c, _ = _place()
        cps = []
        for a in range(len(gs)):
            hr = ins[a].shape[1] // 2
            cps.append(_remote(ins[a].at[:, pl.ds((1 - c) * hr, hr)], outs[a], send.at[a], recv.at[a], (x, y, 1 - c)))
        return cps

    return _swap_plan(gs, [_sds((NCHIP, g.shape[1] // 2, g.shape[2]), g.dtype) for g in gs], (len(gs),), copies)


def _pair_add(g, got, cidx, name):
    _, r, cdim = g.shape
    hr = r // 2
    t = 256 if hr % 256 == 0 else 128
    nt = hr // t

    def body(c_ref, g_ref, o_ref, p_ref, pc_ref):
        sm = g_ref[...] + o_ref[...]
        p_ref[...] = sm
        pc_ref[...] = sm.astype(pc_ref.dtype)

    spec = pl.BlockSpec((1, t, cdim), lambda k, i, c_ref: (k, i, 0))
    return pl.pallas_call(
        body, name=name,
        grid_spec=pltpu.PrefetchScalarGridSpec(
            num_scalar_prefetch=1, grid=(NCHIP, nt),
            in_specs=[pl.BlockSpec((1, t, cdim), lambda k, i, c_ref: (k, c_ref[0] * nt + i, 0)), spec],
            out_specs=[spec, spec]),
        out_shape=[_sds((NCHIP, hr, cdim)), _sds((NCHIP, hr, cdim), COMM)],
        compiler_params=_cp("parallel", "parallel"),
    )(cidx, g, got)


def _chip_exchange_plan(ps):
    def copies(ins, outs, send, recv):
        _, _, c, chips = _place()
        return [_remote(ins[a].at[2 * cx + cy], outs[a].at[j], send.at[a, j], recv.at[a, j], (cx, cy, c))
                for a in range(len(ps)) for j, (cx, cy) in enumerate(chips)]

    return _swap_plan(ps, [_sds((NCHIP - 1,) + p.shape[1:], p.dtype) for p in ps], (len(ps), 3), copies)


def _shard_sum(p, got, kidx, name):
    _, hr, cdim = p.shape
    t = 256 if hr % 256 == 0 else 128

    def body(k_ref, p_ref, g_ref, o_ref):
        sm = p_ref[0]
        for j in range(NCHIP - 1):
            sm = sm + g_ref[j].astype(F32)
        o_ref[...] = sm

    return pl.pallas_call(
        body, name=name,
        grid_spec=pltpu.PrefetchScalarGridSpec(
            num_scalar_prefetch=1, grid=(hr // t,),
            in_specs=[pl.BlockSpec((1, t, cdim), lambda i, k_ref: (k_ref[0], i, 0)),
                      pl.BlockSpec((NCHIP - 1, t, cdim), lambda i, k_ref: (0, i, 0))],
            out_specs=pl.BlockSpec((t, cdim), lambda i, k_ref: (i, 0))),
        out_shape=_sds((hr, cdim)),
        compiler_params=_cp("parallel"),
    )(kidx, p, got)


def _pair_swap_plan(rs):
    def copies(ins, outs, send, recv):
        x, y, c, _ = _place()
        return [_remote(ins[a], outs[a], send.at[a], recv.at[a], (x, y, 1 - c)) for a in range(len(rs))]

    return _swap_plan(rs, [_sds(r.shape, r.dtype) for r in rs], (len(rs),), copies)


def _allgather8_plan(v):
    def pieces(ins, outs, send, recv):
        x, y, c, chips = _place()
        me, sibling = (x, y, c), (x, y, 1 - c)

        def copy(k, block, to, src=None):
            px, py, pc = block
            slot = outs[0].at[4 * px + 2 * py + pc]
            return _remote(slot if src is None else src, slot, send.at[k], recv.at[k], to)

        first = [copy(0, me, sibling, src=ins[0])] + [copy(1 + j, me, (*chip, c), src=ins[0])
                                                      for j, chip in enumerate(chips)]
        passed = [copy(4 + j, (*chip, c), sibling) for j, chip in enumerate(chips)]
        arrivals = [copy(1 + j, (*chip, c), me) for j, chip in enumerate(chips)]
        late = [copy(0, sibling, me)] + [copy(4 + j, (*chip, 1 - c), me) for j, chip in enumerate(chips)]
        return first, passed, arrivals, late

    def start(ins, outs, sems):
        for cp in pieces(ins, outs, *sems[0])[0]:
            cp.start()

    def finish(ins, outs, sems):
        first, passed, arrivals, late = pieces(ins, outs, *sems[0])
        for got, fwd in zip(arrivals, passed):
            got.wait_recv()
            fwd.start()
        for got in late:
            got.wait_recv()
        for cp in first + passed:
            cp.wait_send()

    return _Plan([v], [_sds((8,) + v.shape, v.dtype)], [(7,)], start, finish)


def _own_block(gathered, v):
    me = 4 * lax.axis_index("x") + 2 * lax.axis_index("y") + lax.axis_index("c")
    return lax.dynamic_update_index_in_dim(gathered, v, me, 0)


def _sum_devices(allv):
    _, r, _ = allv.shape

    def body(a_ref, o_ref):
        sm = a_ref[0]
        for d in range(1, 8):
            sm = sm + a_ref[d]
        o_ref[...] = sm

    return pl.pallas_call(
        body, name="small_sum", grid=(1,), in_specs=[_full_spec((8, r, 128))], out_specs=_full_spec((r, 128)),
        out_shape=_sds((r, 128)), compiler_params=_cp("arbitrary"),
    )(allv)


def _pack(arrs):
    rows = []
    for a in arrs:
        f = a.reshape(-1)
        f = jnp.pad(f, (0, (-f.shape[0]) % 128))
        rows.append(f.reshape(-1, 128))
    out = jnp.concatenate(rows, axis=0)
    return jnp.pad(out, ((0, (-out.shape[0]) % 8), (0, 0)))


def _unpack(packed, shapes):
    outs, r0 = [], 0
    for shp in shapes:
        n = math.prod(shp)
        nr = -(-n // 128)
        outs.append(packed[r0:r0 + nr].reshape(-1)[:n].reshape(shp))
        r0 += nr
    return outs


BIG = ("w_in", "w_out", "w_up", "w_down")
CONV = ("conv_ssm_w", "conv_lru_w")
WEIGHTS = ("norm_mix_pre", "w_in", "conv_ssm_w", "conv_ssm_b", "dt_bias", "a_log", "d_skip", "ssm_norm", "conv_lru_w",
           "conv_lru_b", "lru_wa", "lru_ba", "lru_wx", "lru_bx", "lru_lambda", "w_out", "norm_mix_post",
           "norm_mlp_pre", "w_up", "w_down", "norm_mlp_post")
SMALL = tuple(n for n in WEIGHTS if n not in BIG and n not in CONV)
EARLY = ("w_down", "w_up", "w_out")


def _cat_cols(g):
    return jnp.concatenate([g[k] for k in range(NCHIP)], axis=1)


class _Dist:
    def __init__(self, shards, cidx, kidx):
        self.shards, self.cidx, self.kidx = shards, cidx, kidx
        self.halves = {}

    def early_grads(self, w_down, w_up, w_out):
        self.shard_major = [w_down.reshape(NCHIP, D, D), jnp.stack([w_up[:, D * k:D * (k + 1)] for k in range(NCHIP)]),
                            w_out.reshape(NCHIP, D // NCHIP, D)]

    def plan(self, key):
        if key == "proj5":
            return _gather_plan([self.shards["w_out"], self.shards["w_up"]])
        if key == "lru_fwd":
            return _gather_plan([self.shards["w_down"]])
        if key == "lru_bwd":
            return _pair_exchange_plan(self.shard_major)
        if key == "ssd_bwd":
            return _chip_exchange_plan([pc for _, pc in self.pair])
        if key == "conv_bwd":
            return _pair_swap_plan(self.mine)
        return None

    def done(self, key, got, p):
        if key == "proj5":
            g_out, g_up = _own_shards(got, [self.shards["w_out"], self.shards["w_up"]])
            p.update(w_out=g_out.reshape(D, D), w_up=_cat_cols(g_up))
        elif key == "lru_fwd":
            g_down, = _own_shards(got, [self.shards["w_down"]])
            p.update(w_down=g_down.reshape(DFF, D))
        elif key == "lru_bwd":
            self.pair = [_pair_add(gs, o, self.cidx, f"grad_pair_add_{n}")
                         for gs, o, n in zip(self.shard_major, got, EARLY)]
        elif key == "ssd_bwd":
            self.mine = [_shard_sum(pf, o, self.kidx, f"grad_shard_sum_{n}")
                         for (pf, _), o, n in zip(self.pair, got, EARLY)]
        elif key == "conv_bwd":
            self.halves = {n: (mine, other) for n, mine, other in zip(EARLY, self.mine, got)}


def kernel(x, norm_mix_pre, w_in, conv_ssm_w, conv_ssm_b, dt_bias, a_log, d_skip, ssm_norm, conv_lru_w, conv_lru_b, lru_wa, lru_ba, lru_wx, lru_bx, lru_lambda, w_out, norm_mix_post, norm_mlp_pre, w_up, w_down, norm_mlp_post, loss_target, m_norm_mix_pre, m_w_in, m_conv_ssm_w, m_conv_ssm_b, m_dt_bias, m_a_log, m_d_skip, m_ssm_norm, m_conv_lru_w, m_conv_lru_b, m_lru_wa, m_lru_ba, m_lru_wx, m_lru_bx, m_lru_lambda, m_w_out, m_norm_mix_post, m_norm_mlp_pre, m_w_up, m_w_down, m_norm_mlp_post, v_norm_mix_pre, v_w_in, v_conv_ssm_w, v_conv_ssm_b, v_dt_bias, v_a_log, v_d_skip, v_ssm_norm, v_conv_lru_w, v_conv_lru_b, v_lru_wa, v_lru_ba, v_lru_wx, v_lru_bx, v_lru_lambda, v_w_out, v_norm_mix_post, v_norm_mlp_pre, v_w_up, v_w_down, v_norm_mlp_post):
    args = locals()
    w = {n: args[n][0] for n in WEIGHTS}
    m = {n: args["m_" + n][0] for n in WEIGHTS}
    v = {n: args["v_" + n][0] for n in WEIGHTS}
    cidx = lax.axis_index("c").astype(jnp.int32).reshape(1)
    kchip = 2 * lax.axis_index("x") + lax.axis_index("y")
    dist = _Dist({n: w[n].astype(MXU) for n in BIG}, cidx, kchip.astype(jnp.int32).reshape(1))

    first = [dist.shards["w_in"], w["conv_ssm_w"], w["conv_lru_w"]]
    gathered = _own_shards(_run_plan(_gather_plan(first[:1], first[1:]), "gather_w_in"), first)
    w5, wxbc, wdt = _split_w_in(_cat_cols(gathered[0]))
    p = {n: (w[n].reshape(1, -1) if w[n].ndim == 1 else w[n]) for n in SMALL}
    p.update(w5=w5, wxbc=wxbc, wdt=wdt, conv_ssm_w=_cat_cols(gathered[1]), conv_lru_w=_cat_cols(gathered[2]))

    loss, grad_x, g = _local_step(x[0], loss_target[0], p, dist)
    loss = lax.psum(loss, ("x", "y", "c"))

    small_shapes = [g[n].shape for n in SMALL + CONV]
    packed_small = _pack([g[n] for n in SMALL + CONV])
    g_in = jnp.stack([g["w_in"][:, W_IN_SHARD * k:W_IN_SHARD * (k + 1)] for k in range(NCHIP)])
    got_in, all_small = _run_plan(_merge_plans(_pair_exchange_plan([g_in]), _allgather8_plan(packed_small)),
                                  "grad_pair_exchange_w_in")
    p_in, pc_in = _pair_add(g_in, got_in, cidx, "grad_pair_add_w_in")
    from_chips, = _run_plan(_chip_exchange_plan([pc_in]), "grad_chip_exchange_w_in")
    half_in = _shard_sum(p_in, from_chips, dist.kidx, "grad_shard_sum_w_in")
    other_in, = _run_plan(_pair_swap_plan([half_in]), "grad_pair_swap_w_in")
    halves = dict(dist.halves, w_in=(half_in, other_in))

    reduced = {}
    summed = _unpack(_sum_devices(_own_block(all_small, packed_small)), small_shapes)
    for n, s in zip(SMALL + CONV, summed):
        if n in CONV:
            width = w[n].shape[1]
            reduced[n] = lax.dynamic_slice_in_dim(s, kchip * width, width, axis=1)
        else:
            reduced[n] = s.reshape(w[n].shape)

    delta, new_m, new_v = {}, {}, {}
    for n in BIG:
        mine, other = halves[n]
        reduced[n], delta[n], new_m[n], new_v[n] = _adamw_halves(w[n], mine, other, m[n], v[n], cidx, f"adamw_{n}")
    for n in CONV:
        delta[n], new_m[n], new_v[n] = _adamw(w[n], reduced[n], m[n], v[n], f"adamw_{n}")
    shapes = [w[n].shape for n in SMALL]
    packed = [_pack([d[n] for n in SMALL]) for d in (w, reduced, m, v)]
    for d, out in zip((delta, new_m, new_v), _adamw(*packed, "adamw_small")):
        d.update(zip(SMALL, _unpack(out, shapes)))

    lead = lambda d: [d[n][None] for n in WEIGHTS]
    return (loss, grad_x[None], *lead(reduced), *lead(delta), *lead(new_m), *lead(new_v))
```

```python
import functools
import math

import jax
import jax.numpy as jnp
from jax import lax
from jax.experimental import pallas as pl
from jax.experimental.pallas import tpu as pltpu

F32 = jnp.float32
BF16 = jnp.bfloat16
MXU = BF16

D = 1024
DFF = 4096
NH = 16
HP = 64
NG = 2
NS = 128
CH = 128
XBC = D + 2 * NG * NS
GW = D // NG
LRU_C = 8.0
EPS = 1e-6
NCHIP = 4
W_IN_COLS = 6672
W_IN_SHARD = W_IN_COLS // NCHIP

ADAM_LR = 0.001
ADAM_B1 = 0.9
ADAM_B2 = 0.999
ADAM_EPS = 1e-08
ADAM_WD = 0.01
ADAM_STEP = 10

VMEM_LIMIT = 56 * 1024 * 1024
MESH = pl.DeviceIdType.MESH


def _cp(*sem):
    return pltpu.CompilerParams(dimension_semantics=sem, vmem_limit_bytes=VMEM_LIMIT)


def _dot(a, b, ca=1, cb=0, prec=None):
    return lax.dot_general(a, b, (((ca,), (cb,)), ((), ())), precision=prec, preferred_element_type=F32)


def _mdot(a, b, ca=1, cb=0):
    return _dot(a.astype(MXU), b.astype(MXU), ca, cb)


def _bf16_parts(v, n):
    parts = []
    for i in range(n):
        p = v.astype(BF16)
        parts.append(p)
        if i < n - 1:
            v = v - p.astype(F32)
    return parts


def _xdot(a, b, passes, split_b=False):
    if split_b:
        a16 = a.astype(BF16)
        terms = [_dot(a16, p) for p in _bf16_parts(b, passes)]
    else:
        b16 = b.astype(BF16)
        terms = [_dot(p, b16) for p in _bf16_parts(a, passes)]
    return functools.reduce(lambda u, v: u + v, terms)


def _sig(x):
    return 0.5 * jnp.tanh(0.5 * x) + 0.5


def _silu(x):
    return x * _sig(x)


def _dsilu(x):
    s = _sig(x)
    return s * (1.0 + x * (1.0 - s))


def _softplus(x):
    e = jnp.exp(-jnp.abs(x))
    return jnp.maximum(x, 0.0) + jnp.where(e < 1e-4, e * (1.0 - 0.5 * e), jnp.log(1.0 + e))


_GELU_C = math.sqrt(2.0 / math.pi)


def _gelu(x):
    t = jnp.tanh(_GELU_C * (x + 0.044715 * x * x * x))
    return 0.5 * x * (1.0 + t)


def _gelu_and_grad(x):
    x2 = x * x
    t = jnp.tanh(_GELU_C * (x + 0.044715 * x * x2))
    half = 0.5 * (1.0 + t)
    return x * half, half + 0.5 * x * (1.0 - t * t) * _GELU_C * (1.0 + 3.0 * 0.044715 * x2)


def _one_minus_sq(a, la):
    x = 2.0 * la
    series = -x * (1.0 + x * (0.5 + x * (1.0 / 6.0)))
    return jnp.where(x > -0.01, series, 1.0 - a * a)


def _rms(x):
    return lax.rsqrt(jnp.mean(x * x, axis=-1, keepdims=True) + EPS)


def _rms_bwd(x, r, g, dy):
    xn = x * r
    dxh = dy * g
    m = jnp.mean(dxh * xn, axis=-1, keepdims=True)
    return r * (dxh - xn * m), jnp.sum(dy * xn, axis=0, keepdims=True)


def _row_spec(t, c, col=0):
    return pl.BlockSpec((t, c), lambda i: (i, col))


def _rev_spec(t, c, n, col=0):
    return pl.BlockSpec((t, c), lambda i: (n - 1 - i, col))


def _full_spec(shape):
    nd = len(shape)
    return pl.BlockSpec(shape, lambda *_: (0,) * nd)


def _sds(shape, dtype=F32):
    return jax.ShapeDtypeStruct(shape, dtype)


ANY = pl.BlockSpec(memory_space=pl.ANY)


class _Plan:
    def __init__(self, ins, outs, sems, start, finish):
        self.ins, self.outs, self.sems, self.start, self.finish = list(ins), list(outs), list(sems), start, finish


def _merge_plans(*plans):
    def each(fn_name, ins, outs, sems):
        i = o = s = 0
        for p in plans:
            getattr(p, fn_name)(ins[i:i + len(p.ins)], outs[o:o + len(p.outs)], sems[s:s + len(p.sems)])
            i, o, s = i + len(p.ins), o + len(p.outs), s + len(p.sems)

    return _Plan([a for p in plans for a in p.ins], [a for p in plans for a in p.outs],
                 [a for p in plans for a in p.sems], functools.partial(each, "start"), functools.partial(each, "finish"))


def _pcall(body, args, *, name, grid, in_specs, out_specs, out_shape, sem, scratch_shapes=(), plan=None):
    single = not isinstance(out_shape, (list, tuple))
    out_specs = [out_specs] if single else list(out_specs)
    out_shape = [out_shape] if single else list(out_shape)
    if plan is None:
        outs = pl.pallas_call(body, name=name, grid=grid, in_specs=list(in_specs), out_specs=out_specs,
                              out_shape=out_shape, scratch_shapes=list(scratch_shapes),
                              compiler_params=_cp(*sem))(*args)
        return outs[0] if single else outs
    n_in, n_out, n_sc, ni, no = len(in_specs), len(out_shape), len(scratch_shapes), len(plan.ins), len(plan.outs)

    def hosted(*refs):
        b0 = n_in + ni
        b1 = b0 + n_out + no
        sem_refs = refs[b1 + n_sc:]
        sems = [(sem_refs[2 * q], sem_refs[2 * q + 1]) for q in range(len(plan.sems))]
        ids = [pl.program_id(d) for d in range(len(grid))]
        first = functools.reduce(jnp.logical_and, [i == 0 for i in ids])
        last = functools.reduce(jnp.logical_and, [i == g - 1 for i, g in zip(ids, grid)])

        @pl.when(first)
        def _():
            plan.start(refs[n_in:b0], refs[b0 + n_out:b1], sems)

        body(*refs[:n_in], *refs[b0:b0 + n_out], *refs[b1:b1 + n_sc])

        @pl.when(last)
        def _():
            plan.finish(refs[n_in:b0], refs[b0 + n_out:b1], sems)

    dma = [pltpu.SemaphoreType.DMA(shape) for shape in plan.sems for _ in range(2)]
    outs = pl.pallas_call(hosted, name=name, grid=grid, in_specs=list(in_specs) + [ANY] * ni,
                          out_specs=out_specs + [ANY] * no, out_shape=out_shape + plan.outs,
                          scratch_shapes=list(scratch_shapes) + dma,
                          compiler_params=_cp(*("arbitrary",) * len(grid)))(*args, *plan.ins)
    return (outs[0] if single else outs[:n_out]), outs[n_out:]


def _run_plan(plan, name):
    ni, no = len(plan.ins), len(plan.outs)

    def body(*refs):
        sem_refs = refs[ni + no:]
        sems = [(sem_refs[2 * q], sem_refs[2 * q + 1]) for q in range(len(plan.sems))]
        plan.start(refs[:ni], refs[ni:ni + no], sems)
        plan.finish(refs[:ni], refs[ni:ni + no], sems)

    return pl.pallas_call(
        body, name=name, in_specs=[ANY] * ni, out_specs=[ANY] * no, out_shape=plan.outs,
        scratch_shapes=[pltpu.SemaphoreType.DMA(shape) for shape in plan.sems for _ in range(2)],
    )(*plan.ins)


def _matmul(a, b, *, name, ta=False, tb=False, tm=512, tn=1024, tk=1024, out_dtype=F32, a_fn=None, epi=None,
            epi_args=(), plan=None):
    m, k = (a.shape[1], a.shape[0]) if ta else a.shape
    n = b.shape[0] if tb else b.shape[1]
    tm, tn, tk = min(tm, m), min(tn, n), min(tk, k)
    nk = k // tk
    a_spec = pl.BlockSpec((tk, tm), lambda i, j, kk: (kk, i)) if ta else pl.BlockSpec((tm, tk), lambda i, j, kk: (i, kk))
    b_spec = pl.BlockSpec((tn, tk), lambda i, j, kk: (j, kk)) if tb else pl.BlockSpec((tk, tn), lambda i, j, kk: (kk, j))
    e_specs = [pl.BlockSpec((tm, tn), lambda i, j, kk: (i, j)) for _ in epi_args]
    ne = len(epi_args)

    def body(a_ref, b_ref, *rest):
        e_refs, o_ref, acc_ref = rest[:ne], rest[ne], rest[ne + 1]
        kk = pl.program_id(2)

        @pl.when(kk == 0)
        def _():
            acc_ref[...] = jnp.zeros_like(acc_ref)

        av = a_ref[...]
        if a_fn is not None:
            av = a_fn(av)
        acc_ref[...] += _mdot(av, b_ref[...], 0 if ta else 1, 1 if tb else 0)

        @pl.when(kk == nk - 1)
        def _():
            r = acc_ref[...]
            if epi is not None:
                r = epi(r, *[e[...] for e in e_refs])
            o_ref[...] = r.astype(o_ref.dtype)

    return _pcall(
        body, (a, b, *epi_args), name=name, grid=(m // tm, n // tn, nk),
        in_specs=[a_spec, b_spec] + e_specs,
        out_specs=pl.BlockSpec((tm, tn), lambda i, j, kk: (i, j)),
        out_shape=_sds((m, n), out_dtype),
        scratch_shapes=[pltpu.VMEM((tm, tn), F32)],
        sem=("parallel", "parallel", "arbitrary"), plan=plan)


def _relu2(p):
    p = jnp.maximum(p.astype(F32), 0.0)
    return p * p


def _norm_cast(x, g, name):
    s = x.shape[0]
    t = min(512, s)

    def body(x_ref, g_ref, o_ref):
        xv = x_ref[...]
        o_ref[...] = (xv * _rms(xv) * g_ref[...]).astype(o_ref.dtype)

    return pl.pallas_call(
        body, name=name, grid=(s // t,), in_specs=[_row_spec(t, D), _full_spec((1, D))],
        out_specs=_row_spec(t, D), out_shape=_sds((s, D), MXU), compiler_params=_cp("parallel"),
    )(x, g)


def _conv_fwd(xbc_raw, proj5, dt_raw, cw_s, cb_s, cw_l, cb_l, dt_bias):
    s = xbc_raw.shape[0]
    t = min(256, s)

    def body(xs_ref, xl_ref, dtr_ref, cws_ref, cbs_ref, cwl_ref, cbl_ref, dtb_ref, xc_ref, dsl_ref, xr_ref, dt_ref,
             bs_ref, bl_ref):
        @pl.when(pl.program_id(0) == 0)
        def _():
            bs_ref[0:8, :] = jnp.zeros((8, XBC), F32)
            bl_ref[0:8, :] = jnp.zeros((8, D), F32)

        bs_ref[8:t + 8, :] = xs_ref[...]
        bl_ref[8:t + 8, :] = xl_ref[...]

        def conv(buf, w_ref, b_ref):
            acc = b_ref[...] + w_ref[3:4, :] * buf[8:t + 8, :]
            for k in (1, 2, 3):
                acc = acc + w_ref[3 - k:4 - k, :] * buf[8 - k:t + 8 - k, :]
            return acc

        pre = conv(bs_ref, cws_ref, cbs_ref)
        sg = _sig(pre)
        xc_ref[...] = pre * sg
        dsl_ref[...] = (sg * (1.0 + pre * (1.0 - sg))).astype(dsl_ref.dtype)
        xr_ref[...] = conv(bl_ref, cwl_ref, cbl_ref)
        dt_ref[...] = _softplus(dtr_ref[...] + dtb_ref[...])
        bs_ref[0:8, :] = bs_ref[t:t + 8, :]
        bl_ref[0:8, :] = bl_ref[t:t + 8, :]

    return pl.pallas_call(
        body, name="conv_fwd", grid=(s // t,),
        in_specs=[_row_spec(t, XBC), _row_spec(t, D, 2), _row_spec(t, 128), _full_spec((4, XBC)),
                  _full_spec((1, XBC)), _full_spec((4, D)), _full_spec((1, D)), _full_spec((1, 128))],
        out_specs=[_row_spec(t, XBC), _row_spec(t, XBC), _row_spec(t, D), _row_spec(t, 128)],
        out_shape=[_sds((s, XBC)), _sds((s, XBC), BF16), _sds((s, D)), _sds((s, 128))],
        scratch_shapes=[pltpu.VMEM((t + 8, XBC), F32), pltpu.VMEM((t + 8, D), F32)],
        compiler_params=_cp("arbitrary"),
    )(xbc_raw, proj5, dt_raw, cw_s, cb_s, cw_l, cb_l, dt_bias)


def _ssd_chunk_setup(dt_ref, alog_ref, e_ref, at_ref, dtt_ref):
    lane = lax.broadcasted_iota(jnp.int32, (CH, 128), 1)
    row = lax.broadcasted_iota(jnp.int32, (CH, 128), 0)
    lane1 = lax.broadcasted_iota(jnp.int32, (1, 128), 1)
    a = jnp.where(lane1 < NH, -jnp.exp(alog_ref[...]), 0.0)
    dtv = dt_ref[...]
    adt = dtv * a
    tril = row >= lane
    acum = _xdot(tril.astype(F32), adt, 3, split_b=True)
    alast = jnp.sum(adt, axis=0, keepdims=True)
    at_ref[...] = acum.T
    dtt_ref[...] = dtv.T
    e = e_ref[...]
    ea_x = _xdot(jnp.exp(acum), e, 2)
    ws = jnp.exp(alast - acum) * dtv
    ws_x = _xdot(ws, e, 2)
    eal = jnp.exp(alast)
    eal_x = jnp.max(_xdot(jnp.broadcast_to(eal, (8, 128)), e, 3), axis=0, keepdims=True)
    return dict(lane=lane, row=row, tril=tril, a=a, dtv=dtv, acum=acum, alast=alast, ea_x=ea_x, ws=ws, ws_x=ws_x,
                eal=eal, eal_x=eal_x)


def _head_decay(cs, at_ref, dtt_ref, h):
    col = jnp.sum(jnp.where(cs["lane"] == h, cs["acum"], 0.0), axis=1, keepdims=True)
    ld = jnp.where(cs["tril"], jnp.exp(jnp.minimum(col - at_ref[h:h + 1, :], 0.0)), 0.0)
    return ld, dtt_ref[h:h + 1, :]


def _ssd_fwd(xbc_c, dt, proj5, a_log, dskip_x, ssm_norm, expand):
    s = xbc_c.shape[0]
    nc = s // CH

    def body(xc_ref, dt_ref, z_ref, alog_ref, dsk_ref, ng_ref, e_ref, y_ref, ya_ref, st_ref, h_ref, at_ref, dtt_ref,
             yd_ref):
        @pl.when(pl.program_id(0) == 0)
        def _():
            h_ref[...] = jnp.zeros_like(h_ref)

        cs = _ssd_chunk_setup(dt_ref, alog_ref, e_ref, at_ref, dtt_ref)
        lane = cs["lane"]
        for g in range(NG):
            gs = slice(GW * g, GW * (g + 1))
            bg = xc_ref[:, D + NS * g:D + NS * (g + 1)]
            cg = xc_ref[:, D + NG * NS + NS * g:D + NG * NS + NS * (g + 1)]
            cb = _mdot(cg, bg, 1, 1)
            for j in range(4 * g, 4 * g + 4):
                ps = slice(128 * j, 128 * (j + 1))
                xp = xc_ref[:, ps]
                acc = jnp.zeros((CH, 128), F32)
                for hf in range(2):
                    ld, rowdt = _head_decay(cs, at_ref, dtt_ref, 2 * j + hf)
                    hm = (lane >= HP) if hf else (lane < HP)
                    acc = acc + _mdot(cb * ld * rowdt, jnp.where(hm, xp, 0.0))
                yd_ref[:, ps] = acc
            hg = h_ref[:, gs]
            yd_ref[:, gs] += _mdot(cg, hg) * cs["ea_x"][:, gs]
            st = _mdot(bg, xc_ref[:, gs] * cs["ws_x"][:, gs], 0, 0)
            st_ref[0, :, gs] = hg
            h_ref[:, gs] = cs["eal_x"][:, gs] * hg + st
        y = yd_ref[...] + dsk_ref[...] * xc_ref[:, 0:D]
        y_ref[...] = y
        yg = y * _silu(z_ref[...])
        for g in range(NG):
            gs = slice(GW * g, GW * (g + 1))
            seg = yg[:, gs]
            ya_ref[:, gs] = seg * _rms(seg) * ng_ref[:, gs]

    return pl.pallas_call(
        body, name="ssd_fwd", grid=(nc,),
        in_specs=[_row_spec(CH, XBC), _row_spec(CH, 128), _row_spec(CH, D, 0), _full_spec((1, 128)),
                  _full_spec((1, D)), _full_spec((1, D)), _full_spec((128, D))],
        out_specs=[_row_spec(CH, D), _row_spec(CH, D), pl.BlockSpec((1, NS, D), lambda i: (i, 0, 0))],
        out_shape=[_sds((s, D)), _sds((s, D)), _sds((nc, NS, D))],
        scratch_shapes=[pltpu.VMEM((NS, D), F32), pltpu.VMEM((128, 128), F32), pltpu.VMEM((128, 128), F32),
                        pltpu.VMEM((CH, D), F32)],
        compiler_params=_cp("arbitrary"),
    )(xbc_c, dt, proj5, a_log, dskip_x, ssm_norm, expand)


def _lru_gates(xr, wa_ref, wx_ref, ba_ref, bx_ref, lam_ref):
    gr = _sig(_mdot(xr, wa_ref[...]) + ba_ref[...])
    gi = _sig(_mdot(xr, wx_ref[...]) + bx_ref[...])
    sp = _softplus(-lam_ref[...])
    la = -LRU_C * gr * sp
    a = jnp.exp(la)
    oms = _one_minus_sq(a, la)
    inv_mult = lax.rsqrt(oms)
    return gr, gi, sp, a, oms * inv_mult, inv_mult


def _blocked_scan(a, u, carry_ref, a_ref, u_ref, c_ref, out_ref, reverse):
    t = a.shape[0]
    ns = t // 8

    def combine(av, uv, idx, n, sh):
        m = (idx < n - sh) if reverse else (idx >= sh)
        by = n - sh if reverse else sh
        return jnp.where(m, av * pltpu.roll(av, by, 0), av), jnp.where(m, uv + av * pltpu.roll(uv, by, 0), uv)

    row = lax.broadcasted_iota(jnp.int32, (t, D), 0)
    rin = jnp.bitwise_and(row, 7)
    for sh in (1, 2, 4):
        m = (rin < 8 - sh) if reverse else (rin >= sh)
        by = t - sh if reverse else sh
        a, u = jnp.where(m, a * pltpu.roll(a, by, 0), a), jnp.where(m, u + a * pltpu.roll(u, by, 0), u)
    a_ref[...] = a
    u_ref[...] = u
    edge = 0 if reverse else 7
    for j in range(ns):
        c_ref[j:j + 1, :] = a_ref[8 * j + edge:8 * j + edge + 1, :]
    at = c_ref[...]
    for j in range(ns):
        c_ref[j:j + 1, :] = u_ref[8 * j + edge:8 * j + edge + 1, :]
    ut = c_ref[...]
    srow = lax.broadcasted_iota(jnp.int32, (ns, D), 0)
    sh = 1
    while sh < ns:
        at, ut = combine(at, ut, srow, ns, sh)
        sh *= 2
    cv = carry_ref[0:1, :]
    ends = ut + at * cv
    last = 0 if reverse else ns - 1
    first = ns - 1 if reverse else 0
    c_ref[...] = jnp.where(srow == first, cv, pltpu.roll(ends, first if reverse else 1, 0))
    carry_ref[0:1, :] = jnp.sum(jnp.where(srow == last, ends, 0.0), axis=0, keepdims=True)
    for j in range(ns):
        sl = slice(8 * j, 8 * j + 8)
        out_ref[sl, :] = u_ref[sl, :] + a_ref[sl, :] * c_ref[j:j + 1, :]


def _lru_fwd(xr, proj5, ya, wa_bd, wx_bd, ba, bx, lam, plan=None):
    s = xr.shape[0]
    t = min(256, s)

    def body(xr_ref, g_ref, ga_ref, gb_ref, ya_ref, wa_ref, wx_ref, ba_ref, bx_ref, lam_ref, h_ref, mg_ref, hc_ref,
             a_ref, u_ref, c_ref):
        @pl.when(pl.program_id(0) == 0)
        def _():
            hc_ref[...] = jnp.zeros_like(hc_ref)

        xrv = xr_ref[...]
        _, gi, _, a, mult, _ = _lru_gates(xrv, wa_ref, wx_ref, ba_ref, bx_ref, lam_ref)
        _blocked_scan(a, mult * gi * xrv, hc_ref, a_ref, u_ref, c_ref, h_ref, reverse=False)
        yb = h_ref[...] * _gelu(g_ref[...])
        mg_ref[...] = (_sig(ga_ref[...]) * ya_ref[...] + _sig(gb_ref[...]) * yb).astype(mg_ref.dtype)

    return _pcall(
        body, (xr, proj5, proj5, proj5, ya, wa_bd, wx_bd, ba, bx, lam), name="lru_fwd", grid=(s // t,),
        in_specs=[_row_spec(t, D), _row_spec(t, D, 1), _row_spec(t, D, 3), _row_spec(t, D, 4), _row_spec(t, D),
                  _full_spec((D, D)), _full_spec((D, D)), _full_spec((1, D)), _full_spec((1, D)), _full_spec((1, D))],
        out_specs=[_row_spec(t, D), _row_spec(t, D)],
        out_shape=[_sds((s, D)), _sds((s, D), MXU)],
        scratch_shapes=[pltpu.VMEM((8, D), F32), pltpu.VMEM((t, D), F32), pltpu.VMEM((t, D), F32),
                        pltpu.VMEM((t // 8, D), F32)],
        sem=("arbitrary",), plan=plan)


def _out_proj(merged, w_out, x, g2, g3):
    s = x.shape[0]
    t = min(256, s)

    def body(mg_ref, w_ref, x_ref, g2_ref, g3_ref, mix_ref, h1_ref, v_ref):
        mix = _mdot(mg_ref[...], w_ref[...])
        mix_ref[...] = mix
        h1 = x_ref[...] + mix * _rms(mix) * g2_ref[...]
        h1_ref[...] = h1
        v_ref[...] = (h1 * _rms(h1) * g3_ref[...]).astype(v_ref.dtype)

    return pl.pallas_call(
        body, name="out_proj", grid=(s // t,),
        in_specs=[_row_spec(t, D), _full_spec((D, D)), _row_spec(t, D), _full_spec((1, D)), _full_spec((1, D))],
        out_specs=[_row_spec(t, D), _row_spec(t, D), _row_spec(t, D)],
        out_shape=[_sds((s, D)), _sds((s, D)), _sds((s, D), MXU)],
        compiler_params=_cp("parallel"),
    )(merged, w_out, x, g2, g3)


def _down_loss(pre, w_down, h1, target, g4):
    s = pre.shape[0]
    t = min(256, s)

    def body(pre_ref, w_ref, h1_ref, tg_ref, g4_ref, dout_ref, dff_ref, loss_ref, dg4_ref):
        @pl.when(pl.program_id(0) == 0)
        def _():
            loss_ref[...] = jnp.zeros_like(loss_ref)
            dg4_ref[...] = jnp.zeros_like(dg4_ref)

        ff = _mdot(_relu2(pre_ref[...]), w_ref[...])
        r4 = _rms(ff)
        g4v = g4_ref[...]
        diff = h1_ref[...] + ff * r4 * g4v - tg_ref[...]
        sq = jnp.sum(jnp.sum(diff * diff, axis=1, keepdims=True), axis=0, keepdims=True)
        loss_ref[...] += (0.5 / D) * sq
        dout = diff * (1.0 / D)
        dout_ref[...] = dout
        dff, dg = _rms_bwd(ff, r4, g4v, dout)
        dff_ref[...] = dff.astype(dff_ref.dtype)
        dg4_ref[...] += dg

    return pl.pallas_call(
        body, name="down_loss", grid=(s // t,),
        in_specs=[_row_spec(t, DFF), _full_spec((DFF, D)), _row_spec(t, D), _row_spec(t, D), _full_spec((1, D))],
        out_specs=[_row_spec(t, D), _row_spec(t, D), _full_spec((1, 128)), _full_spec((1, D))],
        out_shape=[_sds((s, D)), _sds((s, D), MXU), _sds((1, 128)), _sds((1, D))],
        compiler_params=_cp("arbitrary"),
    )(pre, w_down, h1, target, g4)


def _dv_norms(dpre, w_up, h1, mix, dout, g3, g2):
    s = h1.shape[0]
    t = min(256, s)

    def body(dp_ref, w_ref, h1_ref, mix_ref, dout_ref, g3_ref, g2_ref, dh1_ref, dmix_ref, dg3_ref, dg2_ref):
        @pl.when(pl.program_id(0) == 0)
        def _():
            dg3_ref[...] = jnp.zeros_like(dg3_ref)
            dg2_ref[...] = jnp.zeros_like(dg2_ref)

        dv = _mdot(dp_ref[...], w_ref[...], 1, 1)
        h1 = h1_ref[...]
        dh1n, dg3 = _rms_bwd(h1, _rms(h1), g3_ref[...], dv)
        dh1 = dout_ref[...] + dh1n
        dh1_ref[...] = dh1
        mix = mix_ref[...]
        dmix, dg2 = _rms_bwd(mix, _rms(mix), g2_ref[...], dh1)
        dmix_ref[...] = dmix.astype(dmix_ref.dtype)
        dg3_ref[...] += dg3
        dg2_ref[...] += dg2

    return pl.pallas_call(
        body, name="dv_norms", grid=(s // t,),
        in_specs=[_row_spec(t, DFF), _full_spec((D, DFF)), _row_spec(t, D), _row_spec(t, D), _row_spec(t, D),
                  _full_spec((1, D)), _full_spec((1, D))],
        out_specs=[_row_spec(t, D), _row_spec(t, D), _full_spec((1, D)), _full_spec((1, D))],
        out_shape=[_sds((s, D)), _sds((s, D), MXU), _sds((1, D)), _sds((1, D))],
        compiler_params=_cp("arbitrary"),
    )(dpre, w_up, h1, mix, dout, g3, g2)


def _lru_bwd(dmerged, ya, xr, h, proj5, wa_bd, wx_bd, ba, bx, lam, plan=None):
    s = xr.shape[0]
    t = min(128, s)
    n = s // t
    rs = functools.partial(_rev_spec, t, D, n)

    def body(dm_ref, ya_ref, xr_ref, h_ref, hp_ref, g_ref, ga_ref, gb_ref, wa_ref, wx_ref, ba_ref, bx_ref, lam_ref,
             dya_ref, dga_ref, dgb_ref, dg_ref, dxr_ref, dpr_ref, dpi_ref, dlam_ref, dba_ref, dbx_ref, gc_ref,
             af_ref, an_ref, us_ref, c_ref, gs_ref):
        i = pl.program_id(0)

        @pl.when(i == 0)
        def _():
            gc_ref[...] = jnp.zeros_like(gc_ref)
            af_ref[...] = jnp.zeros_like(af_ref)
            dlam_ref[...] = jnp.zeros_like(dlam_ref)
            dba_ref[...] = jnp.zeros_like(dba_ref)
            dbx_ref[...] = jnp.zeros_like(dbx_ref)

        xrv = xr_ref[...]
        gr, gi, sp, a, mult, inv_mult = _lru_gates(xrv, wa_ref, wx_ref, ba_ref, bx_ref, lam_ref)
        hv = h_ref[...]
        dm = dm_ref[...]
        sa = _sig(ga_ref[...])
        sb = _sig(gb_ref[...])
        gel, dgel = _gelu_and_grad(g_ref[...])
        dya = dm * sa
        dya_ref[...] = dya
        dga_ref[...] = (dya * ya_ref[...] * (1.0 - sa)).astype(dga_ref.dtype)
        dyb = dm * sb
        dybh = dyb * hv
        dgb_ref[...] = (dybh * gel * (1.0 - sb)).astype(dgb_ref.dtype)
        dg_ref[...] = (dybh * dgel).astype(dg_ref.dtype)
        row = lax.broadcasted_iota(jnp.int32, (t, D), 0)
        an = jnp.where(row == t - 1, af_ref[0:1, :], pltpu.roll(a, t - 1, 0))
        _blocked_scan(an, dyb * gel, gc_ref, an_ref, us_ref, c_ref, gs_ref, reverse=True)
        gfull = gs_ref[...]
        af_ref[0:1, :] = jnp.sum(jnp.where(row == 0, a, 0.0), axis=0, keepdims=True)
        hlast = jnp.where(i == n - 1, 0.0, hp_ref[7:8, :])
        hprev = jnp.where(row == 0, hlast, pltpu.roll(hv, 1, 0))
        gx = gfull * xrv
        dgi = gx * mult
        dla = a * (gfull * hprev - gx * gi * a * inv_mult)
        dgr = dla * (-LRU_C * sp)
        dsp = jnp.sum(dla * (-LRU_C * gr), axis=0, keepdims=True)
        dlam_ref[...] += dsp * (-_sig(-lam_ref[...]))
        dpr = dgr * gr * (1.0 - gr)
        dpi = dgi * gi * (1.0 - gi)
        dpr_ref[...] = dpr.astype(dpr_ref.dtype)
        dpi_ref[...] = dpi.astype(dpi_ref.dtype)
        dba_ref[...] += jnp.sum(dpr, axis=0, keepdims=True)
        dbx_ref[...] += jnp.sum(dpi, axis=0, keepdims=True)
        dxr_ref[...] = gfull * mult * gi + _mdot(dpr, wa_ref[...], 1, 1) + _mdot(dpi, wx_ref[...], 1, 1)

    hp_spec = pl.BlockSpec((8, D), lambda i: (jnp.maximum((n - 1 - i) * (t // 8) - 1, 0), 0))
    return _pcall(
        body, (dmerged, ya, xr, h, h, proj5, proj5, proj5, wa_bd, wx_bd, ba, bx, lam), name="lru_bwd", grid=(n,),
        in_specs=[rs(), rs(), rs(), rs(), hp_spec, rs(1), rs(3), rs(4), _full_spec((D, D)), _full_spec((D, D)),
                  _full_spec((1, D)), _full_spec((1, D)), _full_spec((1, D))],
        out_specs=[rs(), rs(), rs(), rs(), rs(), rs(), rs(), _full_spec((1, D)), _full_spec((1, D)),
                   _full_spec((1, D))],
        out_shape=[_sds((s, D)), _sds((s, D), MXU), _sds((s, D), MXU), _sds((s, D), MXU), _sds((s, D)),
                   _sds((s, D), MXU), _sds((s, D), MXU), _sds((1, D)), _sds((1, D)), _sds((1, D))],
        scratch_shapes=[pltpu.VMEM((8, D), F32), pltpu.VMEM((8, D), F32), pltpu.VMEM((t, D), F32),
                        pltpu.VMEM((t, D), F32), pltpu.VMEM((t // 8, D), F32), pltpu.VMEM((t, D), F32)],
        sem=("arbitrary",), plan=plan)


def _ssd_bwd(dya, y, proj5, xbc_c, dt, states, a_log, dskip_x, ssm_norm, expand, reduce_, plan=None):
    s = xbc_c.shape[0]
    nc = s // CH
    rv = functools.partial(_rev_spec, CH)

    def body(dya_ref, y_ref, z_ref, xc_ref, dt_ref, st_ref, alog_ref, dsk_ref, ng_ref, e_ref, et_ref, dz_ref,
             dxc_ref, ddt_ref, dng_ref, ddsk_ref, dalog_ref, dh_ref, at_ref, dtt_ref, dat_ref, ddtt_ref, dy_ref,
             yoffdy_ref, xbds_ref):
        @pl.when(pl.program_id(0) == 0)
        def _():
            dh_ref[...] = jnp.zeros_like(dh_ref)
            dng_ref[...] = jnp.zeros_like(dng_ref)
            ddsk_ref[...] = jnp.zeros_like(ddsk_ref)
            dalog_ref[...] = jnp.zeros_like(dalog_ref)

        cs = _ssd_chunk_setup(dt_ref, alog_ref, e_ref, at_ref, dtt_ref)
        lane, row = cs["lane"], cs["row"]
        et = et_ref[...]
        for g in range(NG):
            gs = slice(GW * g, GW * (g + 1))
            yv = y_ref[:, gs]
            zv = z_ref[:, gs]
            sz = _silu(zv)
            yg = yv * sz
            dyav = dya_ref[:, gs]
            dyg, dng = _rms_bwd(yg, _rms(yg), ng_ref[:, gs], dyav)
            dng_ref[:, gs] += dng
            dy_ref[:, gs] = dyg * sz
            dz_ref[:, gs] = (dyg * yv * _dsilu(zv)).astype(dz_ref.dtype)
        dyv = dy_ref[...]
        xs = xc_ref[:, 0:D]
        ddsk_ref[...] += jnp.sum(dyv * xs, axis=0, keepdims=True)
        dxc_ref[:, 0:D] = dyv * dsk_ref[...]
        dat_ref[...] = jnp.zeros_like(dat_ref)
        ddtt_ref[...] = jnp.zeros_like(ddtt_ref)
        hh = jnp.sum(dh_ref[...] * st_ref[0], axis=0, keepdims=True)
        deal = jnp.max(_xdot(jnp.broadcast_to(hh, (8, D)), et, 3), axis=0, keepdims=True)
        d_acum = jnp.zeros((CH, 128), F32)
        for g in range(NG):
            gs = slice(GW * g, GW * (g + 1))
            bs_ = slice(D + NS * g, D + NS * (g + 1))
            cs_ = slice(D + NG * NS + NS * g, D + NG * NS + NS * (g + 1))
            bg = xc_ref[:, bs_]
            cg = xc_ref[:, cs_]
            cb = _mdot(cg, bg, 1, 1)
            hg = st_ref[0, :, gs]
            dhg = dh_ref[:, gs]
            dyg_ = dy_ref[:, gs]
            xsg = xc_ref[:, gs]
            ea = cs["ea_x"][:, gs]
            wsx = cs["ws_x"][:, gs]
            dp = dyg_ * ea
            yoffdy_ref[:, gs] = dp * _mdot(cg, hg)
            dc = _mdot(dp, hg, 1, 1)
            dhprev = _mdot(cg, dp, 0, 0)
            bds = _mdot(bg, dhg)
            dxc_ref[:, gs] += wsx * bds
            xbds_ref[:, gs] = xsg * bds
            db = _mdot(xsg * wsx, dhg, 1, 1)
            dh_ref[:, gs] = dhprev + cs["eal_x"][:, gs] * dhg
            dcbs = jnp.zeros((CH, CH), F32)
            for j in range(4 * g, 4 * g + 4):
                ps = slice(128 * j, 128 * (j + 1))
                xp = xc_ref[:, ps]
                dyp = dy_ref[:, ps]
                dxacc = jnp.zeros((CH, 128), F32)
                for hf in range(2):
                    hd = 2 * j + hf
                    ld, rowdt = _head_decay(cs, at_ref, dtt_ref, hd)
                    hm = (lane >= HP) if hf else (lane < HP)
                    dym = jnp.where(hm, dyp, 0.0)
                    w = cb * ld * rowdt
                    dw = _mdot(dym, jnp.where(hm, xp, 0.0), 1, 1)
                    dxacc = dxacc + _mdot(w, dym, 0, 0)
                    nm = dw * w
                    ddtt_ref[hd:hd + 1, :] += jnp.sum(dw * cb * ld, axis=0, keepdims=True)
                    d_acum = d_acum + jnp.where(lane == hd, jnp.sum(nm, axis=1, keepdims=True), 0.0)
                    dat_ref[hd:hd + 1, :] -= jnp.sum(nm, axis=0, keepdims=True)
                    dcbs = dcbs + dw * ld * rowdt
                dxc_ref[:, ps] += dxacc
            dxc_ref[:, bs_] = db + _mdot(dcbs, cg, 0, 0)
            dxc_ref[:, cs_] = dc + _mdot(dcbs, bg)
        dws = _xdot(xbds_ref[...], et, 2)
        ws = cs["ws"]
        d_acum = d_acum - dws * ws + _xdot(yoffdy_ref[...], et, 2) + dat_ref[...].T
        d_alast = jnp.sum(dws * ws, axis=0, keepdims=True) + deal * cs["eal"]
        d_acum = d_acum + jnp.where(row == CH - 1, d_alast, 0.0)
        triu = row <= lane
        d_adt = _xdot(triu.astype(F32), d_acum, 3, split_b=True)
        ddt_ref[...] = dws * jnp.exp(cs["alast"] - cs["acum"]) + ddtt_ref[...].T + d_adt * cs["a"]
        dalog_ref[...] += jnp.sum(d_adt * cs["dtv"], axis=0, keepdims=True) * cs["a"]

    return _pcall(
        body, (dya, y, proj5, xbc_c, dt, states, a_log, dskip_x, ssm_norm, expand, reduce_), name="ssd_bwd",
        grid=(nc,),
        in_specs=[rv(D, nc), rv(D, nc), rv(D, nc, 0), rv(XBC, nc), rv(128, nc),
                  pl.BlockSpec((1, NS, D), lambda i: (nc - 1 - i, 0, 0)), _full_spec((1, 128)), _full_spec((1, D)),
                  _full_spec((1, D)), _full_spec((128, D)), _full_spec((D, 128))],
        out_specs=[rv(D, nc), rv(XBC, nc), rv(128, nc), _full_spec((1, D)), _full_spec((1, D)),
                   _full_spec((1, 128))],
        out_shape=[_sds((s, D), MXU), _sds((s, XBC)), _sds((s, 128)), _sds((1, D)), _sds((1, D)), _sds((1, 128))],
        scratch_shapes=[pltpu.VMEM((NS, D), F32), pltpu.VMEM((128, 128), F32), pltpu.VMEM((128, 128), F32),
                        pltpu.VMEM((128, 128), F32), pltpu.VMEM((128, 128), F32), pltpu.VMEM((CH, D), F32),
                        pltpu.VMEM((CH, D), F32), pltpu.VMEM((CH, D), F32)],
        sem=("arbitrary",), plan=plan)


def _conv_bwd(dxbc_c, dsilu, dxr, ddt, xbc_raw, proj5, dt_raw, cw_s, cw_l, dt_bias, plan=None):
    s = xbc_raw.shape[0]
    t = min(256, s)
    n = s // t

    def body(dxc_ref, dsl_ref, dxr_ref, ddt_ref, xs_ref, xl_ref, dtr_ref, cws_ref, cwl_ref, dtb_ref, dxs_ref,
             dxl_ref, ddtr_ref, dcws_ref, dcbs_ref, dcwl_ref, dcbl_ref, ddtb_ref, ds_ref, dl_ref):
        @pl.when(pl.program_id(0) == 0)
        def _():
            ds_ref[t:t + 8, :] = jnp.zeros((8, XBC), F32)
            dl_ref[t:t + 8, :] = jnp.zeros((8, D), F32)
            for r in (dcws_ref, dcbs_ref, dcwl_ref, dcbl_ref, ddtb_ref):
                r[...] = jnp.zeros_like(r)

        ds_ref[0:t, :] = dxc_ref[...] * dsl_ref[...].astype(F32)
        dl_ref[0:t, :] = dxr_ref[...]

        def back(dbuf, x_ref, w_ref, dx_ref, dw_ref, db_ref):
            xv = x_ref[...]
            dpre = dbuf[0:t, :]
            dx = w_ref[3:4, :] * dpre
            dw_ref[3:4, :] += jnp.sum(dpre * xv, axis=0, keepdims=True)
            db_ref[...] += jnp.sum(dpre, axis=0, keepdims=True)
            for k in (1, 2, 3):
                ahead = dbuf[k:t + k, :]
                dx = dx + w_ref[3 - k:4 - k, :] * ahead
                dw_ref[3 - k:4 - k, :] += jnp.sum(ahead * xv, axis=0, keepdims=True)
            dx_ref[...] = dx.astype(dx_ref.dtype)
            dbuf[t:t + 8, :] = dbuf[0:8, :]

        back(ds_ref, xs_ref, cws_ref, dxs_ref, dcws_ref, dcbs_ref)
        back(dl_ref, xl_ref, cwl_ref, dxl_ref, dcwl_ref, dcbl_ref)
        ddtr = ddt_ref[...] * _sig(dtr_ref[...] + dtb_ref[...])
        ddtr_ref[...] = ddtr.astype(ddtr_ref.dtype)
        ddtb_ref[...] += jnp.sum(ddtr, axis=0, keepdims=True)

    rv = functools.partial(_rev_spec, t)
    return _pcall(
        body, (dxbc_c, dsilu, dxr, ddt, xbc_raw, proj5, dt_raw, cw_s, cw_l, dt_bias), name="conv_bwd", grid=(n,),
        in_specs=[rv(XBC, n), rv(XBC, n), rv(D, n), rv(128, n), rv(XBC, n), rv(D, n, 2), rv(128, n),
                  _full_spec((4, XBC)), _full_spec((4, D)), _full_spec((1, 128))],
        out_specs=[rv(XBC, n), rv(D, n), rv(128, n), _full_spec((4, XBC)), _full_spec((1, XBC)), _full_spec((4, D)),
                   _full_spec((1, D)), _full_spec((1, 128))],
        out_shape=[_sds((s, XBC), MXU), _sds((s, D), MXU), _sds((s, 128), MXU), _sds((4, XBC)), _sds((1, XBC)),
                   _sds((4, D)), _sds((1, D)), _sds((1, 128))],
        scratch_shapes=[pltpu.VMEM((t + 8, XBC), F32), pltpu.VMEM((t + 8, D), F32)],
        sem=("arbitrary",), plan=plan)


def _du_norm(pieces5, dxbc, ddtr, w5, wxbc, wdt, x, dh1, g1, plan=None):
    s = x.shape[0]
    t = min(256, s)

    def body(p0, p1, p2, p3, p4, dxbc_ref, ddtr_ref, w5_ref, wx_ref, wd_ref, x_ref, dh1_ref, g1_ref, dx_ref, dg1_ref):
        @pl.when(pl.program_id(0) == 0)
        def _():
            dg1_ref[...] = jnp.zeros_like(dg1_ref)

        du = _mdot(dxbc_ref[...], wx_ref[...], 1, 1) + _mdot(ddtr_ref[...], wd_ref[...], 1, 1)
        for b, p in enumerate((p0, p1, p2, p3, p4)):
            du = du + _mdot(p[...], w5_ref[:, D * b:D * (b + 1)], 1, 1)
        xv = x_ref[...]
        dxn, dg1 = _rms_bwd(xv, _rms(xv), g1_ref[...], du)
        dx_ref[...] = dh1_ref[...] + dxn
        dg1_ref[...] += dg1

    return _pcall(
        body, (*pieces5, dxbc, ddtr, w5, wxbc, wdt, x, dh1, g1), name="du_norm", grid=(s // t,),
        in_specs=[_row_spec(t, D)] * 5 + [_row_spec(t, XBC), _row_spec(t, 128), _full_spec((D, 5 * D)),
                                          _full_spec((D, XBC)), _full_spec((D, 128)), _row_spec(t, D),
                                          _row_spec(t, D), _full_spec((1, D))],
        out_specs=[_row_spec(t, D), _full_spec((1, D))],
        out_shape=[_sds((s, D)), _sds((1, D))],
        sem=("arbitrary",), plan=plan)


def _adamw(w, g, m, v, name):
    r, c = w.shape
    t = r
    if r * c > 256 * 1024:
        t = next(cand for cand in (512, 256, 128, 64, 32, 16, 8) if r % cand == 0 and cand * c <= 512 * 1024)
    bc1 = 1.0 - ADAM_B1 ** ADAM_STEP
    bc2 = 1.0 - ADAM_B2 ** ADAM_STEP

    def body(w_ref, g_ref, m_ref, v_ref, d_ref, nm_ref, nv_ref):
        gv = g_ref[...]
        nm = ADAM_B1 * m_ref[...] + (1.0 - ADAM_B1) * gv
        nv = ADAM_B2 * v_ref[...] + (1.0 - ADAM_B2) * (gv * gv)
        nm_ref[...] = nm
        nv_ref[...] = nv
        d_ref[...] = -ADAM_LR * ((nm / bc1) / (jnp.sqrt(nv / bc2) + ADAM_EPS) + ADAM_WD * w_ref[...])

    spec = pl.BlockSpec((t, c), lambda i: (i, 0))
    return pl.pallas_call(
        body, name=name, grid=(r // t,), in_specs=[spec] * 4, out_specs=[spec] * 3,
        out_shape=[_sds((r, c))] * 3, compiler_params=_cp("parallel"),
    )(w, g, m, v)


def _adamw_halves(w, g_mine, g_other, m, v, cidx, name):
    r, c = w.shape
    hr = r // 2
    t = 256 if hr % 256 == 0 else 128
    nb = hr // t
    bc1 = 1.0 - ADAM_B1 ** ADAM_STEP
    bc2 = 1.0 - ADAM_B2 ** ADAM_STEP

    def body(c_ref, w_ref, gm_ref, go_ref, m_ref, v_ref, g_ref, d_ref, nm_ref, nv_ref):
        mine = (pl.program_id(0) // nb) == c_ref[0]
        gv = jnp.where(mine, gm_ref[...], go_ref[...])
        g_ref[...] = gv
        nm = ADAM_B1 * m_ref[...] + (1.0 - ADAM_B1) * gv
        nv = ADAM_B2 * v_ref[...] + (1.0 - ADAM_B2) * (gv * gv)
        nm_ref[...] = nm
        nv_ref[...] = nv
        d_ref[...] = -ADAM_LR * ((nm / bc1) / (jnp.sqrt(nv / bc2) + ADAM_EPS) + ADAM_WD * w_ref[...])

    spec = pl.BlockSpec((t, c), lambda i, c_ref: (i, 0))
    half = pl.BlockSpec((t, c), lambda i, c_ref: (i % nb, 0))
    return pl.pallas_call(
        body, name=name,
        grid_spec=pltpu.PrefetchScalarGridSpec(num_scalar_prefetch=1, grid=(2 * nb,),
                                               in_specs=[spec, half, half, spec, spec], out_specs=[spec] * 4),
        out_shape=[_sds((r, c))] * 4, compiler_params=_cp("parallel"),
    )(cidx, w, g_mine, g_other, m, v)


def _block_diag(w):
    eye = jnp.eye(NH, dtype=w.dtype)
    return (w[:, :, None, :] * eye[:, None, :, None]).reshape(D, D)


def _diag_blocks(full):
    return jnp.stack([full[HP * h:HP * (h + 1), HP * h:HP * (h + 1)] for h in range(NH)])


def _pad_lanes(v, n=128):
    return jnp.pad(v, ((0, 0), (0, n - v.shape[1])))


def _local_step(x, target, p, dist=None):
    heads = jnp.arange(D, dtype=jnp.int32) // HP
    expand = (jnp.arange(128, dtype=jnp.int32)[:, None] == heads[None, :]).astype(F32)
    reduce_ = expand.T
    dskip_x = jnp.repeat(p["d_skip"], HP, axis=1)
    a_log = _pad_lanes(p["a_log"])
    dt_bias = _pad_lanes(p["dt_bias"])
    w5, wxbc, wdt = p["w5"], p["wxbc"], p["wdt"]
    wa_bd = _block_diag(p["lru_wa"]).astype(MXU)
    wx_bd = _block_diag(p["lru_wx"]).astype(MXU)
    ba = p["lru_ba"].reshape(1, D)
    bx = p["lru_bx"].reshape(1, D)

    def hosted(key, fn):
        plan = dist.plan(key) if dist is not None else None
        if plan is None:
            return fn(plan=None)
        outs, got = fn(plan=plan)
        dist.done(key, got, p)
        return outs

    u = _norm_cast(x, p["norm_mix_pre"], "norm_u")
    proj5 = hosted("proj5", functools.partial(_matmul, u, w5, name="proj5"))
    xbc_raw = _matmul(u, wxbc, name="proj_xbc", tn=XBC)
    dt_raw = _matmul(u, wdt, name="proj_dt")
    xbc_c, dsilu, xr, dt = _conv_fwd(xbc_raw, proj5, dt_raw, p["conv_ssm_w"], p["conv_ssm_b"], p["conv_lru_w"],
                                     p["conv_lru_b"], dt_bias)
    y, ya, states = _ssd_fwd(xbc_c, dt, proj5, a_log, dskip_x, p["ssm_norm"], expand)
    h, merged = hosted("lru_fwd", functools.partial(_lru_fwd, xr, proj5, ya, wa_bd, wx_bd, ba, bx, p["lru_lambda"]))
    mix, h1, v = _out_proj(merged, p["w_out"], x, p["norm_mix_post"], p["norm_mlp_pre"])
    pre = _matmul(v, p["w_up"], name="up_proj", tm=1024, out_dtype=MXU)
    dout, dff, loss, dg4 = _down_loss(pre, p["w_down"], h1, target, p["norm_mlp_post"])

    dpre = _matmul(dff, p["w_down"], name="d_pre", tb=True, tm=1024, out_dtype=MXU,
                   epi=lambda r, pr: r * (2.0 * jnp.maximum(pr.astype(F32), 0.0)), epi_args=(pre,))
    g_w_down = _matmul(pre, dff, name="dw_down", ta=True, tm=1024, tn=1024, tk=512, a_fn=_relu2)
    dh1, dmix, dg3, dg2 = _dv_norms(dpre, p["w_up"], h1, mix, dout, p["norm_mlp_pre"], p["norm_mix_post"])
    g_w_up = _matmul(v, dpre, name="dw_up", ta=True, tm=1024, tn=1024, tk=512)
    dmerged = _matmul(dmix, p["w_out"], name="d_merged", tb=True)
    g_w_out = _matmul(merged, dmix, name="dw_out", ta=True, tm=1024, tn=1024, tk=512)
    if dist is not None:
        dist.early_grads(w_down=g_w_down, w_up=g_w_up, w_out=g_w_out)
    (dya, dga, dgb, dg, dxr, dpr, dpi, dlam, dba, dbx) = hosted("lru_bwd", functools.partial(
        _lru_bwd, dmerged, ya, xr, h, proj5, wa_bd, wx_bd, ba, bx, p["lru_lambda"]))
    g_wa = _diag_blocks(_matmul(xr, dpr, name="dw_lru_a", ta=True, tm=1024, tn=1024, tk=512))
    g_wx = _diag_blocks(_matmul(xr, dpi, name="dw_lru_x", ta=True, tm=1024, tn=1024, tk=512))
    dz, dxbc_c, ddt, dng, ddsk, dalog = hosted("ssd_bwd", functools.partial(
        _ssd_bwd, dya, y, proj5, xbc_c, dt, states, a_log, dskip_x, p["ssm_norm"], expand, reduce_))
    (dxbc, dxl, ddtr, dcws, dcbs, dcwl, dcbl, ddtb) = hosted("conv_bwd", functools.partial(
        _conv_bwd, dxbc_c, dsilu, dxr, ddt, xbc_raw, proj5, dt_raw, p["conv_ssm_w"], p["conv_lru_w"], dt_bias))
    pieces5 = (dz, dg, dxl, dga, dgb)
    gw5 = [_matmul(u, pc, name=f"dw_in_{i}", ta=True, tm=1024, tn=1024, tk=512) for i, pc in enumerate(pieces5)]
    gwxbc = _matmul(u, dxbc, name="dw_in_xbc", ta=True, tm=1024, tn=XBC, tk=512)
    gwdt = _matmul(u, ddtr, name="dw_in_dt", ta=True, tm=1024, tn=128, tk=512)
    g_w_in = jnp.concatenate([gw5[0], gwxbc, gwdt[:, :NH], gw5[1], gw5[2], gw5[3], gw5[4]], axis=1)
    if dist is not None:
        dist.late_grads(w_in=g_w_in)
    grad_x, dg1 = hosted("du_norm", functools.partial(_du_norm, pieces5, dxbc, ddtr, w5, wxbc, wdt, x, dh1,
                                                       p["norm_mix_pre"]))
    grads = {
        "norm_mix_pre": dg1, "w_in": g_w_in, "conv_ssm_w": dcws, "conv_ssm_b": dcbs, "dt_bias": ddtb[:, :NH],
        "a_log": dalog[:, :NH], "d_skip": ddsk.reshape(NH, HP).sum(axis=1)[None, :], "ssm_norm": dng,
        "conv_lru_w": dcwl, "conv_lru_b": dcbl, "lru_wa": g_wa, "lru_ba": dba.reshape(NH, HP), "lru_wx": g_wx,
        "lru_bx": dbx.reshape(NH, HP), "lru_lambda": dlam, "w_out": g_w_out, "norm_mix_post": dg2,
        "norm_mlp_pre": dg3, "w_up": g_w_up, "w_down": g_w_down, "norm_mlp_post": dg4,
    }
    return loss[0, 0], grad_x, grads


def _split_w_in(w_in_full):
    z, xbc, dtc, g, xl, ga, gb = jnp.split(w_in_full, [D, D + XBC, D + XBC + NH, 2 * D + XBC + NH,
                                                        3 * D + XBC + NH, 4 * D + XBC + NH], axis=1)
    return jnp.concatenate([z, g, xl, ga, gb], axis=1), xbc, _pad_lanes(dtc)


COMM = BF16


def _place():
    x, y, c = lax.axis_index("x"), lax.axis_index("y"), lax.axis_index("c")
    chips = [(1 - x, y), (x, 1 - y), (1 - x, 1 - y)]
    return x, y, c, chips


def _remote(src, dst, send_sem, recv_sem, to):
    return pltpu.make_async_remote_copy(src_ref=src, dst_ref=dst, send_sem=send_sem, recv_sem=recv_sem, device_id=to,
                                        device_id_type=MESH)


def _gather_plan(big, small=()):
    nb = len(big)
    arrs = list(big) + list(small)
    na = len(arrs)

    def direct(ins, outs, send, recv):
        x, y, c, chips = _place()
        k = 2 * x + y
        cps = []
        for a in range(na):
            if a < nb:
                hr = ins[a].shape[0] // 2
                src, dst = ins[a].at[pl.ds(c * hr, hr)], outs[a].at[k, pl.ds(c * hr, hr)]
            else:
                src, dst = ins[a], outs[a].at[k]
            cps += [_remote(src, dst, send.at[a, j], recv.at[a, j], (cx, cy, c)) for j, (cx, cy) in enumerate(chips)]
        return cps

    def start(ins, outs, sems):
        for cp in direct(ins, outs, *sems[0]):
            cp.start()

    def finish(ins, outs, sems):
        send, recv = sems[0]
        x, y, c, chips = _place()
        passed = []
        for j, (cx, cy) in enumerate(chips):
            kj = 2 * cx + cy
            for a in range(na):
                if a < nb:
                    hr = ins[a].shape[0] // 2
                    got = outs[a].at[kj, pl.ds(c * hr, hr)]
                    _remote(got, got, send.at[a, j], recv.at[a, j], (cx, cy, c)).wait_recv()
                    passed.append(_remote(got, got, send.at[a, 3 + j], recv.at[a, 3 + j], (x, y, 1 - c)))
                    passed[-1].start()
                else:
                    got = outs[a].at[kj]
                    _remote(got, got, send.at[a, j], recv.at[a, j], (cx, cy, c)).wait_recv()
        for j, (cx, cy) in enumerate(chips):
            for a in range(nb):
                hr = ins[a].shape[0] // 2
                got = outs[a].at[2 * cx + cy, pl.ds((1 - c) * hr, hr)]
                _remote(got, got, send.at[a, 3 + j], recv.at[a, 3 + j], (x, y, 1 - c)).wait_recv()
        for cp in direct(ins, outs, send, recv) + passed:
            cp.wait_send()

    return _Plan(arrs, [_sds((NCHIP,) + a.shape, a.dtype) for a in arrs], [(na, 6)], start, finish)


def _own_shards(gathered, shards):
    kchip = 2 * lax.axis_index("x") + lax.axis_index("y")
    return [lax.dynamic_update_index_in_dim(o, a, kchip, 0) for o, a in zip(gathered, shards)]


def _swap_plan(ins, outs, sems, copies):
    def start(i, o, s):
        for cp in copies(i, o, *s[0]):
            cp.start()

    def finish(i, o, s):
        for cp in copies(i, o, *s[0]):
            cp.wait()

    return _Plan(ins, outs, [sems], start, finish)


def _pair_exchange_plan(gs):
    def copies(ins, outs, send, recv):
        x, y, c, _ = _place()
        cps = []
        for a in range(len(gs)):
            hr = ins[a].shape[1] // 2
            cps.append(_remote(ins[a].at[:, pl.ds((1 - c) * hr, hr)], outs[a], send.at[a], recv.at[a], (x, y, 1 - c)))
        return cps

    return _swap_plan(gs, [_sds((NCHIP, g.shape[1] // 2, g.shape[2]), g.dtype) for g in gs], (len(gs),), copies)


def _pair_add(g, got, cidx, name):
    _, r, cdim = g.shape
    hr = r // 2
    t = 256 if hr % 256 == 0 else 128
    nt = hr // t

    def body(c_ref, g_ref, o_ref, p_ref, pc_ref):
        sm = g_ref[...] + o_ref[...]
        p_ref[...] = sm
        pc_ref[...] = sm.astype(pc_ref.dtype)

    spec = pl.BlockSpec((1, t, cdim), lambda k, i, c_ref: (k, i, 0))
    return pl.pallas_call(
        body, name=name,
        grid_spec=pltpu.PrefetchScalarGridSpec(
            num_scalar_prefetch=1, grid=(NCHIP, nt),
            in_specs=[pl.BlockSpec((1, t, cdim), lambda k, i, c_ref: (k, c_ref[0] * nt + i, 0)), spec],
            out_specs=[spec, spec]),
        out_shape=[_sds((NCHIP, hr, cdim)), _sds((NCHIP, hr, cdim), COMM)],
        compiler_params=_cp("parallel", "parallel"),
    )(cidx, g, got)


def _chip_exchange_plan(ps):
    def copies(ins, outs, send, recv):
        _, _, c, chips = _place()
        return [_remote(ins[a].at[2 * cx + cy], outs[a].at[j], send.at[a, j], recv.at[a, j], (cx, cy, c))
                for a in range(len(ps)) for j, (cx, cy) in enumerate(chips)]

    return _swap_plan(ps, [_sds((NCHIP - 1,) + p.shape[1:], p.dtype) for p in ps], (len(ps), 3), copies)


def _shard_sum(p, got, kidx, name):
    _, hr, cdim = p.shape
    t = 256 if hr % 256 == 0 else 128

    def body(k_ref, p_ref, g_ref, o_ref):
        sm = p_ref[0]
        for j in range(NCHIP - 1):
            sm = sm + g_ref[j].astype(F32)
        o_ref[...] = sm

    return pl.pallas_call(
        body, name=name,
        grid_spec=pltpu.PrefetchScalarGridSpec(
            num_scalar_prefetch=1, grid=(hr // t,),
            in_specs=[pl.BlockSpec((1, t, cdim), lambda i, k_ref: (k_ref[0], i, 0)),
                      pl.BlockSpec((NCHIP - 1, t, cdim), lambda i, k_ref: (0, i, 0))],
            out_specs=pl.BlockSpec((t, cdim), lambda i, k_ref: (i, 0))),
        out_shape=_sds((hr, cdim)),
        compiler_params=_cp("parallel"),
    )(kidx, p, got)


def _pair_swap_plan(rs):
    def copies(ins, outs, send, recv):
        x, y, c, _ = _place()
        return [_remote(ins[a], outs[a], send.at[a], recv.at[a], (x, y, 1 - c)) for a in range(len(rs))]

    return _swap_plan(rs, [_sds(r.shape, r.dtype) for r in rs], (len(rs),), copies)


def _allgather8_plan(v):
    def pieces(ins, outs, send, recv):
        x, y, c, chips = _place()
        me, sibling = (x, y, c), (x, y, 1 - c)

        def copy(k, block, to, src=None):
            px, py, pc = block
            slot = outs[0].at[4 * px + 2 * py + pc]
            return _remote(slot if src is None else src, slot, send.at[k], recv.at[k], to)

        first = [copy(0, me, sibling, src=ins[0])] + [copy(1 + j, me, (*chip, c), src=ins[0])
                                                      for j, chip in enumerate(chips)]
        passed = [copy(4 + j, (*chip, c), sibling) for j, chip in enumerate(chips)]
        arrivals = [copy(1 + j, (*chip, c), me) for j, chip in enumerate(chips)]
        late = [copy(0, sibling, me)] + [copy(4 + j, (*chip, 1 - c), me) for j, chip in enumerate(chips)]
        return first, passed, arrivals, late

    def start(ins, outs, sems):
        for cp in pieces(ins, outs, *sems[0])[0]:
            cp.start()

    def finish(ins, outs, sems):
        first, passed, arrivals, late = pieces(ins, outs, *sems[0])
        for got, fwd in zip(arrivals, passed):
            got.wait_recv()
            fwd.start()
        for got in late:
            got.wait_recv()
        for cp in first + passed:
            cp.wait_send()

    return _Plan([v], [_sds((8,) + v.shape, v.dtype)], [(7,)], start, finish)


def _own_block(gathered, v):
    me = 4 * lax.axis_index("x") + 2 * lax.axis_index("y") + lax.axis_index("c")
    return lax.dynamic_update_index_in_dim(gathered, v, me, 0)


def _sum_devices(allv):
    _, r, _ = allv.shape

    def body(a_ref, o_ref):
        sm = a_ref[0]
        for d in range(1, 8):
            sm = sm + a_ref[d]
        o_ref[...] = sm

    return pl.pallas_call(
        body, name="small_sum", grid=(1,), in_specs=[_full_spec((8, r, 128))], out_specs=_full_spec((r, 128)),
        out_shape=_sds((r, 128)), compiler_params=_cp("arbitrary"),
    )(allv)


def _pack(arrs):
    flat = jnp.concatenate([a.reshape(-1) for a in arrs])
    return jnp.pad(flat, (0, (-flat.shape[0]) % 1024)).reshape(-1, 128)


def _unpack(packed, shapes):
    flat, outs, off = packed.reshape(-1), [], 0
    for shp in shapes:
        n = math.prod(shp)
        outs.append(flat[off:off + n].reshape(shp))
        off += n
    return outs


BIG = ("w_in", "w_out", "w_up", "w_down")
CONV = ("conv_ssm_w", "conv_lru_w")
WEIGHTS = ("norm_mix_pre", "w_in", "conv_ssm_w", "conv_ssm_b", "dt_bias", "a_log", "d_skip", "ssm_norm", "conv_lru_w",
           "conv_lru_b", "lru_wa", "lru_ba", "lru_wx", "lru_bx", "lru_lambda", "w_out", "norm_mix_post",
           "norm_mlp_pre", "w_up", "w_down", "norm_mlp_post")
SMALL = tuple(n for n in WEIGHTS if n not in BIG and n not in CONV)
EARLY = ("w_down", "w_up", "w_out")


def _cat_cols(g):
    return jnp.concatenate([g[k] for k in range(NCHIP)], axis=1)


class _Dist:
    def __init__(self, shards, cidx, kidx):
        self.shards, self.cidx, self.kidx = shards, cidx, kidx
        self.halves = {}

    def early_grads(self, w_down, w_up, w_out):
        self.shard_major = [w_down.reshape(NCHIP, D, D), jnp.stack([w_up[:, D * k:D * (k + 1)] for k in range(NCHIP)]),
                            w_out.reshape(NCHIP, D // NCHIP, D)]

    def late_grads(self, w_in):
        self.g_in = jnp.stack([w_in[:, W_IN_SHARD * k:W_IN_SHARD * (k + 1)] for k in range(NCHIP)])

    def plan(self, key):
        if key == "proj5":
            return _gather_plan([self.shards["w_out"], self.shards["w_up"]])
        if key == "lru_fwd":
            return _gather_plan([self.shards["w_down"]])
        if key == "lru_bwd":
            return _pair_exchange_plan(self.shard_major)
        if key == "ssd_bwd":
            return _chip_exchange_plan([pc for _, pc in self.pair])
        if key == "conv_bwd":
            return _pair_swap_plan(self.mine)
        if key == "du_norm":
            return _pair_exchange_plan([self.g_in])
        return None

    def done(self, key, got, p):
        if key == "proj5":
            g_out, g_up = _own_shards(got, [self.shards["w_out"], self.shards["w_up"]])
            p.update(w_out=g_out.reshape(D, D), w_up=_cat_cols(g_up))
        elif key == "lru_fwd":
            g_down, = _own_shards(got, [self.shards["w_down"]])
            p.update(w_down=g_down.reshape(DFF, D))
        elif key == "lru_bwd":
            self.pair = [_pair_add(gs, o, self.cidx, f"grad_pair_add_{n}")
                         for gs, o, n in zip(self.shard_major, got, EARLY)]
        elif key == "ssd_bwd":
            self.mine = [_shard_sum(pf, o, self.kidx, f"grad_shard_sum_{n}")
                         for (pf, _), o, n in zip(self.pair, got, EARLY)]
        elif key == "conv_bwd":
            self.halves = {n: (mine, other) for n, mine, other in zip(EARLY, self.mine, got)}
        elif key == "du_norm":
            self.got_in, = got


def kernel(x, norm_mix_pre, w_in, conv_ssm_w, conv_ssm_b, dt_bias, a_log, d_skip, ssm_norm, conv_lru_w, conv_lru_b, lru_wa, lru_ba, lru_wx, lru_bx, lru_lambda, w_out, norm_mix_post, norm_mlp_pre, w_up, w_down, norm_mlp_post, loss_target, m_norm_mix_pre, m_w_in, m_conv_ssm_w, m_conv_ssm_b, m_dt_bias, m_a_log, m_d_skip, m_ssm_norm, m_conv_lru_w, m_conv_lru_b, m_lru_wa, m_lru_ba, m_lru_wx, m_lru_bx, m_lru_lambda, m_w_out, m_norm_mix_post, m_norm_mlp_pre, m_w_up, m_w_down, m_norm_mlp_post, v_norm_mix_pre, v_w_in, v_conv_ssm_w, v_conv_ssm_b, v_dt_bias, v_a_log, v_d_skip, v_ssm_norm, v_conv_lru_w, v_conv_lru_b, v_lru_wa, v_lru_ba, v_lru_wx, v_lru_bx, v_lru_lambda, v_w_out, v_norm_mix_post, v_norm_mlp_pre, v_w_up, v_w_down, v_norm_mlp_post):
    args = locals()
    w = {n: args[n][0] for n in WEIGHTS}
    m = {n: args["m_" + n][0] for n in WEIGHTS}
    v = {n: args["v_" + n][0] for n in WEIGHTS}
    cidx = lax.axis_index("c").astype(jnp.int32).reshape(1)
    kchip = 2 * lax.axis_index("x") + lax.axis_index("y")
    dist = _Dist({n: w[n].astype(MXU) for n in BIG}, cidx, kchip.astype(jnp.int32).reshape(1))

    first = [dist.shards["w_in"], w["conv_ssm_w"], w["conv_lru_w"]]
    gathered = _own_shards(_run_plan(_gather_plan(first[:1], first[1:]), "gather_w_in"), first)
    w5, wxbc, wdt = _split_w_in(_cat_cols(gathered[0]))
    p = {n: (w[n].reshape(1, -1) if w[n].ndim == 1 else w[n]) for n in SMALL}
    p.update(w5=w5, wxbc=wxbc, wdt=wdt, conv_ssm_w=_cat_cols(gathered[1]), conv_lru_w=_cat_cols(gathered[2]))

    loss, grad_x, g = _local_step(x[0], loss_target[0], p, dist)

    small_shapes = [g[n].shape for n in SMALL + CONV] + [(1,)]
    packed_small = _pack([g[n] for n in SMALL + CONV] + [loss.reshape(1)])
    p_in, pc_in = _pair_add(dist.g_in, dist.got_in, cidx, "grad_pair_add_w_in")
    from_chips, all_small = _run_plan(_merge_plans(_chip_exchange_plan([pc_in]), _allgather8_plan(packed_small)),
                                      "grad_chip_exchange_w_in")
    half_in = _shard_sum(p_in, from_chips, dist.kidx, "grad_shard_sum_w_in")
    other_in, = _run_plan(_pair_swap_plan([half_in]), "grad_pair_swap_w_in")
    halves = dict(dist.halves, w_in=(half_in, other_in))

    reduced = {}
    *summed, loss = _unpack(_sum_devices(_own_block(all_small, packed_small)), small_shapes)
    loss = loss.reshape(())
    for n, s in zip(SMALL + CONV, summed):
        if n in CONV:
            width = w[n].shape[1]
            reduced[n] = lax.dynamic_slice_in_dim(s, kchip * width, width, axis=1)
        else:
            reduced[n] = s.reshape(w[n].shape)

    delta, new_m, new_v = {}, {}, {}
    for n in BIG:
        mine, other = halves[n]
        reduced[n], delta[n], new_m[n], new_v[n] = _adamw_halves(w[n], mine, other, m[n], v[n], cidx, f"adamw_{n}")
    for n in CONV:
        delta[n], new_m[n], new_v[n] = _adamw(w[n], reduced[n], m[n], v[n], f"adamw_{n}")
    shapes = [w[n].shape for n in SMALL]
    packed = [_pack([d[n] for n in SMALL]) for d in (w, reduced, m, v)]
    for d, out in zip((delta, new_m, new_v), _adamw(*packed, "adamw_small")):
        d.update(zip(SMALL, _unpack(out, shapes)))

    lead = lambda d: [d[n][None] for n in WEIGHTS]
    return (loss, grad_x[None], *lead(reduced), *lead(delta), *lead(new_m), *lead(new_v))
```

```python
import functools
import math

import jax
import jax.numpy as jnp
from jax import lax
from jax.experimental import pallas as pl
from jax.experimental.pallas import tpu as pltpu

F32 = jnp.float32
BF16 = jnp.bfloat16
MXU = BF16

D = 1024
DFF = 4096
NH = 16
HP = 64
NG = 2
NS = 128
CH = 128
XBC = D + 2 * NG * NS
GW = D // NG
LRU_C = 8.0
EPS = 1e-6
NCHIP = 4
W_IN_COLS = 6672
W_IN_SHARD = W_IN_COLS // NCHIP

ADAM_LR = 0.001
ADAM_B1 = 0.9
ADAM_B2 = 0.999
ADAM_EPS = 1e-08
ADAM_WD = 0.01
ADAM_STEP = 10

VMEM_LIMIT = 56 * 1024 * 1024
TK_GRAD = 2048
MESH = pl.DeviceIdType.MESH


def _cp(*sem):
    return pltpu.CompilerParams(dimension_semantics=sem, vmem_limit_bytes=VMEM_LIMIT)


def _dot(a, b, ca=1, cb=0, prec=None):
    return lax.dot_general(a, b, (((ca,), (cb,)), ((), ())), precision=prec, preferred_element_type=F32)


def _mdot(a, b, ca=1, cb=0):
    return _dot(a.astype(MXU), b.astype(MXU), ca, cb)


def _bf16_parts(v, n):
    parts = []
    for i in range(n):
        p = v.astype(BF16)
        parts.append(p)
        if i < n - 1:
            v = v - p.astype(F32)
    return parts


def _xdot(a, b, passes, split_b=False):
    if split_b:
        a16 = a.astype(BF16)
        terms = [_dot(a16, p) for p in _bf16_parts(b, passes)]
    else:
        b16 = b.astype(BF16)
        terms = [_dot(p, b16) for p in _bf16_parts(a, passes)]
    return functools.reduce(lambda u, v: u + v, terms)


def _sig(x):
    return 0.5 * jnp.tanh(0.5 * x) + 0.5


def _silu(x):
    return x * _sig(x)


def _dsilu(x):
    s = _sig(x)
    return s * (1.0 + x * (1.0 - s))


def _softplus(x):
    e = jnp.exp(-jnp.abs(x))
    return jnp.maximum(x, 0.0) + jnp.where(e < 1e-4, e * (1.0 - 0.5 * e), jnp.log(1.0 + e))


_GELU_C = math.sqrt(2.0 / math.pi)


def _gelu(x):
    t = jnp.tanh(_GELU_C * (x + 0.044715 * x * x * x))
    return 0.5 * x * (1.0 + t)


def _gelu_and_grad(x):
    x2 = x * x
    t = jnp.tanh(_GELU_C * (x + 0.044715 * x * x2))
    half = 0.5 * (1.0 + t)
    return x * half, half + 0.5 * x * (1.0 - t * t) * _GELU_C * (1.0 + 3.0 * 0.044715 * x2)


def _one_minus_sq(a, la):
    x = 2.0 * la
    series = -x * (1.0 + x * (0.5 + x * (1.0 / 6.0)))
    return jnp.where(x > -0.01, series, 1.0 - a * a)


def _rms(x):
    return lax.rsqrt(jnp.mean(x * x, axis=-1, keepdims=True) + EPS)


def _rms_bwd(x, r, g, dy):
    xn = x * r
    dxh = dy * g
    m = jnp.mean(dxh * xn, axis=-1, keepdims=True)
    return r * (dxh - xn * m), jnp.sum(dy * xn, axis=0, keepdims=True)


def _row_spec(t, c, col=0):
    return pl.BlockSpec((t, c), lambda i: (i, col))


def _rev_spec(t, c, n, col=0):
    return pl.BlockSpec((t, c), lambda i: (n - 1 - i, col))


def _full_spec(shape):
    nd = len(shape)
    return pl.BlockSpec(shape, lambda *_: (0,) * nd)


def _sds(shape, dtype=F32):
    return jax.ShapeDtypeStruct(shape, dtype)


ANY = pl.BlockSpec(memory_space=pl.ANY)


class _Plan:
    def __init__(self, ins, outs, sems, start, finish):
        self.ins, self.outs, self.sems, self.start, self.finish = list(ins), list(outs), list(sems), start, finish


def _merge_plans(*plans):
    def each(fn_name, ins, outs, sems):
        i = o = s = 0
        for p in plans:
            getattr(p, fn_name)(ins[i:i + len(p.ins)], outs[o:o + len(p.outs)], sems[s:s + len(p.sems)])
            i, o, s = i + len(p.ins), o + len(p.outs), s + len(p.sems)

    return _Plan([a for p in plans for a in p.ins], [a for p in plans for a in p.outs],
                 [a for p in plans for a in p.sems], functools.partial(each, "start"), functools.partial(each, "finish"))


def _pcall(body, args, *, name, grid, in_specs, out_specs, out_shape, sem, scratch_shapes=(), plan=None):
    single = not isinstance(out_shape, (list, tuple))
    out_specs = [out_specs] if single else list(out_specs)
    out_shape = [out_shape] if single else list(out_shape)
    if plan is None:
        outs = pl.pallas_call(body, name=name, grid=grid, in_specs=list(in_specs), out_specs=out_specs,
                              out_shape=out_shape, scratch_shapes=list(scratch_shapes),
                              compiler_params=_cp(*sem))(*args)
        return outs[0] if single else outs
    n_in, n_out, n_sc, ni, no = len(in_specs), len(out_shape), len(scratch_shapes), len(plan.ins), len(plan.outs)

    def hosted(*refs):
        b0 = n_in + ni
        b1 = b0 + n_out + no
        sem_refs = refs[b1 + n_sc:]
        sems = [(sem_refs[2 * q], sem_refs[2 * q + 1]) for q in range(len(plan.sems))]
        ids = [pl.program_id(d) for d in range(len(grid))]
        first = functools.reduce(jnp.logical_and, [i == 0 for i in ids])
        last = functools.reduce(jnp.logical_and, [i == g - 1 for i, g in zip(ids, grid)])

        @pl.when(first)
        def _():
            plan.start(refs[n_in:b0], refs[b0 + n_out:b1], sems)

        body(*refs[:n_in], *refs[b0:b0 + n_out], *refs[b1:b1 + n_sc])

        @pl.when(last)
        def _():
            plan.finish(refs[n_in:b0], refs[b0 + n_out:b1], sems)

    dma = [pltpu.SemaphoreType.DMA(shape) for shape in plan.sems for _ in range(2)]
    outs = pl.pallas_call(hosted, name=name, grid=grid, in_specs=list(in_specs) + [ANY] * ni,
                          out_specs=out_specs + [ANY] * no, out_shape=out_shape + plan.outs,
                          scratch_shapes=list(scratch_shapes) + dma,
                          compiler_params=_cp(*("arbitrary",) * len(grid)))(*args, *plan.ins)
    return (outs[0] if single else outs[:n_out]), outs[n_out:]


def _run_plan(plan, name):
    ni, no = len(plan.ins), len(plan.outs)

    def body(*refs):
        sem_refs = refs[ni + no:]
        sems = [(sem_refs[2 * q], sem_refs[2 * q + 1]) for q in range(len(plan.sems))]
        plan.start(refs[:ni], refs[ni:ni + no], sems)
        plan.finish(refs[:ni], refs[ni:ni + no], sems)

    return pl.pallas_call(
        body, name=name, in_specs=[ANY] * ni, out_specs=[ANY] * no, out_shape=plan.outs,
        scratch_shapes=[pltpu.SemaphoreType.DMA(shape) for shape in plan.sems for _ in range(2)],
    )(*plan.ins)


def _matmul(a, b, *, name, ta=False, tb=False, tm=512, tn=1024, tk=1024, out_dtype=F32, a_fn=None, epi=None,
            epi_args=(), plan=None):
    m, k = (a.shape[1], a.shape[0]) if ta else a.shape
    n = b.shape[0] if tb else b.shape[1]
    tm, tn, tk = min(tm, m), min(tn, n), min(tk, k)
    nk = k // tk
    a_spec = pl.BlockSpec((tk, tm), lambda i, j, kk: (kk, i)) if ta else pl.BlockSpec((tm, tk), lambda i, j, kk: (i, kk))
    b_spec = pl.BlockSpec((tn, tk), lambda i, j, kk: (j, kk)) if tb else pl.BlockSpec((tk, tn), lambda i, j, kk: (kk, j))
    e_specs = [pl.BlockSpec((tm, tn), lambda i, j, kk: (i, j)) for _ in epi_args]
    ne = len(epi_args)

    def body(a_ref, b_ref, *rest):
        e_refs, o_ref = rest[:ne], rest[ne]
        av = a_ref[...]
        if a_fn is not None:
            av = a_fn(av)
        part = _mdot(av, b_ref[...], 0 if ta else 1, 1 if tb else 0)

        def finish(r):
            if epi is not None:
                r = epi(r, *[e[...] for e in e_refs])
            o_ref[...] = r.astype(o_ref.dtype)

        if nk == 1:
            finish(part)
            return
        acc_ref = rest[ne + 1]
        kk = pl.program_id(2)

        @pl.when(kk == 0)
        def _():
            acc_ref[...] = part

        @pl.when(jnp.logical_and(kk > 0, kk < nk - 1))
        def _():
            acc_ref[...] += part

        @pl.when(kk == nk - 1)
        def _():
            finish(acc_ref[...] + part)

    return _pcall(
        body, (a, b, *epi_args), name=name, grid=(m // tm, n // tn, nk),
        in_specs=[a_spec, b_spec] + e_specs,
        out_specs=pl.BlockSpec((tm, tn), lambda i, j, kk: (i, j)),
        out_shape=_sds((m, n), out_dtype),
        scratch_shapes=[pltpu.VMEM((tm, tn), F32)] if nk > 1 else [],
        sem=("parallel", "parallel", "arbitrary"), plan=plan)


def _relu2(p):
    p = jnp.maximum(p.astype(F32), 0.0)
    return p * p


def _norm_cast(x, g, name):
    s = x.shape[0]
    t = min(512, s)

    def body(x_ref, g_ref, o_ref):
        xv = x_ref[...]
        o_ref[...] = (xv * _rms(xv) * g_ref[...]).astype(o_ref.dtype)

    return pl.pallas_call(
        body, name=name, grid=(s // t,), in_specs=[_row_spec(t, D), _full_spec((1, D))],
        out_specs=_row_spec(t, D), out_shape=_sds((s, D), MXU), compiler_params=_cp("parallel"),
    )(x, g)


def _conv_fwd(xbc_raw, proj5, dt_raw, cw_s, cb_s, cw_l, cb_l, dt_bias):
    s = xbc_raw.shape[0]
    t = min(256, s)

    def body(xs_ref, xl_ref, dtr_ref, cws_ref, cbs_ref, cwl_ref, cbl_ref, dtb_ref, xc_ref, dsl_ref, xr_ref, dt_ref,
             bs_ref, bl_ref):
        @pl.when(pl.program_id(0) == 0)
        def _():
            bs_ref[0:8, :] = jnp.zeros((8, XBC), F32)
            bl_ref[0:8, :] = jnp.zeros((8, D), F32)

        bs_ref[8:t + 8, :] = xs_ref[...]
        bl_ref[8:t + 8, :] = xl_ref[...]

        def conv(buf, w_ref, b_ref):
            acc = b_ref[...] + w_ref[3:4, :] * buf[8:t + 8, :]
            for k in (1, 2, 3):
                acc = acc + w_ref[3 - k:4 - k, :] * buf[8 - k:t + 8 - k, :]
            return acc

        pre = conv(bs_ref, cws_ref, cbs_ref)
        sg = _sig(pre)
        xc_ref[...] = pre * sg
        dsl_ref[...] = (sg * (1.0 + pre * (1.0 - sg))).astype(dsl_ref.dtype)
        xr_ref[...] = conv(bl_ref, cwl_ref, cbl_ref)
        dt_ref[...] = _softplus(dtr_ref[...] + dtb_ref[...])
        bs_ref[0:8, :] = bs_ref[t:t + 8, :]
        bl_ref[0:8, :] = bl_ref[t:t + 8, :]

    return pl.pallas_call(
        body, name="conv_fwd", grid=(s // t,),
        in_specs=[_row_spec(t, XBC), _row_spec(t, D, 2), _row_spec(t, 128), _full_spec((4, XBC)),
                  _full_spec((1, XBC)), _full_spec((4, D)), _full_spec((1, D)), _full_spec((1, 128))],
        out_specs=[_row_spec(t, XBC), _row_spec(t, XBC), _row_spec(t, D), _row_spec(t, 128)],
        out_shape=[_sds((s, XBC)), _sds((s, XBC), BF16), _sds((s, D)), _sds((s, 128))],
        scratch_shapes=[pltpu.VMEM((t + 8, XBC), F32), pltpu.VMEM((t + 8, D), F32)],
        compiler_params=_cp("arbitrary"),
    )(xbc_raw, proj5, dt_raw, cw_s, cb_s, cw_l, cb_l, dt_bias)


def _ssd_chunk_setup(dt_ref, alog_ref, e_ref, at_ref, dtt_ref):
    lane = lax.broadcasted_iota(jnp.int32, (CH, 128), 1)
    row = lax.broadcasted_iota(jnp.int32, (CH, 128), 0)
    lane1 = lax.broadcasted_iota(jnp.int32, (1, 128), 1)
    a = jnp.where(lane1 < NH, -jnp.exp(alog_ref[...]), 0.0)
    dtv = dt_ref[...]
    adt = dtv * a
    tril = row >= lane
    acum = _xdot(tril.astype(F32), adt, 3, split_b=True)
    alast = jnp.sum(adt, axis=0, keepdims=True)
    at_ref[...] = acum.T
    dtt_ref[...] = dtv.T
    e = e_ref[...]
    ea_x = _xdot(jnp.exp(acum), e, 2)
    ws = jnp.exp(alast - acum) * dtv
    ws_x = _xdot(ws, e, 2)
    eal = jnp.exp(alast)
    eal_x = jnp.max(_xdot(jnp.broadcast_to(eal, (8, 128)), e, 3), axis=0, keepdims=True)
    return dict(lane=lane, row=row, tril=tril, a=a, dtv=dtv, acum=acum, alast=alast, ea_x=ea_x, ws=ws, ws_x=ws_x,
                eal=eal, eal_x=eal_x)


def _head_decay(cs, at_ref, dtt_ref, h):
    col = jnp.sum(jnp.where(cs["lane"] == h, cs["acum"], 0.0), axis=1, keepdims=True)
    ld = jnp.where(cs["tril"], jnp.exp(jnp.minimum(col - at_ref[h:h + 1, :], 0.0)), 0.0)
    return ld, dtt_ref[h:h + 1, :]


def _ssd_fwd(xbc_c, dt, proj5, a_log, dskip_x, ssm_norm, expand):
    s = xbc_c.shape[0]
    nc = s // CH

    def body(xc_ref, dt_ref, z_ref, alog_ref, dsk_ref, ng_ref, e_ref, y_ref, ya_ref, st_ref, h_ref, at_ref, dtt_ref,
             yd_ref):
        @pl.when(pl.program_id(0) == 0)
        def _():
            h_ref[...] = jnp.zeros_like(h_ref)

        cs = _ssd_chunk_setup(dt_ref, alog_ref, e_ref, at_ref, dtt_ref)
        lane = cs["lane"]
        for g in range(NG):
            gs = slice(GW * g, GW * (g + 1))
            bg = xc_ref[:, D + NS * g:D + NS * (g + 1)]
            cg = xc_ref[:, D + NG * NS + NS * g:D + NG * NS + NS * (g + 1)]
            cb = _mdot(cg, bg, 1, 1)
            for j in range(4 * g, 4 * g + 4):
                ps = slice(128 * j, 128 * (j + 1))
                xp = xc_ref[:, ps]
                acc = jnp.zeros((CH, 128), F32)
                for hf in range(2):
                    ld, rowdt = _head_decay(cs, at_ref, dtt_ref, 2 * j + hf)
                    hm = (lane >= HP) if hf else (lane < HP)
                    acc = acc + _mdot(cb * ld * rowdt, jnp.where(hm, xp, 0.0))
                yd_ref[:, ps] = acc
            hg = h_ref[:, gs]
            yd_ref[:, gs] += _mdot(cg, hg) * cs["ea_x"][:, gs]
            st = _mdot(bg, xc_ref[:, gs] * cs["ws_x"][:, gs], 0, 0)
            st_ref[0, :, gs] = hg
            h_ref[:, gs] = cs["eal_x"][:, gs] * hg + st
        y = yd_ref[...] + dsk_ref[...] * xc_ref[:, 0:D]
        y_ref[...] = y
        yg = y * _silu(z_ref[...])
        for g in range(NG):
            gs = slice(GW * g, GW * (g + 1))
            seg = yg[:, gs]
            ya_ref[:, gs] = seg * _rms(seg) * ng_ref[:, gs]

    return pl.pallas_call(
        body, name="ssd_fwd", grid=(nc,),
        in_specs=[_row_spec(CH, XBC), _row_spec(CH, 128), _row_spec(CH, D, 0), _full_spec((1, 128)),
                  _full_spec((1, D)), _full_spec((1, D)), _full_spec((128, D))],
        out_specs=[_row_spec(CH, D), _row_spec(CH, D), pl.BlockSpec((1, NS, D), lambda i: (i, 0, 0))],
        out_shape=[_sds((s, D)), _sds((s, D)), _sds((nc, NS, D))],
        scratch_shapes=[pltpu.VMEM((NS, D), F32), pltpu.VMEM((128, 128), F32), pltpu.VMEM((128, 128), F32),
                        pltpu.VMEM((CH, D), F32)],
        compiler_params=_cp("arbitrary"),
    )(xbc_c, dt, proj5, a_log, dskip_x, ssm_norm, expand)


def _lru_gates(xr, wa_ref, wx_ref, ba_ref, bx_ref, lam_ref):
    gr = _sig(_mdot(xr, wa_ref[...]) + ba_ref[...])
    gi = _sig(_mdot(xr, wx_ref[...]) + bx_ref[...])
    sp = _softplus(-lam_ref[...])
    la = -LRU_C * gr * sp
    a = jnp.exp(la)
    oms = _one_minus_sq(a, la)
    inv_mult = lax.rsqrt(oms)
    return gr, gi, sp, a, oms * inv_mult, inv_mult


def _blocked_scan(a, u, carry_ref, a_ref, u_ref, c_ref, out_ref, reverse):
    t = a.shape[0]
    ns = t // 8

    def combine(av, uv, idx, n, sh):
        m = (idx < n - sh) if reverse else (idx >= sh)
        by = n - sh if reverse else sh
        return jnp.where(m, av * pltpu.roll(av, by, 0), av), jnp.where(m, uv + av * pltpu.roll(uv, by, 0), uv)

    row = lax.broadcasted_iota(jnp.int32, (t, D), 0)
    rin = jnp.bitwise_and(row, 7)
    for sh in (1, 2, 4):
        m = (rin < 8 - sh) if reverse else (rin >= sh)
        by = t - sh if reverse else sh
        a, u = jnp.where(m, a * pltpu.roll(a, by, 0), a), jnp.where(m, u + a * pltpu.roll(u, by, 0), u)
    a_ref[...] = a
    u_ref[...] = u
    edge = 0 if reverse else 7
    for j in range(ns):
        c_ref[j:j + 1, :] = a_ref[8 * j + edge:8 * j + edge + 1, :]
    at = c_ref[...]
    for j in range(ns):
        c_ref[j:j + 1, :] = u_ref[8 * j + edge:8 * j + edge + 1, :]
    ut = c_ref[...]
    srow = lax.broadcasted_iota(jnp.int32, (ns, D), 0)
    sh = 1
    while sh < ns:
        at, ut = combine(at, ut, srow, ns, sh)
        sh *= 2
    cv = carry_ref[0:1, :]
    ends = ut + at * cv
    last = 0 if reverse else ns - 1
    first = ns - 1 if reverse else 0
    c_ref[...] = jnp.where(srow == first, cv, pltpu.roll(ends, first if reverse else 1, 0))
    carry_ref[0:1, :] = jnp.sum(jnp.where(srow == last, ends, 0.0), axis=0, keepdims=True)
    for j in range(ns):
        sl = slice(8 * j, 8 * j + 8)
        out_ref[sl, :] = u_ref[sl, :] + a_ref[sl, :] * c_ref[j:j + 1, :]


def _lru_fwd(xr, proj5, ya, wa_bd, wx_bd, ba, bx, lam, plan=None):
    s = xr.shape[0]
    t = min(256, s)

    def body(xr_ref, g_ref, ga_ref, gb_ref, ya_ref, wa_ref, wx_ref, ba_ref, bx_ref, lam_ref, h_ref, mg_ref, hc_ref,
             a_ref, u_ref, c_ref):
        @pl.when(pl.program_id(0) == 0)
        def _():
            hc_ref[...] = jnp.zeros_like(hc_ref)

        xrv = xr_ref[...]
        _, gi, _, a, mult, _ = _lru_gates(xrv, wa_ref, wx_ref, ba_ref, bx_ref, lam_ref)
        _blocked_scan(a, mult * gi * xrv, hc_ref, a_ref, u_ref, c_ref, h_ref, reverse=False)
        yb = h_ref[...] * _gelu(g_ref[...])
        mg_ref[...] = (_sig(ga_ref[...]) * ya_ref[...] + _sig(gb_ref[...]) * yb).astype(mg_ref.dtype)

    return _pcall(
        body, (xr, proj5, proj5, proj5, ya, wa_bd, wx_bd, ba, bx, lam), name="lru_fwd", grid=(s // t,),
        in_specs=[_row_spec(t, D), _row_spec(t, D, 1), _row_spec(t, D, 3), _row_spec(t, D, 4), _row_spec(t, D),
                  _full_spec((D, D)), _full_spec((D, D)), _full_spec((1, D)), _full_spec((1, D)), _full_spec((1, D))],
        out_specs=[_row_spec(t, D), _row_spec(t, D)],
        out_shape=[_sds((s, D)), _sds((s, D), MXU)],
        scratch_shapes=[pltpu.VMEM((8, D), F32), pltpu.VMEM((t, D), F32), pltpu.VMEM((t, D), F32),
                        pltpu.VMEM((t // 8, D), F32)],
        sem=("arbitrary",), plan=plan)


def _out_proj(merged, w_out, x, g2, g3):
    s = x.shape[0]
    t = min(256, s)

    def body(mg_ref, w_ref, x_ref, g2_ref, g3_ref, mix_ref, h1_ref, v_ref):
        mix = _mdot(mg_ref[...], w_ref[...])
        mix_ref[...] = mix
        h1 = x_ref[...] + mix * _rms(mix) * g2_ref[...]
        h1_ref[...] = h1
        v_ref[...] = (h1 * _rms(h1) * g3_ref[...]).astype(v_ref.dtype)

    return pl.pallas_call(
        body, name="out_proj", grid=(s // t,),
        in_specs=[_row_spec(t, D), _full_spec((D, D)), _row_spec(t, D), _full_spec((1, D)), _full_spec((1, D))],
        out_specs=[_row_spec(t, D), _row_spec(t, D), _row_spec(t, D)],
        out_shape=[_sds((s, D)), _sds((s, D)), _sds((s, D), MXU)],
        compiler_params=_cp("parallel"),
    )(merged, w_out, x, g2, g3)


def _down_loss(pre, w_down, h1, target, g4):
    s = pre.shape[0]
    t = min(256, s)

    def body(pre_ref, w_ref, h1_ref, tg_ref, g4_ref, dout_ref, dff_ref, loss_ref, dg4_ref):
        @pl.when(pl.program_id(0) == 0)
        def _():
            loss_ref[...] = jnp.zeros_like(loss_ref)
            dg4_ref[...] = jnp.zeros_like(dg4_ref)

        ff = _mdot(_relu2(pre_ref[...]), w_ref[...])
        r4 = _rms(ff)
        g4v = g4_ref[...]
        diff = h1_ref[...] + ff * r4 * g4v - tg_ref[...]
        sq = jnp.sum(jnp.sum(diff * diff, axis=1, keepdims=True), axis=0, keepdims=True)
        loss_ref[...] += (0.5 / D) * sq
        dout = diff * (1.0 / D)
        dout_ref[...] = dout
        dff, dg = _rms_bwd(ff, r4, g4v, dout)
        dff_ref[...] = dff.astype(dff_ref.dtype)
        dg4_ref[...] += dg

    return pl.pallas_call(
        body, name="down_loss", grid=(s // t,),
        in_specs=[_row_spec(t, DFF), _full_spec((DFF, D)), _row_spec(t, D), _row_spec(t, D), _full_spec((1, D))],
        out_specs=[_row_spec(t, D), _row_spec(t, D), _full_spec((1, 128)), _full_spec((1, D))],
        out_shape=[_sds((s, D)), _sds((s, D), MXU), _sds((1, 128)), _sds((1, D))],
        compiler_params=_cp("arbitrary"),
    )(pre, w_down, h1, target, g4)


def _dv_norms(dpre, w_up, h1, mix, dout, g3, g2):
    s = h1.shape[0]
    t = min(256, s)

    def body(dp_ref, w_ref, h1_ref, mix_ref, dout_ref, g3_ref, g2_ref, dh1_ref, dmix_ref, dg3_ref, dg2_ref):
        @pl.when(pl.program_id(0) == 0)
        def _():
            dg3_ref[...] = jnp.zeros_like(dg3_ref)
            dg2_ref[...] = jnp.zeros_like(dg2_ref)

        dv = _mdot(dp_ref[...], w_ref[...], 1, 1)
        h1 = h1_ref[...]
        dh1n, dg3 = _rms_bwd(h1, _rms(h1), g3_ref[...], dv)
        dh1 = dout_ref[...] + dh1n
        dh1_ref[...] = dh1
        mix = mix_ref[...]
        dmix, dg2 = _rms_bwd(mix, _rms(mix), g2_ref[...], dh1)
        dmix_ref[...] = dmix.astype(dmix_ref.dtype)
        dg3_ref[...] += dg3
        dg2_ref[...] += dg2

    return pl.pallas_call(
        body, name="dv_norms", grid=(s // t,),
        in_specs=[_row_spec(t, DFF), _full_spec((D, DFF)), _row_spec(t, D), _row_spec(t, D), _row_spec(t, D),
                  _full_spec((1, D)), _full_spec((1, D))],
        out_specs=[_row_spec(t, D), _row_spec(t, D), _full_spec((1, D)), _full_spec((1, D))],
        out_shape=[_sds((s, D)), _sds((s, D), MXU), _sds((1, D)), _sds((1, D))],
        compiler_params=_cp("arbitrary"),
    )(dpre, w_up, h1, mix, dout, g3, g2)


def _lru_bwd(dmerged, ya, xr, h, proj5, wa_bd, wx_bd, ba, bx, lam, plan=None):
    s = xr.shape[0]
    t = min(128, s)
    n = s // t
    rs = functools.partial(_rev_spec, t, D, n)

    def body(dm_ref, ya_ref, xr_ref, h_ref, hp_ref, g_ref, ga_ref, gb_ref, wa_ref, wx_ref, ba_ref, bx_ref, lam_ref,
             dya_ref, dga_ref, dgb_ref, dg_ref, dxr_ref, dpr_ref, dpi_ref, dlam_ref, dba_ref, dbx_ref, gc_ref,
             af_ref, an_ref, us_ref, c_ref, gs_ref):
        i = pl.program_id(0)

        @pl.when(i == 0)
        def _():
            gc_ref[...] = jnp.zeros_like(gc_ref)
            af_ref[...] = jnp.zeros_like(af_ref)
            dlam_ref[...] = jnp.zeros_like(dlam_ref)
            dba_ref[...] = jnp.zeros_like(dba_ref)
            dbx_ref[...] = jnp.zeros_like(dbx_ref)

        xrv = xr_ref[...]
        gr, gi, sp, a, mult, inv_mult = _lru_gates(xrv, wa_ref, wx_ref, ba_ref, bx_ref, lam_ref)
        hv = h_ref[...]
        dm = dm_ref[...]
        sa = _sig(ga_ref[...])
        sb = _sig(gb_ref[...])
        gel, dgel = _gelu_and_grad(g_ref[...])
        dya = dm * sa
        dya_ref[...] = dya
        dga_ref[...] = (dya * ya_ref[...] * (1.0 - sa)).astype(dga_ref.dtype)
        dyb = dm * sb
        dybh = dyb * hv
        dgb_ref[...] = (dybh * gel * (1.0 - sb)).astype(dgb_ref.dtype)
        dg_ref[...] = (dybh * dgel).astype(dg_ref.dtype)
        row = lax.broadcasted_iota(jnp.int32, (t, D), 0)
        an = jnp.where(row == t - 1, af_ref[0:1, :], pltpu.roll(a, t - 1, 0))
        _blocked_scan(an, dyb * gel, gc_ref, an_ref, us_ref, c_ref, gs_ref, reverse=True)
        gfull = gs_ref[...]
        af_ref[0:1, :] = jnp.sum(jnp.where(row == 0, a, 0.0), axis=0, keepdims=True)
        hlast = jnp.where(i == n - 1, 0.0, hp_ref[7:8, :])
        hprev = jnp.where(row == 0, hlast, pltpu.roll(hv, 1, 0))
        gx = gfull * xrv
        dgi = gx * mult
        dla = a * (gfull * hprev - gx * gi * a * inv_mult)
        dgr = dla * (-LRU_C * sp)
        dsp = jnp.sum(dla * (-LRU_C * gr), axis=0, keepdims=True)
        dlam_ref[...] += dsp * (-_sig(-lam_ref[...]))
        dpr = dgr * gr * (1.0 - gr)
        dpi = dgi * gi * (1.0 - gi)
        dpr_ref[...] = dpr.astype(dpr_ref.dtype)
        dpi_ref[...] = dpi.astype(dpi_ref.dtype)
        dba_ref[...] += jnp.sum(dpr, axis=0, keepdims=True)
        dbx_ref[...] += jnp.sum(dpi, axis=0, keepdims=True)
        dxr_ref[...] = gfull * mult * gi + _mdot(dpr, wa_ref[...], 1, 1) + _mdot(dpi, wx_ref[...], 1, 1)

    hp_spec = pl.BlockSpec((8, D), lambda i: (jnp.maximum((n - 1 - i) * (t // 8) - 1, 0), 0))
    return _pcall(
        body, (dmerged, ya, xr, h, h, proj5, proj5, proj5, wa_bd, wx_bd, ba, bx, lam), name="lru_bwd", grid=(n,),
        in_specs=[rs(), rs(), rs(), rs(), hp_spec, rs(1), rs(3), rs(4), _full_spec((D, D)), _full_spec((D, D)),
                  _full_spec((1, D)), _full_spec((1, D)), _full_spec((1, D))],
        out_specs=[rs(), rs(), rs(), rs(), rs(), rs(), rs(), _full_spec((1, D)), _full_spec((1, D)),
                   _full_spec((1, D))],
        out_shape=[_sds((s, D)), _sds((s, D), MXU), _sds((s, D), MXU), _sds((s, D), MXU), _sds((s, D)),
                   _sds((s, D), MXU), _sds((s, D), MXU), _sds((1, D)), _sds((1, D)), _sds((1, D))],
        scratch_shapes=[pltpu.VMEM((8, D), F32), pltpu.VMEM((8, D), F32), pltpu.VMEM((t, D), F32),
                        pltpu.VMEM((t, D), F32), pltpu.VMEM((t // 8, D), F32), pltpu.VMEM((t, D), F32)],
        sem=("arbitrary",), plan=plan)


def _ssd_bwd(dya, y, proj5, xbc_c, dt, states, a_log, dskip_x, ssm_norm, expand, reduce_, plan=None):
    s = xbc_c.shape[0]
    nc = s // CH
    rv = functools.partial(_rev_spec, CH)

    def body(dya_ref, y_ref, z_ref, xc_ref, dt_ref, st_ref, alog_ref, dsk_ref, ng_ref, e_ref, et_ref, dz_ref,
             dxc_ref, ddt_ref, dng_ref, ddsk_ref, dalog_ref, dh_ref, at_ref, dtt_ref, dat_ref, ddtt_ref, dy_ref,
             yoffdy_ref, xbds_ref):
        @pl.when(pl.program_id(0) == 0)
        def _():
            dh_ref[...] = jnp.zeros_like(dh_ref)
            dng_ref[...] = jnp.zeros_like(dng_ref)
            ddsk_ref[...] = jnp.zeros_like(ddsk_ref)
            dalog_ref[...] = jnp.zeros_like(dalog_ref)

        cs = _ssd_chunk_setup(dt_ref, alog_ref, e_ref, at_ref, dtt_ref)
        lane, row = cs["lane"], cs["row"]
        et = et_ref[...]
        for g in range(NG):
            gs = slice(GW * g, GW * (g + 1))
            yv = y_ref[:, gs]
            zv = z_ref[:, gs]
            sz = _silu(zv)
            yg = yv * sz
            dyav = dya_ref[:, gs]
            dyg, dng = _rms_bwd(yg, _rms(yg), ng_ref[:, gs], dyav)
            dng_ref[:, gs] += dng
            dy_ref[:, gs] = dyg * sz
            dz_ref[:, gs] = (dyg * yv * _dsilu(zv)).astype(dz_ref.dtype)
        dyv = dy_ref[...]
        xs = xc_ref[:, 0:D]
        ddsk_ref[...] += jnp.sum(dyv * xs, axis=0, keepdims=True)
        dxc_ref[:, 0:D] = dyv * dsk_ref[...]
        dat_ref[...] = jnp.zeros_like(dat_ref)
        ddtt_ref[...] = jnp.zeros_like(ddtt_ref)
        hh = jnp.sum(dh_ref[...] * st_ref[0], axis=0, keepdims=True)
        deal = jnp.max(_xdot(jnp.broadcast_to(hh, (8, D)), et, 3), axis=0, keepdims=True)
        d_acum = jnp.zeros((CH, 128), F32)
        for g in range(NG):
            gs = slice(GW * g, GW * (g + 1))
            bs_ = slice(D + NS * g, D + NS * (g + 1))
            cs_ = slice(D + NG * NS + NS * g, D + NG * NS + NS * (g + 1))
            bg = xc_ref[:, bs_]
            cg = xc_ref[:, cs_]
            cb = _mdot(cg, bg, 1, 1)
            hg = st_ref[0, :, gs]
            dhg = dh_ref[:, gs]
            dyg_ = dy_ref[:, gs]
            xsg = xc_ref[:, gs]
            ea = cs["ea_x"][:, gs]
            wsx = cs["ws_x"][:, gs]
            dp = dyg_ * ea
            yoffdy_ref[:, gs] = dp * _mdot(cg, hg)
            dc = _mdot(dp, hg, 1, 1)
            dhprev = _mdot(cg, dp, 0, 0)
            bds = _mdot(bg, dhg)
            dxc_ref[:, gs] += wsx * bds
            xbds_ref[:, gs] = xsg * bds
            db = _mdot(xsg * wsx, dhg, 1, 1)
            dh_ref[:, gs] = dhprev + cs["eal_x"][:, gs] * dhg
            dcbs = jnp.zeros((CH, CH), F32)
            for j in range(4 * g, 4 * g + 4):
                ps = slice(128 * j, 128 * (j + 1))
                xp = xc_ref[:, ps]
                dyp = dy_ref[:, ps]
                dxacc = jnp.zeros((CH, 128), F32)
                for hf in range(2):
                    hd = 2 * j + hf
                    ld, rowdt = _head_decay(cs, at_ref, dtt_ref, hd)
                    hm = (lane >= HP) if hf else (lane < HP)
                    dym = jnp.where(hm, dyp, 0.0)
                    w = cb * ld * rowdt
                    dw = _mdot(dym, jnp.where(hm, xp, 0.0), 1, 1)
                    dxacc = dxacc + _mdot(w, dym, 0, 0)
                    nm = dw * w
                    ddtt_ref[hd:hd + 1, :] += jnp.sum(dw * cb * ld, axis=0, keepdims=True)
                    d_acum = d_acum + jnp.where(lane == hd, jnp.sum(nm, axis=1, keepdims=True), 0.0)
                    dat_ref[hd:hd + 1, :] -= jnp.sum(nm, axis=0, keepdims=True)
                    dcbs = dcbs + dw * ld * rowdt
                dxc_ref[:, ps] += dxacc
            dxc_ref[:, bs_] = db + _mdot(dcbs, cg, 0, 0)
            dxc_ref[:, cs_] = dc + _mdot(dcbs, bg)
        dws = _xdot(xbds_ref[...], et, 2)
        ws = cs["ws"]
        d_acum = d_acum - dws * ws + _xdot(yoffdy_ref[...], et, 2) + dat_ref[...].T
        d_alast = jnp.sum(dws * ws, axis=0, keepdims=True) + deal * cs["eal"]
        d_acum = d_acum + jnp.where(row == CH - 1, d_alast, 0.0)
        triu = row <= lane
        d_adt = _xdot(triu.astype(F32), d_acum, 3, split_b=True)
        ddt_ref[...] = dws * jnp.exp(cs["alast"] - cs["acum"]) + ddtt_ref[...].T + d_adt * cs["a"]
        dalog_ref[...] += jnp.sum(d_adt * cs["dtv"], axis=0, keepdims=True) * cs["a"]

    return _pcall(
        body, (dya, y, proj5, xbc_c, dt, states, a_log, dskip_x, ssm_norm, expand, reduce_), name="ssd_bwd",
        grid=(nc,),
        in_specs=[rv(D, nc), rv(D, nc), rv(D, nc, 0), rv(XBC, nc), rv(128, nc),
                  pl.BlockSpec((1, NS, D), lambda i: (nc - 1 - i, 0, 0)), _full_spec((1, 128)), _full_spec((1, D)),
                  _full_spec((1, D)), _full_spec((128, D)), _full_spec((D, 128))],
        out_specs=[rv(D, nc), rv(XBC, nc), rv(128, nc), _full_spec((1, D)), _full_spec((1, D)),
                   _full_spec((1, 128))],
        out_shape=[_sds((s, D), MXU), _sds((s, XBC)), _sds((s, 128)), _sds((1, D)), _sds((1, D)), _sds((1, 128))],
        scratch_shapes=[pltpu.VMEM((NS, D), F32), pltpu.VMEM((128, 128), F32), pltpu.VMEM((128, 128), F32),
                        pltpu.VMEM((128, 128), F32), pltpu.VMEM((128, 128), F32), pltpu.VMEM((CH, D), F32),
                        pltpu.VMEM((CH, D), F32), pltpu.VMEM((CH, D), F32)],
        sem=("arbitrary",), plan=plan)


def _conv_bwd(dxbc_c, dsilu, dxr, ddt, xbc_raw, proj5, dt_raw, cw_s, cw_l, dt_bias, plan=None):
    s = xbc_raw.shape[0]
    t = min(256, s)
    n = s // t

    def body(dxc_ref, dsl_ref, dxr_ref, ddt_ref, xs_ref, xl_ref, dtr_ref, cws_ref, cwl_ref, dtb_ref, dxs_ref,
             dxl_ref, ddtr_ref, dcws_ref, dcbs_ref, dcwl_ref, dcbl_ref, ddtb_ref, ds_ref, dl_ref):
        @pl.when(pl.program_id(0) == 0)
        def _():
            ds_ref[t:t + 8, :] = jnp.zeros((8, XBC), F32)
            dl_ref[t:t + 8, :] = jnp.zeros((8, D), F32)
            for r in (dcws_ref, dcbs_ref, dcwl_ref, dcbl_ref, ddtb_ref):
                r[...] = jnp.zeros_like(r)

        ds_ref[0:t, :] = dxc_ref[...] * dsl_ref[...].astype(F32)
        dl_ref[0:t, :] = dxr_ref[...]

        def back(dbuf, x_ref, w_ref, dx_ref, dw_ref, db_ref):
            xv = x_ref[...]
            dpre = dbuf[0:t, :]
            dx = w_ref[3:4, :] * dpre
            dw_ref[3:4, :] += jnp.sum(dpre * xv, axis=0, keepdims=True)
            db_ref[...] += jnp.sum(dpre, axis=0, keepdims=True)
            for k in (1, 2, 3):
                ahead = dbuf[k:t + k, :]
                dx = dx + w_ref[3 - k:4 - k, :] * ahead
                dw_ref[3 - k:4 - k, :] += jnp.sum(ahead * xv, axis=0, keepdims=True)
            dx_ref[...] = dx.astype(dx_ref.dtype)
            dbuf[t:t + 8, :] = dbuf[0:8, :]

        back(ds_ref, xs_ref, cws_ref, dxs_ref, dcws_ref, dcbs_ref)
        back(dl_ref, xl_ref, cwl_ref, dxl_ref, dcwl_ref, dcbl_ref)
        ddtr = ddt_ref[...] * _sig(dtr_ref[...] + dtb_ref[...])
        ddtr_ref[...] = ddtr.astype(ddtr_ref.dtype)
        ddtb_ref[...] += jnp.sum(ddtr, axis=0, keepdims=True)

    rv = functools.partial(_rev_spec, t)
    return _pcall(
        body, (dxbc_c, dsilu, dxr, ddt, xbc_raw, proj5, dt_raw, cw_s, cw_l, dt_bias), name="conv_bwd", grid=(n,),
        in_specs=[rv(XBC, n), rv(XBC, n), rv(D, n), rv(128, n), rv(XBC, n), rv(D, n, 2), rv(128, n),
                  _full_spec((4, XBC)), _full_spec((4, D)), _full_spec((1, 128))],
        out_specs=[rv(XBC, n), rv(D, n), rv(128, n), _full_spec((4, XBC)), _full_spec((1, XBC)), _full_spec((4, D)),
                   _full_spec((1, D)), _full_spec((1, 128))],
        out_shape=[_sds((s, XBC), MXU), _sds((s, D), MXU), _sds((s, 128), MXU), _sds((4, XBC)), _sds((1, XBC)),
                   _sds((4, D)), _sds((1, D)), _sds((1, 128))],
        scratch_shapes=[pltpu.VMEM((t + 8, XBC), F32), pltpu.VMEM((t + 8, D), F32)],
        sem=("arbitrary",), plan=plan)


def _du_norm(pieces5, dxbc, ddtr, w5, wxbc, wdt, x, dh1, g1, plan=None):
    s = x.shape[0]
    t = min(256, s)

    def body(p0, p1, p2, p3, p4, dxbc_ref, ddtr_ref, w5_ref, wx_ref, wd_ref, x_ref, dh1_ref, g1_ref, dx_ref, dg1_ref):
        @pl.when(pl.program_id(0) == 0)
        def _():
            dg1_ref[...] = jnp.zeros_like(dg1_ref)

        du = _mdot(dxbc_ref[...], wx_ref[...], 1, 1) + _mdot(ddtr_ref[...], wd_ref[...], 1, 1)
        for b, p in enumerate((p0, p1, p2, p3, p4)):
            du = du + _mdot(p[...], w5_ref[:, D * b:D * (b + 1)], 1, 1)
        xv = x_ref[...]
        dxn, dg1 = _rms_bwd(xv, _rms(xv), g1_ref[...], du)
        dx_ref[...] = dh1_ref[...] + dxn
        dg1_ref[...] += dg1

    return _pcall(
        body, (*pieces5, dxbc, ddtr, w5, wxbc, wdt, x, dh1, g1), name="du_norm", grid=(s // t,),
        in_specs=[_row_spec(t, D)] * 5 + [_row_spec(t, XBC), _row_spec(t, 128), _full_spec((D, 5 * D)),
                                          _full_spec((D, XBC)), _full_spec((D, 128)), _row_spec(t, D),
                                          _row_spec(t, D), _full_spec((1, D))],
        out_specs=[_row_spec(t, D), _full_spec((1, D))],
        out_shape=[_sds((s, D)), _sds((1, D))],
        sem=("arbitrary",), plan=plan)


def _adamw(w, g, m, v, name):
    r, c = w.shape
    t = r
    if r * c > 256 * 1024:
        t = next(cand for cand in (512, 256, 128, 64, 32, 16, 8) if r % cand == 0 and cand * c <= 512 * 1024)
    bc1 = 1.0 - ADAM_B1 ** ADAM_STEP
    bc2 = 1.0 - ADAM_B2 ** ADAM_STEP

    def body(w_ref, g_ref, m_ref, v_ref, d_ref, nm_ref, nv_ref):
        gv = g_ref[...]
        nm = ADAM_B1 * m_ref[...] + (1.0 - ADAM_B1) * gv
        nv = ADAM_B2 * v_ref[...] + (1.0 - ADAM_B2) * (gv * gv)
        nm_ref[...] = nm
        nv_ref[...] = nv
        d_ref[...] = -ADAM_LR * ((nm / bc1) / (jnp.sqrt(nv / bc2) + ADAM_EPS) + ADAM_WD * w_ref[...])

    spec = pl.BlockSpec((t, c), lambda i: (i, 0))
    return pl.pallas_call(
        body, name=name, grid=(r // t,), in_specs=[spec] * 4, out_specs=[spec] * 3,
        out_shape=[_sds((r, c))] * 3, compiler_params=_cp("parallel"),
    )(w, g, m, v)


def _half_blocks(shape, axis):
    r, c = shape
    if axis == 0:
        t = 256 if (r // 2) % 256 == 0 else 128
        nb = (r // 2) // t
        return (t, c), nb, (lambda i: (i, 0)), (lambda i: (i % nb, 0))
    nb = (c // 2) // 128
    return (r, 128), nb, (lambda i: (0, i)), (lambda i: (0, i % nb))


def _adamw_halves(w, g_mine, g_other, m, v, cidx, name, axis=0):
    r, c = w.shape
    blk, nb, whole, part = _half_blocks(w.shape, axis)
    bc1 = 1.0 - ADAM_B1 ** ADAM_STEP
    bc2 = 1.0 - ADAM_B2 ** ADAM_STEP

    def body(c_ref, w_ref, gm_ref, go_ref, m_ref, v_ref, g_ref, d_ref, nm_ref, nv_ref):
        mine = (pl.program_id(0) // nb) == c_ref[0]
        gv = jnp.where(mine, gm_ref[...], go_ref[...])
        g_ref[...] = gv
        nm = ADAM_B1 * m_ref[...] + (1.0 - ADAM_B1) * gv
        nv = ADAM_B2 * v_ref[...] + (1.0 - ADAM_B2) * (gv * gv)
        nm_ref[...] = nm
        nv_ref[...] = nv
        d_ref[...] = -ADAM_LR * ((nm / bc1) / (jnp.sqrt(nv / bc2) + ADAM_EPS) + ADAM_WD * w_ref[...])

    spec = pl.BlockSpec(blk, lambda i, c_ref: whole(i))
    half = pl.BlockSpec(blk, lambda i, c_ref: part(i))
    return pl.pallas_call(
        body, name=name,
        grid_spec=pltpu.PrefetchScalarGridSpec(num_scalar_prefetch=1, grid=(2 * nb,),
                                               in_specs=[spec, half, half, spec, spec], out_specs=[spec] * 4),
        out_shape=[_sds((r, c))] * 4, compiler_params=_cp("parallel"),
    )(cidx, w, g_mine, g_other, m, v)


def _block_diag(w):
    eye = jnp.eye(NH, dtype=w.dtype)
    return (w[:, :, None, :] * eye[:, None, :, None]).reshape(D, D)


def _diag_blocks(full):
    eye = jnp.eye(NH, dtype=full.dtype)
    return (full.reshape(NH, HP, NH, HP) * eye[:, None, :, None]).sum(axis=2)


def _pad_lanes(v, n=128):
    return jnp.pad(v, ((0, 0), (0, n - v.shape[1])))


def _local_step(x, target, p, dist=None):
    heads = jnp.arange(D, dtype=jnp.int32) // HP
    expand = (jnp.arange(128, dtype=jnp.int32)[:, None] == heads[None, :]).astype(F32)
    reduce_ = expand.T
    dskip_x = jnp.repeat(p["d_skip"], HP, axis=1)
    a_log = _pad_lanes(p["a_log"])
    dt_bias = _pad_lanes(p["dt_bias"])
    w5, wxbc, wdt = p["w5"], p["wxbc"], p["wdt"]
    wa_bd = _block_diag(p["lru_wa"]).astype(MXU)
    wx_bd = _block_diag(p["lru_wx"]).astype(MXU)
    ba = p["lru_ba"].reshape(1, D)
    bx = p["lru_bx"].reshape(1, D)

    def hosted(key, fn):
        plan = dist.plan(key) if dist is not None else None
        if plan is None:
            return fn(plan=None)
        outs, got = fn(plan=plan)
        dist.done(key, got, p)
        return outs

    u = _norm_cast(x, p["norm_mix_pre"], "norm_u")
    proj5 = hosted("proj5", functools.partial(_matmul, u, w5, name="proj5"))
    xbc_raw = _matmul(u, wxbc, name="proj_xbc", tn=XBC)
    dt_raw = _matmul(u, wdt, name="proj_dt")
    xbc_c, dsilu, xr, dt = _conv_fwd(xbc_raw, proj5, dt_raw, p["conv_ssm_w"], p["conv_ssm_b"], p["conv_lru_w"],
                                     p["conv_lru_b"], dt_bias)
    y, ya, states = _ssd_fwd(xbc_c, dt, proj5, a_log, dskip_x, p["ssm_norm"], expand)
    h, merged = hosted("lru_fwd", functools.partial(_lru_fwd, xr, proj5, ya, wa_bd, wx_bd, ba, bx, p["lru_lambda"]))
    mix, h1, v = _out_proj(merged, p["w_out"], x, p["norm_mix_post"], p["norm_mlp_pre"])
    pre = _matmul(v, p["w_up"], name="up_proj", tm=1024, out_dtype=MXU)
    dout, dff, loss, dg4 = _down_loss(pre, p["w_down"], h1, target, p["norm_mlp_post"])

    dpre = _matmul(dff, p["w_down"], name="d_pre", tb=True, tm=1024, out_dtype=MXU,
                   epi=lambda r, pr: r * (2.0 * jnp.maximum(pr.astype(F32), 0.0)), epi_args=(pre,))
    g_w_down = _matmul(pre, dff, name="dw_down", ta=True, tm=1024, tn=1024, tk=TK_GRAD, a_fn=_relu2)
    dh1, dmix, dg3, dg2 = _dv_norms(dpre, p["w_up"], h1, mix, dout, p["norm_mlp_pre"], p["norm_mix_post"])
    g_w_up = _matmul(v, dpre, name="dw_up", ta=True, tm=1024, tn=1024, tk=TK_GRAD)
    dmerged = _matmul(dmix, p["w_out"], name="d_merged", tb=True)
    g_w_out = _matmul(merged, dmix, name="dw_out", ta=True, tm=1024, tn=1024, tk=TK_GRAD)
    if dist is not None:
        dist.early_grads(w_down=g_w_down, w_up=g_w_up, w_out=g_w_out)
    (dya, dga, dgb, dg, dxr, dpr, dpi, dlam, dba, dbx) = hosted("lru_bwd", functools.partial(
        _lru_bwd, dmerged, ya, xr, h, proj5, wa_bd, wx_bd, ba, bx, p["lru_lambda"]))
    g_wa = _diag_blocks(_matmul(xr, dpr, name="dw_lru_a", ta=True, tm=1024, tn=1024, tk=TK_GRAD))
    g_wx = _diag_blocks(_matmul(xr, dpi, name="dw_lru_x", ta=True, tm=1024, tn=1024, tk=TK_GRAD))
    dz, dxbc_c, ddt, dng, ddsk, dalog = hosted("ssd_bwd", functools.partial(
        _ssd_bwd, dya, y, proj5, xbc_c, dt, states, a_log, dskip_x, p["ssm_norm"], expand, reduce_))
    (dxbc, dxl, ddtr, dcws, dcbs, dcwl, dcbl, ddtb) = hosted("conv_bwd", functools.partial(
        _conv_bwd, dxbc_c, dsilu, dxr, ddt, xbc_raw, proj5, dt_raw, p["conv_ssm_w"], p["conv_lru_w"], dt_bias))
    pieces5 = (dz, dg, dxl, dga, dgb)
    gw5 = [_matmul(pc, u, name=f"dw_in_{i}", ta=True, tm=1024, tn=1024, tk=TK_GRAD) for i, pc in enumerate(pieces5)]
    gwxbc = _matmul(dxbc, u, name="dw_in_xbc", ta=True, tm=XBC, tn=1024, tk=TK_GRAD)
    gwdt = _matmul(ddtr, u, name="dw_in_dt", ta=True, tm=128, tn=1024, tk=TK_GRAD)
    g_w_in_t = jnp.concatenate([gw5[0], gwxbc, gwdt[:NH], gw5[1], gw5[2], gw5[3], gw5[4]], axis=0)
    if dist is not None:
        dist.late_grads(w_in_t=g_w_in_t)
    grad_x, dg1 = hosted("du_norm", functools.partial(_du_norm, pieces5, dxbc, ddtr, w5, wxbc, wdt, x, dh1,
                                                       p["norm_mix_pre"]))
    grads = {
        "norm_mix_pre": dg1, "w_in_t": g_w_in_t, "conv_ssm_w": dcws, "conv_ssm_b": dcbs, "dt_bias": ddtb[:, :NH],
        "a_log": dalog[:, :NH], "d_skip": ddsk.reshape(NH, HP).sum(axis=1)[None, :], "ssm_norm": dng,
        "conv_lru_w": dcwl, "conv_lru_b": dcbl, "lru_wa": g_wa, "lru_ba": dba.reshape(NH, HP), "lru_wx": g_wx,
        "lru_bx": dbx.reshape(NH, HP), "lru_lambda": dlam, "w_out": g_w_out, "norm_mix_post": dg2,
        "norm_mlp_pre": dg3, "w_up": g_w_up, "w_down": g_w_down, "norm_mlp_post": dg4,
    }
    return loss[0, 0], grad_x, grads


def _split_w_in(w_in_full):
    z, xbc, dtc, g, xl, ga, gb = jnp.split(w_in_full, [D, D + XBC, D + XBC + NH, 2 * D + XBC + NH,
                                                        3 * D + XBC + NH, 4 * D + XBC + NH], axis=1)
    return jnp.concatenate([z, g, xl, ga, gb], axis=1), xbc, _pad_lanes(dtc)


COMM = BF16


def _place():
    x, y, c = lax.axis_index("x"), lax.axis_index("y"), lax.axis_index("c")
    chips = [(1 - x, y), (x, 1 - y), (1 - x, 1 - y)]
    return x, y, c, chips


def _remote(src, dst, send_sem, recv_sem, to):
    return pltpu.make_async_remote_copy(src_ref=src, dst_ref=dst, send_sem=send_sem, recv_sem=recv_sem, device_id=to,
                                        device_id_type=MESH)


def _gather_plan(big, small=()):
    nb = len(big)
    arrs = list(big) + list(small)
    na = len(arrs)

    def direct(ins, outs, send, recv):
        x, y, c, chips = _place()
        k = 2 * x + y
        cps = []
        for a in range(na):
            if a < nb:
                hr = ins[a].shape[0] // 2
                src, dst = ins[a].at[pl.ds(c * hr, hr)], outs[a].at[k, pl.ds(c * hr, hr)]
            else:
                src, dst = ins[a], outs[a].at[k]
            cps += [_remote(src, dst, send.at[a, j], recv.at[a, j], (cx, cy, c)) for j, (cx, cy) in enumerate(chips)]
        return cps

    def start(ins, outs, sems):
        for cp in direct(ins, outs, *sems[0]):
            cp.start()

    def finish(ins, outs, sems):
        send, recv = sems[0]
        x, y, c, chips = _place()
        passed = []
        for j, (cx, cy) in enumerate(chips):
            kj = 2 * cx + cy
            for a in range(na):
                if a < nb:
                    hr = ins[a].shape[0] // 2
                    got = outs[a].at[kj, pl.ds(c * hr, hr)]
                    _remote(got, got, send.at[a, j], recv.at[a, j], (cx, cy, c)).wait_recv()
                    passed.append(_remote(got, got, send.at[a, 3 + j], recv.at[a, 3 + j], (x, y, 1 - c)))
                    passed[-1].start()
                else:
                    got = outs[a].at[kj]
                    _remote(got, got, send.at[a, j], recv.at[a, j], (cx, cy, c)).wait_recv()
        for j, (cx, cy) in enumerate(chips):
            for a in range(nb):
                hr = ins[a].shape[0] // 2
                got = outs[a].at[2 * cx + cy, pl.ds((1 - c) * hr, hr)]
                _remote(got, got, send.at[a, 3 + j], recv.at[a, 3 + j], (x, y, 1 - c)).wait_recv()
        for cp in direct(ins, outs, send, recv) + passed:
            cp.wait_send()

    return _Plan(arrs, [_sds((NCHIP,) + a.shape, a.dtype) for a in arrs], [(na, 6)], start, finish)


def _own_shards(gathered, shards):
    kchip = 2 * lax.axis_index("x") + lax.axis_index("y")
    return [lax.dynamic_update_index_in_dim(o, a, kchip, 0) for o, a in zip(gathered, shards)]


def _swap_plan(ins, outs, sems, copies):
    def start(i, o, s):
        for cp in copies(i, o, *s[0]):
            cp.start()

    def finish(i, o, s):
        for cp in copies(i, o, *s[0]):
            cp.wait()

    return _Plan(ins, outs, [sems], start, finish)


def _half_shape(shape, axis):
    return tuple(d // 2 if i == axis else d for i, d in enumerate(shape))


def _pair_exchange_plan(gs, axis=1):
    def copies(ins, outs, send, recv):
        x, y, c, _ = _place()
        cps = []
        for a in range(len(gs)):
            h = ins[a].shape[axis] // 2
            theirs = pl.ds((1 - c) * h, h)
            src = ins[a].at[:, theirs] if axis == 1 else ins[a].at[:, :, theirs]
            cps.append(_remote(src, outs[a], send.at[a], recv.at[a], (x, y, 1 - c)))
        return cps

    return _swap_plan(gs, [_sds(_half_shape(g.shape, axis), g.dtype) for g in gs], (len(gs),), copies)


def _pair_add(g, got, cidx, name, axis=1):
    half = _half_shape(g.shape, axis)
    blk, nt, _, part = _half_blocks(g.shape[1:], axis - 1)

    def body(c_ref, g_ref, o_ref, p_ref, pc_ref):
        sm = g_ref[...] + o_ref[...]
        p_ref[...] = sm
        pc_ref[...] = sm.astype(pc_ref.dtype)

    def mine(k, i, c_ref):
        j = c_ref[0] * nt + i
        return (k, j, 0) if axis == 1 else (k, 0, j)

    spec = pl.BlockSpec((1,) + blk, lambda k, i, c_ref: (k,) + part(i))
    return pl.pallas_call(
        body, name=name,
        grid_spec=pltpu.PrefetchScalarGridSpec(
            num_scalar_prefetch=1, grid=(NCHIP, nt),
            in_specs=[pl.BlockSpec((1,) + blk, mine), spec], out_specs=[spec, spec]),
        out_shape=[_sds(half), _sds(half, COMM)],
        compiler_params=_cp("parallel", "parallel"),
    )(cidx, g, got)


def _chip_exchange_plan(ps):
    def copies(ins, outs, send, recv):
        _, _, c, chips = _place()
        return [_remote(ins[a].at[2 * cx + cy], outs[a].at[j], send.at[a, j], recv.at[a, j], (cx, cy, c))
                for a in range(len(ps)) for j, (cx, cy) in enumerate(chips)]

    return _swap_plan(ps, [_sds((NCHIP - 1,) + p.shape[1:], p.dtype) for p in ps], (len(ps), 3), copies)


def _shard_sum(p, got, kidx, name, axis=1):
    full = tuple(2 * d if i == axis - 1 else d for i, d in enumerate(p.shape[1:]))
    blk, nt, _, part = _half_blocks(full, axis - 1)

    def body(k_ref, p_ref, g_ref, o_ref):
        sm = p_ref[0]
        for j in range(NCHIP - 1):
            sm = sm + g_ref[j].astype(F32)
        o_ref[...] = sm

    return pl.pallas_call(
        body, name=name,
        grid_spec=pltpu.PrefetchScalarGridSpec(
            num_scalar_prefetch=1, grid=(nt,),
            in_specs=[pl.BlockSpec((1,) + blk, lambda i, k_ref: (k_ref[0],) + part(i)),
                      pl.BlockSpec((NCHIP - 1,) + blk, lambda i, k_ref: (0,) + part(i))],
            out_specs=pl.BlockSpec(blk, lambda i, k_ref: part(i))),
        out_shape=_sds(p.shape[1:]),
        compiler_params=_cp("parallel"),
    )(kidx, p, got)


def _pair_swap_plan(rs):
    def copies(ins, outs, send, recv):
        x, y, c, _ = _place()
        return [_remote(ins[a], outs[a], send.at[a], recv.at[a], (x, y, 1 - c)) for a in range(len(rs))]

    return _swap_plan(rs, [_sds(r.shape, r.dtype) for r in rs], (len(rs),), copies)


def _allgather8_plan(v):
    def pieces(ins, outs, send, recv):
        x, y, c, chips = _place()
        me, sibling = (x, y, c), (x, y, 1 - c)

        def copy(k, block, to, src=None):
            px, py, pc = block
            slot = outs[0].at[4 * px + 2 * py + pc]
            return _remote(slot if src is None else src, slot, send.at[k], recv.at[k], to)

        first = [copy(0, me, sibling, src=ins[0])] + [copy(1 + j, me, (*chip, c), src=ins[0])
                                                      for j, chip in enumerate(chips)]
        passed = [copy(4 + j, (*chip, c), sibling) for j, chip in enumerate(chips)]
        arrivals = [copy(1 + j, (*chip, c), me) for j, chip in enumerate(chips)]
        late = [copy(0, sibling, me)] + [copy(4 + j, (*chip, 1 - c), me) for j, chip in enumerate(chips)]
        return first, passed, arrivals, late

    def start(ins, outs, sems):
        for cp in pieces(ins, outs, *sems[0])[0]:
            cp.start()

    def finish(ins, outs, sems):
        first, passed, arrivals, late = pieces(ins, outs, *sems[0])
        for got, fwd in zip(arrivals, passed):
            got.wait_recv()
            fwd.start()
        for got in late:
            got.wait_recv()
        for cp in first + passed:
            cp.wait_send()

    return _Plan([v], [_sds((8,) + v.shape, v.dtype)], [(7,)], start, finish)


def _own_block(gathered, v):
    me = 4 * lax.axis_index("x") + 2 * lax.axis_index("y") + lax.axis_index("c")
    return lax.dynamic_update_index_in_dim(gathered, v, me, 0)


def _sum_devices(allv):
    _, r, _ = allv.shape

    def body(a_ref, o_ref):
        sm = a_ref[0]
        for d in range(1, 8):
            sm = sm + a_ref[d]
        o_ref[...] = sm

    return pl.pallas_call(
        body, name="small_sum", grid=(1,), in_specs=[_full_spec((8, r, 128))], out_specs=_full_spec((r, 128)),
        out_shape=_sds((r, 128)), compiler_params=_cp("arbitrary"),
    )(allv)


def _pack(arrs):
    flat = jnp.concatenate([a.reshape(-1) for a in arrs])
    return jnp.pad(flat, (0, (-flat.shape[0]) % 1024)).reshape(-1, 128)


def _unpack(packed, shapes):
    flat, outs, off = packed.reshape(-1), [], 0
    for shp in shapes:
        n = math.prod(shp)
        outs.append(flat[off:off + n].reshape(shp))
        off += n
    return outs


BIG = ("w_in", "w_out", "w_up", "w_down")
CONV = ("conv_ssm_w", "conv_lru_w")
WEIGHTS = ("norm_mix_pre", "w_in", "conv_ssm_w", "conv_ssm_b", "dt_bias", "a_log", "d_skip", "ssm_norm", "conv_lru_w",
           "conv_lru_b", "lru_wa", "lru_ba", "lru_wx", "lru_bx", "lru_lambda", "w_out", "norm_mix_post",
           "norm_mlp_pre", "w_up", "w_down", "norm_mlp_post")
SMALL = tuple(n for n in WEIGHTS if n not in BIG and n not in CONV)
EARLY = ("w_down", "w_up", "w_out")


def _cat_cols(g):
    return jnp.concatenate([g[k] for k in range(NCHIP)], axis=1)


class _Dist:
    def __init__(self, shards, cidx, kidx):
        self.shards, self.cidx, self.kidx = shards, cidx, kidx
        self.halves = {}

    def early_grads(self, w_down, w_up, w_out):
        self.shard_major = [w_down.reshape(NCHIP, D, D), jnp.stack([w_up[:, D * k:D * (k + 1)] for k in range(NCHIP)]),
                            w_out.reshape(NCHIP, D // NCHIP, D)]

    def late_grads(self, w_in_t):
        self.g_in = w_in_t.reshape(NCHIP, W_IN_SHARD, D)

    def plan(self, key):
        if key == "proj5":
            return _gather_plan([self.shards["w_out"], self.shards["w_up"]])
        if key == "lru_fwd":
            return _gather_plan([self.shards["w_down"]])
        if key == "lru_bwd":
            return _pair_exchange_plan(self.shard_major)
        if key == "ssd_bwd":
            return _chip_exchange_plan([pc for _, pc in self.pair])
        if key == "conv_bwd":
            return _pair_swap_plan(self.mine)
        if key == "du_norm":
            return _pair_exchange_plan([self.g_in], axis=2)
        return None

    def done(self, key, got, p):
        if key == "proj5":
            g_out, g_up = _own_shards(got, [self.shards["w_out"], self.shards["w_up"]])
            p.update(w_out=g_out.reshape(D, D), w_up=_cat_cols(g_up))
        elif key == "lru_fwd":
            g_down, = _own_shards(got, [self.shards["w_down"]])
            p.update(w_down=g_down.reshape(DFF, D))
        elif key == "lru_bwd":
            self.pair = [_pair_add(gs, o, self.cidx, f"grad_pair_add_{n}")
                         for gs, o, n in zip(self.shard_major, got, EARLY)]
        elif key == "ssd_bwd":
            self.mine = [_shard_sum(pf, o, self.kidx, f"grad_shard_sum_{n}")
                         for (pf, _), o, n in zip(self.pair, got, EARLY)]
        elif key == "conv_bwd":
            self.halves = {n: (mine, other) for n, mine, other in zip(EARLY, self.mine, got)}
        elif key == "du_norm":
            self.got_in, = got


def kernel(x, norm_mix_pre, w_in, conv_ssm_w, conv_ssm_b, dt_bias, a_log, d_skip, ssm_norm, conv_lru_w, conv_lru_b, lru_wa, lru_ba, lru_wx, lru_bx, lru_lambda, w_out, norm_mix_post, norm_mlp_pre, w_up, w_down, norm_mlp_post, loss_target, m_norm_mix_pre, m_w_in, m_conv_ssm_w, m_conv_ssm_b, m_dt_bias, m_a_log, m_d_skip, m_ssm_norm, m_conv_lru_w, m_conv_lru_b, m_lru_wa, m_lru_ba, m_lru_wx, m_lru_bx, m_lru_lambda, m_w_out, m_norm_mix_post, m_norm_mlp_pre, m_w_up, m_w_down, m_norm_mlp_post, v_norm_mix_pre, v_w_in, v_conv_ssm_w, v_conv_ssm_b, v_dt_bias, v_a_log, v_d_skip, v_ssm_norm, v_conv_lru_w, v_conv_lru_b, v_lru_wa, v_lru_ba, v_lru_wx, v_lru_bx, v_lru_lambda, v_w_out, v_norm_mix_post, v_norm_mlp_pre, v_w_up, v_w_down, v_norm_mlp_post):
    args = locals()
    w = {n: args[n][0] for n in WEIGHTS}
    m = {n: args["m_" + n][0] for n in WEIGHTS}
    v = {n: args["v_" + n][0] for n in WEIGHTS}
    cidx = lax.axis_index("c").astype(jnp.int32).reshape(1)
    kchip = 2 * lax.axis_index("x") + lax.axis_index("y")
    dist = _Dist({n: w[n].astype(MXU) for n in BIG}, cidx, kchip.astype(jnp.int32).reshape(1))

    first = [dist.shards["w_in"], w["conv_ssm_w"], w["conv_lru_w"]]
    gathered = _own_shards(_run_plan(_gather_plan(first[:1], first[1:]), "gather_w_in"), first)
    w5, wxbc, wdt = _split_w_in(_cat_cols(gathered[0]))
    p = {n: (w[n].reshape(1, -1) if w[n].ndim == 1 else w[n]) for n in SMALL}
    p.update(w5=w5, wxbc=wxbc, wdt=wdt, conv_ssm_w=_cat_cols(gathered[1]), conv_lru_w=_cat_cols(gathered[2]))

    loss, grad_x, g = _local_step(x[0], loss_target[0], p, dist)

    small_shapes = [g[n].shape for n in SMALL + CONV] + [(1,)]
    packed_small = _pack([g[n] for n in SMALL + CONV] + [loss.reshape(1)])
    p_in, pc_in = _pair_add(dist.g_in, dist.got_in, cidx, "grad_pair_add_w_in", axis=2)
    all_small, from_chips = _run_plan(_merge_plans(_allgather8_plan(packed_small), _chip_exchange_plan([pc_in])),
                                      "grad_chip_exchange_w_in")
    half_in = _shard_sum(p_in, from_chips, dist.kidx, "grad_shard_sum_w_in", axis=2)
    other_in, = _run_plan(_pair_swap_plan([half_in]), "grad_pair_swap_w_in")
    halves = dist.halves

    reduced = {}
    *summed, loss = _unpack(_sum_devices(_own_block(all_small, packed_small)), small_shapes)
    loss = loss.reshape(())
    for n, s in zip(SMALL + CONV, summed):
        if n in CONV:
            width = w[n].shape[1]
            reduced[n] = lax.dynamic_slice_in_dim(s, kchip * width, width, axis=1)
        else:
            reduced[n] = s.reshape(w[n].shape)

    delta, new_m, new_v = {}, {}, {}
    for n in EARLY:
        mine, other = halves[n]
        reduced[n], delta[n], new_m[n], new_v[n] = _adamw_halves(w[n], mine, other, m[n], v[n], cidx, f"adamw_{n}")
    to_t = lambda a: jnp.transpose(a, (2, 0, 1)).reshape(W_IN_SHARD, D)
    from_t = lambda a: jnp.transpose(a.reshape(W_IN_SHARD, 1, D), (1, 2, 0))
    outs_t = _adamw_halves(to_t(w_in), half_in, other_in, to_t(m_w_in), to_t(v_w_in), cidx, "adamw_w_in", axis=1)
    for d, o in zip((reduced, delta, new_m, new_v), outs_t):
        d["w_in"] = from_t(o)[0]
    for n in CONV:
        delta[n], new_m[n], new_v[n] = _adamw(w[n], reduced[n], m[n], v[n], f"adamw_{n}")
    shapes = [w[n].shape for n in SMALL]
    packed = [_pack([d[n] for n in SMALL]) for d in (w, reduced, m, v)]
    for d, out in zip((delta, new_m, new_v), _adamw(*packed, "adamw_small")):
        d.update(zip(SMALL, _unpack(out, shapes)))

    lead = lambda d: [d[n][None] for n in WEIGHTS]
    return (loss, grad_x[None], *lead(reduced), *lead(delta), *lead(new_m), *lead(new_v))
```

```python
import functools
import math

import jax
import jax.numpy as jnp
from jax import lax
from jax.experimental import pallas as pl
from jax.experimental.pallas import tpu as pltpu

F32 = jnp.float32
BF16 = jnp.bfloat16
MXU = BF16

D = 1024
DFF = 4096
NH = 16
HP = 64
NG = 2
NS = 128
CH = 128
XBC = D + 2 * NG * NS
GW = D // NG
LRU_C = 8.0
EPS = 1e-6
NCHIP = 4
W_IN_COLS = 6672
W_IN_SHARD = W_IN_COLS // NCHIP

ADAM_LR = 0.001
ADAM_B1 = 0.9
ADAM_B2 = 0.999
ADAM_EPS = 1e-08
ADAM_WD = 0.01
ADAM_STEP = 10

VMEM_LIMIT = 56 * 1024 * 1024
TK_GRAD = 2048
MESH = pl.DeviceIdType.MESH


def _cp(*sem):
    return pltpu.CompilerParams(dimension_semantics=sem, vmem_limit_bytes=VMEM_LIMIT)


def _dot(a, b, ca=1, cb=0, prec=None):
    return lax.dot_general(a, b, (((ca,), (cb,)), ((), ())), precision=prec, preferred_element_type=F32)


def _mdot(a, b, ca=1, cb=0):
    return _dot(a.astype(MXU), b.astype(MXU), ca, cb)


def _bf16_parts(v, n):
    parts = []
    for i in range(n):
        p = v.astype(BF16)
        parts.append(p)
        if i < n - 1:
            v = v - p.astype(F32)
    return parts


def _xdot(a, b, passes, split_b=False):
    if split_b:
        a16 = a.astype(BF16)
        terms = [_dot(a16, p) for p in _bf16_parts(b, passes)]
    else:
        b16 = b.astype(BF16)
        terms = [_dot(p, b16) for p in _bf16_parts(a, passes)]
    return functools.reduce(lambda u, v: u + v, terms)


def _sig(x):
    return 0.5 * jnp.tanh(0.5 * x) + 0.5


def _silu(x):
    return x * _sig(x)


def _dsilu(x):
    s = _sig(x)
    return s * (1.0 + x * (1.0 - s))


def _softplus(x):
    e = jnp.exp(-jnp.abs(x))
    return jnp.maximum(x, 0.0) + jnp.where(e < 1e-4, e * (1.0 - 0.5 * e), jnp.log(1.0 + e))


_GELU_C = math.sqrt(2.0 / math.pi)


def _gelu(x):
    t = jnp.tanh(_GELU_C * (x + 0.044715 * x * x * x))
    return 0.5 * x * (1.0 + t)


def _gelu_and_grad(x):
    x2 = x * x
    t = jnp.tanh(_GELU_C * (x + 0.044715 * x * x2))
    half = 0.5 * (1.0 + t)
    return x * half, half + 0.5 * x * (1.0 - t * t) * _GELU_C * (1.0 + 3.0 * 0.044715 * x2)


def _one_minus_sq(a, la):
    x = 2.0 * la
    series = -x * (1.0 + x * (0.5 + x * (1.0 / 6.0)))
    return jnp.where(x > -0.01, series, 1.0 - a * a)


def _rms(x):
    return lax.rsqrt(jnp.mean(x * x, axis=-1, keepdims=True) + EPS)


def _rms_bwd(x, r, g, dy):
    xn = x * r
    dxh = dy * g
    m = jnp.mean(dxh * xn, axis=-1, keepdims=True)
    return r * (dxh - xn * m), jnp.sum(dy * xn, axis=0, keepdims=True)


def _row_spec(t, c, col=0):
    return pl.BlockSpec((t, c), lambda i: (i, col))


def _rev_spec(t, c, n, col=0):
    return pl.BlockSpec((t, c), lambda i: (n - 1 - i, col))


def _full_spec(shape):
    nd = len(shape)
    return pl.BlockSpec(shape, lambda *_: (0,) * nd)


def _sds(shape, dtype=F32):
    return jax.ShapeDtypeStruct(shape, dtype)


ANY = pl.BlockSpec(memory_space=pl.ANY)


class _Plan:
    def __init__(self, ins, outs, sems, start, finish):
        self.ins, self.outs, self.sems, self.start, self.finish = list(ins), list(outs), list(sems), start, finish


def _merge_plans(*plans):
    def each(fn_name, ins, outs, sems):
        i = o = s = 0
        for p in plans:
            getattr(p, fn_name)(ins[i:i + len(p.ins)], outs[o:o + len(p.outs)], sems[s:s + len(p.sems)])
            i, o, s = i + len(p.ins), o + len(p.outs), s + len(p.sems)

    return _Plan([a for p in plans for a in p.ins], [a for p in plans for a in p.outs],
                 [a for p in plans for a in p.sems], functools.partial(each, "start"), functools.partial(each, "finish"))


def _pcall(body, args, *, name, grid, in_specs, out_specs, out_shape, sem, scratch_shapes=(), plan=None):
    single = not isinstance(out_shape, (list, tuple))
    out_specs = [out_specs] if single else list(out_specs)
    out_shape = [out_shape] if single else list(out_shape)
    if plan is None:
        outs = pl.pallas_call(body, name=name, grid=grid, in_specs=list(in_specs), out_specs=out_specs,
                              out_shape=out_shape, scratch_shapes=list(scratch_shapes),
                              compiler_params=_cp(*sem))(*args)
        return outs[0] if single else outs
    n_in, n_out, n_sc, ni, no = len(in_specs), len(out_shape), len(scratch_shapes), len(plan.ins), len(plan.outs)

    def hosted(*refs):
        b0 = n_in + ni
        b1 = b0 + n_out + no
        sem_refs = refs[b1 + n_sc:]
        sems = [(sem_refs[2 * q], sem_refs[2 * q + 1]) for q in range(len(plan.sems))]
        ids = [pl.program_id(d) for d in range(len(grid))]
        first = functools.reduce(jnp.logical_and, [i == 0 for i in ids])
        last = functools.reduce(jnp.logical_and, [i == g - 1 for i, g in zip(ids, grid)])

        @pl.when(first)
        def _():
            plan.start(refs[n_in:b0], refs[b0 + n_out:b1], sems)

        body(*refs[:n_in], *refs[b0:b0 + n_out], *refs[b1:b1 + n_sc])

        @pl.when(last)
        def _():
            plan.finish(refs[n_in:b0], refs[b0 + n_out:b1], sems)

    dma = [pltpu.SemaphoreType.DMA(shape) for shape in plan.sems for _ in range(2)]
    outs = pl.pallas_call(hosted, name=name, grid=grid, in_specs=list(in_specs) + [ANY] * ni,
                          out_specs=out_specs + [ANY] * no, out_shape=out_shape + plan.outs,
                          scratch_shapes=list(scratch_shapes) + dma,
                          compiler_params=_cp(*("arbitrary",) * len(grid)))(*args, *plan.ins)
    return (outs[0] if single else outs[:n_out]), outs[n_out:]


def _run_plan(plan, name):
    ni, no = len(plan.ins), len(plan.outs)

    def body(*refs):
        sem_refs = refs[ni + no:]
        sems = [(sem_refs[2 * q], sem_refs[2 * q + 1]) for q in range(len(plan.sems))]
        plan.start(refs[:ni], refs[ni:ni + no], sems)
        plan.finish(refs[:ni], refs[ni:ni + no], sems)

    return pl.pallas_call(
        body, name=name, in_specs=[ANY] * ni, out_specs=[ANY] * no, out_shape=plan.outs,
        scratch_shapes=[pltpu.SemaphoreType.DMA(shape) for shape in plan.sems for _ in range(2)],
    )(*plan.ins)


def _matmul(a, b, *, name, ta=False, tb=False, tm=512, tn=1024, tk=1024, out_dtype=F32, a_fn=None, epi=None,
            epi_args=(), plan=None):
    m, k = (a.shape[1], a.shape[0]) if ta else a.shape
    n = b.shape[0] if tb else b.shape[1]
    tm, tn, tk = min(tm, m), min(tn, n), min(tk, k)
    nk = k // tk
    a_spec = pl.BlockSpec((tk, tm), lambda i, j, kk: (kk, i)) if ta else pl.BlockSpec((tm, tk), lambda i, j, kk: (i, kk))
    b_spec = pl.BlockSpec((tn, tk), lambda i, j, kk: (j, kk)) if tb else pl.BlockSpec((tk, tn), lambda i, j, kk: (kk, j))
    e_specs = [pl.BlockSpec((tm, tn), lambda i, j, kk: (i, j)) for _ in epi_args]
    ne = len(epi_args)

    def body(a_ref, b_ref, *rest):
        e_refs, o_ref = rest[:ne], rest[ne]
        av = a_ref[...]
        if a_fn is not None:
            av = a_fn(av)
        part = _mdot(av, b_ref[...], 0 if ta else 1, 1 if tb else 0)

        def finish(r):
            if epi is not None:
                r = epi(r, *[e[...] for e in e_refs])
            o_ref[...] = r.astype(o_ref.dtype)

        if nk == 1:
            finish(part)
            return
        acc_ref = rest[ne + 1]
        kk = pl.program_id(2)

        @pl.when(kk == 0)
        def _():
            acc_ref[...] = part

        @pl.when(jnp.logical_and(kk > 0, kk < nk - 1))
        def _():
            acc_ref[...] += part

        @pl.when(kk == nk - 1)
        def _():
            finish(acc_ref[...] + part)

    return _pcall(
        body, (a, b, *epi_args), name=name, grid=(m // tm, n // tn, nk),
        in_specs=[a_spec, b_spec] + e_specs,
        out_specs=pl.BlockSpec((tm, tn), lambda i, j, kk: (i, j)),
        out_shape=_sds((m, n), out_dtype),
        scratch_shapes=[pltpu.VMEM((tm, tn), F32)] if nk > 1 else [],
        sem=("parallel", "parallel", "arbitrary"), plan=plan)


def _relu2(p):
    p = jnp.maximum(p.astype(F32), 0.0)
    return p * p


def _norm_cast(x, g, name, plan=None):
    s = x.shape[0]
    t = min(512, s)

    def body(x_ref, g_ref, o_ref):
        xv = x_ref[...]
        o_ref[...] = (xv * _rms(xv) * g_ref[...]).astype(o_ref.dtype)

    return _pcall(body, (x, g), name=name, grid=(s // t,), in_specs=[_row_spec(t, D), _full_spec((1, D))],
                  out_specs=_row_spec(t, D), out_shape=_sds((s, D), MXU), sem=("parallel",), plan=plan)


def _conv_fwd(xbc_raw, proj5, dt_raw, cw_s, cb_s, cw_l, cb_l, dt_bias):
    s = xbc_raw.shape[0]
    t = min(256, s)

    def body(xs_ref, xl_ref, dtr_ref, cws_ref, cbs_ref, cwl_ref, cbl_ref, dtb_ref, xc_ref, dsl_ref, xr_ref, dt_ref,
             bs_ref, bl_ref):
        @pl.when(pl.program_id(0) == 0)
        def _():
            bs_ref[0:8, :] = jnp.zeros((8, XBC), F32)
            bl_ref[0:8, :] = jnp.zeros((8, D), F32)

        bs_ref[8:t + 8, :] = xs_ref[...]
        bl_ref[8:t + 8, :] = xl_ref[...]

        def conv(buf, w_ref, b_ref):
            acc = b_ref[...] + w_ref[3:4, :] * buf[8:t + 8, :]
            for k in (1, 2, 3):
                acc = acc + w_ref[3 - k:4 - k, :] * buf[8 - k:t + 8 - k, :]
            return acc

        pre = conv(bs_ref, cws_ref, cbs_ref)
        sg = _sig(pre)
        xc_ref[...] = pre * sg
        dsl_ref[...] = (sg * (1.0 + pre * (1.0 - sg))).astype(dsl_ref.dtype)
        xr_ref[...] = conv(bl_ref, cwl_ref, cbl_ref)
        dt_ref[...] = _softplus(dtr_ref[...] + dtb_ref[...])
        bs_ref[0:8, :] = bs_ref[t:t + 8, :]
        bl_ref[0:8, :] = bl_ref[t:t + 8, :]

    return pl.pallas_call(
        body, name="conv_fwd", grid=(s // t,),
        in_specs=[_row_spec(t, XBC), _row_spec(t, D, 2), _row_spec(t, 128), _full_spec((4, XBC)),
                  _full_spec((1, XBC)), _full_spec((4, D)), _full_spec((1, D)), _full_spec((1, 128))],
        out_specs=[_row_spec(t, XBC), _row_spec(t, XBC), _row_spec(t, D), _row_spec(t, 128)],
        out_shape=[_sds((s, XBC)), _sds((s, XBC), BF16), _sds((s, D)), _sds((s, 128))],
        scratch_shapes=[pltpu.VMEM((t + 8, XBC), F32), pltpu.VMEM((t + 8, D), F32)],
        compiler_params=_cp("arbitrary"),
    )(xbc_raw, proj5, dt_raw, cw_s, cb_s, cw_l, cb_l, dt_bias)


def _ssd_chunk_setup(dt_ref, alog_ref, e_ref, at_ref, dtt_ref):
    lane = lax.broadcasted_iota(jnp.int32, (CH, 128), 1)
    row = lax.broadcasted_iota(jnp.int32, (CH, 128), 0)
    lane1 = lax.broadcasted_iota(jnp.int32, (1, 128), 1)
    a = jnp.where(lane1 < NH, -jnp.exp(alog_ref[...]), 0.0)
    dtv = dt_ref[...]
    adt = dtv * a
    tril = row >= lane
    acum = _xdot(tril.astype(F32), adt, 3, split_b=True)
    alast = jnp.sum(adt, axis=0, keepdims=True)
    at_ref[...] = acum.T
    dtt_ref[...] = dtv.T
    e = e_ref[...]
    ea_x = _xdot(jnp.exp(acum), e, 2)
    ws = jnp.exp(alast - acum) * dtv
    ws_x = _xdot(ws, e, 2)
    eal = jnp.exp(alast)
    eal_x = jnp.max(_xdot(jnp.broadcast_to(eal, (8, 128)), e, 3), axis=0, keepdims=True)
    return dict(lane=lane, row=row, tril=tril, a=a, dtv=dtv, acum=acum, alast=alast, ea_x=ea_x, ws=ws, ws_x=ws_x,
                eal=eal, eal_x=eal_x)


def _head_decay(cs, at_ref, dtt_ref, h):
    col = jnp.sum(jnp.where(cs["lane"] == h, cs["acum"], 0.0), axis=1, keepdims=True)
    ld = jnp.where(cs["tril"], jnp.exp(jnp.minimum(col - at_ref[h:h + 1, :], 0.0)), 0.0)
    return ld, dtt_ref[h:h + 1, :]


def _ssd_fwd(xbc_c, dt, proj5, a_log, dskip_x, ssm_norm, expand):
    s = xbc_c.shape[0]
    nc = s // CH

    def body(xc_ref, dt_ref, z_ref, alog_ref, dsk_ref, ng_ref, e_ref, y_ref, ya_ref, st_ref, h_ref, at_ref, dtt_ref,
             yd_ref):
        @pl.when(pl.program_id(0) == 0)
        def _():
            h_ref[...] = jnp.zeros_like(h_ref)

        cs = _ssd_chunk_setup(dt_ref, alog_ref, e_ref, at_ref, dtt_ref)
        lane = cs["lane"]
        for g in range(NG):
            gs = slice(GW * g, GW * (g + 1))
            bg = xc_ref[:, D + NS * g:D + NS * (g + 1)]
            cg = xc_ref[:, D + NG * NS + NS * g:D + NG * NS + NS * (g + 1)]
            cb = _mdot(cg, bg, 1, 1)
            for j in range(4 * g, 4 * g + 4):
                ps = slice(128 * j, 128 * (j + 1))
                xp = xc_ref[:, ps]
                acc = jnp.zeros((CH, 128), F32)
                for hf in range(2):
                    ld, rowdt = _head_decay(cs, at_ref, dtt_ref, 2 * j + hf)
                    hm = (lane >= HP) if hf else (lane < HP)
                    acc = acc + _mdot(cb * ld * rowdt, jnp.where(hm, xp, 0.0))
                yd_ref[:, ps] = acc
            hg = h_ref[:, gs]
            yd_ref[:, gs] += _mdot(cg, hg) * cs["ea_x"][:, gs]
            st = _mdot(bg, xc_ref[:, gs] * cs["ws_x"][:, gs], 0, 0)
            st_ref[0, :, gs] = hg
            h_ref[:, gs] = cs["eal_x"][:, gs] * hg + st
        y = yd_ref[...] + dsk_ref[...] * xc_ref[:, 0:D]
        y_ref[...] = y
        yg = y * _silu(z_ref[...])
        for g in range(NG):
            gs = slice(GW * g, GW * (g + 1))
            seg = yg[:, gs]
            ya_ref[:, gs] = seg * _rms(seg) * ng_ref[:, gs]

    return pl.pallas_call(
        body, name="ssd_fwd", grid=(nc,),
        in_specs=[_row_spec(CH, XBC), _row_spec(CH, 128), _row_spec(CH, D, 0), _full_spec((1, 128)),
                  _full_spec((1, D)), _full_spec((1, D)), _full_spec((128, D))],
        out_specs=[_row_spec(CH, D), _row_spec(CH, D), pl.BlockSpec((1, NS, D), lambda i: (i, 0, 0))],
        out_shape=[_sds((s, D)), _sds((s, D)), _sds((nc, NS, D))],
        scratch_shapes=[pltpu.VMEM((NS, D), F32), pltpu.VMEM((128, 128), F32), pltpu.VMEM((128, 128), F32),
                        pltpu.VMEM((CH, D), F32)],
        compiler_params=_cp("arbitrary"),
    )(xbc_c, dt, proj5, a_log, dskip_x, ssm_norm, expand)


def _lru_gates(xr, wa_ref, wx_ref, ba_ref, bx_ref, lam_ref):
    gr = _sig(_mdot(xr, wa_ref[...]) + ba_ref[...])
    gi = _sig(_mdot(xr, wx_ref[...]) + bx_ref[...])
    sp = _softplus(-lam_ref[...])
    la = -LRU_C * gr * sp
    a = jnp.exp(la)
    oms = _one_minus_sq(a, la)
    inv_mult = lax.rsqrt(oms)
    return gr, gi, sp, a, oms * inv_mult, inv_mult


def _blocked_scan(a, u, carry_ref, a_ref, u_ref, c_ref, out_ref, reverse):
    t = a.shape[0]
    ns = t // 8

    def combine(av, uv, idx, n, sh):
        m = (idx < n - sh) if reverse else (idx >= sh)
        by = n - sh if reverse else sh
        return jnp.where(m, av * pltpu.roll(av, by, 0), av), jnp.where(m, uv + av * pltpu.roll(uv, by, 0), uv)

    row = lax.broadcasted_iota(jnp.int32, (t, D), 0)
    rin = jnp.bitwise_and(row, 7)
    for sh in (1, 2, 4):
        m = (rin < 8 - sh) if reverse else (rin >= sh)
        by = t - sh if reverse else sh
        a, u = jnp.where(m, a * pltpu.roll(a, by, 0), a), jnp.where(m, u + a * pltpu.roll(u, by, 0), u)
    a_ref[...] = a
    u_ref[...] = u
    edge = 0 if reverse else 7
    for j in range(ns):
        c_ref[j:j + 1, :] = a_ref[8 * j + edge:8 * j + edge + 1, :]
    at = c_ref[...]
    for j in range(ns):
        c_ref[j:j + 1, :] = u_ref[8 * j + edge:8 * j + edge + 1, :]
    ut = c_ref[...]
    srow = lax.broadcasted_iota(jnp.int32, (ns, D), 0)
    sh = 1
    while sh < ns:
        at, ut = combine(at, ut, srow, ns, sh)
        sh *= 2
    cv = carry_ref[0:1, :]
    ends = ut + at * cv
    last = 0 if reverse else ns - 1
    first = ns - 1 if reverse else 0
    c_ref[...] = jnp.where(srow == first, cv, pltpu.roll(ends, first if reverse else 1, 0))
    carry_ref[0:1, :] = jnp.sum(jnp.where(srow == last, ends, 0.0), axis=0, keepdims=True)
    for j in range(ns):
        sl = slice(8 * j, 8 * j + 8)
        out_ref[sl, :] = u_ref[sl, :] + a_ref[sl, :] * c_ref[j:j + 1, :]


def _lru_fwd(xr, proj5, ya, wa_bd, wx_bd, ba, bx, lam, plan=None):
    s = xr.shape[0]
    t = min(256, s)

    def body(xr_ref, g_ref, ga_ref, gb_ref, ya_ref, wa_ref, wx_ref, ba_ref, bx_ref, lam_ref, h_ref, mg_ref, hc_ref,
             a_ref, u_ref, c_ref):
        @pl.when(pl.program_id(0) == 0)
        def _():
            hc_ref[...] = jnp.zeros_like(hc_ref)

        xrv = xr_ref[...]
        _, gi, _, a, mult, _ = _lru_gates(xrv, wa_ref, wx_ref, ba_ref, bx_ref, lam_ref)
        _blocked_scan(a, mult * gi * xrv, hc_ref, a_ref, u_ref, c_ref, h_ref, reverse=False)
        yb = h_ref[...] * _gelu(g_ref[...])
        mg_ref[...] = (_sig(ga_ref[...]) * ya_ref[...] + _sig(gb_ref[...]) * yb).astype(mg_ref.dtype)

    return _pcall(
        body, (xr, proj5, proj5, proj5, ya, wa_bd, wx_bd, ba, bx, lam), name="lru_fwd", grid=(s // t,),
        in_specs=[_row_spec(t, D), _row_spec(t, D, 1), _row_spec(t, D, 3), _row_spec(t, D, 4), _row_spec(t, D),
                  _full_spec((D, D)), _full_spec((D, D)), _full_spec((1, D)), _full_spec((1, D)), _full_spec((1, D))],
        out_specs=[_row_spec(t, D), _row_spec(t, D)],
        out_shape=[_sds((s, D)), _sds((s, D), MXU)],
        scratch_shapes=[pltpu.VMEM((8, D), F32), pltpu.VMEM((t, D), F32), pltpu.VMEM((t, D), F32),
                        pltpu.VMEM((t // 8, D), F32)],
        sem=("arbitrary",), plan=plan)


def _out_proj(merged, w_out, x, g2, g3):
    s = x.shape[0]
    t = min(256, s)

    def body(mg_ref, w_ref, x_ref, g2_ref, g3_ref, mix_ref, h1_ref, v_ref):
        mix = _mdot(mg_ref[...], w_ref[...])
        mix_ref[...] = mix
        h1 = x_ref[...] + mix * _rms(mix) * g2_ref[...]
        h1_ref[...] = h1
        v_ref[...] = (h1 * _rms(h1) * g3_ref[...]).astype(v_ref.dtype)

    return pl.pallas_call(
        body, name="out_proj", grid=(s // t,),
        in_specs=[_row_spec(t, D), _full_spec((D, D)), _row_spec(t, D), _full_spec((1, D)), _full_spec((1, D))],
        out_specs=[_row_spec(t, D), _row_spec(t, D), _row_spec(t, D)],
        out_shape=[_sds((s, D)), _sds((s, D)), _sds((s, D), MXU)],
        compiler_params=_cp("parallel"),
    )(merged, w_out, x, g2, g3)


def _down_loss(pre, w_down, h1, target, g4):
    s = pre.shape[0]
    t = min(256, s)

    def body(pre_ref, w_ref, h1_ref, tg_ref, g4_ref, dout_ref, dff_ref, loss_ref, dg4_ref):
        @pl.when(pl.program_id(0) == 0)
        def _():
            loss_ref[...] = jnp.zeros_like(loss_ref)
            dg4_ref[...] = jnp.zeros_like(dg4_ref)

        ff = _mdot(_relu2(pre_ref[...]), w_ref[...])
        r4 = _rms(ff)
        g4v = g4_ref[...]
        diff = h1_ref[...] + ff * r4 * g4v - tg_ref[...]
        sq = jnp.sum(jnp.sum(diff * diff, axis=1, keepdims=True), axis=0, keepdims=True)
        loss_ref[...] += (0.5 / D) * sq
        dout = diff * (1.0 / D)
        dout_ref[...] = dout
        dff, dg = _rms_bwd(ff, r4, g4v, dout)
        dff_ref[...] = dff.astype(dff_ref.dtype)
        dg4_ref[...] += dg

    return pl.pallas_call(
        body, name="down_loss", grid=(s // t,),
        in_specs=[_row_spec(t, DFF), _full_spec((DFF, D)), _row_spec(t, D), _row_spec(t, D), _full_spec((1, D))],
        out_specs=[_row_spec(t, D), _row_spec(t, D), _full_spec((1, 128)), _full_spec((1, D))],
        out_shape=[_sds((s, D)), _sds((s, D), MXU), _sds((1, 128)), _sds((1, D))],
        compiler_params=_cp("arbitrary"),
    )(pre, w_down, h1, target, g4)


def _dv_norms(dpre, w_up, h1, mix, dout, g3, g2):
    s = h1.shape[0]
    t = min(256, s)

    def body(dp_ref, w_ref, h1_ref, mix_ref, dout_ref, g3_ref, g2_ref, dh1_ref, dmix_ref, dg3_ref, dg2_ref):
        @pl.when(pl.program_id(0) == 0)
        def _():
            dg3_ref[...] = jnp.zeros_like(dg3_ref)
            dg2_ref[...] = jnp.zeros_like(dg2_ref)

        dv = _mdot(dp_ref[...], w_ref[...], 1, 1)
        h1 = h1_ref[...]
        dh1n, dg3 = _rms_bwd(h1, _rms(h1), g3_ref[...], dv)
        dh1 = dout_ref[...] + dh1n
        dh1_ref[...] = dh1
        mix = mix_ref[...]
        dmix, dg2 = _rms_bwd(mix, _rms(mix), g2_ref[...], dh1)
        dmix_ref[...] = dmix.astype(dmix_ref.dtype)
        dg3_ref[...] += dg3
        dg2_ref[...] += dg2

    return pl.pallas_call(
        body, name="dv_norms", grid=(s // t,),
        in_specs=[_row_spec(t, DFF), _full_spec((D, DFF)), _row_spec(t, D), _row_spec(t, D), _row_spec(t, D),
                  _full_spec((1, D)), _full_spec((1, D))],
        out_specs=[_row_spec(t, D), _row_spec(t, D), _full_spec((1, D)), _full_spec((1, D))],
        out_shape=[_sds((s, D)), _sds((s, D), MXU), _sds((1, D)), _sds((1, D))],
        compiler_params=_cp("arbitrary"),
    )(dpre, w_up, h1, mix, dout, g3, g2)


def _lru_bwd(dmerged, ya, xr, h, proj5, wa_bd, wx_bd, ba, bx, lam, plan=None):
    s = xr.shape[0]
    t = min(128, s)
    n = s // t
    rs = functools.partial(_rev_spec, t, D, n)

    def body(dm_ref, ya_ref, xr_ref, h_ref, hp_ref, g_ref, ga_ref, gb_ref, wa_ref, wx_ref, ba_ref, bx_ref, lam_ref,
             dya_ref, dga_ref, dgb_ref, dg_ref, dxr_ref, dpr_ref, dpi_ref, dlam_ref, dba_ref, dbx_ref, gc_ref,
             af_ref, an_ref, us_ref, c_ref, gs_ref):
        i = pl.program_id(0)

        @pl.when(i == 0)
        def _():
            gc_ref[...] = jnp.zeros_like(gc_ref)
            af_ref[...] = jnp.zeros_like(af_ref)
            dlam_ref[...] = jnp.zeros_like(dlam_ref)
            dba_ref[...] = jnp.zeros_like(dba_ref)
            dbx_ref[...] = jnp.zeros_like(dbx_ref)

        xrv = xr_ref[...]
        gr, gi, sp, a, mult, inv_mult = _lru_gates(xrv, wa_ref, wx_ref, ba_ref, bx_ref, lam_ref)
        hv = h_ref[...]
        dm = dm_ref[...]
        sa = _sig(ga_ref[...])
        sb = _sig(gb_ref[...])
        gel, dgel = _gelu_and_grad(g_ref[...])
        dya = dm * sa
        dya_ref[...] = dya
        dga_ref[...] = (dya * ya_ref[...] * (1.0 - sa)).astype(dga_ref.dtype)
        dyb = dm * sb
        dybh = dyb * hv
        dgb_ref[...] = (dybh * gel * (1.0 - sb)).astype(dgb_ref.dtype)
        dg_ref[...] = (dybh * dgel).astype(dg_ref.dtype)
        row = lax.broadcasted_iota(jnp.int32, (t, D), 0)
        an = jnp.where(row == t - 1, af_ref[0:1, :], pltpu.roll(a, t - 1, 0))
        _blocked_scan(an, dyb * gel, gc_ref, an_ref, us_ref, c_ref, gs_ref, reverse=True)
        gfull = gs_ref[...]
        af_ref[0:1, :] = jnp.sum(jnp.where(row == 0, a, 0.0), axis=0, keepdims=True)
        hlast = jnp.where(i == n - 1, 0.0, hp_ref[7:8, :])
        hprev = jnp.where(row == 0, hlast, pltpu.roll(hv, 1, 0))
        gx = gfull * xrv
        dgi = gx * mult
        dla = a * (gfull * hprev - gx * gi * a * inv_mult)
        dgr = dla * (-LRU_C * sp)
        dsp = jnp.sum(dla * (-LRU_C * gr), axis=0, keepdims=True)
        dlam_ref[...] += dsp * (-_sig(-lam_ref[...]))
        dpr = dgr * gr * (1.0 - gr)
        dpi = dgi * gi * (1.0 - gi)
        dpr_ref[...] = dpr.astype(dpr_ref.dtype)
        dpi_ref[...] = dpi.astype(dpi_ref.dtype)
        dba_ref[...] += jnp.sum(dpr, axis=0, keepdims=True)
        dbx_ref[...] += jnp.sum(dpi, axis=0, keepdims=True)
        dxr_ref[...] = gfull * mult * gi + _mdot(dpr, wa_ref[...], 1, 1) + _mdot(dpi, wx_ref[...], 1, 1)

    hp_spec = pl.BlockSpec((8, D), lambda i: (jnp.maximum((n - 1 - i) * (t // 8) - 1, 0), 0))
    return _pcall(
        body, (dmerged, ya, xr, h, h, proj5, proj5, proj5, wa_bd, wx_bd, ba, bx, lam), name="lru_bwd", grid=(n,),
        in_specs=[rs(), rs(), rs(), rs(), hp_spec, rs(1), rs(3), rs(4), _full_spec((D, D)), _full_spec((D, D)),
                  _full_spec((1, D)), _full_spec((1, D)), _full_spec((1, D))],
        out_specs=[rs(), rs(), rs(), rs(), rs(), rs(), rs(), _full_spec((1, D)), _full_spec((1, D)),
                   _full_spec((1, D))],
        out_shape=[_sds((s, D)), _sds((s, D), MXU), _sds((s, D), MXU), _sds((s, D), MXU), _sds((s, D)),
                   _sds((s, D), MXU), _sds((s, D), MXU), _sds((1, D)), _sds((1, D)), _sds((1, D))],
        scratch_shapes=[pltpu.VMEM((8, D), F32), pltpu.VMEM((8, D), F32), pltpu.VMEM((t, D), F32),
                        pltpu.VMEM((t, D), F32), pltpu.VMEM((t // 8, D), F32), pltpu.VMEM((t, D), F32)],
        sem=("arbitrary",), plan=plan)


def _ssd_bwd(dya, y, proj5, xbc_c, dt, states, a_log, dskip_x, ssm_norm, expand, reduce_, plan=None):
    s = xbc_c.shape[0]
    nc = s // CH
    rv = functools.partial(_rev_spec, CH)

    def body(dya_ref, y_ref, z_ref, xc_ref, dt_ref, st_ref, alog_ref, dsk_ref, ng_ref, e_ref, et_ref, dz_ref,
             dxc_ref, ddt_ref, dng_ref, ddsk_ref, dalog_ref, dh_ref, at_ref, dtt_ref, dat_ref, ddtt_ref, dy_ref,
             yoffdy_ref, xbds_ref):
        @pl.when(pl.program_id(0) == 0)
        def _():
            dh_ref[...] = jnp.zeros_like(dh_ref)
            dng_ref[...] = jnp.zeros_like(dng_ref)
            ddsk_ref[...] = jnp.zeros_like(ddsk_ref)
            dalog_ref[...] = jnp.zeros_like(dalog_ref)

        cs = _ssd_chunk_setup(dt_ref, alog_ref, e_ref, at_ref, dtt_ref)
        lane, row = cs["lane"], cs["row"]
        et = et_ref[...]
        for g in range(NG):
            gs = slice(GW * g, GW * (g + 1))
            yv = y_ref[:, gs]
            zv = z_ref[:, gs]
            sz = _silu(zv)
            yg = yv * sz
            dyav = dya_ref[:, gs]
            dyg, dng = _rms_bwd(yg, _rms(yg), ng_ref[:, gs], dyav)
            dng_ref[:, gs] += dng
            dy_ref[:, gs] = dyg * sz
            dz_ref[:, gs] = (dyg * yv * _dsilu(zv)).astype(dz_ref.dtype)
        dyv = dy_ref[...]
        xs = xc_ref[:, 0:D]
        ddsk_ref[...] += jnp.sum(dyv * xs, axis=0, keepdims=True)
        dxc_ref[:, 0:D] = dyv * dsk_ref[...]
        dat_ref[...] = jnp.zeros_like(dat_ref)
        ddtt_ref[...] = jnp.zeros_like(ddtt_ref)
        hh = jnp.sum(dh_ref[...] * st_ref[0], axis=0, keepdims=True)
        deal = jnp.max(_xdot(jnp.broadcast_to(hh, (8, D)), et, 3), axis=0, keepdims=True)
        d_acum = jnp.zeros((CH, 128), F32)
        for g in range(NG):
            gs = slice(GW * g, GW * (g + 1))
            bs_ = slice(D + NS * g, D + NS * (g + 1))
            cs_ = slice(D + NG * NS + NS * g, D + NG * NS + NS * (g + 1))
            bg = xc_ref[:, bs_]
            cg = xc_ref[:, cs_]
            cb = _mdot(cg, bg, 1, 1)
            hg = st_ref[0, :, gs]
            dhg = dh_ref[:, gs]
            dyg_ = dy_ref[:, gs]
            xsg = xc_ref[:, gs]
            ea = cs["ea_x"][:, gs]
            wsx = cs["ws_x"][:, gs]
            dp = dyg_ * ea
            yoffdy_ref[:, gs] = dp * _mdot(cg, hg)
            dc = _mdot(dp, hg, 1, 1)
            dhprev = _mdot(cg, dp, 0, 0)
            bds = _mdot(bg, dhg)
            dxc_ref[:, gs] += wsx * bds
            xbds_ref[:, gs] = xsg * bds
            db = _mdot(xsg * wsx, dhg, 1, 1)
            dh_ref[:, gs] = dhprev + cs["eal_x"][:, gs] * dhg
            dcbs = jnp.zeros((CH, CH), F32)
            for j in range(4 * g, 4 * g + 4):
                ps = slice(128 * j, 128 * (j + 1))
                xp = xc_ref[:, ps]
                dyp = dy_ref[:, ps]
                dxacc = jnp.zeros((CH, 128), F32)
                for hf in range(2):
                    hd = 2 * j + hf
                    ld, rowdt = _head_decay(cs, at_ref, dtt_ref, hd)
                    hm = (lane >= HP) if hf else (lane < HP)
                    dym = jnp.where(hm, dyp, 0.0)
                    w = cb * ld * rowdt
                    dw = _mdot(dym, jnp.where(hm, xp, 0.0), 1, 1)
                    dxacc = dxacc + _mdot(w, dym, 0, 0)
                    nm = dw * w
                    ddtt_ref[hd:hd + 1, :] += jnp.sum(dw * cb * ld, axis=0, keepdims=True)
                    d_acum = d_acum + jnp.where(lane == hd, jnp.sum(nm, axis=1, keepdims=True), 0.0)
                    dat_ref[hd:hd + 1, :] -= jnp.sum(nm, axis=0, keepdims=True)
                    dcbs = dcbs + dw * ld * rowdt
                dxc_ref[:, ps] += dxacc
            dxc_ref[:, bs_] = db + _mdot(dcbs, cg, 0, 0)
            dxc_ref[:, cs_] = dc + _mdot(dcbs, bg)
        dws = _xdot(xbds_ref[...], et, 2)
        ws = cs["ws"]
        d_acum = d_acum - dws * ws + _xdot(yoffdy_ref[...], et, 2) + dat_ref[...].T
        d_alast = jnp.sum(dws * ws, axis=0, keepdims=True) + deal * cs["eal"]
        d_acum = d_acum + jnp.where(row == CH - 1, d_alast, 0.0)
        triu = row <= lane
        d_adt = _xdot(triu.astype(F32), d_acum, 3, split_b=True)
        ddt_ref[...] = dws * jnp.exp(cs["alast"] - cs["acum"]) + ddtt_ref[...].T + d_adt * cs["a"]
        dalog_ref[...] += jnp.sum(d_adt * cs["dtv"], axis=0, keepdims=True) * cs["a"]

    return _pcall(
        body, (dya, y, proj5, xbc_c, dt, states, a_log, dskip_x, ssm_norm, expand, reduce_), name="ssd_bwd",
        grid=(nc,),
        in_specs=[rv(D, nc), rv(D, nc), rv(D, nc, 0), rv(XBC, nc), rv(128, nc),
                  pl.BlockSpec((1, NS, D), lambda i: (nc - 1 - i, 0, 0)), _full_spec((1, 128)), _full_spec((1, D)),
                  _full_spec((1, D)), _full_spec((128, D)), _full_spec((D, 128))],
        out_specs=[rv(D, nc), rv(XBC, nc), rv(128, nc), _full_spec((1, D)), _full_spec((1, D)),
                   _full_spec((1, 128))],
        out_shape=[_sds((s, D), MXU), _sds((s, XBC)), _sds((s, 128)), _sds((1, D)), _sds((1, D)), _sds((1, 128))],
        scratch_shapes=[pltpu.VMEM((NS, D), F32), pltpu.VMEM((128, 128), F32), pltpu.VMEM((128, 128), F32),
                        pltpu.VMEM((128, 128), F32), pltpu.VMEM((128, 128), F32), pltpu.VMEM((CH, D), F32),
                        pltpu.VMEM((CH, D), F32), pltpu.VMEM((CH, D), F32)],
        sem=("arbitrary",), plan=plan)


def _conv_bwd(dxbc_c, dsilu, dxr, ddt, xbc_raw, proj5, dt_raw, cw_s, cw_l, dt_bias, plan=None):
    s = xbc_raw.shape[0]
    t = min(256, s)
    n = s // t

    def body(dxc_ref, dsl_ref, dxr_ref, ddt_ref, xs_ref, xl_ref, dtr_ref, cws_ref, cwl_ref, dtb_ref, dxs_ref,
             dxl_ref, ddtr_ref, dcws_ref, dcbs_ref, dcwl_ref, dcbl_ref, ddtb_ref, ds_ref, dl_ref):
        @pl.when(pl.program_id(0) == 0)
        def _():
            ds_ref[t:t + 8, :] = jnp.zeros((8, XBC), F32)
            dl_ref[t:t + 8, :] = jnp.zeros((8, D), F32)
            for r in (dcws_ref, dcbs_ref, dcwl_ref, dcbl_ref, ddtb_ref):
                r[...] = jnp.zeros_like(r)

        ds_ref[0:t, :] = dxc_ref[...] * dsl_ref[...].astype(F32)
        dl_ref[0:t, :] = dxr_ref[...]

        def back(dbuf, x_ref, w_ref, dx_ref, dw_ref, db_ref):
            xv = x_ref[...]
            dpre = dbuf[0:t, :]
            dx = w_ref[3:4, :] * dpre
            dw_ref[3:4, :] += jnp.sum(dpre * xv, axis=0, keepdims=True)
            db_ref[...] += jnp.sum(dpre, axis=0, keepdims=True)
            for k in (1, 2, 3):
                ahead = dbuf[k:t + k, :]
                dx = dx + w_ref[3 - k:4 - k, :] * ahead
                dw_ref[3 - k:4 - k, :] += jnp.sum(ahead * xv, axis=0, keepdims=True)
            dx_ref[...] = dx.astype(dx_ref.dtype)
            dbuf[t:t + 8, :] = dbuf[0:8, :]

        back(ds_ref, xs_ref, cws_ref, dxs_ref, dcws_ref, dcbs_ref)
        back(dl_ref, xl_ref, cwl_ref, dxl_ref, dcwl_ref, dcbl_ref)
        ddtr = ddt_ref[...] * _sig(dtr_ref[...] + dtb_ref[...])
        ddtr_ref[...] = ddtr.astype(ddtr_ref.dtype)
        ddtb_ref[...] += jnp.sum(ddtr, axis=0, keepdims=True)

    rv = functools.partial(_rev_spec, t)
    return _pcall(
        body, (dxbc_c, dsilu, dxr, ddt, xbc_raw, proj5, dt_raw, cw_s, cw_l, dt_bias), name="conv_bwd", grid=(n,),
        in_specs=[rv(XBC, n), rv(XBC, n), rv(D, n), rv(128, n), rv(XBC, n), rv(D, n, 2), rv(128, n),
                  _full_spec((4, XBC)), _full_spec((4, D)), _full_spec((1, 128))],
        out_specs=[rv(XBC, n), rv(D, n), rv(128, n), _full_spec((4, XBC)), _full_spec((1, XBC)), _full_spec((4, D)),
                   _full_spec((1, D)), _full_spec((1, 128))],
        out_shape=[_sds((s, XBC), MXU), _sds((s, D), MXU), _sds((s, 128), MXU), _sds((4, XBC)), _sds((1, XBC)),
                   _sds((4, D)), _sds((1, D)), _sds((1, 128))],
        scratch_shapes=[pltpu.VMEM((t + 8, XBC), F32), pltpu.VMEM((t + 8, D), F32)],
        sem=("arbitrary",), plan=plan)


def _du_norm(pieces5, dxbc, ddtr, w5, wxbc, wdt, x, dh1, g1, plan=None):
    s = x.shape[0]
    t = min(256, s)

    def body(p0, p1, p2, p3, p4, dxbc_ref, ddtr_ref, w5_ref, wx_ref, wd_ref, x_ref, dh1_ref, g1_ref, dx_ref, dg1_ref):
        @pl.when(pl.program_id(0) == 0)
        def _():
            dg1_ref[...] = jnp.zeros_like(dg1_ref)

        du = _mdot(dxbc_ref[...], wx_ref[...], 1, 1) + _mdot(ddtr_ref[...], wd_ref[...], 1, 1)
        for b, p in enumerate((p0, p1, p2, p3, p4)):
            du = du + _mdot(p[...], w5_ref[:, D * b:D * (b + 1)], 1, 1)
        xv = x_ref[...]
        dxn, dg1 = _rms_bwd(xv, _rms(xv), g1_ref[...], du)
        dx_ref[...] = dh1_ref[...] + dxn
        dg1_ref[...] += dg1

    return _pcall(
        body, (*pieces5, dxbc, ddtr, w5, wxbc, wdt, x, dh1, g1), name="du_norm", grid=(s // t,),
        in_specs=[_row_spec(t, D)] * 5 + [_row_spec(t, XBC), _row_spec(t, 128), _full_spec((D, 5 * D)),
                                          _full_spec((D, XBC)), _full_spec((D, 128)), _row_spec(t, D),
                                          _row_spec(t, D), _full_spec((1, D))],
        out_specs=[_row_spec(t, D), _full_spec((1, D))],
        out_shape=[_sds((s, D)), _sds((1, D))],
        sem=("arbitrary",), plan=plan)


def _adamw(w, g, m, v, name):
    r, c = w.shape
    t = r
    if r * c > 256 * 1024:
        t = next(cand for cand in (512, 256, 128, 64, 32, 16, 8) if r % cand == 0 and cand * c <= 512 * 1024)
    bc1 = 1.0 - ADAM_B1 ** ADAM_STEP
    bc2 = 1.0 - ADAM_B2 ** ADAM_STEP

    def body(w_ref, g_ref, m_ref, v_ref, d_ref, nm_ref, nv_ref):
        gv = g_ref[...]
        nm = ADAM_B1 * m_ref[...] + (1.0 - ADAM_B1) * gv
        nv = ADAM_B2 * v_ref[...] + (1.0 - ADAM_B2) * (gv * gv)
        nm_ref[...] = nm
        nv_ref[...] = nv
        d_ref[...] = -ADAM_LR * ((nm / bc1) / (jnp.sqrt(nv / bc2) + ADAM_EPS) + ADAM_WD * w_ref[...])

    spec = pl.BlockSpec((t, c), lambda i: (i, 0))
    return pl.pallas_call(
        body, name=name, grid=(r // t,), in_specs=[spec] * 4, out_specs=[spec] * 3,
        out_shape=[_sds((r, c))] * 3, compiler_params=_cp("parallel"),
    )(w, g, m, v)


def _half_blocks(shape, axis):
    r, c = shape
    if axis == 0:
        t = 256 if (r // 2) % 256 == 0 else 128
        nb = (r // 2) // t
        return (t, c), nb, (lambda i: (i, 0)), (lambda i: (i % nb, 0))
    nb = (c // 2) // 128
    return (r, 128), nb, (lambda i: (0, i)), (lambda i: (0, i % nb))


def _adamw_halves(w, g_mine, g_other, m, v, cidx, name, axis=0):
    r, c = w.shape
    blk, nb, whole, part = _half_blocks(w.shape, axis)
    bc1 = 1.0 - ADAM_B1 ** ADAM_STEP
    bc2 = 1.0 - ADAM_B2 ** ADAM_STEP

    def body(c_ref, w_ref, gm_ref, go_ref, m_ref, v_ref, g_ref, d_ref, nm_ref, nv_ref):
        mine = (pl.program_id(0) // nb) == c_ref[0]
        gv = jnp.where(mine, gm_ref[...], go_ref[...])
        g_ref[...] = gv
        nm = ADAM_B1 * m_ref[...] + (1.0 - ADAM_B1) * gv
        nv = ADAM_B2 * v_ref[...] + (1.0 - ADAM_B2) * (gv * gv)
        nm_ref[...] = nm
        nv_ref[...] = nv
        d_ref[...] = -ADAM_LR * ((nm / bc1) / (jnp.sqrt(nv / bc2) + ADAM_EPS) + ADAM_WD * w_ref[...])

    spec = pl.BlockSpec(blk, lambda i, c_ref: whole(i))
    half = pl.BlockSpec(blk, lambda i, c_ref: part(i))
    return pl.pallas_call(
        body, name=name,
        grid_spec=pltpu.PrefetchScalarGridSpec(num_scalar_prefetch=1, grid=(2 * nb,),
                                               in_specs=[spec, half, half, spec, spec], out_specs=[spec] * 4),
        out_shape=[_sds((r, c))] * 4, compiler_params=_cp("parallel"),
    )(cidx, w, g_mine, g_other, m, v)


def _block_diag(w):
    eye = jnp.eye(NH, dtype=w.dtype)
    return (w[:, :, None, :] * eye[:, None, :, None]).reshape(D, D)


def _diag_blocks(full):
    eye = jnp.eye(NH, dtype=full.dtype)
    return (full.reshape(NH, HP, NH, HP) * eye[:, None, :, None]).sum(axis=2)


def _pad_lanes(v, n=128):
    return jnp.pad(v, ((0, 0), (0, n - v.shape[1])))


def _local_step(x, target, p, dist=None):
    heads = jnp.arange(D, dtype=jnp.int32) // HP
    expand = (jnp.arange(128, dtype=jnp.int32)[:, None] == heads[None, :]).astype(F32)
    reduce_ = expand.T
    dskip_x = jnp.repeat(p["d_skip"], HP, axis=1)
    a_log = _pad_lanes(p["a_log"])
    dt_bias = _pad_lanes(p["dt_bias"])
    wa_bd = _block_diag(p["lru_wa"]).astype(MXU)
    wx_bd = _block_diag(p["lru_wx"]).astype(MXU)
    ba = p["lru_ba"].reshape(1, D)
    bx = p["lru_bx"].reshape(1, D)

    def hosted(key, fn):
        plan = dist.plan(key) if dist is not None else None
        if plan is None:
            return fn(plan=None)
        outs, got = fn(plan=plan)
        dist.done(key, got, p)
        return outs

    u = hosted("norm_u", functools.partial(_norm_cast, x, p["norm_mix_pre"], "norm_u"))
    w5, wxbc, wdt = p["w5"], p["wxbc"], p["wdt"]
    proj5 = hosted("proj5", functools.partial(_matmul, u, w5, name="proj5"))
    xbc_raw = _matmul(u, wxbc, name="proj_xbc", tn=XBC)
    dt_raw = _matmul(u, wdt, name="proj_dt")
    xbc_c, dsilu, xr, dt = _conv_fwd(xbc_raw, proj5, dt_raw, p["conv_ssm_w"], p["conv_ssm_b"], p["conv_lru_w"],
                                     p["conv_lru_b"], dt_bias)
    y, ya, states = _ssd_fwd(xbc_c, dt, proj5, a_log, dskip_x, p["ssm_norm"], expand)
    h, merged = hosted("lru_fwd", functools.partial(_lru_fwd, xr, proj5, ya, wa_bd, wx_bd, ba, bx, p["lru_lambda"]))
    mix, h1, v = _out_proj(merged, p["w_out"], x, p["norm_mix_post"], p["norm_mlp_pre"])
    pre = _matmul(v, p["w_up"], name="up_proj", tm=1024, out_dtype=MXU)
    dout, dff, loss, dg4 = _down_loss(pre, p["w_down"], h1, target, p["norm_mlp_post"])

    dpre = _matmul(dff, p["w_down"], name="d_pre", tb=True, tm=1024, out_dtype=MXU,
                   epi=lambda r, pr: r * (2.0 * jnp.maximum(pr.astype(F32), 0.0)), epi_args=(pre,))
    g_w_down = _matmul(pre, dff, name="dw_down", ta=True, tm=1024, tn=1024, tk=TK_GRAD, a_fn=_relu2)
    dh1, dmix, dg3, dg2 = _dv_norms(dpre, p["w_up"], h1, mix, dout, p["norm_mlp_pre"], p["norm_mix_post"])
    g_w_up = _matmul(v, dpre, name="dw_up", ta=True, tm=1024, tn=1024, tk=TK_GRAD)
    dmerged = _matmul(dmix, p["w_out"], name="d_merged", tb=True)
    g_w_out = _matmul(merged, dmix, name="dw_out", ta=True, tm=1024, tn=1024, tk=TK_GRAD)
    if dist is not None:
        dist.early_grads(w_down=g_w_down, w_up=g_w_up, w_out=g_w_out)
    (dya, dga, dgb, dg, dxr, dpr, dpi, dlam, dba, dbx) = hosted("lru_bwd", functools.partial(
        _lru_bwd, dmerged, ya, xr, h, proj5, wa_bd, wx_bd, ba, bx, p["lru_lambda"]))
    g_wa = _diag_blocks(_matmul(xr, dpr, name="dw_lru_a", ta=True, tm=1024, tn=1024, tk=TK_GRAD))
    g_wx = _diag_blocks(_matmul(xr, dpi, name="dw_lru_x", ta=True, tm=1024, tn=1024, tk=TK_GRAD))
    dz, dxbc_c, ddt, dng, ddsk, dalog = hosted("ssd_bwd", functools.partial(
        _ssd_bwd, dya, y, proj5, xbc_c, dt, states, a_log, dskip_x, p["ssm_norm"], expand, reduce_))
    (dxbc, dxl, ddtr, dcws, dcbs, dcwl, dcbl, ddtb) = hosted("conv_bwd", functools.partial(
        _conv_bwd, dxbc_c, dsilu, dxr, ddt, xbc_raw, proj5, dt_raw, p["conv_ssm_w"], p["conv_lru_w"], dt_bias))
    pieces5 = (dz, dg, dxl, dga, dgb)
    gw5 = [_matmul(pc, u, name=f"dw_in_{i}", ta=True, tm=1024, tn=1024, tk=TK_GRAD) for i, pc in enumerate(pieces5)]
    gwxbc = _matmul(dxbc, u, name="dw_in_xbc", ta=True, tm=XBC, tn=1024, tk=TK_GRAD)
    gwdt = _matmul(ddtr, u, name="dw_in_dt", ta=True, tm=128, tn=1024, tk=TK_GRAD)
    g_w_in_t = jnp.concatenate([gw5[0], gwxbc, gwdt[:NH], gw5[1], gw5[2], gw5[3], gw5[4]], axis=0)
    grads = {
        "w_in_t": g_w_in_t, "conv_ssm_w": dcws, "conv_ssm_b": dcbs, "dt_bias": ddtb[:, :NH],
        "a_log": dalog[:, :NH], "d_skip": ddsk.reshape(NH, HP).sum(axis=1)[None, :], "ssm_norm": dng,
        "conv_lru_w": dcwl, "conv_lru_b": dcbl, "lru_wa": g_wa, "lru_ba": dba.reshape(NH, HP), "lru_wx": g_wx,
        "lru_bx": dbx.reshape(NH, HP), "lru_lambda": dlam, "w_out": g_w_out, "norm_mix_post": dg2,
        "norm_mlp_pre": dg3, "w_up": g_w_up, "w_down": g_w_down, "norm_mlp_post": dg4,
    }
    if dist is not None:
        dist.late_grads(grads, loss[0, 0])
    grad_x, grads["norm_mix_pre"] = hosted("du_norm", functools.partial(
        _du_norm, pieces5, dxbc, ddtr, w5, wxbc, wdt, x, dh1, p["norm_mix_pre"]))
    return loss[0, 0], grad_x, grads


def _split_w_in(w_in_full):
    z, xbc, dtc, g, xl, ga, gb = jnp.split(w_in_full, [D, D + XBC, D + XBC + NH, 2 * D + XBC + NH,
                                                        3 * D + XBC + NH, 4 * D + XBC + NH], axis=1)
    return jnp.concatenate([z, g, xl, ga, gb], axis=1), xbc, _pad_lanes(dtc)


COMM = BF16


def _place():
    x, y, c = lax.axis_index("x"), lax.axis_index("y"), lax.axis_index("c")
    chips = [(1 - x, y), (x, 1 - y), (1 - x, 1 - y)]
    return x, y, c, chips


def _remote(src, dst, send_sem, recv_sem, to):
    return pltpu.make_async_remote_copy(src_ref=src, dst_ref=dst, send_sem=send_sem, recv_sem=recv_sem, device_id=to,
                                        device_id_type=MESH)


def _gather_plan(big, small=()):
    nb = len(big)
    arrs = list(big) + list(small)
    na = len(arrs)

    def direct(ins, outs, send, recv):
        x, y, c, chips = _place()
        k = 2 * x + y
        cps = []
        for a in range(na):
            if a < nb:
                hr = ins[a].shape[0] // 2
                src, dst = ins[a].at[pl.ds(c * hr, hr)], outs[a].at[k, pl.ds(c * hr, hr)]
            else:
                src, dst = ins[a], outs[a].at[k]
            cps += [_remote(src, dst, send.at[a, j], recv.at[a, j], (cx, cy, c)) for j, (cx, cy) in enumerate(chips)]
        return cps

    def start(ins, outs, sems):
        for cp in direct(ins, outs, *sems[0]):
            cp.start()

    def finish(ins, outs, sems):
        send, recv = sems[0]
        x, y, c, chips = _place()
        passed = []
        for j, (cx, cy) in enumerate(chips):
            kj = 2 * cx + cy
            for a in range(na):
                if a < nb:
                    hr = ins[a].shape[0] // 2
                    got = outs[a].at[kj, pl.ds(c * hr, hr)]
                    _remote(got, got, send.at[a, j], recv.at[a, j], (cx, cy, c)).wait_recv()
                    passed.append(_remote(got, got, send.at[a, 3 + j], recv.at[a, 3 + j], (x, y, 1 - c)))
                    passed[-1].start()
                else:
                    got = outs[a].at[kj]
                    _remote(got, got, send.at[a, j], recv.at[a, j], (cx, cy, c)).wait_recv()
        for j, (cx, cy) in enumerate(chips):
            for a in range(nb):
                hr = ins[a].shape[0] // 2
                got = outs[a].at[2 * cx + cy, pl.ds((1 - c) * hr, hr)]
                _remote(got, got, send.at[a, 3 + j], recv.at[a, 3 + j], (x, y, 1 - c)).wait_recv()
        for cp in direct(ins, outs, send, recv) + passed:
            cp.wait_send()

    return _Plan(arrs, [_sds((NCHIP,) + a.shape, a.dtype) for a in arrs], [(na, 6)], start, finish)


def _own_shards(gathered, shards):
    kchip = 2 * lax.axis_index("x") + lax.axis_index("y")
    return [lax.dynamic_update_index_in_dim(o, a, kchip, 0) for o, a in zip(gathered, shards)]


def _swap_plan(ins, outs, sems, copies):
    def start(i, o, s):
        for cp in copies(i, o, *s[0]):
            cp.start()

    def finish(i, o, s):
        for cp in copies(i, o, *s[0]):
            cp.wait()

    return _Plan(ins, outs, [sems], start, finish)


def _half_shape(shape, axis):
    return tuple(d // 2 if i == axis else d for i, d in enumerate(shape))


def _pair_exchange_plan(gs, axis=1):
    def copies(ins, outs, send, recv):
        x, y, c, _ = _place()
        cps = []
        for a in range(len(gs)):
            h = ins[a].shape[axis] // 2
            theirs = pl.ds((1 - c) * h, h)
            src = ins[a].at[:, theirs] if axis == 1 else ins[a].at[:, :, theirs]
            cps.append(_remote(src, outs[a], send.at[a], recv.at[a], (x, y, 1 - c)))
        return cps

    return _swap_plan(gs, [_sds(_half_shape(g.shape, axis), g.dtype) for g in gs], (len(gs),), copies)


def _pair_add(g, got, cidx, name, axis=1):
    half = _half_shape(g.shape, axis)
    blk, nt, _, part = _half_blocks(g.shape[1:], axis - 1)

    def body(c_ref, g_ref, o_ref, p_ref, pc_ref):
        sm = g_ref[...] + o_ref[...]
        p_ref[...] = sm
        pc_ref[...] = sm.astype(pc_ref.dtype)

    def mine(k, i, c_ref):
        j = c_ref[0] * nt + i
        return (k, j, 0) if axis == 1 else (k, 0, j)

    spec = pl.BlockSpec((1,) + blk, lambda k, i, c_ref: (k,) + part(i))
    return pl.pallas_call(
        body, name=name,
        grid_spec=pltpu.PrefetchScalarGridSpec(
            num_scalar_prefetch=1, grid=(NCHIP, nt),
            in_specs=[pl.BlockSpec((1,) + blk, mine), spec], out_specs=[spec, spec]),
        out_shape=[_sds(half), _sds(half, COMM)],
        compiler_params=_cp("parallel", "parallel"),
    )(cidx, g, got)


def _chip_exchange_plan(ps):
    def copies(ins, outs, send, recv):
        _, _, c, chips = _place()
        return [_remote(ins[a].at[2 * cx + cy], outs[a].at[j], send.at[a, j], recv.at[a, j], (cx, cy, c))
                for a in range(len(ps)) for j, (cx, cy) in enumerate(chips)]

    return _swap_plan(ps, [_sds((NCHIP - 1,) + p.shape[1:], p.dtype) for p in ps], (len(ps), 3), copies)


def _shard_sum(p, got, kidx, name, axis=1):
    full = tuple(2 * d if i == axis - 1 else d for i, d in enumerate(p.shape[1:]))
    blk, nt, _, part = _half_blocks(full, axis - 1)

    def body(k_ref, p_ref, g_ref, o_ref):
        sm = p_ref[0]
        for j in range(NCHIP - 1):
            sm = sm + g_ref[j].astype(F32)
        o_ref[...] = sm

    return pl.pallas_call(
        body, name=name,
        grid_spec=pltpu.PrefetchScalarGridSpec(
            num_scalar_prefetch=1, grid=(nt,),
            in_specs=[pl.BlockSpec((1,) + blk, lambda i, k_ref: (k_ref[0],) + part(i)),
                      pl.BlockSpec((NCHIP - 1,) + blk, lambda i, k_ref: (0,) + part(i))],
            out_specs=pl.BlockSpec(blk, lambda i, k_ref: part(i))),
        out_shape=_sds(p.shape[1:]),
        compiler_params=_cp("parallel"),
    )(kidx, p, got)


def _pair_swap_plan(rs):
    def copies(ins, outs, send, recv):
        x, y, c, _ = _place()
        return [_remote(ins[a], outs[a], send.at[a], recv.at[a], (x, y, 1 - c)) for a in range(len(rs))]

    return _swap_plan(rs, [_sds(r.shape, r.dtype) for r in rs], (len(rs),), copies)


def _allgather8_plan(v):
    def pieces(ins, outs, send, recv):
        x, y, c, chips = _place()
        me, sibling = (x, y, c), (x, y, 1 - c)

        def copy(k, block, to, src=None):
            px, py, pc = block
            slot = outs[0].at[4 * px + 2 * py + pc]
            return _remote(slot if src is None else src, slot, send.at[k], recv.at[k], to)

        first = [copy(0, me, sibling, src=ins[0])] + [copy(1 + j, me, (*chip, c), src=ins[0])
                                                      for j, chip in enumerate(chips)]
        passed = [copy(4 + j, (*chip, c), sibling) for j, chip in enumerate(chips)]
        arrivals = [copy(1 + j, (*chip, c), me) for j, chip in enumerate(chips)]
        late = [copy(0, sibling, me)] + [copy(4 + j, (*chip, 1 - c), me) for j, chip in enumerate(chips)]
        return first, passed, arrivals, late

    def start(ins, outs, sems):
        for cp in pieces(ins, outs, *sems[0])[0]:
            cp.start()

    def finish(ins, outs, sems):
        first, passed, arrivals, late = pieces(ins, outs, *sems[0])
        for got, fwd in zip(arrivals, passed):
            got.wait_recv()
            fwd.start()
        for got in late:
            got.wait_recv()
        for cp in first + passed:
            cp.wait_send()

    return _Plan([v], [_sds((8,) + v.shape, v.dtype)], [(7,)], start, finish)


def _own_block(gathered, v):
    me = 4 * lax.axis_index("x") + 2 * lax.axis_index("y") + lax.axis_index("c")
    return lax.dynamic_update_index_in_dim(gathered, v, me, 0)


def _sum_devices(allv, name):
    _, r, _ = allv.shape

    def body(a_ref, o_ref):
        sm = a_ref[0]
        for d in range(1, 8):
            sm = sm + a_ref[d]
        o_ref[...] = sm

    return pl.pallas_call(
        body, name=name, grid=(1,), in_specs=[_full_spec((8, r, 128))], out_specs=_full_spec((r, 128)),
        out_shape=_sds((r, 128)), compiler_params=_cp("arbitrary"),
    )(allv)


def _pack(arrs):
    flat = jnp.concatenate([a.reshape(-1) for a in arrs])
    return jnp.pad(flat, (0, (-flat.shape[0]) % 1024)).reshape(-1, 128)


def _unpack(packed, shapes):
    flat, outs, off = packed.reshape(-1), [], 0
    for shp in shapes:
        n = math.prod(shp)
        outs.append(flat[off:off + n].reshape(shp))
        off += n
    return outs


BIG = ("w_in", "w_out", "w_up", "w_down")
CONV = ("conv_ssm_w", "conv_lru_w")
WEIGHTS = ("norm_mix_pre", "w_in", "conv_ssm_w", "conv_ssm_b", "dt_bias", "a_log", "d_skip", "ssm_norm", "conv_lru_w",
           "conv_lru_b", "lru_wa", "lru_ba", "lru_wx", "lru_bx", "lru_lambda", "w_out", "norm_mix_post",
           "norm_mlp_pre", "w_up", "w_down", "norm_mlp_post")
SMALL = tuple(n for n in WEIGHTS if n not in BIG and n not in CONV)
EARLY = ("w_down", "w_up", "w_out")


def _cat_cols(g):
    return jnp.concatenate([g[k] for k in range(NCHIP)], axis=1)


class _Dist:
    def __init__(self, shards, first, cidx, kidx):
        self.shards, self.first, self.cidx, self.kidx = shards, first, cidx, kidx
        self.halves = {}

    def early_grads(self, w_down, w_up, w_out):
        self.shard_major = [w_down.reshape(NCHIP, D, D), jnp.stack([w_up[:, D * k:D * (k + 1)] for k in range(NCHIP)]),
                            w_out.reshape(NCHIP, D // NCHIP, D)]

    def late_grads(self, grads, loss):
        g_in = grads["w_in_t"].reshape(NCHIP, W_IN_SHARD, D)
        got, = _run_plan(_pair_exchange_plan([g_in], axis=2), "grad_pair_exchange_w_in")
        self.p_in, self.pc_in = _pair_add(g_in, got, self.cidx, "grad_pair_add_w_in", axis=2)
        self.small_names = [n for n in SMALL + CONV if n != "norm_mix_pre"]
        self.small_shapes = [grads[n].shape for n in self.small_names] + [(1,)]
        self.packed_small = _pack([grads[n] for n in self.small_names] + [loss.reshape(1)])

    def plan(self, key):
        if key == "norm_u":
            return _gather_plan(self.first[:1], self.first[1:])
        if key == "proj5":
            return _gather_plan([self.shards["w_out"], self.shards["w_up"]])
        if key == "lru_fwd":
            return _gather_plan([self.shards["w_down"]])
        if key == "lru_bwd":
            return _pair_exchange_plan(self.shard_major)
        if key == "ssd_bwd":
            return _chip_exchange_plan([pc for _, pc in self.pair])
        if key == "conv_bwd":
            return _pair_swap_plan(self.mine)
        if key == "du_norm":
            return _merge_plans(_allgather8_plan(self.packed_small), _chip_exchange_plan([self.pc_in]))
        return None

    def done(self, key, got, p):
        if key == "norm_u":
            g_in, g_cs, g_cl = _own_shards(got, self.first)
            w5, wxbc, wdt = _split_w_in(_cat_cols(g_in))
            p.update(w5=w5, wxbc=wxbc, wdt=wdt, conv_ssm_w=_cat_cols(g_cs), conv_lru_w=_cat_cols(g_cl))
        elif key == "proj5":
            g_out, g_up = _own_shards(got, [self.shards["w_out"], self.shards["w_up"]])
            p.update(w_out=g_out.reshape(D, D), w_up=_cat_cols(g_up))
        elif key == "lru_fwd":
            g_down, = _own_shards(got, [self.shards["w_down"]])
            p.update(w_down=g_down.reshape(DFF, D))
        elif key == "lru_bwd":
            self.pair = [_pair_add(gs, o, self.cidx, f"grad_pair_add_{n}")
                         for gs, o, n in zip(self.shard_major, got, EARLY)]
        elif key == "ssd_bwd":
            self.mine = [_shard_sum(pf, o, self.kidx, f"grad_shard_sum_{n}")
                         for (pf, _), o, n in zip(self.pair, got, EARLY)]
        elif key == "conv_bwd":
            self.halves = {n: (mine, other) for n, mine, other in zip(EARLY, self.mine, got)}
        elif key == "du_norm":
            self.all_small, self.from_chips_in = got


def kernel(x, norm_mix_pre, w_in, conv_ssm_w, conv_ssm_b, dt_bias, a_log, d_skip, ssm_norm, conv_lru_w, conv_lru_b, lru_wa, lru_ba, lru_wx, lru_bx, lru_lambda, w_out, norm_mix_post, norm_mlp_pre, w_up, w_down, norm_mlp_post, loss_target, m_norm_mix_pre, m_w_in, m_conv_ssm_w, m_conv_ssm_b, m_dt_bias, m_a_log, m_d_skip, m_ssm_norm, m_conv_lru_w, m_conv_lru_b, m_lru_wa, m_lru_ba, m_lru_wx, m_lru_bx, m_lru_lambda, m_w_out, m_norm_mix_post, m_norm_mlp_pre, m_w_up, m_w_down, m_norm_mlp_post, v_norm_mix_pre, v_w_in, v_conv_ssm_w, v_conv_ssm_b, v_dt_bias, v_a_log, v_d_skip, v_ssm_norm, v_conv_lru_w, v_conv_lru_b, v_lru_wa, v_lru_ba, v_lru_wx, v_lru_bx, v_lru_lambda, v_w_out, v_norm_mix_post, v_norm_mlp_pre, v_w_up, v_w_down, v_norm_mlp_post):
    args = locals()
    w = {n: args[n][0] for n in WEIGHTS}
    m = {n: args["m_" + n][0] for n in WEIGHTS}
    v = {n: args["v_" + n][0] for n in WEIGHTS}
    cidx = lax.axis_index("c").astype(jnp.int32).reshape(1)
    kchip = 2 * lax.axis_index("x") + lax.axis_index("y")
    shards = {n: w[n].astype(MXU) for n in BIG}
    dist = _Dist(shards, [shards["w_in"], w["conv_ssm_w"], w["conv_lru_w"]], cidx, kchip.astype(jnp.int32).reshape(1))
    p = {n: (w[n].reshape(1, -1) if w[n].ndim == 1 else w[n]) for n in SMALL}

    _, grad_x, g = _local_step(x[0], loss_target[0], p, dist)

    half_in = _shard_sum(dist.p_in, dist.from_chips_in, dist.kidx, "grad_shard_sum_w_in", axis=2)
    packed_g1 = _pack([g["norm_mix_pre"]])
    all_g1, other_in = _run_plan(_merge_plans(_allgather8_plan(packed_g1), _pair_swap_plan([half_in])),
                                 "grad_pair_swap_w_in")
    halves = dist.halves

    reduced = {}
    *summed, loss = _unpack(_sum_devices(_own_block(dist.all_small, dist.packed_small), "small_sum"),
                            dist.small_shapes)
    loss = loss.reshape(())
    g1, = _unpack(_sum_devices(_own_block(all_g1, packed_g1), "small_sum_norm_mix_pre"), [g["norm_mix_pre"].shape])
    for n, s in zip(dist.small_names + ["norm_mix_pre"], summed + [g1]):
        if n in CONV:
            width = w[n].shape[1]
            reduced[n] = lax.dynamic_slice_in_dim(s, kchip * width, width, axis=1)
        else:
            reduced[n] = s.reshape(w[n].shape)

    delta, new_m, new_v = {}, {}, {}
    for n in EARLY:
        mine, other = halves[n]
        reduced[n], delta[n], new_m[n], new_v[n] = _adamw_halves(w[n], mine, other, m[n], v[n], cidx, f"adamw_{n}")
    to_t = lambda a: jnp.transpose(a, (2, 0, 1)).reshape(W_IN_SHARD, D)
    from_t = lambda a: jnp.transpose(a.reshape(W_IN_SHARD, 1, D), (1, 2, 0))
    outs_t = _adamw_halves(to_t(w_in), half_in, other_in, to_t(m_w_in), to_t(v_w_in), cidx, "adamw_w_in", axis=1)
    for d, o in zip((reduced, delta, new_m, new_v), outs_t):
        d["w_in"] = from_t(o)[0]
    for n in CONV:
        delta[n], new_m[n], new_v[n] = _adamw(w[n], reduced[n], m[n], v[n], f"adamw_{n}")
    shapes = [w[n].shape for n in SMALL]
    packed = [_pack([d[n] for n in SMALL]) for d in (w, reduced, m, v)]
    for d, out in zip((delta, new_m, new_v), _adamw(*packed, "adamw_small")):
        d.update(zip(SMALL, _unpack(out, shapes)))

    lead = lambda d: [d[n][None] for n in WEIGHTS]
    return (loss, grad_x[None], *lead(reduced), *lead(delta), *lead(new_m), *lead(new_v))
```

```python
import functools
import math

import jax
import jax.numpy as jnp
from jax import lax
from jax.experimental import pallas as pl
from jax.experimental.pallas import tpu as pltpu

F32 = jnp.float32
BF16 = jnp.bfloat16
MXU = BF16

D = 1024
DFF = 4096
NH = 16
HP = 64
NG = 2
NS = 128
CH = 128
XBC = D + 2 * NG * NS
GW = D // NG
LRU_C = 8.0
EPS = 1e-6
NCHIP = 4
W_IN_COLS = 6672
W_IN_SHARD = W_IN_COLS // NCHIP

ADAM_LR = 0.001
ADAM_B1 = 0.9
ADAM_B2 = 0.999
ADAM_EPS = 1e-08
ADAM_WD = 0.01
ADAM_STEP = 10

VMEM_LIMIT = 56 * 1024 * 1024
TK_GRAD = 2048
MID_AT = 0.7
MESH = pl.DeviceIdType.MESH


def _cp(*sem):
    return pltpu.CompilerParams(dimension_semantics=sem, vmem_limit_bytes=VMEM_LIMIT)


def _dot(a, b, ca=1, cb=0, prec=None):
    return lax.dot_general(a, b, (((ca,), (cb,)), ((), ())), precision=prec, preferred_element_type=F32)


def _mdot(a, b, ca=1, cb=0):
    return _dot(a.astype(MXU), b.astype(MXU), ca, cb)


def _bf16_parts(v, n):
    parts = []
    for i in range(n):
        p = v.astype(BF16)
        parts.append(p)
        if i < n - 1:
            v = v - p.astype(F32)
    return parts


def _xdot(a, b, passes, split_b=False):
    if split_b:
        a16 = a.astype(BF16)
        terms = [_dot(a16, p) for p in _bf16_parts(b, passes)]
    else:
        b16 = b.astype(BF16)
        terms = [_dot(p, b16) for p in _bf16_parts(a, passes)]
    return functools.reduce(lambda u, v: u + v, terms)


def _sig(x):
    return 0.5 * jnp.tanh(0.5 * x) + 0.5


def _silu(x):
    return x * _sig(x)


def _dsilu(x):
    s = _sig(x)
    return s * (1.0 + x * (1.0 - s))


def _softplus(x):
    e = jnp.exp(-jnp.abs(x))
    return jnp.maximum(x, 0.0) + jnp.where(e < 1e-4, e * (1.0 - 0.5 * e), jnp.log(1.0 + e))


_GELU_C = math.sqrt(2.0 / math.pi)


def _gelu(x):
    t = jnp.tanh(_GELU_C * (x + 0.044715 * x * x * x))
    return 0.5 * x * (1.0 + t)


def _gelu_and_grad(x):
    x2 = x * x
    t = jnp.tanh(_GELU_C * (x + 0.044715 * x * x2))
    half = 0.5 * (1.0 + t)
    return x * half, half + 0.5 * x * (1.0 - t * t) * _GELU_C * (1.0 + 3.0 * 0.044715 * x2)


def _one_minus_sq(a, la):
    x = 2.0 * la
    series = -x * (1.0 + x * (0.5 + x * (1.0 / 6.0)))
    return jnp.where(x > -0.01, series, 1.0 - a * a)


def _rms(x):
    return lax.rsqrt(jnp.mean(x * x, axis=-1, keepdims=True) + EPS)


def _rms_bwd(x, r, g, dy):
    xn = x * r
    dxh = dy * g
    m = jnp.mean(dxh * xn, axis=-1, keepdims=True)
    return r * (dxh - xn * m), jnp.sum(dy * xn, axis=0, keepdims=True)


def _row_spec(t, c, col=0):
    return pl.BlockSpec((t, c), lambda i: (i, col))


def _rev_spec(t, c, n, col=0):
    return pl.BlockSpec((t, c), lambda i: (n - 1 - i, col))


def _full_spec(shape):
    nd = len(shape)
    return pl.BlockSpec(shape, lambda *_: (0,) * nd)


def _sds(shape, dtype=F32):
    return jax.ShapeDtypeStruct(shape, dtype)


ANY = pl.BlockSpec(memory_space=pl.ANY)


class _Plan:
    def __init__(self, ins, outs, sems, start, finish, mid=None):
        self.ins, self.outs, self.sems = list(ins), list(outs), list(sems)
        self.start, self.finish, self.mid = start, finish, mid or (lambda i, o, s: None)


def _merge_plans(*plans):
    def each(fn_name, ins, outs, sems):
        i = o = s = 0
        for p in plans:
            getattr(p, fn_name)(ins[i:i + len(p.ins)], outs[o:o + len(p.outs)], sems[s:s + len(p.sems)])
            i, o, s = i + len(p.ins), o + len(p.outs), s + len(p.sems)

    return _Plan([a for p in plans for a in p.ins], [a for p in plans for a in p.outs],
                 [a for p in plans for a in p.sems], functools.partial(each, "start"),
                 functools.partial(each, "finish"), functools.partial(each, "mid"))


def _pcall(body, args, *, name, grid, in_specs, out_specs, out_shape, sem, scratch_shapes=(), plan=None):
    single = not isinstance(out_shape, (list, tuple))
    out_specs = [out_specs] if single else list(out_specs)
    out_shape = [out_shape] if single else list(out_shape)
    if plan is None:
        outs = pl.pallas_call(body, name=name, grid=grid, in_specs=list(in_specs), out_specs=out_specs,
                              out_shape=out_shape, scratch_shapes=list(scratch_shapes),
                              compiler_params=_cp(*sem))(*args)
        return outs[0] if single else outs
    n_in, n_out, n_sc, ni, no = len(in_specs), len(out_shape), len(scratch_shapes), len(plan.ins), len(plan.outs)

    def hosted(*refs):
        b0 = n_in + ni
        b1 = b0 + n_out + no
        sem_refs = refs[b1 + n_sc:]
        sems = [(sem_refs[2 * q], sem_refs[2 * q + 1]) for q in range(len(plan.sems))]
        step = functools.reduce(lambda lin, ig: lin * ig[1] + ig[0],
                                [(pl.program_id(d), g) for d, g in enumerate(grid)], 0)
        total = math.prod(grid)

        @pl.when(step == 0)
        def _():
            plan.start(refs[n_in:b0], refs[b0 + n_out:b1], sems)

        body(*refs[:n_in], *refs[b0:b0 + n_out], *refs[b1:b1 + n_sc])

        @pl.when(step == min(int(MID_AT * total), total - 1))
        def _():
            plan.mid(refs[n_in:b0], refs[b0 + n_out:b1], sems)

        @pl.when(step == total - 1)
        def _():
            plan.finish(refs[n_in:b0], refs[b0 + n_out:b1], sems)

    dma = [pltpu.SemaphoreType.DMA(shape) for shape in plan.sems for _ in range(2)]
    outs = pl.pallas_call(hosted, name=name, grid=grid, in_specs=list(in_specs) + [ANY] * ni,
                          out_specs=out_specs + [ANY] * no, out_shape=out_shape + plan.outs,
                          scratch_shapes=list(scratch_shapes) + dma,
                          compiler_params=_cp(*("arbitrary",) * len(grid)))(*args, *plan.ins)
    return (outs[0] if single else outs[:n_out]), outs[n_out:]


def _run_plan(plan, name):
    ni, no = len(plan.ins), len(plan.outs)

    def body(*refs):
        sem_refs = refs[ni + no:]
        sems = [(sem_refs[2 * q], sem_refs[2 * q + 1]) for q in range(len(plan.sems))]
        plan.start(refs[:ni], refs[ni:ni + no], sems)
        plan.mid(refs[:ni], refs[ni:ni + no], sems)
        plan.finish(refs[:ni], refs[ni:ni + no], sems)

    return pl.pallas_call(
        body, name=name, in_specs=[ANY] * ni, out_specs=[ANY] * no, out_shape=plan.outs,
        scratch_shapes=[pltpu.SemaphoreType.DMA(shape) for shape in plan.sems for _ in range(2)],
    )(*plan.ins)


def _matmul(a, b, *, name, ta=False, tb=False, tm=512, tn=1024, tk=1024, out_dtype=F32, a_fn=None, epi=None,
            epi_args=(), plan=None):
    m, k = (a.shape[1], a.shape[0]) if ta else a.shape
    n = b.shape[0] if tb else b.shape[1]
    tm, tn, tk = min(tm, m), min(tn, n), min(tk, k)
    nk = k // tk
    a_spec = pl.BlockSpec((tk, tm), lambda i, j, kk: (kk, i)) if ta else pl.BlockSpec((tm, tk), lambda i, j, kk: (i, kk))
    b_spec = pl.BlockSpec((tn, tk), lambda i, j, kk: (j, kk)) if tb else pl.BlockSpec((tk, tn), lambda i, j, kk: (kk, j))
    e_specs = [pl.BlockSpec((tm, tn), lambda i, j, kk: (i, j)) for _ in epi_args]
    ne = len(epi_args)

    def body(a_ref, b_ref, *rest):
        e_refs, o_ref = rest[:ne], rest[ne]
        av = a_ref[...]
        if a_fn is not None:
            av = a_fn(av)
        part = _mdot(av, b_ref[...], 0 if ta else 1, 1 if tb else 0)

        def finish(r):
            if epi is not None:
                r = epi(r, *[e[...] for e in e_refs])
            o_ref[...] = r.astype(o_ref.dtype)

        if nk == 1:
            finish(part)
            return
        acc_ref = rest[ne + 1]
        kk = pl.program_id(2)

        @pl.when(kk == 0)
        def _():
            acc_ref[...] = part

        @pl.when(jnp.logical_and(kk > 0, kk < nk - 1))
        def _():
            acc_ref[...] += part

        @pl.when(kk == nk - 1)
        def _():
            finish(acc_ref[...] + part)

    return _pcall(
        body, (a, b, *epi_args), name=name, grid=(m // tm, n // tn, nk),
        in_specs=[a_spec, b_spec] + e_specs,
        out_specs=pl.BlockSpec((tm, tn), lambda i, j, kk: (i, j)),
        out_shape=_sds((m, n), out_dtype),
        scratch_shapes=[pltpu.VMEM((tm, tn), F32)] if nk > 1 else [],
        sem=("parallel", "parallel", "arbitrary"), plan=plan)


def _relu2(p):
    p = jnp.maximum(p.astype(F32), 0.0)
    return p * p


def _norm_cast(x, g, name, plan=None):
    s = x.shape[0]
    t = min(512, s)

    def body(x_ref, g_ref, o_ref):
        xv = x_ref[...]
        o_ref[...] = (xv * _rms(xv) * g_ref[...]).astype(o_ref.dtype)

    return _pcall(body, (x, g), name=name, grid=(s // t,), in_specs=[_row_spec(t, D), _full_spec((1, D))],
                  out_specs=_row_spec(t, D), out_shape=_sds((s, D), MXU), sem=("parallel",), plan=plan)


def _conv_fwd(xbc_raw, proj5, dt_raw, cw_s, cb_s, cw_l, cb_l, dt_bias):
    s = xbc_raw.shape[0]
    t = min(256, s)

    def body(xs_ref, xl_ref, dtr_ref, cws_ref, cbs_ref, cwl_ref, cbl_ref, dtb_ref, xc_ref, dsl_ref, xr_ref, dt_ref,
             bs_ref, bl_ref):
        @pl.when(pl.program_id(0) == 0)
        def _():
            bs_ref[0:8, :] = jnp.zeros((8, XBC), F32)
            bl_ref[0:8, :] = jnp.zeros((8, D), F32)

        bs_ref[8:t + 8, :] = xs_ref[...]
        bl_ref[8:t + 8, :] = xl_ref[...]

        def conv(buf, w_ref, b_ref):
            acc = b_ref[...] + w_ref[3:4, :] * buf[8:t + 8, :]
            for k in (1, 2, 3):
                acc = acc + w_ref[3 - k:4 - k, :] * buf[8 - k:t + 8 - k, :]
            return acc

        pre = conv(bs_ref, cws_ref, cbs_ref)
        sg = _sig(pre)
        xc_ref[...] = pre * sg
        dsl_ref[...] = (sg * (1.0 + pre * (1.0 - sg))).astype(dsl_ref.dtype)
        xr_ref[...] = conv(bl_ref, cwl_ref, cbl_ref)
        dt_ref[...] = _softplus(dtr_ref[...] + dtb_ref[...])
        bs_ref[0:8, :] = bs_ref[t:t + 8, :]
        bl_ref[0:8, :] = bl_ref[t:t + 8, :]

    return pl.pallas_call(
        body, name="conv_fwd", grid=(s // t,),
        in_specs=[_row_spec(t, XBC), _row_spec(t, D, 2), _row_spec(t, 128), _full_spec((4, XBC)),
                  _full_spec((1, XBC)), _full_spec((4, D)), _full_spec((1, D)), _full_spec((1, 128))],
        out_specs=[_row_spec(t, XBC), _row_spec(t, XBC), _row_spec(t, D), _row_spec(t, 128)],
        out_shape=[_sds((s, XBC)), _sds((s, XBC), BF16), _sds((s, D)), _sds((s, 128))],
        scratch_shapes=[pltpu.VMEM((t + 8, XBC), F32), pltpu.VMEM((t + 8, D), F32)],
        compiler_params=_cp("arbitrary"),
    )(xbc_raw, proj5, dt_raw, cw_s, cb_s, cw_l, cb_l, dt_bias)


def _ssd_chunk_setup(dt_ref, alog_ref, e_ref, at_ref, dtt_ref):
    lane = lax.broadcasted_iota(jnp.int32, (CH, 128), 1)
    row = lax.broadcasted_iota(jnp.int32, (CH, 128), 0)
    lane1 = lax.broadcasted_iota(jnp.int32, (1, 128), 1)
    a = jnp.where(lane1 < NH, -jnp.exp(alog_ref[...]), 0.0)
    dtv = dt_ref[...]
    adt = dtv * a
    tril = row >= lane
    acum = _xdot(tril.astype(F32), adt, 3, split_b=True)
    alast = jnp.sum(adt, axis=0, keepdims=True)
    at_ref[...] = acum.T
    dtt_ref[...] = dtv.T
    e = e_ref[...]
    ea_x = _xdot(jnp.exp(acum), e, 2)
    ws = jnp.exp(alast - acum) * dtv
    ws_x = _xdot(ws, e, 2)
    eal = jnp.exp(alast)
    eal_x = jnp.max(_xdot(jnp.broadcast_to(eal, (8, 128)), e, 3), axis=0, keepdims=True)
    return dict(lane=lane, row=row, tril=tril, a=a, dtv=dtv, acum=acum, alast=alast, ea_x=ea_x, ws=ws, ws_x=ws_x,
                eal=eal, eal_x=eal_x)


def _head_decay(cs, at_ref, dtt_ref, h):
    col = jnp.sum(jnp.where(cs["lane"] == h, cs["acum"], 0.0), axis=1, keepdims=True)
    ld = jnp.where(cs["tril"], jnp.exp(jnp.minimum(col - at_ref[h:h + 1, :], 0.0)), 0.0)
    return ld, dtt_ref[h:h + 1, :]


def _ssd_fwd(xbc_c, dt, proj5, a_log, dskip_x, ssm_norm, expand):
    s = xbc_c.shape[0]
    nc = s // CH

    def body(xc_ref, dt_ref, z_ref, alog_ref, dsk_ref, ng_ref, e_ref, y_ref, ya_ref, st_ref, h_ref, at_ref, dtt_ref,
             yd_ref):
        @pl.when(pl.program_id(0) == 0)
        def _():
            h_ref[...] = jnp.zeros_like(h_ref)

        cs = _ssd_chunk_setup(dt_ref, alog_ref, e_ref, at_ref, dtt_ref)
        lane = cs["lane"]
        for g in range(NG):
            gs = slice(GW * g, GW * (g + 1))
            bg = xc_ref[:, D + NS * g:D + NS * (g + 1)]
            cg = xc_ref[:, D + NG * NS + NS * g:D + NG * NS + NS * (g + 1)]
            cb = _mdot(cg, bg, 1, 1)
            for j in range(4 * g, 4 * g + 4):
                ps = slice(128 * j, 128 * (j + 1))
                xp = xc_ref[:, ps]
                acc = jnp.zeros((CH, 128), F32)
                for hf in range(2):
                    ld, rowdt = _head_decay(cs, at_ref, dtt_ref, 2 * j + hf)
                    hm = (lane >= HP) if hf else (lane < HP)
                    acc = acc + _mdot(cb * ld * rowdt, jnp.where(hm, xp, 0.0))
                yd_ref[:, ps] = acc
            hg = h_ref[:, gs]
            yd_ref[:, gs] += _mdot(cg, hg) * cs["ea_x"][:, gs]
            st = _mdot(bg, xc_ref[:, gs] * cs["ws_x"][:, gs], 0, 0)
            st_ref[0, :, gs] = hg
            h_ref[:, gs] = cs["eal_x"][:, gs] * hg + st
        y = yd_ref[...] + dsk_ref[...] * xc_ref[:, 0:D]
        y_ref[...] = y
        yg = y * _silu(z_ref[...])
        for g in range(NG):
            gs = slice(GW * g, GW * (g + 1))
            seg = yg[:, gs]
            ya_ref[:, gs] = seg * _rms(seg) * ng_ref[:, gs]

    return pl.pallas_call(
        body, name="ssd_fwd", grid=(nc,),
        in_specs=[_row_spec(CH, XBC), _row_spec(CH, 128), _row_spec(CH, D, 0), _full_spec((1, 128)),
                  _full_spec((1, D)), _full_spec((1, D)), _full_spec((128, D))],
        out_specs=[_row_spec(CH, D), _row_spec(CH, D), pl.BlockSpec((1, NS, D), lambda i: (i, 0, 0))],
        out_shape=[_sds((s, D)), _sds((s, D)), _sds((nc, NS, D))],
        scratch_shapes=[pltpu.VMEM((NS, D), F32), pltpu.VMEM((128, 128), F32), pltpu.VMEM((128, 128), F32),
                        pltpu.VMEM((CH, D), F32)],
        compiler_params=_cp("arbitrary"),
    )(xbc_c, dt, proj5, a_log, dskip_x, ssm_norm, expand)


def _lru_gates(xr, wa_ref, wx_ref, ba_ref, bx_ref, lam_ref):
    gr = _sig(_mdot(xr, wa_ref[...]) + ba_ref[...])
    gi = _sig(_mdot(xr, wx_ref[...]) + bx_ref[...])
    sp = _softplus(-lam_ref[...])
    la = -LRU_C * gr * sp
    a = jnp.exp(la)
    oms = _one_minus_sq(a, la)
    inv_mult = lax.rsqrt(oms)
    return gr, gi, sp, a, oms * inv_mult, inv_mult


def _blocked_scan(a, u, carry_ref, a_ref, u_ref, c_ref, out_ref, reverse):
    t = a.shape[0]
    ns = t // 8

    def combine(av, uv, idx, n, sh):
        m = (idx < n - sh) if reverse else (idx >= sh)
        by = n - sh if reverse else sh
        return jnp.where(m, av * pltpu.roll(av, by, 0), av), jnp.where(m, uv + av * pltpu.roll(uv, by, 0), uv)

    row = lax.broadcasted_iota(jnp.int32, (t, D), 0)
    rin = jnp.bitwise_and(row, 7)
    for sh in (1, 2, 4):
        m = (rin < 8 - sh) if reverse else (rin >= sh)
        by = t - sh if reverse else sh
        a, u = jnp.where(m, a * pltpu.roll(a, by, 0), a), jnp.where(m, u + a * pltpu.roll(u, by, 0), u)
    a_ref[...] = a
    u_ref[...] = u
    edge = 0 if reverse else 7
    for j in range(ns):
        c_ref[j:j + 1, :] = a_ref[8 * j + edge:8 * j + edge + 1, :]
    at = c_ref[...]
    for j in range(ns):
        c_ref[j:j + 1, :] = u_ref[8 * j + edge:8 * j + edge + 1, :]
    ut = c_ref[...]
    srow = lax.broadcasted_iota(jnp.int32, (ns, D), 0)
    sh = 1
    while sh < ns:
        at, ut = combine(at, ut, srow, ns, sh)
        sh *= 2
    cv = carry_ref[0:1, :]
    ends = ut + at * cv
    last = 0 if reverse else ns - 1
    first = ns - 1 if reverse else 0
    c_ref[...] = jnp.where(srow == first, cv, pltpu.roll(ends, first if reverse else 1, 0))
    carry_ref[0:1, :] = jnp.sum(jnp.where(srow == last, ends, 0.0), axis=0, keepdims=True)
    for j in range(ns):
        sl = slice(8 * j, 8 * j + 8)
        out_ref[sl, :] = u_ref[sl, :] + a_ref[sl, :] * c_ref[j:j + 1, :]


def _lru_fwd(xr, proj5, ya, wa_bd, wx_bd, ba, bx, lam, plan=None):
    s = xr.shape[0]
    t = min(256, s)

    def body(xr_ref, g_ref, ga_ref, gb_ref, ya_ref, wa_ref, wx_ref, ba_ref, bx_ref, lam_ref, h_ref, mg_ref, hc_ref,
             a_ref, u_ref, c_ref):
        @pl.when(pl.program_id(0) == 0)
        def _():
            hc_ref[...] = jnp.zeros_like(hc_ref)

        xrv = xr_ref[...]
        _, gi, _, a, mult, _ = _lru_gates(xrv, wa_ref, wx_ref, ba_ref, bx_ref, lam_ref)
        _blocked_scan(a, mult * gi * xrv, hc_ref, a_ref, u_ref, c_ref, h_ref, reverse=False)
        yb = h_ref[...] * _gelu(g_ref[...])
        mg_ref[...] = (_sig(ga_ref[...]) * ya_ref[...] + _sig(gb_ref[...]) * yb).astype(mg_ref.dtype)

    return _pcall(
        body, (xr, proj5, proj5, proj5, ya, wa_bd, wx_bd, ba, bx, lam), name="lru_fwd", grid=(s // t,),
        in_specs=[_row_spec(t, D), _row_spec(t, D, 1), _row_spec(t, D, 3), _row_spec(t, D, 4), _row_spec(t, D),
                  _full_spec((D, D)), _full_spec((D, D)), _full_spec((1, D)), _full_spec((1, D)), _full_spec((1, D))],
        out_specs=[_row_spec(t, D), _row_spec(t, D)],
        out_shape=[_sds((s, D)), _sds((s, D), MXU)],
        scratch_shapes=[pltpu.VMEM((8, D), F32), pltpu.VMEM((t, D), F32), pltpu.VMEM((t, D), F32),
                        pltpu.VMEM((t // 8, D), F32)],
        sem=("arbitrary",), plan=plan)


def _out_proj(merged, w_out, x, g2, g3):
    s = x.shape[0]
    t = min(256, s)

    def body(mg_ref, w_ref, x_ref, g2_ref, g3_ref, mix_ref, h1_ref, v_ref):
        mix = _mdot(mg_ref[...], w_ref[...])
        mix_ref[...] = mix
        h1 = x_ref[...] + mix * _rms(mix) * g2_ref[...]
        h1_ref[...] = h1
        v_ref[...] = (h1 * _rms(h1) * g3_ref[...]).astype(v_ref.dtype)

    return pl.pallas_call(
        body, name="out_proj", grid=(s // t,),
        in_specs=[_row_spec(t, D), _full_spec((D, D)), _row_spec(t, D), _full_spec((1, D)), _full_spec((1, D))],
        out_specs=[_row_spec(t, D), _row_spec(t, D), _row_spec(t, D)],
        out_shape=[_sds((s, D)), _sds((s, D)), _sds((s, D), MXU)],
        compiler_params=_cp("parallel"),
    )(merged, w_out, x, g2, g3)


def _down_loss(pre, w_down, h1, target, g4):
    s = pre.shape[0]
    t = min(256, s)

    def body(pre_ref, w_ref, h1_ref, tg_ref, g4_ref, dout_ref, dff_ref, loss_ref, dg4_ref):
        @pl.when(pl.program_id(0) == 0)
        def _():
            loss_ref[...] = jnp.zeros_like(loss_ref)
            dg4_ref[...] = jnp.zeros_like(dg4_ref)

        ff = _mdot(_relu2(pre_ref[...]), w_ref[...])
        r4 = _rms(ff)
        g4v = g4_ref[...]
        diff = h1_ref[...] + ff * r4 * g4v - tg_ref[...]
        sq = jnp.sum(jnp.sum(diff * diff, axis=1, keepdims=True), axis=0, keepdims=True)
        loss_ref[...] += (0.5 / D) * sq
        dout = diff * (1.0 / D)
        dout_ref[...] = dout
        dff, dg = _rms_bwd(ff, r4, g4v, dout)
        dff_ref[...] = dff.astype(dff_ref.dtype)
        dg4_ref[...] += dg

    return pl.pallas_call(
        body, name="down_loss", grid=(s // t,),
        in_specs=[_row_spec(t, DFF), _full_spec((DFF, D)), _row_spec(t, D), _row_spec(t, D), _full_spec((1, D))],
        out_specs=[_row_spec(t, D), _row_spec(t, D), _full_spec((1, 128)), _full_spec((1, D))],
        out_shape=[_sds((s, D)), _sds((s, D), MXU), _sds((1, 128)), _sds((1, D))],
        compiler_params=_cp("arbitrary"),
    )(pre, w_down, h1, target, g4)


def _dv_norms(dpre, w_up, h1, mix, dout, g3, g2):
    s = h1.shape[0]
    t = min(256, s)

    def body(dp_ref, w_ref, h1_ref, mix_ref, dout_ref, g3_ref, g2_ref, dh1_ref, dmix_ref, dg3_ref, dg2_ref):
        @pl.when(pl.program_id(0) == 0)
        def _():
            dg3_ref[...] = jnp.zeros_like(dg3_ref)
            dg2_ref[...] = jnp.zeros_like(dg2_ref)

        dv = _mdot(dp_ref[...], w_ref[...], 1, 1)
        h1 = h1_ref[...]
        dh1n, dg3 = _rms_bwd(h1, _rms(h1), g3_ref[...], dv)
        dh1 = dout_ref[...] + dh1n
        dh1_ref[...] = dh1
        mix = mix_ref[...]
        dmix, dg2 = _rms_bwd(mix, _rms(mix), g2_ref[...], dh1)
        dmix_ref[...] = dmix.astype(dmix_ref.dtype)
        dg3_ref[...] += dg3
        dg2_ref[...] += dg2

    return pl.pallas_call(
        body, name="dv_norms", grid=(s // t,),
        in_specs=[_row_spec(t, DFF), _full_spec((D, DFF)), _row_spec(t, D), _row_spec(t, D), _row_spec(t, D),
                  _full_spec((1, D)), _full_spec((1, D))],
        out_specs=[_row_spec(t, D), _row_spec(t, D), _full_spec((1, D)), _full_spec((1, D))],
        out_shape=[_sds((s, D)), _sds((s, D), MXU), _sds((1, D)), _sds((1, D))],
        compiler_params=_cp("arbitrary"),
    )(dpre, w_up, h1, mix, dout, g3, g2)


def _lru_bwd(dmerged, ya, xr, h, proj5, wa_bd, wx_bd, ba, bx, lam, plan=None):
    s = xr.shape[0]
    t = min(128, s)
    n = s // t
    rs = functools.partial(_rev_spec, t, D, n)

    def body(dm_ref, ya_ref, xr_ref, h_ref, hp_ref, g_ref, ga_ref, gb_ref, wa_ref, wx_ref, ba_ref, bx_ref, lam_ref,
             dya_ref, dga_ref, dgb_ref, dg_ref, dxr_ref, dpr_ref, dpi_ref, dlam_ref, dba_ref, dbx_ref, gc_ref,
             af_ref, an_ref, us_ref, c_ref, gs_ref):
        i = pl.program_id(0)

        @pl.when(i == 0)
        def _():
            gc_ref[...] = jnp.zeros_like(gc_ref)
            af_ref[...] = jnp.zeros_like(af_ref)
            dlam_ref[...] = jnp.zeros_like(dlam_ref)
            dba_ref[...] = jnp.zeros_like(dba_ref)
            dbx_ref[...] = jnp.zeros_like(dbx_ref)

        xrv = xr_ref[...]
        gr, gi, sp, a, mult, inv_mult = _lru_gates(xrv, wa_ref, wx_ref, ba_ref, bx_ref, lam_ref)
        hv = h_ref[...]
        dm = dm_ref[...]
        sa = _sig(ga_ref[...])
        sb = _sig(gb_ref[...])
        gel, dgel = _gelu_and_grad(g_ref[...])
        dya = dm * sa
        dya_ref[...] = dya
        dga_ref[...] = (dya * ya_ref[...] * (1.0 - sa)).astype(dga_ref.dtype)
        dyb = dm * sb
        dybh = dyb * hv
        dgb_ref[...] = (dybh * gel * (1.0 - sb)).astype(dgb_ref.dtype)
        dg_ref[...] = (dybh * dgel).astype(dg_ref.dtype)
        row = lax.broadcasted_iota(jnp.int32, (t, D), 0)
        an = jnp.where(row == t - 1, af_ref[0:1, :], pltpu.roll(a, t - 1, 0))
        _blocked_scan(an, dyb * gel, gc_ref, an_ref, us_ref, c_ref, gs_ref, reverse=True)
        gfull = gs_ref[...]
        af_ref[0:1, :] = jnp.sum(jnp.where(row == 0, a, 0.0), axis=0, keepdims=True)
        hlast = jnp.where(i == n - 1, 0.0, hp_ref[7:8, :])
        hprev = jnp.where(row == 0, hlast, pltpu.roll(hv, 1, 0))
        gx = gfull * xrv
        dgi = gx * mult
        dla = a * (gfull * hprev - gx * gi * a * inv_mult)
        dgr = dla * (-LRU_C * sp)
        dsp = jnp.sum(dla * (-LRU_C * gr), axis=0, keepdims=True)
        dlam_ref[...] += dsp * (-_sig(-lam_ref[...]))
        dpr = dgr * gr * (1.0 - gr)
        dpi = dgi * gi * (1.0 - gi)
        dpr_ref[...] = dpr.astype(dpr_ref.dtype)
        dpi_ref[...] = dpi.astype(dpi_ref.dtype)
        dba_ref[...] += jnp.sum(dpr, axis=0, keepdims=True)
        dbx_ref[...] += jnp.sum(dpi, axis=0, keepdims=True)
        dxr_ref[...] = gfull * mult * gi + _mdot(dpr, wa_ref[...], 1, 1) + _mdot(dpi, wx_ref[...], 1, 1)

    hp_spec = pl.BlockSpec((8, D), lambda i: (jnp.maximum((n - 1 - i) * (t // 8) - 1, 0), 0))
    return _pcall(
        body, (dmerged, ya, xr, h, h, proj5, proj5, proj5, wa_bd, wx_bd, ba, bx, lam), name="lru_bwd", grid=(n,),
        in_specs=[rs(), rs(), rs(), rs(), hp_spec, rs(1), rs(3), rs(4), _full_spec((D, D)), _full_spec((D, D)),
                  _full_spec((1, D)), _full_spec((1, D)), _full_spec((1, D))],
        out_specs=[rs(), rs(), rs(), rs(), rs(), rs(), rs(), _full_spec((1, D)), _full_spec((1, D)),
                   _full_spec((1, D))],
        out_shape=[_sds((s, D)), _sds((s, D), MXU), _sds((s, D), MXU), _sds((s, D), MXU), _sds((s, D)),
                   _sds((s, D), MXU), _sds((s, D), MXU), _sds((1, D)), _sds((1, D)), _sds((1, D))],
        scratch_shapes=[pltpu.VMEM((8, D), F32), pltpu.VMEM((8, D), F32), pltpu.VMEM((t, D), F32),
                        pltpu.VMEM((t, D), F32), pltpu.VMEM((t // 8, D), F32), pltpu.VMEM((t, D), F32)],
        sem=("arbitrary",), plan=plan)


def _ssd_bwd(dya, y, proj5, xbc_c, dt, states, a_log, dskip_x, ssm_norm, expand, reduce_, plan=None):
    s = xbc_c.shape[0]
    nc = s // CH
    rv = functools.partial(_rev_spec, CH)

    def body(dya_ref, y_ref, z_ref, xc_ref, dt_ref, st_ref, alog_ref, dsk_ref, ng_ref, e_ref, et_ref, dz_ref,
             dxc_ref, ddt_ref, dng_ref, ddsk_ref, dalog_ref, dh_ref, at_ref, dtt_ref, dat_ref, ddtt_ref, dy_ref,
             yoffdy_ref, xbds_ref):
        @pl.when(pl.program_id(0) == 0)
        def _():
            dh_ref[...] = jnp.zeros_like(dh_ref)
            dng_ref[...] = jnp.zeros_like(dng_ref)
            ddsk_ref[...] = jnp.zeros_like(ddsk_ref)
            dalog_ref[...] = jnp.zeros_like(dalog_ref)

        cs = _ssd_chunk_setup(dt_ref, alog_ref, e_ref, at_ref, dtt_ref)
        lane, row = cs["lane"], cs["row"]
        et = et_ref[...]
        for g in range(NG):
            gs = slice(GW * g, GW * (g + 1))
            yv = y_ref[:, gs]
            zv = z_ref[:, gs]
            sz = _silu(zv)
            yg = yv * sz
            dyav = dya_ref[:, gs]
            dyg, dng = _rms_bwd(yg, _rms(yg), ng_ref[:, gs], dyav)
            dng_ref[:, gs] += dng
            dy_ref[:, gs] = dyg * sz
            dz_ref[:, gs] = (dyg * yv * _dsilu(zv)).astype(dz_ref.dtype)
        dyv = dy_ref[...]
        xs = xc_ref[:, 0:D]
        ddsk_ref[...] += jnp.sum(dyv * xs, axis=0, keepdims=True)
        dxc_ref[:, 0:D] = dyv * dsk_ref[...]
        dat_ref[...] = jnp.zeros_like(dat_ref)
        ddtt_ref[...] = jnp.zeros_like(ddtt_ref)
        hh = jnp.sum(dh_ref[...] * st_ref[0], axis=0, keepdims=True)
        deal = jnp.max(_xdot(jnp.broadcast_to(hh, (8, D)), et, 3), axis=0, keepdims=True)
        d_acum = jnp.zeros((CH, 128), F32)
        for g in range(NG):
            gs = slice(GW * g, GW * (g + 1))
            bs_ = slice(D + NS * g, D + NS * (g + 1))
            cs_ = slice(D + NG * NS + NS * g, D + NG * NS + NS * (g + 1))
            bg = xc_ref[:, bs_]
            cg = xc_ref[:, cs_]
            cb = _mdot(cg, bg, 1, 1)
            hg = st_ref[0, :, gs]
            dhg = dh_ref[:, gs]
            dyg_ = dy_ref[:, gs]
            xsg = xc_ref[:, gs]
            ea = cs["ea_x"][:, gs]
            wsx = cs["ws_x"][:, gs]
            dp = dyg_ * ea
            yoffdy_ref[:, gs] = dp * _mdot(cg, hg)
            dc = _mdot(dp, hg, 1, 1)
            dhprev = _mdot(cg, dp, 0, 0)
            bds = _mdot(bg, dhg)
            dxc_ref[:, gs] += wsx * bds
            xbds_ref[:, gs] = xsg * bds
            db = _mdot(xsg * wsx, dhg, 1, 1)
            dh_ref[:, gs] = dhprev + cs["eal_x"][:, gs] * dhg
            dcbs = jnp.zeros((CH, CH), F32)
            for j in range(4 * g, 4 * g + 4):
                ps = slice(128 * j, 128 * (j + 1))
                xp = xc_ref[:, ps]
                dyp = dy_ref[:, ps]
                dxacc = jnp.zeros((CH, 128), F32)
                for hf in range(2):
                    hd = 2 * j + hf
                    ld, rowdt = _head_decay(cs, at_ref, dtt_ref, hd)
                    hm = (lane >= HP) if hf else (lane < HP)
                    dym = jnp.where(hm, dyp, 0.0)
                    w = cb * ld * rowdt
                    dw = _mdot(dym, jnp.where(hm, xp, 0.0), 1, 1)
                    dxacc = dxacc + _mdot(w, dym, 0, 0)
                    nm = dw * w
                    ddtt_ref[hd:hd + 1, :] += jnp.sum(dw * cb * ld, axis=0, keepdims=True)
                    d_acum = d_acum + jnp.where(lane == hd, jnp.sum(nm, axis=1, keepdims=True), 0.0)
                    dat_ref[hd:hd + 1, :] -= jnp.sum(nm, axis=0, keepdims=True)
                    dcbs = dcbs + dw * ld * rowdt
                dxc_ref[:, ps] += dxacc
            dxc_ref[:, bs_] = db + _mdot(dcbs, cg, 0, 0)
            dxc_ref[:, cs_] = dc + _mdot(dcbs, bg)
        dws = _xdot(xbds_ref[...], et, 2)
        ws = cs["ws"]
        d_acum = d_acum - dws * ws + _xdot(yoffdy_ref[...], et, 2) + dat_ref[...].T
        d_alast = jnp.sum(dws * ws, axis=0, keepdims=True) + deal * cs["eal"]
        d_acum = d_acum + jnp.where(row == CH - 1, d_alast, 0.0)
        triu = row <= lane
        d_adt = _xdot(triu.astype(F32), d_acum, 3, split_b=True)
        ddt_ref[...] = dws * jnp.exp(cs["alast"] - cs["acum"]) + ddtt_ref[...].T + d_adt * cs["a"]
        dalog_ref[...] += jnp.sum(d_adt * cs["dtv"], axis=0, keepdims=True) * cs["a"]

    return _pcall(
        body, (dya, y, proj5, xbc_c, dt, states, a_log, dskip_x, ssm_norm, expand, reduce_), name="ssd_bwd",
        grid=(nc,),
        in_specs=[rv(D, nc), rv(D, nc), rv(D, nc, 0), rv(XBC, nc), rv(128, nc),
                  pl.BlockSpec((1, NS, D), lambda i: (nc - 1 - i, 0, 0)), _full_spec((1, 128)), _full_spec((1, D)),
                  _full_spec((1, D)), _full_spec((128, D)), _full_spec((D, 128))],
        out_specs=[rv(D, nc), rv(XBC, nc), rv(128, nc), _full_spec((1, D)), _full_spec((1, D)),
                   _full_spec((1, 128))],
        out_shape=[_sds((s, D), MXU), _sds((s, XBC)), _sds((s, 128)), _sds((1, D)), _sds((1, D)), _sds((1, 128))],
        scratch_shapes=[pltpu.VMEM((NS, D), F32), pltpu.VMEM((128, 128), F32), pltpu.VMEM((128, 128), F32),
                        pltpu.VMEM((128, 128), F32), pltpu.VMEM((128, 128), F32), pltpu.VMEM((CH, D), F32),
                        pltpu.VMEM((CH, D), F32), pltpu.VMEM((CH, D), F32)],
        sem=("arbitrary",), plan=plan)


def _conv_bwd(dxbc_c, dsilu, dxr, ddt, xbc_raw, proj5, dt_raw, cw_s, cw_l, dt_bias, plan=None):
    s = xbc_raw.shape[0]
    t = min(256, s)
    n = s // t

    def body(dxc_ref, dsl_ref, dxr_ref, ddt_ref, xs_ref, xl_ref, dtr_ref, cws_ref, cwl_ref, dtb_ref, dxs_ref,
             dxl_ref, ddtr_ref, dcws_ref, dcbs_ref, dcwl_ref, dcbl_ref, ddtb_ref, ds_ref, dl_ref):
        @pl.when(pl.program_id(0) == 0)
        def _():
            ds_ref[t:t + 8, :] = jnp.zeros((8, XBC), F32)
            dl_ref[t:t + 8, :] = jnp.zeros((8, D), F32)
            for r in (dcws_ref, dcbs_ref, dcwl_ref, dcbl_ref, ddtb_ref):
                r[...] = jnp.zeros_like(r)

        ds_ref[0:t, :] = dxc_ref[...] * dsl_ref[...].astype(F32)
        dl_ref[0:t, :] = dxr_ref[...]

        def back(dbuf, x_ref, w_ref, dx_ref, dw_ref, db_ref):
            xv = x_ref[...]
            dpre = dbuf[0:t, :]
            dx = w_ref[3:4, :] * dpre
            dw_ref[3:4, :] += jnp.sum(dpre * xv, axis=0, keepdims=True)
            db_ref[...] += jnp.sum(dpre, axis=0, keepdims=True)
            for k in (1, 2, 3):
                ahead = dbuf[k:t + k, :]
                dx = dx + w_ref[3 - k:4 - k, :] * ahead
                dw_ref[3 - k:4 - k, :] += jnp.sum(ahead * xv, axis=0, keepdims=True)
            dx_ref[...] = dx.astype(dx_ref.dtype)
            dbuf[t:t + 8, :] = dbuf[0:8, :]

        back(ds_ref, xs_ref, cws_ref, dxs_ref, dcws_ref, dcbs_ref)
        back(dl_ref, xl_ref, cwl_ref, dxl_ref, dcwl_ref, dcbl_ref)
        ddtr = ddt_ref[...] * _sig(dtr_ref[...] + dtb_ref[...])
        ddtr_ref[...] = ddtr.astype(ddtr_ref.dtype)
        ddtb_ref[...] += jnp.sum(ddtr, axis=0, keepdims=True)

    rv = functools.partial(_rev_spec, t)
    return _pcall(
        body, (dxbc_c, dsilu, dxr, ddt, xbc_raw, proj5, dt_raw, cw_s, cw_l, dt_bias), name="conv_bwd", grid=(n,),
        in_specs=[rv(XBC, n), rv(XBC, n), rv(D, n), rv(128, n), rv(XBC, n), rv(D, n, 2), rv(128, n),
                  _full_spec((4, XBC)), _full_spec((4, D)), _full_spec((1, 128))],
        out_specs=[rv(XBC, n), rv(D, n), rv(128, n), _full_spec((4, XBC)), _full_spec((1, XBC)), _full_spec((4, D)),
                   _full_spec((1, D)), _full_spec((1, 128))],
        out_shape=[_sds((s, XBC), MXU), _sds((s, D), MXU), _sds((s, 128), MXU), _sds((4, XBC)), _sds((1, XBC)),
                   _sds((4, D)), _sds((1, D)), _sds((1, 128))],
        scratch_shapes=[pltpu.VMEM((t + 8, XBC), F32), pltpu.VMEM((t + 8, D), F32)],
        sem=("arbitrary",), plan=plan)


def _du_norm(pieces5, dxbc, ddtr, w5, wxbc, wdt, x, dh1, g1, plan=None):
    s = x.shape[0]
    t = min(256, s)

    def body(p0, p1, p2, p3, p4, dxbc_ref, ddtr_ref, w5_ref, wx_ref, wd_ref, x_ref, dh1_ref, g1_ref, dx_ref, dg1_ref):
        @pl.when(pl.program_id(0) == 0)
        def _():
            dg1_ref[...] = jnp.zeros_like(dg1_ref)

        du = _mdot(dxbc_ref[...], wx_ref[...]) + _mdot(ddtr_ref[...], wd_ref[...])
        for b, p in enumerate((p0, p1, p2, p3, p4)):
            du = du + _mdot(p[...], w5_ref[D * b:D * (b + 1), :])
        xv = x_ref[...]
        dxn, dg1 = _rms_bwd(xv, _rms(xv), g1_ref[...], du)
        dx_ref[...] = dh1_ref[...] + dxn
        dg1_ref[...] += dg1

    return _pcall(
        body, (*pieces5, dxbc, ddtr, w5, wxbc, wdt, x, dh1, g1), name="du_norm", grid=(s // t,),
        in_specs=[_row_spec(t, D)] * 5 + [_row_spec(t, XBC), _row_spec(t, 128), _full_spec((5 * D, D)),
                                          _full_spec((XBC, D)), _full_spec((128, D)), _row_spec(t, D),
                                          _row_spec(t, D), _full_spec((1, D))],
        out_specs=[_row_spec(t, D), _full_spec((1, D))],
        out_shape=[_sds((s, D)), _sds((1, D))],
        sem=("arbitrary",), plan=plan)


def _adamw(w, g, m, v, name):
    r, c = w.shape
    t = r
    if r * c > 256 * 1024:
        t = next(cand for cand in (512, 256, 128, 64, 32, 16, 8) if r % cand == 0 and cand * c <= 512 * 1024)
    bc1 = 1.0 - ADAM_B1 ** ADAM_STEP
    bc2 = 1.0 - ADAM_B2 ** ADAM_STEP

    def body(w_ref, g_ref, m_ref, v_ref, d_ref, nm_ref, nv_ref):
        gv = g_ref[...]
        nm = ADAM_B1 * m_ref[...] + (1.0 - ADAM_B1) * gv
        nv = ADAM_B2 * v_ref[...] + (1.0 - ADAM_B2) * (gv * gv)
        nm_ref[...] = nm
        nv_ref[...] = nv
        d_ref[...] = -ADAM_LR * ((nm / bc1) / (jnp.sqrt(nv / bc2) + ADAM_EPS) + ADAM_WD * w_ref[...])

    spec = pl.BlockSpec((t, c), lambda i: (i, 0))
    return pl.pallas_call(
        body, name=name, grid=(r // t,), in_specs=[spec] * 4, out_specs=[spec] * 3,
        out_shape=[_sds((r, c))] * 3, compiler_params=_cp("parallel"),
    )(w, g, m, v)


def _half_blocks(shape, axis):
    r, c = shape
    if axis == 0:
        t = 256 if (r // 2) % 256 == 0 else 128
        nb = (r // 2) // t
        return (t, c), nb, (lambda i: (i, 0)), (lambda i: (i % nb, 0))
    nb = (c // 2) // 128
    return (r, 128), nb, (lambda i: (0, i)), (lambda i: (0, i % nb))


def _adamw_halves(w, g_mine, g_other, m, v, cidx, name, axis=0):
    r, c = w.shape
    blk, nb, whole, part = _half_blocks(w.shape, axis)
    bc1 = 1.0 - ADAM_B1 ** ADAM_STEP
    bc2 = 1.0 - ADAM_B2 ** ADAM_STEP

    def body(c_ref, w_ref, gm_ref, go_ref, m_ref, v_ref, g_ref, d_ref, nm_ref, nv_ref):
        mine = (pl.program_id(0) // nb) == c_ref[0]
        gv = jnp.where(mine, gm_ref[...], go_ref[...])
        g_ref[...] = gv
        nm = ADAM_B1 * m_ref[...] + (1.0 - ADAM_B1) * gv
        nv = ADAM_B2 * v_ref[...] + (1.0 - ADAM_B2) * (gv * gv)
        nm_ref[...] = nm
        nv_ref[...] = nv
        d_ref[...] = -ADAM_LR * ((nm / bc1) / (jnp.sqrt(nv / bc2) + ADAM_EPS) + ADAM_WD * w_ref[...])

    spec = pl.BlockSpec(blk, lambda i, c_ref: whole(i))
    half = pl.BlockSpec(blk, lambda i, c_ref: part(i))
    return pl.pallas_call(
        body, name=name,
        grid_spec=pltpu.PrefetchScalarGridSpec(num_scalar_prefetch=1, grid=(2 * nb,),
                                               in_specs=[spec, half, half, spec, spec], out_specs=[spec] * 4),
        out_shape=[_sds((r, c))] * 4, compiler_params=_cp("parallel"),
    )(cidx, w, g_mine, g_other, m, v)


def _block_diag(w):
    eye = jnp.eye(NH, dtype=w.dtype)
    return (w[:, :, None, :] * eye[:, None, :, None]).reshape(D, D)


def _diag_blocks(full):
    eye = jnp.eye(NH, dtype=full.dtype)
    return (full.reshape(NH, HP, NH, HP) * eye[:, None, :, None]).sum(axis=2)


def _pad_lanes(v, n=128):
    return jnp.pad(v, ((0, 0), (0, n - v.shape[1])))


def _local_step(x, target, p, dist=None):
    heads = jnp.arange(D, dtype=jnp.int32) // HP
    expand = (jnp.arange(128, dtype=jnp.int32)[:, None] == heads[None, :]).astype(F32)
    reduce_ = expand.T
    dskip_x = jnp.repeat(p["d_skip"], HP, axis=1)
    a_log = _pad_lanes(p["a_log"])
    dt_bias = _pad_lanes(p["dt_bias"])
    wa_bd = _block_diag(p["lru_wa"]).astype(MXU)
    wx_bd = _block_diag(p["lru_wx"]).astype(MXU)
    ba = p["lru_ba"].reshape(1, D)
    bx = p["lru_bx"].reshape(1, D)

    def hosted(key, fn):
        plan = dist.plan(key) if dist is not None else None
        if plan is None:
            return fn(plan=None)
        outs, got = fn(plan=plan)
        dist.done(key, got, p)
        return outs

    u = hosted("norm_u", functools.partial(_norm_cast, x, p["norm_mix_pre"], "norm_u"))
    w5, wxbc, wdt = p["w5"], p["wxbc"], p["wdt"]
    proj5 = hosted("proj5", functools.partial(_matmul, u, w5, name="proj5", tb=True))
    xbc_raw = _matmul(u, wxbc, name="proj_xbc", tb=True, tn=XBC)
    dt_raw = _matmul(u, wdt, name="proj_dt", tb=True)
    xbc_c, dsilu, xr, dt = _conv_fwd(xbc_raw, proj5, dt_raw, p["conv_ssm_w"], p["conv_ssm_b"], p["conv_lru_w"],
                                     p["conv_lru_b"], dt_bias)
    y, ya, states = _ssd_fwd(xbc_c, dt, proj5, a_log, dskip_x, p["ssm_norm"], expand)
    h, merged = hosted("lru_fwd", functools.partial(_lru_fwd, xr, proj5, ya, wa_bd, wx_bd, ba, bx, p["lru_lambda"]))
    mix, h1, v = _out_proj(merged, p["w_out"], x, p["norm_mix_post"], p["norm_mlp_pre"])
    pre = _matmul(v, p["w_up"], name="up_proj", tm=1024, out_dtype=MXU)
    dout, dff, loss, dg4 = _down_loss(pre, p["w_down"], h1, target, p["norm_mlp_post"])

    dpre = _matmul(dff, p["w_down"], name="d_pre", tb=True, tm=1024, out_dtype=MXU,
                   epi=lambda r, pr: r * (2.0 * jnp.maximum(pr.astype(F32), 0.0)), epi_args=(pre,))
    g_w_down = _matmul(pre, dff, name="dw_down", ta=True, tm=1024, tn=1024, tk=TK_GRAD, a_fn=_relu2)
    dh1, dmix, dg3, dg2 = _dv_norms(dpre, p["w_up"], h1, mix, dout, p["norm_mlp_pre"], p["norm_mix_post"])
    g_w_up = _matmul(v, dpre, name="dw_up", ta=True, tm=1024, tn=1024, tk=TK_GRAD)
    dmerged = _matmul(dmix, p["w_out"], name="d_merged", tb=True)
    g_w_out = _matmul(merged, dmix, name="dw_out", ta=True, tm=1024, tn=1024, tk=TK_GRAD)
    if dist is not None:
        dist.early_grads(w_down=g_w_down, w_up=g_w_up, w_out=g_w_out)
    (dya, dga, dgb, dg, dxr, dpr, dpi, dlam, dba, dbx) = hosted("lru_bwd", functools.partial(
        _lru_bwd, dmerged, ya, xr, h, proj5, wa_bd, wx_bd, ba, bx, p["lru_lambda"]))
    g_wa = _diag_blocks(_matmul(xr, dpr, name="dw_lru_a", ta=True, tm=1024, tn=1024, tk=TK_GRAD))
    g_wx = _diag_blocks(_matmul(xr, dpi, name="dw_lru_x", ta=True, tm=1024, tn=1024, tk=TK_GRAD))
    dz, dxbc_c, ddt, dng, ddsk, dalog = hosted("ssd_bwd", functools.partial(
        _ssd_bwd, dya, y, proj5, xbc_c, dt, states, a_log, dskip_x, p["ssm_norm"], expand, reduce_))
    (dxbc, dxl, ddtr, dcws, dcbs, dcwl, dcbl, ddtb) = hosted("conv_bwd", functools.partial(
        _conv_bwd, dxbc_c, dsilu, dxr, ddt, xbc_raw, proj5, dt_raw, p["conv_ssm_w"], p["conv_lru_w"], dt_bias))
    pieces5 = (dz, dg, dxl, dga, dgb)
    gw5 = [_matmul(pc, u, name=f"dw_in_{i}", ta=True, tm=1024, tn=1024, tk=TK_GRAD) for i, pc in enumerate(pieces5)]
    gwxbc = _matmul(dxbc, u, name="dw_in_xbc", ta=True, tm=XBC, tn=1024, tk=TK_GRAD)
    gwdt = _matmul(ddtr, u, name="dw_in_dt", ta=True, tm=128, tn=1024, tk=TK_GRAD)
    g_w_in_t = jnp.concatenate([gw5[0], gwxbc, gwdt[:NH], gw5[1], gw5[2], gw5[3], gw5[4]], axis=0)
    grads = {
        "w_in_t": g_w_in_t, "conv_ssm_w": dcws, "conv_ssm_b": dcbs, "dt_bias": ddtb[:, :NH],
        "a_log": dalog[:, :NH], "d_skip": ddsk.reshape(NH, HP).sum(axis=1)[None, :], "ssm_norm": dng,
        "conv_lru_w": dcwl, "conv_lru_b": dcbl, "lru_wa": g_wa, "lru_ba": dba.reshape(NH, HP), "lru_wx": g_wx,
        "lru_bx": dbx.reshape(NH, HP), "lru_lambda": dlam, "w_out": g_w_out, "norm_mix_post": dg2,
        "norm_mlp_pre": dg3, "w_up": g_w_up, "w_down": g_w_down, "norm_mlp_post": dg4,
    }
    if dist is not None:
        dist.late_grads(grads, loss[0, 0])
    grad_x, grads["norm_mix_pre"] = hosted("du_norm", functools.partial(
        _du_norm, pieces5, dxbc, ddtr, w5, wxbc, wdt, x, dh1, p["norm_mix_pre"]))
    return loss[0, 0], grad_x, grads


def _split_w_in_t(w_in_t):
    z, xbc, dtc, g, xl, ga, gb = jnp.split(w_in_t, [D, D + XBC, D + XBC + NH, 2 * D + XBC + NH,
                                                    3 * D + XBC + NH, 4 * D + XBC + NH], axis=0)
    return jnp.concatenate([z, g, xl, ga, gb], axis=0), xbc, jnp.pad(dtc, ((0, 128 - NH), (0, 0)))


COMM = BF16


def _place():
    x, y, c = lax.axis_index("x"), lax.axis_index("y"), lax.axis_index("c")
    chips = [(1 - x, y), (x, 1 - y), (1 - x, 1 - y)]
    return x, y, c, chips


def _remote(src, dst, send_sem, recv_sem, to):
    return pltpu.make_async_remote_copy(src_ref=src, dst_ref=dst, send_sem=send_sem, recv_sem=recv_sem, device_id=to,
                                        device_id_type=MESH)


def _gather_plan(big, small=(), axes=None):
    nb = len(big)
    arrs = list(big) + list(small)
    na = len(arrs)
    axes = list(axes or [0] * nb)

    def half(ref, a, k, which):
        h = arrs[a].shape[axes[a]] // 2
        cut = (pl.ds(which * h, h),) if axes[a] == 0 else (slice(None), pl.ds(which * h, h))
        return ref.at[cut] if k is None else ref.at[(k,) + cut]

    def direct(ins, outs, send, recv):
        x, y, c, chips = _place()
        k = 2 * x + y
        cps = []
        for a in range(na):
            src, dst = (half(ins[a], a, None, c), half(outs[a], a, k, c)) if a < nb else (ins[a], outs[a].at[k])
            cps += [_remote(src, dst, send.at[a, j], recv.at[a, j], (cx, cy, c)) for j, (cx, cy) in enumerate(chips)]
        return cps

    def passed(outs, send, recv):
        x, y, c, chips = _place()
        cps = []
        for j, (cx, cy) in enumerate(chips):
            for a in range(nb):
                got = half(outs[a], a, 2 * cx + cy, c)
                cps.append(_remote(got, got, send.at[a, 3 + j], recv.at[a, 3 + j], (x, y, 1 - c)))
        return cps

    def start(ins, outs, sems):
        for cp in direct(ins, outs, *sems[0]):
            cp.start()

    def mid(ins, outs, sems):
        send, recv = sems[0]
        _, _, c, chips = _place()
        fwd = passed(outs, send, recv)
        for j, (cx, cy) in enumerate(chips):
            kj = 2 * cx + cy
            for a in range(na):
                got = half(outs[a], a, kj, c) if a < nb else outs[a].at[kj]
                _remote(got, got, send.at[a, j], recv.at[a, j], (cx, cy, c)).wait_recv()
                if a < nb:
                    fwd[j * nb + a].start()

    def finish(ins, outs, sems):
        send, recv = sems[0]
        x, y, c, chips = _place()
        for j, (cx, cy) in enumerate(chips):
            for a in range(nb):
                got = half(outs[a], a, 2 * cx + cy, 1 - c)
                _remote(got, got, send.at[a, 3 + j], recv.at[a, 3 + j], (x, y, 1 - c)).wait_recv()
        for cp in direct(ins, outs, send, recv) + passed(outs, send, recv):
            cp.wait_send()

    return _Plan(arrs, [_sds((NCHIP,) + a.shape, a.dtype) for a in arrs], [(na, 6)], start, finish, mid)


def _own_shards(gathered, shards):
    kchip = 2 * lax.axis_index("x") + lax.axis_index("y")
    return [lax.dynamic_update_index_in_dim(o, a, kchip, 0) for o, a in zip(gathered, shards)]


def _swap_plan(ins, outs, sems, copies):
    def start(i, o, s):
        for cp in copies(i, o, *s[0]):
            cp.start()

    def finish(i, o, s):
        for cp in copies(i, o, *s[0]):
            cp.wait()

    return _Plan(ins, outs, [sems], start, finish)


def _half_shape(shape, axis):
    return tuple(d // 2 if i == axis else d for i, d in enumerate(shape))


def _pair_exchange_plan(gs, axis=1):
    def copies(ins, outs, send, recv):
        x, y, c, _ = _place()
        cps = []
        for a in range(len(gs)):
            h = ins[a].shape[axis] // 2
            theirs = pl.ds((1 - c) * h, h)
            src = ins[a].at[:, theirs] if axis == 1 else ins[a].at[:, :, theirs]
            cps.append(_remote(src, outs[a], send.at[a], recv.at[a], (x, y, 1 - c)))
        return cps

    return _swap_plan(gs, [_sds(_half_shape(g.shape, axis), g.dtype) for g in gs], (len(gs),), copies)


def _pair_add(g, got, cidx, name, axis=1):
    half = _half_shape(g.shape, axis)
    blk, nt, _, part = _half_blocks(g.shape[1:], axis - 1)

    def body(c_ref, g_ref, o_ref, p_ref, pc_ref):
        sm = g_ref[...] + o_ref[...]
        p_ref[...] = sm
        pc_ref[...] = sm.astype(pc_ref.dtype)

    def mine(k, i, c_ref):
        j = c_ref[0] * nt + i
        return (k, j, 0) if axis == 1 else (k, 0, j)

    spec = pl.BlockSpec((1,) + blk, lambda k, i, c_ref: (k,) + part(i))
    return pl.pallas_call(
        body, name=name,
        grid_spec=pltpu.PrefetchScalarGridSpec(
            num_scalar_prefetch=1, grid=(NCHIP, nt),
            in_specs=[pl.BlockSpec((1,) + blk, mine), spec], out_specs=[spec, spec]),
        out_shape=[_sds(half), _sds(half, COMM)],
        compiler_params=_cp("parallel", "parallel"),
    )(cidx, g, got)


def _chip_exchange_plan(ps):
    def copies(ins, outs, send, recv):
        _, _, c, chips = _place()
        return [_remote(ins[a].at[2 * cx + cy], outs[a].at[j], send.at[a, j], recv.at[a, j], (cx, cy, c))
                for a in range(len(ps)) for j, (cx, cy) in enumerate(chips)]

    return _swap_plan(ps, [_sds((NCHIP - 1,) + p.shape[1:], p.dtype) for p in ps], (len(ps), 3), copies)


def _shard_sum(p, got, kidx, name, axis=1):
    full = tuple(2 * d if i == axis - 1 else d for i, d in enumerate(p.shape[1:]))
    blk, nt, _, part = _half_blocks(full, axis - 1)

    def body(k_ref, p_ref, g_ref, o_ref):
        sm = p_ref[0]
        for j in range(NCHIP - 1):
            sm = sm + g_ref[j].astype(F32)
        o_ref[...] = sm

    return pl.pallas_call(
        body, name=name,
        grid_spec=pltpu.PrefetchScalarGridSpec(
            num_scalar_prefetch=1, grid=(nt,),
            in_specs=[pl.BlockSpec((1,) + blk, lambda i, k_ref: (k_ref[0],) + part(i)),
                      pl.BlockSpec((NCHIP - 1,) + blk, lambda i, k_ref: (0,) + part(i))],
            out_specs=pl.BlockSpec(blk, lambda i, k_ref: part(i))),
        out_shape=_sds(p.shape[1:]),
        compiler_params=_cp("parallel"),
    )(kidx, p, got)


def _pair_swap_plan(rs):
    def copies(ins, outs, send, recv):
        x, y, c, _ = _place()
        return [_remote(ins[a], outs[a], send.at[a], recv.at[a], (x, y, 1 - c)) for a in range(len(rs))]

    return _swap_plan(rs, [_sds(r.shape, r.dtype) for r in rs], (len(rs),), copies)


def _allgather8_plan(v):
    def pieces(ins, outs, send, recv):
        x, y, c, chips = _place()
        me, sibling = (x, y, c), (x, y, 1 - c)

        def copy(k, block, to, src=None):
            px, py, pc = block
            slot = outs[0].at[4 * px + 2 * py + pc]
            return _remote(slot if src is None else src, slot, send.at[k], recv.at[k], to)

        first = [copy(0, me, sibling, src=ins[0])] + [copy(1 + j, me, (*chip, c), src=ins[0])
                                                      for j, chip in enumerate(chips)]
        passed = [copy(4 + j, (*chip, c), sibling) for j, chip in enumerate(chips)]
        arrivals = [copy(1 + j, (*chip, c), me) for j, chip in enumerate(chips)]
        late = [copy(0, sibling, me)] + [copy(4 + j, (*chip, 1 - c), me) for j, chip in enumerate(chips)]
        return first, passed, arrivals, late

    def start(ins, outs, sems):
        for cp in pieces(ins, outs, *sems[0])[0]:
            cp.start()

    def mid(ins, outs, sems):
        _, passed, arrivals, _ = pieces(ins, outs, *sems[0])
        for got, fwd in zip(arrivals, passed):
            got.wait_recv()
            fwd.start()

    def finish(ins, outs, sems):
        first, passed, _, late = pieces(ins, outs, *sems[0])
        for got in late:
            got.wait_recv()
        for cp in first + passed:
            cp.wait_send()

    return _Plan([v], [_sds((8,) + v.shape, v.dtype)], [(7,)], start, finish, mid)


def _own_block(gathered, v):
    me = 4 * lax.axis_index("x") + 2 * lax.axis_index("y") + lax.axis_index("c")
    return lax.dynamic_update_index_in_dim(gathered, v, me, 0)


def _sum_devices(allv, name):
    _, r, _ = allv.shape

    def body(a_ref, o_ref):
        sm = a_ref[0]
        for d in range(1, 8):
            sm = sm + a_ref[d]
        o_ref[...] = sm

    return pl.pallas_call(
        body, name=name, grid=(1,), in_specs=[_full_spec((8, r, 128))], out_specs=_full_spec((r, 128)),
        out_shape=_sds((r, 128)), compiler_params=_cp("arbitrary"),
    )(allv)


def _pack(arrs):
    flat = jnp.concatenate([a.reshape(-1) for a in arrs])
    return jnp.pad(flat, (0, (-flat.shape[0]) % 1024)).reshape(-1, 128)


def _unpack(packed, shapes):
    flat, outs, off = packed.reshape(-1), [], 0
    for shp in shapes:
        n = math.prod(shp)
        outs.append(flat[off:off + n].reshape(shp))
        off += n
    return outs


BIG = ("w_in", "w_out", "w_up", "w_down")
CONV = ("conv_ssm_w", "conv_lru_w")
WEIGHTS = ("norm_mix_pre", "w_in", "conv_ssm_w", "conv_ssm_b", "dt_bias", "a_log", "d_skip", "ssm_norm", "conv_lru_w",
           "conv_lru_b", "lru_wa", "lru_ba", "lru_wx", "lru_bx", "lru_lambda", "w_out", "norm_mix_post",
           "norm_mlp_pre", "w_up", "w_down", "norm_mlp_post")
SMALL = tuple(n for n in WEIGHTS if n not in BIG and n not in CONV)
EARLY = ("w_down", "w_up", "w_out")


def _cat_cols(g):
    return jnp.concatenate([g[k] for k in range(NCHIP)], axis=1)


class _Dist:
    def __init__(self, shards, first, cidx, kidx):
        self.shards, self.first, self.cidx, self.kidx = shards, first, cidx, kidx
        self.halves = {}

    def early_grads(self, w_down, w_up, w_out):
        self.shard_major = [w_down.reshape(NCHIP, D, D), jnp.stack([w_up[:, D * k:D * (k + 1)] for k in range(NCHIP)]),
                            w_out.reshape(NCHIP, D // NCHIP, D)]

    def late_grads(self, grads, loss):
        g_in = grads["w_in_t"].reshape(NCHIP, W_IN_SHARD, D)
        got, = _run_plan(_pair_exchange_plan([g_in], axis=2), "grad_pair_exchange_w_in")
        self.p_in, self.pc_in = _pair_add(g_in, got, self.cidx, "grad_pair_add_w_in", axis=2)
        self.small_names = [n for n in SMALL + CONV if n != "norm_mix_pre"]
        self.small_shapes = [grads[n].shape for n in self.small_names] + [(1,)]
        self.packed_small = _pack([grads[n] for n in self.small_names] + [loss.reshape(1)])

    def plan(self, key):
        if key == "norm_u":
            return _gather_plan(self.first[:1], self.first[1:], axes=[1])
        if key == "proj5":
            return _gather_plan([self.shards["w_out"], self.shards["w_up"]])
        if key == "lru_fwd":
            return _gather_plan([self.shards["w_down"]])
        if key == "lru_bwd":
            return _pair_exchange_plan(self.shard_major)
        if key == "ssd_bwd":
            return _chip_exchange_plan([pc for _, pc in self.pair])
        if key == "conv_bwd":
            return _pair_swap_plan(self.mine)
        if key == "du_norm":
            return _merge_plans(_allgather8_plan(self.packed_small), _chip_exchange_plan([self.pc_in]))
        return None

    def done(self, key, got, p):
        if key == "norm_u":
            g_in, g_cs, g_cl = _own_shards(got, self.first)
            w5, wxbc, wdt = _split_w_in_t(g_in.reshape(W_IN_COLS, D))
            p.update(w5=w5, wxbc=wxbc, wdt=wdt, conv_ssm_w=_cat_cols(g_cs), conv_lru_w=_cat_cols(g_cl))
        elif key == "proj5":
            g_out, g_up = _own_shards(got, [self.shards["w_out"], self.shards["w_up"]])
            p.update(w_out=g_out.reshape(D, D), w_up=_cat_cols(g_up))
        elif key == "lru_fwd":
            g_down, = _own_shards(got, [self.shards["w_down"]])
            p.update(w_down=g_down.reshape(DFF, D))
        elif key == "lru_bwd":
            self.pair = [_pair_add(gs, o, self.cidx, f"grad_pair_add_{n}")
                         for gs, o, n in zip(self.shard_major, got, EARLY)]
        elif key == "ssd_bwd":
            self.mine = [_shard_sum(pf, o, self.kidx, f"grad_shard_sum_{n}")
                         for (pf, _), o, n in zip(self.pair, got, EARLY)]
        elif key == "conv_bwd":
            self.halves = {n: (mine, other) for n, mine, other in zip(EARLY, self.mine, got)}
        elif key == "du_norm":
            self.all_small, self.from_chips_in = got


def kernel(x, norm_mix_pre, w_in, conv_ssm_w, conv_ssm_b, dt_bias, a_log, d_skip, ssm_norm, conv_lru_w, conv_lru_b, lru_wa, lru_ba, lru_wx, lru_bx, lru_lambda, w_out, norm_mix_post, norm_mlp_pre, w_up, w_down, norm_mlp_post, loss_target, m_norm_mix_pre, m_w_in, m_conv_ssm_w, m_conv_ssm_b, m_dt_bias, m_a_log, m_d_skip, m_ssm_norm, m_conv_lru_w, m_conv_lru_b, m_lru_wa, m_lru_ba, m_lru_wx, m_lru_bx, m_lru_lambda, m_w_out, m_norm_mix_post, m_norm_mlp_pre, m_w_up, m_w_down, m_norm_mlp_post, v_norm_mix_pre, v_w_in, v_conv_ssm_w, v_conv_ssm_b, v_dt_bias, v_a_log, v_d_skip, v_ssm_norm, v_conv_lru_w, v_conv_lru_b, v_lru_wa, v_lru_ba, v_lru_wx, v_lru_bx, v_lru_lambda, v_w_out, v_norm_mix_post, v_norm_mlp_pre, v_w_up, v_w_down, v_norm_mlp_post):
    args = locals()
    w = {n: args[n][0] for n in WEIGHTS}
    m = {n: args["m_" + n][0] for n in WEIGHTS}
    v = {n: args["v_" + n][0] for n in WEIGHTS}
    cidx = lax.axis_index("c").astype(jnp.int32).reshape(1)
    kchip = 2 * lax.axis_index("x") + lax.axis_index("y")
    to_t = lambda a: jnp.transpose(a, (2, 0, 1)).reshape(W_IN_SHARD, D)
    from_t = lambda a: jnp.transpose(a.reshape(W_IN_SHARD, 1, D), (1, 2, 0))
    shards = {n: (to_t(w_in) if n == "w_in" else w[n]).astype(MXU) for n in BIG}
    dist = _Dist(shards, [shards["w_in"], w["conv_ssm_w"], w["conv_lru_w"]], cidx, kchip.astype(jnp.int32).reshape(1))
    p = {n: (w[n].reshape(1, -1) if w[n].ndim == 1 else w[n]) for n in SMALL}

    _, grad_x, g = _local_step(x[0], loss_target[0], p, dist)

    half_in = _shard_sum(dist.p_in, dist.from_chips_in, dist.kidx, "grad_shard_sum_w_in", axis=2)
    packed_g1 = _pack([g["norm_mix_pre"]])
    all_g1, other_in = _run_plan(_merge_plans(_allgather8_plan(packed_g1), _pair_swap_plan([half_in])),
                                 "grad_pair_swap_w_in")
    halves = dist.halves

    reduced = {}
    *summed, loss = _unpack(_sum_devices(_own_block(dist.all_small, dist.packed_small), "small_sum"),
                            dist.small_shapes)
    loss = loss.reshape(())
    g1, = _unpack(_sum_devices(_own_block(all_g1, packed_g1), "small_sum_norm_mix_pre"), [g["norm_mix_pre"].shape])
    for n, s in zip(dist.small_names + ["norm_mix_pre"], summed + [g1]):
        if n in CONV:
            width = w[n].shape[1]
            reduced[n] = lax.dynamic_slice_in_dim(s, kchip * width, width, axis=1)
        else:
            reduced[n] = s.reshape(w[n].shape)

    delta, new_m, new_v = {}, {}, {}
    for n in EARLY:
        mine, other = halves[n]
        reduced[n], delta[n], new_m[n], new_v[n] = _adamw_halves(w[n], mine, other, m[n], v[n], cidx, f"adamw_{n}")
    outs_t = _adamw_halves(to_t(w_in), half_in, other_in, to_t(m_w_in), to_t(v_w_in), cidx, "adamw_w_in", axis=1)
    for d, o in zip((reduced, delta, new_m, new_v), outs_t):
        d["w_in"] = from_t(o)[0]
    for n in CONV:
        delta[n], new_m[n], new_v[n] = _adamw(w[n], reduced[n], m[n], v[n], f"adamw_{n}")
    shapes = [w[n].shape for n in SMALL]
    packed = [_pack([d[n] for n in SMALL]) for d in (w, reduced, m, v)]
    for d, out in zip((delta, new_m, new_v), _adamw(*packed, "adamw_small")):
        d.update(zip(SMALL, _unpack(out, shapes)))

    lead = lambda d: [d[n][None] for n in WEIGHTS]
    return (loss, grad_x[None], *lead(reduced), *lead(delta), *lead(new_m), *lead(new_v))
```

```python
import functools
import math

import jax
import jax.numpy as jnp
from jax import lax
from jax.experimental import pallas as pl
from jax.experimental.pallas import tpu as pltpu

F32 = jnp.float32
BF16 = jnp.bfloat16
MXU = BF16

D = 1024
DFF = 4096
NH = 16
HP = 64
NG = 2
NS = 128
CH = 128
XBC = D + 2 * NG * NS
GW = D // NG
LRU_C = 8.0
EPS = 1e-6
NCHIP = 4
W_IN_COLS = 6672
W_IN_SHARD = W_IN_COLS // NCHIP

ADAM_LR = 0.001
ADAM_B1 = 0.9
ADAM_B2 = 0.999
ADAM_EPS = 1e-08
ADAM_WD = 0.01
ADAM_STEP = 10

VMEM_LIMIT = 56 * 1024 * 1024
TK_GRAD = 2048
MID_AT = 0.7
MESH = pl.DeviceIdType.MESH


def _cp(*sem):
    return pltpu.CompilerParams(dimension_semantics=sem, vmem_limit_bytes=VMEM_LIMIT)


def _dot(a, b, ca=1, cb=0, prec=None):
    return lax.dot_general(a, b, (((ca,), (cb,)), ((), ())), precision=prec, preferred_element_type=F32)


def _mdot(a, b, ca=1, cb=0):
    return _dot(a.astype(MXU), b.astype(MXU), ca, cb)


def _bf16_parts(v, n):
    parts = []
    for i in range(n):
        p = v.astype(BF16)
        parts.append(p)
        if i < n - 1:
            v = v - p.astype(F32)
    return parts


def _xdot(a, b, passes, split_b=False):
    if split_b:
        a16 = a.astype(BF16)
        terms = [_dot(a16, p) for p in _bf16_parts(b, passes)]
    else:
        b16 = b.astype(BF16)
        terms = [_dot(p, b16) for p in _bf16_parts(a, passes)]
    return functools.reduce(lambda u, v: u + v, terms)


def _sig(x):
    return 0.5 * jnp.tanh(0.5 * x) + 0.5


def _silu(x):
    return x * _sig(x)


def _dsilu(x):
    s = _sig(x)
    return s * (1.0 + x * (1.0 - s))


def _softplus(x):
    e = jnp.exp(-jnp.abs(x))
    return jnp.maximum(x, 0.0) + jnp.where(e < 1e-4, e * (1.0 - 0.5 * e), jnp.log(1.0 + e))


_GELU_C = math.sqrt(2.0 / math.pi)


def _gelu(x):
    t = jnp.tanh(_GELU_C * (x + 0.044715 * x * x * x))
    return 0.5 * x * (1.0 + t)


def _gelu_and_grad(x):
    x2 = x * x
    t = jnp.tanh(_GELU_C * (x + 0.044715 * x * x2))
    half = 0.5 * (1.0 + t)
    return x * half, half + 0.5 * x * (1.0 - t * t) * _GELU_C * (1.0 + 3.0 * 0.044715 * x2)


def _one_minus_sq(a, la):
    x = 2.0 * la
    series = -x * (1.0 + x * (0.5 + x * (1.0 / 6.0)))
    return jnp.where(x > -0.01, series, 1.0 - a * a)


def _rms(x):
    return lax.rsqrt(jnp.mean(x * x, axis=-1, keepdims=True) + EPS)


def _rms_bwd(x, r, g, dy):
    xn = x * r
    dxh = dy * g
    m = jnp.mean(dxh * xn, axis=-1, keepdims=True)
    return r * (dxh - xn * m), jnp.sum(dy * xn, axis=0, keepdims=True)


def _row_spec(t, c, col=0):
    return pl.BlockSpec((t, c), lambda i: (i, col))


def _rev_spec(t, c, n, col=0):
    return pl.BlockSpec((t, c), lambda i: (n - 1 - i, col))


def _full_spec(shape):
    nd = len(shape)
    return pl.BlockSpec(shape, lambda *_: (0,) * nd)


def _sds(shape, dtype=F32):
    return jax.ShapeDtypeStruct(shape, dtype)


ANY = pl.BlockSpec(memory_space=pl.ANY)


class _Plan:
    def __init__(self, ins, outs, sems, start, finish, mid=None):
        self.ins, self.outs, self.sems = list(ins), list(outs), list(sems)
        self.start, self.finish, self.mid = start, finish, mid or (lambda i, o, s: None)


def _merge_plans(*plans):
    def each(fn_name, ins, outs, sems):
        i = o = s = 0
        for p in plans:
            getattr(p, fn_name)(ins[i:i + len(p.ins)], outs[o:o + len(p.outs)], sems[s:s + len(p.sems)])
            i, o, s = i + len(p.ins), o + len(p.outs), s + len(p.sems)

    return _Plan([a for p in plans for a in p.ins], [a for p in plans for a in p.outs],
                 [a for p in plans for a in p.sems], functools.partial(each, "start"),
                 functools.partial(each, "finish"), functools.partial(each, "mid"))


def _pcall(body, args, *, name, grid, in_specs, out_specs, out_shape, sem, scratch_shapes=(), plan=None):
    single = not isinstance(out_shape, (list, tuple))
    out_specs = [out_specs] if single else list(out_specs)
    out_shape = [out_shape] if single else list(out_shape)
    if plan is None:
        outs = pl.pallas_call(body, name=name, grid=grid, in_specs=list(in_specs), out_specs=out_specs,
                              out_shape=out_shape, scratch_shapes=list(scratch_shapes),
                              compiler_params=_cp(*sem))(*args)
        return outs[0] if single else outs
    n_in, n_out, n_sc, ni, no = len(in_specs), len(out_shape), len(scratch_shapes), len(plan.ins), len(plan.outs)

    def hosted(*refs):
        b0 = n_in + ni
        b1 = b0 + n_out + no
        sem_refs = refs[b1 + n_sc:]
        sems = [(sem_refs[2 * q], sem_refs[2 * q + 1]) for q in range(len(plan.sems))]
        step = functools.reduce(lambda lin, ig: lin * ig[1] + ig[0],
                                [(pl.program_id(d), g) for d, g in enumerate(grid)], 0)
        total = math.prod(grid)

        @pl.when(step == 0)
        def _():
            plan.start(refs[n_in:b0], refs[b0 + n_out:b1], sems)

        body(*refs[:n_in], *refs[b0:b0 + n_out], *refs[b1:b1 + n_sc])

        @pl.when(step == min(int(MID_AT * total), total - 1))
        def _():
            plan.mid(refs[n_in:b0], refs[b0 + n_out:b1], sems)

        @pl.when(step == total - 1)
        def _():
            plan.finish(refs[n_in:b0], refs[b0 + n_out:b1], sems)

    dma = [pltpu.SemaphoreType.DMA(shape) for shape in plan.sems for _ in range(2)]
    outs = pl.pallas_call(hosted, name=name, grid=grid, in_specs=list(in_specs) + [ANY] * ni,
                          out_specs=out_specs + [ANY] * no, out_shape=out_shape + plan.outs,
                          scratch_shapes=list(scratch_shapes) + dma,
                          compiler_params=_cp(*("arbitrary",) * len(grid)))(*args, *plan.ins)
    return (outs[0] if single else outs[:n_out]), outs[n_out:]


def _run_plan(plan, name):
    ni, no = len(plan.ins), len(plan.outs)

    def body(*refs):
        sem_refs = refs[ni + no:]
        sems = [(sem_refs[2 * q], sem_refs[2 * q + 1]) for q in range(len(plan.sems))]
        plan.start(refs[:ni], refs[ni:ni + no], sems)
        plan.mid(refs[:ni], refs[ni:ni + no], sems)
        plan.finish(refs[:ni], refs[ni:ni + no], sems)

    return pl.pallas_call(
        body, name=name, in_specs=[ANY] * ni, out_specs=[ANY] * no, out_shape=plan.outs,
        scratch_shapes=[pltpu.SemaphoreType.DMA(shape) for shape in plan.sems for _ in range(2)],
    )(*plan.ins)


def _matmul(a, b, *, name, ta=False, tb=False, tm=512, tn=1024, tk=1024, out_dtype=F32, a_fn=None, epi=None,
            epi_args=(), plan=None):
    m, k = (a.shape[1], a.shape[0]) if ta else a.shape
    n = b.shape[0] if tb else b.shape[1]
    tm, tn, tk = min(tm, m), min(tn, n), min(tk, k)
    nk = k // tk
    a_spec = pl.BlockSpec((tk, tm), lambda i, j, kk: (kk, i)) if ta else pl.BlockSpec((tm, tk), lambda i, j, kk: (i, kk))
    b_spec = pl.BlockSpec((tn, tk), lambda i, j, kk: (j, kk)) if tb else pl.BlockSpec((tk, tn), lambda i, j, kk: (kk, j))
    e_specs = [pl.BlockSpec((tm, tn), lambda i, j, kk: (i, j)) for _ in epi_args]
    ne = len(epi_args)

    def body(a_ref, b_ref, *rest):
        e_refs, o_ref = rest[:ne], rest[ne]
        av = a_ref[...]
        if a_fn is not None:
            av = a_fn(av)
        part = _mdot(av, b_ref[...], 0 if ta else 1, 1 if tb else 0)

        def finish(r):
            if epi is not None:
                r = epi(r, *[e[...] for e in e_refs])
            o_ref[...] = r.astype(o_ref.dtype)

        if nk == 1:
            finish(part)
            return
        acc_ref = rest[ne + 1]
        kk = pl.program_id(2)

        @pl.when(kk == 0)
        def _():
            acc_ref[...] = part

        @pl.when(jnp.logical_and(kk > 0, kk < nk - 1))
        def _():
            acc_ref[...] += part

        @pl.when(kk == nk - 1)
        def _():
            finish(acc_ref[...] + part)

    return _pcall(
        body, (a, b, *epi_args), name=name, grid=(m // tm, n // tn, nk),
        in_specs=[a_spec, b_spec] + e_specs,
        out_specs=pl.BlockSpec((tm, tn), lambda i, j, kk: (i, j)),
        out_shape=_sds((m, n), out_dtype),
        scratch_shapes=[pltpu.VMEM((tm, tn), F32)] if nk > 1 else [],
        sem=("parallel", "parallel", "arbitrary"), plan=plan)


def _relu2(p):
    p = jnp.maximum(p.astype(F32), 0.0)
    return p * p


def _norm_cast(x, g, name, plan=None):
    s = x.shape[0]
    t = min(512, s)

    def body(x_ref, g_ref, o_ref):
        xv = x_ref[...]
        o_ref[...] = (xv * _rms(xv) * g_ref[...]).astype(o_ref.dtype)

    return _pcall(body, (x, g), name=name, grid=(s // t,), in_specs=[_row_spec(t, D), _full_spec((1, D))],
                  out_specs=_row_spec(t, D), out_shape=_sds((s, D), MXU), sem=("parallel",), plan=plan)


def _conv_fwd(xbc_raw, proj5, dt_raw, cw_s, cb_s, cw_l, cb_l, dt_bias):
    s = xbc_raw.shape[0]
    t = min(256, s)

    def body(xs_ref, xl_ref, dtr_ref, cws_ref, cbs_ref, cwl_ref, cbl_ref, dtb_ref, xc_ref, dsl_ref, xr_ref, dt_ref,
             bs_ref, bl_ref):
        @pl.when(pl.program_id(0) == 0)
        def _():
            bs_ref[0:8, :] = jnp.zeros((8, XBC), F32)
            bl_ref[0:8, :] = jnp.zeros((8, D), F32)

        bs_ref[8:t + 8, :] = xs_ref[...]
        bl_ref[8:t + 8, :] = xl_ref[...].astype(F32)

        def conv(buf, w_ref, b_ref):
            acc = b_ref[...] + w_ref[3:4, :] * buf[8:t + 8, :]
            for k in (1, 2, 3):
                acc = acc + w_ref[3 - k:4 - k, :] * buf[8 - k:t + 8 - k, :]
            return acc

        pre = conv(bs_ref, cws_ref, cbs_ref)
        sg = _sig(pre)
        xc_ref[...] = pre * sg
        dsl_ref[...] = (sg * (1.0 + pre * (1.0 - sg))).astype(dsl_ref.dtype)
        xr_ref[...] = conv(bl_ref, cwl_ref, cbl_ref)
        dt_ref[...] = _softplus(dtr_ref[...] + dtb_ref[...])
        bs_ref[0:8, :] = bs_ref[t:t + 8, :]
        bl_ref[0:8, :] = bl_ref[t:t + 8, :]

    return pl.pallas_call(
        body, name="conv_fwd", grid=(s // t,),
        in_specs=[_row_spec(t, XBC), _row_spec(t, D, 2), _row_spec(t, 128), _full_spec((4, XBC)),
                  _full_spec((1, XBC)), _full_spec((4, D)), _full_spec((1, D)), _full_spec((1, 128))],
        out_specs=[_row_spec(t, XBC), _row_spec(t, XBC), _row_spec(t, D), _row_spec(t, 128)],
        out_shape=[_sds((s, XBC)), _sds((s, XBC), BF16), _sds((s, D)), _sds((s, 128))],
        scratch_shapes=[pltpu.VMEM((t + 8, XBC), F32), pltpu.VMEM((t + 8, D), F32)],
        compiler_params=_cp("arbitrary"),
    )(xbc_raw, proj5, dt_raw, cw_s, cb_s, cw_l, cb_l, dt_bias)


def _ssd_chunk_setup(dt_ref, alog_ref, e_ref, at_ref, dtt_ref):
    lane = lax.broadcasted_iota(jnp.int32, (CH, 128), 1)
    row = lax.broadcasted_iota(jnp.int32, (CH, 128), 0)
    lane1 = lax.broadcasted_iota(jnp.int32, (1, 128), 1)
    a = jnp.where(lane1 < NH, -jnp.exp(alog_ref[...]), 0.0)
    dtv = dt_ref[...]
    adt = dtv * a
    tril = row >= lane
    acum = _xdot(tril.astype(F32), adt, 3, split_b=True)
    alast = jnp.sum(adt, axis=0, keepdims=True)
    at_ref[...] = acum.T
    dtt_ref[...] = dtv.T
    e = e_ref[...]
    ea_x = _xdot(jnp.exp(acum), e, 2)
    ws = jnp.exp(alast - acum) * dtv
    ws_x = _xdot(ws, e, 2)
    eal = jnp.exp(alast)
    eal_x = jnp.max(_xdot(jnp.broadcast_to(eal, (8, 128)), e, 3), axis=0, keepdims=True)
    return dict(lane=lane, row=row, tril=tril, a=a, dtv=dtv, acum=acum, alast=alast, ea_x=ea_x, ws=ws, ws_x=ws_x,
                eal=eal, eal_x=eal_x)


def _head_decay(cs, at_ref, dtt_ref, h):
    col = jnp.sum(jnp.where(cs["lane"] == h, cs["acum"], 0.0), axis=1, keepdims=True)
    ld = jnp.where(cs["tril"], jnp.exp(jnp.minimum(col - at_ref[h:h + 1, :], 0.0)), 0.0)
    return ld, dtt_ref[h:h + 1, :]


def _ssd_fwd(xbc_c, dt, proj5, a_log, dskip_x, ssm_norm, expand):
    s = xbc_c.shape[0]
    nc = s // CH

    def body(xc_ref, dt_ref, z_ref, alog_ref, dsk_ref, ng_ref, e_ref, y_ref, ya_ref, st_ref, h_ref, at_ref, dtt_ref,
             yd_ref):
        @pl.when(pl.program_id(0) == 0)
        def _():
            h_ref[...] = jnp.zeros_like(h_ref)

        cs = _ssd_chunk_setup(dt_ref, alog_ref, e_ref, at_ref, dtt_ref)
        lane = cs["lane"]
        for g in range(NG):
            gs = slice(GW * g, GW * (g + 1))
            bg = xc_ref[:, D + NS * g:D + NS * (g + 1)]
            cg = xc_ref[:, D + NG * NS + NS * g:D + NG * NS + NS * (g + 1)]
            cb = _mdot(cg, bg, 1, 1)
            for j in range(4 * g, 4 * g + 4):
                ps = slice(128 * j, 128 * (j + 1))
                xp = xc_ref[:, ps]
                acc = jnp.zeros((CH, 128), F32)
                for hf in range(2):
                    ld, rowdt = _head_decay(cs, at_ref, dtt_ref, 2 * j + hf)
                    hm = (lane >= HP) if hf else (lane < HP)
                    acc = acc + _mdot(cb * ld * rowdt, jnp.where(hm, xp, 0.0))
                yd_ref[:, ps] = acc
            hg = h_ref[:, gs]
            yd_ref[:, gs] += _mdot(cg, hg) * cs["ea_x"][:, gs]
            st = _mdot(bg, xc_ref[:, gs] * cs["ws_x"][:, gs], 0, 0)
            st_ref[0, :, gs] = hg
            h_ref[:, gs] = cs["eal_x"][:, gs] * hg + st
        y = yd_ref[...] + dsk_ref[...] * xc_ref[:, 0:D]
        y_ref[...] = y
        yg = y * _silu(z_ref[...].astype(F32))
        for g in range(NG):
            gs = slice(GW * g, GW * (g + 1))
            seg = yg[:, gs]
            ya_ref[:, gs] = seg * _rms(seg) * ng_ref[:, gs]

    return pl.pallas_call(
        body, name="ssd_fwd", grid=(nc,),
        in_specs=[_row_spec(CH, XBC), _row_spec(CH, 128), _row_spec(CH, D, 0), _full_spec((1, 128)),
                  _full_spec((1, D)), _full_spec((1, D)), _full_spec((128, D))],
        out_specs=[_row_spec(CH, D), _row_spec(CH, D), pl.BlockSpec((1, NS, D), lambda i: (i, 0, 0))],
        out_shape=[_sds((s, D)), _sds((s, D)), _sds((nc, NS, D))],
        scratch_shapes=[pltpu.VMEM((NS, D), F32), pltpu.VMEM((128, 128), F32), pltpu.VMEM((128, 128), F32),
                        pltpu.VMEM((CH, D), F32)],
        compiler_params=_cp("arbitrary"),
    )(xbc_c, dt, proj5, a_log, dskip_x, ssm_norm, expand)


def _lru_gates(xr, wa_ref, wx_ref, ba_ref, bx_ref, lam_ref):
    gr = _sig(_mdot(xr, wa_ref[...]) + ba_ref[...])
    gi = _sig(_mdot(xr, wx_ref[...]) + bx_ref[...])
    sp = _softplus(-lam_ref[...])
    la = -LRU_C * gr * sp
    a = jnp.exp(la)
    oms = _one_minus_sq(a, la)
    inv_mult = lax.rsqrt(oms)
    return gr, gi, sp, a, oms * inv_mult, inv_mult


def _blocked_scan(a, u, carry_ref, a_ref, u_ref, c_ref, out_ref, reverse):
    t = a.shape[0]
    ns = t // 8

    def combine(av, uv, idx, n, sh):
        m = (idx < n - sh) if reverse else (idx >= sh)
        by = n - sh if reverse else sh
        return jnp.where(m, av * pltpu.roll(av, by, 0), av), jnp.where(m, uv + av * pltpu.roll(uv, by, 0), uv)

    row = lax.broadcasted_iota(jnp.int32, (t, D), 0)
    rin = jnp.bitwise_and(row, 7)
    for sh in (1, 2, 4):
        m = (rin < 8 - sh) if reverse else (rin >= sh)
        by = t - sh if reverse else sh
        a, u = jnp.where(m, a * pltpu.roll(a, by, 0), a), jnp.where(m, u + a * pltpu.roll(u, by, 0), u)
    a_ref[...] = a
    u_ref[...] = u
    edge = 0 if reverse else 7
    for j in range(ns):
        c_ref[j:j + 1, :] = a_ref[8 * j + edge:8 * j + edge + 1, :]
    at = c_ref[...]
    for j in range(ns):
        c_ref[j:j + 1, :] = u_ref[8 * j + edge:8 * j + edge + 1, :]
    ut = c_ref[...]
    srow = lax.broadcasted_iota(jnp.int32, (ns, D), 0)
    sh = 1
    while sh < ns:
        at, ut = combine(at, ut, srow, ns, sh)
        sh *= 2
    cv = carry_ref[0:1, :]
    ends = ut + at * cv
    last = 0 if reverse else ns - 1
    first = ns - 1 if reverse else 0
    c_ref[...] = jnp.where(srow == first, cv, pltpu.roll(ends, first if reverse else 1, 0))
    carry_ref[0:1, :] = jnp.sum(jnp.where(srow == last, ends, 0.0), axis=0, keepdims=True)
    for j in range(ns):
        sl = slice(8 * j, 8 * j + 8)
        out_ref[sl, :] = u_ref[sl, :] + a_ref[sl, :] * c_ref[j:j + 1, :]


def _lru_fwd(xr, proj5, ya, wa_bd, wx_bd, ba, bx, lam, plan=None):
    s = xr.shape[0]
    t = min(256, s)

    def body(xr_ref, g_ref, ga_ref, gb_ref, ya_ref, wa_ref, wx_ref, ba_ref, bx_ref, lam_ref, h_ref, mg_ref, hc_ref,
             a_ref, u_ref, c_ref):
        @pl.when(pl.program_id(0) == 0)
        def _():
            hc_ref[...] = jnp.zeros_like(hc_ref)

        xrv = xr_ref[...]
        _, gi, _, a, mult, _ = _lru_gates(xrv, wa_ref, wx_ref, ba_ref, bx_ref, lam_ref)
        _blocked_scan(a, mult * gi * xrv, hc_ref, a_ref, u_ref, c_ref, h_ref, reverse=False)
        yb = h_ref[...] * _gelu(g_ref[...].astype(F32))
        mg_ref[...] = (_sig(ga_ref[...].astype(F32)) * ya_ref[...]
                       + _sig(gb_ref[...].astype(F32)) * yb).astype(mg_ref.dtype)

    return _pcall(
        body, (xr, proj5, proj5, proj5, ya, wa_bd, wx_bd, ba, bx, lam), name="lru_fwd", grid=(s // t,),
        in_specs=[_row_spec(t, D), _row_spec(t, D, 1), _row_spec(t, D, 3), _row_spec(t, D, 4), _row_spec(t, D),
                  _full_spec((D, D)), _full_spec((D, D)), _full_spec((1, D)), _full_spec((1, D)), _full_spec((1, D))],
        out_specs=[_row_spec(t, D), _row_spec(t, D)],
        out_shape=[_sds((s, D)), _sds((s, D), MXU)],
        scratch_shapes=[pltpu.VMEM((8, D), F32), pltpu.VMEM((t, D), F32), pltpu.VMEM((t, D), F32),
                        pltpu.VMEM((t // 8, D), F32)],
        sem=("arbitrary",), plan=plan)


def _out_proj(merged, w_out, x, g2, g3):
    s = x.shape[0]
    t = min(256, s)

    def body(mg_ref, w_ref, x_ref, g2_ref, g3_ref, mix_ref, h1_ref, v_ref):
        mix = _mdot(mg_ref[...], w_ref[...])
        mix_ref[...] = mix
        h1 = x_ref[...] + mix * _rms(mix) * g2_ref[...]
        h1_ref[...] = h1
        v_ref[...] = (h1 * _rms(h1) * g3_ref[...]).astype(v_ref.dtype)

    return pl.pallas_call(
        body, name="out_proj", grid=(s // t,),
        in_specs=[_row_spec(t, D), _full_spec((D, D)), _row_spec(t, D), _full_spec((1, D)), _full_spec((1, D))],
        out_specs=[_row_spec(t, D), _row_spec(t, D), _row_spec(t, D)],
        out_shape=[_sds((s, D)), _sds((s, D)), _sds((s, D), MXU)],
        compiler_params=_cp("parallel"),
    )(merged, w_out, x, g2, g3)


def _down_loss(pre, w_down, h1, target, g4):
    s = pre.shape[0]
    t = min(256, s)

    def body(pre_ref, w_ref, h1_ref, tg_ref, g4_ref, dout_ref, dff_ref, loss_ref, dg4_ref):
        @pl.when(pl.program_id(0) == 0)
        def _():
            loss_ref[...] = jnp.zeros_like(loss_ref)
            dg4_ref[...] = jnp.zeros_like(dg4_ref)

        ff = _mdot(_relu2(pre_ref[...]), w_ref[...])
        r4 = _rms(ff)
        g4v = g4_ref[...]
        diff = h1_ref[...] + ff * r4 * g4v - tg_ref[...]
        sq = jnp.sum(jnp.sum(diff * diff, axis=1, keepdims=True), axis=0, keepdims=True)
        loss_ref[...] += (0.5 / D) * sq
        dout = diff * (1.0 / D)
        dout_ref[...] = dout
        dff, dg = _rms_bwd(ff, r4, g4v, dout)
        dff_ref[...] = dff.astype(dff_ref.dtype)
        dg4_ref[...] += dg

    return pl.pallas_call(
        body, name="down_loss", grid=(s // t,),
        in_specs=[_row_spec(t, DFF), _full_spec((DFF, D)), _row_spec(t, D), _row_spec(t, D), _full_spec((1, D))],
        out_specs=[_row_spec(t, D), _row_spec(t, D), _full_spec((1, 128)), _full_spec((1, D))],
        out_shape=[_sds((s, D)), _sds((s, D), MXU), _sds((1, 128)), _sds((1, D))],
        compiler_params=_cp("arbitrary"),
    )(pre, w_down, h1, target, g4)


def _dv_norms(dpre, w_up, h1, mix, dout, g3, g2):
    s = h1.shape[0]
    t = min(256, s)

    def body(dp_ref, w_ref, h1_ref, mix_ref, dout_ref, g3_ref, g2_ref, dh1_ref, dmix_ref, dg3_ref, dg2_ref):
        @pl.when(pl.program_id(0) == 0)
        def _():
            dg3_ref[...] = jnp.zeros_like(dg3_ref)
            dg2_ref[...] = jnp.zeros_like(dg2_ref)

        dv = _mdot(dp_ref[...], w_ref[...], 1, 1)
        h1 = h1_ref[...]
        dh1n, dg3 = _rms_bwd(h1, _rms(h1), g3_ref[...], dv)
        dh1 = dout_ref[...] + dh1n
        dh1_ref[...] = dh1
        mix = mix_ref[...]
        dmix, dg2 = _rms_bwd(mix, _rms(mix), g2_ref[...], dh1)
        dmix_ref[...] = dmix.astype(dmix_ref.dtype)
        dg3_ref[...] += dg3
        dg2_ref[...] += dg2

    return pl.pallas_call(
        body, name="dv_norms", grid=(s // t,),
        in_specs=[_row_spec(t, DFF), _full_spec((D, DFF)), _row_spec(t, D), _row_spec(t, D), _row_spec(t, D),
                  _full_spec((1, D)), _full_spec((1, D))],
        out_specs=[_row_spec(t, D), _row_spec(t, D), _full_spec((1, D)), _full_spec((1, D))],
        out_shape=[_sds((s, D)), _sds((s, D), MXU), _sds((1, D)), _sds((1, D))],
        compiler_params=_cp("arbitrary"),
    )(dpre, w_up, h1, mix, dout, g3, g2)


def _lru_bwd(dmerged, ya, xr, h, proj5, wa_bd, wx_bd, ba, bx, lam, plan=None):
    s = xr.shape[0]
    t = min(128, s)
    n = s // t
    rs = functools.partial(_rev_spec, t, D, n)

    def body(dm_ref, ya_ref, xr_ref, h_ref, hp_ref, g_ref, ga_ref, gb_ref, wa_ref, wx_ref, ba_ref, bx_ref, lam_ref,
             dya_ref, dga_ref, dgb_ref, dg_ref, dxr_ref, dpr_ref, dpi_ref, dlam_ref, dba_ref, dbx_ref, gc_ref,
             af_ref, an_ref, us_ref, c_ref, gs_ref):
        i = pl.program_id(0)

        @pl.when(i == 0)
        def _():
            gc_ref[...] = jnp.zeros_like(gc_ref)
            af_ref[...] = jnp.zeros_like(af_ref)
            dlam_ref[...] = jnp.zeros_like(dlam_ref)
            dba_ref[...] = jnp.zeros_like(dba_ref)
            dbx_ref[...] = jnp.zeros_like(dbx_ref)

        xrv = xr_ref[...]
        gr, gi, sp, a, mult, inv_mult = _lru_gates(xrv, wa_ref, wx_ref, ba_ref, bx_ref, lam_ref)
        hv = h_ref[...]
        dm = dm_ref[...]
        sa = _sig(ga_ref[...].astype(F32))
        sb = _sig(gb_ref[...].astype(F32))
        gel, dgel = _gelu_and_grad(g_ref[...].astype(F32))
        dya = dm * sa
        dya_ref[...] = dya
        dga_ref[...] = (dya * ya_ref[...] * (1.0 - sa)).astype(dga_ref.dtype)
        dyb = dm * sb
        dybh = dyb * hv
        dgb_ref[...] = (dybh * gel * (1.0 - sb)).astype(dgb_ref.dtype)
        dg_ref[...] = (dybh * dgel).astype(dg_ref.dtype)
        row = lax.broadcasted_iota(jnp.int32, (t, D), 0)
        an = jnp.where(row == t - 1, af_ref[0:1, :], pltpu.roll(a, t - 1, 0))
        _blocked_scan(an, dyb * gel, gc_ref, an_ref, us_ref, c_ref, gs_ref, reverse=True)
        gfull = gs_ref[...]
        af_ref[0:1, :] = jnp.sum(jnp.where(row == 0, a, 0.0), axis=0, keepdims=True)
        hlast = jnp.where(i == n - 1, 0.0, hp_ref[7:8, :])
        hprev = jnp.where(row == 0, hlast, pltpu.roll(hv, 1, 0))
        gx = gfull * xrv
        dgi = gx * mult
        dla = a * (gfull * hprev - gx * gi * a * inv_mult)
        dgr = dla * (-LRU_C * sp)
        dsp = jnp.sum(dla * (-LRU_C * gr), axis=0, keepdims=True)
        dlam_ref[...] += dsp * (-_sig(-lam_ref[...]))
        dpr = dgr * gr * (1.0 - gr)
        dpi = dgi * gi * (1.0 - gi)
        dpr_ref[...] = dpr.astype(dpr_ref.dtype)
        dpi_ref[...] = dpi.astype(dpi_ref.dtype)
        dba_ref[...] += jnp.sum(dpr, axis=0, keepdims=True)
        dbx_ref[...] += jnp.sum(dpi, axis=0, keepdims=True)
        dxr_ref[...] = gfull * mult * gi + _mdot(dpr, wa_ref[...], 1, 1) + _mdot(dpi, wx_ref[...], 1, 1)

    hp_spec = pl.BlockSpec((8, D), lambda i: (jnp.maximum((n - 1 - i) * (t // 8) - 1, 0), 0))
    return _pcall(
        body, (dmerged, ya, xr, h, h, proj5, proj5, proj5, wa_bd, wx_bd, ba, bx, lam), name="lru_bwd", grid=(n,),
        in_specs=[rs(), rs(), rs(), rs(), hp_spec, rs(1), rs(3), rs(4), _full_spec((D, D)), _full_spec((D, D)),
                  _full_spec((1, D)), _full_spec((1, D)), _full_spec((1, D))],
        out_specs=[rs(), rs(), rs(), rs(), rs(), rs(), rs(), _full_spec((1, D)), _full_spec((1, D)),
                   _full_spec((1, D))],
        out_shape=[_sds((s, D)), _sds((s, D), MXU), _sds((s, D), MXU), _sds((s, D), MXU), _sds((s, D)),
                   _sds((s, D), MXU), _sds((s, D), MXU), _sds((1, D)), _sds((1, D)), _sds((1, D))],
        scratch_shapes=[pltpu.VMEM((8, D), F32), pltpu.VMEM((8, D), F32), pltpu.VMEM((t, D), F32),
                        pltpu.VMEM((t, D), F32), pltpu.VMEM((t // 8, D), F32), pltpu.VMEM((t, D), F32)],
        sem=("arbitrary",), plan=plan)


def _ssd_bwd(dya, y, proj5, xbc_c, dt, states, a_log, dskip_x, ssm_norm, expand, reduce_, plan=None):
    s = xbc_c.shape[0]
    nc = s // CH
    rv = functools.partial(_rev_spec, CH)

    def body(dya_ref, y_ref, z_ref, xc_ref, dt_ref, st_ref, alog_ref, dsk_ref, ng_ref, e_ref, et_ref, dz_ref,
             dxc_ref, ddt_ref, dng_ref, ddsk_ref, dalog_ref, dh_ref, at_ref, dtt_ref, dat_ref, ddtt_ref, dy_ref,
             yoffdy_ref, xbds_ref):
        @pl.when(pl.program_id(0) == 0)
        def _():
            dh_ref[...] = jnp.zeros_like(dh_ref)
            dng_ref[...] = jnp.zeros_like(dng_ref)
            ddsk_ref[...] = jnp.zeros_like(ddsk_ref)
            dalog_ref[...] = jnp.zeros_like(dalog_ref)

        cs = _ssd_chunk_setup(dt_ref, alog_ref, e_ref, at_ref, dtt_ref)
        lane, row = cs["lane"], cs["row"]
        et = et_ref[...]
        for g in range(NG):
            gs = slice(GW * g, GW * (g + 1))
            yv = y_ref[:, gs]
            zv = z_ref[:, gs].astype(F32)
            sz = _silu(zv)
            yg = yv * sz
            dyav = dya_ref[:, gs]
            dyg, dng = _rms_bwd(yg, _rms(yg), ng_ref[:, gs], dyav)
            dng_ref[:, gs] += dng
            dy_ref[:, gs] = dyg * sz
            dz_ref[:, gs] = (dyg * yv * _dsilu(zv)).astype(dz_ref.dtype)
        dyv = dy_ref[...]
        xs = xc_ref[:, 0:D]
        ddsk_ref[...] += jnp.sum(dyv * xs, axis=0, keepdims=True)
        dxc_ref[:, 0:D] = dyv * dsk_ref[...]
        dat_ref[...] = jnp.zeros_like(dat_ref)
        ddtt_ref[...] = jnp.zeros_like(ddtt_ref)
        hh = jnp.sum(dh_ref[...] * st_ref[0], axis=0, keepdims=True)
        deal = jnp.max(_xdot(jnp.broadcast_to(hh, (8, D)), et, 3), axis=0, keepdims=True)
        d_acum = jnp.zeros((CH, 128), F32)
        for g in range(NG):
            gs = slice(GW * g, GW * (g + 1))
            bs_ = slice(D + NS * g, D + NS * (g + 1))
            cs_ = slice(D + NG * NS + NS * g, D + NG * NS + NS * (g + 1))
            bg = xc_ref[:, bs_]
            cg = xc_ref[:, cs_]
            cb = _mdot(cg, bg, 1, 1)
            hg = st_ref[0, :, gs]
            dhg = dh_ref[:, gs]
            dyg_ = dy_ref[:, gs]
            xsg = xc_ref[:, gs]
            ea = cs["ea_x"][:, gs]
            wsx = cs["ws_x"][:, gs]
            dp = dyg_ * ea
            yoffdy_ref[:, gs] = dp * _mdot(cg, hg)
            dc = _mdot(dp, hg, 1, 1)
            dhprev = _mdot(cg, dp, 0, 0)
            bds = _mdot(bg, dhg)
            dxc_ref[:, gs] += wsx * bds
            xbds_ref[:, gs] = xsg * bds
            db = _mdot(xsg * wsx, dhg, 1, 1)
            dh_ref[:, gs] = dhprev + cs["eal_x"][:, gs] * dhg
            dcbs = jnp.zeros((CH, CH), F32)
            for j in range(4 * g, 4 * g + 4):
                ps = slice(128 * j, 128 * (j + 1))
                xp = xc_ref[:, ps]
                dyp = dy_ref[:, ps]
                dxacc = jnp.zeros((CH, 128), F32)
                for hf in range(2):
                    hd = 2 * j + hf
                    ld, rowdt = _head_decay(cs, at_ref, dtt_ref, hd)
                    hm = (lane >= HP) if hf else (lane < HP)
                    dym = jnp.where(hm, dyp, 0.0)
                    w = cb * ld * rowdt
                    dw = _mdot(dym, jnp.where(hm, xp, 0.0), 1, 1)
                    dxacc = dxacc + _mdot(w, dym, 0, 0)
                    nm = dw * w
                    ddtt_ref[hd:hd + 1, :] += jnp.sum(dw * cb * ld, axis=0, keepdims=True)
                    d_acum = d_acum + jnp.where(lane == hd, jnp.sum(nm, axis=1, keepdims=True), 0.0)
                    dat_ref[hd:hd + 1, :] -= jnp.sum(nm, axis=0, keepdims=True)
                    dcbs = dcbs + dw * ld * rowdt
                dxc_ref[:, ps] += dxacc
            dxc_ref[:, bs_] = db + _mdot(dcbs, cg, 0, 0)
            dxc_ref[:, cs_] = dc + _mdot(dcbs, bg)
        dws = _xdot(xbds_ref[...], et, 2)
        ws = cs["ws"]
        d_acum = d_acum - dws * ws + _xdot(yoffdy_ref[...], et, 2) + dat_ref[...].T
        d_alast = jnp.sum(dws * ws, axis=0, keepdims=True) + deal * cs["eal"]
        d_acum = d_acum + jnp.where(row == CH - 1, d_alast, 0.0)
        triu = row <= lane
        d_adt = _xdot(triu.astype(F32), d_acum, 3, split_b=True)
        ddt_ref[...] = dws * jnp.exp(cs["alast"] - cs["acum"]) + ddtt_ref[...].T + d_adt * cs["a"]
        dalog_ref[...] += jnp.sum(d_adt * cs["dtv"], axis=0, keepdims=True) * cs["a"]

    return _pcall(
        body, (dya, y, proj5, xbc_c, dt, states, a_log, dskip_x, ssm_norm, expand, reduce_), name="ssd_bwd",
        grid=(nc,),
        in_specs=[rv(D, nc), rv(D, nc), rv(D, nc, 0), rv(XBC, nc), rv(128, nc),
                  pl.BlockSpec((1, NS, D), lambda i: (nc - 1 - i, 0, 0)), _full_spec((1, 128)), _full_spec((1, D)),
                  _full_spec((1, D)), _full_spec((128, D)), _full_spec((D, 128))],
        out_specs=[rv(D, nc), rv(XBC, nc), rv(128, nc), _full_spec((1, D)), _full_spec((1, D)),
                   _full_spec((1, 128))],
        out_shape=[_sds((s, D), MXU), _sds((s, XBC)), _sds((s, 128)), _sds((1, D)), _sds((1, D)), _sds((1, 128))],
        scratch_shapes=[pltpu.VMEM((NS, D), F32), pltpu.VMEM((128, 128), F32), pltpu.VMEM((128, 128), F32),
                        pltpu.VMEM((128, 128), F32), pltpu.VMEM((128, 128), F32), pltpu.VMEM((CH, D), F32),
                        pltpu.VMEM((CH, D), F32), pltpu.VMEM((CH, D), F32)],
        sem=("arbitrary",), plan=plan)


def _conv_bwd(dxbc_c, dsilu, dxr, ddt, xbc_raw, proj5, dt_raw, cw_s, cw_l, dt_bias, plan=None):
    s = xbc_raw.shape[0]
    t = min(256, s)
    n = s // t

    def body(dxc_ref, dsl_ref, dxr_ref, ddt_ref, xs_ref, xl_ref, dtr_ref, cws_ref, cwl_ref, dtb_ref, dxs_ref,
             dxl_ref, ddtr_ref, dcws_ref, dcbs_ref, dcwl_ref, dcbl_ref, ddtb_ref, ds_ref, dl_ref):
        @pl.when(pl.program_id(0) == 0)
        def _():
            ds_ref[t:t + 8, :] = jnp.zeros((8, XBC), F32)
            dl_ref[t:t + 8, :] = jnp.zeros((8, D), F32)
            for r in (dcws_ref, dcbs_ref, dcwl_ref, dcbl_ref, ddtb_ref):
                r[...] = jnp.zeros_like(r)

        ds_ref[0:t, :] = dxc_ref[...] * dsl_ref[...].astype(F32)
        dl_ref[0:t, :] = dxr_ref[...]

        def back(dbuf, x_ref, w_ref, dx_ref, dw_ref, db_ref):
            xv = x_ref[...].astype(F32)
            dpre = dbuf[0:t, :]
            dx = w_ref[3:4, :] * dpre
            dw_ref[3:4, :] += jnp.sum(dpre * xv, axis=0, keepdims=True)
            db_ref[...] += jnp.sum(dpre, axis=0, keepdims=True)
            for k in (1, 2, 3):
                ahead = dbuf[k:t + k, :]
                dx = dx + w_ref[3 - k:4 - k, :] * ahead
                dw_ref[3 - k:4 - k, :] += jnp.sum(ahead * xv, axis=0, keepdims=True)
            dx_ref[...] = dx.astype(dx_ref.dtype)
            dbuf[t:t + 8, :] = dbuf[0:8, :]

        back(ds_ref, xs_ref, cws_ref, dxs_ref, dcws_ref, dcbs_ref)
        back(dl_ref, xl_ref, cwl_ref, dxl_ref, dcwl_ref, dcbl_ref)
        ddtr = ddt_ref[...] * _sig(dtr_ref[...] + dtb_ref[...])
        ddtr_ref[...] = ddtr.astype(ddtr_ref.dtype)
        ddtb_ref[...] += jnp.sum(ddtr, axis=0, keepdims=True)

    rv = functools.partial(_rev_spec, t)
    return _pcall(
        body, (dxbc_c, dsilu, dxr, ddt, xbc_raw, proj5, dt_raw, cw_s, cw_l, dt_bias), name="conv_bwd", grid=(n,),
        in_specs=[rv(XBC, n), rv(XBC, n), rv(D, n), rv(128, n), rv(XBC, n), rv(D, n, 2), rv(128, n),
                  _full_spec((4, XBC)), _full_spec((4, D)), _full_spec((1, 128))],
        out_specs=[rv(XBC, n), rv(D, n), rv(128, n), _full_spec((4, XBC)), _full_spec((1, XBC)), _full_spec((4, D)),
                   _full_spec((1, D)), _full_spec((1, 128))],
        out_shape=[_sds((s, XBC), MXU), _sds((s, D), MXU), _sds((s, 128), MXU), _sds((4, XBC)), _sds((1, XBC)),
                   _sds((4, D)), _sds((1, D)), _sds((1, 128))],
        scratch_shapes=[pltpu.VMEM((t + 8, XBC), F32), pltpu.VMEM((t + 8, D), F32)],
        sem=("arbitrary",), plan=plan)


def _du_norm(pieces5, dxbc, ddtr, w5, wxbc, wdt, x, dh1, g1, plan=None):
    s = x.shape[0]
    t = min(256, s)

    def body(p0, p1, p2, p3, p4, dxbc_ref, ddtr_ref, w5_ref, wx_ref, wd_ref, x_ref, dh1_ref, g1_ref, dx_ref, dg1_ref):
        @pl.when(pl.program_id(0) == 0)
        def _():
            dg1_ref[...] = jnp.zeros_like(dg1_ref)

        du = _mdot(dxbc_ref[...], wx_ref[...]) + _mdot(ddtr_ref[...], wd_ref[...])
        for b, p in enumerate((p0, p1, p2, p3, p4)):
            du = du + _mdot(p[...], w5_ref[D * b:D * (b + 1), :])
        xv = x_ref[...]
        dxn, dg1 = _rms_bwd(xv, _rms(xv), g1_ref[...], du)
        dx_ref[...] = dh1_ref[...] + dxn
        dg1_ref[...] += dg1

    return _pcall(
        body, (*pieces5, dxbc, ddtr, w5, wxbc, wdt, x, dh1, g1), name="du_norm", grid=(s // t,),
        in_specs=[_row_spec(t, D)] * 5 + [_row_spec(t, XBC), _row_spec(t, 128), _full_spec((5 * D, D)),
                                          _full_spec((XBC, D)), _full_spec((128, D)), _row_spec(t, D),
                                          _row_spec(t, D), _full_spec((1, D))],
        out_specs=[_row_spec(t, D), _full_spec((1, D))],
        out_shape=[_sds((s, D)), _sds((1, D))],
        sem=("arbitrary",), plan=plan)


def _adamw(w, g, m, v, name):
    r, c = w.shape
    t = r
    if r * c > 256 * 1024:
        t = next(cand for cand in (512, 256, 128, 64, 32, 16, 8) if r % cand == 0 and cand * c <= 512 * 1024)
    bc1 = 1.0 - ADAM_B1 ** ADAM_STEP
    bc2 = 1.0 - ADAM_B2 ** ADAM_STEP

    def body(w_ref, g_ref, m_ref, v_ref, d_ref, nm_ref, nv_ref):
        gv = g_ref[...]
        nm = ADAM_B1 * m_ref[...] + (1.0 - ADAM_B1) * gv
        nv = ADAM_B2 * v_ref[...] + (1.0 - ADAM_B2) * (gv * gv)
        nm_ref[...] = nm
        nv_ref[...] = nv
        d_ref[...] = -ADAM_LR * ((nm / bc1) / (jnp.sqrt(nv / bc2) + ADAM_EPS) + ADAM_WD * w_ref[...])

    spec = pl.BlockSpec((t, c), lambda i: (i, 0))
    return pl.pallas_call(
        body, name=name, grid=(r // t,), in_specs=[spec] * 4, out_specs=[spec] * 3,
        out_shape=[_sds((r, c))] * 3, compiler_params=_cp("parallel"),
    )(w, g, m, v)


def _half_blocks(shape, axis):
    r, c = shape
    if axis == 0:
        t = 256 if (r // 2) % 256 == 0 else 128
        nb = (r // 2) // t
        return (t, c), nb, (lambda i: (i, 0)), (lambda i: (i % nb, 0))
    nb = (c // 2) // 128
    return (r, 128), nb, (lambda i: (0, i)), (lambda i: (0, i % nb))


def _adamw_halves(w, g_mine, g_other, m, v, cidx, name, axis=0):
    r, c = w.shape
    blk, nb, whole, part = _half_blocks(w.shape, axis)
    bc1 = 1.0 - ADAM_B1 ** ADAM_STEP
    bc2 = 1.0 - ADAM_B2 ** ADAM_STEP

    def body(c_ref, w_ref, gm_ref, go_ref, m_ref, v_ref, g_ref, d_ref, nm_ref, nv_ref):
        mine = (pl.program_id(0) // nb) == c_ref[0]
        gv = jnp.where(mine, gm_ref[...], go_ref[...])
        g_ref[...] = gv
        nm = ADAM_B1 * m_ref[...] + (1.0 - ADAM_B1) * gv
        nv = ADAM_B2 * v_ref[...] + (1.0 - ADAM_B2) * (gv * gv)
        nm_ref[...] = nm
        nv_ref[...] = nv
        d_ref[...] = -ADAM_LR * ((nm / bc1) / (jnp.sqrt(nv / bc2) + ADAM_EPS) + ADAM_WD * w_ref[...])

    spec = pl.BlockSpec(blk, lambda i, c_ref: whole(i))
    half = pl.BlockSpec(blk, lambda i, c_ref: part(i))
    return pl.pallas_call(
        body, name=name,
        grid_spec=pltpu.PrefetchScalarGridSpec(num_scalar_prefetch=1, grid=(2 * nb,),
                                               in_specs=[spec, half, half, spec, spec], out_specs=[spec] * 4),
        out_shape=[_sds((r, c))] * 4, compiler_params=_cp("parallel"),
    )(cidx, w, g_mine, g_other, m, v)


def _block_diag(w):
    eye = jnp.eye(NH, dtype=w.dtype)
    return (w[:, :, None, :] * eye[:, None, :, None]).reshape(D, D)


def _diag_blocks(full):
    eye = jnp.eye(NH, dtype=full.dtype)
    return (full.reshape(NH, HP, NH, HP) * eye[:, None, :, None]).sum(axis=2)


def _pad_lanes(v, n=128):
    return jnp.pad(v, ((0, 0), (0, n - v.shape[1])))


def _local_step(x, target, p, dist=None):
    heads = jnp.arange(D, dtype=jnp.int32) // HP
    expand = (jnp.arange(128, dtype=jnp.int32)[:, None] == heads[None, :]).astype(F32)
    reduce_ = expand.T
    dskip_x = jnp.repeat(p["d_skip"], HP, axis=1)
    a_log = _pad_lanes(p["a_log"])
    dt_bias = _pad_lanes(p["dt_bias"])
    wa_bd = _block_diag(p["lru_wa"]).astype(MXU)
    wx_bd = _block_diag(p["lru_wx"]).astype(MXU)
    ba = p["lru_ba"].reshape(1, D)
    bx = p["lru_bx"].reshape(1, D)

    def hosted(key, fn):
        plan = dist.plan(key) if dist is not None else None
        if plan is None:
            return fn(plan=None)
        outs, got = fn(plan=plan)
        dist.done(key, got, p)
        return outs

    u = hosted("norm_u", functools.partial(_norm_cast, x, p["norm_mix_pre"], "norm_u"))
    w5, wxbc, wdt = p["w5"], p["wxbc"], p["wdt"]
    proj5 = hosted("proj5", functools.partial(_matmul, u, w5, name="proj5", tb=True, tm=1024, out_dtype=MXU))
    xbc_raw = _matmul(u, wxbc, name="proj_xbc", tb=True, tn=XBC)
    dt_raw = _matmul(u, wdt, name="proj_dt", tb=True)
    xbc_c, dsilu, xr, dt = _conv_fwd(xbc_raw, proj5, dt_raw, p["conv_ssm_w"], p["conv_ssm_b"], p["conv_lru_w"],
                                     p["conv_lru_b"], dt_bias)
    y, ya, states = _ssd_fwd(xbc_c, dt, proj5, a_log, dskip_x, p["ssm_norm"], expand)
    h, merged = hosted("lru_fwd", functools.partial(_lru_fwd, xr, proj5, ya, wa_bd, wx_bd, ba, bx, p["lru_lambda"]))
    mix, h1, v = _out_proj(merged, p["w_out"], x, p["norm_mix_post"], p["norm_mlp_pre"])
    pre = _matmul(v, p["w_up"], name="up_proj", tm=1024, out_dtype=MXU)
    dout, dff, loss, dg4 = _down_loss(pre, p["w_down"], h1, target, p["norm_mlp_post"])

    dpre = _matmul(dff, p["w_down"], name="d_pre", tb=True, tm=1024, out_dtype=MXU,
                   epi=lambda r, pr: r * (2.0 * jnp.maximum(pr.astype(F32), 0.0)), epi_args=(pre,))
    g_w_down = _matmul(pre, dff, name="dw_down", ta=True, tm=1024, tn=1024, tk=TK_GRAD, a_fn=_relu2)
    dh1, dmix, dg3, dg2 = _dv_norms(dpre, p["w_up"], h1, mix, dout, p["norm_mlp_pre"], p["norm_mix_post"])
    g_w_up = _matmul(v, dpre, name="dw_up", ta=True, tm=1024, tn=1024, tk=TK_GRAD)
    dmerged = _matmul(dmix, p["w_out"], name="d_merged", tb=True)
    g_w_out = _matmul(merged, dmix, name="dw_out", ta=True, tm=1024, tn=1024, tk=TK_GRAD)
    if dist is not None:
        dist.early_grads(w_down=g_w_down, w_up=g_w_up, w_out=g_w_out)
    (dya, dga, dgb, dg, dxr, dpr, dpi, dlam, dba, dbx) = hosted("lru_bwd", functools.partial(
        _lru_bwd, dmerged, ya, xr, h, proj5, wa_bd, wx_bd, ba, bx, p["lru_lambda"]))
    g_wa = _diag_blocks(_matmul(xr, dpr, name="dw_lru_a", ta=True, tm=1024, tn=1024, tk=TK_GRAD))
    g_wx = _diag_blocks(_matmul(xr, dpi, name="dw_lru_x", ta=True, tm=1024, tn=1024, tk=TK_GRAD))
    dz, dxbc_c, ddt, dng, ddsk, dalog = hosted("ssd_bwd", functools.partial(
        _ssd_bwd, dya, y, proj5, xbc_c, dt, states, a_log, dskip_x, p["ssm_norm"], expand, reduce_))
    (dxbc, dxl, ddtr, dcws, dcbs, dcwl, dcbl, ddtb) = hosted("conv_bwd", functools.partial(
        _conv_bwd, dxbc_c, dsilu, dxr, ddt, xbc_raw, proj5, dt_raw, p["conv_ssm_w"], p["conv_lru_w"], dt_bias))
    pieces5 = (dz, dg, dxl, dga, dgb)
    gw5 = [_matmul(pc, u, name=f"dw_in_{i}", ta=True, tm=1024, tn=1024, tk=TK_GRAD) for i, pc in enumerate(pieces5)]
    gwxbc = _matmul(dxbc, u, name="dw_in_xbc", ta=True, tm=XBC, tn=1024, tk=TK_GRAD)
    gwdt = _matmul(ddtr, u, name="dw_in_dt", ta=True, tm=128, tn=1024, tk=TK_GRAD)
    g_w_in_t = jnp.concatenate([gw5[0], gwxbc, gwdt[:NH], gw5[1], gw5[2], gw5[3], gw5[4]], axis=0)
    grads = {
        "w_in_t": g_w_in_t, "conv_ssm_w": dcws, "conv_ssm_b": dcbs, "dt_bias": ddtb[:, :NH],
        "a_log": dalog[:, :NH], "d_skip": ddsk.reshape(NH, HP).sum(axis=1)[None, :], "ssm_norm": dng,
        "conv_lru_w": dcwl, "conv_lru_b": dcbl, "lru_wa": g_wa, "lru_ba": dba.reshape(NH, HP), "lru_wx": g_wx,
        "lru_bx": dbx.reshape(NH, HP), "lru_lambda": dlam, "w_out": g_w_out, "norm_mix_post": dg2,
        "norm_mlp_pre": dg3, "w_up": g_w_up, "w_down": g_w_down, "norm_mlp_post": dg4,
    }
    if dist is not None:
        dist.late_grads(grads, loss[0, 0])
    grad_x, grads["norm_mix_pre"] = hosted("du_norm", functools.partial(
        _du_norm, pieces5, dxbc, ddtr, w5, wxbc, wdt, x, dh1, p["norm_mix_pre"]))
    return loss[0, 0], grad_x, grads


def _split_w_in_t(w_in_t):
    z, xbc, dtc, g, xl, ga, gb = jnp.split(w_in_t, [D, D + XBC, D + XBC + NH, 2 * D + XBC + NH,
                                                    3 * D + XBC + NH, 4 * D + XBC + NH], axis=0)
    return jnp.concatenate([z, g, xl, ga, gb], axis=0), xbc, jnp.pad(dtc, ((0, 128 - NH), (0, 0)))


COMM = BF16


def _place():
    x, y, c = lax.axis_index("x"), lax.axis_index("y"), lax.axis_index("c")
    chips = [(1 - x, y), (x, 1 - y), (1 - x, 1 - y)]
    return x, y, c, chips


def _remote(src, dst, send_sem, recv_sem, to):
    return pltpu.make_async_remote_copy(src_ref=src, dst_ref=dst, send_sem=send_sem, recv_sem=recv_sem, device_id=to,
                                        device_id_type=MESH)


def _gather_plan(big, small=(), axes=None):
    nb = len(big)
    arrs = list(big) + list(small)
    na = len(arrs)
    axes = list(axes or [0] * nb)

    def half(ref, a, k, which):
        h = arrs[a].shape[axes[a]] // 2
        cut = (pl.ds(which * h, h),) if axes[a] == 0 else (slice(None), pl.ds(which * h, h))
        return ref.at[cut] if k is None else ref.at[(k,) + cut]

    def direct(ins, outs, send, recv):
        x, y, c, chips = _place()
        k = 2 * x + y
        cps = []
        for a in range(na):
            src, dst = (half(ins[a], a, None, c), half(outs[a], a, k, c)) if a < nb else (ins[a], outs[a].at[k])
            cps += [_remote(src, dst, send.at[a, j], recv.at[a, j], (cx, cy, c)) for j, (cx, cy) in enumerate(chips)]
        return cps

    def passed(outs, send, recv):
        x, y, c, chips = _place()
        cps = []
        for j, (cx, cy) in enumerate(chips):
            for a in range(nb):
                got = half(outs[a], a, 2 * cx + cy, c)
                cps.append(_remote(got, got, send.at[a, 3 + j], recv.at[a, 3 + j], (x, y, 1 - c)))
        return cps

    def start(ins, outs, sems):
        for cp in direct(ins, outs, *sems[0]):
            cp.start()

    def mid(ins, outs, sems):
        send, recv = sems[0]
        _, _, c, chips = _place()
        fwd = passed(outs, send, recv)
        for j, (cx, cy) in enumerate(chips):
            kj = 2 * cx + cy
            for a in range(na):
                got = half(outs[a], a, kj, c) if a < nb else outs[a].at[kj]
                _remote(got, got, send.at[a, j], recv.at[a, j], (cx, cy, c)).wait_recv()
                if a < nb:
                    fwd[j * nb + a].start()

    def finish(ins, outs, sems):
        send, recv = sems[0]
        x, y, c, chips = _place()
        for j, (cx, cy) in enumerate(chips):
            for a in range(nb):
                got = half(outs[a], a, 2 * cx + cy, 1 - c)
                _remote(got, got, send.at[a, 3 + j], recv.at[a, 3 + j], (x, y, 1 - c)).wait_recv()
        for cp in direct(ins, outs, send, recv) + passed(outs, send, recv):
            cp.wait_send()

    return _Plan(arrs, [_sds((NCHIP,) + a.shape, a.dtype) for a in arrs], [(na, 6)], start, finish, mid)


def _own_shards(gathered, shards):
    kchip = 2 * lax.axis_index("x") + lax.axis_index("y")
    return [lax.dynamic_update_index_in_dim(o, a, kchip, 0) for o, a in zip(gathered, shards)]


def _swap_plan(ins, outs, sems, copies):
    def start(i, o, s):
        for cp in copies(i, o, *s[0]):
            cp.start()

    def finish(i, o, s):
        for cp in copies(i, o, *s[0]):
            cp.wait()

    return _Plan(ins, outs, [sems], start, finish)


def _half_shape(shape, axis):
    return tuple(d // 2 if i == axis else d for i, d in enumerate(shape))


def _pair_exchange_plan(gs, axis=1):
    def copies(ins, outs, send, recv):
        x, y, c, _ = _place()
        cps = []
        for a in range(len(gs)):
            h = ins[a].shape[axis] // 2
            theirs = pl.ds((1 - c) * h, h)
            src = ins[a].at[:, theirs] if axis == 1 else ins[a].at[:, :, theirs]
            cps.append(_remote(src, outs[a], send.at[a], recv.at[a], (x, y, 1 - c)))
        return cps

    return _swap_plan(gs, [_sds(_half_shape(g.shape, axis), g.dtype) for g in gs], (len(gs),), copies)


def _pair_add(g, got, cidx, name, axis=1):
    half = _half_shape(g.shape, axis)
    blk, nt, _, part = _half_blocks(g.shape[1:], axis - 1)

    def body(c_ref, g_ref, o_ref, p_ref, pc_ref):
        sm = g_ref[...] + o_ref[...]
        p_ref[...] = sm
        pc_ref[...] = sm.astype(pc_ref.dtype)

    def mine(k, i, c_ref):
        j = c_ref[0] * nt + i
        return (k, j, 0) if axis == 1 else (k, 0, j)

    spec = pl.BlockSpec((1,) + blk, lambda k, i, c_ref: (k,) + part(i))
    return pl.pallas_call(
        body, name=name,
        grid_spec=pltpu.PrefetchScalarGridSpec(
            num_scalar_prefetch=1, grid=(NCHIP, nt),
            in_specs=[pl.BlockSpec((1,) + blk, mine), spec], out_specs=[spec, spec]),
        out_shape=[_sds(half), _sds(half, COMM)],
        compiler_params=_cp("parallel", "parallel"),
    )(cidx, g, got)


def _chip_exchange_plan(ps):
    def copies(ins, outs, send, recv):
        _, _, c, chips = _place()
        return [_remote(ins[a].at[2 * cx + cy], outs[a].at[j], send.at[a, j], recv.at[a, j], (cx, cy, c))
                for a in range(len(ps)) for j, (cx, cy) in enumerate(chips)]

    return _swap_plan(ps, [_sds((NCHIP - 1,) + p.shape[1:], p.dtype) for p in ps], (len(ps), 3), copies)


def _shard_sum(p, got, kidx, name, axis=1):
    full = tuple(2 * d if i == axis - 1 else d for i, d in enumerate(p.shape[1:]))
    blk, nt, _, part = _half_blocks(full, axis - 1)

    def body(k_ref, p_ref, g_ref, o_ref):
        sm = p_ref[0]
        for j in range(NCHIP - 1):
            sm = sm + g_ref[j].astype(F32)
        o_ref[...] = sm

    return pl.pallas_call(
        body, name=name,
        grid_spec=pltpu.PrefetchScalarGridSpec(
            num_scalar_prefetch=1, grid=(nt,),
            in_specs=[pl.BlockSpec((1,) + blk, lambda i, k_ref: (k_ref[0],) + part(i)),
                      pl.BlockSpec((NCHIP - 1,) + blk, lambda i, k_ref: (0,) + part(i))],
            out_specs=pl.BlockSpec(blk, lambda i, k_ref: part(i))),
        out_shape=_sds(p.shape[1:]),
        compiler_params=_cp("parallel"),
    )(kidx, p, got)


def _pair_swap_plan(rs):
    def copies(ins, outs, send, recv):
        x, y, c, _ = _place()
        return [_remote(ins[a], outs[a], send.at[a], recv.at[a], (x, y, 1 - c)) for a in range(len(rs))]

    return _swap_plan(rs, [_sds(r.shape, r.dtype) for r in rs], (len(rs),), copies)


def _allgather8_plan(v):
    def pieces(ins, outs, send, recv):
        x, y, c, chips = _place()
        me, sibling = (x, y, c), (x, y, 1 - c)

        def copy(k, block, to, src=None):
            px, py, pc = block
            slot = outs[0].at[4 * px + 2 * py + pc]
            return _remote(slot if src is None else src, slot, send.at[k], recv.at[k], to)

        first = [copy(0, me, sibling, src=ins[0])] + [copy(1 + j, me, (*chip, c), src=ins[0])
                                                      for j, chip in enumerate(chips)]
        passed = [copy(4 + j, (*chip, c), sibling) for j, chip in enumerate(chips)]
        arrivals = [copy(1 + j, (*chip, c), me) for j, chip in enumerate(chips)]
        late = [copy(0, sibling, me)] + [copy(4 + j, (*chip, 1 - c), me) for j, chip in enumerate(chips)]
        return first, passed, arrivals, late

    def start(ins, outs, sems):
        for cp in pieces(ins, outs, *sems[0])[0]:
            cp.start()

    def mid(ins, outs, sems):
        _, passed, arrivals, _ = pieces(ins, outs, *sems[0])
        for got, fwd in zip(arrivals, passed):
            got.wait_recv()
            fwd.start()

    def finish(ins, outs, sems):
        first, passed, _, late = pieces(ins, outs, *sems[0])
        for got in late:
            got.wait_recv()
        for cp in first + passed:
            cp.wait_send()

    return _Plan([v], [_sds((8,) + v.shape, v.dtype)], [(7,)], start, finish, mid)


def _own_block(gathered, v):
    me = 4 * lax.axis_index("x") + 2 * lax.axis_index("y") + lax.axis_index("c")
    return lax.dynamic_update_index_in_dim(gathered, v, me, 0)


def _sum_devices(allv, name):
    _, r, _ = allv.shape

    def body(a_ref, o_ref):
        sm = a_ref[0]
        for d in range(1, 8):
            sm = sm + a_ref[d]
        o_ref[...] = sm

    return pl.pallas_call(
        body, name=name, grid=(1,), in_specs=[_full_spec((8, r, 128))], out_specs=_full_spec((r, 128)),
        out_shape=_sds((r, 128)), compiler_params=_cp("arbitrary"),
    )(allv)


def _pack(arrs):
    flat = jnp.concatenate([a.reshape(-1) for a in arrs])
    return jnp.pad(flat, (0, (-flat.shape[0]) % 1024)).reshape(-1, 128)


def _unpack(packed, shapes):
    flat, outs, off = packed.reshape(-1), [], 0
    for shp in shapes:
        n = math.prod(shp)
        outs.append(flat[off:off + n].reshape(shp))
        off += n
    return outs


BIG = ("w_in", "w_out", "w_up", "w_down")
CONV = ("conv_ssm_w", "conv_lru_w")
WEIGHTS = ("norm_mix_pre", "w_in", "conv_ssm_w", "conv_ssm_b", "dt_bias", "a_log", "d_skip", "ssm_norm", "conv_lru_w",
           "conv_lru_b", "lru_wa", "lru_ba", "lru_wx", "lru_bx", "lru_lambda", "w_out", "norm_mix_post",
           "norm_mlp_pre", "w_up", "w_down", "norm_mlp_post")
SMALL = tuple(n for n in WEIGHTS if n not in BIG and n not in CONV)
EARLY = ("w_down", "w_up", "w_out")


def _cat_cols(g):
    return jnp.concatenate([g[k] for k in range(NCHIP)], axis=1)


class _Dist:
    def __init__(self, shards, first, cidx, kidx):
        self.shards, self.first, self.cidx, self.kidx = shards, first, cidx, kidx
        self.halves = {}

    def early_grads(self, w_down, w_up, w_out):
        self.shard_major = [w_down.reshape(NCHIP, D, D), jnp.stack([w_up[:, D * k:D * (k + 1)] for k in range(NCHIP)]),
                            w_out.reshape(NCHIP, D // NCHIP, D)]

    def late_grads(self, grads, loss):
        g_in = grads["w_in_t"].reshape(NCHIP, W_IN_SHARD, D)
        got, = _run_plan(_pair_exchange_plan([g_in], axis=2), "grad_pair_exchange_w_in")
        self.p_in, self.pc_in = _pair_add(g_in, got, self.cidx, "grad_pair_add_w_in", axis=2)
        self.small_names = [n for n in SMALL + CONV if n != "norm_mix_pre"]
        self.small_shapes = [grads[n].shape for n in self.small_names] + [(1,)]
        self.packed_small = _pack([grads[n] for n in self.small_names] + [loss.reshape(1)])

    def plan(self, key):
        if key == "norm_u":
            return _gather_plan(self.first[:1], self.first[1:], axes=[1])
        if key == "proj5":
            return _gather_plan([self.shards["w_out"], self.shards["w_up"]])
        if key == "lru_fwd":
            return _gather_plan([self.shards["w_down"]])
        if key == "lru_bwd":
            return _pair_exchange_plan(self.shard_major)
        if key == "ssd_bwd":
            return _chip_exchange_plan([pc for _, pc in self.pair])
        if key == "conv_bwd":
            return _pair_swap_plan(self.mine)
        if key == "du_norm":
            return _merge_plans(_allgather8_plan(self.packed_small), _chip_exchange_plan([self.pc_in]))
        return None

    def done(self, key, got, p):
        if key == "norm_u":
            g_in, g_cs, g_cl = _own_shards(got, self.first)
            w5, wxbc, wdt = _split_w_in_t(g_in.reshape(W_IN_COLS, D))
            p.update(w5=w5, wxbc=wxbc, wdt=wdt, conv_ssm_w=_cat_cols(g_cs), conv_lru_w=_cat_cols(g_cl))
        elif key == "proj5":
            g_out, g_up = _own_shards(got, [self.shards["w_out"], self.shards["w_up"]])
            p.update(w_out=g_out.reshape(D, D), w_up=_cat_cols(g_up))
        elif key == "lru_fwd":
            g_down, = _own_shards(got, [self.shards["w_down"]])
            p.update(w_down=g_down.reshape(DFF, D))
        elif key == "lru_bwd":
            self.pair = [_pair_add(gs, o, self.cidx, f"grad_pair_add_{n}")
                         for gs, o, n in zip(self.shard_major, got, EARLY)]
        elif key == "ssd_bwd":
            self.mine = [_shard_sum(pf, o, self.kidx, f"grad_shard_sum_{n}")
                         for (pf, _), o, n in zip(self.pair, got, EARLY)]
        elif key == "conv_bwd":
            self.halves = {n: (mine, other) for n, mine, other in zip(EARLY, self.mine, got)}
        elif key == "du_norm":
            self.all_small, self.from_chips_in = got


def kernel(x, norm_mix_pre, w_in, conv_ssm_w, conv_ssm_b, dt_bias, a_log, d_skip, ssm_norm, conv_lru_w, conv_lru_b, lru_wa, lru_ba, lru_wx, lru_bx, lru_lambda, w_out, norm_mix_post, norm_mlp_pre, w_up, w_down, norm_mlp_post, loss_target, m_norm_mix_pre, m_w_in, m_conv_ssm_w, m_conv_ssm_b, m_dt_bias, m_a_log, m_d_skip, m_ssm_norm, m_conv_lru_w, m_conv_lru_b, m_lru_wa, m_lru_ba, m_lru_wx, m_lru_bx, m_lru_lambda, m_w_out, m_norm_mix_post, m_norm_mlp_pre, m_w_up, m_w_down, m_norm_mlp_post, v_norm_mix_pre, v_w_in, v_conv_ssm_w, v_conv_ssm_b, v_dt_bias, v_a_log, v_d_skip, v_ssm_norm, v_conv_lru_w, v_conv_lru_b, v_lru_wa, v_lru_ba, v_lru_wx, v_lru_bx, v_lru_lambda, v_w_out, v_norm_mix_post, v_norm_mlp_pre, v_w_up, v_w_down, v_norm_mlp_post):
    args = locals()
    w = {n: args[n][0] for n in WEIGHTS}
    m = {n: args["m_" + n][0] for n in WEIGHTS}
    v = {n: args["v_" + n][0] for n in WEIGHTS}
    cidx = lax.axis_index("c").astype(jnp.int32).reshape(1)
    kchip = 2 * lax.axis_index("x") + lax.axis_index("y")
    to_t = lambda a: jnp.transpose(a, (2, 0, 1)).reshape(W_IN_SHARD, D)
    from_t = lambda a: jnp.transpose(a.reshape(W_IN_SHARD, 1, D), (1, 2, 0))
    shards = {n: (to_t(w_in) if n == "w_in" else w[n]).astype(MXU) for n in BIG}
    dist = _Dist(shards, [shards["w_in"], w["conv_ssm_w"], w["conv_lru_w"]], cidx, kchip.astype(jnp.int32).reshape(1))
    p = {n: (w[n].reshape(1, -1) if w[n].ndim == 1 else w[n]) for n in SMALL}

    _, grad_x, g = _local_step(x[0], loss_target[0], p, dist)

    half_in = _shard_sum(dist.p_in, dist.from_chips_in, dist.kidx, "grad_shard_sum_w_in", axis=2)
    packed_g1 = _pack([g["norm_mix_pre"]])
    all_g1, other_in = _run_plan(_merge_plans(_allgather8_plan(packed_g1), _pair_swap_plan([half_in])),
                                 "grad_pair_swap_w_in")
    halves = dist.halves

    reduced = {}
    *summed, loss = _unpack(_sum_devices(_own_block(dist.all_small, dist.packed_small), "small_sum"),
                            dist.small_shapes)
    loss = loss.reshape(())
    g1, = _unpack(_sum_devices(_own_block(all_g1, packed_g1), "small_sum_norm_mix_pre"), [g["norm_mix_pre"].shape])
    for n, s in zip(dist.small_names + ["norm_mix_pre"], summed + [g1]):
        if n in CONV:
            width = w[n].shape[1]
            reduced[n] = lax.dynamic_slice_in_dim(s, kchip * width, width, axis=1)
        else:
            reduced[n] = s.reshape(w[n].shape)

    delta, new_m, new_v = {}, {}, {}
    for n in EARLY:
        mine, other = halves[n]
        reduced[n], delta[n], new_m[n], new_v[n] = _adamw_halves(w[n], mine, other, m[n], v[n], cidx, f"adamw_{n}")
    outs_t = _adamw_halves(to_t(w_in), half_in, other_in, to_t(m_w_in), to_t(v_w_in), cidx, "adamw_w_in", axis=1)
    for d, o in zip((reduced, delta, new_m, new_v), outs_t):
        d["w_in"] = from_t(o)[0]
    for n in CONV:
        delta[n], new_m[n], new_v[n] = _adamw(w[n], reduced[n], m[n], v[n], f"adamw_{n}")
    shapes = [w[n].shape for n in SMALL]
    packed = [_pack([d[n] for n in SMALL]) for d in (w, reduced, m, v)]
    for d, out in zip((delta, new_m, new_v), _adamw(*packed, "adamw_small")):
        d.update(zip(SMALL, _unpack(out, shapes)))

    lead = lambda d: [d[n][None] for n in WEIGHTS]
    return (loss, grad_x[None], *lead(reduced), *lead(delta), *lead(new_m), *lead(new_v))
```

```python
import functools
import math

import jax
import jax.numpy as jnp
from jax import lax
from jax.experimental import pallas as pl
from jax.experimental.pallas import tpu as pltpu

F32 = jnp.float32
BF16 = jnp.bfloat16
MXU = BF16

D = 1024
DFF = 4096
NH = 16
HP = 64
NG = 2
NS = 128
CH = 128
XBC = D + 2 * NG * NS
GW = D // NG
LRU_C = 8.0
EPS = 1e-6
NCHIP = 4
W_IN_COLS = 6672
W_IN_SHARD = W_IN_COLS // NCHIP

ADAM_LR = 0.001
ADAM_B1 = 0.9
ADAM_B2 = 0.999
ADAM_EPS = 1e-08
ADAM_WD = 0.01
ADAM_STEP = 10

VMEM_LIMIT = 56 * 1024 * 1024
TK_GRAD = 2048
MID_AT = 0.7
ROWS_FUSED = 512
MESH = pl.DeviceIdType.MESH


def _cp(*sem):
    return pltpu.CompilerParams(dimension_semantics=sem, vmem_limit_bytes=VMEM_LIMIT)


def _dot(a, b, ca=1, cb=0, prec=None):
    return lax.dot_general(a, b, (((ca,), (cb,)), ((), ())), precision=prec, preferred_element_type=F32)


def _mdot(a, b, ca=1, cb=0):
    return _dot(a.astype(MXU), b.astype(MXU), ca, cb)


def _bf16_parts(v, n):
    parts = []
    for i in range(n):
        p = v.astype(BF16)
        parts.append(p)
        if i < n - 1:
            v = v - p.astype(F32)
    return parts


def _xdot(a, b, passes, split_b=False):
    if split_b:
        a16 = a.astype(BF16)
        terms = [_dot(a16, p) for p in _bf16_parts(b, passes)]
    else:
        b16 = b.astype(BF16)
        terms = [_dot(p, b16) for p in _bf16_parts(a, passes)]
    return functools.reduce(lambda u, v: u + v, terms)


def _sig(x):
    return 0.5 * jnp.tanh(0.5 * x) + 0.5


def _silu(x):
    return x * _sig(x)


def _dsilu(x):
    s = _sig(x)
    return s * (1.0 + x * (1.0 - s))


def _softplus(x):
    e = jnp.exp(-jnp.abs(x))
    return jnp.maximum(x, 0.0) + jnp.where(e < 1e-4, e * (1.0 - 0.5 * e), jnp.log(1.0 + e))


_GELU_C = math.sqrt(2.0 / math.pi)


def _gelu(x):
    t = jnp.tanh(_GELU_C * (x + 0.044715 * x * x * x))
    return 0.5 * x * (1.0 + t)


def _gelu_and_grad(x):
    x2 = x * x
    t = jnp.tanh(_GELU_C * (x + 0.044715 * x * x2))
    half = 0.5 * (1.0 + t)
    return x * half, half + 0.5 * x * (1.0 - t * t) * _GELU_C * (1.0 + 3.0 * 0.044715 * x2)


def _one_minus_sq(a, la):
    x = 2.0 * la
    series = -x * (1.0 + x * (0.5 + x * (1.0 / 6.0)))
    return jnp.where(x > -0.01, series, 1.0 - a * a)


def _rms(x):
    return lax.rsqrt(jnp.mean(x * x, axis=-1, keepdims=True) + EPS)


def _rms_bwd(x, r, g, dy):
    xn = x * r
    dxh = dy * g
    m = jnp.mean(dxh * xn, axis=-1, keepdims=True)
    return r * (dxh - xn * m), jnp.sum(dy * xn, axis=0, keepdims=True)


def _row_spec(t, c, col=0):
    return pl.BlockSpec((t, c), lambda i: (i, col))


def _rev_spec(t, c, n, col=0):
    return pl.BlockSpec((t, c), lambda i: (n - 1 - i, col))


def _full_spec(shape, once=False):
    nd = len(shape)
    if once:
        return pl.BlockSpec(shape, lambda *_: (0,) * nd, pipeline_mode=pl.Buffered(1))
    return pl.BlockSpec(shape, lambda *_: (0,) * nd)


def _sds(shape, dtype=F32):
    return jax.ShapeDtypeStruct(shape, dtype)


ANY = pl.BlockSpec(memory_space=pl.ANY)


class _Plan:
    def __init__(self, ins, outs, sems, start, finish, mid=None):
        self.ins, self.outs, self.sems = list(ins), list(outs), list(sems)
        self.start, self.finish, self.mid = start, finish, mid or (lambda i, o, s: None)


def _merge_plans(*plans):
    def each(fn_name, ins, outs, sems):
        i = o = s = 0
        for p in plans:
            getattr(p, fn_name)(ins[i:i + len(p.ins)], outs[o:o + len(p.outs)], sems[s:s + len(p.sems)])
            i, o, s = i + len(p.ins), o + len(p.outs), s + len(p.sems)

    return _Plan([a for p in plans for a in p.ins], [a for p in plans for a in p.outs],
                 [a for p in plans for a in p.sems], functools.partial(each, "start"),
                 functools.partial(each, "finish"), functools.partial(each, "mid"))


def _pcall(body, args, *, name, grid, in_specs, out_specs, out_shape, sem, scratch_shapes=(), plan=None):
    single = not isinstance(out_shape, (list, tuple))
    out_specs = [out_specs] if single else list(out_specs)
    out_shape = [out_shape] if single else list(out_shape)
    if plan is None:
        outs = pl.pallas_call(body, name=name, grid=grid, in_specs=list(in_specs), out_specs=out_specs,
                              out_shape=out_shape, scratch_shapes=list(scratch_shapes),
                              compiler_params=_cp(*sem))(*args)
        return outs[0] if single else outs
    n_in, n_out, n_sc, ni, no = len(in_specs), len(out_shape), len(scratch_shapes), len(plan.ins), len(plan.outs)

    def hosted(*refs):
        b0 = n_in + ni
        b1 = b0 + n_out + no
        sem_refs = refs[b1 + n_sc:]
        sems = [(sem_refs[2 * q], sem_refs[2 * q + 1]) for q in range(len(plan.sems))]
        step = functools.reduce(lambda lin, ig: lin * ig[1] + ig[0],
                                [(pl.program_id(d), g) for d, g in enumerate(grid)], 0)
        total = math.prod(grid)

        @pl.when(step == 0)
        def _():
            plan.start(refs[n_in:b0], refs[b0 + n_out:b1], sems)

        body(*refs[:n_in], *refs[b0:b0 + n_out], *refs[b1:b1 + n_sc])

        @pl.when(step == min(int(MID_AT * total), total - 1))
        def _():
            plan.mid(refs[n_in:b0], refs[b0 + n_out:b1], sems)

        @pl.when(step == total - 1)
        def _():
            plan.finish(refs[n_in:b0], refs[b0 + n_out:b1], sems)

    dma = [pltpu.SemaphoreType.DMA(shape) for shape in plan.sems for _ in range(2)]
    outs = pl.pallas_call(hosted, name=name, grid=grid, in_specs=list(in_specs) + [ANY] * ni,
                          out_specs=out_specs + [ANY] * no, out_shape=out_shape + plan.outs,
                          scratch_shapes=list(scratch_shapes) + dma,
                          compiler_params=_cp(*("arbitrary",) * len(grid)))(*args, *plan.ins)
    return (outs[0] if single else outs[:n_out]), outs[n_out:]


def _run_plan(plan, name):
    ni, no = len(plan.ins), len(plan.outs)

    def body(*refs):
        sem_refs = refs[ni + no:]
        sems = [(sem_refs[2 * q], sem_refs[2 * q + 1]) for q in range(len(plan.sems))]
        plan.start(refs[:ni], refs[ni:ni + no], sems)
        plan.mid(refs[:ni], refs[ni:ni + no], sems)
        plan.finish(refs[:ni], refs[ni:ni + no], sems)

    return pl.pallas_call(
        body, name=name, in_specs=[ANY] * ni, out_specs=[ANY] * no, out_shape=plan.outs,
        scratch_shapes=[pltpu.SemaphoreType.DMA(shape) for shape in plan.sems for _ in range(2)],
    )(*plan.ins)


def _matmul(a, b, *, name, ta=False, tb=False, tm=512, tn=1024, tk=1024, out_dtype=F32, a_fn=None, epi=None,
            epi_args=(), plan=None):
    m, k = (a.shape[1], a.shape[0]) if ta else a.shape
    n = b.shape[0] if tb else b.shape[1]
    tm, tn, tk = min(tm, m), min(tn, n), min(tk, k)
    nk = k // tk
    a_spec = pl.BlockSpec((tk, tm), lambda i, j, kk: (kk, i)) if ta else pl.BlockSpec((tm, tk), lambda i, j, kk: (i, kk))
    b_spec = pl.BlockSpec((tn, tk), lambda i, j, kk: (j, kk)) if tb else pl.BlockSpec((tk, tn), lambda i, j, kk: (kk, j))
    e_specs = [pl.BlockSpec((tm, tn), lambda i, j, kk: (i, j)) for _ in epi_args]
    ne = len(epi_args)

    def body(a_ref, b_ref, *rest):
        e_refs, o_ref = rest[:ne], rest[ne]
        av = a_ref[...]
        if a_fn is not None:
            av = a_fn(av)
        part = _mdot(av, b_ref[...], 0 if ta else 1, 1 if tb else 0)

        def finish(r):
            if epi is not None:
                r = epi(r, *[e[...] for e in e_refs])
            o_ref[...] = r.astype(o_ref.dtype)

        if nk == 1:
            finish(part)
            return
        acc_ref = rest[ne + 1]
        kk = pl.program_id(2)

        @pl.when(kk == 0)
        def _():
            acc_ref[...] = part

        @pl.when(jnp.logical_and(kk > 0, kk < nk - 1))
        def _():
            acc_ref[...] += part

        @pl.when(kk == nk - 1)
        def _():
            finish(acc_ref[...] + part)

    return _pcall(
        body, (a, b, *epi_args), name=name, grid=(m // tm, n // tn, nk),
        in_specs=[a_spec, b_spec] + e_specs,
        out_specs=pl.BlockSpec((tm, tn), lambda i, j, kk: (i, j)),
        out_shape=_sds((m, n), out_dtype),
        scratch_shapes=[pltpu.VMEM((tm, tn), F32)] if nk > 1 else [],
        sem=("parallel", "parallel", "arbitrary"), plan=plan)


def _relu2(p):
    p = jnp.maximum(p, jnp.zeros((), p.dtype))
    return p * p


def _norm_cast(x, g, name, plan=None):
    s = x.shape[0]
    t = min(512, s)

    def body(x_ref, g_ref, o_ref):
        xv = x_ref[...]
        o_ref[...] = (xv * _rms(xv) * g_ref[...]).astype(o_ref.dtype)

    return _pcall(body, (x, g), name=name, grid=(s // t,), in_specs=[_row_spec(t, D), _full_spec((1, D))],
                  out_specs=_row_spec(t, D), out_shape=_sds((s, D), MXU), sem=("parallel",), plan=plan)


def _conv_fwd(xbc_raw, proj5, dt_raw, cw_s, cb_s, cw_l, cb_l, dt_bias):
    s = xbc_raw.shape[0]
    t = min(256, s)

    def body(xs_ref, xl_ref, dtr_ref, cws_ref, cbs_ref, cwl_ref, cbl_ref, dtb_ref, xc_ref, dsl_ref, xr_ref, dt_ref,
             bs_ref, bl_ref):
        @pl.when(pl.program_id(0) == 0)
        def _():
            bs_ref[0:8, :] = jnp.zeros((8, XBC), F32)
            bl_ref[0:8, :] = jnp.zeros((8, D), F32)

        bs_ref[8:t + 8, :] = xs_ref[...]
        bl_ref[8:t + 8, :] = xl_ref[...].astype(F32)

        def conv(buf, w_ref, b_ref):
            acc = b_ref[...] + w_ref[3:4, :] * buf[8:t + 8, :]
            for k in (1, 2, 3):
                acc = acc + w_ref[3 - k:4 - k, :] * buf[8 - k:t + 8 - k, :]
            return acc

        pre = conv(bs_ref, cws_ref, cbs_ref)
        sg = _sig(pre)
        xc_ref[...] = pre * sg
        dsl_ref[...] = (sg * (1.0 + pre * (1.0 - sg))).astype(dsl_ref.dtype)
        xr_ref[...] = conv(bl_ref, cwl_ref, cbl_ref)
        dt_ref[...] = _softplus(dtr_ref[...] + dtb_ref[...])
        bs_ref[0:8, :] = bs_ref[t:t + 8, :]
        bl_ref[0:8, :] = bl_ref[t:t + 8, :]

    return pl.pallas_call(
        body, name="conv_fwd", grid=(s // t,),
        in_specs=[_row_spec(t, XBC), _row_spec(t, D, 2), _row_spec(t, 128), _full_spec((4, XBC)),
                  _full_spec((1, XBC)), _full_spec((4, D)), _full_spec((1, D)), _full_spec((1, 128))],
        out_specs=[_row_spec(t, XBC), _row_spec(t, XBC), _row_spec(t, D), _row_spec(t, 128)],
        out_shape=[_sds((s, XBC)), _sds((s, XBC), BF16), _sds((s, D)), _sds((s, 128))],
        scratch_shapes=[pltpu.VMEM((t + 8, XBC), F32), pltpu.VMEM((t + 8, D), F32)],
        compiler_params=_cp("arbitrary"),
    )(xbc_raw, proj5, dt_raw, cw_s, cb_s, cw_l, cb_l, dt_bias)


def _ssd_chunk_setup(dt_ref, alog_ref, e_ref, at_ref, dtt_ref):
    lane = lax.broadcasted_iota(jnp.int32, (CH, 128), 1)
    row = lax.broadcasted_iota(jnp.int32, (CH, 128), 0)
    lane1 = lax.broadcasted_iota(jnp.int32, (1, 128), 1)
    a = jnp.where(lane1 < NH, -jnp.exp(alog_ref[...]), 0.0)
    dtv = dt_ref[...]
    adt = dtv * a
    tril = row >= lane
    acum = _xdot(tril.astype(F32), adt, 3, split_b=True)
    alast = jnp.sum(adt, axis=0, keepdims=True)
    at_ref[...] = acum.T
    dtt_ref[...] = dtv.T
    e = e_ref[...]
    ea_x = _xdot(jnp.exp(acum), e, 2)
    ws = jnp.exp(alast - acum) * dtv
    ws_x = _xdot(ws, e, 2)
    eal = jnp.exp(alast)
    eal_x = jnp.max(_xdot(jnp.broadcast_to(eal, (8, 128)), e, 3), axis=0, keepdims=True)
    return dict(lane=lane, row=row, tril=tril, a=a, dtv=dtv, acum=acum, alast=alast, ea_x=ea_x, ws=ws, ws_x=ws_x,
                eal=eal, eal_x=eal_x)


def _head_decay(cs, at_ref, dtt_ref, h):
    col = jnp.sum(jnp.where(cs["lane"] == h, cs["acum"], 0.0), axis=1, keepdims=True)
    ld = jnp.where(cs["tril"], jnp.exp(jnp.minimum(col - at_ref[h:h + 1, :], 0.0)), 0.0)
    return ld, dtt_ref[h:h + 1, :]


def _ssd_fwd(xbc_c, dt, proj5, a_log, dskip_x, ssm_norm, expand):
    s = xbc_c.shape[0]
    nc = s // CH

    def body(xc_ref, dt_ref, z_ref, alog_ref, dsk_ref, ng_ref, e_ref, y_ref, ya_ref, st_ref, h_ref, at_ref, dtt_ref,
             yd_ref):
        @pl.when(pl.program_id(0) == 0)
        def _():
            h_ref[...] = jnp.zeros_like(h_ref)

        cs = _ssd_chunk_setup(dt_ref, alog_ref, e_ref, at_ref, dtt_ref)
        lane = cs["lane"]
        for g in range(NG):
            gs = slice(GW * g, GW * (g + 1))
            bg = xc_ref[:, D + NS * g:D + NS * (g + 1)]
            cg = xc_ref[:, D + NG * NS + NS * g:D + NG * NS + NS * (g + 1)]
            cb = _mdot(cg, bg, 1, 1)
            for j in range(4 * g, 4 * g + 4):
                ps = slice(128 * j, 128 * (j + 1))
                xp = xc_ref[:, ps]
                acc = jnp.zeros((CH, 128), F32)
                for hf in range(2):
                    ld, rowdt = _head_decay(cs, at_ref, dtt_ref, 2 * j + hf)
                    hm = (lane >= HP) if hf else (lane < HP)
                    acc = acc + _mdot(cb * ld * rowdt, jnp.where(hm, xp, 0.0))
                yd_ref[:, ps] = acc
            hg = h_ref[:, gs]
            yd_ref[:, gs] += _mdot(cg, hg) * cs["ea_x"][:, gs]
            st = _mdot(bg, xc_ref[:, gs] * cs["ws_x"][:, gs], 0, 0)
            st_ref[0, :, gs] = hg
            h_ref[:, gs] = cs["eal_x"][:, gs] * hg + st
        y = yd_ref[...] + dsk_ref[...] * xc_ref[:, 0:D]
        y_ref[...] = y
        yg = y * _silu(z_ref[...].astype(F32))
        for g in range(NG):
            gs = slice(GW * g, GW * (g + 1))
            seg = yg[:, gs]
            ya_ref[:, gs] = seg * _rms(seg) * ng_ref[:, gs]

    return pl.pallas_call(
        body, name="ssd_fwd", grid=(nc,),
        in_specs=[_row_spec(CH, XBC), _row_spec(CH, 128), _row_spec(CH, D, 0), _full_spec((1, 128)),
                  _full_spec((1, D)), _full_spec((1, D)), _full_spec((128, D))],
        out_specs=[_row_spec(CH, D), _row_spec(CH, D), pl.BlockSpec((1, NS, D), lambda i: (i, 0, 0))],
        out_shape=[_sds((s, D)), _sds((s, D)), _sds((nc, NS, D))],
        scratch_shapes=[pltpu.VMEM((NS, D), F32), pltpu.VMEM((128, 128), F32), pltpu.VMEM((128, 128), F32),
                        pltpu.VMEM((CH, D), F32)],
        compiler_params=_cp("arbitrary"),
    )(xbc_c, dt, proj5, a_log, dskip_x, ssm_norm, expand)


def _lru_gates(xr, wa_ref, wx_ref, ba_ref, bx_ref, lam_ref):
    gr = _sig(_mdot(xr, wa_ref[...]) + ba_ref[...])
    gi = _sig(_mdot(xr, wx_ref[...]) + bx_ref[...])
    sp = _softplus(-lam_ref[...])
    la = -LRU_C * gr * sp
    a = jnp.exp(la)
    oms = _one_minus_sq(a, la)
    inv_mult = lax.rsqrt(oms)
    return gr, gi, sp, a, oms * inv_mult, inv_mult


def _blocked_scan(a, u, carry_ref, a_ref, u_ref, c_ref, out_ref, reverse):
    t = a.shape[0]
    ns = t // 8

    def combine(av, uv, idx, n, sh):
        m = (idx < n - sh) if reverse else (idx >= sh)
        by = n - sh if reverse else sh
        return jnp.where(m, av * pltpu.roll(av, by, 0), av), jnp.where(m, uv + av * pltpu.roll(uv, by, 0), uv)

    row = lax.broadcasted_iota(jnp.int32, (t, D), 0)
    rin = jnp.bitwise_and(row, 7)
    for sh in (1, 2, 4):
        m = (rin < 8 - sh) if reverse else (rin >= sh)
        by = t - sh if reverse else sh
        a, u = jnp.where(m, a * pltpu.roll(a, by, 0), a), jnp.where(m, u + a * pltpu.roll(u, by, 0), u)
    a_ref[...] = a
    u_ref[...] = u
    edge = 0 if reverse else 7
    for j in range(ns):
        c_ref[j:j + 1, :] = a_ref[8 * j + edge:8 * j + edge + 1, :]
    at = c_ref[...]
    for j in range(ns):
        c_ref[j:j + 1, :] = u_ref[8 * j + edge:8 * j + edge + 1, :]
    ut = c_ref[...]
    srow = lax.broadcasted_iota(jnp.int32, (ns, D), 0)
    sh = 1
    while sh < ns:
        at, ut = combine(at, ut, srow, ns, sh)
        sh *= 2
    cv = carry_ref[0:1, :]
    ends = ut + at * cv
    last = 0 if reverse else ns - 1
    first = ns - 1 if reverse else 0
    c_ref[...] = jnp.where(srow == first, cv, pltpu.roll(ends, first if reverse else 1, 0))
    carry_ref[0:1, :] = jnp.sum(jnp.where(srow == last, ends, 0.0), axis=0, keepdims=True)
    for j in range(ns):
        sl = slice(8 * j, 8 * j + 8)
        out_ref[sl, :] = u_ref[sl, :] + a_ref[sl, :] * c_ref[j:j + 1, :]


def _lru_fwd(xr, proj5, ya, wa_bd, wx_bd, ba, bx, lam, plan=None):
    s = xr.shape[0]
    t = min(256, s)

    def body(xr_ref, g_ref, ga_ref, gb_ref, ya_ref, wa_ref, wx_ref, ba_ref, bx_ref, lam_ref, h_ref, mg_ref, gr_ref,
             gi_ref, ao_ref, mo_ref, hc_ref, a_ref, u_ref, c_ref):
        @pl.when(pl.program_id(0) == 0)
        def _():
            hc_ref[...] = jnp.zeros_like(hc_ref)

        xrv = xr_ref[...]
        gr, gi, _, a, mult, _ = _lru_gates(xrv, wa_ref, wx_ref, ba_ref, bx_ref, lam_ref)
        gr_ref[...], gi_ref[...], ao_ref[...], mo_ref[...] = gr, gi, a, mult
        _blocked_scan(a, mult * gi * xrv, hc_ref, a_ref, u_ref, c_ref, h_ref, reverse=False)
        yb = h_ref[...] * _gelu(g_ref[...].astype(F32))
        mg_ref[...] = (_sig(ga_ref[...].astype(F32)) * ya_ref[...]
                       + _sig(gb_ref[...].astype(F32)) * yb).astype(mg_ref.dtype)

    return _pcall(
        body, (xr, proj5, proj5, proj5, ya, wa_bd, wx_bd, ba, bx, lam), name="lru_fwd", grid=(s // t,),
        in_specs=[_row_spec(t, D), _row_spec(t, D, 1), _row_spec(t, D, 3), _row_spec(t, D, 4), _row_spec(t, D),
                  _full_spec((D, D), once=True), _full_spec((D, D), once=True), _full_spec((1, D)),
                  _full_spec((1, D)), _full_spec((1, D))],
        out_specs=[_row_spec(t, D)] * 6,
        out_shape=[_sds((s, D)), _sds((s, D), MXU)] + [_sds((s, D))] * 4,
        scratch_shapes=[pltpu.VMEM((8, D), F32), pltpu.VMEM((t, D), F32), pltpu.VMEM((t, D), F32),
                        pltpu.VMEM((t // 8, D), F32)],
        sem=("arbitrary",), plan=plan)


def _out_proj(merged, w_out, x, g2, g3):
    s = x.shape[0]
    t = min(ROWS_FUSED, s)

    def body(mg_ref, w_ref, x_ref, g2_ref, g3_ref, mix_ref, h1_ref, v_ref):
        mix = _mdot(mg_ref[...], w_ref[...])
        mix_ref[...] = mix
        h1 = x_ref[...] + mix * _rms(mix) * g2_ref[...]
        h1_ref[...] = h1
        v_ref[...] = (h1 * _rms(h1) * g3_ref[...]).astype(v_ref.dtype)

    return pl.pallas_call(
        body, name="out_proj", grid=(s // t,),
        in_specs=[_row_spec(t, D), _full_spec((D, D), once=True), _row_spec(t, D), _full_spec((1, D)),
                  _full_spec((1, D))],
        out_specs=[_row_spec(t, D), _row_spec(t, D), _row_spec(t, D)],
        out_shape=[_sds((s, D)), _sds((s, D)), _sds((s, D), MXU)],
        compiler_params=_cp("parallel"),
    )(merged, w_out, x, g2, g3)


def _down_loss(pre, w_down, h1, target, g4):
    s = pre.shape[0]
    t = min(ROWS_FUSED, s)

    def body(pre_ref, w_ref, h1_ref, tg_ref, g4_ref, dout_ref, dff_ref, loss_ref, dg4_ref):
        @pl.when(pl.program_id(0) == 0)
        def _():
            loss_ref[...] = jnp.zeros_like(loss_ref)
            dg4_ref[...] = jnp.zeros_like(dg4_ref)

        ff = _mdot(_relu2(pre_ref[...]), w_ref[...])
        r4 = _rms(ff)
        g4v = g4_ref[...]
        diff = h1_ref[...] + ff * r4 * g4v - tg_ref[...]
        sq = jnp.sum(jnp.sum(diff * diff, axis=1, keepdims=True), axis=0, keepdims=True)
        loss_ref[...] += (0.5 / D) * sq
        dout = diff * (1.0 / D)
        dout_ref[...] = dout
        dff, dg = _rms_bwd(ff, r4, g4v, dout)
        dff_ref[...] = dff.astype(dff_ref.dtype)
        dg4_ref[...] += dg

    return pl.pallas_call(
        body, name="down_loss", grid=(s // t,),
        in_specs=[_row_spec(t, DFF), _full_spec((DFF, D), once=True), _row_spec(t, D), _row_spec(t, D),
                  _full_spec((1, D))],
        out_specs=[_row_spec(t, D), _row_spec(t, D), _full_spec((1, 128)), _full_spec((1, D))],
        out_shape=[_sds((s, D)), _sds((s, D), MXU), _sds((1, 128)), _sds((1, D))],
        compiler_params=_cp("arbitrary"),
    )(pre, w_down, h1, target, g4)


def _dv_norms(dpre, w_up, h1, mix, dout, g3, g2):
    s = h1.shape[0]
    t = min(ROWS_FUSED, s)

    def body(dp_ref, w_ref, h1_ref, mix_ref, dout_ref, g3_ref, g2_ref, dh1_ref, dmix_ref, dg3_ref, dg2_ref):
        @pl.when(pl.program_id(0) == 0)
        def _():
            dg3_ref[...] = jnp.zeros_like(dg3_ref)
            dg2_ref[...] = jnp.zeros_like(dg2_ref)

        dv = _mdot(dp_ref[...], w_ref[...], 1, 1)
        h1 = h1_ref[...]
        dh1n, dg3 = _rms_bwd(h1, _rms(h1), g3_ref[...], dv)
        dh1 = dout_ref[...] + dh1n
        dh1_ref[...] = dh1
        mix = mix_ref[...]
        dmix, dg2 = _rms_bwd(mix, _rms(mix), g2_ref[...], dh1)
        dmix_ref[...] = dmix.astype(dmix_ref.dtype)
        dg3_ref[...] += dg3
        dg2_ref[...] += dg2

    return pl.pallas_call(
        body, name="dv_norms", grid=(s // t,),
        in_specs=[_row_spec(t, DFF), _full_spec((D, DFF), once=True), _row_spec(t, D), _row_spec(t, D),
                  _row_spec(t, D), _full_spec((1, D)), _full_spec((1, D))],
        out_specs=[_row_spec(t, D), _row_spec(t, D), _full_spec((1, D)), _full_spec((1, D))],
        out_shape=[_sds((s, D)), _sds((s, D), MXU), _sds((1, D)), _sds((1, D))],
        compiler_params=_cp("arbitrary"),
    )(dpre, w_up, h1, mix, dout, g3, g2)


def _lru_bwd(dmerged, ya, xr, h, proj5, gates, wa_bd, wx_bd, lam, plan=None):
    s = xr.shape[0]
    t = min(128, s)
    n = s // t
    rs = functools.partial(_rev_spec, t, D, n)

    def body(dm_ref, ya_ref, xr_ref, h_ref, hp_ref, g_ref, ga_ref, gb_ref, gr_ref, gi_ref, a_ref, m_ref, wa_ref,
             wx_ref, lam_ref, dya_ref, dga_ref, dgb_ref, dg_ref, dxr_ref, dpr_ref, dpi_ref, dlam_ref, dba_ref,
             dbx_ref, gc_ref, af_ref, an_ref, us_ref, c_ref, gs_ref):
        i = pl.program_id(0)

        @pl.when(i == 0)
        def _():
            gc_ref[...] = jnp.zeros_like(gc_ref)
            af_ref[...] = jnp.zeros_like(af_ref)
            dlam_ref[...] = jnp.zeros_like(dlam_ref)
            dba_ref[...] = jnp.zeros_like(dba_ref)
            dbx_ref[...] = jnp.zeros_like(dbx_ref)

        xrv = xr_ref[...]
        gr, gi, a, mult = gr_ref[...], gi_ref[...], a_ref[...], m_ref[...]
        sp = _softplus(-lam_ref[...])
        inv_mult = 1.0 / mult
        hv = h_ref[...]
        dm = dm_ref[...]
        sa = _sig(ga_ref[...].astype(F32))
        sb = _sig(gb_ref[...].astype(F32))
        gel, dgel = _gelu_and_grad(g_ref[...].astype(F32))
        dya = dm * sa
        dya_ref[...] = dya
        dga_ref[...] = (dya * ya_ref[...] * (1.0 - sa)).astype(dga_ref.dtype)
        dyb = dm * sb
        dybh = dyb * hv
        dgb_ref[...] = (dybh * gel * (1.0 - sb)).astype(dgb_ref.dtype)
        dg_ref[...] = (dybh * dgel).astype(dg_ref.dtype)
        row = lax.broadcasted_iota(jnp.int32, (t, D), 0)
        an = jnp.where(row == t - 1, af_ref[0:1, :], pltpu.roll(a, t - 1, 0))
        _blocked_scan(an, dyb * gel, gc_ref, an_ref, us_ref, c_ref, gs_ref, reverse=True)
        gfull = gs_ref[...]
        af_ref[0:1, :] = jnp.sum(jnp.where(row == 0, a, 0.0), axis=0, keepdims=True)
        hlast = jnp.where(i == n - 1, 0.0, hp_ref[7:8, :])
        hprev = jnp.where(row == 0, hlast, pltpu.roll(hv, 1, 0))
        gx = gfull * xrv
        dgi = gx * mult
        dla = a * (gfull * hprev - gx * gi * a * inv_mult)
        dgr = dla * (-LRU_C * sp)
        dsp = jnp.sum(dla * (-LRU_C * gr), axis=0, keepdims=True)
        dlam_ref[...] += dsp * (-_sig(-lam_ref[...]))
        dpr = dgr * gr * (1.0 - gr)
        dpi = dgi * gi * (1.0 - gi)
        dpr_ref[...] = dpr.astype(dpr_ref.dtype)
        dpi_ref[...] = dpi.astype(dpi_ref.dtype)
        dba_ref[...] += jnp.sum(dpr, axis=0, keepdims=True)
        dbx_ref[...] += jnp.sum(dpi, axis=0, keepdims=True)
        dxr_ref[...] = gfull * mult * gi + _mdot(dpr, wa_ref[...], 1, 1) + _mdot(dpi, wx_ref[...], 1, 1)

    hp_spec = pl.BlockSpec((8, D), lambda i: (jnp.maximum((n - 1 - i) * (t // 8) - 1, 0), 0))
    return _pcall(
        body, (dmerged, ya, xr, h, h, proj5, proj5, proj5, *gates, wa_bd, wx_bd, lam), name="lru_bwd", grid=(n,),
        in_specs=[rs(), rs(), rs(), rs(), hp_spec, rs(1), rs(3), rs(4), rs(), rs(), rs(), rs(),
                  _full_spec((D, D), once=True), _full_spec((D, D), once=True), _full_spec((1, D))],
        out_specs=[rs(), rs(), rs(), rs(), rs(), rs(), rs(), _full_spec((1, D)), _full_spec((1, D)),
                   _full_spec((1, D))],
        out_shape=[_sds((s, D)), _sds((s, D), MXU), _sds((s, D), MXU), _sds((s, D), MXU), _sds((s, D)),
                   _sds((s, D), MXU), _sds((s, D), MXU), _sds((1, D)), _sds((1, D)), _sds((1, D))],
        scratch_shapes=[pltpu.VMEM((8, D), F32), pltpu.VMEM((8, D), F32), pltpu.VMEM((t, D), F32),
                        pltpu.VMEM((t, D), F32), pltpu.VMEM((t // 8, D), F32), pltpu.VMEM((t, D), F32)],
        sem=("arbitrary",), plan=plan)


def _ssd_bwd(dya, y, proj5, xbc_c, dt, states, a_log, dskip_x, ssm_norm, expand, reduce_, plan=None):
    s = xbc_c.shape[0]
    nc = s // CH
    rv = functools.partial(_rev_spec, CH)

    def body(dya_ref, y_ref, z_ref, xc_ref, dt_ref, st_ref, alog_ref, dsk_ref, ng_ref, e_ref, et_ref, dz_ref,
             dxc_ref, ddt_ref, dng_ref, ddsk_ref, dalog_ref, dh_ref, at_ref, dtt_ref, dat_ref, ddtt_ref, dy_ref,
             yoffdy_ref, xbds_ref):
        @pl.when(pl.program_id(0) == 0)
        def _():
            dh_ref[...] = jnp.zeros_like(dh_ref)
            dng_ref[...] = jnp.zeros_like(dng_ref)
            ddsk_ref[...] = jnp.zeros_like(ddsk_ref)
            dalog_ref[...] = jnp.zeros_like(dalog_ref)

        cs = _ssd_chunk_setup(dt_ref, alog_ref, e_ref, at_ref, dtt_ref)
        lane, row = cs["lane"], cs["row"]
        et = et_ref[...]
        for g in range(NG):
            gs = slice(GW * g, GW * (g + 1))
            yv = y_ref[:, gs]
            zv = z_ref[:, gs].astype(F32)
            sz = _silu(zv)
            yg = yv * sz
            dyav = dya_ref[:, gs]
            dyg, dng = _rms_bwd(yg, _rms(yg), ng_ref[:, gs], dyav)
            dng_ref[:, gs] += dng
            dy_ref[:, gs] = dyg * sz
            dz_ref[:, gs] = (dyg * yv * _dsilu(zv)).astype(dz_ref.dtype)
        dyv = dy_ref[...]
        xs = xc_ref[:, 0:D]
        ddsk_ref[...] += jnp.sum(dyv * xs, axis=0, keepdims=True)
        dxc_ref[:, 0:D] = dyv * dsk_ref[...]
        dat_ref[...] = jnp.zeros_like(dat_ref)
        ddtt_ref[...] = jnp.zeros_like(ddtt_ref)
        hh = jnp.sum(dh_ref[...] * st_ref[0], axis=0, keepdims=True)
        deal = jnp.max(_xdot(jnp.broadcast_to(hh, (8, D)), et, 3), axis=0, keepdims=True)
        d_acum = jnp.zeros((CH, 128), F32)
        for g in range(NG):
            gs = slice(GW * g, GW * (g + 1))
            bs_ = slice(D + NS * g, D + NS * (g + 1))
            cs_ = slice(D + NG * NS + NS * g, D + NG * NS + NS * (g + 1))
            bg = xc_ref[:, bs_]
            cg = xc_ref[:, cs_]
            cb = _mdot(cg, bg, 1, 1)
            hg = st_ref[0, :, gs]
            dhg = dh_ref[:, gs]
            dyg_ = dy_ref[:, gs]
            xsg = xc_ref[:, gs]
            ea = cs["ea_x"][:, gs]
            wsx = cs["ws_x"][:, gs]
            dp = dyg_ * ea
            yoffdy_ref[:, gs] = dp * _mdot(cg, hg)
            dc = _mdot(dp, hg, 1, 1)
            dhprev = _mdot(cg, dp, 0, 0)
            bds = _mdot(bg, dhg)
            dxc_ref[:, gs] += wsx * bds
            xbds_ref[:, gs] = xsg * bds
            db = _mdot(xsg * wsx, dhg, 1, 1)
            dh_ref[:, gs] = dhprev + cs["eal_x"][:, gs] * dhg
            dcbs = jnp.zeros((CH, CH), F32)
            for j in range(4 * g, 4 * g + 4):
                ps = slice(128 * j, 128 * (j + 1))
                xp = xc_ref[:, ps]
                dyp = dy_ref[:, ps]
                dxacc = jnp.zeros((CH, 128), F32)
                for hf in range(2):
                    hd = 2 * j + hf
                    ld, rowdt = _head_decay(cs, at_ref, dtt_ref, hd)
                    hm = (lane >= HP) if hf else (lane < HP)
                    dym = jnp.where(hm, dyp, 0.0)
                    w = cb * ld * rowdt
                    dw = _mdot(dym, jnp.where(hm, xp, 0.0), 1, 1)
                    dxacc = dxacc + _mdot(w, dym, 0, 0)
                    nm = dw * w
                    ddtt_ref[hd:hd + 1, :] += jnp.sum(dw * cb * ld, axis=0, keepdims=True)
                    d_acum = d_acum + jnp.where(lane == hd, jnp.sum(nm, axis=1, keepdims=True), 0.0)
                    dat_ref[hd:hd + 1, :] -= jnp.sum(nm, axis=0, keepdims=True)
                    dcbs = dcbs + dw * ld * rowdt
                dxc_ref[:, ps] += dxacc
            dxc_ref[:, bs_] = db + _mdot(dcbs, cg, 0, 0)
            dxc_ref[:, cs_] = dc + _mdot(dcbs, bg)
        dws = _xdot(xbds_ref[...], et, 2)
        ws = cs["ws"]
        d_acum = d_acum - dws * ws + _xdot(yoffdy_ref[...], et, 2) + dat_ref[...].T
        d_alast = jnp.sum(dws * ws, axis=0, keepdims=True) + deal * cs["eal"]
        d_acum = d_acum + jnp.where(row == CH - 1, d_alast, 0.0)
        triu = row <= lane
        d_adt = _xdot(triu.astype(F32), d_acum, 3, split_b=True)
        ddt_ref[...] = dws * jnp.exp(cs["alast"] - cs["acum"]) + ddtt_ref[...].T + d_adt * cs["a"]
        dalog_ref[...] += jnp.sum(d_adt * cs["dtv"], axis=0, keepdims=True) * cs["a"]

    return _pcall(
        body, (dya, y, proj5, xbc_c, dt, states, a_log, dskip_x, ssm_norm, expand, reduce_), name="ssd_bwd",
        grid=(nc,),
        in_specs=[rv(D, nc), rv(D, nc), rv(D, nc, 0), rv(XBC, nc), rv(128, nc),
                  pl.BlockSpec((1, NS, D), lambda i: (nc - 1 - i, 0, 0)), _full_spec((1, 128)), _full_spec((1, D)),
                  _full_spec((1, D)), _full_spec((128, D)), _full_spec((D, 128))],
        out_specs=[rv(D, nc), rv(XBC, nc), rv(128, nc), _full_spec((1, D)), _full_spec((1, D)),
                   _full_spec((1, 128))],
        out_shape=[_sds((s, D), MXU), _sds((s, XBC)), _sds((s, 128)), _sds((1, D)), _sds((1, D)), _sds((1, 128))],
        scratch_shapes=[pltpu.VMEM((NS, D), F32), pltpu.VMEM((128, 128), F32), pltpu.VMEM((128, 128), F32),
                        pltpu.VMEM((128, 128), F32), pltpu.VMEM((128, 128), F32), pltpu.VMEM((CH, D), F32),
                        pltpu.VMEM((CH, D), F32), pltpu.VMEM((CH, D), F32)],
        sem=("arbitrary",), plan=plan)


def _conv_bwd(dxbc_c, dsilu, dxr, ddt, xbc_raw, proj5, dt_raw, cw_s, cw_l, dt_bias, plan=None):
    s = xbc_raw.shape[0]
    t = min(256, s)
    n = s // t

    def body(dxc_ref, dsl_ref, dxr_ref, ddt_ref, xs_ref, xl_ref, dtr_ref, cws_ref, cwl_ref, dtb_ref, dxs_ref,
             dxl_ref, ddtr_ref, dcws_ref, dcbs_ref, dcwl_ref, dcbl_ref, ddtb_ref, ds_ref, dl_ref):
        @pl.when(pl.program_id(0) == 0)
        def _():
            ds_ref[t:t + 8, :] = jnp.zeros((8, XBC), F32)
            dl_ref[t:t + 8, :] = jnp.zeros((8, D), F32)
            for r in (dcws_ref, dcbs_ref, dcwl_ref, dcbl_ref, ddtb_ref):
                r[...] = jnp.zeros_like(r)

        ds_ref[0:t, :] = dxc_ref[...] * dsl_ref[...].astype(F32)
        dl_ref[0:t, :] = dxr_ref[...]

        def back(dbuf, x_ref, w_ref, dx_ref, dw_ref, db_ref):
            xv = x_ref[...].astype(F32)
            dpre = dbuf[0:t, :]
            dx = w_ref[3:4, :] * dpre
            dw_ref[3:4, :] += jnp.sum(dpre * xv, axis=0, keepdims=True)
            db_ref[...] += jnp.sum(dpre, axis=0, keepdims=True)
            for k in (1, 2, 3):
                ahead = dbuf[k:t + k, :]
                dx = dx + w_ref[3 - k:4 - k, :] * ahead
                dw_ref[3 - k:4 - k, :] += jnp.sum(ahead * xv, axis=0, keepdims=True)
            dx_ref[...] = dx.astype(dx_ref.dtype)
            dbuf[t:t + 8, :] = dbuf[0:8, :]

        back(ds_ref, xs_ref, cws_ref, dxs_ref, dcws_ref, dcbs_ref)
        back(dl_ref, xl_ref, cwl_ref, dxl_ref, dcwl_ref, dcbl_ref)
        ddtr = ddt_ref[...] * _sig(dtr_ref[...] + dtb_ref[...])
        ddtr_ref[...] = ddtr.astype(ddtr_ref.dtype)
        ddtb_ref[...] += jnp.sum(ddtr, axis=0, keepdims=True)

    rv = functools.partial(_rev_spec, t)
    return _pcall(
        body, (dxbc_c, dsilu, dxr, ddt, xbc_raw, proj5, dt_raw, cw_s, cw_l, dt_bias), name="conv_bwd", grid=(n,),
        in_specs=[rv(XBC, n), rv(XBC, n), rv(D, n), rv(128, n), rv(XBC, n), rv(D, n, 2), rv(128, n),
                  _full_spec((4, XBC)), _full_spec((4, D)), _full_spec((1, 128))],
        out_specs=[rv(XBC, n), rv(D, n), rv(128, n), _full_spec((4, XBC)), _full_spec((1, XBC)), _full_spec((4, D)),
                   _full_spec((1, D)), _full_spec((1, 128))],
        out_shape=[_sds((s, XBC), MXU), _sds((s, D), MXU), _sds((s, 128), MXU), _sds((4, XBC)), _sds((1, XBC)),
                   _sds((4, D)), _sds((1, D)), _sds((1, 128))],
        scratch_shapes=[pltpu.VMEM((t + 8, XBC), F32), pltpu.VMEM((t + 8, D), F32)],
        sem=("arbitrary",), plan=plan)


def _du_norm(pieces5, dxbc, ddtr, w5, wxbc, wdt, x, dh1, g1, plan=None):
    s = x.shape[0]
    t = min(ROWS_FUSED, s)

    def body(p0, p1, p2, p3, p4, dxbc_ref, ddtr_ref, w5_ref, wx_ref, wd_ref, x_ref, dh1_ref, g1_ref, dx_ref, dg1_ref):
        @pl.when(pl.program_id(0) == 0)
        def _():
            dg1_ref[...] = jnp.zeros_like(dg1_ref)

        du = _mdot(dxbc_ref[...], wx_ref[...]) + _mdot(ddtr_ref[...], wd_ref[...])
        for b, p in enumerate((p0, p1, p2, p3, p4)):
            du = du + _mdot(p[...], w5_ref[D * b:D * (b + 1), :])
        xv = x_ref[...]
        dxn, dg1 = _rms_bwd(xv, _rms(xv), g1_ref[...], du)
        dx_ref[...] = dh1_ref[...] + dxn
        dg1_ref[...] += dg1

    return _pcall(
        body, (*pieces5, dxbc, ddtr, w5, wxbc, wdt, x, dh1, g1), name="du_norm", grid=(s // t,),
        in_specs=[_row_spec(t, D)] * 5 + [_row_spec(t, XBC), _row_spec(t, 128), _full_spec((5 * D, D), once=True),
                                          _full_spec((XBC, D), once=True), _full_spec((128, D), once=True),
                                          _row_spec(t, D), _row_spec(t, D), _full_spec((1, D))],
        out_specs=[_row_spec(t, D), _full_spec((1, D))],
        out_shape=[_sds((s, D)), _sds((1, D))],
        sem=("arbitrary",), plan=plan)


def _adamw(w, g, m, v, name):
    r, c = w.shape
    t = r
    if r * c > 256 * 1024:
        t = next(cand for cand in (512, 256, 128, 64, 32, 16, 8) if r % cand == 0 and cand * c <= 512 * 1024)
    bc1 = 1.0 - ADAM_B1 ** ADAM_STEP
    bc2 = 1.0 - ADAM_B2 ** ADAM_STEP

    def body(w_ref, g_ref, m_ref, v_ref, d_ref, nm_ref, nv_ref):
        gv = g_ref[...]
        nm = ADAM_B1 * m_ref[...] + (1.0 - ADAM_B1) * gv
        nv = ADAM_B2 * v_ref[...] + (1.0 - ADAM_B2) * (gv * gv)
        nm_ref[...] = nm
        nv_ref[...] = nv
        d_ref[...] = -ADAM_LR * ((nm / bc1) / (jnp.sqrt(nv / bc2) + ADAM_EPS) + ADAM_WD * w_ref[...])

    spec = pl.BlockSpec((t, c), lambda i: (i, 0))
    return pl.pallas_call(
        body, name=name, grid=(r // t,), in_specs=[spec] * 4, out_specs=[spec] * 3,
        out_shape=[_sds((r, c))] * 3, compiler_params=_cp("parallel"),
    )(w, g, m, v)


def _half_blocks(shape, axis):
    r, c = shape
    if axis == 0:
        t = 256 if (r // 2) % 256 == 0 else 128
        nb = (r // 2) // t
        return (t, c), nb, (lambda i: (i, 0)), (lambda i: (i % nb, 0))
    nb = (c // 2) // 128
    return (r, 128), nb, (lambda i: (0, i)), (lambda i: (0, i % nb))


def _adamw_halves(w, g_mine, g_other, m, v, cidx, name, axis=0):
    r, c = w.shape
    blk, nb, whole, part = _half_blocks(w.shape, axis)
    bc1 = 1.0 - ADAM_B1 ** ADAM_STEP
    bc2 = 1.0 - ADAM_B2 ** ADAM_STEP

    def body(c_ref, w_ref, gm_ref, go_ref, m_ref, v_ref, g_ref, d_ref, nm_ref, nv_ref):
        mine = (pl.program_id(0) // nb) == c_ref[0]
        gv = jnp.where(mine, gm_ref[...], go_ref[...])
        g_ref[...] = gv
        nm = ADAM_B1 * m_ref[...] + (1.0 - ADAM_B1) * gv
        nv = ADAM_B2 * v_ref[...] + (1.0 - ADAM_B2) * (gv * gv)
        nm_ref[...] = nm
        nv_ref[...] = nv
        d_ref[...] = -ADAM_LR * ((nm / bc1) / (jnp.sqrt(nv / bc2) + ADAM_EPS) + ADAM_WD * w_ref[...])

    spec = pl.BlockSpec(blk, lambda i, c_ref: whole(i))
    half = pl.BlockSpec(blk, lambda i, c_ref: part(i))
    return pl.pallas_call(
        body, name=name,
        grid_spec=pltpu.PrefetchScalarGridSpec(num_scalar_prefetch=1, grid=(2 * nb,),
                                               in_specs=[spec, half, half, spec, spec], out_specs=[spec] * 4),
        out_shape=[_sds((r, c))] * 4, compiler_params=_cp("parallel"),
    )(cidx, w, g_mine, g_other, m, v)


def _block_diag(w):
    eye = jnp.eye(NH, dtype=w.dtype)
    return (w[:, :, None, :] * eye[:, None, :, None]).reshape(D, D)


def _diag_blocks(full):
    eye = jnp.eye(NH, dtype=full.dtype)
    return (full.reshape(NH, HP, NH, HP) * eye[:, None, :, None]).sum(axis=2)


def _pad_lanes(v, n=128):
    return jnp.pad(v, ((0, 0), (0, n - v.shape[1])))


def _local_step(x, target, p, dist=None):
    heads = jnp.arange(D, dtype=jnp.int32) // HP
    expand = (jnp.arange(128, dtype=jnp.int32)[:, None] == heads[None, :]).astype(F32)
    reduce_ = expand.T
    dskip_x = jnp.repeat(p["d_skip"], HP, axis=1)
    a_log = _pad_lanes(p["a_log"])
    dt_bias = _pad_lanes(p["dt_bias"])
    wa_bd = _block_diag(p["lru_wa"]).astype(MXU)
    wx_bd = _block_diag(p["lru_wx"]).astype(MXU)
    ba = p["lru_ba"].reshape(1, D)
    bx = p["lru_bx"].reshape(1, D)

    def hosted(key, fn):
        plan = dist.plan(key) if dist is not None else None
        if plan is None:
            return fn(plan=None)
        outs, got = fn(plan=plan)
        dist.done(key, got, p)
        return outs

    u = hosted("norm_u", functools.partial(_norm_cast, x, p["norm_mix_pre"], "norm_u"))
    w5, wxbc, wdt = p["w5"], p["wxbc"], p["wdt"]
    proj5 = hosted("proj5", functools.partial(_matmul, u, w5, name="proj5", tb=True, tm=1024, out_dtype=MXU))
    xbc_raw = _matmul(u, wxbc, name="proj_xbc", tb=True, tn=XBC)
    dt_raw = _matmul(u, wdt, name="proj_dt", tb=True)
    xbc_c, dsilu, xr, dt = _conv_fwd(xbc_raw, proj5, dt_raw, p["conv_ssm_w"], p["conv_ssm_b"], p["conv_lru_w"],
                                     p["conv_lru_b"], dt_bias)
    y, ya, states = _ssd_fwd(xbc_c, dt, proj5, a_log, dskip_x, p["ssm_norm"], expand)
    h, merged, *gates = hosted("lru_fwd", functools.partial(_lru_fwd, xr, proj5, ya, wa_bd, wx_bd, ba, bx,
                                                            p["lru_lambda"]))
    mix, h1, v = _out_proj(merged, p["w_out"], x, p["norm_mix_post"], p["norm_mlp_pre"])
    pre = _matmul(v, p["w_up"], name="up_proj", tm=1024, out_dtype=MXU)
    dout, dff, loss, dg4 = _down_loss(pre, p["w_down"], h1, target, p["norm_mlp_post"])

    dpre = _matmul(dff, p["w_down"], name="d_pre", tb=True, tm=1024, out_dtype=MXU,
                   epi=lambda r, pr: r * (2.0 * jnp.maximum(pr.astype(F32), 0.0)), epi_args=(pre,))
    g_w_down = _matmul(pre, dff, name="dw_down", ta=True, tm=1024, tn=1024, tk=TK_GRAD, a_fn=_relu2)
    dh1, dmix, dg3, dg2 = _dv_norms(dpre, p["w_up"], h1, mix, dout, p["norm_mlp_pre"], p["norm_mix_post"])
    g_w_up = _matmul(v, dpre, name="dw_up", ta=True, tm=1024, tn=1024, tk=TK_GRAD)
    dmerged = _matmul(dmix, p["w_out"], name="d_merged", tb=True)
    g_w_out = _matmul(merged, dmix, name="dw_out", ta=True, tm=1024, tn=1024, tk=TK_GRAD)
    if dist is not None:
        dist.early_grads(w_down=g_w_down, w_up=g_w_up, w_out=g_w_out)
    (dya, dga, dgb, dg, dxr, dpr, dpi, dlam, dba, dbx) = hosted("lru_bwd", functools.partial(
        _lru_bwd, dmerged, ya, xr, h, proj5, gates, wa_bd, wx_bd, p["lru_lambda"]))
    g_wa = _diag_blocks(_matmul(xr, dpr, name="dw_lru_a", ta=True, tm=1024, tn=1024, tk=TK_GRAD))
    g_wx = _diag_blocks(_matmul(xr, dpi, name="dw_lru_x", ta=True, tm=1024, tn=1024, tk=TK_GRAD))
    dz, dxbc_c, ddt, dng, ddsk, dalog = hosted("ssd_bwd", functools.partial(
        _ssd_bwd, dya, y, proj5, xbc_c, dt, states, a_log, dskip_x, p["ssm_norm"], expand, reduce_))
    (dxbc, dxl, ddtr, dcws, dcbs, dcwl, dcbl, ddtb) = hosted("conv_bwd", functools.partial(
        _conv_bwd, dxbc_c, dsilu, dxr, ddt, xbc_raw, proj5, dt_raw, p["conv_ssm_w"], p["conv_lru_w"], dt_bias))
    pieces5 = (dz, dg, dxl, dga, dgb)
    gw5 = [_matmul(pc, u, name=f"dw_in_{i}", ta=True, tm=1024, tn=1024, tk=TK_GRAD) for i, pc in enumerate(pieces5)]
    gwxbc = _matmul(dxbc, u, name="dw_in_xbc", ta=True, tm=XBC, tn=1024, tk=TK_GRAD)
    gwdt = _matmul(ddtr, u, name="dw_in_dt", ta=True, tm=128, tn=1024, tk=TK_GRAD)
    g_w_in_t = jnp.concatenate([gw5[0], gwxbc, gwdt[:NH], gw5[1], gw5[2], gw5[3], gw5[4]], axis=0)
    grads = {
        "w_in_t": g_w_in_t, "conv_ssm_w": dcws, "conv_ssm_b": dcbs, "dt_bias": ddtb[:, :NH],
        "a_log": dalog[:, :NH], "d_skip": ddsk.reshape(NH, HP).sum(axis=1)[None, :], "ssm_norm": dng,
        "conv_lru_w": dcwl, "conv_lru_b": dcbl, "lru_wa": g_wa, "lru_ba": dba.reshape(NH, HP), "lru_wx": g_wx,
        "lru_bx": dbx.reshape(NH, HP), "lru_lambda": dlam, "w_out": g_w_out, "norm_mix_post": dg2,
        "norm_mlp_pre": dg3, "w_up": g_w_up, "w_down": g_w_down, "norm_mlp_post": dg4,
    }
    if dist is not None:
        dist.late_grads(grads, loss[0, 0])
    grad_x, grads["norm_mix_pre"] = hosted("du_norm", functools.partial(
        _du_norm, pieces5, dxbc, ddtr, w5, wxbc, wdt, x, dh1, p["norm_mix_pre"]))
    return loss[0, 0], grad_x, grads


def _split_w_in_t(w_in_t):
    z, xbc, dtc, g, xl, ga, gb = jnp.split(w_in_t, [D, D + XBC, D + XBC + NH, 2 * D + XBC + NH,
                                                    3 * D + XBC + NH, 4 * D + XBC + NH], axis=0)
    return jnp.concatenate([z, g, xl, ga, gb], axis=0), xbc, jnp.pad(dtc, ((0, 128 - NH), (0, 0)))


COMM = BF16


def _place():
    x, y, c = lax.axis_index("x"), lax.axis_index("y"), lax.axis_index("c")
    chips = [(1 - x, y), (x, 1 - y), (1 - x, 1 - y)]
    return x, y, c, chips


def _remote(src, dst, send_sem, recv_sem, to):
    return pltpu.make_async_remote_copy(src_ref=src, dst_ref=dst, send_sem=send_sem, recv_sem=recv_sem, device_id=to,
                                        device_id_type=MESH)


def _gather_plan(big, small=(), axes=None):
    nb = len(big)
    arrs = list(big) + list(small)
    na = len(arrs)
    axes = list(axes or [0] * nb)

    def half(ref, a, k, which):
        h = arrs[a].shape[axes[a]] // 2
        cut = (pl.ds(which * h, h),) if axes[a] == 0 else (slice(None), pl.ds(which * h, h))
        return ref.at[cut] if k is None else ref.at[(k,) + cut]

    def direct(ins, outs, send, recv):
        x, y, c, chips = _place()
        k = 2 * x + y
        cps = []
        for a in range(na):
            src, dst = (half(ins[a], a, None, c), half(outs[a], a, k, c)) if a < nb else (ins[a], outs[a].at[k])
            cps += [_remote(src, dst, send.at[a, j], recv.at[a, j], (cx, cy, c)) for j, (cx, cy) in enumerate(chips)]
        return cps

    def passed(outs, send, recv):
        x, y, c, chips = _place()
        cps = []
        for j, (cx, cy) in enumerate(chips):
            for a in range(nb):
                got = half(outs[a], a, 2 * cx + cy, c)
                cps.append(_remote(got, got, send.at[a, 3 + j], recv.at[a, 3 + j], (x, y, 1 - c)))
        return cps

    def start(ins, outs, sems):
        for cp in direct(ins, outs, *sems[0]):
            cp.start()

    def mid(ins, outs, sems):
        send, recv = sems[0]
        _, _, c, chips = _place()
        fwd = passed(outs, send, recv)
        for j, (cx, cy) in enumerate(chips):
            kj = 2 * cx + cy
            for a in range(na):
                got = half(outs[a], a, kj, c) if a < nb else outs[a].at[kj]
                _remote(got, got, send.at[a, j], recv.at[a, j], (cx, cy, c)).wait_recv()
                if a < nb:
                    fwd[j * nb + a].start()

    def finish(ins, outs, sems):
        send, recv = sems[0]
        x, y, c, chips = _place()
        for j, (cx, cy) in enumerate(chips):
            for a in range(nb):
                got = half(outs[a], a, 2 * cx + cy, 1 - c)
                _remote(got, got, send.at[a, 3 + j], recv.at[a, 3 + j], (x, y, 1 - c)).wait_recv()
        for cp in direct(ins, outs, send, recv) + passed(outs, send, recv):
            cp.wait_send()

    return _Plan(arrs, [_sds((NCHIP,) + a.shape, a.dtype) for a in arrs], [(na, 6)], start, finish, mid)


def _own_shards(gathered, shards):
    kchip = 2 * lax.axis_index("x") + lax.axis_index("y")
    return [lax.dynamic_update_index_in_dim(o, a, kchip, 0) for o, a in zip(gathered, shards)]


def _swap_plan(ins, outs, sems, copies):
    def start(i, o, s):
        for cp in copies(i, o, *s[0]):
            cp.start()

    def finish(i, o, s):
        for cp in copies(i, o, *s[0]):
            cp.wait()

    return _Plan(ins, outs, [sems], start, finish)


def _half_shape(shape, axis):
    return tuple(d // 2 if i == axis else d for i, d in enumerate(shape))


def _pair_exchange_plan(gs, axis=1):
    def copies(ins, outs, send, recv):
        x, y, c, _ = _place()
        cps = []
        for a in range(len(gs)):
            h = ins[a].shape[axis] // 2
            theirs = pl.ds((1 - c) * h, h)
            src = ins[a].at[:, theirs] if axis == 1 else ins[a].at[:, :, theirs]
            cps.append(_remote(src, outs[a], send.at[a], recv.at[a], (x, y, 1 - c)))
        return cps

    return _swap_plan(gs, [_sds(_half_shape(g.shape, axis), g.dtype) for g in gs], (len(gs),), copies)


def _pair_add(g, got, cidx, name, axis=1):
    half = _half_shape(g.shape, axis)
    blk, nt, _, part = _half_blocks(g.shape[1:], axis - 1)

    def body(c_ref, g_ref, o_ref, p_ref, pc_ref):
        sm = g_ref[...] + o_ref[...]
        p_ref[...] = sm
        pc_ref[...] = sm.astype(pc_ref.dtype)

    def mine(k, i, c_ref):
        j = c_ref[0] * nt + i
        return (k, j, 0) if axis == 1 else (k, 0, j)

    spec = pl.BlockSpec((1,) + blk, lambda k, i, c_ref: (k,) + part(i))
    return pl.pallas_call(
        body, name=name,
        grid_spec=pltpu.PrefetchScalarGridSpec(
            num_scalar_prefetch=1, grid=(NCHIP, nt),
            in_specs=[pl.BlockSpec((1,) + blk, mine), spec], out_specs=[spec, spec]),
        out_shape=[_sds(half), _sds(half, COMM)],
        compiler_params=_cp("parallel", "parallel"),
    )(cidx, g, got)


def _chip_exchange_plan(ps):
    def copies(ins, outs, send, recv):
        _, _, c, chips = _place()
        return [_remote(ins[a].at[2 * cx + cy], outs[a].at[j], send.at[a, j], recv.at[a, j], (cx, cy, c))
                for a in range(len(ps)) for j, (cx, cy) in enumerate(chips)]

    return _swap_plan(ps, [_sds((NCHIP - 1,) + p.shape[1:], p.dtype) for p in ps], (len(ps), 3), copies)


def _shard_sum(p, got, kidx, name, axis=1):
    full = tuple(2 * d if i == axis - 1 else d for i, d in enumerate(p.shape[1:]))
    blk, nt, _, part = _half_blocks(full, axis - 1)

    def body(k_ref, p_ref, g_ref, o_ref):
        sm = p_ref[0]
        for j in range(NCHIP - 1):
            sm = sm + g_ref[j].astype(F32)
        o_ref[...] = sm

    return pl.pallas_call(
        body, name=name,
        grid_spec=pltpu.PrefetchScalarGridSpec(
            num_scalar_prefetch=1, grid=(nt,),
            in_specs=[pl.BlockSpec((1,) + blk, lambda i, k_ref: (k_ref[0],) + part(i)),
                      pl.BlockSpec((NCHIP - 1,) + blk, lambda i, k_ref: (0,) + part(i))],
            out_specs=pl.BlockSpec(blk, lambda i, k_ref: part(i))),
        out_shape=_sds(p.shape[1:]),
        compiler_params=_cp("parallel"),
    )(kidx, p, got)


def _pair_swap_plan(rs):
    def copies(ins, outs, send, recv):
        x, y, c, _ = _place()
        return [_remote(ins[a], outs[a], send.at[a], recv.at[a], (x, y, 1 - c)) for a in range(len(rs))]

    return _swap_plan(rs, [_sds(r.shape, r.dtype) for r in rs], (len(rs),), copies)


def _allgather8_plan(v):
    def pieces(ins, outs, send, recv):
        x, y, c, chips = _place()
        me, sibling = (x, y, c), (x, y, 1 - c)

        def copy(k, block, to, src=None):
            px, py, pc = block
            slot = outs[0].at[4 * px + 2 * py + pc]
            return _remote(slot if src is None else src, slot, send.at[k], recv.at[k], to)

        first = [copy(0, me, sibling, src=ins[0])] + [copy(1 + j, me, (*chip, c), src=ins[0])
                                                      for j, chip in enumerate(chips)]
        passed = [copy(4 + j, (*chip, c), sibling) for j, chip in enumerate(chips)]
        arrivals = [copy(1 + j, (*chip, c), me) for j, chip in enumerate(chips)]
        late = [copy(0, sibling, me)] + [copy(4 + j, (*chip, 1 - c), me) for j, chip in enumerate(chips)]
        return first, passed, arrivals, late

    def start(ins, outs, sems):
        for cp in pieces(ins, outs, *sems[0])[0]:
            cp.start()

    def mid(ins, outs, sems):
        _, passed, arrivals, _ = pieces(ins, outs, *sems[0])
        for got, fwd in zip(arrivals, passed):
            got.wait_recv()
            fwd.start()

    def finish(ins, outs, sems):
        first, passed, _, late = pieces(ins, outs, *sems[0])
        for got in late:
            got.wait_recv()
        for cp in first + passed:
            cp.wait_send()

    return _Plan([v], [_sds((8,) + v.shape, v.dtype)], [(7,)], start, finish, mid)


def _own_block(gathered, v):
    me = 4 * lax.axis_index("x") + 2 * lax.axis_index("y") + lax.axis_index("c")
    return lax.dynamic_update_index_in_dim(gathered, v, me, 0)


def _sum_devices(allv, name):
    _, r, _ = allv.shape

    def body(a_ref, o_ref):
        sm = a_ref[0]
        for d in range(1, 8):
            sm = sm + a_ref[d]
        o_ref[...] = sm

    return pl.pallas_call(
        body, name=name, grid=(1,), in_specs=[_full_spec((8, r, 128))], out_specs=_full_spec((r, 128)),
        out_shape=_sds((r, 128)), compiler_params=_cp("arbitrary"),
    )(allv)


def _pack(arrs):
    flat = jnp.concatenate([a.reshape(-1) for a in arrs])
    return jnp.pad(flat, (0, (-flat.shape[0]) % 1024)).reshape(-1, 128)


def _unpack(packed, shapes):
    flat, outs, off = packed.reshape(-1), [], 0
    for shp in shapes:
        n = math.prod(shp)
        outs.append(flat[off:off + n].reshape(shp))
        off += n
    return outs


BIG = ("w_in", "w_out", "w_up", "w_down")
CONV = ("conv_ssm_w", "conv_lru_w")
WEIGHTS = ("norm_mix_pre", "w_in", "conv_ssm_w", "conv_ssm_b", "dt_bias", "a_log", "d_skip", "ssm_norm", "conv_lru_w",
           "conv_lru_b", "lru_wa", "lru_ba", "lru_wx", "lru_bx", "lru_lambda", "w_out", "norm_mix_post",
           "norm_mlp_pre", "w_up", "w_down", "norm_mlp_post")
SMALL = tuple(n for n in WEIGHTS if n not in BIG and n not in CONV)
EARLY = ("w_down", "w_up", "w_out")


def _cat_cols(g):
    return jnp.concatenate([g[k] for k in range(NCHIP)], axis=1)


class _Dist:
    def __init__(self, shards, first, cidx, kidx):
        self.shards, self.first, self.cidx, self.kidx = shards, first, cidx, kidx
        self.halves = {}

    def early_grads(self, w_down, w_up, w_out):
        self.shard_major = [w_down.reshape(NCHIP, D, D), jnp.stack([w_up[:, D * k:D * (k + 1)] for k in range(NCHIP)]),
                            w_out.reshape(NCHIP, D // NCHIP, D)]

    def late_grads(self, grads, loss):
        g_in = grads["w_in_t"].reshape(NCHIP, W_IN_SHARD, D)
        got, = _run_plan(_pair_exchange_plan([g_in], axis=2), "grad_pair_exchange_w_in")
        self.p_in, self.pc_in = _pair_add(g_in, got, self.cidx, "grad_pair_add_w_in", axis=2)
        self.small_names = [n for n in SMALL + CONV if n != "norm_mix_pre"]
        self.small_shapes = [grads[n].shape for n in self.small_names] + [(1,)]
        self.packed_small = _pack([grads[n] for n in self.small_names] + [loss.reshape(1)])

    def plan(self, key):
        if key == "norm_u":
            return _gather_plan(self.first[:1], self.first[1:], axes=[1])
        if key == "proj5":
            return _gather_plan([self.shards["w_out"], self.shards["w_up"]])
        if key == "lru_fwd":
            return _gather_plan([self.shards["w_down"]])
        if key == "lru_bwd":
            return _pair_exchange_plan(self.shard_major)
        if key == "ssd_bwd":
            return _chip_exchange_plan([pc for _, pc in self.pair])
        if key == "conv_bwd":
            return _pair_swap_plan(self.mine)
        if key == "du_norm":
            return _merge_plans(_allgather8_plan(self.packed_small), _chip_exchange_plan([self.pc_in]))
        return None

    def done(self, key, got, p):
        if key == "norm_u":
            g_in, g_cs, g_cl = _own_shards(got, self.first)
            w5, wxbc, wdt = _split_w_in_t(g_in.reshape(W_IN_COLS, D))
            p.update(w5=w5, wxbc=wxbc, wdt=wdt, conv_ssm_w=_cat_cols(g_cs), conv_lru_w=_cat_cols(g_cl))
        elif key == "proj5":
            g_out, g_up = _own_shards(got, [self.shards["w_out"], self.shards["w_up"]])
            p.update(w_out=g_out.reshape(D, D), w_up=_cat_cols(g_up))
        elif key == "lru_fwd":
            g_down, = _own_shards(got, [self.shards["w_down"]])
            p.update(w_down=g_down.reshape(DFF, D))
        elif key == "lru_bwd":
            self.pair = [_pair_add(gs, o, self.cidx, f"grad_pair_add_{n}")
                         for gs, o, n in zip(self.shard_major, got, EARLY)]
        elif key == "ssd_bwd":
            self.mine = [_shard_sum(pf, o, self.kidx, f"grad_shard_sum_{n}")
                         for (pf, _), o, n in zip(self.pair, got, EARLY)]
        elif key == "conv_bwd":
            self.halves = {n: (mine, other) for n, mine, other in zip(EARLY, self.mine, got)}
        elif key == "du_norm":
            self.all_small, self.from_chips_in = got


def kernel(x, norm_mix_pre, w_in, conv_ssm_w, conv_ssm_b, dt_bias, a_log, d_skip, ssm_norm, conv_lru_w, conv_lru_b, lru_wa, lru_ba, lru_wx, lru_bx, lru_lambda, w_out, norm_mix_post, norm_mlp_pre, w_up, w_down, norm_mlp_post, loss_target, m_norm_mix_pre, m_w_in, m_conv_ssm_w, m_conv_ssm_b, m_dt_bias, m_a_log, m_d_skip, m_ssm_norm, m_conv_lru_w, m_conv_lru_b, m_lru_wa, m_lru_ba, m_lru_wx, m_lru_bx, m_lru_lambda, m_w_out, m_norm_mix_post, m_norm_mlp_pre, m_w_up, m_w_down, m_norm_mlp_post, v_norm_mix_pre, v_w_in, v_conv_ssm_w, v_conv_ssm_b, v_dt_bias, v_a_log, v_d_skip, v_ssm_norm, v_conv_lru_w, v_conv_lru_b, v_lru_wa, v_lru_ba, v_lru_wx, v_lru_bx, v_lru_lambda, v_w_out, v_norm_mix_post, v_norm_mlp_pre, v_w_up, v_w_down, v_norm_mlp_post):
    args = locals()
    w = {n: args[n][0] for n in WEIGHTS}
    m = {n: args["m_" + n][0] for n in WEIGHTS}
    v = {n: args["v_" + n][0] for n in WEIGHTS}
    cidx = lax.axis_index("c").astype(jnp.int32).reshape(1)
    kchip = 2 * lax.axis_index("x") + lax.axis_index("y")
    to_t = lambda a: jnp.transpose(a, (2, 0, 1)).reshape(W_IN_SHARD, D)
    from_t = lambda a: jnp.transpose(a.reshape(W_IN_SHARD, 1, D), (1, 2, 0))
    shards = {n: (to_t(w_in) if n == "w_in" else w[n]).astype(MXU) for n in BIG}
    dist = _Dist(shards, [shards["w_in"], w["conv_ssm_w"], w["conv_lru_w"]], cidx, kchip.astype(jnp.int32).reshape(1))
    p = {n: (w[n].reshape(1, -1) if w[n].ndim == 1 else w[n]) for n in SMALL}

    _, grad_x, g = _local_step(x[0], loss_target[0], p, dist)

    half_in = _shard_sum(dist.p_in, dist.from_chips_in, dist.kidx, "grad_shard_sum_w_in", axis=2)
    packed_g1 = _pack([g["norm_mix_pre"]])
    all_g1, other_in = _run_plan(_merge_plans(_allgather8_plan(packed_g1), _pair_swap_plan([half_in])),
                                 "grad_pair_swap_w_in")
    halves = dist.halves

    reduced = {}
    *summed, loss = _unpack(_sum_devices(_own_block(dist.all_small, dist.packed_small), "small_sum"),
                            dist.small_shapes)
    loss = loss.reshape(())
    g1, = _unpack(_sum_devices(_own_block(all_g1, packed_g1), "small_sum_norm_mix_pre"), [g["norm_mix_pre"].shape])
    for n, s in zip(dist.small_names + ["norm_mix_pre"], summed + [g1]):
        if n in CONV:
            width = w[n].shape[1]
            reduced[n] = lax.dynamic_slice_in_dim(s, kchip * width, width, axis=1)
        else:
            reduced[n] = s.reshape(w[n].shape)

    delta, new_m, new_v = {}, {}, {}
    for n in EARLY:
        mine, other = halves[n]
        reduced[n], delta[n], new_m[n], new_v[n] = _adamw_halves(w[n], mine, other, m[n], v[n], cidx, f"adamw_{n}")
    outs_t = _adamw_halves(to_t(w_in), half_in, other_in, to_t(m_w_in), to_t(v_w_in), cidx, "adamw_w_in", axis=1)
    for d, o in zip((reduced, delta, new_m, new_v), outs_t):
        d["w_in"] = from_t(o)[0]
    for n in CONV:
        delta[n], new_m[n], new_v[n] = _adamw(w[n], reduced[n], m[n], v[n], f"adamw_{n}")
    shapes = [w[n].shape for n in SMALL]
    packed = [_pack([d[n] for n in SMALL]) for d in (w, reduced, m, v)]
    for d, out in zip((delta, new_m, new_v), _adamw(*packed, "adamw_small")):
        d.update(zip(SMALL, _unpack(out, shapes)))

    lead = lambda d: [d[n][None] for n in WEIGHTS]
    return (loss, grad_x[None], *lead(reduced), *lead(delta), *lead(new_m), *lead(new_v))
```

```python
import functools
import math

import jax
import jax.numpy as jnp
from jax import lax
from jax.experimental import pallas as pl
from jax.experimental.pallas import tpu as pltpu

F32 = jnp.float32
BF16 = jnp.bfloat16
MXU = BF16

D = 1024
DFF = 4096
NH = 16
HP = 64
NG = 2
NS = 128
CH = 128
XBC = D + 2 * NG * NS
GW = D // NG
LRU_C = 8.0
EPS = 1e-6
NCHIP = 4
W_IN_COLS = 6672
W_IN_SHARD = W_IN_COLS // NCHIP

ADAM_LR = 0.001
ADAM_B1 = 0.9
ADAM_B2 = 0.999
ADAM_EPS = 1e-08
ADAM_WD = 0.01
ADAM_STEP = 10

VMEM_LIMIT = 56 * 1024 * 1024
TK_GRAD = 2048
MID_AT = 0.7
ROWS_FUSED = 512
COL_G, COL_GA, COL_GB, COL_Z, COL_XL = range(5)
MESH = pl.DeviceIdType.MESH


def _cp(*sem):
    return pltpu.CompilerParams(dimension_semantics=sem, vmem_limit_bytes=VMEM_LIMIT)


def _dot(a, b, ca=1, cb=0, prec=None):
    return lax.dot_general(a, b, (((ca,), (cb,)), ((), ())), precision=prec, preferred_element_type=F32)


def _mdot(a, b, ca=1, cb=0):
    return _dot(a.astype(MXU), b.astype(MXU), ca, cb)


def _bf16_parts(v, n):
    parts = []
    for i in range(n):
        p = v.astype(BF16)
        parts.append(p)
        if i < n - 1:
            v = v - p.astype(F32)
    return parts


def _xdot(a, b, passes, split_b=False):
    if split_b:
        a16 = a.astype(BF16)
        terms = [_dot(a16, p) for p in _bf16_parts(b, passes)]
    else:
        b16 = b.astype(BF16)
        terms = [_dot(p, b16) for p in _bf16_parts(a, passes)]
    return functools.reduce(lambda u, v: u + v, terms)


def _sig(x):
    return 0.5 * jnp.tanh(0.5 * x) + 0.5


def _silu(x):
    return x * _sig(x)


def _dsilu(x):
    s = _sig(x)
    return s * (1.0 + x * (1.0 - s))


def _softplus(x):
    e = jnp.exp(-jnp.abs(x))
    return jnp.maximum(x, 0.0) + jnp.where(e < 1e-4, e * (1.0 - 0.5 * e), jnp.log(1.0 + e))


_GELU_C = math.sqrt(2.0 / math.pi)


def _gelu(x):
    t = jnp.tanh(_GELU_C * (x + 0.044715 * x * x * x))
    return 0.5 * x * (1.0 + t)


def _gelu_and_grad(x):
    x2 = x * x
    t = jnp.tanh(_GELU_C * (x + 0.044715 * x * x2))
    half = 0.5 * (1.0 + t)
    return x * half, half + 0.5 * x * (1.0 - t * t) * _GELU_C * (1.0 + 3.0 * 0.044715 * x2)


def _one_minus_sq(a, la):
    x = 2.0 * la
    series = -x * (1.0 + x * (0.5 + x * (1.0 / 6.0)))
    return jnp.where(x > -0.01, series, 1.0 - a * a)


def _rms(x):
    return lax.rsqrt(jnp.mean(x * x, axis=-1, keepdims=True) + EPS)


def _rms_bwd(x, r, g, dy):
    xn = x * r
    dxh = dy * g
    m = jnp.mean(dxh * xn, axis=-1, keepdims=True)
    return r * (dxh - xn * m), jnp.sum(dy * xn, axis=0, keepdims=True)


def _row_spec(t, c, col=0):
    return pl.BlockSpec((t, c), lambda i: (i, col))


def _rev_spec(t, c, n, col=0):
    return pl.BlockSpec((t, c), lambda i: (n - 1 - i, col))


def _full_spec(shape, once=False):
    nd = len(shape)
    if once:
        return pl.BlockSpec(shape, lambda *_: (0,) * nd, pipeline_mode=pl.Buffered(1))
    return pl.BlockSpec(shape, lambda *_: (0,) * nd)


def _sds(shape, dtype=F32):
    return jax.ShapeDtypeStruct(shape, dtype)


ANY = pl.BlockSpec(memory_space=pl.ANY)


class _Plan:
    def __init__(self, ins, outs, sems, start, finish, mid=None):
        self.ins, self.outs, self.sems = list(ins), list(outs), list(sems)
        self.start, self.finish, self.mid = start, finish, mid or (lambda i, o, s: None)


def _merge_plans(*plans):
    def each(fn_name, ins, outs, sems):
        i = o = s = 0
        for p in plans:
            getattr(p, fn_name)(ins[i:i + len(p.ins)], outs[o:o + len(p.outs)], sems[s:s + len(p.sems)])
            i, o, s = i + len(p.ins), o + len(p.outs), s + len(p.sems)

    return _Plan([a for p in plans for a in p.ins], [a for p in plans for a in p.outs],
                 [a for p in plans for a in p.sems], functools.partial(each, "start"),
                 functools.partial(each, "finish"), functools.partial(each, "mid"))


def _pcall(body, args, *, name, grid, in_specs, out_specs, out_shape, sem, scratch_shapes=(), plan=None):
    single = not isinstance(out_shape, (list, tuple))
    out_specs = [out_specs] if single else list(out_specs)
    out_shape = [out_shape] if single else list(out_shape)
    if plan is None:
        outs = pl.pallas_call(body, name=name, grid=grid, in_specs=list(in_specs), out_specs=out_specs,
                              out_shape=out_shape, scratch_shapes=list(scratch_shapes),
                              compiler_params=_cp(*sem))(*args)
        return outs[0] if single else outs
    n_in, n_out, n_sc, ni, no = len(in_specs), len(out_shape), len(scratch_shapes), len(plan.ins), len(plan.outs)

    def hosted(*refs):
        b0 = n_in + ni
        b1 = b0 + n_out + no
        sem_refs = refs[b1 + n_sc:]
        sems = [(sem_refs[2 * q], sem_refs[2 * q + 1]) for q in range(len(plan.sems))]
        step = functools.reduce(lambda lin, ig: lin * ig[1] + ig[0],
                                [(pl.program_id(d), g) for d, g in enumerate(grid)], 0)
        total = math.prod(grid)

        @pl.when(step == 0)
        def _():
            plan.start(refs[n_in:b0], refs[b0 + n_out:b1], sems)

        body(*refs[:n_in], *refs[b0:b0 + n_out], *refs[b1:b1 + n_sc])

        @pl.when(step == min(int(MID_AT * total), total - 1))
        def _():
            plan.mid(refs[n_in:b0], refs[b0 + n_out:b1], sems)

        @pl.when(step == total - 1)
        def _():
            plan.finish(refs[n_in:b0], refs[b0 + n_out:b1], sems)

    dma = [pltpu.SemaphoreType.DMA(shape) for shape in plan.sems for _ in range(2)]
    outs = pl.pallas_call(hosted, name=name, grid=grid, in_specs=list(in_specs) + [ANY] * ni,
                          out_specs=out_specs + [ANY] * no, out_shape=out_shape + plan.outs,
                          scratch_shapes=list(scratch_shapes) + dma,
                          compiler_params=_cp(*("arbitrary",) * len(grid)))(*args, *plan.ins)
    return (outs[0] if single else outs[:n_out]), outs[n_out:]


def _run_plan(plan, name):
    ni, no = len(plan.ins), len(plan.outs)

    def body(*refs):
        sem_refs = refs[ni + no:]
        sems = [(sem_refs[2 * q], sem_refs[2 * q + 1]) for q in range(len(plan.sems))]
        plan.start(refs[:ni], refs[ni:ni + no], sems)
        plan.mid(refs[:ni], refs[ni:ni + no], sems)
        plan.finish(refs[:ni], refs[ni:ni + no], sems)

    return pl.pallas_call(
        body, name=name, in_specs=[ANY] * ni, out_specs=[ANY] * no, out_shape=plan.outs,
        scratch_shapes=[pltpu.SemaphoreType.DMA(shape) for shape in plan.sems for _ in range(2)],
    )(*plan.ins)


def _matmul(a, b, *, name, ta=False, tb=False, tm=512, tn=1024, tk=1024, out_dtype=F32, a_fn=None, epi=None,
            epi_args=(), plan=None):
    m, k = (a.shape[1], a.shape[0]) if ta else a.shape
    n = b.shape[0] if tb else b.shape[1]
    tm, tn, tk = min(tm, m), min(tn, n), min(tk, k)
    nk = k // tk
    a_spec = pl.BlockSpec((tk, tm), lambda i, j, kk: (kk, i)) if ta else pl.BlockSpec((tm, tk), lambda i, j, kk: (i, kk))
    b_spec = pl.BlockSpec((tn, tk), lambda i, j, kk: (j, kk)) if tb else pl.BlockSpec((tk, tn), lambda i, j, kk: (kk, j))
    e_specs = [pl.BlockSpec((tm, tn), lambda i, j, kk: (i, j)) for _ in epi_args]
    ne = len(epi_args)

    def body(a_ref, b_ref, *rest):
        e_refs, o_ref = rest[:ne], rest[ne]
        av = a_ref[...]
        if a_fn is not None:
            av = a_fn(av)
        part = _mdot(av, b_ref[...], 0 if ta else 1, 1 if tb else 0)

        def finish(r):
            if epi is not None:
                r = epi(r, *[e[...] for e in e_refs])
            o_ref[...] = r.astype(o_ref.dtype)

        if nk == 1:
            finish(part)
            return
        acc_ref = rest[ne + 1]
        kk = pl.program_id(2)

        @pl.when(kk == 0)
        def _():
            acc_ref[...] = part

        @pl.when(jnp.logical_and(kk > 0, kk < nk - 1))
        def _():
            acc_ref[...] += part

        @pl.when(kk == nk - 1)
        def _():
            finish(acc_ref[...] + part)

    return _pcall(
        body, (a, b, *epi_args), name=name, grid=(m // tm, n // tn, nk),
        in_specs=[a_spec, b_spec] + e_specs,
        out_specs=pl.BlockSpec((tm, tn), lambda i, j, kk: (i, j)),
        out_shape=_sds((m, n), out_dtype),
        scratch_shapes=[pltpu.VMEM((tm, tn), F32)] if nk > 1 else [],
        sem=("parallel", "parallel", "arbitrary"), plan=plan)


def _relu2(p):
    p = jnp.maximum(p, jnp.zeros((), p.dtype))
    return p * p


def _norm_cast(x, g, name, plan=None):
    s = x.shape[0]
    t = min(512, s)

    def body(x_ref, g_ref, o_ref):
        xv = x_ref[...]
        o_ref[...] = (xv * _rms(xv) * g_ref[...]).astype(o_ref.dtype)

    return _pcall(body, (x, g), name=name, grid=(s // t,), in_specs=[_row_spec(t, D), _full_spec((1, D))],
                  out_specs=_row_spec(t, D), out_shape=_sds((s, D), MXU), sem=("parallel",), plan=plan)


def _conv_fwd(xbc_raw, proj5, dt_raw, cw_s, cb_s, cw_l, cb_l, dt_bias):
    s = xbc_raw.shape[0]
    t = min(256, s)

    def body(xs_ref, xl_ref, dtr_ref, cws_ref, cbs_ref, cwl_ref, cbl_ref, dtb_ref, xc_ref, dsl_ref, xr_ref, dt_ref,
             bs_ref, bl_ref):
        @pl.when(pl.program_id(0) == 0)
        def _():
            bs_ref[0:8, :] = jnp.zeros((8, XBC), F32)
            bl_ref[0:8, :] = jnp.zeros((8, D), F32)

        bs_ref[8:t + 8, :] = xs_ref[...]
        bl_ref[8:t + 8, :] = xl_ref[...].astype(F32)

        def conv(buf, w_ref, b_ref):
            acc = b_ref[...] + w_ref[3:4, :] * buf[8:t + 8, :]
            for k in (1, 2, 3):
                acc = acc + w_ref[3 - k:4 - k, :] * buf[8 - k:t + 8 - k, :]
            return acc

        pre = conv(bs_ref, cws_ref, cbs_ref)
        sg = _sig(pre)
        xc_ref[...] = pre * sg
        dsl_ref[...] = (sg * (1.0 + pre * (1.0 - sg))).astype(dsl_ref.dtype)
        xr_ref[...] = conv(bl_ref, cwl_ref, cbl_ref)
        dt_ref[...] = _softplus(dtr_ref[...] + dtb_ref[...])
        bs_ref[0:8, :] = bs_ref[t:t + 8, :]
        bl_ref[0:8, :] = bl_ref[t:t + 8, :]

    return pl.pallas_call(
        body, name="conv_fwd", grid=(s // t,),
        in_specs=[_row_spec(t, XBC), _row_spec(t, D, COL_XL), _row_spec(t, 128), _full_spec((4, XBC)),
                  _full_spec((1, XBC)), _full_spec((4, D)), _full_spec((1, D)), _full_spec((1, 128))],
        out_specs=[_row_spec(t, XBC), _row_spec(t, XBC), _row_spec(t, D), _row_spec(t, 128)],
        out_shape=[_sds((s, XBC)), _sds((s, XBC), BF16), _sds((s, D)), _sds((s, 128))],
        scratch_shapes=[pltpu.VMEM((t + 8, XBC), F32), pltpu.VMEM((t + 8, D), F32)],
        compiler_params=_cp("arbitrary"),
    )(xbc_raw, proj5, dt_raw, cw_s, cb_s, cw_l, cb_l, dt_bias)


def _ssd_chunk_setup(dt_ref, alog_ref, e_ref, at_ref, dtt_ref):
    lane = lax.broadcasted_iota(jnp.int32, (CH, 128), 1)
    row = lax.broadcasted_iota(jnp.int32, (CH, 128), 0)
    lane1 = lax.broadcasted_iota(jnp.int32, (1, 128), 1)
    a = jnp.where(lane1 < NH, -jnp.exp(alog_ref[...]), 0.0)
    dtv = dt_ref[...]
    adt = dtv * a
    tril = row >= lane
    acum = _xdot(tril.astype(F32), adt, 3, split_b=True)
    alast = jnp.sum(adt, axis=0, keepdims=True)
    at_ref[...] = acum.T
    dtt_ref[...] = dtv.T
    e = e_ref[...]
    ea_x = _xdot(jnp.exp(acum), e, 2)
    ws = jnp.exp(alast - acum) * dtv
    ws_x = _xdot(ws, e, 2)
    eal = jnp.exp(alast)
    eal_x = jnp.max(_xdot(jnp.broadcast_to(eal, (8, 128)), e, 3), axis=0, keepdims=True)
    return dict(lane=lane, row=row, tril=tril, a=a, dtv=dtv, acum=acum, alast=alast, ea_x=ea_x, ws=ws, ws_x=ws_x,
                eal=eal, eal_x=eal_x)


def _head_decay(cs, at_ref, dtt_ref, h):
    col = jnp.sum(jnp.where(cs["lane"] == h, cs["acum"], 0.0), axis=1, keepdims=True)
    ld = jnp.where(cs["tril"], jnp.exp(jnp.minimum(col - at_ref[h:h + 1, :], 0.0)), 0.0)
    return ld, dtt_ref[h:h + 1, :]


def _ssd_fwd(xbc_c, dt, proj5, a_log, dskip_x, ssm_norm, expand):
    s = xbc_c.shape[0]
    nc = s // CH

    def body(xc_ref, dt_ref, z_ref, alog_ref, dsk_ref, ng_ref, e_ref, y_ref, ya_ref, st_ref, h_ref, at_ref, dtt_ref,
             yd_ref):
        @pl.when(pl.program_id(0) == 0)
        def _():
            h_ref[...] = jnp.zeros_like(h_ref)

        cs = _ssd_chunk_setup(dt_ref, alog_ref, e_ref, at_ref, dtt_ref)
        lane = cs["lane"]
        for g in range(NG):
            gs = slice(GW * g, GW * (g + 1))
            bg = xc_ref[:, D + NS * g:D + NS * (g + 1)]
            cg = xc_ref[:, D + NG * NS + NS * g:D + NG * NS + NS * (g + 1)]
            cb = _mdot(cg, bg, 1, 1)
            for j in range(4 * g, 4 * g + 4):
                ps = slice(128 * j, 128 * (j + 1))
                xp = xc_ref[:, ps]
                acc = jnp.zeros((CH, 128), F32)
                for hf in range(2):
                    ld, rowdt = _head_decay(cs, at_ref, dtt_ref, 2 * j + hf)
                    hm = (lane >= HP) if hf else (lane < HP)
                    acc = acc + _mdot(cb * ld * rowdt, jnp.where(hm, xp, 0.0))
                yd_ref[:, ps] = acc
            hg = h_ref[:, gs]
            yd_ref[:, gs] += _mdot(cg, hg) * cs["ea_x"][:, gs]
            st = _mdot(bg, xc_ref[:, gs] * cs["ws_x"][:, gs], 0, 0)
            st_ref[0, :, gs] = hg
            h_ref[:, gs] = cs["eal_x"][:, gs] * hg + st
        y = yd_ref[...] + dsk_ref[...] * xc_ref[:, 0:D]
        y_ref[...] = y
        yg = y * _silu(z_ref[...].astype(F32))
        for g in range(NG):
            gs = slice(GW * g, GW * (g + 1))
            seg = yg[:, gs]
            ya_ref[:, gs] = seg * _rms(seg) * ng_ref[:, gs]

    return pl.pallas_call(
        body, name="ssd_fwd", grid=(nc,),
        in_specs=[_row_spec(CH, XBC), _row_spec(CH, 128), _row_spec(CH, D, COL_Z), _full_spec((1, 128)),
                  _full_spec((1, D)), _full_spec((1, D)), _full_spec((128, D))],
        out_specs=[_row_spec(CH, D), _row_spec(CH, D), pl.BlockSpec((1, NS, D), lambda i: (i, 0, 0))],
        out_shape=[_sds((s, D)), _sds((s, D)), _sds((nc, NS, D))],
        scratch_shapes=[pltpu.VMEM((NS, D), F32), pltpu.VMEM((128, 128), F32), pltpu.VMEM((128, 128), F32),
                        pltpu.VMEM((CH, D), F32)],
        compiler_params=_cp("arbitrary"),
    )(xbc_c, dt, proj5, a_log, dskip_x, ssm_norm, expand)


def _lru_gates(xr, wab_ref, ba_ref, bx_ref, lam_ref):
    pre = _mdot(xr, wab_ref[...])
    gr = _sig(pre[:, 0:D] + ba_ref[...])
    gi = _sig(pre[:, D:2 * D] + bx_ref[...])
    sp = _softplus(-lam_ref[...])
    la = -LRU_C * gr * sp
    a = jnp.exp(la)
    oms = _one_minus_sq(a, la)
    inv_mult = lax.rsqrt(oms)
    return gr, gi, sp, a, oms * inv_mult, inv_mult


def _blocked_scan(a, u, carry_ref, a_ref, u_ref, c_ref, out_ref, reverse):
    t = a.shape[0]
    ns = t // 8

    def combine(av, uv, idx, n, sh):
        m = (idx < n - sh) if reverse else (idx >= sh)
        by = n - sh if reverse else sh
        return jnp.where(m, av * pltpu.roll(av, by, 0), av), jnp.where(m, uv + av * pltpu.roll(uv, by, 0), uv)

    row = lax.broadcasted_iota(jnp.int32, (t, D), 0)
    rin = jnp.bitwise_and(row, 7)
    for sh in (1, 2, 4):
        m = (rin < 8 - sh) if reverse else (rin >= sh)
        by = t - sh if reverse else sh
        a, u = jnp.where(m, a * pltpu.roll(a, by, 0), a), jnp.where(m, u + a * pltpu.roll(u, by, 0), u)
    a_ref[...] = a
    u_ref[...] = u
    edge = 0 if reverse else 7
    for j in range(ns):
        c_ref[j:j + 1, :] = a_ref[8 * j + edge:8 * j + edge + 1, :]
    at = c_ref[...]
    for j in range(ns):
        c_ref[j:j + 1, :] = u_ref[8 * j + edge:8 * j + edge + 1, :]
    ut = c_ref[...]
    srow = lax.broadcasted_iota(jnp.int32, (ns, D), 0)
    sh = 1
    while sh < ns:
        at, ut = combine(at, ut, srow, ns, sh)
        sh *= 2
    cv = carry_ref[0:1, :]
    ends = ut + at * cv
    last = 0 if reverse else ns - 1
    first = ns - 1 if reverse else 0
    c_ref[...] = jnp.where(srow == first, cv, pltpu.roll(ends, first if reverse else 1, 0))
    carry_ref[0:1, :] = jnp.sum(jnp.where(srow == last, ends, 0.0), axis=0, keepdims=True)
    for j in range(ns):
        sl = slice(8 * j, 8 * j + 8)
        out_ref[sl, :] = u_ref[sl, :] + a_ref[sl, :] * c_ref[j:j + 1, :]


def _lru_fwd(xr, proj5, ya, wab, ba, bx, lam, plan=None):
    s = xr.shape[0]
    t = min(256, s)

    def body(xr_ref, g_ref, ga_ref, gb_ref, ya_ref, wab_ref, ba_ref, bx_ref, lam_ref, h_ref, mg_ref, gr_ref,
             gi_ref, ao_ref, mo_ref, hc_ref, a_ref, u_ref, c_ref):
        @pl.when(pl.program_id(0) == 0)
        def _():
            hc_ref[...] = jnp.zeros_like(hc_ref)

        xrv = xr_ref[...]
        gr, gi, _, a, mult, _ = _lru_gates(xrv, wab_ref, ba_ref, bx_ref, lam_ref)
        gr_ref[...], gi_ref[...], ao_ref[...], mo_ref[...] = gr, gi, a, mult
        _blocked_scan(a, mult * gi * xrv, hc_ref, a_ref, u_ref, c_ref, h_ref, reverse=False)
        yb = h_ref[...] * _gelu(g_ref[...].astype(F32))
        mg_ref[...] = (_sig(ga_ref[...].astype(F32)) * ya_ref[...]
                       + _sig(gb_ref[...].astype(F32)) * yb).astype(mg_ref.dtype)

    return _pcall(
        body, (xr, proj5, proj5, proj5, ya, wab, ba, bx, lam), name="lru_fwd", grid=(s // t,),
        in_specs=[_row_spec(t, D), _row_spec(t, D, COL_G), _row_spec(t, D, COL_GA), _row_spec(t, D, COL_GB),
                  _row_spec(t, D), _full_spec((D, 2 * D), once=True), _full_spec((1, D)), _full_spec((1, D)),
                  _full_spec((1, D))],
        out_specs=[_row_spec(t, D)] * 6,
        out_shape=[_sds((s, D)), _sds((s, D), MXU)] + [_sds((s, D))] * 4,
        scratch_shapes=[pltpu.VMEM((8, D), F32), pltpu.VMEM((t, D), F32), pltpu.VMEM((t, D), F32),
                        pltpu.VMEM((t // 8, D), F32)],
        sem=("arbitrary",), plan=plan)


def _out_proj(merged, w_out, x, g2, g3):
    s = x.shape[0]
    t = min(ROWS_FUSED, s)

    def body(mg_ref, w_ref, x_ref, g2_ref, g3_ref, mix_ref, h1_ref, v_ref):
        mix = _mdot(mg_ref[...], w_ref[...])
        mix_ref[...] = mix
        h1 = x_ref[...] + mix * _rms(mix) * g2_ref[...]
        h1_ref[...] = h1
        v_ref[...] = (h1 * _rms(h1) * g3_ref[...]).astype(v_ref.dtype)

    return pl.pallas_call(
        body, name="out_proj", grid=(s // t,),
        in_specs=[_row_spec(t, D), _full_spec((D, D), once=True), _row_spec(t, D), _full_spec((1, D)),
                  _full_spec((1, D))],
        out_specs=[_row_spec(t, D), _row_spec(t, D), _row_spec(t, D)],
        out_shape=[_sds((s, D)), _sds((s, D)), _sds((s, D), MXU)],
        compiler_params=_cp("parallel"),
    )(merged, w_out, x, g2, g3)


def _down_loss(pre, w_down, h1, target, g4):
    s = pre.shape[0]
    t = min(ROWS_FUSED, s)

    def body(pre_ref, w_ref, h1_ref, tg_ref, g4_ref, dout_ref, dff_ref, loss_ref, dg4_ref):
        @pl.when(pl.program_id(0) == 0)
        def _():
            loss_ref[...] = jnp.zeros_like(loss_ref)
            dg4_ref[...] = jnp.zeros_like(dg4_ref)

        ff = _mdot(_relu2(pre_ref[...]), w_ref[...])
        r4 = _rms(ff)
        g4v = g4_ref[...]
        diff = h1_ref[...] + ff * r4 * g4v - tg_ref[...]
        sq = jnp.sum(jnp.sum(diff * diff, axis=1, keepdims=True), axis=0, keepdims=True)
        loss_ref[...] += (0.5 / D) * sq
        dout = diff * (1.0 / D)
        dout_ref[...] = dout
        dff, dg = _rms_bwd(ff, r4, g4v, dout)
        dff_ref[...] = dff.astype(dff_ref.dtype)
        dg4_ref[...] += dg

    return pl.pallas_call(
        body, name="down_loss", grid=(s // t,),
        in_specs=[_row_spec(t, DFF), _full_spec((DFF, D), once=True), _row_spec(t, D), _row_spec(t, D),
                  _full_spec((1, D))],
        out_specs=[_row_spec(t, D), _row_spec(t, D), _full_spec((1, 128)), _full_spec((1, D))],
        out_shape=[_sds((s, D)), _sds((s, D), MXU), _sds((1, 128)), _sds((1, D))],
        compiler_params=_cp("arbitrary"),
    )(pre, w_down, h1, target, g4)


def _dv_norms(dpre, w_up, h1, mix, dout, g3, g2):
    s = h1.shape[0]
    t = min(ROWS_FUSED, s)

    def body(dp_ref, w_ref, h1_ref, mix_ref, dout_ref, g3_ref, g2_ref, dh1_ref, dmix_ref, dg3_ref, dg2_ref):
        @pl.when(pl.program_id(0) == 0)
        def _():
            dg3_ref[...] = jnp.zeros_like(dg3_ref)
            dg2_ref[...] = jnp.zeros_like(dg2_ref)

        dv = _mdot(dp_ref[...], w_ref[...], 1, 1)
        h1 = h1_ref[...]
        dh1n, dg3 = _rms_bwd(h1, _rms(h1), g3_ref[...], dv)
        dh1 = dout_ref[...] + dh1n
        dh1_ref[...] = dh1
        mix = mix_ref[...]
        dmix, dg2 = _rms_bwd(mix, _rms(mix), g2_ref[...], dh1)
        dmix_ref[...] = dmix.astype(dmix_ref.dtype)
        dg3_ref[...] += dg3
        dg2_ref[...] += dg2

    return pl.pallas_call(
        body, name="dv_norms", grid=(s // t,),
        in_specs=[_row_spec(t, DFF), _full_spec((D, DFF), once=True), _row_spec(t, D), _row_spec(t, D),
                  _row_spec(t, D), _full_spec((1, D)), _full_spec((1, D))],
        out_specs=[_row_spec(t, D), _row_spec(t, D), _full_spec((1, D)), _full_spec((1, D))],
        out_shape=[_sds((s, D)), _sds((s, D), MXU), _sds((1, D)), _sds((1, D))],
        compiler_params=_cp("arbitrary"),
    )(dpre, w_up, h1, mix, dout, g3, g2)


def _lru_bwd(dmerged, ya, xr, h, proj5, gates, wab_t, lam, plan=None):
    s = xr.shape[0]
    t = min(128, s)
    n = s // t
    rs = functools.partial(_rev_spec, t, D, n)

    def body(dm_ref, ya_ref, xr_ref, h_ref, hp_ref, g_ref, ga_ref, gb_ref, gr_ref, gi_ref, a_ref, m_ref, wab_ref,
             lam_ref, dya_ref, d3_ref, dxr_ref, dp2_ref, dlam_ref, dba_ref, dbx_ref, gc_ref, af_ref, an_ref, us_ref,
             c_ref, gs_ref):
        i = pl.program_id(0)

        @pl.when(i == 0)
        def _():
            gc_ref[...] = jnp.zeros_like(gc_ref)
            af_ref[...] = jnp.zeros_like(af_ref)
            dlam_ref[...] = jnp.zeros_like(dlam_ref)
            dba_ref[...] = jnp.zeros_like(dba_ref)
            dbx_ref[...] = jnp.zeros_like(dbx_ref)

        xrv = xr_ref[...]
        gr, gi, a, mult = gr_ref[...], gi_ref[...], a_ref[...], m_ref[...]
        sp = _softplus(-lam_ref[...])
        inv_mult = 1.0 / mult
        hv = h_ref[...]
        dm = dm_ref[...]
        sa = _sig(ga_ref[...].astype(F32))
        sb = _sig(gb_ref[...].astype(F32))
        gel, dgel = _gelu_and_grad(g_ref[...].astype(F32))
        dya = dm * sa
        dya_ref[...] = dya
        dyb = dm * sb
        dybh = dyb * hv
        d3_ref[:, 0:D] = (dybh * dgel).astype(d3_ref.dtype)
        d3_ref[:, D:2 * D] = (dya * ya_ref[...] * (1.0 - sa)).astype(d3_ref.dtype)
        d3_ref[:, 2 * D:3 * D] = (dybh * gel * (1.0 - sb)).astype(d3_ref.dtype)
        row = lax.broadcasted_iota(jnp.int32, (t, D), 0)
        an = jnp.where(row == t - 1, af_ref[0:1, :], pltpu.roll(a, t - 1, 0))
        _blocked_scan(an, dyb * gel, gc_ref, an_ref, us_ref, c_ref, gs_ref, reverse=True)
        gfull = gs_ref[...]
        af_ref[0:1, :] = jnp.sum(jnp.where(row == 0, a, 0.0), axis=0, keepdims=True)
        hlast = jnp.where(i == n - 1, 0.0, hp_ref[7:8, :])
        hprev = jnp.where(row == 0, hlast, pltpu.roll(hv, 1, 0))
        gx = gfull * xrv
        dgi = gx * mult
        dla = a * (gfull * hprev - gx * gi * a * inv_mult)
        dgr = dla * (-LRU_C * sp)
        dsp = jnp.sum(dla * (-LRU_C * gr), axis=0, keepdims=True)
        dlam_ref[...] += dsp * (-_sig(-lam_ref[...]))
        dpr = dgr * gr * (1.0 - gr)
        dpi = dgi * gi * (1.0 - gi)
        dp2_ref[:, 0:D] = dpr.astype(dp2_ref.dtype)
        dp2_ref[:, D:2 * D] = dpi.astype(dp2_ref.dtype)
        dba_ref[...] += jnp.sum(dpr, axis=0, keepdims=True)
        dbx_ref[...] += jnp.sum(dpi, axis=0, keepdims=True)
        dxr_ref[...] = gfull * mult * gi + _mdot(dp2_ref[...], wab_ref[...])

    hp_spec = pl.BlockSpec((8, D), lambda i: (jnp.maximum((n - 1 - i) * (t // 8) - 1, 0), 0))
    wide = lambda c: pl.BlockSpec((t, c), lambda i: (n - 1 - i, 0))
    return _pcall(
        body, (dmerged, ya, xr, h, h, proj5, proj5, proj5, *gates, wab_t, lam), name="lru_bwd", grid=(n,),
        in_specs=[rs(), rs(), rs(), rs(), hp_spec, rs(COL_G), rs(COL_GA), rs(COL_GB), rs(), rs(), rs(), rs(),
                  _full_spec((2 * D, D), once=True), _full_spec((1, D))],
        out_specs=[rs(), wide(3 * D), rs(), wide(2 * D), _full_spec((1, D)), _full_spec((1, D)), _full_spec((1, D))],
        out_shape=[_sds((s, D)), _sds((s, 3 * D), MXU), _sds((s, D)), _sds((s, 2 * D), MXU), _sds((1, D)),
                   _sds((1, D)), _sds((1, D))],
        scratch_shapes=[pltpu.VMEM((8, D), F32), pltpu.VMEM((8, D), F32), pltpu.VMEM((t, D), F32),
                        pltpu.VMEM((t, D), F32), pltpu.VMEM((t // 8, D), F32), pltpu.VMEM((t, D), F32)],
        sem=("arbitrary",), plan=plan)


def _ssd_bwd(dya, y, proj5, xbc_c, dt, states, a_log, dskip_x, ssm_norm, expand, reduce_, plan=None):
    s = xbc_c.shape[0]
    nc = s // CH
    rv = functools.partial(_rev_spec, CH)

    def body(dya_ref, y_ref, z_ref, xc_ref, dt_ref, st_ref, alog_ref, dsk_ref, ng_ref, e_ref, et_ref, dz_ref,
             dxc_ref, ddt_ref, dng_ref, ddsk_ref, dalog_ref, dh_ref, at_ref, dtt_ref, dat_ref, ddtt_ref, dy_ref,
             yoffdy_ref, xbds_ref):
        @pl.when(pl.program_id(0) == 0)
        def _():
            dh_ref[...] = jnp.zeros_like(dh_ref)
            dng_ref[...] = jnp.zeros_like(dng_ref)
            ddsk_ref[...] = jnp.zeros_like(ddsk_ref)
            dalog_ref[...] = jnp.zeros_like(dalog_ref)

        cs = _ssd_chunk_setup(dt_ref, alog_ref, e_ref, at_ref, dtt_ref)
        lane, row = cs["lane"], cs["row"]
        et = et_ref[...]
        for g in range(NG):
            gs = slice(GW * g, GW * (g + 1))
            yv = y_ref[:, gs]
            zv = z_ref[:, gs].astype(F32)
            sz = _silu(zv)
            yg = yv * sz
            dyav = dya_ref[:, gs]
            dyg, dng = _rms_bwd(yg, _rms(yg), ng_ref[:, gs], dyav)
            dng_ref[:, gs] += dng
            dy_ref[:, gs] = dyg * sz
            dz_ref[:, gs] = (dyg * yv * _dsilu(zv)).astype(dz_ref.dtype)
        dyv = dy_ref[...]
        xs = xc_ref[:, 0:D]
        ddsk_ref[...] += jnp.sum(dyv * xs, axis=0, keepdims=True)
        dxc_ref[:, 0:D] = dyv * dsk_ref[...]
        dat_ref[...] = jnp.zeros_like(dat_ref)
        ddtt_ref[...] = jnp.zeros_like(ddtt_ref)
        hh = jnp.sum(dh_ref[...] * st_ref[0], axis=0, keepdims=True)
        deal = jnp.max(_xdot(jnp.broadcast_to(hh, (8, D)), et, 3), axis=0, keepdims=True)
        d_acum = jnp.zeros((CH, 128), F32)
        for g in range(NG):
            gs = slice(GW * g, GW * (g + 1))
            bs_ = slice(D + NS * g, D + NS * (g + 1))
            cs_ = slice(D + NG * NS + NS * g, D + NG * NS + NS * (g + 1))
            bg = xc_ref[:, bs_]
            cg = xc_ref[:, cs_]
            cb = _mdot(cg, bg, 1, 1)
            hg = st_ref[0, :, gs]
            dhg = dh_ref[:, gs]
            dyg_ = dy_ref[:, gs]
            xsg = xc_ref[:, gs]
            ea = cs["ea_x"][:, gs]
            wsx = cs["ws_x"][:, gs]
            dp = dyg_ * ea
            yoffdy_ref[:, gs] = dp * _mdot(cg, hg)
            dc = _mdot(dp, hg, 1, 1)
            dhprev = _mdot(cg, dp, 0, 0)
            bds = _mdot(bg, dhg)
            dxc_ref[:, gs] += wsx * bds
            xbds_ref[:, gs] = xsg * bds
            db = _mdot(xsg * wsx, dhg, 1, 1)
            dh_ref[:, gs] = dhprev + cs["eal_x"][:, gs] * dhg
            dcbs = jnp.zeros((CH, CH), F32)
            for j in range(4 * g, 4 * g + 4):
                ps = slice(128 * j, 128 * (j + 1))
                xp = xc_ref[:, ps]
                dyp = dy_ref[:, ps]
                dxacc = jnp.zeros((CH, 128), F32)
                for hf in range(2):
                    hd = 2 * j + hf
                    ld, rowdt = _head_decay(cs, at_ref, dtt_ref, hd)
                    hm = (lane >= HP) if hf else (lane < HP)
                    dym = jnp.where(hm, dyp, 0.0)
                    w = cb * ld * rowdt
                    dw = _mdot(dym, jnp.where(hm, xp, 0.0), 1, 1)
                    dxacc = dxacc + _mdot(w, dym, 0, 0)
                    nm = dw * w
                    ddtt_ref[hd:hd + 1, :] += jnp.sum(dw * cb * ld, axis=0, keepdims=True)
                    d_acum = d_acum + jnp.where(lane == hd, jnp.sum(nm, axis=1, keepdims=True), 0.0)
                    dat_ref[hd:hd + 1, :] -= jnp.sum(nm, axis=0, keepdims=True)
                    dcbs = dcbs + dw * ld * rowdt
                dxc_ref[:, ps] += dxacc
            dxc_ref[:, bs_] = db + _mdot(dcbs, cg, 0, 0)
            dxc_ref[:, cs_] = dc + _mdot(dcbs, bg)
        dws = _xdot(xbds_ref[...], et, 2)
        ws = cs["ws"]
        d_acum = d_acum - dws * ws + _xdot(yoffdy_ref[...], et, 2) + dat_ref[...].T
        d_alast = jnp.sum(dws * ws, axis=0, keepdims=True) + deal * cs["eal"]
        d_acum = d_acum + jnp.where(row == CH - 1, d_alast, 0.0)
        triu = row <= lane
        d_adt = _xdot(triu.astype(F32), d_acum, 3, split_b=True)
        ddt_ref[...] = dws * jnp.exp(cs["alast"] - cs["acum"]) + ddtt_ref[...].T + d_adt * cs["a"]
        dalog_ref[...] += jnp.sum(d_adt * cs["dtv"], axis=0, keepdims=True) * cs["a"]

    return _pcall(
        body, (dya, y, proj5, xbc_c, dt, states, a_log, dskip_x, ssm_norm, expand, reduce_), name="ssd_bwd",
        grid=(nc,),
        in_specs=[rv(D, nc), rv(D, nc), rv(D, nc, COL_Z), rv(XBC, nc), rv(128, nc),
                  pl.BlockSpec((1, NS, D), lambda i: (nc - 1 - i, 0, 0)), _full_spec((1, 128)), _full_spec((1, D)),
                  _full_spec((1, D)), _full_spec((128, D)), _full_spec((D, 128))],
        out_specs=[rv(D, nc), rv(XBC, nc), rv(128, nc), _full_spec((1, D)), _full_spec((1, D)),
                   _full_spec((1, 128))],
        out_shape=[_sds((s, D), MXU), _sds((s, XBC)), _sds((s, 128)), _sds((1, D)), _sds((1, D)), _sds((1, 128))],
        scratch_shapes=[pltpu.VMEM((NS, D), F32), pltpu.VMEM((128, 128), F32), pltpu.VMEM((128, 128), F32),
                        pltpu.VMEM((128, 128), F32), pltpu.VMEM((128, 128), F32), pltpu.VMEM((CH, D), F32),
                        pltpu.VMEM((CH, D), F32), pltpu.VMEM((CH, D), F32)],
        sem=("arbitrary",), plan=plan)


def _conv_bwd(dxbc_c, dsilu, dxr, ddt, xbc_raw, proj5, dt_raw, cw_s, cw_l, dt_bias, plan=None):
    s = xbc_raw.shape[0]
    t = min(256, s)
    n = s // t

    def body(dxc_ref, dsl_ref, dxr_ref, ddt_ref, xs_ref, xl_ref, dtr_ref, cws_ref, cwl_ref, dtb_ref, dxs_ref,
             dxl_ref, ddtr_ref, dcws_ref, dcbs_ref, dcwl_ref, dcbl_ref, ddtb_ref, ds_ref, dl_ref):
        @pl.when(pl.program_id(0) == 0)
        def _():
            ds_ref[t:t + 8, :] = jnp.zeros((8, XBC), F32)
            dl_ref[t:t + 8, :] = jnp.zeros((8, D), F32)
            for r in (dcws_ref, dcbs_ref, dcwl_ref, dcbl_ref, ddtb_ref):
                r[...] = jnp.zeros_like(r)

        ds_ref[0:t, :] = dxc_ref[...] * dsl_ref[...].astype(F32)
        dl_ref[0:t, :] = dxr_ref[...]

        def back(dbuf, x_ref, w_ref, dx_ref, dw_ref, db_ref):
            xv = x_ref[...].astype(F32)
            dpre = dbuf[0:t, :]
            dx = w_ref[3:4, :] * dpre
            dw_ref[3:4, :] += jnp.sum(dpre * xv, axis=0, keepdims=True)
            db_ref[...] += jnp.sum(dpre, axis=0, keepdims=True)
            for k in (1, 2, 3):
                ahead = dbuf[k:t + k, :]
                dx = dx + w_ref[3 - k:4 - k, :] * ahead
                dw_ref[3 - k:4 - k, :] += jnp.sum(ahead * xv, axis=0, keepdims=True)
            dx_ref[...] = dx.astype(dx_ref.dtype)
            dbuf[t:t + 8, :] = dbuf[0:8, :]

        back(ds_ref, xs_ref, cws_ref, dxs_ref, dcws_ref, dcbs_ref)
        back(dl_ref, xl_ref, cwl_ref, dxl_ref, dcwl_ref, dcbl_ref)
        ddtr = ddt_ref[...] * _sig(dtr_ref[...] + dtb_ref[...])
        ddtr_ref[...] = ddtr.astype(ddtr_ref.dtype)
        ddtb_ref[...] += jnp.sum(ddtr, axis=0, keepdims=True)

    rv = functools.partial(_rev_spec, t)
    return _pcall(
        body, (dxbc_c, dsilu, dxr, ddt, xbc_raw, proj5, dt_raw, cw_s, cw_l, dt_bias), name="conv_bwd", grid=(n,),
        in_specs=[rv(XBC, n), rv(XBC, n), rv(D, n), rv(128, n), rv(XBC, n), rv(D, n, COL_XL), rv(128, n),
                  _full_spec((4, XBC)), _full_spec((4, D)), _full_spec((1, 128))],
        out_specs=[rv(XBC, n), rv(D, n), rv(128, n), _full_spec((4, XBC)), _full_spec((1, XBC)), _full_spec((4, D)),
                   _full_spec((1, D)), _full_spec((1, 128))],
        out_shape=[_sds((s, XBC), MXU), _sds((s, D), MXU), _sds((s, 128), MXU), _sds((4, XBC)), _sds((1, XBC)),
                   _sds((4, D)), _sds((1, D)), _sds((1, 128))],
        scratch_shapes=[pltpu.VMEM((t + 8, XBC), F32), pltpu.VMEM((t + 8, D), F32)],
        sem=("arbitrary",), plan=plan)


def _du_norm(d3, dz, dxl, dxbc, ddtr, w5, wxbc, wdt, x, dh1, g1, plan=None):
    s = x.shape[0]
    t = min(ROWS_FUSED, s)

    def body(d3_ref, dz_ref, dxl_ref, dxbc_ref, ddtr_ref, w5_ref, wx_ref, wd_ref, x_ref, dh1_ref, g1_ref, dx_ref,
             dg1_ref):
        @pl.when(pl.program_id(0) == 0)
        def _():
            dg1_ref[...] = jnp.zeros_like(dg1_ref)

        du = (_mdot(dxbc_ref[...], wx_ref[...]) + _mdot(ddtr_ref[...], wd_ref[...])
              + _mdot(d3_ref[...], w5_ref[0:3 * D, :])
              + _mdot(dz_ref[...], w5_ref[COL_Z * D:(COL_Z + 1) * D, :])
              + _mdot(dxl_ref[...], w5_ref[COL_XL * D:(COL_XL + 1) * D, :]))
        xv = x_ref[...]
        dxn, dg1 = _rms_bwd(xv, _rms(xv), g1_ref[...], du)
        dx_ref[...] = dh1_ref[...] + dxn
        dg1_ref[...] += dg1

    return _pcall(
        body, (d3, dz, dxl, dxbc, ddtr, w5, wxbc, wdt, x, dh1, g1), name="du_norm", grid=(s // t,),
        in_specs=[_row_spec(t, 3 * D), _row_spec(t, D), _row_spec(t, D), _row_spec(t, XBC), _row_spec(t, 128),
                  _full_spec((5 * D, D), once=True), _full_spec((XBC, D), once=True),
                  _full_spec((128, D), once=True), _row_spec(t, D), _row_spec(t, D), _full_spec((1, D))],
        out_specs=[_row_spec(t, D), _full_spec((1, D))],
        out_shape=[_sds((s, D)), _sds((1, D))],
        sem=("arbitrary",), plan=plan)


def _adamw(w, g, m, v, name):
    r, c = w.shape
    t = r
    if r * c > 256 * 1024:
        t = next(cand for cand in (512, 256, 128, 64, 32, 16, 8) if r % cand == 0 and cand * c <= 512 * 1024)
    bc1 = 1.0 - ADAM_B1 ** ADAM_STEP
    bc2 = 1.0 - ADAM_B2 ** ADAM_STEP

    def body(w_ref, g_ref, m_ref, v_ref, d_ref, nm_ref, nv_ref):
        gv = g_ref[...]
        nm = ADAM_B1 * m_ref[...] + (1.0 - ADAM_B1) * gv
        nv = ADAM_B2 * v_ref[...] + (1.0 - ADAM_B2) * (gv * gv)
        nm_ref[...] = nm
        nv_ref[...] = nv
        d_ref[...] = -ADAM_LR * ((nm / bc1) / (jnp.sqrt(nv / bc2) + ADAM_EPS) + ADAM_WD * w_ref[...])

    spec = pl.BlockSpec((t, c), lambda i: (i, 0))
    return pl.pallas_call(
        body, name=name, grid=(r // t,), in_specs=[spec] * 4, out_specs=[spec] * 3,
        out_shape=[_sds((r, c))] * 3, compiler_params=_cp("parallel"),
    )(w, g, m, v)


def _half_blocks(shape, axis):
    r, c = shape
    if axis == 0:
        t = 256 if (r // 2) % 256 == 0 else 128
        nb = (r // 2) // t
        return (t, c), nb, (lambda i: (i, 0)), (lambda i: (i % nb, 0))
    nb = (c // 2) // 128
    return (r, 128), nb, (lambda i: (0, i)), (lambda i: (0, i % nb))


def _adamw_halves(w, g_mine, g_other, m, v, cidx, name, axis=0):
    r, c = w.shape
    blk, nb, whole, part = _half_blocks(w.shape, axis)
    bc1 = 1.0 - ADAM_B1 ** ADAM_STEP
    bc2 = 1.0 - ADAM_B2 ** ADAM_STEP

    def body(c_ref, w_ref, gm_ref, go_ref, m_ref, v_ref, g_ref, d_ref, nm_ref, nv_ref):
        mine = (pl.program_id(0) // nb) == c_ref[0]
        gv = jnp.where(mine, gm_ref[...], go_ref[...])
        g_ref[...] = gv
        nm = ADAM_B1 * m_ref[...] + (1.0 - ADAM_B1) * gv
        nv = ADAM_B2 * v_ref[...] + (1.0 - ADAM_B2) * (gv * gv)
        nm_ref[...] = nm
        nv_ref[...] = nv
        d_ref[...] = -ADAM_LR * ((nm / bc1) / (jnp.sqrt(nv / bc2) + ADAM_EPS) + ADAM_WD * w_ref[...])

    spec = pl.BlockSpec(blk, lambda i, c_ref: whole(i))
    half = pl.BlockSpec(blk, lambda i, c_ref: part(i))
    return pl.pallas_call(
        body, name=name,
        grid_spec=pltpu.PrefetchScalarGridSpec(num_scalar_prefetch=1, grid=(2 * nb,),
                                               in_specs=[spec, half, half, spec, spec], out_specs=[spec] * 4),
        out_shape=[_sds((r, c))] * 4, compiler_params=_cp("parallel"),
    )(cidx, w, g_mine, g_other, m, v)


def _block_diag(w):
    eye = jnp.eye(NH, dtype=w.dtype)
    return (w[:, :, None, :] * eye[:, None, :, None]).reshape(D, D)


def _diag_blocks(full):
    eye = jnp.eye(NH, dtype=full.dtype)
    return (full.reshape(NH, HP, NH, HP) * eye[:, None, :, None]).sum(axis=2)


def _pad_lanes(v, n=128):
    return jnp.pad(v, ((0, 0), (0, n - v.shape[1])))


def _local_step(x, target, p, dist=None):
    heads = jnp.arange(D, dtype=jnp.int32) // HP
    expand = (jnp.arange(128, dtype=jnp.int32)[:, None] == heads[None, :]).astype(F32)
    reduce_ = expand.T
    dskip_x = jnp.repeat(p["d_skip"], HP, axis=1)
    a_log = _pad_lanes(p["a_log"])
    dt_bias = _pad_lanes(p["dt_bias"])
    wab = jnp.concatenate([_block_diag(p["lru_wa"]), _block_diag(p["lru_wx"])], axis=1).astype(MXU)
    ba = p["lru_ba"].reshape(1, D)
    bx = p["lru_bx"].reshape(1, D)

    def hosted(key, fn):
        plan = dist.plan(key) if dist is not None else None
        if plan is None:
            return fn(plan=None)
        outs, got = fn(plan=plan)
        dist.done(key, got, p)
        return outs

    u = hosted("norm_u", functools.partial(_norm_cast, x, p["norm_mix_pre"], "norm_u"))
    w5, wxbc, wdt = p["w5"], p["wxbc"], p["wdt"]
    proj5 = hosted("proj5", functools.partial(_matmul, u, w5, name="proj5", tb=True, tm=1024, out_dtype=MXU))
    xbc_raw = _matmul(u, wxbc, name="proj_xbc", tb=True, tn=XBC)
    dt_raw = _matmul(u, wdt, name="proj_dt", tb=True)
    xbc_c, dsilu, xr, dt = _conv_fwd(xbc_raw, proj5, dt_raw, p["conv_ssm_w"], p["conv_ssm_b"], p["conv_lru_w"],
                                     p["conv_lru_b"], dt_bias)
    y, ya, states = _ssd_fwd(xbc_c, dt, proj5, a_log, dskip_x, p["ssm_norm"], expand)
    h, merged, *gates = hosted("lru_fwd", functools.partial(_lru_fwd, xr, proj5, ya, wab, ba, bx, p["lru_lambda"]))
    mix, h1, v = _out_proj(merged, p["w_out"], x, p["norm_mix_post"], p["norm_mlp_pre"])
    pre = _matmul(v, p["w_up"], name="up_proj", tm=1024, out_dtype=MXU)
    dout, dff, loss, dg4 = _down_loss(pre, p["w_down"], h1, target, p["norm_mlp_post"])

    dpre = _matmul(dff, p["w_down"], name="d_pre", tb=True, tm=1024, out_dtype=MXU,
                   epi=lambda r, pr: r * (2.0 * jnp.maximum(pr.astype(F32), 0.0)), epi_args=(pre,))
    g_w_down = _matmul(pre, dff, name="dw_down", ta=True, tm=1024, tn=1024, tk=TK_GRAD, a_fn=_relu2)
    dh1, dmix, dg3, dg2 = _dv_norms(dpre, p["w_up"], h1, mix, dout, p["norm_mlp_pre"], p["norm_mix_post"])
    g_w_up = _matmul(v, dpre, name="dw_up", ta=True, tm=1024, tn=1024, tk=TK_GRAD)
    dmerged = _matmul(dmix, p["w_out"], name="d_merged", tb=True)
    g_w_out = _matmul(merged, dmix, name="dw_out", ta=True, tm=1024, tn=1024, tk=TK_GRAD)
    if dist is not None:
        dist.early_grads(w_down=g_w_down, w_up=g_w_up, w_out=g_w_out)
    dya, d3, dxr, dp2, dlam, dba, dbx = hosted("lru_bwd", functools.partial(
        _lru_bwd, dmerged, ya, xr, h, proj5, gates, wab.T, p["lru_lambda"]))
    g_wab = _matmul(xr, dp2, name="dw_lru", ta=True, tm=1024, tn=1024, tk=TK_GRAD)
    g_wa, g_wx = _diag_blocks(g_wab[:, :D]), _diag_blocks(g_wab[:, D:])
    dz, dxbc_c, ddt, dng, ddsk, dalog = hosted("ssd_bwd", functools.partial(
        _ssd_bwd, dya, y, proj5, xbc_c, dt, states, a_log, dskip_x, p["ssm_norm"], expand, reduce_))
    (dxbc, dxl, ddtr, dcws, dcbs, dcwl, dcbl, ddtb) = hosted("conv_bwd", functools.partial(
        _conv_bwd, dxbc_c, dsilu, dxr, ddt, xbc_raw, proj5, dt_raw, p["conv_ssm_w"], p["conv_lru_w"], dt_bias))
    gw3 = _matmul(d3, u, name="dw_in_lru", ta=True, tm=1024, tn=1024, tk=TK_GRAD)
    gwz = _matmul(dz, u, name="dw_in_z", ta=True, tm=1024, tn=1024, tk=TK_GRAD)
    gwxl = _matmul(dxl, u, name="dw_in_xl", ta=True, tm=1024, tn=1024, tk=TK_GRAD)
    gwxbc = _matmul(dxbc, u, name="dw_in_xbc", ta=True, tm=XBC, tn=1024, tk=TK_GRAD)
    gwdt = _matmul(ddtr, u, name="dw_in_dt", ta=True, tm=128, tn=1024, tk=TK_GRAD)
    g_w_in_t = jnp.concatenate([gwz, gwxbc, gwdt[:NH], gw3[:D], gwxl, gw3[D:2 * D], gw3[2 * D:]], axis=0)
    grads = {
        "w_in_t": g_w_in_t, "conv_ssm_w": dcws, "conv_ssm_b": dcbs, "dt_bias": ddtb[:, :NH],
        "a_log": dalog[:, :NH], "d_skip": ddsk.reshape(NH, HP).sum(axis=1)[None, :], "ssm_norm": dng,
        "conv_lru_w": dcwl, "conv_lru_b": dcbl, "lru_wa": g_wa, "lru_ba": dba.reshape(NH, HP), "lru_wx": g_wx,
        "lru_bx": dbx.reshape(NH, HP), "lru_lambda": dlam, "w_out": g_w_out, "norm_mix_post": dg2,
        "norm_mlp_pre": dg3, "w_up": g_w_up, "w_down": g_w_down, "norm_mlp_post": dg4,
    }
    if dist is not None:
        dist.late_grads(grads, loss[0, 0])
    grad_x, grads["norm_mix_pre"] = hosted("du_norm", functools.partial(
        _du_norm, d3, dz, dxl, dxbc, ddtr, w5, wxbc, wdt, x, dh1, p["norm_mix_pre"]))
    return loss[0, 0], grad_x, grads


def _split_w_in_t(w_in_t):
    z, xbc, dtc, g, xl, ga, gb = jnp.split(w_in_t, [D, D + XBC, D + XBC + NH, 2 * D + XBC + NH,
                                                    3 * D + XBC + NH, 4 * D + XBC + NH], axis=0)
    return jnp.concatenate([g, ga, gb, z, xl], axis=0), xbc, jnp.pad(dtc, ((0, 128 - NH), (0, 0)))


COMM = BF16


def _place():
    x, y, c = lax.axis_index("x"), lax.axis_index("y"), lax.axis_index("c")
    chips = [(1 - x, y), (x, 1 - y), (1 - x, 1 - y)]
    return x, y, c, chips


def _remote(src, dst, send_sem, recv_sem, to):
    return pltpu.make_async_remote_copy(src_ref=src, dst_ref=dst, send_sem=send_sem, recv_sem=recv_sem, device_id=to,
                                        device_id_type=MESH)


def _gather_plan(big, small=(), axes=None):
    nb = len(big)
    arrs = list(big) + list(small)
    na = len(arrs)
    axes = list(axes or [0] * nb)

    def half(ref, a, k, which):
        h = arrs[a].shape[axes[a]] // 2
        cut = (pl.ds(which * h, h),) if axes[a] == 0 else (slice(None), pl.ds(which * h, h))
        return ref.at[cut] if k is None else ref.at[(k,) + cut]

    def direct(ins, outs, send, recv):
        x, y, c, chips = _place()
        k = 2 * x + y
        cps = []
        for a in range(na):
            src, dst = (half(ins[a], a, None, c), half(outs[a], a, k, c)) if a < nb else (ins[a], outs[a].at[k])
            cps += [_remote(src, dst, send.at[a, j], recv.at[a, j], (cx, cy, c)) for j, (cx, cy) in enumerate(chips)]
        return cps

    def passed(outs, send, recv):
        x, y, c, chips = _place()
        cps = []
        for j, (cx, cy) in enumerate(chips):
            for a in range(nb):
                got = half(outs[a], a, 2 * cx + cy, c)
                cps.append(_remote(got, got, send.at[a, 3 + j], recv.at[a, 3 + j], (x, y, 1 - c)))
        return cps

    def start(ins, outs, sems):
        for cp in direct(ins, outs, *sems[0]):
            cp.start()

    def mid(ins, outs, sems):
        send, recv = sems[0]
        _, _, c, chips = _place()
        fwd = passed(outs, send, recv)
        for j, (cx, cy) in enumerate(chips):
            kj = 2 * cx + cy
            for a in range(na):
                got = half(outs[a], a, kj, c) if a < nb else outs[a].at[kj]
                _remote(got, got, send.at[a, j], recv.at[a, j], (cx, cy, c)).wait_recv()
                if a < nb:
                    fwd[j * nb + a].start()

    def finish(ins, outs, sems):
        send, recv = sems[0]
        x, y, c, chips = _place()
        for j, (cx, cy) in enumerate(chips):
            for a in range(nb):
                got = half(outs[a], a, 2 * cx + cy, 1 - c)
                _remote(got, got, send.at[a, 3 + j], recv.at[a, 3 + j], (x, y, 1 - c)).wait_recv()
        for cp in direct(ins, outs, send, recv) + passed(outs, send, recv):
            cp.wait_send()

    return _Plan(arrs, [_sds((NCHIP,) + a.shape, a.dtype) for a in arrs], [(na, 6)], start, finish, mid)


def _own_shards(gathered, shards):
    kchip = 2 * lax.axis_index("x") + lax.axis_index("y")
    return [lax.dynamic_update_index_in_dim(o, a, kchip, 0) for o, a in zip(gathered, shards)]


def _swap_plan(ins, outs, sems, copies):
    def start(i, o, s):
        for cp in copies(i, o, *s[0]):
            cp.start()

    def finish(i, o, s):
        for cp in copies(i, o, *s[0]):
            cp.wait()

    return _Plan(ins, outs, [sems], start, finish)


def _half_shape(shape, axis):
    return tuple(d // 2 if i == axis else d for i, d in enumerate(shape))


def _pair_exchange_plan(gs, axis=1):
    def copies(ins, outs, send, recv):
        x, y, c, _ = _place()
        cps = []
        for a in range(len(gs)):
            h = ins[a].shape[axis] // 2
            theirs = pl.ds((1 - c) * h, h)
            src = ins[a].at[:, theirs] if axis == 1 else ins[a].at[:, :, theirs]
            cps.append(_remote(src, outs[a], send.at[a], recv.at[a], (x, y, 1 - c)))
        return cps

    return _swap_plan(gs, [_sds(_half_shape(g.shape, axis), g.dtype) for g in gs], (len(gs),), copies)


def _pair_add(g, got, cidx, name, axis=1):
    half = _half_shape(g.shape, axis)
    blk, nt, _, part = _half_blocks(g.shape[1:], axis - 1)

    def body(c_ref, g_ref, o_ref, p_ref, pc_ref):
        sm = g_ref[...] + o_ref[...]
        p_ref[...] = sm
        pc_ref[...] = sm.astype(pc_ref.dtype)

    def mine(k, i, c_ref):
        j = c_ref[0] * nt + i
        return (k, j, 0) if axis == 1 else (k, 0, j)

    spec = pl.BlockSpec((1,) + blk, lambda k, i, c_ref: (k,) + part(i))
    return pl.pallas_call(
        body, name=name,
        grid_spec=pltpu.PrefetchScalarGridSpec(
            num_scalar_prefetch=1, grid=(NCHIP, nt),
            in_specs=[pl.BlockSpec((1,) + blk, mine), spec], out_specs=[spec, spec]),
        out_shape=[_sds(half), _sds(half, COMM)],
        compiler_params=_cp("parallel", "parallel"),
    )(cidx, g, got)


def _chip_exchange_plan(ps):
    def copies(ins, outs, send, recv):
        _, _, c, chips = _place()
        return [_remote(ins[a].at[2 * cx + cy], outs[a].at[j], send.at[a, j], recv.at[a, j], (cx, cy, c))
                for a in range(len(ps)) for j, (cx, cy) in enumerate(chips)]

    return _swap_plan(ps, [_sds((NCHIP - 1,) + p.shape[1:], p.dtype) for p in ps], (len(ps), 3), copies)


def _shard_sum(p, got, kidx, name, axis=1):
    full = tuple(2 * d if i == axis - 1 else d for i, d in enumerate(p.shape[1:]))
    blk, nt, _, part = _half_blocks(full, axis - 1)

    def body(k_ref, p_ref, g_ref, o_ref):
        sm = p_ref[0]
        for j in range(NCHIP - 1):
            sm = sm + g_ref[j].astype(F32)
        o_ref[...] = sm

    return pl.pallas_call(
        body, name=name,
        grid_spec=pltpu.PrefetchScalarGridSpec(
            num_scalar_prefetch=1, grid=(nt,),
            in_specs=[pl.BlockSpec((1,) + blk, lambda i, k_ref: (k_ref[0],) + part(i)),
                      pl.BlockSpec((NCHIP - 1,) + blk, lambda i, k_ref: (0,) + part(i))],
            out_specs=pl.BlockSpec(blk, lambda i, k_ref: part(i))),
        out_shape=_sds(p.shape[1:]),
        compiler_params=_cp("parallel"),
    )(kidx, p, got)


def _pair_swap_plan(rs):
    def copies(ins, outs, send, recv):
        x, y, c, _ = _place()
        return [_remote(ins[a], outs[a], send.at[a], recv.at[a], (x, y, 1 - c)) for a in range(len(rs))]

    return _swap_plan(rs, [_sds(r.shape, r.dtype) for r in rs], (len(rs),), copies)


def _allgather8_plan(v):
    def pieces(ins, outs, send, recv):
        x, y, c, chips = _place()
        me, sibling = (x, y, c), (x, y, 1 - c)

        def copy(k, block, to, src=None):
            px, py, pc = block
            slot = outs[0].at[4 * px + 2 * py + pc]
            return _remote(slot if src is None else src, slot, send.at[k], recv.at[k], to)

        first = [copy(0, me, sibling, src=ins[0])] + [copy(1 + j, me, (*chip, c), src=ins[0])
                                                      for j, chip in enumerate(chips)]
        passed = [copy(4 + j, (*chip, c), sibling) for j, chip in enumerate(chips)]
        arrivals = [copy(1 + j, (*chip, c), me) for j, chip in enumerate(chips)]
        late = [copy(0, sibling, me)] + [copy(4 + j, (*chip, 1 - c), me) for j, chip in enumerate(chips)]
        return first, passed, arrivals, late

    def start(ins, outs, sems):
        for cp in pieces(ins, outs, *sems[0])[0]:
            cp.start()

    def mid(ins, outs, sems):
        _, passed, arrivals, _ = pieces(ins, outs, *sems[0])
        for got, fwd in zip(arrivals, passed):
            got.wait_recv()
            fwd.start()

    def finish(ins, outs, sems):
        first, passed, _, late = pieces(ins, outs, *sems[0])
        for got in late:
            got.wait_recv()
        for cp in first + passed:
            cp.wait_send()

    return _Plan([v], [_sds((8,) + v.shape, v.dtype)], [(7,)], start, finish, mid)


def _own_block(gathered, v):
    me = 4 * lax.axis_index("x") + 2 * lax.axis_index("y") + lax.axis_index("c")
    return lax.dynamic_update_index_in_dim(gathered, v, me, 0)


def _sum_devices(allv, name):
    _, r, _ = allv.shape

    def body(a_ref, o_ref):
        sm = a_ref[0]
        for d in range(1, 8):
            sm = sm + a_ref[d]
        o_ref[...] = sm

    return pl.pallas_call(
        body, name=name, grid=(1,), in_specs=[_full_spec((8, r, 128))], out_specs=_full_spec((r, 128)),
        out_shape=_sds((r, 128)), compiler_params=_cp("arbitrary"),
    )(allv)


def _pack(arrs):
    flat = jnp.concatenate([a.reshape(-1) for a in arrs])
    return jnp.pad(flat, (0, (-flat.shape[0]) % 1024)).reshape(-1, 128)


def _unpack(packed, shapes):
    flat, outs, off = packed.reshape(-1), [], 0
    for shp in shapes:
        n = math.prod(shp)
        outs.append(flat[off:off + n].reshape(shp))
        off += n
    return outs


BIG = ("w_in", "w_out", "w_up", "w_down")
CONV = ("conv_ssm_w", "conv_lru_w")
WEIGHTS = ("norm_mix_pre", "w_in", "conv_ssm_w", "conv_ssm_b", "dt_bias", "a_log", "d_skip", "ssm_norm", "conv_lru_w",
           "conv_lru_b", "lru_wa", "lru_ba", "lru_wx", "lru_bx", "lru_lambda", "w_out", "norm_mix_post",
           "norm_mlp_pre", "w_up", "w_down", "norm_mlp_post")
SMALL = tuple(n for n in WEIGHTS if n not in BIG and n not in CONV)
EARLY = ("w_down", "w_up", "w_out")


def _cat_cols(g):
    return jnp.concatenate([g[k] for k in range(NCHIP)], axis=1)


class _Dist:
    def __init__(self, shards, first, cidx, kidx):
        self.shards, self.first, self.cidx, self.kidx = shards, first, cidx, kidx
        self.halves = {}

    def early_grads(self, w_down, w_up, w_out):
        self.shard_major = [w_down.reshape(NCHIP, D, D), jnp.stack([w_up[:, D * k:D * (k + 1)] for k in range(NCHIP)]),
                            w_out.reshape(NCHIP, D // NCHIP, D)]

    def late_grads(self, grads, loss):
        g_in = grads["w_in_t"].reshape(NCHIP, W_IN_SHARD, D)
        got, = _run_plan(_pair_exchange_plan([g_in], axis=2), "grad_pair_exchange_w_in")
        self.p_in, self.pc_in = _pair_add(g_in, got, self.cidx, "grad_pair_add_w_in", axis=2)
        self.small_names = [n for n in SMALL + CONV if n != "norm_mix_pre"]
        self.small_shapes = [grads[n].shape for n in self.small_names] + [(1,)]
        self.packed_small = _pack([grads[n] for n in self.small_names] + [loss.reshape(1)])

    def plan(self, key):
        if key == "norm_u":
            return _gather_plan(self.first[:1], self.first[1:], axes=[1])
        if key == "proj5":
            return _gather_plan([self.shards["w_out"], self.shards["w_up"]])
        if key == "lru_fwd":
            return _gather_plan([self.shards["w_down"]])
        if key == "lru_bwd":
            return _pair_exchange_plan(self.shard_major)
        if key == "ssd_bwd":
            return _chip_exchange_plan([pc for _, pc in self.pair])
        if key == "conv_bwd":
            return _pair_swap_plan(self.mine)
        if key == "du_norm":
            return _merge_plans(_allgather8_plan(self.packed_small), _chip_exchange_plan([self.pc_in]))
        return None

    def done(self, key, got, p):
        if key == "norm_u":
            g_in, g_cs, g_cl = _own_shards(got, self.first)
            w5, wxbc, wdt = _split_w_in_t(g_in.reshape(W_IN_COLS, D))
            p.update(w5=w5, wxbc=wxbc, wdt=wdt, conv_ssm_w=_cat_cols(g_cs), conv_lru_w=_cat_cols(g_cl))
        elif key == "proj5":
            g_out, g_up = _own_shards(got, [self.shards["w_out"], self.shards["w_up"]])
            p.update(w_out=g_out.reshape(D, D), w_up=_cat_cols(g_up))
        elif key == "lru_fwd":
            g_down, = _own_shards(got, [self.shards["w_down"]])
            p.update(w_down=g_down.reshape(DFF, D))
        elif key == "lru_bwd":
            self.pair = [_pair_add(gs, o, self.cidx, f"grad_pair_add_{n}")
                         for gs, o, n in zip(self.shard_major, got, EARLY)]
        elif key == "ssd_bwd":
            self.mine = [_shard_sum(pf, o, self.kidx, f"grad_shard_sum_{n}")
                         for (pf, _), o, n in zip(self.pair, got, EARLY)]
        elif key == "conv_bwd":
            self.halves = {n: (mine, other) for n, mine, other in zip(EARLY, self.mine, got)}
        elif key == "du_norm":
            self.all_small, self.from_chips_in = got


def kernel(x, norm_mix_pre, w_in, conv_ssm_w, conv_ssm_b, dt_bias, a_log, d_skip, ssm_norm, conv_lru_w, conv_lru_b, lru_wa, lru_ba, lru_wx, lru_bx, lru_lambda, w_out, norm_mix_post, norm_mlp_pre, w_up, w_down, norm_mlp_post, loss_target, m_norm_mix_pre, m_w_in, m_conv_ssm_w, m_conv_ssm_b, m_dt_bias, m_a_log, m_d_skip, m_ssm_norm, m_conv_lru_w, m_conv_lru_b, m_lru_wa, m_lru_ba, m_lru_wx, m_lru_bx, m_lru_lambda, m_w_out, m_norm_mix_post, m_norm_mlp_pre, m_w_up, m_w_down, m_norm_mlp_post, v_norm_mix_pre, v_w_in, v_conv_ssm_w, v_conv_ssm_b, v_dt_bias, v_a_log, v_d_skip, v_ssm_norm, v_conv_lru_w, v_conv_lru_b, v_lru_wa, v_lru_ba, v_lru_wx, v_lru_bx, v_lru_lambda, v_w_out, v_norm_mix_post, v_norm_mlp_pre, v_w_up, v_w_down, v_norm_mlp_post):
    args = locals()
    w = {n: args[n][0] for n in WEIGHTS}
    m = {n: args["m_" + n][0] for n in WEIGHTS}
    v = {n: args["v_" + n][0] for n in WEIGHTS}
    cidx = lax.axis_index("c").astype(jnp.int32).reshape(1)
    kchip = 2 * lax.axis_index("x") + lax.axis_index("y")
    to_t = lambda a: jnp.transpose(a, (2, 0, 1)).reshape(W_IN_SHARD, D)
    from_t = lambda a: jnp.transpose(a.reshape(W_IN_SHARD, 1, D), (1, 2, 0))
    shards = {n: (to_t(w_in) if n == "w_in" else w[n]).astype(MXU) for n in BIG}
    dist = _Dist(shards, [shards["w_in"], w["conv_ssm_w"], w["conv_lru_w"]], cidx, kchip.astype(jnp.int32).reshape(1))
    p = {n: (w[n].reshape(1, -1) if w[n].ndim == 1 else w[n]) for n in SMALL}

    _, grad_x, g = _local_step(x[0], loss_target[0], p, dist)

    half_in = _shard_sum(dist.p_in, dist.from_chips_in, dist.kidx, "grad_shard_sum_w_in", axis=2)
    packed_g1 = _pack([g["norm_mix_pre"]])
    all_g1, other_in = _run_plan(_merge_plans(_allgather8_plan(packed_g1), _pair_swap_plan([half_in])),
                                 "grad_pair_swap_w_in")
    halves = dist.halves

    reduced = {}
    *summed, loss = _unpack(_sum_devices(_own_block(dist.all_small, dist.packed_small), "small_sum"),
                            dist.small_shapes)
    loss = loss.reshape(())
    g1, = _unpack(_sum_devices(_own_block(all_g1, packed_g1), "small_sum_norm_mix_pre"), [g["norm_mix_pre"].shape])
    for n, s in zip(dist.small_names + ["norm_mix_pre"], summed + [g1]):
        if n in CONV:
            width = w[n].shape[1]
            reduced[n] = lax.dynamic_slice_in_dim(s, kchip * width, width, axis=1)
        else:
            reduced[n] = s.reshape(w[n].shape)

    delta, new_m, new_v = {}, {}, {}
    for n in EARLY:
        mine, other = halves[n]
        reduced[n], delta[n], new_m[n], new_v[n] = _adamw_halves(w[n], mine, other, m[n], v[n], cidx, f"adamw_{n}")
    outs_t = _adamw_halves(to_t(w_in), half_in, other_in, to_t(m_w_in), to_t(v_w_in), cidx, "adamw_w_in", axis=1)
    for d, o in zip((reduced, delta, new_m, new_v), outs_t):
        d["w_in"] = from_t(o)[0]
    for n in CONV:
        delta[n], new_m[n], new_v[n] = _adamw(w[n], reduced[n], m[n], v[n], f"adamw_{n}")
    shapes = [w[n].shape for n in SMALL]
    packed = [_pack([d[n] for n in SMALL]) for d in (w, reduced, m, v)]
    for d, out in zip((delta, new_m, new_v), _adamw(*packed, "adamw_small")):
        d.update(zip(SMALL, _unpack(out, shapes)))

    lead = lambda d: [d[n][None] for n in WEIGHTS]
    return (loss, grad_x[None], *lead(reduced), *lead(delta), *lead(new_m), *lead(new_v))
```

```python
import functools
import math

import jax
import jax.numpy as jnp
from jax import lax
from jax.experimental import pallas as pl
from jax.experimental.pallas import tpu as pltpu

F32 = jnp.float32
BF16 = jnp.bfloat16
MXU = BF16

D = 1024
DFF = 4096
NH = 16
HP = 64
NG = 2
NS = 128
CH = 128
XBC = D + 2 * NG * NS
GW = D // NG
LRU_C = 8.0
EPS = 1e-6
NCHIP = 4
W_IN_COLS = 6672
W_IN_SHARD = W_IN_COLS // NCHIP

ADAM_LR = 0.001
ADAM_B1 = 0.9
ADAM_B2 = 0.999
ADAM_EPS = 1e-08
ADAM_WD = 0.01
ADAM_STEP = 10

VMEM_LIMIT = 56 * 1024 * 1024
TK_GRAD = 2048
MID_AT = 0.7
ROWS_FUSED = 512
COL_G, COL_GA, COL_GB, COL_Z, COL_XL = range(5)
MESH = pl.DeviceIdType.MESH


def _cp(*sem):
    return pltpu.CompilerParams(dimension_semantics=sem, vmem_limit_bytes=VMEM_LIMIT)


def _dot(a, b, ca=1, cb=0, prec=None):
    return lax.dot_general(a, b, (((ca,), (cb,)), ((), ())), precision=prec, preferred_element_type=F32)


def _mdot(a, b, ca=1, cb=0):
    return _dot(a.astype(MXU), b.astype(MXU), ca, cb)


def _bf16_parts(v, n):
    parts = []
    for i in range(n):
        p = v.astype(BF16)
        parts.append(p)
        if i < n - 1:
            v = v - p.astype(F32)
    return parts


def _xdot(a, b, passes, split_b=False):
    if split_b:
        a16 = a.astype(BF16)
        terms = [_dot(a16, p) for p in _bf16_parts(b, passes)]
    else:
        b16 = b.astype(BF16)
        terms = [_dot(p, b16) for p in _bf16_parts(a, passes)]
    return functools.reduce(lambda u, v: u + v, terms)


def _sig(x):
    return 0.5 * jnp.tanh(0.5 * x) + 0.5


def _silu(x):
    return x * _sig(x)


def _dsilu(x):
    s = _sig(x)
    return s * (1.0 + x * (1.0 - s))


def _softplus(x):
    e = jnp.exp(-jnp.abs(x))
    return jnp.maximum(x, 0.0) + jnp.where(e < 1e-4, e * (1.0 - 0.5 * e), jnp.log(1.0 + e))


_GELU_C = math.sqrt(2.0 / math.pi)


def _gelu(x):
    t = jnp.tanh(_GELU_C * (x + 0.044715 * x * x * x))
    return 0.5 * x * (1.0 + t)


def _gelu_and_grad(x):
    x2 = x * x
    t = jnp.tanh(_GELU_C * (x + 0.044715 * x * x2))
    half = 0.5 * (1.0 + t)
    return x * half, half + 0.5 * x * (1.0 - t * t) * _GELU_C * (1.0 + 3.0 * 0.044715 * x2)


def _one_minus_sq(a, la):
    x = 2.0 * la
    series = -x * (1.0 + x * (0.5 + x * (1.0 / 6.0)))
    return jnp.where(x > -0.01, series, 1.0 - a * a)


def _rms(x):
    return lax.rsqrt(jnp.mean(x * x, axis=-1, keepdims=True) + EPS)


def _rms_bwd(x, r, g, dy):
    xn = x * r
    dxh = dy * g
    m = jnp.mean(dxh * xn, axis=-1, keepdims=True)
    return r * (dxh - xn * m), jnp.sum(dy * xn, axis=0, keepdims=True)


def _row_spec(t, c, col=0):
    return pl.BlockSpec((t, c), lambda i: (i, col))


def _rev_spec(t, c, n, col=0):
    return pl.BlockSpec((t, c), lambda i: (n - 1 - i, col))


def _full_spec(shape, once=False):
    nd = len(shape)
    if once:
        return pl.BlockSpec(shape, lambda *_: (0,) * nd, pipeline_mode=pl.Buffered(1))
    return pl.BlockSpec(shape, lambda *_: (0,) * nd)


def _sds(shape, dtype=F32):
    return jax.ShapeDtypeStruct(shape, dtype)


ANY = pl.BlockSpec(memory_space=pl.ANY)


class _Plan:
    def __init__(self, ins, outs, sems, start, finish, mid=None):
        self.ins, self.outs, self.sems = list(ins), list(outs), list(sems)
        self.start, self.finish, self.mid = start, finish, mid or (lambda i, o, s: None)


def _merge_plans(*plans):
    def each(fn_name, ins, outs, sems):
        i = o = s = 0
        for p in plans:
            getattr(p, fn_name)(ins[i:i + len(p.ins)], outs[o:o + len(p.outs)], sems[s:s + len(p.sems)])
            i, o, s = i + len(p.ins), o + len(p.outs), s + len(p.sems)

    return _Plan([a for p in plans for a in p.ins], [a for p in plans for a in p.outs],
                 [a for p in plans for a in p.sems], functools.partial(each, "start"),
                 functools.partial(each, "finish"), functools.partial(each, "mid"))


def _pcall(body, args, *, name, grid, in_specs, out_specs, out_shape, sem, scratch_shapes=(), plan=None):
    single = not isinstance(out_shape, (list, tuple))
    out_specs = [out_specs] if single else list(out_specs)
    out_shape = [out_shape] if single else list(out_shape)
    if plan is None:
        outs = pl.pallas_call(body, name=name, grid=grid, in_specs=list(in_specs), out_specs=out_specs,
                              out_shape=out_shape, scratch_shapes=list(scratch_shapes),
                              compiler_params=_cp(*sem))(*args)
        return outs[0] if single else outs
    n_in, n_out, n_sc, ni, no = len(in_specs), len(out_shape), len(scratch_shapes), len(plan.ins), len(plan.outs)

    def hosted(*refs):
        b0 = n_in + ni
        b1 = b0 + n_out + no
        sem_refs = refs[b1 + n_sc:]
        sems = [(sem_refs[2 * q], sem_refs[2 * q + 1]) for q in range(len(plan.sems))]
        step = functools.reduce(lambda lin, ig: lin * ig[1] + ig[0],
                                [(pl.program_id(d), g) for d, g in enumerate(grid)], 0)
        total = math.prod(grid)

        @pl.when(step == 0)
        def _():
            plan.start(refs[n_in:b0], refs[b0 + n_out:b1], sems)

        body(*refs[:n_in], *refs[b0:b0 + n_out], *refs[b1:b1 + n_sc])

        @pl.when(step == min(int(MID_AT * total), total - 1))
        def _():
            plan.mid(refs[n_in:b0], refs[b0 + n_out:b1], sems)

        @pl.when(step == total - 1)
        def _():
            plan.finish(refs[n_in:b0], refs[b0 + n_out:b1], sems)

    dma = [pltpu.SemaphoreType.DMA(shape) for shape in plan.sems for _ in range(2)]
    outs = pl.pallas_call(hosted, name=name, grid=grid, in_specs=list(in_specs) + [ANY] * ni,
                          out_specs=out_specs + [ANY] * no, out_shape=out_shape + plan.outs,
                          scratch_shapes=list(scratch_shapes) + dma,
                          compiler_params=_cp(*("arbitrary",) * len(grid)))(*args, *plan.ins)
    return (outs[0] if single else outs[:n_out]), outs[n_out:]


def _run_plan(plan, name):
    ni, no = len(plan.ins), len(plan.outs)

    def body(*refs):
        sem_refs = refs[ni + no:]
        sems = [(sem_refs[2 * q], sem_refs[2 * q + 1]) for q in range(len(plan.sems))]
        plan.start(refs[:ni], refs[ni:ni + no], sems)
        plan.mid(refs[:ni], refs[ni:ni + no], sems)
        plan.finish(refs[:ni], refs[ni:ni + no], sems)

    return pl.pallas_call(
        body, name=name, in_specs=[ANY] * ni, out_specs=[ANY] * no, out_shape=plan.outs,
        scratch_shapes=[pltpu.SemaphoreType.DMA(shape) for shape in plan.sems for _ in range(2)],
    )(*plan.ins)


def _matmul(a, b, *, name, ta=False, tb=False, tm=512, tn=1024, tk=1024, out_dtype=F32, a_fn=None, epi=None,
            epi_args=(), plan=None):
    m, k = (a.shape[1], a.shape[0]) if ta else a.shape
    n = b.shape[0] if tb else b.shape[1]
    tm, tn, tk = min(tm, m), min(tn, n), min(tk, k)
    nk = k // tk
    a_spec = pl.BlockSpec((tk, tm), lambda i, j, kk: (kk, i)) if ta else pl.BlockSpec((tm, tk), lambda i, j, kk: (i, kk))
    b_spec = pl.BlockSpec((tn, tk), lambda i, j, kk: (j, kk)) if tb else pl.BlockSpec((tk, tn), lambda i, j, kk: (kk, j))
    e_specs = [pl.BlockSpec((tm, tn), lambda i, j, kk: (i, j)) for _ in epi_args]
    ne = len(epi_args)

    def body(a_ref, b_ref, *rest):
        e_refs, o_ref = rest[:ne], rest[ne]
        av = a_ref[...]
        if a_fn is not None:
            av = a_fn(av)
        part = _mdot(av, b_ref[...], 0 if ta else 1, 1 if tb else 0)

        def finish(r):
            if epi is not None:
                r = epi(r, *[e[...] for e in e_refs])
            o_ref[...] = r.astype(o_ref.dtype)

        if nk == 1:
            finish(part)
            return
        acc_ref = rest[ne + 1]
        kk = pl.program_id(2)

        @pl.when(kk == 0)
        def _():
            acc_ref[...] = part

        @pl.when(jnp.logical_and(kk > 0, kk < nk - 1))
        def _():
            acc_ref[...] += part

        @pl.when(kk == nk - 1)
        def _():
            finish(acc_ref[...] + part)

    return _pcall(
        body, (a, b, *epi_args), name=name, grid=(m // tm, n // tn, nk),
        in_specs=[a_spec, b_spec] + e_specs,
        out_specs=pl.BlockSpec((tm, tn), lambda i, j, kk: (i, j)),
        out_shape=_sds((m, n), out_dtype),
        scratch_shapes=[pltpu.VMEM((tm, tn), F32)] if nk > 1 else [],
        sem=("parallel", "parallel", "arbitrary"), plan=plan)


def _relu2(p):
    p = jnp.maximum(p, jnp.zeros((), p.dtype))
    return p * p


def _norm_cast(x, g, name, plan=None):
    s = x.shape[0]
    t = min(512, s)

    def body(x_ref, g_ref, o_ref):
        xv = x_ref[...]
        o_ref[...] = (xv * _rms(xv) * g_ref[...]).astype(o_ref.dtype)

    return _pcall(body, (x, g), name=name, grid=(s // t,), in_specs=[_row_spec(t, D), _full_spec((1, D))],
                  out_specs=_row_spec(t, D), out_shape=_sds((s, D), MXU), sem=("parallel",), plan=plan)


def _conv_fwd(xbc_raw, proj5, dt_raw, cw_s, cb_s, cw_l, cb_l, dt_bias):
    s = xbc_raw.shape[0]
    t = min(256, s)

    def body(xs_ref, xl_ref, dtr_ref, cws_ref, cbs_ref, cwl_ref, cbl_ref, dtb_ref, xc_ref, dsl_ref, xr_ref, dt_ref,
             bs_ref, bl_ref):
        @pl.when(pl.program_id(0) == 0)
        def _():
            bs_ref[0:8, :] = jnp.zeros((8, XBC), F32)
            bl_ref[0:8, :] = jnp.zeros((8, D), F32)

        bs_ref[8:t + 8, :] = xs_ref[...]
        bl_ref[8:t + 8, :] = xl_ref[...].astype(F32)

        def conv(buf, w_ref, b_ref):
            acc = b_ref[...] + w_ref[3:4, :] * buf[8:t + 8, :]
            for k in (1, 2, 3):
                acc = acc + w_ref[3 - k:4 - k, :] * buf[8 - k:t + 8 - k, :]
            return acc

        pre = conv(bs_ref, cws_ref, cbs_ref)
        sg = _sig(pre)
        xc_ref[...] = pre * sg
        dsl_ref[...] = (sg * (1.0 + pre * (1.0 - sg))).astype(dsl_ref.dtype)
        xr_ref[...] = conv(bl_ref, cwl_ref, cbl_ref)
        dt_ref[...] = _softplus(dtr_ref[...] + dtb_ref[...])
        bs_ref[0:8, :] = bs_ref[t:t + 8, :]
        bl_ref[0:8, :] = bl_ref[t:t + 8, :]

    return pl.pallas_call(
        body, name="conv_fwd", grid=(s // t,),
        in_specs=[_row_spec(t, XBC), _row_spec(t, D, COL_XL), _row_spec(t, 128), _full_spec((4, XBC)),
                  _full_spec((1, XBC)), _full_spec((4, D)), _full_spec((1, D)), _full_spec((1, 128))],
        out_specs=[_row_spec(t, XBC), _row_spec(t, XBC), _row_spec(t, D), _row_spec(t, 128)],
        out_shape=[_sds((s, XBC)), _sds((s, XBC), BF16), _sds((s, D)), _sds((s, 128))],
        scratch_shapes=[pltpu.VMEM((t + 8, XBC), F32), pltpu.VMEM((t + 8, D), F32)],
        compiler_params=_cp("arbitrary"),
    )(xbc_raw, proj5, dt_raw, cw_s, cb_s, cw_l, cb_l, dt_bias)


def _ssd_chunk_setup(dt_ref, alog_ref, e_ref, at_ref, dtt_ref):
    lane = lax.broadcasted_iota(jnp.int32, (CH, 128), 1)
    row = lax.broadcasted_iota(jnp.int32, (CH, 128), 0)
    lane1 = lax.broadcasted_iota(jnp.int32, (1, 128), 1)
    a = jnp.where(lane1 < NH, -jnp.exp(alog_ref[...]), 0.0)
    dtv = dt_ref[...]
    adt = dtv * a
    tril = row >= lane
    acum = _xdot(tril.astype(F32), adt, 3, split_b=True)
    alast = jnp.sum(adt, axis=0, keepdims=True)
    at_ref[...] = acum.T
    dtt_ref[...] = dtv.T
    e = e_ref[...]
    ea_x = _xdot(jnp.exp(acum), e, 2)
    ws = jnp.exp(alast - acum) * dtv
    ws_x = _xdot(ws, e, 2)
    eal = jnp.exp(alast)
    eal_x = jnp.max(_xdot(jnp.broadcast_to(eal, (8, 128)), e, 3), axis=0, keepdims=True)
    return dict(lane=lane, row=row, tril=tril, a=a, dtv=dtv, acum=acum, alast=alast, ea_x=ea_x, ws=ws, ws_x=ws_x,
                eal=eal, eal_x=eal_x)


def _head_decay(cs, at_ref, dtt_ref, h):
    col = jnp.sum(jnp.where(cs["lane"] == h, cs["acum"], 0.0), axis=1, keepdims=True)
    ld = jnp.where(cs["tril"], jnp.exp(jnp.minimum(col - at_ref[h:h + 1, :], 0.0)), 0.0)
    return ld, dtt_ref[h:h + 1, :]


def _ssd_fwd(xbc_c, dt, proj5, a_log, dskip_x, ssm_norm, expand):
    s = xbc_c.shape[0]
    nc = s // CH

    def body(xc_ref, dt_ref, z_ref, alog_ref, dsk_ref, ng_ref, e_ref, y_ref, ya_ref, st_ref, h_ref, at_ref, dtt_ref,
             yd_ref):
        @pl.when(pl.program_id(0) == 0)
        def _():
            h_ref[...] = jnp.zeros_like(h_ref)

        cs = _ssd_chunk_setup(dt_ref, alog_ref, e_ref, at_ref, dtt_ref)
        lane = cs["lane"]
        for g in range(NG):
            gs = slice(GW * g, GW * (g + 1))
            bg = xc_ref[:, D + NS * g:D + NS * (g + 1)]
            cg = xc_ref[:, D + NG * NS + NS * g:D + NG * NS + NS * (g + 1)]
            cb = _mdot(cg, bg, 1, 1)
            for j in range(4 * g, 4 * g + 4):
                ps = slice(128 * j, 128 * (j + 1))
                xp = xc_ref[:, ps]
                acc = jnp.zeros((CH, 128), F32)
                for hf in range(2):
                    ld, rowdt = _head_decay(cs, at_ref, dtt_ref, 2 * j + hf)
                    hm = (lane >= HP) if hf else (lane < HP)
                    acc = acc + _mdot(cb * ld * rowdt, jnp.where(hm, xp, 0.0))
                yd_ref[:, ps] = acc
            hg = h_ref[:, gs]
            yd_ref[:, gs] += _mdot(cg, hg) * cs["ea_x"][:, gs]
            st = _mdot(bg, xc_ref[:, gs] * cs["ws_x"][:, gs], 0, 0)
            st_ref[0, :, gs] = hg
            h_ref[:, gs] = cs["eal_x"][:, gs] * hg + st
        y = yd_ref[...] + dsk_ref[...] * xc_ref[:, 0:D]
        y_ref[...] = y
        yg = y * _silu(z_ref[...].astype(F32))
        for g in range(NG):
            gs = slice(GW * g, GW * (g + 1))
            seg = yg[:, gs]
            ya_ref[:, gs] = seg * _rms(seg) * ng_ref[:, gs]

    return pl.pallas_call(
        body, name="ssd_fwd", grid=(nc,),
        in_specs=[_row_spec(CH, XBC), _row_spec(CH, 128), _row_spec(CH, D, COL_Z), _full_spec((1, 128)),
                  _full_spec((1, D)), _full_spec((1, D)), _full_spec((128, D))],
        out_specs=[_row_spec(CH, D), _row_spec(CH, D), pl.BlockSpec((1, NS, D), lambda i: (i, 0, 0))],
        out_shape=[_sds((s, D)), _sds((s, D)), _sds((nc, NS, D))],
        scratch_shapes=[pltpu.VMEM((NS, D), F32), pltpu.VMEM((128, 128), F32), pltpu.VMEM((128, 128), F32),
                        pltpu.VMEM((CH, D), F32)],
        compiler_params=_cp("arbitrary"),
    )(xbc_c, dt, proj5, a_log, dskip_x, ssm_norm, expand)


def _lru_gates(xr, wab_ref, ba_ref, bx_ref, lam_ref):
    pre = _mdot(xr, wab_ref[...])
    gr = _sig(pre[:, 0:D] + ba_ref[...])
    gi = _sig(pre[:, D:2 * D] + bx_ref[...])
    sp = _softplus(-lam_ref[...])
    la = -LRU_C * gr * sp
    a = jnp.exp(la)
    oms = _one_minus_sq(a, la)
    inv_mult = lax.rsqrt(oms)
    return gr, gi, sp, a, oms * inv_mult, inv_mult


def _blocked_scan(a, u, carry_ref, a_ref, u_ref, c_ref, out_ref, reverse):
    t = a.shape[0]
    ns = t // 8

    def combine(av, uv, idx, n, sh):
        m = (idx < n - sh) if reverse else (idx >= sh)
        by = n - sh if reverse else sh
        return jnp.where(m, av * pltpu.roll(av, by, 0), av), jnp.where(m, uv + av * pltpu.roll(uv, by, 0), uv)

    row = lax.broadcasted_iota(jnp.int32, (t, D), 0)
    rin = jnp.bitwise_and(row, 7)
    for sh in (1, 2, 4):
        m = (rin < 8 - sh) if reverse else (rin >= sh)
        by = t - sh if reverse else sh
        a, u = jnp.where(m, a * pltpu.roll(a, by, 0), a), jnp.where(m, u + a * pltpu.roll(u, by, 0), u)
    a_ref[...] = a
    u_ref[...] = u
    edge = 0 if reverse else 7
    for j in range(ns):
        c_ref[j:j + 1, :] = a_ref[8 * j + edge:8 * j + edge + 1, :]
    at = c_ref[...]
    for j in range(ns):
        c_ref[j:j + 1, :] = u_ref[8 * j + edge:8 * j + edge + 1, :]
    ut = c_ref[...]
    srow = lax.broadcasted_iota(jnp.int32, (ns, D), 0)
    sh = 1
    while sh < ns:
        at, ut = combine(at, ut, srow, ns, sh)
        sh *= 2
    cv = carry_ref[0:1, :]
    ends = ut + at * cv
    last = 0 if reverse else ns - 1
    first = ns - 1 if reverse else 0
    c_ref[...] = jnp.where(srow == first, cv, pltpu.roll(ends, first if reverse else 1, 0))
    carry_ref[0:1, :] = jnp.sum(jnp.where(srow == last, ends, 0.0), axis=0, keepdims=True)
    for j in range(ns):
        sl = slice(8 * j, 8 * j + 8)
        out_ref[sl, :] = u_ref[sl, :] + a_ref[sl, :] * c_ref[j:j + 1, :]


def _lru_fwd(xr, proj5, ya, wab, ba, bx, lam, plan=None):
    s = xr.shape[0]
    t = min(256, s)

    def body(xr_ref, g_ref, ga_ref, gb_ref, ya_ref, wab_ref, ba_ref, bx_ref, lam_ref, h_ref, mg_ref, gr_ref,
             gi_ref, ao_ref, mo_ref, hc_ref, a_ref, u_ref, c_ref):
        @pl.when(pl.program_id(0) == 0)
        def _():
            hc_ref[...] = jnp.zeros_like(hc_ref)

        xrv = xr_ref[...]
        gr, gi, _, a, mult, _ = _lru_gates(xrv, wab_ref, ba_ref, bx_ref, lam_ref)
        gr_ref[...], gi_ref[...], ao_ref[...], mo_ref[...] = gr, gi, a, mult
        _blocked_scan(a, mult * gi * xrv, hc_ref, a_ref, u_ref, c_ref, h_ref, reverse=False)
        yb = h_ref[...] * _gelu(g_ref[...].astype(F32))
        mg_ref[...] = (_sig(ga_ref[...].astype(F32)) * ya_ref[...]
                       + _sig(gb_ref[...].astype(F32)) * yb).astype(mg_ref.dtype)

    return _pcall(
        body, (xr, proj5, proj5, proj5, ya, wab, ba, bx, lam), name="lru_fwd", grid=(s // t,),
        in_specs=[_row_spec(t, D), _row_spec(t, D, COL_G), _row_spec(t, D, COL_GA), _row_spec(t, D, COL_GB),
                  _row_spec(t, D), _full_spec((D, 2 * D), once=True), _full_spec((1, D)), _full_spec((1, D)),
                  _full_spec((1, D))],
        out_specs=[_row_spec(t, D)] * 6,
        out_shape=[_sds((s, D)), _sds((s, D), MXU)] + [_sds((s, D))] * 4,
        scratch_shapes=[pltpu.VMEM((8, D), F32), pltpu.VMEM((t, D), F32), pltpu.VMEM((t, D), F32),
                        pltpu.VMEM((t // 8, D), F32)],
        sem=("arbitrary",), plan=plan)


def _out_up_proj(merged, w_out, x, g2, g3, w_up):
    s = x.shape[0]
    t = min(ROWS_FUSED, s)

    def body(mg_ref, w_ref, x_ref, g2_ref, g3_ref, wu_ref, mix_ref, h1_ref, v_ref, pre_ref):
        mix = _mdot(mg_ref[...], w_ref[...])
        mix_ref[...] = mix
        h1 = x_ref[...] + mix * _rms(mix) * g2_ref[...]
        h1_ref[...] = h1
        v = (h1 * _rms(h1) * g3_ref[...]).astype(v_ref.dtype)
        v_ref[...] = v
        pre_ref[...] = _mdot(v, wu_ref[...]).astype(pre_ref.dtype)

    return pl.pallas_call(
        body, name="out_up_proj", grid=(s // t,),
        in_specs=[_row_spec(t, D), _full_spec((D, D), once=True), _row_spec(t, D), _full_spec((1, D)),
                  _full_spec((1, D)), _full_spec((D, DFF), once=True)],
        out_specs=[_row_spec(t, D), _row_spec(t, D), _row_spec(t, D), _row_spec(t, DFF)],
        out_shape=[_sds((s, D)), _sds((s, D)), _sds((s, D), MXU), _sds((s, DFF), MXU)],
        compiler_params=_cp("parallel"),
    )(merged, w_out, x, g2, g3, w_up)


def _down_loss(pre, w_down, h1, target, g4):
    s = pre.shape[0]
    t = min(ROWS_FUSED, s)

    def body(pre_ref, w_ref, h1_ref, tg_ref, g4_ref, dout_ref, dff_ref, loss_ref, dg4_ref):
        @pl.when(pl.program_id(0) == 0)
        def _():
            loss_ref[...] = jnp.zeros_like(loss_ref)
            dg4_ref[...] = jnp.zeros_like(dg4_ref)

        ff = _mdot(_relu2(pre_ref[...]), w_ref[...])
        r4 = _rms(ff)
        g4v = g4_ref[...]
        diff = h1_ref[...] + ff * r4 * g4v - tg_ref[...]
        sq = jnp.sum(jnp.sum(diff * diff, axis=1, keepdims=True), axis=0, keepdims=True)
        loss_ref[...] += (0.5 / D) * sq
        dout = diff * (1.0 / D)
        dout_ref[...] = dout
        dff, dg = _rms_bwd(ff, r4, g4v, dout)
        dff_ref[...] = dff.astype(dff_ref.dtype)
        dg4_ref[...] += dg

    return pl.pallas_call(
        body, name="down_loss", grid=(s // t,),
        in_specs=[_row_spec(t, DFF), _full_spec((DFF, D), once=True), _row_spec(t, D), _row_spec(t, D),
                  _full_spec((1, D))],
        out_specs=[_row_spec(t, D), _row_spec(t, D), _full_spec((1, 128)), _full_spec((1, D))],
        out_shape=[_sds((s, D)), _sds((s, D), MXU), _sds((1, 128)), _sds((1, D))],
        compiler_params=_cp("arbitrary"),
    )(pre, w_down, h1, target, g4)


def _dv_norms(dpre, w_up, h1, mix, dout, g3, g2, w_out):
    s = h1.shape[0]
    t = min(ROWS_FUSED, s)

    def body(dp_ref, w_ref, h1_ref, mix_ref, dout_ref, g3_ref, g2_ref, wo_ref, dh1_ref, dmix_ref, dmg_ref, dg3_ref,
             dg2_ref):
        @pl.when(pl.program_id(0) == 0)
        def _():
            dg3_ref[...] = jnp.zeros_like(dg3_ref)
            dg2_ref[...] = jnp.zeros_like(dg2_ref)

        dv = _mdot(dp_ref[...], w_ref[...], 1, 1)
        h1 = h1_ref[...]
        dh1n, dg3 = _rms_bwd(h1, _rms(h1), g3_ref[...], dv)
        dh1 = dout_ref[...] + dh1n
        dh1_ref[...] = dh1
        mix = mix_ref[...]
        dmix, dg2 = _rms_bwd(mix, _rms(mix), g2_ref[...], dh1)
        dmix = dmix.astype(dmix_ref.dtype)
        dmix_ref[...] = dmix
        dmg_ref[...] = _mdot(dmix, wo_ref[...], 1, 1)
        dg3_ref[...] += dg3
        dg2_ref[...] += dg2

    return pl.pallas_call(
        body, name="dv_norms", grid=(s // t,),
        in_specs=[_row_spec(t, DFF), _full_spec((D, DFF), once=True), _row_spec(t, D), _row_spec(t, D),
                  _row_spec(t, D), _full_spec((1, D)), _full_spec((1, D)), _full_spec((D, D), once=True)],
        out_specs=[_row_spec(t, D), _row_spec(t, D), _row_spec(t, D), _full_spec((1, D)), _full_spec((1, D))],
        out_shape=[_sds((s, D)), _sds((s, D), MXU), _sds((s, D)), _sds((1, D)), _sds((1, D))],
        compiler_params=_cp("arbitrary"),
    )(dpre, w_up, h1, mix, dout, g3, g2, w_out)


def _lru_bwd(dmerged, ya, xr, h, proj5, gates, wab_t, lam, plan=None):
    s = xr.shape[0]
    t = min(128, s)
    n = s // t
    rs = functools.partial(_rev_spec, t, D, n)

    def body(dm_ref, ya_ref, xr_ref, h_ref, hp_ref, g_ref, ga_ref, gb_ref, gr_ref, gi_ref, a_ref, m_ref, wab_ref,
             lam_ref, dya_ref, d3_ref, dxr_ref, dp2_ref, dlam_ref, dba_ref, dbx_ref, gc_ref, af_ref, an_ref, us_ref,
             c_ref, gs_ref):
        i = pl.program_id(0)

        @pl.when(i == 0)
        def _():
            gc_ref[...] = jnp.zeros_like(gc_ref)
            af_ref[...] = jnp.zeros_like(af_ref)
            dlam_ref[...] = jnp.zeros_like(dlam_ref)
            dba_ref[...] = jnp.zeros_like(dba_ref)
            dbx_ref[...] = jnp.zeros_like(dbx_ref)

        xrv = xr_ref[...]
        gr, gi, a, mult = gr_ref[...], gi_ref[...], a_ref[...], m_ref[...]
        sp = _softplus(-lam_ref[...])
        inv_mult = 1.0 / mult
        hv = h_ref[...]
        dm = dm_ref[...]
        sa = _sig(ga_ref[...].astype(F32))
        sb = _sig(gb_ref[...].astype(F32))
        gel, dgel = _gelu_and_grad(g_ref[...].astype(F32))
        dya = dm * sa
        dya_ref[...] = dya
        dyb = dm * sb
        dybh = dyb * hv
        d3_ref[:, 0:D] = (dybh * dgel).astype(d3_ref.dtype)
        d3_ref[:, D:2 * D] = (dya * ya_ref[...] * (1.0 - sa)).astype(d3_ref.dtype)
        d3_ref[:, 2 * D:3 * D] = (dybh * gel * (1.0 - sb)).astype(d3_ref.dtype)
        row = lax.broadcasted_iota(jnp.int32, (t, D), 0)
        an = jnp.where(row == t - 1, af_ref[0:1, :], pltpu.roll(a, t - 1, 0))
        _blocked_scan(an, dyb * gel, gc_ref, an_ref, us_ref, c_ref, gs_ref, reverse=True)
        gfull = gs_ref[...]
        af_ref[0:1, :] = jnp.sum(jnp.where(row == 0, a, 0.0), axis=0, keepdims=True)
        hlast = jnp.where(i == n - 1, 0.0, hp_ref[7:8, :])
        hprev = jnp.where(row == 0, hlast, pltpu.roll(hv, 1, 0))
        gx = gfull * xrv
        dgi = gx * mult
        dla = a * (gfull * hprev - gx * gi * a * inv_mult)
        dgr = dla * (-LRU_C * sp)
        dsp = jnp.sum(dla * (-LRU_C * gr), axis=0, keepdims=True)
        dlam_ref[...] += dsp * (-_sig(-lam_ref[...]))
        dpr = dgr * gr * (1.0 - gr)
        dpi = dgi * gi * (1.0 - gi)
        dp2_ref[:, 0:D] = dpr.astype(dp2_ref.dtype)
        dp2_ref[:, D:2 * D] = dpi.astype(dp2_ref.dtype)
        dba_ref[...] += jnp.sum(dpr, axis=0, keepdims=True)
        dbx_ref[...] += jnp.sum(dpi, axis=0, keepdims=True)
        dxr_ref[...] = gfull * mult * gi + _mdot(dp2_ref[...], wab_ref[...])

    hp_spec = pl.BlockSpec((8, D), lambda i: (jnp.maximum((n - 1 - i) * (t // 8) - 1, 0), 0))
    wide = lambda c: pl.BlockSpec((t, c), lambda i: (n - 1 - i, 0))
    return _pcall(
        body, (dmerged, ya, xr, h, h, proj5, proj5, proj5, *gates, wab_t, lam), name="lru_bwd", grid=(n,),
        in_specs=[rs(), rs(), rs(), rs(), hp_spec, rs(COL_G), rs(COL_GA), rs(COL_GB), rs(), rs(), rs(), rs(),
                  _full_spec((2 * D, D), once=True), _full_spec((1, D))],
        out_specs=[rs(), wide(3 * D), rs(), wide(2 * D), _full_spec((1, D)), _full_spec((1, D)), _full_spec((1, D))],
        out_shape=[_sds((s, D)), _sds((s, 3 * D), MXU), _sds((s, D)), _sds((s, 2 * D), MXU), _sds((1, D)),
                   _sds((1, D)), _sds((1, D))],
        scratch_shapes=[pltpu.VMEM((8, D), F32), pltpu.VMEM((8, D), F32), pltpu.VMEM((t, D), F32),
                        pltpu.VMEM((t, D), F32), pltpu.VMEM((t // 8, D), F32), pltpu.VMEM((t, D), F32)],
        sem=("arbitrary",), plan=plan)


def _ssd_bwd(dya, y, proj5, xbc_c, dt, states, a_log, dskip_x, ssm_norm, expand, reduce_, plan=None):
    s = xbc_c.shape[0]
    nc = s // CH
    rv = functools.partial(_rev_spec, CH)

    def body(dya_ref, y_ref, z_ref, xc_ref, dt_ref, st_ref, alog_ref, dsk_ref, ng_ref, e_ref, et_ref, dz_ref,
             dxc_ref, ddt_ref, dng_ref, ddsk_ref, dalog_ref, dh_ref, at_ref, dtt_ref, dat_ref, ddtt_ref, dy_ref,
             yoffdy_ref, xbds_ref):
        @pl.when(pl.program_id(0) == 0)
        def _():
            dh_ref[...] = jnp.zeros_like(dh_ref)
            dng_ref[...] = jnp.zeros_like(dng_ref)
            ddsk_ref[...] = jnp.zeros_like(ddsk_ref)
            dalog_ref[...] = jnp.zeros_like(dalog_ref)

        cs = _ssd_chunk_setup(dt_ref, alog_ref, e_ref, at_ref, dtt_ref)
        lane, row = cs["lane"], cs["row"]
        et = et_ref[...]
        for g in range(NG):
            gs = slice(GW * g, GW * (g + 1))
            yv = y_ref[:, gs]
            zv = z_ref[:, gs].astype(F32)
            sz = _silu(zv)
            yg = yv * sz
            dyav = dya_ref[:, gs]
            dyg, dng = _rms_bwd(yg, _rms(yg), ng_ref[:, gs], dyav)
            dng_ref[:, gs] += dng
            dy_ref[:, gs] = dyg * sz
            dz_ref[:, gs] = (dyg * yv * _dsilu(zv)).astype(dz_ref.dtype)
        dyv = dy_ref[...]
        xs = xc_ref[:, 0:D]
        ddsk_ref[...] += jnp.sum(dyv * xs, axis=0, keepdims=True)
        dxc_ref[:, 0:D] = dyv * dsk_ref[...]
        dat_ref[...] = jnp.zeros_like(dat_ref)
        ddtt_ref[...] = jnp.zeros_like(ddtt_ref)
        hh = jnp.sum(dh_ref[...] * st_ref[0], axis=0, keepdims=True)
        deal = jnp.max(_xdot(jnp.broadcast_to(hh, (8, D)), et, 3), axis=0, keepdims=True)
        d_acum = jnp.zeros((CH, 128), F32)
        for g in range(NG):
            gs = slice(GW * g, GW * (g + 1))
            bs_ = slice(D + NS * g, D + NS * (g + 1))
            cs_ = slice(D + NG * NS + NS * g, D + NG * NS + NS * (g + 1))
            bg = xc_ref[:, bs_]
            cg = xc_ref[:, cs_]
            cb = _mdot(cg, bg, 1, 1)
            hg = st_ref[0, :, gs]
            dhg = dh_ref[:, gs]
            dyg_ = dy_ref[:, gs]
            xsg = xc_ref[:, gs]
            ea = cs["ea_x"][:, gs]
            wsx = cs["ws_x"][:, gs]
            dp = dyg_ * ea
            yoffdy_ref[:, gs] = dp * _mdot(cg, hg)
            dc = _mdot(dp, hg, 1, 1)
            dhprev = _mdot(cg, dp, 0, 0)
            bds = _mdot(bg, dhg)
            dxc_ref[:, gs] += wsx * bds
            xbds_ref[:, gs] = xsg * bds
            db = _mdot(xsg * wsx, dhg, 1, 1)
            dh_ref[:, gs] = dhprev + cs["eal_x"][:, gs] * dhg
            dcbs = jnp.zeros((CH, CH), F32)
            for j in range(4 * g, 4 * g + 4):
                ps = slice(128 * j, 128 * (j + 1))
                xp = xc_ref[:, ps]
                dyp = dy_ref[:, ps]
                dxacc = jnp.zeros((CH, 128), F32)
                for hf in range(2):
                    hd = 2 * j + hf
                    ld, rowdt = _head_decay(cs, at_ref, dtt_ref, hd)
                    hm = (lane >= HP) if hf else (lane < HP)
                    dym = jnp.where(hm, dyp, 0.0)
                    w = cb * ld * rowdt
                    dw = _mdot(dym, jnp.where(hm, xp, 0.0), 1, 1)
                    dxacc = dxacc + _mdot(w, dym, 0, 0)
                    nm = dw * w
                    ddtt_ref[hd:hd + 1, :] += jnp.sum(dw * cb * ld, axis=0, keepdims=True)
                    d_acum = d_acum + jnp.where(lane == hd, jnp.sum(nm, axis=1, keepdims=True), 0.0)
                    dat_ref[hd:hd + 1, :] -= jnp.sum(nm, axis=0, keepdims=True)
                    dcbs = dcbs + dw * ld * rowdt
                dxc_ref[:, ps] += dxacc
            dxc_ref[:, bs_] = db + _mdot(dcbs, cg, 0, 0)
            dxc_ref[:, cs_] = dc + _mdot(dcbs, bg)
        dws = _xdot(xbds_ref[...], et, 2)
        ws = cs["ws"]
        d_acum = d_acum - dws * ws + _xdot(yoffdy_ref[...], et, 2) + dat_ref[...].T
        d_alast = jnp.sum(dws * ws, axis=0, keepdims=True) + deal * cs["eal"]
        d_acum = d_acum + jnp.where(row == CH - 1, d_alast, 0.0)
        triu = row <= lane
        d_adt = _xdot(triu.astype(F32), d_acum, 3, split_b=True)
        ddt_ref[...] = dws * jnp.exp(cs["alast"] - cs["acum"]) + ddtt_ref[...].T + d_adt * cs["a"]
        dalog_ref[...] += jnp.sum(d_adt * cs["dtv"], axis=0, keepdims=True) * cs["a"]

    return _pcall(
        body, (dya, y, proj5, xbc_c, dt, states, a_log, dskip_x, ssm_norm, expand, reduce_), name="ssd_bwd",
        grid=(nc,),
        in_specs=[rv(D, nc), rv(D, nc), rv(D, nc, COL_Z), rv(XBC, nc), rv(128, nc),
                  pl.BlockSpec((1, NS, D), lambda i: (nc - 1 - i, 0, 0)), _full_spec((1, 128)), _full_spec((1, D)),
                  _full_spec((1, D)), _full_spec((128, D)), _full_spec((D, 128))],
        out_specs=[rv(D, nc), rv(XBC, nc), rv(128, nc), _full_spec((1, D)), _full_spec((1, D)),
                   _full_spec((1, 128))],
        out_shape=[_sds((s, D), MXU), _sds((s, XBC)), _sds((s, 128)), _sds((1, D)), _sds((1, D)), _sds((1, 128))],
        scratch_shapes=[pltpu.VMEM((NS, D), F32), pltpu.VMEM((128, 128), F32), pltpu.VMEM((128, 128), F32),
                        pltpu.VMEM((128, 128), F32), pltpu.VMEM((128, 128), F32), pltpu.VMEM((CH, D), F32),
                        pltpu.VMEM((CH, D), F32), pltpu.VMEM((CH, D), F32)],
        sem=("arbitrary",), plan=plan)


def _conv_bwd(dxbc_c, dsilu, dxr, ddt, xbc_raw, proj5, dt_raw, cw_s, cw_l, dt_bias, plan=None):
    s = xbc_raw.shape[0]
    t = min(256, s)
    n = s // t

    def body(dxc_ref, dsl_ref, dxr_ref, ddt_ref, xs_ref, xl_ref, dtr_ref, cws_ref, cwl_ref, dtb_ref, dxs_ref,
             dxl_ref, ddtr_ref, dcws_ref, dcbs_ref, dcwl_ref, dcbl_ref, ddtb_ref, ds_ref, dl_ref):
        @pl.when(pl.program_id(0) == 0)
        def _():
            ds_ref[t:t + 8, :] = jnp.zeros((8, XBC), F32)
            dl_ref[t:t + 8, :] = jnp.zeros((8, D), F32)
            for r in (dcws_ref, dcbs_ref, dcwl_ref, dcbl_ref, ddtb_ref):
                r[...] = jnp.zeros_like(r)

        ds_ref[0:t, :] = dxc_ref[...] * dsl_ref[...].astype(F32)
        dl_ref[0:t, :] = dxr_ref[...]

        def back(dbuf, x_ref, w_ref, dx_ref, dw_ref, db_ref):
            xv = x_ref[...].astype(F32)
            dpre = dbuf[0:t, :]
            dx = w_ref[3:4, :] * dpre
            dw_ref[3:4, :] += jnp.sum(dpre * xv, axis=0, keepdims=True)
            db_ref[...] += jnp.sum(dpre, axis=0, keepdims=True)
            for k in (1, 2, 3):
                ahead = dbuf[k:t + k, :]
                dx = dx + w_ref[3 - k:4 - k, :] * ahead
                dw_ref[3 - k:4 - k, :] += jnp.sum(ahead * xv, axis=0, keepdims=True)
            dx_ref[...] = dx.astype(dx_ref.dtype)
            dbuf[t:t + 8, :] = dbuf[0:8, :]

        back(ds_ref, xs_ref, cws_ref, dxs_ref, dcws_ref, dcbs_ref)
        back(dl_ref, xl_ref, cwl_ref, dxl_ref, dcwl_ref, dcbl_ref)
        ddtr = ddt_ref[...] * _sig(dtr_ref[...] + dtb_ref[...])
        ddtr_ref[...] = ddtr.astype(ddtr_ref.dtype)
        ddtb_ref[...] += jnp.sum(ddtr, axis=0, keepdims=True)

    rv = functools.partial(_rev_spec, t)
    return _pcall(
        body, (dxbc_c, dsilu, dxr, ddt, xbc_raw, proj5, dt_raw, cw_s, cw_l, dt_bias), name="conv_bwd", grid=(n,),
        in_specs=[rv(XBC, n), rv(XBC, n), rv(D, n), rv(128, n), rv(XBC, n), rv(D, n, COL_XL), rv(128, n),
                  _full_spec((4, XBC)), _full_spec((4, D)), _full_spec((1, 128))],
        out_specs=[rv(XBC, n), rv(D, n), rv(128, n), _full_spec((4, XBC)), _full_spec((1, XBC)), _full_spec((4, D)),
                   _full_spec((1, D)), _full_spec((1, 128))],
        out_shape=[_sds((s, XBC), MXU), _sds((s, D), MXU), _sds((s, 128), MXU), _sds((4, XBC)), _sds((1, XBC)),
                   _sds((4, D)), _sds((1, D)), _sds((1, 128))],
        scratch_shapes=[pltpu.VMEM((t + 8, XBC), F32), pltpu.VMEM((t + 8, D), F32)],
        sem=("arbitrary",), plan=plan)


def _du_norm(d3, dz, dxl, dxbc, ddtr, w5, wxbc, wdt, x, dh1, g1, plan=None):
    s = x.shape[0]
    t = min(ROWS_FUSED, s)

    def body(d3_ref, dz_ref, dxl_ref, dxbc_ref, ddtr_ref, w5_ref, wx_ref, wd_ref, x_ref, dh1_ref, g1_ref, dx_ref,
             dg1_ref):
        @pl.when(pl.program_id(0) == 0)
        def _():
            dg1_ref[...] = jnp.zeros_like(dg1_ref)

        du = (_mdot(dxbc_ref[...], wx_ref[...]) + _mdot(ddtr_ref[...], wd_ref[...])
              + _mdot(d3_ref[...], w5_ref[0:3 * D, :])
              + _mdot(dz_ref[...], w5_ref[COL_Z * D:(COL_Z + 1) * D, :])
              + _mdot(dxl_ref[...], w5_ref[COL_XL * D:(COL_XL + 1) * D, :]))
        xv = x_ref[...]
        dxn, dg1 = _rms_bwd(xv, _rms(xv), g1_ref[...], du)
        dx_ref[...] = dh1_ref[...] + dxn
        dg1_ref[...] += dg1

    return _pcall(
        body, (d3, dz, dxl, dxbc, ddtr, w5, wxbc, wdt, x, dh1, g1), name="du_norm", grid=(s // t,),
        in_specs=[_row_spec(t, 3 * D), _row_spec(t, D), _row_spec(t, D), _row_spec(t, XBC), _row_spec(t, 128),
                  _full_spec((5 * D, D), once=True), _full_spec((XBC, D), once=True),
                  _full_spec((128, D), once=True), _row_spec(t, D), _row_spec(t, D), _full_spec((1, D))],
        out_specs=[_row_spec(t, D), _full_spec((1, D))],
        out_shape=[_sds((s, D)), _sds((1, D))],
        sem=("arbitrary",), plan=plan)


def _adamw(w, g, m, v, name):
    r, c = w.shape
    t = r
    if r * c > 256 * 1024:
        t = next(cand for cand in (512, 256, 128, 64, 32, 16, 8) if r % cand == 0 and cand * c <= 512 * 1024)
    bc1 = 1.0 - ADAM_B1 ** ADAM_STEP
    bc2 = 1.0 - ADAM_B2 ** ADAM_STEP

    def body(w_ref, g_ref, m_ref, v_ref, d_ref, nm_ref, nv_ref):
        gv = g_ref[...]
        nm = ADAM_B1 * m_ref[...] + (1.0 - ADAM_B1) * gv
        nv = ADAM_B2 * v_ref[...] + (1.0 - ADAM_B2) * (gv * gv)
        nm_ref[...] = nm
        nv_ref[...] = nv
        d_ref[...] = -ADAM_LR * ((nm / bc1) / (jnp.sqrt(nv / bc2) + ADAM_EPS) + ADAM_WD * w_ref[...])

    spec = pl.BlockSpec((t, c), lambda i: (i, 0))
    return pl.pallas_call(
        body, name=name, grid=(r // t,), in_specs=[spec] * 4, out_specs=[spec] * 3,
        out_shape=[_sds((r, c))] * 3, compiler_params=_cp("parallel"),
    )(w, g, m, v)


def _half_blocks(shape, axis):
    r, c = shape
    if axis == 0:
        t = 256 if (r // 2) % 256 == 0 else 128
        nb = (r // 2) // t
        return (t, c), nb, (lambda i: (i, 0)), (lambda i: (i % nb, 0))
    nb = (c // 2) // 128
    return (r, 128), nb, (lambda i: (0, i)), (lambda i: (0, i % nb))


def _adamw_halves(w, g_mine, g_other, m, v, cidx, name, axis=0):
    r, c = w.shape
    blk, nb, whole, part = _half_blocks(w.shape, axis)
    bc1 = 1.0 - ADAM_B1 ** ADAM_STEP
    bc2 = 1.0 - ADAM_B2 ** ADAM_STEP

    def body(c_ref, w_ref, gm_ref, go_ref, m_ref, v_ref, g_ref, d_ref, nm_ref, nv_ref):
        mine = (pl.program_id(0) // nb) == c_ref[0]
        gv = jnp.where(mine, gm_ref[...], go_ref[...])
        g_ref[...] = gv
        nm = ADAM_B1 * m_ref[...] + (1.0 - ADAM_B1) * gv
        nv = ADAM_B2 * v_ref[...] + (1.0 - ADAM_B2) * (gv * gv)
        nm_ref[...] = nm
        nv_ref[...] = nv
        d_ref[...] = -ADAM_LR * ((nm / bc1) / (jnp.sqrt(nv / bc2) + ADAM_EPS) + ADAM_WD * w_ref[...])

    spec = pl.BlockSpec(blk, lambda i, c_ref: whole(i))
    half = pl.BlockSpec(blk, lambda i, c_ref: part(i))
    return pl.pallas_call(
        body, name=name,
        grid_spec=pltpu.PrefetchScalarGridSpec(num_scalar_prefetch=1, grid=(2 * nb,),
                                               in_specs=[spec, half, half, spec, spec], out_specs=[spec] * 4),
        out_shape=[_sds((r, c))] * 4, compiler_params=_cp("parallel"),
    )(cidx, w, g_mine, g_other, m, v)


def _block_diag(w):
    eye = jnp.eye(NH, dtype=w.dtype)
    return (w[:, :, None, :] * eye[:, None, :, None]).reshape(D, D)


def _diag_blocks(full):
    eye = jnp.eye(NH, dtype=full.dtype)
    return (full.reshape(NH, HP, NH, HP) * eye[:, None, :, None]).sum(axis=2)


def _pad_lanes(v, n=128):
    return jnp.pad(v, ((0, 0), (0, n - v.shape[1])))


def _local_step(x, target, p, dist=None):
    heads = jnp.arange(D, dtype=jnp.int32) // HP
    expand = (jnp.arange(128, dtype=jnp.int32)[:, None] == heads[None, :]).astype(F32)
    reduce_ = expand.T
    dskip_x = jnp.repeat(p["d_skip"], HP, axis=1)
    a_log = _pad_lanes(p["a_log"])
    dt_bias = _pad_lanes(p["dt_bias"])
    wab = jnp.concatenate([_block_diag(p["lru_wa"]), _block_diag(p["lru_wx"])], axis=1).astype(MXU)
    ba = p["lru_ba"].reshape(1, D)
    bx = p["lru_bx"].reshape(1, D)

    def hosted(key, fn):
        plan = dist.plan(key) if dist is not None else None
        if plan is None:
            return fn(plan=None)
        outs, got = fn(plan=plan)
        dist.done(key, got, p)
        return outs

    u = hosted("norm_u", functools.partial(_norm_cast, x, p["norm_mix_pre"], "norm_u"))
    w5, wxbc, wdt = p["w5"], p["wxbc"], p["wdt"]
    proj5 = hosted("proj5", functools.partial(_matmul, u, w5, name="proj5", tb=True, tm=1024, out_dtype=MXU))
    xbc_raw = _matmul(u, wxbc, name="proj_xbc", tb=True, tn=XBC)
    dt_raw = _matmul(u, wdt, name="proj_dt", tb=True)
    xbc_c, dsilu, xr, dt = _conv_fwd(xbc_raw, proj5, dt_raw, p["conv_ssm_w"], p["conv_ssm_b"], p["conv_lru_w"],
                                     p["conv_lru_b"], dt_bias)
    y, ya, states = _ssd_fwd(xbc_c, dt, proj5, a_log, dskip_x, p["ssm_norm"], expand)
    h, merged, *gates = hosted("lru_fwd", functools.partial(_lru_fwd, xr, proj5, ya, wab, ba, bx, p["lru_lambda"]))
    mix, h1, v, pre = _out_up_proj(merged, p["w_out"], x, p["norm_mix_post"], p["norm_mlp_pre"], p["w_up"])
    dout, dff, loss, dg4 = _down_loss(pre, p["w_down"], h1, target, p["norm_mlp_post"])

    dpre = _matmul(dff, p["w_down"], name="d_pre", tb=True, tm=1024, out_dtype=MXU,
                   epi=lambda r, pr: r * (2.0 * jnp.maximum(pr.astype(F32), 0.0)), epi_args=(pre,))
    g_w_down = _matmul(pre, dff, name="dw_down", ta=True, tm=1024, tn=1024, tk=TK_GRAD, a_fn=_relu2)
    dh1, dmix, dmerged, dg3, dg2 = _dv_norms(dpre, p["w_up"], h1, mix, dout, p["norm_mlp_pre"], p["norm_mix_post"],
                                             p["w_out"])
    g_w_up = _matmul(v, dpre, name="dw_up", ta=True, tm=1024, tn=1024, tk=TK_GRAD)
    g_w_out = _matmul(merged, dmix, name="dw_out", ta=True, tm=1024, tn=1024, tk=TK_GRAD)
    if dist is not None:
        dist.early_grads(w_down=g_w_down, w_up=g_w_up, w_out=g_w_out)
    dya, d3, dxr, dp2, dlam, dba, dbx = hosted("lru_bwd", functools.partial(
        _lru_bwd, dmerged, ya, xr, h, proj5, gates, wab.T, p["lru_lambda"]))
    g_wab = _matmul(xr, dp2, name="dw_lru", ta=True, tm=1024, tn=1024, tk=TK_GRAD)
    g_wa, g_wx = _diag_blocks(g_wab[:, :D]), _diag_blocks(g_wab[:, D:])
    dz, dxbc_c, ddt, dng, ddsk, dalog = hosted("ssd_bwd", functools.partial(
        _ssd_bwd, dya, y, proj5, xbc_c, dt, states, a_log, dskip_x, p["ssm_norm"], expand, reduce_))
    (dxbc, dxl, ddtr, dcws, dcbs, dcwl, dcbl, ddtb) = hosted("conv_bwd", functools.partial(
        _conv_bwd, dxbc_c, dsilu, dxr, ddt, xbc_raw, proj5, dt_raw, p["conv_ssm_w"], p["conv_lru_w"], dt_bias))
    gw3 = _matmul(d3, u, name="dw_in_lru", ta=True, tm=1024, tn=1024, tk=TK_GRAD)
    gwz = _matmul(dz, u, name="dw_in_z", ta=True, tm=1024, tn=1024, tk=TK_GRAD)
    gwxl = _matmul(dxl, u, name="dw_in_xl", ta=True, tm=1024, tn=1024, tk=TK_GRAD)
    gwxbc = _matmul(dxbc, u, name="dw_in_xbc", ta=True, tm=XBC, tn=1024, tk=TK_GRAD)
    gwdt = _matmul(ddtr, u, name="dw_in_dt", ta=True, tm=128, tn=1024, tk=TK_GRAD)
    g_w_in_t = jnp.concatenate([gwz, gwxbc, gwdt[:NH], gw3[:D], gwxl, gw3[D:2 * D], gw3[2 * D:]], axis=0)
    grads = {
        "w_in_t": g_w_in_t, "conv_ssm_w": dcws, "conv_ssm_b": dcbs, "dt_bias": ddtb[:, :NH],
        "a_log": dalog[:, :NH], "d_skip": ddsk.reshape(NH, HP).sum(axis=1)[None, :], "ssm_norm": dng,
        "conv_lru_w": dcwl, "conv_lru_b": dcbl, "lru_wa": g_wa, "lru_ba": dba.reshape(NH, HP), "lru_wx": g_wx,
        "lru_bx": dbx.reshape(NH, HP), "lru_lambda": dlam, "w_out": g_w_out, "norm_mix_post": dg2,
        "norm_mlp_pre": dg3, "w_up": g_w_up, "w_down": g_w_down, "norm_mlp_post": dg4,
    }
    if dist is not None:
        dist.late_grads(grads, loss[0, 0])
    grad_x, grads["norm_mix_pre"] = hosted("du_norm", functools.partial(
        _du_norm, d3, dz, dxl, dxbc, ddtr, w5, wxbc, wdt, x, dh1, p["norm_mix_pre"]))
    return loss[0, 0], grad_x, grads


def _split_w_in_t(w_in_t):
    z, xbc, dtc, g, xl, ga, gb = jnp.split(w_in_t, [D, D + XBC, D + XBC + NH, 2 * D + XBC + NH,
                                                    3 * D + XBC + NH, 4 * D + XBC + NH], axis=0)
    return jnp.concatenate([g, ga, gb, z, xl], axis=0), xbc, jnp.pad(dtc, ((0, 128 - NH), (0, 0)))


COMM = BF16


def _place():
    x, y, c = lax.axis_index("x"), lax.axis_index("y"), lax.axis_index("c")
    chips = [(1 - x, y), (x, 1 - y), (1 - x, 1 - y)]
    return x, y, c, chips


def _remote(src, dst, send_sem, recv_sem, to):
    return pltpu.make_async_remote_copy(src_ref=src, dst_ref=dst, send_sem=send_sem, recv_sem=recv_sem, device_id=to,
                                        device_id_type=MESH)


def _gather_plan(big, small=(), axes=None):
    nb = len(big)
    arrs = list(big) + list(small)
    na = len(arrs)
    axes = list(axes or [0] * nb)

    def half(ref, a, k, which):
        h = arrs[a].shape[axes[a]] // 2
        cut = (pl.ds(which * h, h),) if axes[a] == 0 else (slice(None), pl.ds(which * h, h))
        return ref.at[cut] if k is None else ref.at[(k,) + cut]

    def direct(ins, outs, send, recv):
        x, y, c, chips = _place()
        k = 2 * x + y
        cps = []
        for a in range(na):
            src, dst = (half(ins[a], a, None, c), half(outs[a], a, k, c)) if a < nb else (ins[a], outs[a].at[k])
            cps += [_remote(src, dst, send.at[a, j], recv.at[a, j], (cx, cy, c)) for j, (cx, cy) in enumerate(chips)]
        return cps

    def passed(outs, send, recv):
        x, y, c, chips = _place()
        cps = []
        for j, (cx, cy) in enumerate(chips):
            for a in range(nb):
                got = half(outs[a], a, 2 * cx + cy, c)
                cps.append(_remote(got, got, send.at[a, 3 + j], recv.at[a, 3 + j], (x, y, 1 - c)))
        return cps

    def start(ins, outs, sems):
        for cp in direct(ins, outs, *sems[0]):
            cp.start()

    def mid(ins, outs, sems):
        send, recv = sems[0]
        _, _, c, chips = _place()
        fwd = passed(outs, send, recv)
        for j, (cx, cy) in enumerate(chips):
            kj = 2 * cx + cy
            for a in range(na):
                got = half(outs[a], a, kj, c) if a < nb else outs[a].at[kj]
                _remote(got, got, send.at[a, j], recv.at[a, j], (cx, cy, c)).wait_recv()
                if a < nb:
                    fwd[j * nb + a].start()

    def finish(ins, outs, sems):
        send, recv = sems[0]
        x, y, c, chips = _place()
        for j, (cx, cy) in enumerate(chips):
            for a in range(nb):
                got = half(outs[a], a, 2 * cx + cy, 1 - c)
                _remote(got, got, send.at[a, 3 + j], recv.at[a, 3 + j], (x, y, 1 - c)).wait_recv()
        for cp in direct(ins, outs, send, recv) + passed(outs, send, recv):
            cp.wait_send()

    return _Plan(arrs, [_sds((NCHIP,) + a.shape, a.dtype) for a in arrs], [(na, 6)], start, finish, mid)


def _own_shards(gathered, shards):
    kchip = 2 * lax.axis_index("x") + lax.axis_index("y")
    return [lax.dynamic_update_index_in_dim(o, a, kchip, 0) for o, a in zip(gathered, shards)]


def _swap_plan(ins, outs, sems, copies):
    def start(i, o, s):
        for cp in copies(i, o, *s[0]):
            cp.start()

    def finish(i, o, s):
        for cp in copies(i, o, *s[0]):
            cp.wait()

    return _Plan(ins, outs, [sems], start, finish)


def _half_shape(shape, axis):
    return tuple(d // 2 if i == axis else d for i, d in enumerate(shape))


def _pair_exchange_plan(gs, axis=1):
    def copies(ins, outs, send, recv):
        x, y, c, _ = _place()
        cps = []
        for a in range(len(gs)):
            h = ins[a].shape[axis] // 2
            theirs = pl.ds((1 - c) * h, h)
            src = ins[a].at[:, theirs] if axis == 1 else ins[a].at[:, :, theirs]
            cps.append(_remote(src, outs[a], send.at[a], recv.at[a], (x, y, 1 - c)))
        return cps

    return _swap_plan(gs, [_sds(_half_shape(g.shape, axis), g.dtype) for g in gs], (len(gs),), copies)


def _pair_add(g, got, cidx, name, axis=1):
    half = _half_shape(g.shape, axis)
    blk, nt, _, part = _half_blocks(g.shape[1:], axis - 1)

    def body(c_ref, g_ref, o_ref, p_ref, pc_ref):
        sm = g_ref[...] + o_ref[...]
        p_ref[...] = sm
        pc_ref[...] = sm.astype(pc_ref.dtype)

    def mine(k, i, c_ref):
        j = c_ref[0] * nt + i
        return (k, j, 0) if axis == 1 else (k, 0, j)

    spec = pl.BlockSpec((1,) + blk, lambda k, i, c_ref: (k,) + part(i))
    return pl.pallas_call(
        body, name=name,
        grid_spec=pltpu.PrefetchScalarGridSpec(
            num_scalar_prefetch=1, grid=(NCHIP, nt),
            in_specs=[pl.BlockSpec((1,) + blk, mine), spec], out_specs=[spec, spec]),
        out_shape=[_sds(half), _sds(half, COMM)],
        compiler_params=_cp("parallel", "parallel"),
    )(cidx, g, got)


def _chip_exchange_plan(ps):
    def copies(ins, outs, send, recv):
        _, _, c, chips = _place()
        return [_remote(ins[a].at[2 * cx + cy], outs[a].at[j], send.at[a, j], recv.at[a, j], (cx, cy, c))
                for a in range(len(ps)) for j, (cx, cy) in enumerate(chips)]

    return _swap_plan(ps, [_sds((NCHIP - 1,) + p.shape[1:], p.dtype) for p in ps], (len(ps), 3), copies)


def _shard_sum(p, got, kidx, name, axis=1):
    full = tuple(2 * d if i == axis - 1 else d for i, d in enumerate(p.shape[1:]))
    blk, nt, _, part = _half_blocks(full, axis - 1)

    def body(k_ref, p_ref, g_ref, o_ref):
        sm = p_ref[0]
        for j in range(NCHIP - 1):
            sm = sm + g_ref[j].astype(F32)
        o_ref[...] = sm

    return pl.pallas_call(
        body, name=name,
        grid_spec=pltpu.PrefetchScalarGridSpec(
            num_scalar_prefetch=1, grid=(nt,),
            in_specs=[pl.BlockSpec((1,) + blk, lambda i, k_ref: (k_ref[0],) + part(i)),
                      pl.BlockSpec((NCHIP - 1,) + blk, lambda i, k_ref: (0,) + part(i))],
            out_specs=pl.BlockSpec(blk, lambda i, k_ref: part(i))),
        out_shape=_sds(p.shape[1:]),
        compiler_params=_cp("parallel"),
    )(kidx, p, got)


def _pair_swap_plan(rs):
    def copies(ins, outs, send, recv):
        x, y, c, _ = _place()
        return [_remote(ins[a], outs[a], send.at[a], recv.at[a], (x, y, 1 - c)) for a in range(len(rs))]

    return _swap_plan(rs, [_sds(r.shape, r.dtype) for r in rs], (len(rs),), copies)


def _allgather8_plan(v):
    def pieces(ins, outs, send, recv):
        x, y, c, chips = _place()
        me, sibling = (x, y, c), (x, y, 1 - c)

        def copy(k, block, to, src=None):
            px, py, pc = block
            slot = outs[0].at[4 * px + 2 * py + pc]
            return _remote(slot if src is None else src, slot, send.at[k], recv.at[k], to)

        first = [copy(0, me, sibling, src=ins[0])] + [copy(1 + j, me, (*chip, c), src=ins[0])
                                                      for j, chip in enumerate(chips)]
        passed = [copy(4 + j, (*chip, c), sibling) for j, chip in enumerate(chips)]
        arrivals = [copy(1 + j, (*chip, c), me) for j, chip in enumerate(chips)]
        late = [copy(0, sibling, me)] + [copy(4 + j, (*chip, 1 - c), me) for j, chip in enumerate(chips)]
        return first, passed, arrivals, late

    def start(ins, outs, sems):
        for cp in pieces(ins, outs, *sems[0])[0]:
            cp.start()

    def mid(ins, outs, sems):
        _, passed, arrivals, _ = pieces(ins, outs, *sems[0])
        for got, fwd in zip(arrivals, passed):
            got.wait_recv()
            fwd.start()

    def finish(ins, outs, sems):
        first, passed, _, late = pieces(ins, outs, *sems[0])
        for got in late:
            got.wait_recv()
        for cp in first + passed:
            cp.wait_send()

    return _Plan([v], [_sds((8,) + v.shape, v.dtype)], [(7,)], start, finish, mid)


def _own_block(gathered, v):
    me = 4 * lax.axis_index("x") + 2 * lax.axis_index("y") + lax.axis_index("c")
    return lax.dynamic_update_index_in_dim(gathered, v, me, 0)


def _sum_devices(allv, name):
    _, r, _ = allv.shape

    def body(a_ref, o_ref):
        sm = a_ref[0]
        for d in range(1, 8):
            sm = sm + a_ref[d]
        o_ref[...] = sm

    return pl.pallas_call(
        body, name=name, grid=(1,), in_specs=[_full_spec((8, r, 128))], out_specs=_full_spec((r, 128)),
        out_shape=_sds((r, 128)), compiler_params=_cp("arbitrary"),
    )(allv)


def _pack(arrs):
    flat = jnp.concatenate([a.reshape(-1) for a in arrs])
    return jnp.pad(flat, (0, (-flat.shape[0]) % 1024)).reshape(-1, 128)


def _unpack(packed, shapes):
    flat, outs, off = packed.reshape(-1), [], 0
    for shp in shapes:
        n = math.prod(shp)
        outs.append(flat[off:off + n].reshape(shp))
        off += n
    return outs


BIG = ("w_in", "w_out", "w_up", "w_down")
CONV = ("conv_ssm_w", "conv_lru_w")
WEIGHTS = ("norm_mix_pre", "w_in", "conv_ssm_w", "conv_ssm_b", "dt_bias", "a_log", "d_skip", "ssm_norm", "conv_lru_w",
           "conv_lru_b", "lru_wa", "lru_ba", "lru_wx", "lru_bx", "lru_lambda", "w_out", "norm_mix_post",
           "norm_mlp_pre", "w_up", "w_down", "norm_mlp_post")
SMALL = tuple(n for n in WEIGHTS if n not in BIG and n not in CONV)
EARLY = ("w_down", "w_up", "w_out")


def _cat_cols(g):
    return jnp.concatenate([g[k] for k in range(NCHIP)], axis=1)


class _Dist:
    def __init__(self, shards, first, cidx, kidx):
        self.shards, self.first, self.cidx, self.kidx = shards, first, cidx, kidx
        self.halves = {}

    def early_grads(self, w_down, w_up, w_out):
        self.shard_major = [w_down.reshape(NCHIP, D, D), jnp.stack([w_up[:, D * k:D * (k + 1)] for k in range(NCHIP)]),
                            w_out.reshape(NCHIP, D // NCHIP, D)]

    def late_grads(self, grads, loss):
        g_in = grads["w_in_t"].reshape(NCHIP, W_IN_SHARD, D)
        got, = _run_plan(_pair_exchange_plan([g_in], axis=2), "grad_pair_exchange_w_in")
        self.p_in, self.pc_in = _pair_add(g_in, got, self.cidx, "grad_pair_add_w_in", axis=2)
        self.small_names = [n for n in SMALL + CONV if n != "norm_mix_pre"]
        self.small_shapes = [grads[n].shape for n in self.small_names] + [(1,)]
        self.packed_small = _pack([grads[n] for n in self.small_names] + [loss.reshape(1)])

    def plan(self, key):
        if key == "norm_u":
            return _gather_plan(self.first[:1], self.first[1:], axes=[1])
        if key == "proj5":
            return _gather_plan([self.shards["w_out"], self.shards["w_up"]])
        if key == "lru_fwd":
            return _gather_plan([self.shards["w_down"]])
        if key == "lru_bwd":
            return _pair_exchange_plan(self.shard_major)
        if key == "ssd_bwd":
            return _chip_exchange_plan([pc for _, pc in self.pair])
        if key == "conv_bwd":
            return _pair_swap_plan(self.mine)
        if key == "du_norm":
            return _merge_plans(_allgather8_plan(self.packed_small), _chip_exchange_plan([self.pc_in]))
        return None

    def done(self, key, got, p):
        if key == "norm_u":
            g_in, g_cs, g_cl = _own_shards(got, self.first)
            w5, wxbc, wdt = _split_w_in_t(g_in.reshape(W_IN_COLS, D))
            p.update(w5=w5, wxbc=wxbc, wdt=wdt, conv_ssm_w=_cat_cols(g_cs), conv_lru_w=_cat_cols(g_cl))
        elif key == "proj5":
            g_out, g_up = _own_shards(got, [self.shards["w_out"], self.shards["w_up"]])
            p.update(w_out=g_out.reshape(D, D), w_up=_cat_cols(g_up))
        elif key == "lru_fwd":
            g_down, = _own_shards(got, [self.shards["w_down"]])
            p.update(w_down=g_down.reshape(DFF, D))
        elif key == "lru_bwd":
            self.pair = [_pair_add(gs, o, self.cidx, f"grad_pair_add_{n}")
                         for gs, o, n in zip(self.shard_major, got, EARLY)]
        elif key == "ssd_bwd":
            self.mine = [_shard_sum(pf, o, self.kidx, f"grad_shard_sum_{n}")
                         for (pf, _), o, n in zip(self.pair, got, EARLY)]
        elif key == "conv_bwd":
            self.halves = {n: (mine, other) for n, mine, other in zip(EARLY, self.mine, got)}
        elif key == "du_norm":
            self.all_small, self.from_chips_in = got


def kernel(x, norm_mix_pre, w_in, conv_ssm_w, conv_ssm_b, dt_bias, a_log, d_skip, ssm_norm, conv_lru_w, conv_lru_b, lru_wa, lru_ba, lru_wx, lru_bx, lru_lambda, w_out, norm_mix_post, norm_mlp_pre, w_up, w_down, norm_mlp_post, loss_target, m_norm_mix_pre, m_w_in, m_conv_ssm_w, m_conv_ssm_b, m_dt_bias, m_a_log, m_d_skip, m_ssm_norm, m_conv_lru_w, m_conv_lru_b, m_lru_wa, m_lru_ba, m_lru_wx, m_lru_bx, m_lru_lambda, m_w_out, m_norm_mix_post, m_norm_mlp_pre, m_w_up, m_w_down, m_norm_mlp_post, v_norm_mix_pre, v_w_in, v_conv_ssm_w, v_conv_ssm_b, v_dt_bias, v_a_log, v_d_skip, v_ssm_norm, v_conv_lru_w, v_conv_lru_b, v_lru_wa, v_lru_ba, v_lru_wx, v_lru_bx, v_lru_lambda, v_w_out, v_norm_mix_post, v_norm_mlp_pre, v_w_up, v_w_down, v_norm_mlp_post):
    args = locals()
    w = {n: args[n][0] for n in WEIGHTS}
    m = {n: args["m_" + n][0] for n in WEIGHTS}
    v = {n: args["v_" + n][0] for n in WEIGHTS}
    cidx = lax.axis_index("c").astype(jnp.int32).reshape(1)
    kchip = 2 * lax.axis_index("x") + lax.axis_index("y")
    to_t = lambda a: jnp.transpose(a, (2, 0, 1)).reshape(W_IN_SHARD, D)
    from_t = lambda a: jnp.transpose(a.reshape(W_IN_SHARD, 1, D), (1, 2, 0))
    shards = {n: (to_t(w_in) if n == "w_in" else w[n]).astype(MXU) for n in BIG}
    dist = _Dist(shards, [shards["w_in"], w["conv_ssm_w"], w["conv_lru_w"]], cidx, kchip.astype(jnp.int32).reshape(1))
    p = {n: (w[n].reshape(1, -1) if w[n].ndim == 1 else w[n]) for n in SMALL}

    _, grad_x, g = _local_step(x[0], loss_target[0], p, dist)

    half_in = _shard_sum(dist.p_in, dist.from_chips_in, dist.kidx, "grad_shard_sum_w_in", axis=2)
    packed_g1 = _pack([g["norm_mix_pre"]])
    all_g1, other_in = _run_plan(_merge_plans(_allgather8_plan(packed_g1), _pair_swap_plan([half_in])),
                                 "grad_pair_swap_w_in")
    halves = dist.halves

    reduced = {}
    *summed, loss = _unpack(_sum_devices(_own_block(dist.all_small, dist.packed_small), "small_sum"),
                            dist.small_shapes)
    loss = loss.reshape(())
    g1, = _unpack(_sum_devices(_own_block(all_g1, packed_g1), "small_sum_norm_mix_pre"), [g["norm_mix_pre"].shape])
    for n, s in zip(dist.small_names + ["norm_mix_pre"], summed + [g1]):
        if n in CONV:
            width = w[n].shape[1]
            reduced[n] = lax.dynamic_slice_in_dim(s, kchip * width, width, axis=1)
        else:
            reduced[n] = s.reshape(w[n].shape)

    delta, new_m, new_v = {}, {}, {}
    for n in EARLY:
        mine, other = halves[n]
        reduced[n], delta[n], new_m[n], new_v[n] = _adamw_halves(w[n], mine, other, m[n], v[n], cidx, f"adamw_{n}")
    outs_t = _adamw_halves(to_t(w_in), half_in, other_in, to_t(m_w_in), to_t(v_w_in), cidx, "adamw_w_in", axis=1)
    for d, o in zip((reduced, delta, new_m, new_v), outs_t):
        d["w_in"] = from_t(o)[0]
    for n in CONV:
        delta[n], new_m[n], new_v[n] = _adamw(w[n], reduced[n], m[n], v[n], f"adamw_{n}")
    shapes = [w[n].shape for n in SMALL]
    packed = [_pack([d[n] for n in SMALL]) for d in (w, reduced, m, v)]
    for d, out in zip((delta, new_m, new_v), _adamw(*packed, "adamw_small")):
        d.update(zip(SMALL, _unpack(out, shapes)))

    lead = lambda d: [d[n][None] for n in WEIGHTS]
    return (loss, grad_x[None], *lead(reduced), *lead(delta), *lead(new_m), *lead(new_v))
```

```python
import functools
import math

import jax
import jax.numpy as jnp
from jax import lax
from jax.experimental import pallas as pl
from jax.experimental.pallas import tpu as pltpu

F32 = jnp.float32
BF16 = jnp.bfloat16
MXU = BF16

D = 1024
DFF = 4096
NH = 16
HP = 64
NG = 2
NS = 128
CH = 128
XBC = D + 2 * NG * NS
GW = D // NG
LRU_C = 8.0
EPS = 1e-6
NCHIP = 4
W_IN_COLS = 6672
W_IN_SHARD = W_IN_COLS // NCHIP

ADAM_LR = 0.001
ADAM_B1 = 0.9
ADAM_B2 = 0.999
ADAM_EPS = 1e-08
ADAM_WD = 0.01
ADAM_STEP = 10

VMEM_LIMIT = 56 * 1024 * 1024
TK_GRAD = 2048
MID_AT = 0.7
ROWS_FUSED = 512
COL_G, COL_GA, COL_GB, COL_Z, COL_XL = range(5)
MESH = pl.DeviceIdType.MESH


def _cp(*sem):
    return pltpu.CompilerParams(dimension_semantics=sem, vmem_limit_bytes=VMEM_LIMIT)


def _dot(a, b, ca=1, cb=0, prec=None):
    return lax.dot_general(a, b, (((ca,), (cb,)), ((), ())), precision=prec, preferred_element_type=F32)


def _mdot(a, b, ca=1, cb=0):
    return _dot(a.astype(MXU), b.astype(MXU), ca, cb)


def _bf16_parts(v, n):
    parts = []
    for i in range(n):
        p = v.astype(BF16)
        parts.append(p)
        if i < n - 1:
            v = v - p.astype(F32)
    return parts


def _xdot(a, b, passes, split_b=False):
    if split_b:
        a16 = a.astype(BF16)
        terms = [_dot(a16, p) for p in _bf16_parts(b, passes)]
    else:
        b16 = b.astype(BF16)
        terms = [_dot(p, b16) for p in _bf16_parts(a, passes)]
    return functools.reduce(lambda u, v: u + v, terms)


def _sig(x):
    return 0.5 * jnp.tanh(0.5 * x) + 0.5


def _silu(x):
    return x * _sig(x)


def _dsilu(x):
    s = _sig(x)
    return s * (1.0 + x * (1.0 - s))


def _softplus(x):
    e = jnp.exp(-jnp.abs(x))
    return jnp.maximum(x, 0.0) + jnp.where(e < 1e-4, e * (1.0 - 0.5 * e), jnp.log(1.0 + e))


_GELU_C = math.sqrt(2.0 / math.pi)


def _gelu(x):
    t = jnp.tanh(_GELU_C * (x + 0.044715 * x * x * x))
    return 0.5 * x * (1.0 + t)


def _gelu_and_grad(x):
    x2 = x * x
    t = jnp.tanh(_GELU_C * (x + 0.044715 * x * x2))
    half = 0.5 * (1.0 + t)
    return x * half, half + 0.5 * x * (1.0 - t * t) * _GELU_C * (1.0 + 3.0 * 0.044715 * x2)


def _one_minus_sq(a, la):
    x = 2.0 * la
    series = -x * (1.0 + x * (0.5 + x * (1.0 / 6.0)))
    return jnp.where(x > -0.01, series, 1.0 - a * a)


def _rms(x):
    return lax.rsqrt(jnp.mean(x * x, axis=-1, keepdims=True) + EPS)


def _rms_bwd(x, r, g, dy):
    xn = x * r
    dxh = dy * g
    m = jnp.mean(dxh * xn, axis=-1, keepdims=True)
    return r * (dxh - xn * m), jnp.sum(dy * xn, axis=0, keepdims=True)


def _row_spec(t, c, col=0):
    return pl.BlockSpec((t, c), lambda i: (i, col))


def _rev_spec(t, c, n, col=0):
    return pl.BlockSpec((t, c), lambda i: (n - 1 - i, col))


def _full_spec(shape, once=False):
    nd = len(shape)
    if once:
        return pl.BlockSpec(shape, lambda *_: (0,) * nd, pipeline_mode=pl.Buffered(1))
    return pl.BlockSpec(shape, lambda *_: (0,) * nd)


def _sds(shape, dtype=F32):
    return jax.ShapeDtypeStruct(shape, dtype)


ANY = pl.BlockSpec(memory_space=pl.ANY)


class _Plan:
    def __init__(self, ins, outs, sems, start, finish, mid=None):
        self.ins, self.outs, self.sems = list(ins), list(outs), list(sems)
        self.start, self.finish, self.mid = start, finish, mid or (lambda i, o, s: None)


def _merge_plans(*plans):
    def each(fn_name, ins, outs, sems):
        i = o = s = 0
        for p in plans:
            getattr(p, fn_name)(ins[i:i + len(p.ins)], outs[o:o + len(p.outs)], sems[s:s + len(p.sems)])
            i, o, s = i + len(p.ins), o + len(p.outs), s + len(p.sems)

    return _Plan([a for p in plans for a in p.ins], [a for p in plans for a in p.outs],
                 [a for p in plans for a in p.sems], functools.partial(each, "start"),
                 functools.partial(each, "finish"), functools.partial(each, "mid"))


def _pcall(body, args, *, name, grid, in_specs, out_specs, out_shape, sem, scratch_shapes=(), plan=None):
    single = not isinstance(out_shape, (list, tuple))
    out_specs = [out_specs] if single else list(out_specs)
    out_shape = [out_shape] if single else list(out_shape)
    if plan is None:
        outs = pl.pallas_call(body, name=name, grid=grid, in_specs=list(in_specs), out_specs=out_specs,
                              out_shape=out_shape, scratch_shapes=list(scratch_shapes),
                              compiler_params=_cp(*sem))(*args)
        return outs[0] if single else outs
    n_in, n_out, n_sc, ni, no = len(in_specs), len(out_shape), len(scratch_shapes), len(plan.ins), len(plan.outs)

    def hosted(*refs):
        b0 = n_in + ni
        b1 = b0 + n_out + no
        sem_refs = refs[b1 + n_sc:]
        sems = [(sem_refs[2 * q], sem_refs[2 * q + 1]) for q in range(len(plan.sems))]
        step = functools.reduce(lambda lin, ig: lin * ig[1] + ig[0],
                                [(pl.program_id(d), g) for d, g in enumerate(grid)], 0)
        total = math.prod(grid)

        @pl.when(step == 0)
        def _():
            plan.start(refs[n_in:b0], refs[b0 + n_out:b1], sems)

        body(*refs[:n_in], *refs[b0:b0 + n_out], *refs[b1:b1 + n_sc])

        @pl.when(step == min(int(MID_AT * total), total - 1))
        def _():
            plan.mid(refs[n_in:b0], refs[b0 + n_out:b1], sems)

        @pl.when(step == total - 1)
        def _():
            plan.finish(refs[n_in:b0], refs[b0 + n_out:b1], sems)

    dma = [pltpu.SemaphoreType.DMA(shape) for shape in plan.sems for _ in range(2)]
    outs = pl.pallas_call(hosted, name=name, grid=grid, in_specs=list(in_specs) + [ANY] * ni,
                          out_specs=out_specs + [ANY] * no, out_shape=out_shape + plan.outs,
                          scratch_shapes=list(scratch_shapes) + dma,
                          compiler_params=_cp(*("arbitrary",) * len(grid)))(*args, *plan.ins)
    return (outs[0] if single else outs[:n_out]), outs[n_out:]


def _run_plan(plan, name):
    ni, no = len(plan.ins), len(plan.outs)

    def body(*refs):
        sem_refs = refs[ni + no:]
        sems = [(sem_refs[2 * q], sem_refs[2 * q + 1]) for q in range(len(plan.sems))]
        plan.start(refs[:ni], refs[ni:ni + no], sems)
        plan.mid(refs[:ni], refs[ni:ni + no], sems)
        plan.finish(refs[:ni], refs[ni:ni + no], sems)

    return pl.pallas_call(
        body, name=name, in_specs=[ANY] * ni, out_specs=[ANY] * no, out_shape=plan.outs,
        scratch_shapes=[pltpu.SemaphoreType.DMA(shape) for shape in plan.sems for _ in range(2)],
    )(*plan.ins)


def _matmul(a, b, *, name, ta=False, tb=False, tm=512, tn=1024, tk=1024, out_dtype=F32, a_fn=None, epi=None,
            epi_args=(), plan=None):
    m, k = (a.shape[1], a.shape[0]) if ta else a.shape
    n = b.shape[0] if tb else b.shape[1]
    tm, tn, tk = min(tm, m), min(tn, n), min(tk, k)
    nk = k // tk
    a_spec = pl.BlockSpec((tk, tm), lambda i, j, kk: (kk, i)) if ta else pl.BlockSpec((tm, tk), lambda i, j, kk: (i, kk))
    b_spec = pl.BlockSpec((tn, tk), lambda i, j, kk: (j, kk)) if tb else pl.BlockSpec((tk, tn), lambda i, j, kk: (kk, j))
    e_specs = [pl.BlockSpec((tm, tn), lambda i, j, kk: (i, j)) for _ in epi_args]
    ne = len(epi_args)

    def body(a_ref, b_ref, *rest):
        e_refs, o_ref = rest[:ne], rest[ne]
        av = a_ref[...]
        if a_fn is not None:
            av = a_fn(av)
        part = _mdot(av, b_ref[...], 0 if ta else 1, 1 if tb else 0)

        def finish(r):
            if epi is not None:
                r = epi(r, *[e[...] for e in e_refs])
            o_ref[...] = r.astype(o_ref.dtype)

        if nk == 1:
            finish(part)
            return
        acc_ref = rest[ne + 1]
        kk = pl.program_id(2)

        @pl.when(kk == 0)
        def _():
            acc_ref[...] = part

        @pl.when(jnp.logical_and(kk > 0, kk < nk - 1))
        def _():
            acc_ref[...] += part

        @pl.when(kk == nk - 1)
        def _():
            finish(acc_ref[...] + part)

    return _pcall(
        body, (a, b, *epi_args), name=name, grid=(m // tm, n // tn, nk),
        in_specs=[a_spec, b_spec] + e_specs,
        out_specs=pl.BlockSpec((tm, tn), lambda i, j, kk: (i, j)),
        out_shape=_sds((m, n), out_dtype),
        scratch_shapes=[pltpu.VMEM((tm, tn), F32)] if nk > 1 else [],
        sem=("parallel", "parallel", "arbitrary"), plan=plan)


def _relu2(p):
    p = jnp.maximum(p, jnp.zeros((), p.dtype))
    return p * p


def _norm_cast(x, g, name, plan=None):
    s = x.shape[0]
    t = min(512, s)

    def body(x_ref, g_ref, o_ref):
        xv = x_ref[...]
        o_ref[...] = (xv * _rms(xv) * g_ref[...]).astype(o_ref.dtype)

    return _pcall(body, (x, g), name=name, grid=(s // t,), in_specs=[_row_spec(t, D), _full_spec((1, D))],
                  out_specs=_row_spec(t, D), out_shape=_sds((s, D), MXU), sem=("parallel",), plan=plan)


def _conv_fwd(xbc_raw, proj5, dt_raw, cw_s, cb_s, cw_l, cb_l, dt_bias):
    s = xbc_raw.shape[0]
    t = min(256, s)

    def body(xs_ref, xl_ref, dtr_ref, cws_ref, cbs_ref, cwl_ref, cbl_ref, dtb_ref, xc_ref, dsl_ref, xr_ref, dt_ref,
             bs_ref, bl_ref):
        @pl.when(pl.program_id(0) == 0)
        def _():
            bs_ref[0:8, :] = jnp.zeros((8, XBC), F32)
            bl_ref[0:8, :] = jnp.zeros((8, D), F32)

        bs_ref[8:t + 8, :] = xs_ref[...]
        bl_ref[8:t + 8, :] = xl_ref[...].astype(F32)

        def conv(buf, w_ref, b_ref):
            acc = b_ref[...] + w_ref[3:4, :] * buf[8:t + 8, :]
            for k in (1, 2, 3):
                acc = acc + w_ref[3 - k:4 - k, :] * buf[8 - k:t + 8 - k, :]
            return acc

        pre = conv(bs_ref, cws_ref, cbs_ref)
        sg = _sig(pre)
        xc_ref[...] = pre * sg
        dsl_ref[...] = (sg * (1.0 + pre * (1.0 - sg))).astype(dsl_ref.dtype)
        xr_ref[...] = conv(bl_ref, cwl_ref, cbl_ref)
        dt_ref[...] = _softplus(dtr_ref[...] + dtb_ref[...])
        bs_ref[0:8, :] = bs_ref[t:t + 8, :]
        bl_ref[0:8, :] = bl_ref[t:t + 8, :]

    return pl.pallas_call(
        body, name="conv_fwd", grid=(s // t,),
        in_specs=[_row_spec(t, XBC), _row_spec(t, D, COL_XL), _row_spec(t, 128), _full_spec((4, XBC)),
                  _full_spec((1, XBC)), _full_spec((4, D)), _full_spec((1, D)), _full_spec((1, 128))],
        out_specs=[_row_spec(t, XBC), _row_spec(t, XBC), _row_spec(t, D), _row_spec(t, 128)],
        out_shape=[_sds((s, XBC)), _sds((s, XBC), BF16), _sds((s, D)), _sds((s, 128))],
        scratch_shapes=[pltpu.VMEM((t + 8, XBC), F32), pltpu.VMEM((t + 8, D), F32)],
        compiler_params=_cp("arbitrary"),
    )(xbc_raw, proj5, dt_raw, cw_s, cb_s, cw_l, cb_l, dt_bias)


def _ssd_chunk_setup(dt_ref, alog_ref, e_ref, at_ref, dtt_ref):
    lane = lax.broadcasted_iota(jnp.int32, (CH, 128), 1)
    row = lax.broadcasted_iota(jnp.int32, (CH, 128), 0)
    lane1 = lax.broadcasted_iota(jnp.int32, (1, 128), 1)
    a = jnp.where(lane1 < NH, -jnp.exp(alog_ref[...]), 0.0)
    dtv = dt_ref[...]
    adt = dtv * a
    tril = row >= lane
    acum = _xdot(tril.astype(F32), adt, 3, split_b=True)
    alast = jnp.sum(adt, axis=0, keepdims=True)
    at_ref[...] = acum.T
    dtt_ref[...] = dtv.T
    e = e_ref[...]
    ea_x = _xdot(jnp.exp(acum), e, 2)
    ws = jnp.exp(alast - acum) * dtv
    ws_x = _xdot(ws, e, 2)
    eal = jnp.exp(alast)
    eal_x = jnp.max(_xdot(jnp.broadcast_to(eal, (8, 128)), e, 3), axis=0, keepdims=True)
    return dict(lane=lane, row=row, tril=tril, a=a, dtv=dtv, acum=acum, alast=alast, ea_x=ea_x, ws=ws, ws_x=ws_x,
                eal=eal, eal_x=eal_x)


def _head_decay(cs, at_ref, dtt_ref, h):
    col = jnp.sum(jnp.where(cs["lane"] == h, cs["acum"], 0.0), axis=1, keepdims=True)
    ld = jnp.where(cs["tril"], jnp.exp(jnp.minimum(col - at_ref[h:h + 1, :], 0.0)), 0.0)
    return ld, dtt_ref[h:h + 1, :]


def _ssd_fwd(xbc_c, dt, proj5, a_log, dskip_x, ssm_norm, expand):
    s = xbc_c.shape[0]
    nc = s // CH

    def body(xc_ref, dt_ref, z_ref, alog_ref, dsk_ref, ng_ref, e_ref, y_ref, ya_ref, st_ref, h_ref, at_ref, dtt_ref,
             yd_ref):
        @pl.when(pl.program_id(0) == 0)
        def _():
            h_ref[...] = jnp.zeros_like(h_ref)

        cs = _ssd_chunk_setup(dt_ref, alog_ref, e_ref, at_ref, dtt_ref)
        lane = cs["lane"]
        for g in range(NG):
            gs = slice(GW * g, GW * (g + 1))
            bg = xc_ref[:, D + NS * g:D + NS * (g + 1)]
            cg = xc_ref[:, D + NG * NS + NS * g:D + NG * NS + NS * (g + 1)]
            cb = _mdot(cg, bg, 1, 1)
            for j in range(4 * g, 4 * g + 4):
                ps = slice(128 * j, 128 * (j + 1))
                xp = xc_ref[:, ps]
                acc = jnp.zeros((CH, 128), F32)
                for hf in range(2):
                    ld, rowdt = _head_decay(cs, at_ref, dtt_ref, 2 * j + hf)
                    hm = (lane >= HP) if hf else (lane < HP)
                    acc = acc + _mdot(cb * ld * rowdt, jnp.where(hm, xp, 0.0))
                yd_ref[:, ps] = acc
            hg = h_ref[:, gs]
            yd_ref[:, gs] += _mdot(cg, hg) * cs["ea_x"][:, gs]
            st = _mdot(bg, xc_ref[:, gs] * cs["ws_x"][:, gs], 0, 0)
            st_ref[0, :, gs] = hg
            h_ref[:, gs] = cs["eal_x"][:, gs] * hg + st
        y = yd_ref[...] + dsk_ref[...] * xc_ref[:, 0:D]
        y_ref[...] = y
        yg = y * _silu(z_ref[...].astype(F32))
        for g in range(NG):
            gs = slice(GW * g, GW * (g + 1))
            seg = yg[:, gs]
            ya_ref[:, gs] = seg * _rms(seg) * ng_ref[:, gs]

    return pl.pallas_call(
        body, name="ssd_fwd", grid=(nc,),
        in_specs=[_row_spec(CH, XBC), _row_spec(CH, 128), _row_spec(CH, D, COL_Z), _full_spec((1, 128)),
                  _full_spec((1, D)), _full_spec((1, D)), _full_spec((128, D))],
        out_specs=[_row_spec(CH, D), _row_spec(CH, D), pl.BlockSpec((1, NS, D), lambda i: (i, 0, 0))],
        out_shape=[_sds((s, D)), _sds((s, D)), _sds((nc, NS, D))],
        scratch_shapes=[pltpu.VMEM((NS, D), F32), pltpu.VMEM((128, 128), F32), pltpu.VMEM((128, 128), F32),
                        pltpu.VMEM((CH, D), F32)],
        compiler_params=_cp("arbitrary"),
    )(xbc_c, dt, proj5, a_log, dskip_x, ssm_norm, expand)


def _lru_gates(xr, wab_ref, ba_ref, bx_ref, lam_ref):
    pre = _mdot(xr, wab_ref[...])
    gr = _sig(pre[:, 0:D] + ba_ref[...])
    gi = _sig(pre[:, D:2 * D] + bx_ref[...])
    sp = _softplus(-lam_ref[...])
    la = -LRU_C * gr * sp
    a = jnp.exp(la)
    oms = _one_minus_sq(a, la)
    inv_mult = lax.rsqrt(oms)
    return gr, gi, sp, a, oms * inv_mult, inv_mult


def _blocked_scan(a, u, carry_ref, a_ref, u_ref, c_ref, out_ref, reverse):
    t = a.shape[0]
    ns = t // 8

    def combine(av, uv, idx, n, sh):
        m = (idx < n - sh) if reverse else (idx >= sh)
        by = n - sh if reverse else sh
        return jnp.where(m, av * pltpu.roll(av, by, 0), av), jnp.where(m, uv + av * pltpu.roll(uv, by, 0), uv)

    row = lax.broadcasted_iota(jnp.int32, (t, D), 0)
    rin = jnp.bitwise_and(row, 7)
    for sh in (1, 2, 4):
        m = (rin < 8 - sh) if reverse else (rin >= sh)
        by = t - sh if reverse else sh
        a, u = jnp.where(m, a * pltpu.roll(a, by, 0), a), jnp.where(m, u + a * pltpu.roll(u, by, 0), u)
    a_ref[...] = a
    u_ref[...] = u
    edge = 0 if reverse else 7
    for j in range(ns):
        c_ref[j:j + 1, :] = a_ref[8 * j + edge:8 * j + edge + 1, :]
    at = c_ref[...]
    for j in range(ns):
        c_ref[j:j + 1, :] = u_ref[8 * j + edge:8 * j + edge + 1, :]
    ut = c_ref[...]
    srow = lax.broadcasted_iota(jnp.int32, (ns, D), 0)
    sh = 1
    while sh < ns:
        at, ut = combine(at, ut, srow, ns, sh)
        sh *= 2
    cv = carry_ref[0:1, :]
    ends = ut + at * cv
    last = 0 if reverse else ns - 1
    first = ns - 1 if reverse else 0
    c_ref[...] = jnp.where(srow == first, cv, pltpu.roll(ends, first if reverse else 1, 0))
    carry_ref[0:1, :] = jnp.sum(jnp.where(srow == last, ends, 0.0), axis=0, keepdims=True)
    for j in range(ns):
        sl = slice(8 * j, 8 * j + 8)
        out_ref[sl, :] = u_ref[sl, :] + a_ref[sl, :] * c_ref[j:j + 1, :]


def _lru_fwd(xr, proj5, ya, wab, ba, bx, lam, plan=None):
    s = xr.shape[0]
    t = min(256, s)

    def body(xr_ref, g_ref, ga_ref, gb_ref, ya_ref, wab_ref, ba_ref, bx_ref, lam_ref, h_ref, mg_ref, gr_ref,
             gi_ref, ao_ref, mo_ref, hc_ref, a_ref, u_ref, c_ref):
        @pl.when(pl.program_id(0) == 0)
        def _():
            hc_ref[...] = jnp.zeros_like(hc_ref)

        xrv = xr_ref[...]
        gr, gi, _, a, mult, _ = _lru_gates(xrv, wab_ref, ba_ref, bx_ref, lam_ref)
        gr_ref[...], gi_ref[...], ao_ref[...], mo_ref[...] = gr, gi, a, mult
        _blocked_scan(a, mult * gi * xrv, hc_ref, a_ref, u_ref, c_ref, h_ref, reverse=False)
        yb = h_ref[...] * _gelu(g_ref[...].astype(F32))
        mg_ref[...] = (_sig(ga_ref[...].astype(F32)) * ya_ref[...]
                       + _sig(gb_ref[...].astype(F32)) * yb).astype(mg_ref.dtype)

    return _pcall(
        body, (xr, proj5, proj5, proj5, ya, wab, ba, bx, lam), name="lru_fwd", grid=(s // t,),
        in_specs=[_row_spec(t, D), _row_spec(t, D, COL_G), _row_spec(t, D, COL_GA), _row_spec(t, D, COL_GB),
                  _row_spec(t, D), _full_spec((D, 2 * D), once=True), _full_spec((1, D)), _full_spec((1, D)),
                  _full_spec((1, D))],
        out_specs=[_row_spec(t, D)] * 6,
        out_shape=[_sds((s, D)), _sds((s, D), MXU)] + [_sds((s, D))] * 4,
        scratch_shapes=[pltpu.VMEM((8, D), F32), pltpu.VMEM((t, D), F32), pltpu.VMEM((t, D), F32),
                        pltpu.VMEM((t // 8, D), F32)],
        sem=("arbitrary",), plan=plan)


def _out_up_proj(merged, w_out, x, g2, g3, w_up):
    s = x.shape[0]
    t = min(ROWS_FUSED, s)

    def body(mg_ref, w_ref, x_ref, g2_ref, g3_ref, wu_ref, mix_ref, h1_ref, v_ref, pre_ref):
        mix = _mdot(mg_ref[...], w_ref[...])
        mix_ref[...] = mix
        h1 = x_ref[...] + mix * _rms(mix) * g2_ref[...]
        h1_ref[...] = h1
        v = (h1 * _rms(h1) * g3_ref[...]).astype(v_ref.dtype)
        v_ref[...] = v
        pre_ref[...] = _mdot(v, wu_ref[...]).astype(pre_ref.dtype)

    return pl.pallas_call(
        body, name="out_up_proj", grid=(s // t,),
        in_specs=[_row_spec(t, D), _full_spec((D, D), once=True), _row_spec(t, D), _full_spec((1, D)),
                  _full_spec((1, D)), _full_spec((D, DFF), once=True)],
        out_specs=[_row_spec(t, D), _row_spec(t, D), _row_spec(t, D), _row_spec(t, DFF)],
        out_shape=[_sds((s, D)), _sds((s, D)), _sds((s, D), MXU), _sds((s, DFF), MXU)],
        compiler_params=_cp("parallel"),
    )(merged, w_out, x, g2, g3, w_up)


def _down_loss(pre, w_down, h1, target, g4):
    s = pre.shape[0]
    t = min(ROWS_FUSED, s)

    def body(pre_ref, w_ref, h1_ref, tg_ref, g4_ref, dout_ref, dff_ref, loss_ref, dg4_ref):
        @pl.when(pl.program_id(0) == 0)
        def _():
            loss_ref[...] = jnp.zeros_like(loss_ref)
            dg4_ref[...] = jnp.zeros_like(dg4_ref)

        ff = _mdot(_relu2(pre_ref[...]), w_ref[...])
        r4 = _rms(ff)
        g4v = g4_ref[...]
        diff = h1_ref[...] + ff * r4 * g4v - tg_ref[...]
        sq = jnp.sum(jnp.sum(diff * diff, axis=1, keepdims=True), axis=0, keepdims=True)
        loss_ref[...] += (0.5 / D) * sq
        dout = diff * (1.0 / D)
        dout_ref[...] = dout
        dff, dg = _rms_bwd(ff, r4, g4v, dout)
        dff_ref[...] = dff.astype(dff_ref.dtype)
        dg4_ref[...] += dg

    return pl.pallas_call(
        body, name="down_loss", grid=(s // t,),
        in_specs=[_row_spec(t, DFF), _full_spec((DFF, D), once=True), _row_spec(t, D), _row_spec(t, D),
                  _full_spec((1, D))],
        out_specs=[_row_spec(t, D), _row_spec(t, D), _full_spec((1, 128)), _full_spec((1, D))],
        out_shape=[_sds((s, D)), _sds((s, D), MXU), _sds((1, 128)), _sds((1, D))],
        compiler_params=_cp("arbitrary"),
    )(pre, w_down, h1, target, g4)


def _dv_norms(dpre, w_up, h1, mix, dout, g3, g2, w_out):
    s = h1.shape[0]
    t = min(ROWS_FUSED, s)

    def body(dp_ref, w_ref, h1_ref, mix_ref, dout_ref, g3_ref, g2_ref, wo_ref, dh1_ref, dmix_ref, dmg_ref, dg3_ref,
             dg2_ref):
        @pl.when(pl.program_id(0) == 0)
        def _():
            dg3_ref[...] = jnp.zeros_like(dg3_ref)
            dg2_ref[...] = jnp.zeros_like(dg2_ref)

        dv = _mdot(dp_ref[...], w_ref[...], 1, 1)
        h1 = h1_ref[...]
        dh1n, dg3 = _rms_bwd(h1, _rms(h1), g3_ref[...], dv)
        dh1 = dout_ref[...] + dh1n
        dh1_ref[...] = dh1
        mix = mix_ref[...]
        dmix, dg2 = _rms_bwd(mix, _rms(mix), g2_ref[...], dh1)
        dmix = dmix.astype(dmix_ref.dtype)
        dmix_ref[...] = dmix
        dmg_ref[...] = _mdot(dmix, wo_ref[...], 1, 1)
        dg3_ref[...] += dg3
        dg2_ref[...] += dg2

    return pl.pallas_call(
        body, name="dv_norms", grid=(s // t,),
        in_specs=[_row_spec(t, DFF), _full_spec((D, DFF), once=True), _row_spec(t, D), _row_spec(t, D),
                  _row_spec(t, D), _full_spec((1, D)), _full_spec((1, D)), _full_spec((D, D), once=True)],
        out_specs=[_row_spec(t, D), _row_spec(t, D), _row_spec(t, D), _full_spec((1, D)), _full_spec((1, D))],
        out_shape=[_sds((s, D)), _sds((s, D), MXU), _sds((s, D)), _sds((1, D)), _sds((1, D))],
        compiler_params=_cp("arbitrary"),
    )(dpre, w_up, h1, mix, dout, g3, g2, w_out)


def _lru_bwd(dmerged, ya, xr, h, proj5, gates, wab_t, lam, plan=None):
    s = xr.shape[0]
    t = min(128, s)
    n = s // t
    rs = functools.partial(_rev_spec, t, D, n)

    def body(dm_ref, ya_ref, xr_ref, h_ref, hp_ref, g_ref, ga_ref, gb_ref, gr_ref, gi_ref, a_ref, m_ref, wab_ref,
             lam_ref, dya_ref, d3_ref, dxr_ref, dp2_ref, dlam_ref, dba_ref, dbx_ref, gc_ref, af_ref, an_ref, us_ref,
             c_ref, gs_ref):
        i = pl.program_id(0)

        @pl.when(i == 0)
        def _():
            gc_ref[...] = jnp.zeros_like(gc_ref)
            af_ref[...] = jnp.zeros_like(af_ref)
            dlam_ref[...] = jnp.zeros_like(dlam_ref)
            dba_ref[...] = jnp.zeros_like(dba_ref)
            dbx_ref[...] = jnp.zeros_like(dbx_ref)

        xrv = xr_ref[...]
        gr, gi, a, mult = gr_ref[...], gi_ref[...], a_ref[...], m_ref[...]
        sp = _softplus(-lam_ref[...])
        inv_mult = 1.0 / mult
        hv = h_ref[...]
        dm = dm_ref[...]
        sa = _sig(ga_ref[...].astype(F32))
        sb = _sig(gb_ref[...].astype(F32))
        gel, dgel = _gelu_and_grad(g_ref[...].astype(F32))
        dya = dm * sa
        dya_ref[...] = dya
        dyb = dm * sb
        dybh = dyb * hv
        d3_ref[:, 0:D] = (dybh * dgel).astype(d3_ref.dtype)
        d3_ref[:, D:2 * D] = (dya * ya_ref[...] * (1.0 - sa)).astype(d3_ref.dtype)
        d3_ref[:, 2 * D:3 * D] = (dybh * gel * (1.0 - sb)).astype(d3_ref.dtype)
        row = lax.broadcasted_iota(jnp.int32, (t, D), 0)
        an = jnp.where(row == t - 1, af_ref[0:1, :], pltpu.roll(a, t - 1, 0))
        _blocked_scan(an, dyb * gel, gc_ref, an_ref, us_ref, c_ref, gs_ref, reverse=True)
        gfull = gs_ref[...]
        af_ref[0:1, :] = jnp.sum(jnp.where(row == 0, a, 0.0), axis=0, keepdims=True)
        hlast = jnp.where(i == n - 1, 0.0, hp_ref[7:8, :])
        hprev = jnp.where(row == 0, hlast, pltpu.roll(hv, 1, 0))
        gx = gfull * xrv
        dgi = gx * mult
        dla = a * (gfull * hprev - gx * gi * a * inv_mult)
        dgr = dla * (-LRU_C * sp)
        dsp = jnp.sum(dla * (-LRU_C * gr), axis=0, keepdims=True)
        dlam_ref[...] += dsp * (-_sig(-lam_ref[...]))
        dpr = dgr * gr * (1.0 - gr)
        dpi = dgi * gi * (1.0 - gi)
        dp2_ref[:, 0:D] = dpr.astype(dp2_ref.dtype)
        dp2_ref[:, D:2 * D] = dpi.astype(dp2_ref.dtype)
        dba_ref[...] += jnp.sum(dpr, axis=0, keepdims=True)
        dbx_ref[...] += jnp.sum(dpi, axis=0, keepdims=True)
        dxr_ref[...] = gfull * mult * gi + _mdot(dp2_ref[...], wab_ref[...])

    hp_spec = pl.BlockSpec((8, D), lambda i: (jnp.maximum((n - 1 - i) * (t // 8) - 1, 0), 0))
    wide = lambda c: pl.BlockSpec((t, c), lambda i: (n - 1 - i, 0))
    return _pcall(
        body, (dmerged, ya, xr, h, h, proj5, proj5, proj5, *gates, wab_t, lam), name="lru_bwd", grid=(n,),
        in_specs=[rs(), rs(), rs(), rs(), hp_spec, rs(COL_G), rs(COL_GA), rs(COL_GB), rs(), rs(), rs(), rs(),
                  _full_spec((2 * D, D), once=True), _full_spec((1, D))],
        out_specs=[rs(), wide(3 * D), rs(), wide(2 * D), _full_spec((1, D)), _full_spec((1, D)), _full_spec((1, D))],
        out_shape=[_sds((s, D)), _sds((s, 3 * D), MXU), _sds((s, D)), _sds((s, 2 * D), MXU), _sds((1, D)),
                   _sds((1, D)), _sds((1, D))],
        scratch_shapes=[pltpu.VMEM((8, D), F32), pltpu.VMEM((8, D), F32), pltpu.VMEM((t, D), F32),
                        pltpu.VMEM((t, D), F32), pltpu.VMEM((t // 8, D), F32), pltpu.VMEM((t, D), F32)],
        sem=("arbitrary",), plan=plan)


def _ssd_bwd(dya, y, proj5, xbc_c, dt, states, a_log, dskip_x, ssm_norm, expand, reduce_, plan=None):
    s = xbc_c.shape[0]
    nc = s // CH
    rv = functools.partial(_rev_spec, CH)

    def body(dya_ref, y_ref, z_ref, xc_ref, dt_ref, st_ref, alog_ref, dsk_ref, ng_ref, e_ref, et_ref, dz_ref,
             dxc_ref, ddt_ref, dng_ref, ddsk_ref, dalog_ref, dh_ref, at_ref, dtt_ref, dat_ref, ddtt_ref, dy_ref,
             yoffdy_ref, xbds_ref):
        @pl.when(pl.program_id(0) == 0)
        def _():
            dh_ref[...] = jnp.zeros_like(dh_ref)
            dng_ref[...] = jnp.zeros_like(dng_ref)
            ddsk_ref[...] = jnp.zeros_like(ddsk_ref)
            dalog_ref[...] = jnp.zeros_like(dalog_ref)

        cs = _ssd_chunk_setup(dt_ref, alog_ref, e_ref, at_ref, dtt_ref)
        lane, row = cs["lane"], cs["row"]
        et = et_ref[...]
        for g in range(NG):
            gs = slice(GW * g, GW * (g + 1))
            yv = y_ref[:, gs]
            zv = z_ref[:, gs].astype(F32)
            sz = _silu(zv)
            yg = yv * sz
            dyav = dya_ref[:, gs]
            dyg, dng = _rms_bwd(yg, _rms(yg), ng_ref[:, gs], dyav)
            dng_ref[:, gs] += dng
            dy_ref[:, gs] = dyg * sz
            dz_ref[:, gs] = (dyg * yv * _dsilu(zv)).astype(dz_ref.dtype)
        dyv = dy_ref[...]
        xs = xc_ref[:, 0:D]
        ddsk_ref[...] += jnp.sum(dyv * xs, axis=0, keepdims=True)
        dxc_ref[:, 0:D] = dyv * dsk_ref[...]
        dat_ref[...] = jnp.zeros_like(dat_ref)
        ddtt_ref[...] = jnp.zeros_like(ddtt_ref)
        hh = jnp.sum(dh_ref[...] * st_ref[0], axis=0, keepdims=True)
        deal = jnp.max(_xdot(jnp.broadcast_to(hh, (8, D)), et, 3), axis=0, keepdims=True)
        d_acum = jnp.zeros((CH, 128), F32)
        for g in range(NG):
            gs = slice(GW * g, GW * (g + 1))
            bs_ = slice(D + NS * g, D + NS * (g + 1))
            cs_ = slice(D + NG * NS + NS * g, D + NG * NS + NS * (g + 1))
            bg = xc_ref[:, bs_]
            cg = xc_ref[:, cs_]
            cb = _mdot(cg, bg, 1, 1)
            hg = st_ref[0, :, gs]
            dhg = dh_ref[:, gs]
            dyg_ = dy_ref[:, gs]
            xsg = xc_ref[:, gs]
            ea = cs["ea_x"][:, gs]
            wsx = cs["ws_x"][:, gs]
            dp = dyg_ * ea
            yoffdy_ref[:, gs] = dp * _mdot(cg, hg)
            dc = _mdot(dp, hg, 1, 1)
            dhprev = _mdot(cg, dp, 0, 0)
            bds = _mdot(bg, dhg)
            dxc_ref[:, gs] += wsx * bds
            xbds_ref[:, gs] = xsg * bds
            db = _mdot(xsg * wsx, dhg, 1, 1)
            dh_ref[:, gs] = dhprev + cs["eal_x"][:, gs] * dhg
            dcbs = jnp.zeros((CH, CH), F32)
            for j in range(4 * g, 4 * g + 4):
                ps = slice(128 * j, 128 * (j + 1))
                xp = xc_ref[:, ps]
                dyp = dy_ref[:, ps]
                dxacc = jnp.zeros((CH, 128), F32)
                for hf in range(2):
                    hd = 2 * j + hf
                    ld, rowdt = _head_decay(cs, at_ref, dtt_ref, hd)
                    hm = (lane >= HP) if hf else (lane < HP)
                    dym = jnp.where(hm, dyp, 0.0)
                    w = cb * ld * rowdt
                    dw = _mdot(dym, jnp.where(hm, xp, 0.0), 1, 1)
                    dxacc = dxacc + _mdot(w, dym, 0, 0)
                    nm = dw * w
                    ddtt_ref[hd:hd + 1, :] += jnp.sum(dw * cb * ld, axis=0, keepdims=True)
                    d_acum = d_acum + jnp.where(lane == hd, jnp.sum(nm, axis=1, keepdims=True), 0.0)
                    dat_ref[hd:hd + 1, :] -= jnp.sum(nm, axis=0, keepdims=True)
                    dcbs = dcbs + dw * ld * rowdt
                dxc_ref[:, ps] += dxacc
            dxc_ref[:, bs_] = db + _mdot(dcbs, cg, 0, 0)
            dxc_ref[:, cs_] = dc + _mdot(dcbs, bg)
        dws = _xdot(xbds_ref[...], et, 2)
        ws = cs["ws"]
        d_acum = d_acum - dws * ws + _xdot(yoffdy_ref[...], et, 2) + dat_ref[...].T
        d_alast = jnp.sum(dws * ws, axis=0, keepdims=True) + deal * cs["eal"]
        d_acum = d_acum + jnp.where(row == CH - 1, d_alast, 0.0)
        triu = row <= lane
        d_adt = _xdot(triu.astype(F32), d_acum, 3, split_b=True)
        ddt_ref[...] = dws * jnp.exp(cs["alast"] - cs["acum"]) + ddtt_ref[...].T + d_adt * cs["a"]
        dalog_ref[...] += jnp.sum(d_adt * cs["dtv"], axis=0, keepdims=True) * cs["a"]

    return _pcall(
        body, (dya, y, proj5, xbc_c, dt, states, a_log, dskip_x, ssm_norm, expand, reduce_), name="ssd_bwd",
        grid=(nc,),
        in_specs=[rv(D, nc), rv(D, nc), rv(D, nc, COL_Z), rv(XBC, nc), rv(128, nc),
                  pl.BlockSpec((1, NS, D), lambda i: (nc - 1 - i, 0, 0)), _full_spec((1, 128)), _full_spec((1, D)),
                  _full_spec((1, D)), _full_spec((128, D)), _full_spec((D, 128))],
        out_specs=[rv(D, nc), rv(XBC, nc), rv(128, nc), _full_spec((1, D)), _full_spec((1, D)),
                   _full_spec((1, 128))],
        out_shape=[_sds((s, D), MXU), _sds((s, XBC)), _sds((s, 128)), _sds((1, D)), _sds((1, D)), _sds((1, 128))],
        scratch_shapes=[pltpu.VMEM((NS, D), F32), pltpu.VMEM((128, 128), F32), pltpu.VMEM((128, 128), F32),
                        pltpu.VMEM((128, 128), F32), pltpu.VMEM((128, 128), F32), pltpu.VMEM((CH, D), F32),
                        pltpu.VMEM((CH, D), F32), pltpu.VMEM((CH, D), F32)],
        sem=("arbitrary",), plan=plan)


def _conv_bwd(dxbc_c, dsilu, dxr, ddt, xbc_raw, proj5, dt_raw, cw_s, cw_l, dt_bias, plan=None):
    s = xbc_raw.shape[0]
    t = min(256, s)
    n = s // t

    def body(dxc_ref, dsl_ref, dxr_ref, ddt_ref, xs_ref, xl_ref, dtr_ref, cws_ref, cwl_ref, dtb_ref, dxs_ref,
             dxl_ref, ddtr_ref, dcws_ref, dcbs_ref, dcwl_ref, dcbl_ref, ddtb_ref, ds_ref, dl_ref):
        @pl.when(pl.program_id(0) == 0)
        def _():
            ds_ref[t:t + 8, :] = jnp.zeros((8, XBC), F32)
            dl_ref[t:t + 8, :] = jnp.zeros((8, D), F32)
            for r in (dcws_ref, dcbs_ref, dcwl_ref, dcbl_ref, ddtb_ref):
                r[...] = jnp.zeros_like(r)

        ds_ref[0:t, :] = dxc_ref[...] * dsl_ref[...].astype(F32)
        dl_ref[0:t, :] = dxr_ref[...]

        def back(dbuf, x_ref, w_ref, dx_ref, dw_ref, db_ref):
            xv = x_ref[...].astype(F32)
            dpre = dbuf[0:t, :]
            dx = w_ref[3:4, :] * dpre
            dw_ref[3:4, :] += jnp.sum(dpre * xv, axis=0, keepdims=True)
            db_ref[...] += jnp.sum(dpre, axis=0, keepdims=True)
            for k in (1, 2, 3):
                ahead = dbuf[k:t + k, :]
                dx = dx + w_ref[3 - k:4 - k, :] * ahead
                dw_ref[3 - k:4 - k, :] += jnp.sum(ahead * xv, axis=0, keepdims=True)
            dx_ref[...] = dx.astype(dx_ref.dtype)
            dbuf[t:t + 8, :] = dbuf[0:8, :]

        back(ds_ref, xs_ref, cws_ref, dxs_ref, dcws_ref, dcbs_ref)
        back(dl_ref, xl_ref, cwl_ref, dxl_ref, dcwl_ref, dcbl_ref)
        ddtr = ddt_ref[...] * _sig(dtr_ref[...] + dtb_ref[...])
        ddtr_ref[...] = ddtr.astype(ddtr_ref.dtype)
        ddtb_ref[...] += jnp.sum(ddtr, axis=0, keepdims=True)

    rv = functools.partial(_rev_spec, t)
    return _pcall(
        body, (dxbc_c, dsilu, dxr, ddt, xbc_raw, proj5, dt_raw, cw_s, cw_l, dt_bias), name="conv_bwd", grid=(n,),
        in_specs=[rv(XBC, n), rv(XBC, n), rv(D, n), rv(128, n), rv(XBC, n), rv(D, n, COL_XL), rv(128, n),
                  _full_spec((4, XBC)), _full_spec((4, D)), _full_spec((1, 128))],
        out_specs=[rv(XBC, n), rv(D, n), rv(128, n), _full_spec((4, XBC)), _full_spec((1, XBC)), _full_spec((4, D)),
                   _full_spec((1, D)), _full_spec((1, 128))],
        out_shape=[_sds((s, XBC), MXU), _sds((s, D), MXU), _sds((s, 128), MXU), _sds((4, XBC)), _sds((1, XBC)),
                   _sds((4, D)), _sds((1, D)), _sds((1, 128))],
        scratch_shapes=[pltpu.VMEM((t + 8, XBC), F32), pltpu.VMEM((t + 8, D), F32)],
        sem=("arbitrary",), plan=plan)


def _du_norm(d3, dz, dxl, dxbc, ddtr, w5, wxbc, wdt, x, dh1, g1, plan=None):
    s = x.shape[0]
    t = min(ROWS_FUSED, s)

    def body(d3_ref, dz_ref, dxl_ref, dxbc_ref, ddtr_ref, w5_ref, wx_ref, wd_ref, x_ref, dh1_ref, g1_ref, dx_ref,
             dg1_ref):
        @pl.when(pl.program_id(0) == 0)
        def _():
            dg1_ref[...] = jnp.zeros_like(dg1_ref)

        du = (_mdot(dxbc_ref[...], wx_ref[...]) + _mdot(ddtr_ref[...], wd_ref[...])
              + _mdot(d3_ref[...], w5_ref[0:3 * D, :])
              + _mdot(dz_ref[...], w5_ref[COL_Z * D:(COL_Z + 1) * D, :])
              + _mdot(dxl_ref[...], w5_ref[COL_XL * D:(COL_XL + 1) * D, :]))
        xv = x_ref[...]
        dxn, dg1 = _rms_bwd(xv, _rms(xv), g1_ref[...], du)
        dx_ref[...] = dh1_ref[...] + dxn
        dg1_ref[...] += dg1

    return _pcall(
        body, (d3, dz, dxl, dxbc, ddtr, w5, wxbc, wdt, x, dh1, g1), name="du_norm", grid=(s // t,),
        in_specs=[_row_spec(t, 3 * D), _row_spec(t, D), _row_spec(t, D), _row_spec(t, XBC), _row_spec(t, 128),
                  _full_spec((5 * D, D), once=True), _full_spec((XBC, D), once=True),
                  _full_spec((128, D), once=True), _row_spec(t, D), _row_spec(t, D), _full_spec((1, D))],
        out_specs=[_row_spec(t, D), _full_spec((1, D))],
        out_shape=[_sds((s, D)), _sds((1, D))],
        sem=("arbitrary",), plan=plan)


def _adamw(w, g, m, v, name):
    r, c = w.shape
    t = r
    if r * c > 256 * 1024:
        t = next(cand for cand in (512, 256, 128, 64, 32, 16, 8) if r % cand == 0 and cand * c <= 512 * 1024)
    bc1 = 1.0 - ADAM_B1 ** ADAM_STEP
    bc2 = 1.0 - ADAM_B2 ** ADAM_STEP

    def body(w_ref, g_ref, m_ref, v_ref, d_ref, nm_ref, nv_ref):
        gv = g_ref[...]
        nm = ADAM_B1 * m_ref[...] + (1.0 - ADAM_B1) * gv
        nv = ADAM_B2 * v_ref[...] + (1.0 - ADAM_B2) * (gv * gv)
        nm_ref[...] = nm
        nv_ref[...] = nv
        d_ref[...] = -ADAM_LR * ((nm / bc1) / (jnp.sqrt(nv / bc2) + ADAM_EPS) + ADAM_WD * w_ref[...])

    spec = pl.BlockSpec((t, c), lambda i: (i, 0))
    return pl.pallas_call(
        body, name=name, grid=(r // t,), in_specs=[spec] * 4, out_specs=[spec] * 3,
        out_shape=[_sds((r, c))] * 3, compiler_params=_cp("parallel"),
    )(w, g, m, v)


def _half_blocks(shape, axis):
    r, c = shape
    if axis == 0:
        t = 256 if (r // 2) % 256 == 0 else 128
        nb = (r // 2) // t
        return (t, c), nb, (lambda i: (i, 0)), (lambda i: (i % nb, 0))
    nb = (c // 2) // 128
    return (r, 128), nb, (lambda i: (0, i)), (lambda i: (0, i % nb))


def _adamw_halves(w, g_mine, g_other, m, v, cidx, name, axis=0):
    r, c = w.shape
    blk, nb, whole, part = _half_blocks(w.shape, axis)
    bc1 = 1.0 - ADAM_B1 ** ADAM_STEP
    bc2 = 1.0 - ADAM_B2 ** ADAM_STEP

    def body(c_ref, w_ref, gm_ref, go_ref, m_ref, v_ref, g_ref, d_ref, nm_ref, nv_ref):
        mine = (pl.program_id(0) // nb) == c_ref[0]
        gv = jnp.where(mine, gm_ref[...], go_ref[...])
        g_ref[...] = gv
        nm = ADAM_B1 * m_ref[...] + (1.0 - ADAM_B1) * gv
        nv = ADAM_B2 * v_ref[...] + (1.0 - ADAM_B2) * (gv * gv)
        nm_ref[...] = nm
        nv_ref[...] = nv
        d_ref[...] = -ADAM_LR * ((nm / bc1) / (jnp.sqrt(nv / bc2) + ADAM_EPS) + ADAM_WD * w_ref[...])

    spec = pl.BlockSpec(blk, lambda i, c_ref: whole(i))
    half = pl.BlockSpec(blk, lambda i, c_ref: part(i))
    return pl.pallas_call(
        body, name=name,
        grid_spec=pltpu.PrefetchScalarGridSpec(num_scalar_prefetch=1, grid=(2 * nb,),
                                               in_specs=[spec, half, half, spec, spec], out_specs=[spec] * 4),
        out_shape=[_sds((r, c))] * 4, compiler_params=_cp("parallel"),
    )(cidx, w, g_mine, g_other, m, v)


def _block_diag(w):
    eye = jnp.eye(NH, dtype=w.dtype)
    return (w[:, :, None, :] * eye[:, None, :, None]).reshape(D, D)


def _diag_blocks(full):
    eye = jnp.eye(NH, dtype=full.dtype)
    return (full.reshape(NH, HP, NH, HP) * eye[:, None, :, None]).sum(axis=2)


def _pad_lanes(v, n=128):
    return jnp.pad(v, ((0, 0), (0, n - v.shape[1])))


def _local_step(x, target, p, dist=None):
    heads = jnp.arange(D, dtype=jnp.int32) // HP
    expand = (jnp.arange(128, dtype=jnp.int32)[:, None] == heads[None, :]).astype(F32)
    reduce_ = expand.T
    dskip_x = jnp.repeat(p["d_skip"], HP, axis=1)
    a_log = _pad_lanes(p["a_log"])
    dt_bias = _pad_lanes(p["dt_bias"])
    wab = jnp.concatenate([_block_diag(p["lru_wa"]), _block_diag(p["lru_wx"])], axis=1).astype(MXU)
    ba = p["lru_ba"].reshape(1, D)
    bx = p["lru_bx"].reshape(1, D)

    def hosted(key, fn):
        plan = dist.plan(key) if dist is not None else None
        if plan is None:
            return fn(plan=None)
        outs, got = fn(plan=plan)
        dist.done(key, got, p)
        return outs

    u = hosted("norm_u", functools.partial(_norm_cast, x, p["norm_mix_pre"], "norm_u"))
    w5, wxbc, wdt = p["w5"], p["wxbc"], p["wdt"]
    proj5 = hosted("proj5", functools.partial(_matmul, u, w5, name="proj5", tb=True, tm=1024, out_dtype=MXU))
    xbc_raw = _matmul(u, wxbc, name="proj_xbc", tb=True, tn=XBC)
    dt_raw = _matmul(u, wdt, name="proj_dt", tb=True)
    xbc_c, dsilu, xr, dt = _conv_fwd(xbc_raw, proj5, dt_raw, p["conv_ssm_w"], p["conv_ssm_b"], p["conv_lru_w"],
                                     p["conv_lru_b"], dt_bias)
    y, ya, states = _ssd_fwd(xbc_c, dt, proj5, a_log, dskip_x, p["ssm_norm"], expand)
    h, merged, *gates = hosted("lru_fwd", functools.partial(_lru_fwd, xr, proj5, ya, wab, ba, bx, p["lru_lambda"]))
    mix, h1, v, pre = _out_up_proj(merged, p["w_out"], x, p["norm_mix_post"], p["norm_mlp_pre"], p["w_up"])
    dout, dff, loss, dg4 = _down_loss(pre, p["w_down"], h1, target, p["norm_mlp_post"])

    dpre = _matmul(dff, p["w_down"], name="d_pre", tb=True, tm=1024, out_dtype=MXU,
                   epi=lambda r, pr: r * (2.0 * jnp.maximum(pr.astype(F32), 0.0)), epi_args=(pre,))
    g_w_down = _matmul(pre, dff, name="dw_down", ta=True, tm=1024, tn=1024, tk=TK_GRAD, a_fn=_relu2)
    dh1, dmix, dmerged, dg3, dg2 = _dv_norms(dpre, p["w_up"], h1, mix, dout, p["norm_mlp_pre"], p["norm_mix_post"],
                                             p["w_out"])
    g_w_up = _matmul(v, dpre, name="dw_up", ta=True, tm=1024, tn=1024, tk=TK_GRAD)
    g_w_out = _matmul(merged, dmix, name="dw_out", ta=True, tm=1024, tn=1024, tk=TK_GRAD)
    if dist is not None:
        dist.early_grads(w_down=g_w_down, w_up=g_w_up, w_out=g_w_out)
    dya, d3, dxr, dp2, dlam, dba, dbx = hosted("lru_bwd", functools.partial(
        _lru_bwd, dmerged, ya, xr, h, proj5, gates, wab.T, p["lru_lambda"]))
    g_wab = _matmul(xr, dp2, name="dw_lru", ta=True, tm=1024, tn=1024, tk=TK_GRAD)
    g_wa, g_wx = _diag_blocks(g_wab[:, :D]), _diag_blocks(g_wab[:, D:])
    dz, dxbc_c, ddt, dng, ddsk, dalog = hosted("ssd_bwd", functools.partial(
        _ssd_bwd, dya, y, proj5, xbc_c, dt, states, a_log, dskip_x, p["ssm_norm"], expand, reduce_))
    (dxbc, dxl, ddtr, dcws, dcbs, dcwl, dcbl, ddtb) = hosted("conv_bwd", functools.partial(
        _conv_bwd, dxbc_c, dsilu, dxr, ddt, xbc_raw, proj5, dt_raw, p["conv_ssm_w"], p["conv_lru_w"], dt_bias))
    gw3 = _matmul(d3, u, name="dw_in_lru", ta=True, tm=1024, tn=1024, tk=TK_GRAD)
    gwz = _matmul(dz, u, name="dw_in_z", ta=True, tm=1024, tn=1024, tk=TK_GRAD)
    gwxl = _matmul(dxl, u, name="dw_in_xl", ta=True, tm=1024, tn=1024, tk=TK_GRAD)
    gwxbc = _matmul(dxbc, u, name="dw_in_xbc", ta=True, tm=XBC, tn=1024, tk=TK_GRAD)
    gwdt = _matmul(ddtr, u, name="dw_in_dt", ta=True, tm=128, tn=1024, tk=TK_GRAD)
    g_w_in_t = jnp.concatenate([gwz, gwxbc, gwdt[:NH], gw3[:D], gwxl, gw3[D:2 * D], gw3[2 * D:]], axis=0)
    grads = {
        "w_in_t": g_w_in_t, "conv_ssm_w": dcws, "conv_ssm_b": dcbs, "dt_bias": ddtb[:, :NH],
        "a_log": dalog[:, :NH], "d_skip": ddsk.reshape(NH, HP).sum(axis=1)[None, :], "ssm_norm": dng,
        "conv_lru_w": dcwl, "conv_lru_b": dcbl, "lru_wa": g_wa, "lru_ba": dba.reshape(NH, HP), "lru_wx": g_wx,
        "lru_bx": dbx.reshape(NH, HP), "lru_lambda": dlam, "w_out": g_w_out, "norm_mix_post": dg2,
        "norm_mlp_pre": dg3, "w_up": g_w_up, "w_down": g_w_down, "norm_mlp_post": dg4,
    }
    if dist is not None:
        dist.late_grads(grads, loss[0, 0])
    grad_x, grads["norm_mix_pre"] = hosted("du_norm", functools.partial(
        _du_norm, d3, dz, dxl, dxbc, ddtr, w5, wxbc, wdt, x, dh1, p["norm_mix_pre"]))
    return loss[0, 0], grad_x, grads


def _split_w_in_t(shards):
    per = shards.shape[1]

    def rows(lo, hi):
        parts = []
        for k in range(shards.shape[0]):
            a, b = max(lo, k * per), min(hi, (k + 1) * per)
            if a < b:
                parts.append(shards[k, a - k * per:b - k * per])
        return parts

    bounds = [0, D, D + XBC, D + XBC + NH, 2 * D + XBC + NH, 3 * D + XBC + NH, 4 * D + XBC + NH, 5 * D + XBC + NH]
    z, xbc, dtc, g, xl, ga, gb = [rows(lo, hi) for lo, hi in zip(bounds[:-1], bounds[1:])]
    return (jnp.concatenate(g + ga + gb + z + xl, axis=0), jnp.concatenate(xbc, axis=0),
            jnp.pad(jnp.concatenate(dtc, axis=0), ((0, 128 - NH), (0, 0))))


COMM = BF16


def _place():
    x, y, c = lax.axis_index("x"), lax.axis_index("y"), lax.axis_index("c")
    chips = [(1 - x, y), (x, 1 - y), (1 - x, 1 - y)]
    return x, y, c, chips


def _remote(src, dst, send_sem, recv_sem, to):
    return pltpu.make_async_remote_copy(src_ref=src, dst_ref=dst, send_sem=send_sem, recv_sem=recv_sem, device_id=to,
                                        device_id_type=MESH)


def _gather_plan(big, small=(), axes=None):
    nb = len(big)
    arrs = list(big) + list(small)
    na = len(arrs)
    axes = list(axes or [0] * nb)

    def half(ref, a, k, which):
        h = arrs[a].shape[axes[a]] // 2
        cut = (pl.ds(which * h, h),) if axes[a] == 0 else (slice(None), pl.ds(which * h, h))
        return ref.at[cut] if k is None else ref.at[(k,) + cut]

    def direct(ins, outs, send, recv):
        x, y, c, chips = _place()
        k = 2 * x + y
        cps = []
        for a in range(na):
            src, dst = (half(ins[a], a, None, c), half(outs[a], a, k, c)) if a < nb else (ins[a], outs[a].at[k])
            cps += [_remote(src, dst, send.at[a, j], recv.at[a, j], (cx, cy, c)) for j, (cx, cy) in enumerate(chips)]
        return cps

    def passed(outs, send, recv):
        x, y, c, chips = _place()
        cps = []
        for j, (cx, cy) in enumerate(chips):
            for a in range(nb):
                got = half(outs[a], a, 2 * cx + cy, c)
                cps.append(_remote(got, got, send.at[a, 3 + j], recv.at[a, 3 + j], (x, y, 1 - c)))
        return cps

    def start(ins, outs, sems):
        for cp in direct(ins, outs, *sems[0]):
            cp.start()

    def mid(ins, outs, sems):
        send, recv = sems[0]
        _, _, c, chips = _place()
        fwd = passed(outs, send, recv)
        for j, (cx, cy) in enumerate(chips):
            kj = 2 * cx + cy
            for a in range(na):
                got = half(outs[a], a, kj, c) if a < nb else outs[a].at[kj]
                _remote(got, got, send.at[a, j], recv.at[a, j], (cx, cy, c)).wait_recv()
                if a < nb:
                    fwd[j * nb + a].start()

    def finish(ins, outs, sems):
        send, recv = sems[0]
        x, y, c, chips = _place()
        for j, (cx, cy) in enumerate(chips):
            for a in range(nb):
                got = half(outs[a], a, 2 * cx + cy, 1 - c)
                _remote(got, got, send.at[a, 3 + j], recv.at[a, 3 + j], (x, y, 1 - c)).wait_recv()
        for cp in direct(ins, outs, send, recv) + passed(outs, send, recv):
            cp.wait_send()

    return _Plan(arrs, [_sds((NCHIP,) + a.shape, a.dtype) for a in arrs], [(na, 6)], start, finish, mid)


def _own_shards(gathered, shards):
    kchip = 2 * lax.axis_index("x") + lax.axis_index("y")
    return [lax.dynamic_update_index_in_dim(o, a, kchip, 0) for o, a in zip(gathered, shards)]


def _swap_plan(ins, outs, sems, copies):
    def start(i, o, s):
        for cp in copies(i, o, *s[0]):
            cp.start()

    def finish(i, o, s):
        for cp in copies(i, o, *s[0]):
            cp.wait()

    return _Plan(ins, outs, [sems], start, finish)


def _half_shape(shape, axis):
    return tuple(d // 2 if i == axis else d for i, d in enumerate(shape))


def _pair_exchange_plan(gs, axis=1):
    def copies(ins, outs, send, recv):
        x, y, c, _ = _place()
        cps = []
        for a in range(len(gs)):
            h = ins[a].shape[axis] // 2
            theirs = pl.ds((1 - c) * h, h)
            src = ins[a].at[:, theirs] if axis == 1 else ins[a].at[:, :, theirs]
            cps.append(_remote(src, outs[a], send.at[a], recv.at[a], (x, y, 1 - c)))
        return cps

    return _swap_plan(gs, [_sds(_half_shape(g.shape, axis), g.dtype) for g in gs], (len(gs),), copies)


def _pair_add(g, got, cidx, name, axis=1):
    half = _half_shape(g.shape, axis)
    blk, nt, _, part = _half_blocks(g.shape[1:], axis - 1)

    def body(c_ref, g_ref, o_ref, p_ref, pc_ref):
        sm = g_ref[...] + o_ref[...]
        p_ref[...] = sm
        pc_ref[...] = sm.astype(pc_ref.dtype)

    def mine(k, i, c_ref):
        j = c_ref[0] * nt + i
        return (k, j, 0) if axis == 1 else (k, 0, j)

    spec = pl.BlockSpec((1,) + blk, lambda k, i, c_ref: (k,) + part(i))
    return pl.pallas_call(
        body, name=name,
        grid_spec=pltpu.PrefetchScalarGridSpec(
            num_scalar_prefetch=1, grid=(g.shape[0], nt),
            in_specs=[pl.BlockSpec((1,) + blk, mine), spec], out_specs=[spec, spec]),
        out_shape=[_sds(half), _sds(half, COMM)],
        compiler_params=_cp("parallel", "parallel"),
    )(cidx, g, got)


def _chip_exchange_plan(ps):
    def copies(ins, outs, send, recv):
        _, _, c, chips = _place()
        return [_remote(ins[a].at[2 * cx + cy], outs[a].at[j], send.at[a, j], recv.at[a, j], (cx, cy, c))
                for a in range(len(ps)) for j, (cx, cy) in enumerate(chips)]

    return _swap_plan(ps, [_sds((NCHIP - 1,) + p.shape[1:], p.dtype) for p in ps], (len(ps), 3), copies)


def _shard_sum(p, got, kidx, name, axis=1):
    full = tuple(2 * d if i == axis - 1 else d for i, d in enumerate(p.shape[1:]))
    blk, nt, _, part = _half_blocks(full, axis - 1)

    def body(k_ref, p_ref, g_ref, o_ref):
        sm = p_ref[0]
        for j in range(NCHIP - 1):
            sm = sm + g_ref[j].astype(F32)
        o_ref[...] = sm

    return pl.pallas_call(
        body, name=name,
        grid_spec=pltpu.PrefetchScalarGridSpec(
            num_scalar_prefetch=1, grid=(nt,),
            in_specs=[pl.BlockSpec((1,) + blk, lambda i, k_ref: (k_ref[0],) + part(i)),
                      pl.BlockSpec((NCHIP - 1,) + blk, lambda i, k_ref: (0,) + part(i))],
            out_specs=pl.BlockSpec(blk, lambda i, k_ref: part(i))),
        out_shape=_sds(p.shape[1:]),
        compiler_params=_cp("parallel"),
    )(kidx, p, got)


def _pair_swap_plan(rs):
    def copies(ins, outs, send, recv):
        x, y, c, _ = _place()
        return [_remote(ins[a], outs[a], send.at[a], recv.at[a], (x, y, 1 - c)) for a in range(len(rs))]

    return _swap_plan(rs, [_sds(r.shape, r.dtype) for r in rs], (len(rs),), copies)


def _allgather8_plan(v):
    def pieces(ins, outs, send, recv):
        x, y, c, chips = _place()
        me, sibling = (x, y, c), (x, y, 1 - c)

        def copy(k, block, to, src=None):
            px, py, pc = block
            slot = outs[0].at[4 * px + 2 * py + pc]
            return _remote(slot if src is None else src, slot, send.at[k], recv.at[k], to)

        first = [copy(0, me, sibling, src=ins[0])] + [copy(1 + j, me, (*chip, c), src=ins[0])
                                                      for j, chip in enumerate(chips)]
        passed = [copy(4 + j, (*chip, c), sibling) for j, chip in enumerate(chips)]
        arrivals = [copy(1 + j, (*chip, c), me) for j, chip in enumerate(chips)]
        late = [copy(0, sibling, me)] + [copy(4 + j, (*chip, 1 - c), me) for j, chip in enumerate(chips)]
        return first, passed, arrivals, late

    def start(ins, outs, sems):
        for cp in pieces(ins, outs, *sems[0])[0]:
            cp.start()

    def mid(ins, outs, sems):
        _, passed, arrivals, _ = pieces(ins, outs, *sems[0])
        for got, fwd in zip(arrivals, passed):
            got.wait_recv()
            fwd.start()

    def finish(ins, outs, sems):
        first, passed, _, late = pieces(ins, outs, *sems[0])
        for got in late:
            got.wait_recv()
        for cp in first + passed:
            cp.wait_send()

    return _Plan([v], [_sds((8,) + v.shape, v.dtype)], [(7,)], start, finish, mid)


def _own_block(gathered, v):
    me = 4 * lax.axis_index("x") + 2 * lax.axis_index("y") + lax.axis_index("c")
    return lax.dynamic_update_index_in_dim(gathered, v, me, 0)


def _sum_devices(allv, name):
    _, r, _ = allv.shape

    def body(a_ref, o_ref):
        sm = a_ref[0]
        for d in range(1, 8):
            sm = sm + a_ref[d]
        o_ref[...] = sm

    return pl.pallas_call(
        body, name=name, grid=(1,), in_specs=[_full_spec((8, r, 128))], out_specs=_full_spec((r, 128)),
        out_shape=_sds((r, 128)), compiler_params=_cp("arbitrary"),
    )(allv)


def _pack(arrs):
    flat = jnp.concatenate([a.reshape(-1) for a in arrs])
    return jnp.pad(flat, (0, (-flat.shape[0]) % 1024)).reshape(-1, 128)


def _unpack(packed, shapes):
    flat, outs, off = packed.reshape(-1), [], 0
    for shp in shapes:
        n = math.prod(shp)
        outs.append(flat[off:off + n].reshape(shp))
        off += n
    return outs


BIG = ("w_in", "w_out", "w_up", "w_down")
CONV = ("conv_ssm_w", "conv_lru_w")
WEIGHTS = ("norm_mix_pre", "w_in", "conv_ssm_w", "conv_ssm_b", "dt_bias", "a_log", "d_skip", "ssm_norm", "conv_lru_w",
           "conv_lru_b", "lru_wa", "lru_ba", "lru_wx", "lru_bx", "lru_lambda", "w_out", "norm_mix_post",
           "norm_mlp_pre", "w_up", "w_down", "norm_mlp_post")
SMALL = tuple(n for n in WEIGHTS if n not in BIG and n not in CONV)
EARLY = ("w_down", "w_up", "w_out")


def _cat_cols(g):
    return jnp.concatenate([g[k] for k in range(NCHIP)], axis=1)


class _Dist:
    def __init__(self, shards, first, cidx, kidx):
        self.shards, self.first, self.cidx, self.kidx = shards, first, cidx, kidx
        self.halves = {}

    def early_grads(self, w_down, w_up, w_out):
        self.shard_major = [w_down.reshape(NCHIP, D, D), jnp.stack([w_up[:, D * k:D * (k + 1)] for k in range(NCHIP)]),
                            w_out.reshape(NCHIP, D // NCHIP, D)]

    def late_grads(self, grads, loss):
        g_in = grads["w_in_t"][None]
        got, = _run_plan(_pair_exchange_plan([g_in], axis=2), "grad_pair_exchange_w_in")
        p_all, pc_all = _pair_add(g_in, got, self.cidx, "grad_pair_add_w_in", axis=2)
        self.p_in = lax.dynamic_slice_in_dim(p_all[0], self.kidx[0] * W_IN_SHARD, W_IN_SHARD, axis=0)[None]
        self.pc_in = pc_all.reshape(NCHIP, W_IN_SHARD, D // 2)
        self.small_names = [n for n in SMALL + CONV if n != "norm_mix_pre"]
        self.small_shapes = [grads[n].shape for n in self.small_names] + [(1,)]
        self.packed_small = _pack([grads[n] for n in self.small_names] + [loss.reshape(1)])

    def plan(self, key):
        if key == "norm_u":
            return _gather_plan(self.first[:1], self.first[1:], axes=[1])
        if key == "proj5":
            return _gather_plan([self.shards["w_out"], self.shards["w_up"]])
        if key == "lru_fwd":
            return _gather_plan([self.shards["w_down"]])
        if key == "lru_bwd":
            return _pair_exchange_plan(self.shard_major)
        if key == "ssd_bwd":
            return _chip_exchange_plan([pc for _, pc in self.pair])
        if key == "conv_bwd":
            return _pair_swap_plan(self.mine)
        if key == "du_norm":
            return _merge_plans(_allgather8_plan(self.packed_small), _chip_exchange_plan([self.pc_in]))
        return None

    def done(self, key, got, p):
        if key == "norm_u":
            g_in, g_cs, g_cl = _own_shards(got, self.first)
            w5, wxbc, wdt = _split_w_in_t(g_in)
            p.update(w5=w5, wxbc=wxbc, wdt=wdt, conv_ssm_w=_cat_cols(g_cs), conv_lru_w=_cat_cols(g_cl))
        elif key == "proj5":
            g_out, g_up = _own_shards(got, [self.shards["w_out"], self.shards["w_up"]])
            p.update(w_out=g_out.reshape(D, D), w_up=_cat_cols(g_up))
        elif key == "lru_fwd":
            g_down, = _own_shards(got, [self.shards["w_down"]])
            p.update(w_down=g_down.reshape(DFF, D))
        elif key == "lru_bwd":
            self.pair = [_pair_add(gs, o, self.cidx, f"grad_pair_add_{n}")
                         for gs, o, n in zip(self.shard_major, got, EARLY)]
        elif key == "ssd_bwd":
            self.mine = [_shard_sum(pf, o, self.kidx, f"grad_shard_sum_{n}")
                         for (pf, _), o, n in zip(self.pair, got, EARLY)]
        elif key == "conv_bwd":
            self.halves = {n: (mine, other) for n, mine, other in zip(EARLY, self.mine, got)}
        elif key == "du_norm":
            self.all_small, self.from_chips_in = got


def kernel(x, norm_mix_pre, w_in, conv_ssm_w, conv_ssm_b, dt_bias, a_log, d_skip, ssm_norm, conv_lru_w, conv_lru_b, lru_wa, lru_ba, lru_wx, lru_bx, lru_lambda, w_out, norm_mix_post, norm_mlp_pre, w_up, w_down, norm_mlp_post, loss_target, m_norm_mix_pre, m_w_in, m_conv_ssm_w, m_conv_ssm_b, m_dt_bias, m_a_log, m_d_skip, m_ssm_norm, m_conv_lru_w, m_conv_lru_b, m_lru_wa, m_lru_ba, m_lru_wx, m_lru_bx, m_lru_lambda, m_w_out, m_norm_mix_post, m_norm_mlp_pre, m_w_up, m_w_down, m_norm_mlp_post, v_norm_mix_pre, v_w_in, v_conv_ssm_w, v_conv_ssm_b, v_dt_bias, v_a_log, v_d_skip, v_ssm_norm, v_conv_lru_w, v_conv_lru_b, v_lru_wa, v_lru_ba, v_lru_wx, v_lru_bx, v_lru_lambda, v_w_out, v_norm_mix_post, v_norm_mlp_pre, v_w_up, v_w_down, v_norm_mlp_post):
    args = locals()
    w = {n: args[n][0] for n in WEIGHTS}
    m = {n: args["m_" + n][0] for n in WEIGHTS}
    v = {n: args["v_" + n][0] for n in WEIGHTS}
    cidx = lax.axis_index("c").astype(jnp.int32).reshape(1)
    kchip = 2 * lax.axis_index("x") + lax.axis_index("y")
    to_t = lambda a: jnp.transpose(a, (2, 0, 1)).reshape(W_IN_SHARD, D)
    from_t = lambda a: jnp.transpose(a.reshape(W_IN_SHARD, 1, D), (1, 2, 0))
    shards = {n: (to_t(w_in) if n == "w_in" else w[n]).astype(MXU) for n in BIG}
    dist = _Dist(shards, [shards["w_in"], w["conv_ssm_w"], w["conv_lru_w"]], cidx, kchip.astype(jnp.int32).reshape(1))
    p = {n: (w[n].reshape(1, -1) if w[n].ndim == 1 else w[n]) for n in SMALL}

    _, grad_x, g = _local_step(x[0], loss_target[0], p, dist)

    half_in = _shard_sum(dist.p_in, dist.from_chips_in, jnp.zeros((1,), jnp.int32), "grad_shard_sum_w_in", axis=2)
    packed_g1 = _pack([g["norm_mix_pre"]])
    all_g1, other_in = _run_plan(_merge_plans(_allgather8_plan(packed_g1), _pair_swap_plan([half_in])),
                                 "grad_pair_swap_w_in")
    halves = dist.halves

    reduced = {}
    *summed, loss = _unpack(_sum_devices(_own_block(dist.all_small, dist.packed_small), "small_sum"),
                            dist.small_shapes)
    loss = loss.reshape(())
    g1, = _unpack(_sum_devices(_own_block(all_g1, packed_g1), "small_sum_norm_mix_pre"), [g["norm_mix_pre"].shape])
    for n, s in zip(dist.small_names + ["norm_mix_pre"], summed + [g1]):
        if n in CONV:
            width = w[n].shape[1]
            reduced[n] = lax.dynamic_slice_in_dim(s, kchip * width, width, axis=1)
        else:
            reduced[n] = s.reshape(w[n].shape)

    delta, new_m, new_v = {}, {}, {}
    for n in EARLY:
        mine, other = halves[n]
        reduced[n], delta[n], new_m[n], new_v[n] = _adamw_halves(w[n], mine, other, m[n], v[n], cidx, f"adamw_{n}")
    outs_t = _adamw_halves(to_t(w_in), half_in, other_in, to_t(m_w_in), to_t(v_w_in), cidx, "adamw_w_in", axis=1)
    for d, o in zip((reduced, delta, new_m, new_v), outs_t):
        d["w_in"] = from_t(o)[0]
    for n in CONV:
        delta[n], new_m[n], new_v[n] = _adamw(w[n], reduced[n], m[n], v[n], f"adamw_{n}")
    shapes = [w[n].shape for n in SMALL]
    packed = [_pack([d[n] for n in SMALL]) for d in (w, reduced, m, v)]
    for d, out in zip((delta, new_m, new_v), _adamw(*packed, "adamw_small")):
        d.update(zip(SMALL, _unpack(out, shapes)))

    lead = lambda d: [d[n][None] for n in WEIGHTS]
    return (loss, grad_x[None], *lead(reduced), *lead(delta), *lead(new_m), *lead(new_v))
```

```python
import functools
import math

import jax
import jax.numpy as jnp
from jax import lax
from jax.experimental import pallas as pl
from jax.experimental.pallas import tpu as pltpu

F32 = jnp.float32
BF16 = jnp.bfloat16
MXU = BF16

D = 1024
DFF = 4096
NH = 16
HP = 64
NG = 2
NS = 128
CH = 128
XBC = D + 2 * NG * NS
GW = D // NG
LRU_C = 8.0
EPS = 1e-6
NCHIP = 4
W_IN_COLS = 6672
W_IN_SHARD = W_IN_COLS // NCHIP

ADAM_LR = 0.001
ADAM_B1 = 0.9
ADAM_B2 = 0.999
ADAM_EPS = 1e-08
ADAM_WD = 0.01
ADAM_STEP = 10

VMEM_LIMIT = 56 * 1024 * 1024
TK_GRAD = 2048
MID_AT = 0.7
ROWS_FUSED = 512
COL_G, COL_GA, COL_GB, COL_Z, COL_XL = range(5)
MESH = pl.DeviceIdType.MESH


def _cp(*sem):
    return pltpu.CompilerParams(dimension_semantics=sem, vmem_limit_bytes=VMEM_LIMIT)


def _dot(a, b, ca=1, cb=0, prec=None):
    return lax.dot_general(a, b, (((ca,), (cb,)), ((), ())), precision=prec, preferred_element_type=F32)


def _mdot(a, b, ca=1, cb=0):
    return _dot(a.astype(MXU), b.astype(MXU), ca, cb)


def _bf16_parts(v, n):
    parts = []
    for i in range(n):
        p = v.astype(BF16)
        parts.append(p)
        if i < n - 1:
            v = v - p.astype(F32)
    return parts


def _xdot(a, b, passes, split_b=False):
    if split_b:
        a16 = a.astype(BF16)
        terms = [_dot(a16, p) for p in _bf16_parts(b, passes)]
    else:
        b16 = b.astype(BF16)
        terms = [_dot(p, b16) for p in _bf16_parts(a, passes)]
    return functools.reduce(lambda u, v: u + v, terms)


def _sig(x):
    return 0.5 * jnp.tanh(0.5 * x) + 0.5


def _silu(x):
    return x * _sig(x)


def _dsilu(x):
    s = _sig(x)
    return s * (1.0 + x * (1.0 - s))


def _softplus(x):
    e = jnp.exp(-jnp.abs(x))
    return jnp.maximum(x, 0.0) + jnp.where(e < 1e-4, e * (1.0 - 0.5 * e), jnp.log(1.0 + e))


_GELU_C = math.sqrt(2.0 / math.pi)


def _gelu(x):
    t = jnp.tanh(_GELU_C * (x + 0.044715 * x * x * x))
    return 0.5 * x * (1.0 + t)


def _gelu_and_grad(x):
    x2 = x * x
    t = jnp.tanh(_GELU_C * (x + 0.044715 * x * x2))
    half = 0.5 * (1.0 + t)
    return x * half, half + 0.5 * x * (1.0 - t * t) * _GELU_C * (1.0 + 3.0 * 0.044715 * x2)


def _one_minus_sq(a, la):
    x = 2.0 * la
    series = -x * (1.0 + x * (0.5 + x * (1.0 / 6.0)))
    return jnp.where(x > -0.01, series, 1.0 - a * a)


def _rms(x):
    return lax.rsqrt(jnp.mean(x * x, axis=-1, keepdims=True) + EPS)


def _rms_bwd(x, r, g, dy):
    xn = x * r
    dxh = dy * g
    m = jnp.mean(dxh * xn, axis=-1, keepdims=True)
    return r * (dxh - xn * m), jnp.sum(dy * xn, axis=0, keepdims=True)


def _row_spec(t, c, col=0):
    return pl.BlockSpec((t, c), lambda i: (i, col))


def _rev_spec(t, c, n, col=0):
    return pl.BlockSpec((t, c), lambda i: (n - 1 - i, col))


def _full_spec(shape, once=False):
    nd = len(shape)
    if once:
        return pl.BlockSpec(shape, lambda *_: (0,) * nd, pipeline_mode=pl.Buffered(1))
    return pl.BlockSpec(shape, lambda *_: (0,) * nd)


def _sds(shape, dtype=F32):
    return jax.ShapeDtypeStruct(shape, dtype)


ANY = pl.BlockSpec(memory_space=pl.ANY)


class _Plan:
    def __init__(self, ins, outs, sems, start, finish, mid=None):
        self.ins, self.outs, self.sems = list(ins), list(outs), list(sems)
        self.start, self.finish, self.mid = start, finish, mid or (lambda i, o, s: None)


def _merge_plans(*plans):
    def each(fn_name, ins, outs, sems):
        i = o = s = 0
        for p in plans:
            getattr(p, fn_name)(ins[i:i + len(p.ins)], outs[o:o + len(p.outs)], sems[s:s + len(p.sems)])
            i, o, s = i + len(p.ins), o + len(p.outs), s + len(p.sems)

    return _Plan([a for p in plans for a in p.ins], [a for p in plans for a in p.outs],
                 [a for p in plans for a in p.sems], functools.partial(each, "start"),
                 functools.partial(each, "finish"), functools.partial(each, "mid"))


def _pcall(body, args, *, name, grid, in_specs, out_specs, out_shape, sem, scratch_shapes=(), plan=None):
    single = not isinstance(out_shape, (list, tuple))
    out_specs = [out_specs] if single else list(out_specs)
    out_shape = [out_shape] if single else list(out_shape)
    if plan is None:
        outs = pl.pallas_call(body, name=name, grid=grid, in_specs=list(in_specs), out_specs=out_specs,
                              out_shape=out_shape, scratch_shapes=list(scratch_shapes),
                              compiler_params=_cp(*sem))(*args)
        return outs[0] if single else outs
    n_in, n_out, n_sc, ni, no = len(in_specs), len(out_shape), len(scratch_shapes), len(plan.ins), len(plan.outs)

    def hosted(*refs):
        b0 = n_in + ni
        b1 = b0 + n_out + no
        sem_refs = refs[b1 + n_sc:]
        sems = [(sem_refs[2 * q], sem_refs[2 * q + 1]) for q in range(len(plan.sems))]
        step = functools.reduce(lambda lin, ig: lin * ig[1] + ig[0],
                                [(pl.program_id(d), g) for d, g in enumerate(grid)], 0)
        total = math.prod(grid)

        @pl.when(step == 0)
        def _():
            plan.start(refs[n_in:b0], refs[b0 + n_out:b1], sems)

        body(*refs[:n_in], *refs[b0:b0 + n_out], *refs[b1:b1 + n_sc])

        @pl.when(step == min(int(MID_AT * total), total - 1))
        def _():
            plan.mid(refs[n_in:b0], refs[b0 + n_out:b1], sems)

        @pl.when(step == total - 1)
        def _():
            plan.finish(refs[n_in:b0], refs[b0 + n_out:b1], sems)

    dma = [pltpu.SemaphoreType.DMA(shape) for shape in plan.sems for _ in range(2)]
    outs = pl.pallas_call(hosted, name=name, grid=grid, in_specs=list(in_specs) + [ANY] * ni,
                          out_specs=out_specs + [ANY] * no, out_shape=out_shape + plan.outs,
                          scratch_shapes=list(scratch_shapes) + dma,
                          compiler_params=_cp(*("arbitrary",) * len(grid)))(*args, *plan.ins)
    return (outs[0] if single else outs[:n_out]), outs[n_out:]


def _run_plan(plan, name):
    ni, no = len(plan.ins), len(plan.outs)

    def body(*refs):
        sem_refs = refs[ni + no:]
        sems = [(sem_refs[2 * q], sem_refs[2 * q + 1]) for q in range(len(plan.sems))]
        plan.start(refs[:ni], refs[ni:ni + no], sems)
        plan.mid(refs[:ni], refs[ni:ni + no], sems)
        plan.finish(refs[:ni], refs[ni:ni + no], sems)

    return pl.pallas_call(
        body, name=name, in_specs=[ANY] * ni, out_specs=[ANY] * no, out_shape=plan.outs,
        scratch_shapes=[pltpu.SemaphoreType.DMA(shape) for shape in plan.sems for _ in range(2)],
    )(*plan.ins)


def _matmul(a, b, *, name, ta=False, tb=False, tm=512, tn=1024, tk=1024, out_dtype=F32, a_fn=None, epi=None,
            epi_args=(), plan=None):
    m, k = (a.shape[1], a.shape[0]) if ta else a.shape
    n = b.shape[0] if tb else b.shape[1]
    tm, tn, tk = min(tm, m), min(tn, n), min(tk, k)
    nk = k // tk
    a_spec = pl.BlockSpec((tk, tm), lambda i, j, kk: (kk, i)) if ta else pl.BlockSpec((tm, tk), lambda i, j, kk: (i, kk))
    b_spec = pl.BlockSpec((tn, tk), lambda i, j, kk: (j, kk)) if tb else pl.BlockSpec((tk, tn), lambda i, j, kk: (kk, j))
    e_specs = [pl.BlockSpec((tm, tn), lambda i, j, kk: (i, j)) for _ in epi_args]
    ne = len(epi_args)

    def body(a_ref, b_ref, *rest):
        e_refs, o_ref = rest[:ne], rest[ne]
        av = a_ref[...]
        if a_fn is not None:
            av = a_fn(av)
        part = _mdot(av, b_ref[...], 0 if ta else 1, 1 if tb else 0)

        def finish(r):
            if epi is not None:
                r = epi(r, *[e[...] for e in e_refs])
            o_ref[...] = r.astype(o_ref.dtype)

        if nk == 1:
            finish(part)
            return
        acc_ref = rest[ne + 1]
        kk = pl.program_id(2)

        @pl.when(kk == 0)
        def _():
            acc_ref[...] = part

        @pl.when(jnp.logical_and(kk > 0, kk < nk - 1))
        def _():
            acc_ref[...] += part

        @pl.when(kk == nk - 1)
        def _():
            finish(acc_ref[...] + part)

    return _pcall(
        body, (a, b, *epi_args), name=name, grid=(m // tm, n // tn, nk),
        in_specs=[a_spec, b_spec] + e_specs,
        out_specs=pl.BlockSpec((tm, tn), lambda i, j, kk: (i, j)),
        out_shape=_sds((m, n), out_dtype),
        scratch_shapes=[pltpu.VMEM((tm, tn), F32)] if nk > 1 else [],
        sem=("parallel", "parallel", "arbitrary"), plan=plan)


def _relu2(p):
    p = jnp.maximum(p, jnp.zeros((), p.dtype))
    return p * p


def _norm_cast(x, g, name, plan=None):
    s = x.shape[0]
    t = min(512, s)

    def body(x_ref, g_ref, o_ref):
        xv = x_ref[...]
        o_ref[...] = (xv * _rms(xv) * g_ref[...]).astype(o_ref.dtype)

    return _pcall(body, (x, g), name=name, grid=(s // t,), in_specs=[_row_spec(t, D), _full_spec((1, D))],
                  out_specs=_row_spec(t, D), out_shape=_sds((s, D), MXU), sem=("parallel",), plan=plan)


def _conv_fwd(xbc_raw, proj5, dt_raw, cw_s, cb_s, cw_l, cb_l, dt_bias):
    s = xbc_raw.shape[0]
    t = min(256, s)

    def body(xs_ref, xl_ref, dtr_ref, cws_ref, cbs_ref, cwl_ref, cbl_ref, dtb_ref, xc_ref, dsl_ref, xr_ref, dt_ref,
             bs_ref, bl_ref):
        @pl.when(pl.program_id(0) == 0)
        def _():
            bs_ref[0:8, :] = jnp.zeros((8, XBC), F32)
            bl_ref[0:8, :] = jnp.zeros((8, D), F32)

        bs_ref[8:t + 8, :] = xs_ref[...]
        bl_ref[8:t + 8, :] = xl_ref[...].astype(F32)

        def conv(buf, w_ref, b_ref):
            acc = b_ref[...] + w_ref[3:4, :] * buf[8:t + 8, :]
            for k in (1, 2, 3):
                acc = acc + w_ref[3 - k:4 - k, :] * buf[8 - k:t + 8 - k, :]
            return acc

        pre = conv(bs_ref, cws_ref, cbs_ref)
        sg = _sig(pre)
        xc_ref[...] = pre * sg
        dsl_ref[...] = (sg * (1.0 + pre * (1.0 - sg))).astype(dsl_ref.dtype)
        xr_ref[...] = conv(bl_ref, cwl_ref, cbl_ref)
        dt_ref[...] = _softplus(dtr_ref[...] + dtb_ref[...])
        bs_ref[0:8, :] = bs_ref[t:t + 8, :]
        bl_ref[0:8, :] = bl_ref[t:t + 8, :]

    return pl.pallas_call(
        body, name="conv_fwd", grid=(s // t,),
        in_specs=[_row_spec(t, XBC), _row_spec(t, D, COL_XL), _row_spec(t, 128), _full_spec((4, XBC)),
                  _full_spec((1, XBC)), _full_spec((4, D)), _full_spec((1, D)), _full_spec((1, 128))],
        out_specs=[_row_spec(t, XBC), _row_spec(t, XBC), _row_spec(t, D), _row_spec(t, 128)],
        out_shape=[_sds((s, XBC)), _sds((s, XBC), BF16), _sds((s, D)), _sds((s, 128))],
        scratch_shapes=[pltpu.VMEM((t + 8, XBC), F32), pltpu.VMEM((t + 8, D), F32)],
        compiler_params=_cp("arbitrary"),
    )(xbc_raw, proj5, dt_raw, cw_s, cb_s, cw_l, cb_l, dt_bias)


def _ssd_chunk_setup(dt_ref, alog_ref, e_ref, at_ref, dtt_ref):
    lane = lax.broadcasted_iota(jnp.int32, (CH, 128), 1)
    row = lax.broadcasted_iota(jnp.int32, (CH, 128), 0)
    lane1 = lax.broadcasted_iota(jnp.int32, (1, 128), 1)
    a = jnp.where(lane1 < NH, -jnp.exp(alog_ref[...]), 0.0)
    dtv = dt_ref[...]
    adt = dtv * a
    tril = row >= lane
    acum = _xdot(tril.astype(F32), adt, 3, split_b=True)
    alast = jnp.sum(adt, axis=0, keepdims=True)
    at_ref[...] = acum.T
    dtt_ref[...] = dtv.T
    e = e_ref[...]
    ea_x = _xdot(jnp.exp(acum), e, 2)
    ws = jnp.exp(alast - acum) * dtv
    ws_x = _xdot(ws, e, 2)
    eal = jnp.exp(alast)
    eal_x = jnp.max(_xdot(jnp.broadcast_to(eal, (8, 128)), e, 3), axis=0, keepdims=True)
    return dict(lane=lane, row=row, tril=tril, a=a, dtv=dtv, acum=acum, alast=alast, ea_x=ea_x, ws=ws, ws_x=ws_x,
                eal=eal, eal_x=eal_x)


def _head_decay(cs, at_ref, dtt_ref, h):
    col = jnp.sum(jnp.where(cs["lane"] == h, cs["acum"], 0.0), axis=1, keepdims=True)
    ld = jnp.where(cs["tril"], jnp.exp(jnp.minimum(col - at_ref[h:h + 1, :], 0.0)), 0.0)
    return ld, dtt_ref[h:h + 1, :]


def _ssd_fwd(xbc_c, dt, proj5, a_log, dskip_x, ssm_norm, expand):
    s = xbc_c.shape[0]
    nc = s // CH

    def body(xc_ref, dt_ref, z_ref, alog_ref, dsk_ref, ng_ref, e_ref, y_ref, ya_ref, st_ref, h_ref, at_ref, dtt_ref,
             yd_ref):
        @pl.when(pl.program_id(0) == 0)
        def _():
            h_ref[...] = jnp.zeros_like(h_ref)

        cs = _ssd_chunk_setup(dt_ref, alog_ref, e_ref, at_ref, dtt_ref)
        lane = cs["lane"]
        for g in range(NG):
            gs = slice(GW * g, GW * (g + 1))
            bg = xc_ref[:, D + NS * g:D + NS * (g + 1)]
            cg = xc_ref[:, D + NG * NS + NS * g:D + NG * NS + NS * (g + 1)]
            cb = _mdot(cg, bg, 1, 1)
            for j in range(4 * g, 4 * g + 4):
                ps = slice(128 * j, 128 * (j + 1))
                xp = xc_ref[:, ps]
                acc = jnp.zeros((CH, 128), F32)
                for hf in range(2):
                    ld, rowdt = _head_decay(cs, at_ref, dtt_ref, 2 * j + hf)
                    hm = (lane >= HP) if hf else (lane < HP)
                    acc = acc + _mdot(cb * ld * rowdt, jnp.where(hm, xp, 0.0))
                yd_ref[:, ps] = acc
            hg = h_ref[:, gs]
            yd_ref[:, gs] += _mdot(cg, hg) * cs["ea_x"][:, gs]
            st = _mdot(bg, xc_ref[:, gs] * cs["ws_x"][:, gs], 0, 0)
            st_ref[0, :, gs] = hg
            h_ref[:, gs] = cs["eal_x"][:, gs] * hg + st
        y = yd_ref[...] + dsk_ref[...] * xc_ref[:, 0:D]
        y_ref[...] = y
        yg = y * _silu(z_ref[...].astype(F32))
        for g in range(NG):
            gs = slice(GW * g, GW * (g + 1))
            seg = yg[:, gs]
            ya_ref[:, gs] = seg * _rms(seg) * ng_ref[:, gs]

    return pl.pallas_call(
        body, name="ssd_fwd", grid=(nc,),
        in_specs=[_row_spec(CH, XBC), _row_spec(CH, 128), _row_spec(CH, D, COL_Z), _full_spec((1, 128)),
                  _full_spec((1, D)), _full_spec((1, D)), _full_spec((128, D))],
        out_specs=[_row_spec(CH, D), _row_spec(CH, D), pl.BlockSpec((1, NS, D), lambda i: (i, 0, 0))],
        out_shape=[_sds((s, D)), _sds((s, D)), _sds((nc, NS, D))],
        scratch_shapes=[pltpu.VMEM((NS, D), F32), pltpu.VMEM((128, 128), F32), pltpu.VMEM((128, 128), F32),
                        pltpu.VMEM((CH, D), F32)],
        compiler_params=_cp("arbitrary"),
    )(xbc_c, dt, proj5, a_log, dskip_x, ssm_norm, expand)


def _lru_gates(xr, wab_ref, ba_ref, bx_ref, lam_ref):
    pre = _mdot(xr, wab_ref[...])
    gr = _sig(pre[:, 0:D] + ba_ref[...])
    gi = _sig(pre[:, D:2 * D] + bx_ref[...])
    sp = _softplus(-lam_ref[...])
    la = -LRU_C * gr * sp
    a = jnp.exp(la)
    oms = _one_minus_sq(a, la)
    inv_mult = lax.rsqrt(oms)
    return gr, gi, sp, a, oms * inv_mult, inv_mult


def _blocked_scan(a, u, carry_ref, a_ref, u_ref, c_ref, out_ref, reverse):
    t = a.shape[0]
    ns = t // 8

    def combine(av, uv, idx, n, sh):
        m = (idx < n - sh) if reverse else (idx >= sh)
        by = n - sh if reverse else sh
        return jnp.where(m, av * pltpu.roll(av, by, 0), av), jnp.where(m, uv + av * pltpu.roll(uv, by, 0), uv)

    row = lax.broadcasted_iota(jnp.int32, (t, D), 0)
    rin = jnp.bitwise_and(row, 7)
    for sh in (1, 2, 4):
        m = (rin < 8 - sh) if reverse else (rin >= sh)
        by = t - sh if reverse else sh
        a, u = jnp.where(m, a * pltpu.roll(a, by, 0), a), jnp.where(m, u + a * pltpu.roll(u, by, 0), u)
    a_ref[...] = a
    u_ref[...] = u
    edge = 0 if reverse else 7
    for j in range(ns):
        c_ref[j:j + 1, :] = a_ref[8 * j + edge:8 * j + edge + 1, :]
    at = c_ref[...]
    for j in range(ns):
        c_ref[j:j + 1, :] = u_ref[8 * j + edge:8 * j + edge + 1, :]
    ut = c_ref[...]
    srow = lax.broadcasted_iota(jnp.int32, (ns, D), 0)
    sh = 1
    while sh < ns:
        at, ut = combine(at, ut, srow, ns, sh)
        sh *= 2
    cv = carry_ref[0:1, :]
    ends = ut + at * cv
    last = 0 if reverse else ns - 1
    first = ns - 1 if reverse else 0
    c_ref[...] = jnp.where(srow == first, cv, pltpu.roll(ends, first if reverse else 1, 0))
    carry_ref[0:1, :] = jnp.sum(jnp.where(srow == last, ends, 0.0), axis=0, keepdims=True)
    for j in range(ns):
        sl = slice(8 * j, 8 * j + 8)
        out_ref[sl, :] = u_ref[sl, :] + a_ref[sl, :] * c_ref[j:j + 1, :]


def _lru_fwd(xr, proj5, ya, wab, ba, bx, lam, plan=None):
    s = xr.shape[0]
    t = min(256, s)

    def body(xr_ref, g_ref, ga_ref, gb_ref, ya_ref, wab_ref, ba_ref, bx_ref, lam_ref, h_ref, mg_ref, gr_ref,
             gi_ref, ao_ref, mo_ref, hc_ref, a_ref, u_ref, c_ref):
        @pl.when(pl.program_id(0) == 0)
        def _():
            hc_ref[...] = jnp.zeros_like(hc_ref)

        xrv = xr_ref[...]
        gr, gi, _, a, mult, _ = _lru_gates(xrv, wab_ref, ba_ref, bx_ref, lam_ref)
        gr_ref[...], gi_ref[...], ao_ref[...], mo_ref[...] = gr, gi, a, mult
        _blocked_scan(a, mult * gi * xrv, hc_ref, a_ref, u_ref, c_ref, h_ref, reverse=False)
        yb = h_ref[...] * _gelu(g_ref[...].astype(F32))
        mg_ref[...] = (_sig(ga_ref[...].astype(F32)) * ya_ref[...]
                       + _sig(gb_ref[...].astype(F32)) * yb).astype(mg_ref.dtype)

    return _pcall(
        body, (xr, proj5, proj5, proj5, ya, wab, ba, bx, lam), name="lru_fwd", grid=(s // t,),
        in_specs=[_row_spec(t, D), _row_spec(t, D, COL_G), _row_spec(t, D, COL_GA), _row_spec(t, D, COL_GB),
                  _row_spec(t, D), _full_spec((D, 2 * D), once=True), _full_spec((1, D)), _full_spec((1, D)),
                  _full_spec((1, D))],
        out_specs=[_row_spec(t, D)] * 6,
        out_shape=[_sds((s, D)), _sds((s, D), MXU)] + [_sds((s, D))] * 4,
        scratch_shapes=[pltpu.VMEM((8, D), F32), pltpu.VMEM((t, D), F32), pltpu.VMEM((t, D), F32),
                        pltpu.VMEM((t // 8, D), F32)],
        sem=("arbitrary",), plan=plan)


def _out_up_proj(merged, w_out, x, g2, g3, w_up):
    s = x.shape[0]
    t = min(ROWS_FUSED, s)

    def body(mg_ref, w_ref, x_ref, g2_ref, g3_ref, wu_ref, mix_ref, h1_ref, v_ref, pre_ref):
        mix = _mdot(mg_ref[...], w_ref[...])
        mix_ref[...] = mix
        h1 = x_ref[...] + mix * _rms(mix) * g2_ref[...]
        h1_ref[...] = h1
        v = (h1 * _rms(h1) * g3_ref[...]).astype(v_ref.dtype)
        v_ref[...] = v
        pre_ref[...] = _mdot(v, wu_ref[...]).astype(pre_ref.dtype)

    return pl.pallas_call(
        body, name="out_up_proj", grid=(s // t,),
        in_specs=[_row_spec(t, D), _full_spec((D, D), once=True), _row_spec(t, D), _full_spec((1, D)),
                  _full_spec((1, D)), _full_spec((D, DFF), once=True)],
        out_specs=[_row_spec(t, D), _row_spec(t, D), _row_spec(t, D), _row_spec(t, DFF)],
        out_shape=[_sds((s, D)), _sds((s, D)), _sds((s, D), MXU), _sds((s, DFF), MXU)],
        compiler_params=_cp("parallel"),
    )(merged, w_out, x, g2, g3, w_up)


def _down_loss(pre, w_down, h1, target, g4):
    s = pre.shape[0]
    t = min(ROWS_FUSED, s)

    def body(pre_ref, w_ref, h1_ref, tg_ref, g4_ref, dout_ref, dff_ref, loss_ref, dg4_ref):
        @pl.when(pl.program_id(0) == 0)
        def _():
            loss_ref[...] = jnp.zeros_like(loss_ref)
            dg4_ref[...] = jnp.zeros_like(dg4_ref)

        ff = _mdot(_relu2(pre_ref[...]), w_ref[...])
        r4 = _rms(ff)
        g4v = g4_ref[...]
        diff = h1_ref[...] + ff * r4 * g4v - tg_ref[...]
        sq = jnp.sum(jnp.sum(diff * diff, axis=1, keepdims=True), axis=0, keepdims=True)
        loss_ref[...] += (0.5 / D) * sq
        dout = diff * (1.0 / D)
        dout_ref[...] = dout
        dff, dg = _rms_bwd(ff, r4, g4v, dout)
        dff_ref[...] = dff.astype(dff_ref.dtype)
        dg4_ref[...] += dg

    return pl.pallas_call(
        body, name="down_loss", grid=(s // t,),
        in_specs=[_row_spec(t, DFF), _full_spec((DFF, D), once=True), _row_spec(t, D), _row_spec(t, D),
                  _full_spec((1, D))],
        out_specs=[_row_spec(t, D), _row_spec(t, D), _full_spec((1, 128)), _full_spec((1, D))],
        out_shape=[_sds((s, D)), _sds((s, D), MXU), _sds((1, 128)), _sds((1, D))],
        compiler_params=_cp("arbitrary"),
    )(pre, w_down, h1, target, g4)


def _dv_norms(dpre, w_up, h1, mix, dout, g3, g2, w_out):
    s = h1.shape[0]
    t = min(ROWS_FUSED, s)

    def body(dp_ref, w_ref, h1_ref, mix_ref, dout_ref, g3_ref, g2_ref, wo_ref, dh1_ref, dmix_ref, dmg_ref, dg3_ref,
             dg2_ref):
        @pl.when(pl.program_id(0) == 0)
        def _():
            dg3_ref[...] = jnp.zeros_like(dg3_ref)
            dg2_ref[...] = jnp.zeros_like(dg2_ref)

        dv = _mdot(dp_ref[...], w_ref[...], 1, 1)
        h1 = h1_ref[...]
        dh1n, dg3 = _rms_bwd(h1, _rms(h1), g3_ref[...], dv)
        dh1 = dout_ref[...] + dh1n
        dh1_ref[...] = dh1
        mix = mix_ref[...]
        dmix, dg2 = _rms_bwd(mix, _rms(mix), g2_ref[...], dh1)
        dmix = dmix.astype(dmix_ref.dtype)
        dmix_ref[...] = dmix
        dmg_ref[...] = _mdot(dmix, wo_ref[...], 1, 1)
        dg3_ref[...] += dg3
        dg2_ref[...] += dg2

    return pl.pallas_call(
        body, name="dv_norms", grid=(s // t,),
        in_specs=[_row_spec(t, DFF), _full_spec((D, DFF), once=True), _row_spec(t, D), _row_spec(t, D),
                  _row_spec(t, D), _full_spec((1, D)), _full_spec((1, D)), _full_spec((D, D), once=True)],
        out_specs=[_row_spec(t, D), _row_spec(t, D), _row_spec(t, D), _full_spec((1, D)), _full_spec((1, D))],
        out_shape=[_sds((s, D)), _sds((s, D), MXU), _sds((s, D)), _sds((1, D)), _sds((1, D))],
        compiler_params=_cp("arbitrary"),
    )(dpre, w_up, h1, mix, dout, g3, g2, w_out)


def _lru_bwd(dmerged, ya, xr, h, proj5, gates, wab_t, lam, plan=None):
    s = xr.shape[0]
    t = min(128, s)
    n = s // t
    rs = functools.partial(_rev_spec, t, D, n)

    def body(dm_ref, ya_ref, xr_ref, h_ref, hp_ref, g_ref, ga_ref, gb_ref, gr_ref, gi_ref, a_ref, m_ref, wab_ref,
             lam_ref, dya_ref, d3_ref, dxr_ref, dp2_ref, dlam_ref, dba_ref, dbx_ref, gc_ref, af_ref, an_ref, us_ref,
             c_ref, gs_ref):
        i = pl.program_id(0)

        @pl.when(i == 0)
        def _():
            gc_ref[...] = jnp.zeros_like(gc_ref)
            af_ref[...] = jnp.zeros_like(af_ref)
            dlam_ref[...] = jnp.zeros_like(dlam_ref)
            dba_ref[...] = jnp.zeros_like(dba_ref)
            dbx_ref[...] = jnp.zeros_like(dbx_ref)

        xrv = xr_ref[...]
        gr, gi, a, mult = gr_ref[...], gi_ref[...], a_ref[...], m_ref[...]
        sp = _softplus(-lam_ref[...])
        inv_mult = 1.0 / mult
        hv = h_ref[...]
        dm = dm_ref[...]
        sa = _sig(ga_ref[...].astype(F32))
        sb = _sig(gb_ref[...].astype(F32))
        gel, dgel = _gelu_and_grad(g_ref[...].astype(F32))
        dya = dm * sa
        dya_ref[...] = dya
        dyb = dm * sb
        dybh = dyb * hv
        d3_ref[:, 0:D] = (dybh * dgel).astype(d3_ref.dtype)
        d3_ref[:, D:2 * D] = (dya * ya_ref[...] * (1.0 - sa)).astype(d3_ref.dtype)
        d3_ref[:, 2 * D:3 * D] = (dybh * gel * (1.0 - sb)).astype(d3_ref.dtype)
        row = lax.broadcasted_iota(jnp.int32, (t, D), 0)
        an = jnp.where(row == t - 1, af_ref[0:1, :], pltpu.roll(a, t - 1, 0))
        _blocked_scan(an, dyb * gel, gc_ref, an_ref, us_ref, c_ref, gs_ref, reverse=True)
        gfull = gs_ref[...]
        af_ref[0:1, :] = jnp.sum(jnp.where(row == 0, a, 0.0), axis=0, keepdims=True)
        hlast = jnp.where(i == n - 1, 0.0, hp_ref[7:8, :])
        hprev = jnp.where(row == 0, hlast, pltpu.roll(hv, 1, 0))
        gx = gfull * xrv
        dgi = gx * mult
        dla = a * (gfull * hprev - gx * gi * a * inv_mult)
        dgr = dla * (-LRU_C * sp)
        dsp = jnp.sum(dla * (-LRU_C * gr), axis=0, keepdims=True)
        dlam_ref[...] += dsp * (-_sig(-lam_ref[...]))
        dpr = dgr * gr * (1.0 - gr)
        dpi = dgi * gi * (1.0 - gi)
        dp2_ref[:, 0:D] = dpr.astype(dp2_ref.dtype)
        dp2_ref[:, D:2 * D] = dpi.astype(dp2_ref.dtype)
        dba_ref[...] += jnp.sum(dpr, axis=0, keepdims=True)
        dbx_ref[...] += jnp.sum(dpi, axis=0, keepdims=True)
        dxr_ref[...] = gfull * mult * gi + _mdot(dp2_ref[...], wab_ref[...])

    hp_spec = pl.BlockSpec((8, D), lambda i: (jnp.maximum((n - 1 - i) * (t // 8) - 1, 0), 0))
    wide = lambda c: pl.BlockSpec((t, c), lambda i: (n - 1 - i, 0))
    return _pcall(
        body, (dmerged, ya, xr, h, h, proj5, proj5, proj5, *gates, wab_t, lam), name="lru_bwd", grid=(n,),
        in_specs=[rs(), rs(), rs(), rs(), hp_spec, rs(COL_G), rs(COL_GA), rs(COL_GB), rs(), rs(), rs(), rs(),
                  _full_spec((2 * D, D), once=True), _full_spec((1, D))],
        out_specs=[rs(), wide(3 * D), rs(), wide(2 * D), _full_spec((1, D)), _full_spec((1, D)), _full_spec((1, D))],
        out_shape=[_sds((s, D)), _sds((s, 3 * D), MXU), _sds((s, D)), _sds((s, 2 * D), MXU), _sds((1, D)),
                   _sds((1, D)), _sds((1, D))],
        scratch_shapes=[pltpu.VMEM((8, D), F32), pltpu.VMEM((8, D), F32), pltpu.VMEM((t, D), F32),
                        pltpu.VMEM((t, D), F32), pltpu.VMEM((t // 8, D), F32), pltpu.VMEM((t, D), F32)],
        sem=("arbitrary",), plan=plan)


def _ssd_bwd(dya, y, proj5, xbc_c, dt, states, a_log, dskip_x, ssm_norm, expand, reduce_, plan=None):
    s = xbc_c.shape[0]
    nc = s // CH
    rv = functools.partial(_rev_spec, CH)

    def body(dya_ref, y_ref, z_ref, xc_ref, dt_ref, st_ref, alog_ref, dsk_ref, ng_ref, e_ref, et_ref, dz_ref,
             dxc_ref, ddt_ref, dng_ref, ddsk_ref, dalog_ref, dh_ref, at_ref, dtt_ref, dat_ref, ddtt_ref, dy_ref,
             yoffdy_ref, xbds_ref):
        @pl.when(pl.program_id(0) == 0)
        def _():
            dh_ref[...] = jnp.zeros_like(dh_ref)
            dng_ref[...] = jnp.zeros_like(dng_ref)
            ddsk_ref[...] = jnp.zeros_like(ddsk_ref)
            dalog_ref[...] = jnp.zeros_like(dalog_ref)

        cs = _ssd_chunk_setup(dt_ref, alog_ref, e_ref, at_ref, dtt_ref)
        lane, row = cs["lane"], cs["row"]
        et = et_ref[...]
        for g in range(NG):
            gs = slice(GW * g, GW * (g + 1))
            yv = y_ref[:, gs]
            zv = z_ref[:, gs].astype(F32)
            sz = _silu(zv)
            yg = yv * sz
            dyav = dya_ref[:, gs]
            dyg, dng = _rms_bwd(yg, _rms(yg), ng_ref[:, gs], dyav)
            dng_ref[:, gs] += dng
            dy_ref[:, gs] = dyg * sz
            dz_ref[:, gs] = (dyg * yv * _dsilu(zv)).astype(dz_ref.dtype)
        dyv = dy_ref[...]
        xs = xc_ref[:, 0:D]
        ddsk_ref[...] += jnp.sum(dyv * xs, axis=0, keepdims=True)
        dxc_ref[:, 0:D] = dyv * dsk_ref[...]
        dat_ref[...] = jnp.zeros_like(dat_ref)
        ddtt_ref[...] = jnp.zeros_like(ddtt_ref)
        hh = jnp.sum(dh_ref[...] * st_ref[0], axis=0, keepdims=True)
        deal = jnp.max(_xdot(jnp.broadcast_to(hh, (8, D)), et, 3), axis=0, keepdims=True)
        d_acum = jnp.zeros((CH, 128), F32)
        for g in range(NG):
            gs = slice(GW * g, GW * (g + 1))
            bs_ = slice(D + NS * g, D + NS * (g + 1))
            cs_ = slice(D + NG * NS + NS * g, D + NG * NS + NS * (g + 1))
            bg = xc_ref[:, bs_]
            cg = xc_ref[:, cs_]
            cb = _mdot(cg, bg, 1, 1)
            hg = st_ref[0, :, gs]
            dhg = dh_ref[:, gs]
            dyg_ = dy_ref[:, gs]
            xsg = xc_ref[:, gs]
            ea = cs["ea_x"][:, gs]
            wsx = cs["ws_x"][:, gs]
            dp = dyg_ * ea
            yoffdy_ref[:, gs] = dp * _mdot(cg, hg)
            dc = _mdot(dp, hg, 1, 1)
            dhprev = _mdot(cg, dp, 0, 0)
            bds = _mdot(bg, dhg)
            dxc_ref[:, gs] += wsx * bds
            xbds_ref[:, gs] = xsg * bds
            db = _mdot(xsg * wsx, dhg, 1, 1)
            dh_ref[:, gs] = dhprev + cs["eal_x"][:, gs] * dhg
            dcbs = jnp.zeros((CH, CH), F32)
            for j in range(4 * g, 4 * g + 4):
                ps = slice(128 * j, 128 * (j + 1))
                xp = xc_ref[:, ps]
                dyp = dy_ref[:, ps]
                dxacc = jnp.zeros((CH, 128), F32)
                for hf in range(2):
                    hd = 2 * j + hf
                    ld, rowdt = _head_decay(cs, at_ref, dtt_ref, hd)
                    hm = (lane >= HP) if hf else (lane < HP)
                    dym = jnp.where(hm, dyp, 0.0)
                    w = cb * ld * rowdt
                    dw = _mdot(dym, jnp.where(hm, xp, 0.0), 1, 1)
                    dxacc = dxacc + _mdot(w, dym, 0, 0)
                    nm = dw * w
                    ddtt_ref[hd:hd + 1, :] += jnp.sum(dw * cb * ld, axis=0, keepdims=True)
                    d_acum = d_acum + jnp.where(lane == hd, jnp.sum(nm, axis=1, keepdims=True), 0.0)
                    dat_ref[hd:hd + 1, :] -= jnp.sum(nm, axis=0, keepdims=True)
                    dcbs = dcbs + dw * ld * rowdt
                dxc_ref[:, ps] += dxacc
            dxc_ref[:, bs_] = db + _mdot(dcbs, cg, 0, 0)
            dxc_ref[:, cs_] = dc + _mdot(dcbs, bg)
        dws = _xdot(xbds_ref[...], et, 2)
        ws = cs["ws"]
        d_acum = d_acum - dws * ws + _xdot(yoffdy_ref[...], et, 2) + dat_ref[...].T
        d_alast = jnp.sum(dws * ws, axis=0, keepdims=True) + deal * cs["eal"]
        d_acum = d_acum + jnp.where(row == CH - 1, d_alast, 0.0)
        triu = row <= lane
        d_adt = _xdot(triu.astype(F32), d_acum, 3, split_b=True)
        ddt_ref[...] = dws * jnp.exp(cs["alast"] - cs["acum"]) + ddtt_ref[...].T + d_adt * cs["a"]
        dalog_ref[...] += jnp.sum(d_adt * cs["dtv"], axis=0, keepdims=True) * cs["a"]

    return _pcall(
        body, (dya, y, proj5, xbc_c, dt, states, a_log, dskip_x, ssm_norm, expand, reduce_), name="ssd_bwd",
        grid=(nc,),
        in_specs=[rv(D, nc), rv(D, nc), rv(D, nc, COL_Z), rv(XBC, nc), rv(128, nc),
                  pl.BlockSpec((1, NS, D), lambda i: (nc - 1 - i, 0, 0)), _full_spec((1, 128)), _full_spec((1, D)),
                  _full_spec((1, D)), _full_spec((128, D)), _full_spec((D, 128))],
        out_specs=[rv(D, nc), rv(XBC, nc), rv(128, nc), _full_spec((1, D)), _full_spec((1, D)),
                   _full_spec((1, 128))],
        out_shape=[_sds((s, D), MXU), _sds((s, XBC)), _sds((s, 128)), _sds((1, D)), _sds((1, D)), _sds((1, 128))],
        scratch_shapes=[pltpu.VMEM((NS, D), F32), pltpu.VMEM((128, 128), F32), pltpu.VMEM((128, 128), F32),
                        pltpu.VMEM((128, 128), F32), pltpu.VMEM((128, 128), F32), pltpu.VMEM((CH, D), F32),
                        pltpu.VMEM((CH, D), F32), pltpu.VMEM((CH, D), F32)],
        sem=("arbitrary",), plan=plan)


def _conv_bwd(dxbc_c, dsilu, dxr, ddt, xbc_raw, proj5, dt_raw, cw_s, cw_l, dt_bias, plan=None):
    s = xbc_raw.shape[0]
    t = min(256, s)
    n = s // t

    def body(dxc_ref, dsl_ref, dxr_ref, ddt_ref, xs_ref, xl_ref, dtr_ref, cws_ref, cwl_ref, dtb_ref, dxs_ref,
             dxl_ref, ddtr_ref, dcws_ref, dcbs_ref, dcwl_ref, dcbl_ref, ddtb_ref, ds_ref, dl_ref):
        @pl.when(pl.program_id(0) == 0)
        def _():
            ds_ref[t:t + 8, :] = jnp.zeros((8, XBC), F32)
            dl_ref[t:t + 8, :] = jnp.zeros((8, D), F32)
            for r in (dcws_ref, dcbs_ref, dcwl_ref, dcbl_ref, ddtb_ref):
                r[...] = jnp.zeros_like(r)

        ds_ref[0:t, :] = dxc_ref[...] * dsl_ref[...].astype(F32)
        dl_ref[0:t, :] = dxr_ref[...]

        def back(dbuf, x_ref, w_ref, dx_ref, dw_ref, db_ref):
            xv = x_ref[...].astype(F32)
            dpre = dbuf[0:t, :]
            dx = w_ref[3:4, :] * dpre
            dw_ref[3:4, :] += jnp.sum(dpre * xv, axis=0, keepdims=True)
            db_ref[...] += jnp.sum(dpre, axis=0, keepdims=True)
            for k in (1, 2, 3):
                ahead = dbuf[k:t + k, :]
                dx = dx + w_ref[3 - k:4 - k, :] * ahead
                dw_ref[3 - k:4 - k, :] += jnp.sum(ahead * xv, axis=0, keepdims=True)
            dx_ref[...] = dx.astype(dx_ref.dtype)
            dbuf[t:t + 8, :] = dbuf[0:8, :]

        back(ds_ref, xs_ref, cws_ref, dxs_ref, dcws_ref, dcbs_ref)
        back(dl_ref, xl_ref, cwl_ref, dxl_ref, dcwl_ref, dcbl_ref)
        ddtr = ddt_ref[...] * _sig(dtr_ref[...] + dtb_ref[...])
        ddtr_ref[...] = ddtr.astype(ddtr_ref.dtype)
        ddtb_ref[...] += jnp.sum(ddtr, axis=0, keepdims=True)

    rv = functools.partial(_rev_spec, t)
    return _pcall(
        body, (dxbc_c, dsilu, dxr, ddt, xbc_raw, proj5, dt_raw, cw_s, cw_l, dt_bias), name="conv_bwd", grid=(n,),
        in_specs=[rv(XBC, n), rv(XBC, n), rv(D, n), rv(128, n), rv(XBC, n), rv(D, n, COL_XL), rv(128, n),
                  _full_spec((4, XBC)), _full_spec((4, D)), _full_spec((1, 128))],
        out_specs=[rv(XBC, n), rv(D, n), rv(128, n), _full_spec((4, XBC)), _full_spec((1, XBC)), _full_spec((4, D)),
                   _full_spec((1, D)), _full_spec((1, 128))],
        out_shape=[_sds((s, XBC), MXU), _sds((s, D), MXU), _sds((s, 128), MXU), _sds((4, XBC)), _sds((1, XBC)),
                   _sds((4, D)), _sds((1, D)), _sds((1, 128))],
        scratch_shapes=[pltpu.VMEM((t + 8, XBC), F32), pltpu.VMEM((t + 8, D), F32)],
        sem=("arbitrary",), plan=plan)


def _du_norm(d3, dz, dxl, dxbc, ddtr, w5, wxbc, wdt, x, dh1, g1, plan=None):
    s = x.shape[0]
    t = min(ROWS_FUSED, s)

    def body(d3_ref, dz_ref, dxl_ref, dxbc_ref, ddtr_ref, w5_ref, wx_ref, wd_ref, x_ref, dh1_ref, g1_ref, dx_ref,
             dg1_ref):
        @pl.when(pl.program_id(0) == 0)
        def _():
            dg1_ref[...] = jnp.zeros_like(dg1_ref)

        du = (_mdot(dxbc_ref[...], wx_ref[...]) + _mdot(ddtr_ref[...], wd_ref[...])
              + _mdot(d3_ref[...], w5_ref[0:3 * D, :])
              + _mdot(dz_ref[...], w5_ref[COL_Z * D:(COL_Z + 1) * D, :])
              + _mdot(dxl_ref[...], w5_ref[COL_XL * D:(COL_XL + 1) * D, :]))
        xv = x_ref[...]
        dxn, dg1 = _rms_bwd(xv, _rms(xv), g1_ref[...], du)
        dx_ref[...] = dh1_ref[...] + dxn
        dg1_ref[...] += dg1

    return _pcall(
        body, (d3, dz, dxl, dxbc, ddtr, w5, wxbc, wdt, x, dh1, g1), name="du_norm", grid=(s // t,),
        in_specs=[_row_spec(t, 3 * D), _row_spec(t, D), _row_spec(t, D), _row_spec(t, XBC), _row_spec(t, 128),
                  _full_spec((5 * D, D), once=True), _full_spec((XBC, D), once=True),
                  _full_spec((128, D), once=True), _row_spec(t, D), _row_spec(t, D), _full_spec((1, D))],
        out_specs=[_row_spec(t, D), _full_spec((1, D))],
        out_shape=[_sds((s, D)), _sds((1, D))],
        sem=("arbitrary",), plan=plan)


def _adamw(w, g, m, v, name):
    r, c = w.shape
    t = r
    if r * c > 256 * 1024:
        t = next(cand for cand in (512, 256, 128, 64, 32, 16, 8) if r % cand == 0 and cand * c <= 512 * 1024)
    bc1 = 1.0 - ADAM_B1 ** ADAM_STEP
    bc2 = 1.0 - ADAM_B2 ** ADAM_STEP

    def body(w_ref, g_ref, m_ref, v_ref, d_ref, nm_ref, nv_ref):
        gv = g_ref[...]
        nm = ADAM_B1 * m_ref[...] + (1.0 - ADAM_B1) * gv
        nv = ADAM_B2 * v_ref[...] + (1.0 - ADAM_B2) * (gv * gv)
        nm_ref[...] = nm
        nv_ref[...] = nv
        d_ref[...] = -ADAM_LR * ((nm / bc1) / (jnp.sqrt(nv / bc2) + ADAM_EPS) + ADAM_WD * w_ref[...])

    spec = pl.BlockSpec((t, c), lambda i: (i, 0))
    return pl.pallas_call(
        body, name=name, grid=(r // t,), in_specs=[spec] * 4, out_specs=[spec] * 3,
        out_shape=[_sds((r, c))] * 3, compiler_params=_cp("parallel"),
    )(w, g, m, v)


def _half_blocks(shape, axis):
    r, c = shape
    if axis == 0:
        t = 256 if (r // 2) % 256 == 0 else 128
        nb = (r // 2) // t
        return (t, c), nb, (lambda i: (i, 0)), (lambda i: (i % nb, 0))
    nb = (c // 2) // 128
    return (r, 128), nb, (lambda i: (0, i)), (lambda i: (0, i % nb))


def _adamw_halves(w, g_mine, g_other, m, v, cidx, name, axis=0):
    r, c = w.shape
    blk, nb, whole, part = _half_blocks(w.shape, axis)
    bc1 = 1.0 - ADAM_B1 ** ADAM_STEP
    bc2 = 1.0 - ADAM_B2 ** ADAM_STEP

    def body(c_ref, w_ref, gm_ref, go_ref, m_ref, v_ref, g_ref, d_ref, nm_ref, nv_ref):
        mine = (pl.program_id(0) // nb) == c_ref[0]
        gv = jnp.where(mine, gm_ref[...], go_ref[...])
        g_ref[...] = gv
        nm = ADAM_B1 * m_ref[...] + (1.0 - ADAM_B1) * gv
        nv = ADAM_B2 * v_ref[...] + (1.0 - ADAM_B2) * (gv * gv)
        nm_ref[...] = nm
        nv_ref[...] = nv
        d_ref[...] = -ADAM_LR * ((nm / bc1) / (jnp.sqrt(nv / bc2) + ADAM_EPS) + ADAM_WD * w_ref[...])

    spec = pl.BlockSpec(blk, lambda i, c_ref: whole(i))
    half = pl.BlockSpec(blk, lambda i, c_ref: part(i))
    return pl.pallas_call(
        body, name=name,
        grid_spec=pltpu.PrefetchScalarGridSpec(num_scalar_prefetch=1, grid=(2 * nb,),
                                               in_specs=[spec, half, half, spec, spec], out_specs=[spec] * 4),
        out_shape=[_sds((r, c))] * 4, compiler_params=_cp("parallel"),
    )(cidx, w, g_mine, g_other, m, v)


def _block_diag(w):
    eye = jnp.eye(NH, dtype=w.dtype)
    return (w[:, :, None, :] * eye[:, None, :, None]).reshape(D, D)


def _diag_blocks(full):
    eye = jnp.eye(NH, dtype=full.dtype)
    return (full.reshape(NH, HP, NH, HP) * eye[:, None, :, None]).sum(axis=2)


def _pad_lanes(v, n=128):
    return jnp.pad(v, ((0, 0), (0, n - v.shape[1])))


def _local_step(x, target, p, dist=None):
    heads = jnp.arange(D, dtype=jnp.int32) // HP
    expand = (jnp.arange(128, dtype=jnp.int32)[:, None] == heads[None, :]).astype(F32)
    reduce_ = expand.T
    dskip_x = jnp.repeat(p["d_skip"], HP, axis=1)
    a_log = _pad_lanes(p["a_log"])
    dt_bias = _pad_lanes(p["dt_bias"])
    wab = jnp.concatenate([_block_diag(p["lru_wa"]), _block_diag(p["lru_wx"])], axis=1).astype(MXU)
    ba = p["lru_ba"].reshape(1, D)
    bx = p["lru_bx"].reshape(1, D)

    def hosted(key, fn):
        plan = dist.plan(key) if dist is not None else None
        if plan is None:
            return fn(plan=None)
        outs, got = fn(plan=plan)
        dist.done(key, got, p)
        return outs

    u = hosted("norm_u", functools.partial(_norm_cast, x, p["norm_mix_pre"], "norm_u"))
    w5, wxbc, wdt = p["w5"], p["wxbc"], p["wdt"]
    proj5 = hosted("proj5", functools.partial(_matmul, u, w5, name="proj5", tb=True, tm=1024, out_dtype=MXU))
    xbc_raw = _matmul(u, wxbc, name="proj_xbc", tb=True, tn=XBC)
    dt_raw = _matmul(u, wdt, name="proj_dt", tb=True)
    xbc_c, dsilu, xr, dt = _conv_fwd(xbc_raw, proj5, dt_raw, p["conv_ssm_w"], p["conv_ssm_b"], p["conv_lru_w"],
                                     p["conv_lru_b"], dt_bias)
    y, ya, states = _ssd_fwd(xbc_c, dt, proj5, a_log, dskip_x, p["ssm_norm"], expand)
    h, merged, *gates = hosted("lru_fwd", functools.partial(_lru_fwd, xr, proj5, ya, wab, ba, bx, p["lru_lambda"]))
    mix, h1, v, pre = _out_up_proj(merged, p["w_out"], x, p["norm_mix_post"], p["norm_mlp_pre"], p["w_up"])
    dout, dff, loss, dg4 = _down_loss(pre, p["w_down"], h1, target, p["norm_mlp_post"])

    dpre = _matmul(dff, p["w_down"], name="d_pre", tb=True, tm=1024, out_dtype=MXU,
                   epi=lambda r, pr: r * (2.0 * jnp.maximum(pr.astype(F32), 0.0)), epi_args=(pre,))
    g_w_down = _matmul(pre, dff, name="dw_down", ta=True, tm=1024, tn=1024, tk=TK_GRAD, a_fn=_relu2)
    dh1, dmix, dmerged, dg3, dg2 = _dv_norms(dpre, p["w_up"], h1, mix, dout, p["norm_mlp_pre"], p["norm_mix_post"],
                                             p["w_out"])
    g_w_up = _matmul(v, dpre, name="dw_up", ta=True, tm=1024, tn=1024, tk=TK_GRAD)
    g_w_out = _matmul(merged, dmix, name="dw_out", ta=True, tm=1024, tn=1024, tk=TK_GRAD)
    if dist is not None:
        dist.early_grads(w_down=g_w_down, w_up=g_w_up, w_out=g_w_out)
    dya, d3, dxr, dp2, dlam, dba, dbx = hosted("lru_bwd", functools.partial(
        _lru_bwd, dmerged, ya, xr, h, proj5, gates, wab.T, p["lru_lambda"]))
    g_wab = _matmul(xr, dp2, name="dw_lru", ta=True, tm=1024, tn=1024, tk=TK_GRAD)
    g_wa, g_wx = _diag_blocks(g_wab[:, :D]), _diag_blocks(g_wab[:, D:])
    dz, dxbc_c, ddt, dng, ddsk, dalog = hosted("ssd_bwd", functools.partial(
        _ssd_bwd, dya, y, proj5, xbc_c, dt, states, a_log, dskip_x, p["ssm_norm"], expand, reduce_))
    (dxbc, dxl, ddtr, dcws, dcbs, dcwl, dcbl, ddtb) = hosted("conv_bwd", functools.partial(
        _conv_bwd, dxbc_c, dsilu, dxr, ddt, xbc_raw, proj5, dt_raw, p["conv_ssm_w"], p["conv_lru_w"], dt_bias))
    gw3 = _matmul(d3, u, name="dw_in_lru", ta=True, tm=1024, tn=1024, tk=TK_GRAD)
    gwz = _matmul(dz, u, name="dw_in_z", ta=True, tm=1024, tn=1024, tk=TK_GRAD)
    gwxl = _matmul(dxl, u, name="dw_in_xl", ta=True, tm=1024, tn=1024, tk=TK_GRAD)
    gwxbc = _matmul(dxbc, u, name="dw_in_xbc", ta=True, tm=XBC, tn=1024, tk=TK_GRAD)
    gwdt = _matmul(ddtr, u, name="dw_in_dt", ta=True, tm=128, tn=1024, tk=TK_GRAD)
    g_w_in_t = jnp.concatenate([gwz, gwxbc, gwdt[:NH], gw3[:D], gwxl, gw3[D:2 * D], gw3[2 * D:]], axis=0)
    grads = {
        "w_in_t": g_w_in_t, "conv_ssm_w": dcws, "conv_ssm_b": dcbs, "dt_bias": ddtb[:, :NH],
        "a_log": dalog[:, :NH], "d_skip": ddsk.reshape(NH, HP).sum(axis=1)[None, :], "ssm_norm": dng,
        "conv_lru_w": dcwl, "conv_lru_b": dcbl, "lru_wa": g_wa, "lru_ba": dba.reshape(NH, HP), "lru_wx": g_wx,
        "lru_bx": dbx.reshape(NH, HP), "lru_lambda": dlam, "w_out": g_w_out, "norm_mix_post": dg2,
        "norm_mlp_pre": dg3, "w_up": g_w_up, "w_down": g_w_down, "norm_mlp_post": dg4,
    }
    if dist is not None:
        dist.late_grads(grads, loss[0, 0])
    grad_x, grads["norm_mix_pre"] = hosted("du_norm", functools.partial(
        _du_norm, d3, dz, dxl, dxbc, ddtr, w5, wxbc, wdt, x, dh1, p["norm_mix_pre"]))
    return loss[0, 0], grad_x, grads


def _split_w_in_t(w_in_t):
    z, xbc, dtc, g, xl, ga, gb = jnp.split(w_in_t, [D, D + XBC, D + XBC + NH, 2 * D + XBC + NH,
                                                    3 * D + XBC + NH, 4 * D + XBC + NH], axis=0)
    return jnp.concatenate([g, ga, gb, z, xl], axis=0), xbc, jnp.pad(dtc, ((0, 128 - NH), (0, 0)))


COMM = BF16


def _place():
    x, y, c = lax.axis_index("x"), lax.axis_index("y"), lax.axis_index("c")
    chips = [(1 - x, y), (x, 1 - y), (1 - x, 1 - y)]
    return x, y, c, chips


def _remote(src, dst, send_sem, recv_sem, to):
    return pltpu.make_async_remote_copy(src_ref=src, dst_ref=dst, send_sem=send_sem, recv_sem=recv_sem, device_id=to,
                                        device_id_type=MESH)


def _gather_plan(big, small=(), axes=None):
    nb = len(big)
    arrs = list(big) + list(small)
    na = len(arrs)
    axes = list(axes or [0] * nb)

    def half(ref, a, k, which):
        h = arrs[a].shape[axes[a]] // 2
        cut = (pl.ds(which * h, h),) if axes[a] == 0 else (slice(None), pl.ds(which * h, h))
        return ref.at[cut] if k is None else ref.at[(k,) + cut]

    def direct(ins, outs, send, recv):
        x, y, c, chips = _place()
        k = 2 * x + y
        cps = []
        for a in range(na):
            src, dst = (half(ins[a], a, None, c), half(outs[a], a, k, c)) if a < nb else (ins[a], outs[a].at[k])
            cps += [_remote(src, dst, send.at[a, j], recv.at[a, j], (cx, cy, c)) for j, (cx, cy) in enumerate(chips)]
        return cps

    def passed(outs, send, recv):
        x, y, c, chips = _place()
        cps = []
        for j, (cx, cy) in enumerate(chips):
            for a in range(nb):
                got = half(outs[a], a, 2 * cx + cy, c)
                cps.append(_remote(got, got, send.at[a, 3 + j], recv.at[a, 3 + j], (x, y, 1 - c)))
        return cps

    def start(ins, outs, sems):
        for cp in direct(ins, outs, *sems[0]):
            cp.start()

    def mid(ins, outs, sems):
        send, recv = sems[0]
        _, _, c, chips = _place()
        fwd = passed(outs, send, recv)
        for j, (cx, cy) in enumerate(chips):
            kj = 2 * cx + cy
            for a in range(na):
                got = half(outs[a], a, kj, c) if a < nb else outs[a].at[kj]
                _remote(got, got, send.at[a, j], recv.at[a, j], (cx, cy, c)).wait_recv()
                if a < nb:
                    fwd[j * nb + a].start()

    def finish(ins, outs, sems):
        send, recv = sems[0]
        x, y, c, chips = _place()
        for j, (cx, cy) in enumerate(chips):
            for a in range(nb):
                got = half(outs[a], a, 2 * cx + cy, 1 - c)
                _remote(got, got, send.at[a, 3 + j], recv.at[a, 3 + j], (x, y, 1 - c)).wait_recv()
        for cp in direct(ins, outs, send, recv) + passed(outs, send, recv):
            cp.wait_send()

    return _Plan(arrs, [_sds((NCHIP,) + a.shape, a.dtype) for a in arrs], [(na, 6)], start, finish, mid)


def _own_shards(gathered, shards):
    kchip = 2 * lax.axis_index("x") + lax.axis_index("y")
    return [lax.dynamic_update_index_in_dim(o, a, kchip, 0) for o, a in zip(gathered, shards)]


def _swap_plan(ins, outs, sems, copies):
    def start(i, o, s):
        for cp in copies(i, o, *s[0]):
            cp.start()

    def finish(i, o, s):
        for cp in copies(i, o, *s[0]):
            cp.wait()

    return _Plan(ins, outs, [sems], start, finish)


def _half_shape(shape, axis):
    return tuple(d // 2 if i == axis else d for i, d in enumerate(shape))


def _pair_exchange_plan(gs, axis=1):
    def copies(ins, outs, send, recv):
        x, y, c, _ = _place()
        cps = []
        for a in range(len(gs)):
            h = ins[a].shape[axis] // 2
            theirs = pl.ds((1 - c) * h, h)
            src = ins[a].at[:, theirs] if axis == 1 else ins[a].at[:, :, theirs]
            cps.append(_remote(src, outs[a], send.at[a], recv.at[a], (x, y, 1 - c)))
        return cps

    return _swap_plan(gs, [_sds(_half_shape(g.shape, axis), g.dtype) for g in gs], (len(gs),), copies)


def _pair_add(g, got, cidx, name, axis=1):
    half = _half_shape(g.shape, axis)
    blk, nt, _, part = _half_blocks(g.shape[1:], axis - 1)

    def body(c_ref, g_ref, o_ref, p_ref, pc_ref):
        sm = g_ref[...] + o_ref[...]
        p_ref[...] = sm
        pc_ref[...] = sm.astype(pc_ref.dtype)

    def mine(k, i, c_ref):
        j = c_ref[0] * nt + i
        return (k, j, 0) if axis == 1 else (k, 0, j)

    spec = pl.BlockSpec((1,) + blk, lambda k, i, c_ref: (k,) + part(i))
    return pl.pallas_call(
        body, name=name,
        grid_spec=pltpu.PrefetchScalarGridSpec(
            num_scalar_prefetch=1, grid=(g.shape[0], nt),
            in_specs=[pl.BlockSpec((1,) + blk, mine), spec], out_specs=[spec, spec]),
        out_shape=[_sds(half), _sds(half, COMM)],
        compiler_params=_cp("parallel", "parallel"),
    )(cidx, g, got)


def _chip_exchange_plan(ps):
    def copies(ins, outs, send, recv):
        _, _, c, chips = _place()
        return [_remote(ins[a].at[2 * cx + cy], outs[a].at[j], send.at[a, j], recv.at[a, j], (cx, cy, c))
                for a in range(len(ps)) for j, (cx, cy) in enumerate(chips)]

    return _swap_plan(ps, [_sds((NCHIP - 1,) + p.shape[1:], p.dtype) for p in ps], (len(ps), 3), copies)


def _shard_sum(p, got, kidx, name, axis=1):
    full = tuple(2 * d if i == axis - 1 else d for i, d in enumerate(p.shape[1:]))
    blk, nt, _, part = _half_blocks(full, axis - 1)

    def body(k_ref, p_ref, g_ref, o_ref):
        sm = p_ref[0]
        for j in range(NCHIP - 1):
            sm = sm + g_ref[j].astype(F32)
        o_ref[...] = sm

    return pl.pallas_call(
        body, name=name,
        grid_spec=pltpu.PrefetchScalarGridSpec(
            num_scalar_prefetch=1, grid=(nt,),
            in_specs=[pl.BlockSpec((1,) + blk, lambda i, k_ref: (k_ref[0],) + part(i)),
                      pl.BlockSpec((NCHIP - 1,) + blk, lambda i, k_ref: (0,) + part(i))],
            out_specs=pl.BlockSpec(blk, lambda i, k_ref: part(i))),
        out_shape=_sds(p.shape[1:]),
        compiler_params=_cp("parallel"),
    )(kidx, p, got)


def _pair_swap_plan(rs):
    def copies(ins, outs, send, recv):
        x, y, c, _ = _place()
        return [_remote(ins[a], outs[a], send.at[a], recv.at[a], (x, y, 1 - c)) for a in range(len(rs))]

    return _swap_plan(rs, [_sds(r.shape, r.dtype) for r in rs], (len(rs),), copies)


def _allgather8_plan(v):
    def pieces(ins, outs, send, recv):
        x, y, c, chips = _place()
        me, sibling = (x, y, c), (x, y, 1 - c)

        def copy(k, block, to, src=None):
            px, py, pc = block
            slot = outs[0].at[4 * px + 2 * py + pc]
            return _remote(slot if src is None else src, slot, send.at[k], recv.at[k], to)

        first = [copy(0, me, sibling, src=ins[0])] + [copy(1 + j, me, (*chip, c), src=ins[0])
                                                      for j, chip in enumerate(chips)]
        passed = [copy(4 + j, (*chip, c), sibling) for j, chip in enumerate(chips)]
        arrivals = [copy(1 + j, (*chip, c), me) for j, chip in enumerate(chips)]
        late = [copy(0, sibling, me)] + [copy(4 + j, (*chip, 1 - c), me) for j, chip in enumerate(chips)]
        return first, passed, arrivals, late

    def start(ins, outs, sems):
        for cp in pieces(ins, outs, *sems[0])[0]:
            cp.start()

    def mid(ins, outs, sems):
        _, passed, arrivals, _ = pieces(ins, outs, *sems[0])
        for got, fwd in zip(arrivals, passed):
            got.wait_recv()
            fwd.start()

    def finish(ins, outs, sems):
        first, passed, _, late = pieces(ins, outs, *sems[0])
        for got in late:
            got.wait_recv()
        for cp in first + passed:
            cp.wait_send()

    return _Plan([v], [_sds((8,) + v.shape, v.dtype)], [(7,)], start, finish, mid)


def _own_block(gathered, v):
    me = 4 * lax.axis_index("x") + 2 * lax.axis_index("y") + lax.axis_index("c")
    return lax.dynamic_update_index_in_dim(gathered, v, me, 0)


def _sum_devices(allv, name):
    _, r, _ = allv.shape

    def body(a_ref, o_ref):
        sm = a_ref[0]
        for d in range(1, 8):
            sm = sm + a_ref[d]
        o_ref[...] = sm

    return pl.pallas_call(
        body, name=name, grid=(1,), in_specs=[_full_spec((8, r, 128))], out_specs=_full_spec((r, 128)),
        out_shape=_sds((r, 128)), compiler_params=_cp("arbitrary"),
    )(allv)


def _pack(arrs):
    flat = jnp.concatenate([a.reshape(-1) for a in arrs])
    return jnp.pad(flat, (0, (-flat.shape[0]) % 1024)).reshape(-1, 128)


def _unpack(packed, shapes):
    flat, outs, off = packed.reshape(-1), [], 0
    for shp in shapes:
        n = math.prod(shp)
        outs.append(flat[off:off + n].reshape(shp))
        off += n
    return outs


BIG = ("w_in", "w_out", "w_up", "w_down")
CONV = ("conv_ssm_w", "conv_lru_w")
WEIGHTS = ("norm_mix_pre", "w_in", "conv_ssm_w", "conv_ssm_b", "dt_bias", "a_log", "d_skip", "ssm_norm", "conv_lru_w",
           "conv_lru_b", "lru_wa", "lru_ba", "lru_wx", "lru_bx", "lru_lambda", "w_out", "norm_mix_post",
           "norm_mlp_pre", "w_up", "w_down", "norm_mlp_post")
SMALL = tuple(n for n in WEIGHTS if n not in BIG and n not in CONV)
EARLY = ("w_down", "w_up", "w_out")


def _cat_cols(g):
    return jnp.concatenate([g[k] for k in range(NCHIP)], axis=1)


class _Dist:
    def __init__(self, shards, first, cidx, kidx):
        self.shards, self.first, self.cidx, self.kidx = shards, first, cidx, kidx
        self.halves = {}

    def early_grads(self, w_down, w_up, w_out):
        self.shard_major = [w_down.reshape(NCHIP, D, D), jnp.stack([w_up[:, D * k:D * (k + 1)] for k in range(NCHIP)]),
                            w_out.reshape(NCHIP, D // NCHIP, D)]

    def late_grads(self, grads, loss):
        g_in = grads["w_in_t"][None]
        got, = _run_plan(_pair_exchange_plan([g_in], axis=2), "grad_pair_exchange_w_in")
        p_all, pc_all = _pair_add(g_in, got, self.cidx, "grad_pair_add_w_in", axis=2)
        self.p_in = lax.dynamic_slice_in_dim(p_all[0], self.kidx[0] * W_IN_SHARD, W_IN_SHARD, axis=0)[None]
        self.pc_in = pc_all.reshape(NCHIP, W_IN_SHARD, D // 2)
        self.small_names = [n for n in SMALL + CONV if n != "norm_mix_pre"]
        self.small_shapes = [grads[n].shape for n in self.small_names] + [(1,)]
        self.packed_small = _pack([grads[n] for n in self.small_names] + [loss.reshape(1)])

    def plan(self, key):
        if key == "norm_u":
            return _gather_plan(self.first[:1], self.first[1:], axes=[1])
        if key == "proj5":
            return _gather_plan([self.shards["w_out"], self.shards["w_up"]])
        if key == "lru_fwd":
            return _gather_plan([self.shards["w_down"]])
        if key == "lru_bwd":
            return _pair_exchange_plan(self.shard_major)
        if key == "ssd_bwd":
            return _chip_exchange_plan([pc for _, pc in self.pair])
        if key == "conv_bwd":
            return _pair_swap_plan(self.mine)
        if key == "du_norm":
            return _merge_plans(_allgather8_plan(self.packed_small), _chip_exchange_plan([self.pc_in]))
        return None

    def done(self, key, got, p):
        if key == "norm_u":
            g_in, g_cs, g_cl = _own_shards(got, self.first)
            w5, wxbc, wdt = _split_w_in_t(g_in.reshape(W_IN_COLS, D))
            p.update(w5=w5, wxbc=wxbc, wdt=wdt, conv_ssm_w=_cat_cols(g_cs), conv_lru_w=_cat_cols(g_cl))
        elif key == "proj5":
            g_out, g_up = _own_shards(got, [self.shards["w_out"], self.shards["w_up"]])
            p.update(w_out=g_out.reshape(D, D), w_up=_cat_cols(g_up))
        elif key == "lru_fwd":
            g_down, = _own_shards(got, [self.shards["w_down"]])
            p.update(w_down=g_down.reshape(DFF, D))
        elif key == "lru_bwd":
            self.pair = [_pair_add(gs, o, self.cidx, f"grad_pair_add_{n}")
                         for gs, o, n in zip(self.shard_major, got, EARLY)]
        elif key == "ssd_bwd":
            self.mine = [_shard_sum(pf, o, self.kidx, f"grad_shard_sum_{n}")
                         for (pf, _), o, n in zip(self.pair, got, EARLY)]
        elif key == "conv_bwd":
            self.halves = {n: (mine, other) for n, mine, other in zip(EARLY, self.mine, got)}
        elif key == "du_norm":
            self.all_small, self.from_chips_in = got


def kernel(x, norm_mix_pre, w_in, conv_ssm_w, conv_ssm_b, dt_bias, a_log, d_skip, ssm_norm, conv_lru_w, conv_lru_b, lru_wa, lru_ba, lru_wx, lru_bx, lru_lambda, w_out, norm_mix_post, norm_mlp_pre, w_up, w_down, norm_mlp_post, loss_target, m_norm_mix_pre, m_w_in, m_conv_ssm_w, m_conv_ssm_b, m_dt_bias, m_a_log, m_d_skip, m_ssm_norm, m_conv_lru_w, m_conv_lru_b, m_lru_wa, m_lru_ba, m_lru_wx, m_lru_bx, m_lru_lambda, m_w_out, m_norm_mix_post, m_norm_mlp_pre, m_w_up, m_w_down, m_norm_mlp_post, v_norm_mix_pre, v_w_in, v_conv_ssm_w, v_conv_ssm_b, v_dt_bias, v_a_log, v_d_skip, v_ssm_norm, v_conv_lru_w, v_conv_lru_b, v_lru_wa, v_lru_ba, v_lru_wx, v_lru_bx, v_lru_lambda, v_w_out, v_norm_mix_post, v_norm_mlp_pre, v_w_up, v_w_down, v_norm_mlp_post):
    args = locals()
    w = {n: args[n][0] for n in WEIGHTS}
    m = {n: args["m_" + n][0] for n in WEIGHTS}
    v = {n: args["v_" + n][0] for n in WEIGHTS}
    cidx = lax.axis_index("c").astype(jnp.int32).reshape(1)
    kchip = 2 * lax.axis_index("x") + lax.axis_index("y")
    to_t = lambda a: jnp.transpose(a, (2, 0, 1)).reshape(W_IN_SHARD, D)
    from_t = lambda a: jnp.transpose(a.reshape(W_IN_SHARD, 1, D), (1, 2, 0))
    shards = {n: (to_t(w_in) if n == "w_in" else w[n]).astype(MXU) for n in BIG}
    dist = _Dist(shards, [shards["w_in"], w["conv_ssm_w"], w["conv_lru_w"]], cidx, kchip.astype(jnp.int32).reshape(1))
    p = {n: (w[n].reshape(1, -1) if w[n].ndim == 1 else w[n]) for n in SMALL}

    _, grad_x, g = _local_step(x[0], loss_target[0], p, dist)

    half_in = _shard_sum(dist.p_in, dist.from_chips_in, jnp.zeros((1,), jnp.int32), "grad_shard_sum_w_in", axis=2)
    packed_g1 = _pack([g["norm_mix_pre"]])
    all_g1, other_in = _run_plan(_merge_plans(_allgather8_plan(packed_g1), _pair_swap_plan([half_in])),
                                 "grad_pair_swap_w_in")
    halves = dist.halves

    reduced = {}
    *summed, loss = _unpack(_sum_devices(_own_block(dist.all_small, dist.packed_small), "small_sum"),
                            dist.small_shapes)
    loss = loss.reshape(())
    g1, = _unpack(_sum_devices(_own_block(all_g1, packed_g1), "small_sum_norm_mix_pre"), [g["norm_mix_pre"].shape])
    for n, s in zip(dist.small_names + ["norm_mix_pre"], summed + [g1]):
        if n in CONV:
            width = w[n].shape[1]
            reduced[n] = lax.dynamic_slice_in_dim(s, kchip * width, width, axis=1)
        else:
            reduced[n] = s.reshape(w[n].shape)

    delta, new_m, new_v = {}, {}, {}
    for n in EARLY:
        mine, other = halves[n]
        reduced[n], delta[n], new_m[n], new_v[n] = _adamw_halves(w[n], mine, other, m[n], v[n], cidx, f"adamw_{n}")
    outs_t = _adamw_halves(to_t(w_in), half_in, other_in, to_t(m_w_in), to_t(v_w_in), cidx, "adamw_w_in", axis=1)
    for d, o in zip((reduced, delta, new_m, new_v), outs_t):
        d["w_in"] = from_t(o)[0]
    for n in CONV:
        delta[n], new_m[n], new_v[n] = _adamw(w[n], reduced[n], m[n], v[n], f"adamw_{n}")
    shapes = [w[n].shape for n in SMALL]
    packed = [_pack([d[n] for n in SMALL]) for d in (w, reduced, m, v)]
    for d, out in zip((delta, new_m, new_v), _adamw(*packed, "adamw_small")):
        d.update(zip(SMALL, _unpack(out, shapes)))

    lead = lambda d: [d[n][None] for n in WEIGHTS]
    return (loss, grad_x[None], *lead(reduced), *lead(delta), *lead(new_m), *lead(new_v))
```

```python
import functools
import math

import jax
import jax.numpy as jnp
from jax import lax
from jax.experimental import pallas as pl
from jax.experimental.pallas import tpu as pltpu

F32 = jnp.float32
BF16 = jnp.bfloat16
MXU = BF16

D = 1024
DFF = 4096
NH = 16
HP = 64
NG = 2
NS = 128
CH = 128
XBC = D + 2 * NG * NS
GW = D // NG
LRU_C = 8.0
EPS = 1e-6
NCHIP = 4
W_IN_COLS = 6672
W_IN_SHARD = W_IN_COLS // NCHIP

ADAM_LR = 0.001
ADAM_B1 = 0.9
ADAM_B2 = 0.999
ADAM_EPS = 1e-08
ADAM_WD = 0.01
ADAM_STEP = 10

VMEM_LIMIT = 56 * 1024 * 1024
TK_GRAD = 2048
MID_AT = 0.7
ROWS_FUSED = 512
ROWS_CONV = 512
COL_G, COL_GA, COL_GB, COL_Z, COL_XL = range(5)
MESH = pl.DeviceIdType.MESH


def _cp(*sem):
    return pltpu.CompilerParams(dimension_semantics=sem, vmem_limit_bytes=VMEM_LIMIT)


def _dot(a, b, ca=1, cb=0, prec=None):
    return lax.dot_general(a, b, (((ca,), (cb,)), ((), ())), precision=prec, preferred_element_type=F32)


def _mdot(a, b, ca=1, cb=0):
    return _dot(a.astype(MXU), b.astype(MXU), ca, cb)


def _bf16_parts(v, n):
    parts = []
    for i in range(n):
        p = v.astype(BF16)
        parts.append(p)
        if i < n - 1:
            v = v - p.astype(F32)
    return parts


def _xdot(a, b, passes, split_b=False):
    if split_b:
        a16 = a.astype(BF16)
        terms = [_dot(a16, p) for p in _bf16_parts(b, passes)]
    else:
        b16 = b.astype(BF16)
        terms = [_dot(p, b16) for p in _bf16_parts(a, passes)]
    return functools.reduce(lambda u, v: u + v, terms)


def _sig(x):
    return 0.5 * jnp.tanh(0.5 * x) + 0.5


def _silu(x):
    return x * _sig(x)


def _dsilu(x):
    s = _sig(x)
    return s * (1.0 + x * (1.0 - s))


def _softplus(x):
    e = jnp.exp(-jnp.abs(x))
    return jnp.maximum(x, 0.0) + jnp.where(e < 1e-4, e * (1.0 - 0.5 * e), jnp.log(1.0 + e))


_GELU_C = math.sqrt(2.0 / math.pi)


def _gelu(x):
    t = jnp.tanh(_GELU_C * (x + 0.044715 * x * x * x))
    return 0.5 * x * (1.0 + t)


def _gelu_and_grad(x):
    x2 = x * x
    t = jnp.tanh(_GELU_C * (x + 0.044715 * x * x2))
    half = 0.5 * (1.0 + t)
    return x * half, half + 0.5 * x * (1.0 - t * t) * _GELU_C * (1.0 + 3.0 * 0.044715 * x2)


def _one_minus_sq(a, la):
    x = 2.0 * la
    series = -x * (1.0 + x * (0.5 + x * (1.0 / 6.0)))
    return jnp.where(x > -0.01, series, 1.0 - a * a)


def _rms(x):
    return lax.rsqrt(jnp.mean(x * x, axis=-1, keepdims=True) + EPS)


def _rms_bwd(x, r, g, dy):
    xn = x * r
    dxh = dy * g
    m = jnp.mean(dxh * xn, axis=-1, keepdims=True)
    return r * (dxh - xn * m), jnp.sum(dy * xn, axis=0, keepdims=True)


def _row_spec(t, c, col=0):
    return pl.BlockSpec((t, c), lambda i: (i, col))


def _rev_spec(t, c, n, col=0):
    return pl.BlockSpec((t, c), lambda i: (n - 1 - i, col))


def _full_spec(shape, once=False):
    nd = len(shape)
    if once:
        return pl.BlockSpec(shape, lambda *_: (0,) * nd, pipeline_mode=pl.Buffered(1))
    return pl.BlockSpec(shape, lambda *_: (0,) * nd)


def _sds(shape, dtype=F32):
    return jax.ShapeDtypeStruct(shape, dtype)


ANY = pl.BlockSpec(memory_space=pl.ANY)


class _Plan:
    def __init__(self, ins, outs, sems, start, finish, mid=None):
        self.ins, self.outs, self.sems = list(ins), list(outs), list(sems)
        self.start, self.finish, self.mid = start, finish, mid or (lambda i, o, s: None)


def _merge_plans(*plans):
    def each(fn_name, ins, outs, sems):
        i = o = s = 0
        for p in plans:
            getattr(p, fn_name)(ins[i:i + len(p.ins)], outs[o:o + len(p.outs)], sems[s:s + len(p.sems)])
            i, o, s = i + len(p.ins), o + len(p.outs), s + len(p.sems)

    return _Plan([a for p in plans for a in p.ins], [a for p in plans for a in p.outs],
                 [a for p in plans for a in p.sems], functools.partial(each, "start"),
                 functools.partial(each, "finish"), functools.partial(each, "mid"))


def _pcall(body, args, *, name, grid, in_specs, out_specs, out_shape, sem, scratch_shapes=(), plan=None):
    single = not isinstance(out_shape, (list, tuple))
    out_specs = [out_specs] if single else list(out_specs)
    out_shape = [out_shape] if single else list(out_shape)
    if plan is None:
        outs = pl.pallas_call(body, name=name, grid=grid, in_specs=list(in_specs), out_specs=out_specs,
                              out_shape=out_shape, scratch_shapes=list(scratch_shapes),
                              compiler_params=_cp(*sem))(*args)
        return outs[0] if single else outs
    n_in, n_out, n_sc, ni, no = len(in_specs), len(out_shape), len(scratch_shapes), len(plan.ins), len(plan.outs)

    def hosted(*refs):
        b0 = n_in + ni
        b1 = b0 + n_out + no
        sem_refs = refs[b1 + n_sc:]
        sems = [(sem_refs[2 * q], sem_refs[2 * q + 1]) for q in range(len(plan.sems))]
        step = functools.reduce(lambda lin, ig: lin * ig[1] + ig[0],
                                [(pl.program_id(d), g) for d, g in enumerate(grid)], 0)
        total = math.prod(grid)

        @pl.when(step == 0)
        def _():
            plan.start(refs[n_in:b0], refs[b0 + n_out:b1], sems)

        body(*refs[:n_in], *refs[b0:b0 + n_out], *refs[b1:b1 + n_sc])

        @pl.when(step == min(int(MID_AT * total), total - 1))
        def _():
            plan.mid(refs[n_in:b0], refs[b0 + n_out:b1], sems)

        @pl.when(step == total - 1)
        def _():
            plan.finish(refs[n_in:b0], refs[b0 + n_out:b1], sems)

    dma = [pltpu.SemaphoreType.DMA(shape) for shape in plan.sems for _ in range(2)]
    outs = pl.pallas_call(hosted, name=name, grid=grid, in_specs=list(in_specs) + [ANY] * ni,
                          out_specs=out_specs + [ANY] * no, out_shape=out_shape + plan.outs,
                          scratch_shapes=list(scratch_shapes) + dma,
                          compiler_params=_cp(*("arbitrary",) * len(grid)))(*args, *plan.ins)
    return (outs[0] if single else outs[:n_out]), outs[n_out:]


def _run_plan(plan, name):
    ni, no = len(plan.ins), len(plan.outs)

    def body(*refs):
        sem_refs = refs[ni + no:]
        sems = [(sem_refs[2 * q], sem_refs[2 * q + 1]) for q in range(len(plan.sems))]
        plan.start(refs[:ni], refs[ni:ni + no], sems)
        plan.mid(refs[:ni], refs[ni:ni + no], sems)
        plan.finish(refs[:ni], refs[ni:ni + no], sems)

    return pl.pallas_call(
        body, name=name, in_specs=[ANY] * ni, out_specs=[ANY] * no, out_shape=plan.outs,
        scratch_shapes=[pltpu.SemaphoreType.DMA(shape) for shape in plan.sems for _ in range(2)],
    )(*plan.ins)


def _matmul(a, b, *, name, ta=False, tb=False, tm=512, tn=1024, tk=1024, out_dtype=F32, a_fn=None, epi=None,
            epi_args=(), plan=None, col_blocks_major=False):
    m, k = (a.shape[1], a.shape[0]) if ta else a.shape
    n = b.shape[0] if tb else b.shape[1]
    tm, tn, tk = min(tm, m), min(tn, n), min(tk, k)
    nk = k // tk
    a_spec = pl.BlockSpec((tk, tm), lambda i, j, kk: (kk, i)) if ta else pl.BlockSpec((tm, tk), lambda i, j, kk: (i, kk))
    b_spec = pl.BlockSpec((tn, tk), lambda i, j, kk: (j, kk)) if tb else pl.BlockSpec((tk, tn), lambda i, j, kk: (kk, j))
    e_specs = [pl.BlockSpec((tm, tn), lambda i, j, kk: (i, j)) for _ in epi_args]
    ne = len(epi_args)

    def body(a_ref, b_ref, *rest):
        e_refs, o_ref = rest[:ne], rest[ne]
        av = a_ref[...]
        if a_fn is not None:
            av = a_fn(av)
        part = _mdot(av, b_ref[...], 0 if ta else 1, 1 if tb else 0)

        def finish(r):
            if epi is not None:
                r = epi(r, *[e[...] for e in e_refs])
            o_ref[...] = r.astype(o_ref.dtype)

        if nk == 1:
            finish(part)
            return
        acc_ref = rest[ne + 1]
        kk = pl.program_id(2)

        @pl.when(kk == 0)
        def _():
            acc_ref[...] = part

        @pl.when(jnp.logical_and(kk > 0, kk < nk - 1))
        def _():
            acc_ref[...] += part

        @pl.when(kk == nk - 1)
        def _():
            finish(acc_ref[...] + part)

    if col_blocks_major:
        out_spec = pl.BlockSpec((None, tm, tn), lambda i, j, kk: (j, i, 0))
        out_shape = _sds((n // tn, m, tn), out_dtype)
    else:
        out_spec = pl.BlockSpec((tm, tn), lambda i, j, kk: (i, j))
        out_shape = _sds((m, n), out_dtype)
    return _pcall(
        body, (a, b, *epi_args), name=name, grid=(m // tm, n // tn, nk),
        in_specs=[a_spec, b_spec] + e_specs, out_specs=out_spec, out_shape=out_shape,
        scratch_shapes=[pltpu.VMEM((tm, tn), F32)] if nk > 1 else [],
        sem=("parallel", "parallel", "arbitrary"), plan=plan)


def _relu2(p):
    p = jnp.maximum(p, jnp.zeros((), p.dtype))
    return p * p


def _norm_cast(x, g, name, plan=None):
    s = x.shape[0]
    t = min(512, s)

    def body(x_ref, g_ref, o_ref):
        xv = x_ref[...]
        o_ref[...] = (xv * _rms(xv) * g_ref[...]).astype(o_ref.dtype)

    return _pcall(body, (x, g), name=name, grid=(s // t,), in_specs=[_row_spec(t, D), _full_spec((1, D))],
                  out_specs=_row_spec(t, D), out_shape=_sds((s, D), MXU), sem=("parallel",), plan=plan)


def _conv_fwd(xbc_raw, proj5, dt_raw, cw_s, cb_s, cw_l, cb_l, dt_bias):
    s = xbc_raw.shape[0]
    t = min(ROWS_CONV, s)

    def body(xs_ref, xl_ref, dtr_ref, cws_ref, cbs_ref, cwl_ref, cbl_ref, dtb_ref, xc_ref, dsl_ref, xr_ref, dt_ref,
             bs_ref, bl_ref):
        @pl.when(pl.program_id(0) == 0)
        def _():
            bs_ref[0:8, :] = jnp.zeros((8, XBC), F32)
            bl_ref[0:8, :] = jnp.zeros((8, D), F32)

        bs_ref[8:t + 8, :] = xs_ref[...]
        bl_ref[8:t + 8, :] = xl_ref[...].astype(F32)

        def conv(buf, w_ref, b_ref):
            acc = b_ref[...] + w_ref[3:4, :] * buf[8:t + 8, :]
            for k in (1, 2, 3):
                acc = acc + w_ref[3 - k:4 - k, :] * buf[8 - k:t + 8 - k, :]
            return acc

        pre = conv(bs_ref, cws_ref, cbs_ref)
        sg = _sig(pre)
        xc_ref[...] = pre * sg
        dsl_ref[...] = (sg * (1.0 + pre * (1.0 - sg))).astype(dsl_ref.dtype)
        xr_ref[...] = conv(bl_ref, cwl_ref, cbl_ref)
        dt_ref[...] = _softplus(dtr_ref[...] + dtb_ref[...])
        bs_ref[0:8, :] = bs_ref[t:t + 8, :]
        bl_ref[0:8, :] = bl_ref[t:t + 8, :]

    return pl.pallas_call(
        body, name="conv_fwd", grid=(s // t,),
        in_specs=[_row_spec(t, XBC), _row_spec(t, D, COL_XL), _row_spec(t, 128), _full_spec((4, XBC)),
                  _full_spec((1, XBC)), _full_spec((4, D)), _full_spec((1, D)), _full_spec((1, 128))],
        out_specs=[_row_spec(t, XBC), _row_spec(t, XBC), _row_spec(t, D), _row_spec(t, 128)],
        out_shape=[_sds((s, XBC)), _sds((s, XBC), BF16), _sds((s, D)), _sds((s, 128))],
        scratch_shapes=[pltpu.VMEM((t + 8, XBC), F32), pltpu.VMEM((t + 8, D), F32)],
        compiler_params=_cp("arbitrary"),
    )(xbc_raw, proj5, dt_raw, cw_s, cb_s, cw_l, cb_l, dt_bias)


def _ssd_chunk_setup(dt_ref, alog_ref, e_ref, at_ref, dtt_ref):
    lane = lax.broadcasted_iota(jnp.int32, (CH, 128), 1)
    row = lax.broadcasted_iota(jnp.int32, (CH, 128), 0)
    lane1 = lax.broadcasted_iota(jnp.int32, (1, 128), 1)
    a = jnp.where(lane1 < NH, -jnp.exp(alog_ref[...]), 0.0)
    dtv = dt_ref[...]
    adt = dtv * a
    tril = row >= lane
    acum = _xdot(tril.astype(F32), adt, 3, split_b=True)
    alast = jnp.sum(adt, axis=0, keepdims=True)
    at_ref[...] = acum.T
    dtt_ref[...] = dtv.T
    e = e_ref[...]
    ea_x = _xdot(jnp.exp(acum), e, 2)
    ws = jnp.exp(alast - acum) * dtv
    ws_x = _xdot(ws, e, 2)
    eal = jnp.exp(alast)
    eal_x = jnp.max(_xdot(jnp.broadcast_to(eal, (8, 128)), e, 3), axis=0, keepdims=True)
    return dict(lane=lane, row=row, tril=tril, a=a, dtv=dtv, acum=acum, alast=alast, ea_x=ea_x, ws=ws, ws_x=ws_x,
                eal=eal, eal_x=eal_x)


def _head_decay(cs, at_ref, dtt_ref, h):
    col = jnp.sum(jnp.where(cs["lane"] == h, cs["acum"], 0.0), axis=1, keepdims=True)
    ld = jnp.where(cs["tril"], jnp.exp(jnp.minimum(col - at_ref[h:h + 1, :], 0.0)), 0.0)
    return ld, dtt_ref[h:h + 1, :]


def _ssd_fwd(xbc_c, dt, proj5, a_log, dskip_x, ssm_norm, expand):
    s = xbc_c.shape[0]
    nc = s // CH

    def body(xc_ref, dt_ref, z_ref, alog_ref, dsk_ref, ng_ref, e_ref, y_ref, ya_ref, st_ref, h_ref, at_ref, dtt_ref,
             yd_ref):
        @pl.when(pl.program_id(0) == 0)
        def _():
            h_ref[...] = jnp.zeros_like(h_ref)

        cs = _ssd_chunk_setup(dt_ref, alog_ref, e_ref, at_ref, dtt_ref)
        lane = cs["lane"]
        for g in range(NG):
            gs = slice(GW * g, GW * (g + 1))
            bg = xc_ref[:, D + NS * g:D + NS * (g + 1)]
            cg = xc_ref[:, D + NG * NS + NS * g:D + NG * NS + NS * (g + 1)]
            cb = _mdot(cg, bg, 1, 1)
            for j in range(4 * g, 4 * g + 4):
                ps = slice(128 * j, 128 * (j + 1))
                xp = xc_ref[:, ps]
                acc = jnp.zeros((CH, 128), F32)
                for hf in range(2):
                    ld, rowdt = _head_decay(cs, at_ref, dtt_ref, 2 * j + hf)
                    hm = (lane >= HP) if hf else (lane < HP)
                    acc = acc + _mdot(cb * ld * rowdt, jnp.where(hm, xp, 0.0))
                yd_ref[:, ps] = acc
            hg = h_ref[:, gs]
            yd_ref[:, gs] += _mdot(cg, hg) * cs["ea_x"][:, gs]
            st = _mdot(bg, xc_ref[:, gs] * cs["ws_x"][:, gs], 0, 0)
            st_ref[0, :, gs] = hg
            h_ref[:, gs] = cs["eal_x"][:, gs] * hg + st
        y = yd_ref[...] + dsk_ref[...] * xc_ref[:, 0:D]
        y_ref[...] = y
        yg = y * _silu(z_ref[...].astype(F32))
        for g in range(NG):
            gs = slice(GW * g, GW * (g + 1))
            seg = yg[:, gs]
            ya_ref[:, gs] = seg * _rms(seg) * ng_ref[:, gs]

    return pl.pallas_call(
        body, name="ssd_fwd", grid=(nc,),
        in_specs=[_row_spec(CH, XBC), _row_spec(CH, 128), _row_spec(CH, D, COL_Z), _full_spec((1, 128)),
                  _full_spec((1, D)), _full_spec((1, D)), _full_spec((128, D))],
        out_specs=[_row_spec(CH, D), _row_spec(CH, D), pl.BlockSpec((1, NS, D), lambda i: (i, 0, 0))],
        out_shape=[_sds((s, D)), _sds((s, D)), _sds((nc, NS, D))],
        scratch_shapes=[pltpu.VMEM((NS, D), F32), pltpu.VMEM((128, 128), F32), pltpu.VMEM((128, 128), F32),
                        pltpu.VMEM((CH, D), F32)],
        compiler_params=_cp("arbitrary"),
    )(xbc_c, dt, proj5, a_log, dskip_x, ssm_norm, expand)


def _lru_gates(xr, wab_ref, ba_ref, bx_ref, lam_ref):
    pre = _mdot(xr, wab_ref[...])
    gr = _sig(pre[:, 0:D] + ba_ref[...])
    gi = _sig(pre[:, D:2 * D] + bx_ref[...])
    sp = _softplus(-lam_ref[...])
    la = -LRU_C * gr * sp
    a = jnp.exp(la)
    oms = _one_minus_sq(a, la)
    inv_mult = lax.rsqrt(oms)
    return gr, gi, sp, a, oms * inv_mult, inv_mult


def _blocked_scan(a, u, carry_ref, a_ref, u_ref, c_ref, out_ref, reverse):
    t = a.shape[0]
    ns = t // 8

    def combine(av, uv, idx, n, sh):
        m = (idx < n - sh) if reverse else (idx >= sh)
        by = n - sh if reverse else sh
        return jnp.where(m, av * pltpu.roll(av, by, 0), av), jnp.where(m, uv + av * pltpu.roll(uv, by, 0), uv)

    row = lax.broadcasted_iota(jnp.int32, (t, D), 0)
    rin = jnp.bitwise_and(row, 7)
    for sh in (1, 2, 4):
        m = (rin < 8 - sh) if reverse else (rin >= sh)
        by = t - sh if reverse else sh
        a, u = jnp.where(m, a * pltpu.roll(a, by, 0), a), jnp.where(m, u + a * pltpu.roll(u, by, 0), u)
    a_ref[...] = a
    u_ref[...] = u
    edge = 0 if reverse else 7
    for j in range(ns):
        c_ref[j:j + 1, :] = a_ref[8 * j + edge:8 * j + edge + 1, :]
    at = c_ref[...]
    for j in range(ns):
        c_ref[j:j + 1, :] = u_ref[8 * j + edge:8 * j + edge + 1, :]
    ut = c_ref[...]
    srow = lax.broadcasted_iota(jnp.int32, (ns, D), 0)
    sh = 1
    while sh < ns:
        at, ut = combine(at, ut, srow, ns, sh)
        sh *= 2
    cv = carry_ref[0:1, :]
    ends = ut + at * cv
    last = 0 if reverse else ns - 1
    first = ns - 1 if reverse else 0
    c_ref[...] = jnp.where(srow == first, cv, pltpu.roll(ends, first if reverse else 1, 0))
    carry_ref[0:1, :] = jnp.sum(jnp.where(srow == last, ends, 0.0), axis=0, keepdims=True)
    for j in range(ns):
        sl = slice(8 * j, 8 * j + 8)
        out_ref[sl, :] = u_ref[sl, :] + a_ref[sl, :] * c_ref[j:j + 1, :]


def _lru_fwd(xr, proj5, ya, wab, ba, bx, lam, plan=None):
    s = xr.shape[0]
    t = min(256, s)

    def body(xr_ref, g_ref, ga_ref, gb_ref, ya_ref, wab_ref, ba_ref, bx_ref, lam_ref, h_ref, mg_ref, gr_ref,
             gi_ref, ao_ref, mo_ref, hc_ref, a_ref, u_ref, c_ref):
        @pl.when(pl.program_id(0) == 0)
        def _():
            hc_ref[...] = jnp.zeros_like(hc_ref)

        xrv = xr_ref[...]
        gr, gi, _, a, mult, _ = _lru_gates(xrv, wab_ref, ba_ref, bx_ref, lam_ref)
        gr_ref[...], gi_ref[...], ao_ref[...], mo_ref[...] = gr, gi, a, mult
        _blocked_scan(a, mult * gi * xrv, hc_ref, a_ref, u_ref, c_ref, h_ref, reverse=False)
        yb = h_ref[...] * _gelu(g_ref[...].astype(F32))
        mg_ref[...] = (_sig(ga_ref[...].astype(F32)) * ya_ref[...]
                       + _sig(gb_ref[...].astype(F32)) * yb).astype(mg_ref.dtype)

    return _pcall(
        body, (xr, proj5, proj5, proj5, ya, wab, ba, bx, lam), name="lru_fwd", grid=(s // t,),
        in_specs=[_row_spec(t, D), _row_spec(t, D, COL_G), _row_spec(t, D, COL_GA), _row_spec(t, D, COL_GB),
                  _row_spec(t, D), _full_spec((D, 2 * D), once=True), _full_spec((1, D)), _full_spec((1, D)),
                  _full_spec((1, D))],
        out_specs=[_row_spec(t, D)] * 6,
        out_shape=[_sds((s, D)), _sds((s, D), MXU)] + [_sds((s, D))] * 4,
        scratch_shapes=[pltpu.VMEM((8, D), F32), pltpu.VMEM((t, D), F32), pltpu.VMEM((t, D), F32),
                        pltpu.VMEM((t // 8, D), F32)],
        sem=("arbitrary",), plan=plan)


def _out_up_proj(merged, w_out, x, g2, g3, w_up):
    s = x.shape[0]
    t = min(ROWS_FUSED, s)

    def body(mg_ref, w_ref, x_ref, g2_ref, g3_ref, wu_ref, mix_ref, h1_ref, v_ref, pre_ref):
        mix = _mdot(mg_ref[...], w_ref[...])
        mix_ref[...] = mix
        h1 = x_ref[...] + mix * _rms(mix) * g2_ref[...]
        h1_ref[...] = h1
        v = (h1 * _rms(h1) * g3_ref[...]).astype(v_ref.dtype)
        v_ref[...] = v
        pre_ref[...] = _mdot(v, wu_ref[...]).astype(pre_ref.dtype)

    return pl.pallas_call(
        body, name="out_up_proj", grid=(s // t,),
        in_specs=[_row_spec(t, D), _full_spec((D, D), once=True), _row_spec(t, D), _full_spec((1, D)),
                  _full_spec((1, D)), _full_spec((D, DFF), once=True)],
        out_specs=[_row_spec(t, D), _row_spec(t, D), _row_spec(t, D), _row_spec(t, DFF)],
        out_shape=[_sds((s, D)), _sds((s, D)), _sds((s, D), MXU), _sds((s, DFF), MXU)],
        compiler_params=_cp("parallel"),
    )(merged, w_out, x, g2, g3, w_up)


def _down_loss(pre, w_down, h1, target, g4):
    s = pre.shape[0]
    t = min(ROWS_FUSED, s)

    def body(pre_ref, w_ref, h1_ref, tg_ref, g4_ref, dout_ref, dff_ref, loss_ref, dg4_ref):
        @pl.when(pl.program_id(0) == 0)
        def _():
            loss_ref[...] = jnp.zeros_like(loss_ref)
            dg4_ref[...] = jnp.zeros_like(dg4_ref)

        ff = _mdot(_relu2(pre_ref[...]), w_ref[...])
        r4 = _rms(ff)
        g4v = g4_ref[...]
        diff = h1_ref[...] + ff * r4 * g4v - tg_ref[...]
        sq = jnp.sum(jnp.sum(diff * diff, axis=1, keepdims=True), axis=0, keepdims=True)
        loss_ref[...] += (0.5 / D) * sq
        dout = diff * (1.0 / D)
        dout_ref[...] = dout
        dff, dg = _rms_bwd(ff, r4, g4v, dout)
        dff_ref[...] = dff.astype(dff_ref.dtype)
        dg4_ref[...] += dg

    return pl.pallas_call(
        body, name="down_loss", grid=(s // t,),
        in_specs=[_row_spec(t, DFF), _full_spec((DFF, D), once=True), _row_spec(t, D), _row_spec(t, D),
                  _full_spec((1, D))],
        out_specs=[_row_spec(t, D), _row_spec(t, D), _full_spec((1, 128)), _full_spec((1, D))],
        out_shape=[_sds((s, D)), _sds((s, D), MXU), _sds((1, 128)), _sds((1, D))],
        compiler_params=_cp("arbitrary"),
    )(pre, w_down, h1, target, g4)


def _dv_norms(dpre, w_up, h1, mix, dout, g3, g2, w_out):
    s = h1.shape[0]
    t = min(ROWS_FUSED, s)

    def body(dp_ref, w_ref, h1_ref, mix_ref, dout_ref, g3_ref, g2_ref, wo_ref, dh1_ref, dmix_ref, dmg_ref, dg3_ref,
             dg2_ref):
        @pl.when(pl.program_id(0) == 0)
        def _():
            dg3_ref[...] = jnp.zeros_like(dg3_ref)
            dg2_ref[...] = jnp.zeros_like(dg2_ref)

        dv = _mdot(dp_ref[...], w_ref[...], 1, 1)
        h1 = h1_ref[...]
        dh1n, dg3 = _rms_bwd(h1, _rms(h1), g3_ref[...], dv)
        dh1 = dout_ref[...] + dh1n
        dh1_ref[...] = dh1
        mix = mix_ref[...]
        dmix, dg2 = _rms_bwd(mix, _rms(mix), g2_ref[...], dh1)
        dmix = dmix.astype(dmix_ref.dtype)
        dmix_ref[...] = dmix
        dmg_ref[...] = _mdot(dmix, wo_ref[...], 1, 1)
        dg3_ref[...] += dg3
        dg2_ref[...] += dg2

    return pl.pallas_call(
        body, name="dv_norms", grid=(s // t,),
        in_specs=[_row_spec(t, DFF), _full_spec((D, DFF), once=True), _row_spec(t, D), _row_spec(t, D),
                  _row_spec(t, D), _full_spec((1, D)), _full_spec((1, D)), _full_spec((D, D), once=True)],
        out_specs=[_row_spec(t, D), _row_spec(t, D), _row_spec(t, D), _full_spec((1, D)), _full_spec((1, D))],
        out_shape=[_sds((s, D)), _sds((s, D), MXU), _sds((s, D)), _sds((1, D)), _sds((1, D))],
        compiler_params=_cp("arbitrary"),
    )(dpre, w_up, h1, mix, dout, g3, g2, w_out)


def _lru_bwd(dmerged, ya, xr, h, proj5, gates, wab_t, lam, plan=None):
    s = xr.shape[0]
    t = min(128, s)
    n = s // t
    rs = functools.partial(_rev_spec, t, D, n)

    def body(dm_ref, ya_ref, xr_ref, h_ref, hp_ref, g_ref, ga_ref, gb_ref, gr_ref, gi_ref, a_ref, m_ref, wab_ref,
             lam_ref, dya_ref, d3_ref, dxr_ref, dp2_ref, dlam_ref, dba_ref, dbx_ref, gc_ref, af_ref, an_ref, us_ref,
             c_ref, gs_ref):
        i = pl.program_id(0)

        @pl.when(i == 0)
        def _():
            gc_ref[...] = jnp.zeros_like(gc_ref)
            af_ref[...] = jnp.zeros_like(af_ref)
            dlam_ref[...] = jnp.zeros_like(dlam_ref)
            dba_ref[...] = jnp.zeros_like(dba_ref)
            dbx_ref[...] = jnp.zeros_like(dbx_ref)

        xrv = xr_ref[...]
        gr, gi, a, mult = gr_ref[...], gi_ref[...], a_ref[...], m_ref[...]
        sp = _softplus(-lam_ref[...])
        inv_mult = 1.0 / mult
        hv = h_ref[...]
        dm = dm_ref[...]
        sa = _sig(ga_ref[...].astype(F32))
        sb = _sig(gb_ref[...].astype(F32))
        gel, dgel = _gelu_and_grad(g_ref[...].astype(F32))
        dya = dm * sa
        dya_ref[...] = dya
        dyb = dm * sb
        dybh = dyb * hv
        d3_ref[:, 0:D] = (dybh * dgel).astype(d3_ref.dtype)
        d3_ref[:, D:2 * D] = (dya * ya_ref[...] * (1.0 - sa)).astype(d3_ref.dtype)
        d3_ref[:, 2 * D:3 * D] = (dybh * gel * (1.0 - sb)).astype(d3_ref.dtype)
        row = lax.broadcasted_iota(jnp.int32, (t, D), 0)
        an = jnp.where(row == t - 1, af_ref[0:1, :], pltpu.roll(a, t - 1, 0))
        _blocked_scan(an, dyb * gel, gc_ref, an_ref, us_ref, c_ref, gs_ref, reverse=True)
        gfull = gs_ref[...]
        af_ref[0:1, :] = jnp.sum(jnp.where(row == 0, a, 0.0), axis=0, keepdims=True)
        hlast = jnp.where(i == n - 1, 0.0, hp_ref[7:8, :])
        hprev = jnp.where(row == 0, hlast, pltpu.roll(hv, 1, 0))
        gx = gfull * xrv
        dgi = gx * mult
        dla = a * (gfull * hprev - gx * gi * a * inv_mult)
        dgr = dla * (-LRU_C * sp)
        dsp = jnp.sum(dla * (-LRU_C * gr), axis=0, keepdims=True)
        dlam_ref[...] += dsp * (-_sig(-lam_ref[...]))
        dpr = dgr * gr * (1.0 - gr)
        dpi = dgi * gi * (1.0 - gi)
        dp2_ref[:, 0:D] = dpr.astype(dp2_ref.dtype)
        dp2_ref[:, D:2 * D] = dpi.astype(dp2_ref.dtype)
        dba_ref[...] += jnp.sum(dpr, axis=0, keepdims=True)
        dbx_ref[...] += jnp.sum(dpi, axis=0, keepdims=True)
        dxr_ref[...] = gfull * mult * gi + _mdot(dp2_ref[...], wab_ref[...])

    hp_spec = pl.BlockSpec((8, D), lambda i: (jnp.maximum((n - 1 - i) * (t // 8) - 1, 0), 0))
    wide = lambda c: pl.BlockSpec((t, c), lambda i: (n - 1 - i, 0))
    return _pcall(
        body, (dmerged, ya, xr, h, h, proj5, proj5, proj5, *gates, wab_t, lam), name="lru_bwd", grid=(n,),
        in_specs=[rs(), rs(), rs(), rs(), hp_spec, rs(COL_G), rs(COL_GA), rs(COL_GB), rs(), rs(), rs(), rs(),
                  _full_spec((2 * D, D), once=True), _full_spec((1, D))],
        out_specs=[rs(), wide(3 * D), rs(), wide(2 * D), _full_spec((1, D)), _full_spec((1, D)), _full_spec((1, D))],
        out_shape=[_sds((s, D)), _sds((s, 3 * D), MXU), _sds((s, D)), _sds((s, 2 * D), MXU), _sds((1, D)),
                   _sds((1, D)), _sds((1, D))],
        scratch_shapes=[pltpu.VMEM((8, D), F32), pltpu.VMEM((8, D), F32), pltpu.VMEM((t, D), F32),
                        pltpu.VMEM((t, D), F32), pltpu.VMEM((t // 8, D), F32), pltpu.VMEM((t, D), F32)],
        sem=("arbitrary",), plan=plan)


def _ssd_bwd(dya, y, proj5, xbc_c, dt, states, a_log, dskip_x, ssm_norm, expand, reduce_, plan=None):
    s = xbc_c.shape[0]
    nc = s // CH
    rv = functools.partial(_rev_spec, CH)

    def body(dya_ref, y_ref, z_ref, xc_ref, dt_ref, st_ref, alog_ref, dsk_ref, ng_ref, e_ref, et_ref, dz_ref,
             dxc_ref, ddt_ref, dng_ref, ddsk_ref, dalog_ref, dh_ref, at_ref, dtt_ref, dat_ref, ddtt_ref, dy_ref,
             yoffdy_ref, xbds_ref):
        @pl.when(pl.program_id(0) == 0)
        def _():
            dh_ref[...] = jnp.zeros_like(dh_ref)
            dng_ref[...] = jnp.zeros_like(dng_ref)
            ddsk_ref[...] = jnp.zeros_like(ddsk_ref)
            dalog_ref[...] = jnp.zeros_like(dalog_ref)

        cs = _ssd_chunk_setup(dt_ref, alog_ref, e_ref, at_ref, dtt_ref)
        lane, row = cs["lane"], cs["row"]
        et = et_ref[...]
        for g in range(NG):
            gs = slice(GW * g, GW * (g + 1))
            yv = y_ref[:, gs]
            zv = z_ref[:, gs].astype(F32)
            sz = _silu(zv)
            yg = yv * sz
            dyav = dya_ref[:, gs]
            dyg, dng = _rms_bwd(yg, _rms(yg), ng_ref[:, gs], dyav)
            dng_ref[:, gs] += dng
            dy_ref[:, gs] = dyg * sz
            dz_ref[:, gs] = (dyg * yv * _dsilu(zv)).astype(dz_ref.dtype)
        dyv = dy_ref[...]
        xs = xc_ref[:, 0:D]
        ddsk_ref[...] += jnp.sum(dyv * xs, axis=0, keepdims=True)
        dxc_ref[:, 0:D] = dyv * dsk_ref[...]
        dat_ref[...] = jnp.zeros_like(dat_ref)
        ddtt_ref[...] = jnp.zeros_like(ddtt_ref)
        hh = jnp.sum(dh_ref[...] * st_ref[0], axis=0, keepdims=True)
        deal = jnp.max(_xdot(jnp.broadcast_to(hh, (8, D)), et, 3), axis=0, keepdims=True)
        d_acum = jnp.zeros((CH, 128), F32)
        for g in range(NG):
            gs = slice(GW * g, GW * (g + 1))
            bs_ = slice(D + NS * g, D + NS * (g + 1))
            cs_ = slice(D + NG * NS + NS * g, D + NG * NS + NS * (g + 1))
            bg = xc_ref[:, bs_]
            cg = xc_ref[:, cs_]
            cb = _mdot(cg, bg, 1, 1)
            hg = st_ref[0, :, gs]
            dhg = dh_ref[:, gs]
            dyg_ = dy_ref[:, gs]
            xsg = xc_ref[:, gs]
            ea = cs["ea_x"][:, gs]
            wsx = cs["ws_x"][:, gs]
            dp = dyg_ * ea
            yoffdy_ref[:, gs] = dp * _mdot(cg, hg)
            dc = _mdot(dp, hg, 1, 1)
            dhprev = _mdot(cg, dp, 0, 0)
            bds = _mdot(bg, dhg)
            dxc_ref[:, gs] += wsx * bds
            xbds_ref[:, gs] = xsg * bds
            db = _mdot(xsg * wsx, dhg, 1, 1)
            dh_ref[:, gs] = dhprev + cs["eal_x"][:, gs] * dhg
            dcbs = jnp.zeros((CH, CH), F32)
            for j in range(4 * g, 4 * g + 4):
                ps = slice(128 * j, 128 * (j + 1))
                xp = xc_ref[:, ps]
                dyp = dy_ref[:, ps]
                dxacc = jnp.zeros((CH, 128), F32)
                for hf in range(2):
                    hd = 2 * j + hf
                    ld, rowdt = _head_decay(cs, at_ref, dtt_ref, hd)
                    hm = (lane >= HP) if hf else (lane < HP)
                    dym = jnp.where(hm, dyp, 0.0)
                    w = cb * ld * rowdt
                    dw = _mdot(dym, jnp.where(hm, xp, 0.0), 1, 1)
                    dxacc = dxacc + _mdot(w, dym, 0, 0)
                    nm = dw * w
                    ddtt_ref[hd:hd + 1, :] += jnp.sum(dw * cb * ld, axis=0, keepdims=True)
                    d_acum = d_acum + jnp.where(lane == hd, jnp.sum(nm, axis=1, keepdims=True), 0.0)
                    dat_ref[hd:hd + 1, :] -= jnp.sum(nm, axis=0, keepdims=True)
                    dcbs = dcbs + dw * ld * rowdt
                dxc_ref[:, ps] += dxacc
            dxc_ref[:, bs_] = db + _mdot(dcbs, cg, 0, 0)
            dxc_ref[:, cs_] = dc + _mdot(dcbs, bg)
        dws = _xdot(xbds_ref[...], et, 2)
        ws = cs["ws"]
        d_acum = d_acum - dws * ws + _xdot(yoffdy_ref[...], et, 2) + dat_ref[...].T
        d_alast = jnp.sum(dws * ws, axis=0, keepdims=True) + deal * cs["eal"]
        d_acum = d_acum + jnp.where(row == CH - 1, d_alast, 0.0)
        triu = row <= lane
        d_adt = _xdot(triu.astype(F32), d_acum, 3, split_b=True)
        ddt_ref[...] = dws * jnp.exp(cs["alast"] - cs["acum"]) + ddtt_ref[...].T + d_adt * cs["a"]
        dalog_ref[...] += jnp.sum(d_adt * cs["dtv"], axis=0, keepdims=True) * cs["a"]

    return _pcall(
        body, (dya, y, proj5, xbc_c, dt, states, a_log, dskip_x, ssm_norm, expand, reduce_), name="ssd_bwd",
        grid=(nc,),
        in_specs=[rv(D, nc), rv(D, nc), rv(D, nc, COL_Z), rv(XBC, nc), rv(128, nc),
                  pl.BlockSpec((1, NS, D), lambda i: (nc - 1 - i, 0, 0)), _full_spec((1, 128)), _full_spec((1, D)),
                  _full_spec((1, D)), _full_spec((128, D)), _full_spec((D, 128))],
        out_specs=[rv(D, nc), rv(XBC, nc), rv(128, nc), _full_spec((1, D)), _full_spec((1, D)),
                   _full_spec((1, 128))],
        out_shape=[_sds((s, D), MXU), _sds((s, XBC)), _sds((s, 128)), _sds((1, D)), _sds((1, D)), _sds((1, 128))],
        scratch_shapes=[pltpu.VMEM((NS, D), F32), pltpu.VMEM((128, 128), F32), pltpu.VMEM((128, 128), F32),
                        pltpu.VMEM((128, 128), F32), pltpu.VMEM((128, 128), F32), pltpu.VMEM((CH, D), F32),
                        pltpu.VMEM((CH, D), F32), pltpu.VMEM((CH, D), F32)],
        sem=("arbitrary",), plan=plan)


def _conv_bwd(dxbc_c, dsilu, dxr, ddt, xbc_raw, proj5, dt_raw, cw_s, cw_l, dt_bias, plan=None):
    s = xbc_raw.shape[0]
    t = min(ROWS_CONV, s)
    n = s // t

    def body(dxc_ref, dsl_ref, dxr_ref, ddt_ref, xs_ref, xl_ref, dtr_ref, cws_ref, cwl_ref, dtb_ref, dxs_ref,
             dxl_ref, ddtr_ref, dcws_ref, dcbs_ref, dcwl_ref, dcbl_ref, ddtb_ref, ds_ref, dl_ref):
        @pl.when(pl.program_id(0) == 0)
        def _():
            ds_ref[t:t + 8, :] = jnp.zeros((8, XBC), F32)
            dl_ref[t:t + 8, :] = jnp.zeros((8, D), F32)
            for r in (dcws_ref, dcbs_ref, dcwl_ref, dcbl_ref, ddtb_ref):
                r[...] = jnp.zeros_like(r)

        ds_ref[0:t, :] = dxc_ref[...] * dsl_ref[...].astype(F32)
        dl_ref[0:t, :] = dxr_ref[...]

        def back(dbuf, x_ref, w_ref, dx_ref, dw_ref, db_ref):
            xv = x_ref[...].astype(F32)
            dpre = dbuf[0:t, :]
            dx = w_ref[3:4, :] * dpre
            dw_ref[3:4, :] += jnp.sum(dpre * xv, axis=0, keepdims=True)
            db_ref[...] += jnp.sum(dpre, axis=0, keepdims=True)
            for k in (1, 2, 3):
                ahead = dbuf[k:t + k, :]
                dx = dx + w_ref[3 - k:4 - k, :] * ahead
                dw_ref[3 - k:4 - k, :] += jnp.sum(ahead * xv, axis=0, keepdims=True)
            dx_ref[...] = dx.astype(dx_ref.dtype)
            dbuf[t:t + 8, :] = dbuf[0:8, :]

        back(ds_ref, xs_ref, cws_ref, dxs_ref, dcws_ref, dcbs_ref)
        back(dl_ref, xl_ref, cwl_ref, dxl_ref, dcwl_ref, dcbl_ref)
        ddtr = ddt_ref[...] * _sig(dtr_ref[...] + dtb_ref[...])
        ddtr_ref[...] = ddtr.astype(ddtr_ref.dtype)
        ddtb_ref[...] += jnp.sum(ddtr, axis=0, keepdims=True)

    rv = functools.partial(_rev_spec, t)
    return _pcall(
        body, (dxbc_c, dsilu, dxr, ddt, xbc_raw, proj5, dt_raw, cw_s, cw_l, dt_bias), name="conv_bwd", grid=(n,),
        in_specs=[rv(XBC, n), rv(XBC, n), rv(D, n), rv(128, n), rv(XBC, n), rv(D, n, COL_XL), rv(128, n),
                  _full_spec((4, XBC)), _full_spec((4, D)), _full_spec((1, 128))],
        out_specs=[rv(XBC, n), rv(D, n), rv(128, n), _full_spec((4, XBC)), _full_spec((1, XBC)), _full_spec((4, D)),
                   _full_spec((1, D)), _full_spec((1, 128))],
        out_shape=[_sds((s, XBC), MXU), _sds((s, D), MXU), _sds((s, 128), MXU), _sds((4, XBC)), _sds((1, XBC)),
                   _sds((4, D)), _sds((1, D)), _sds((1, 128))],
        scratch_shapes=[pltpu.VMEM((t + 8, XBC), F32), pltpu.VMEM((t + 8, D), F32)],
        sem=("arbitrary",), plan=plan)


def _du_norm(d3, dz, dxl, dxbc, ddtr, w5, wxbc, wdt, x, dh1, g1, plan=None):
    s = x.shape[0]
    t = min(ROWS_FUSED, s)

    def body(d3_ref, dz_ref, dxl_ref, dxbc_ref, ddtr_ref, w5_ref, wx_ref, wd_ref, x_ref, dh1_ref, g1_ref, dx_ref,
             dg1_ref):
        @pl.when(pl.program_id(0) == 0)
        def _():
            dg1_ref[...] = jnp.zeros_like(dg1_ref)

        du = (_mdot(dxbc_ref[...], wx_ref[...]) + _mdot(ddtr_ref[...], wd_ref[...])
              + _mdot(d3_ref[...], w5_ref[0:3 * D, :])
              + _mdot(dz_ref[...], w5_ref[COL_Z * D:(COL_Z + 1) * D, :])
              + _mdot(dxl_ref[...], w5_ref[COL_XL * D:(COL_XL + 1) * D, :]))
        xv = x_ref[...]
        dxn, dg1 = _rms_bwd(xv, _rms(xv), g1_ref[...], du)
        dx_ref[...] = dh1_ref[...] + dxn
        dg1_ref[...] += dg1

    return _pcall(
        body, (d3, dz, dxl, dxbc, ddtr, w5, wxbc, wdt, x, dh1, g1), name="du_norm", grid=(s // t,),
        in_specs=[_row_spec(t, 3 * D), _row_spec(t, D), _row_spec(t, D), _row_spec(t, XBC), _row_spec(t, 128),
                  _full_spec((5 * D, D), once=True), _full_spec((XBC, D), once=True),
                  _full_spec((128, D), once=True), _row_spec(t, D), _row_spec(t, D), _full_spec((1, D))],
        out_specs=[_row_spec(t, D), _full_spec((1, D))],
        out_shape=[_sds((s, D)), _sds((1, D))],
        sem=("arbitrary",), plan=plan)


def _adamw(w, g, m, v, name):
    r, c = w.shape
    t = r
    if r * c > 256 * 1024:
        t = next(cand for cand in (512, 256, 128, 64, 32, 16, 8) if r % cand == 0 and cand * c <= 512 * 1024)
    bc1 = 1.0 - ADAM_B1 ** ADAM_STEP
    bc2 = 1.0 - ADAM_B2 ** ADAM_STEP

    def body(w_ref, g_ref, m_ref, v_ref, d_ref, nm_ref, nv_ref):
        gv = g_ref[...]
        nm = ADAM_B1 * m_ref[...] + (1.0 - ADAM_B1) * gv
        nv = ADAM_B2 * v_ref[...] + (1.0 - ADAM_B2) * (gv * gv)
        nm_ref[...] = nm
        nv_ref[...] = nv
        d_ref[...] = -ADAM_LR * ((nm / bc1) / (jnp.sqrt(nv / bc2) + ADAM_EPS) + ADAM_WD * w_ref[...])

    spec = pl.BlockSpec((t, c), lambda i: (i, 0))
    return pl.pallas_call(
        body, name=name, grid=(r // t,), in_specs=[spec] * 4, out_specs=[spec] * 3,
        out_shape=[_sds((r, c))] * 3, compiler_params=_cp("parallel"),
    )(w, g, m, v)


def _half_blocks(shape, axis):
    r, c = shape
    if axis == 0:
        t = 256 if (r // 2) % 256 == 0 else 128
        nb = (r // 2) // t
        return (t, c), nb, (lambda i: (i, 0)), (lambda i: (i % nb, 0))
    nb = (c // 2) // 128
    return (r, 128), nb, (lambda i: (0, i)), (lambda i: (0, i % nb))


def _adamw_halves(w, g_mine, g_other, m, v, cidx, name, axis=0):
    r, c = w.shape
    blk, nb, whole, part = _half_blocks(w.shape, axis)
    bc1 = 1.0 - ADAM_B1 ** ADAM_STEP
    bc2 = 1.0 - ADAM_B2 ** ADAM_STEP

    def body(c_ref, w_ref, gm_ref, go_ref, m_ref, v_ref, g_ref, d_ref, nm_ref, nv_ref):
        mine = (pl.program_id(0) // nb) == c_ref[0]
        gv = jnp.where(mine, gm_ref[...], go_ref[...])
        g_ref[...] = gv
        nm = ADAM_B1 * m_ref[...] + (1.0 - ADAM_B1) * gv
        nv = ADAM_B2 * v_ref[...] + (1.0 - ADAM_B2) * (gv * gv)
        nm_ref[...] = nm
        nv_ref[...] = nv
        d_ref[...] = -ADAM_LR * ((nm / bc1) / (jnp.sqrt(nv / bc2) + ADAM_EPS) + ADAM_WD * w_ref[...])

    spec = pl.BlockSpec(blk, lambda i, c_ref: whole(i))
    half = pl.BlockSpec(blk, lambda i, c_ref: part(i))
    return pl.pallas_call(
        body, name=name,
        grid_spec=pltpu.PrefetchScalarGridSpec(num_scalar_prefetch=1, grid=(2 * nb,),
                                               in_specs=[spec, half, half, spec, spec], out_specs=[spec] * 4),
        out_shape=[_sds((r, c))] * 4, compiler_params=_cp("parallel"),
    )(cidx, w, g_mine, g_other, m, v)


def _block_diag(w):
    eye = jnp.eye(NH, dtype=w.dtype)
    return (w[:, :, None, :] * eye[:, None, :, None]).reshape(D, D)


def _diag_blocks(full):
    eye = jnp.eye(NH, dtype=full.dtype)
    return (full.reshape(NH, HP, NH, HP) * eye[:, None, :, None]).sum(axis=2)


def _pad_lanes(v, n=128):
    return jnp.pad(v, ((0, 0), (0, n - v.shape[1])))


def _local_step(x, target, p, dist=None):
    heads = jnp.arange(D, dtype=jnp.int32) // HP
    expand = (jnp.arange(128, dtype=jnp.int32)[:, None] == heads[None, :]).astype(F32)
    reduce_ = expand.T
    dskip_x = jnp.repeat(p["d_skip"], HP, axis=1)
    a_log = _pad_lanes(p["a_log"])
    dt_bias = _pad_lanes(p["dt_bias"])
    wab = jnp.concatenate([_block_diag(p["lru_wa"]), _block_diag(p["lru_wx"])], axis=1).astype(MXU)
    ba = p["lru_ba"].reshape(1, D)
    bx = p["lru_bx"].reshape(1, D)

    def hosted(key, fn):
        plan = dist.plan(key) if dist is not None else None
        if plan is None:
            return fn(plan=None)
        outs, got = fn(plan=plan)
        dist.done(key, got, p)
        return outs

    u = hosted("norm_u", functools.partial(_norm_cast, x, p["norm_mix_pre"], "norm_u"))
    w5, wxbc, wdt = p["w5"], p["wxbc"], p["wdt"]
    proj5 = hosted("proj5", functools.partial(_matmul, u, w5, name="proj5", tb=True, tm=1024, out_dtype=MXU))
    xbc_raw = _matmul(u, wxbc, name="proj_xbc", tb=True, tn=XBC)
    dt_raw = _matmul(u, wdt, name="proj_dt", tb=True)
    xbc_c, dsilu, xr, dt = _conv_fwd(xbc_raw, proj5, dt_raw, p["conv_ssm_w"], p["conv_ssm_b"], p["conv_lru_w"],
                                     p["conv_lru_b"], dt_bias)
    y, ya, states = _ssd_fwd(xbc_c, dt, proj5, a_log, dskip_x, p["ssm_norm"], expand)
    h, merged, *gates = hosted("lru_fwd", functools.partial(_lru_fwd, xr, proj5, ya, wab, ba, bx, p["lru_lambda"]))
    mix, h1, v, pre = _out_up_proj(merged, p["w_out"], x, p["norm_mix_post"], p["norm_mlp_pre"], p["w_up"])
    dout, dff, loss, dg4 = _down_loss(pre, p["w_down"], h1, target, p["norm_mlp_post"])

    dpre = _matmul(dff, p["w_down"], name="d_pre", tb=True, tm=1024, out_dtype=MXU,
                   epi=lambda r, pr: r * (2.0 * jnp.maximum(pr.astype(F32), 0.0)), epi_args=(pre,))
    g_w_down = _matmul(pre, dff, name="dw_down", ta=True, tm=1024, tn=1024, tk=TK_GRAD, a_fn=_relu2)
    dh1, dmix, dmerged, dg3, dg2 = _dv_norms(dpre, p["w_up"], h1, mix, dout, p["norm_mlp_pre"], p["norm_mix_post"],
                                             p["w_out"])
    g_w_up = _matmul(v, dpre, name="dw_up", ta=True, tm=1024, tn=1024, tk=TK_GRAD, col_blocks_major=True)
    g_w_out = _matmul(merged, dmix, name="dw_out", ta=True, tm=1024, tn=1024, tk=TK_GRAD)
    if dist is not None:
        dist.early_grads(w_down=g_w_down, w_up=g_w_up, w_out=g_w_out)
    dya, d3, dxr, dp2, dlam, dba, dbx = hosted("lru_bwd", functools.partial(
        _lru_bwd, dmerged, ya, xr, h, proj5, gates, wab.T, p["lru_lambda"]))
    g_wab = _matmul(xr, dp2, name="dw_lru", ta=True, tm=1024, tn=1024, tk=TK_GRAD)
    g_wa, g_wx = _diag_blocks(g_wab[:, :D]), _diag_blocks(g_wab[:, D:])
    dz, dxbc_c, ddt, dng, ddsk, dalog = hosted("ssd_bwd", functools.partial(
        _ssd_bwd, dya, y, proj5, xbc_c, dt, states, a_log, dskip_x, p["ssm_norm"], expand, reduce_))
    (dxbc, dxl, ddtr, dcws, dcbs, dcwl, dcbl, ddtb) = hosted("conv_bwd", functools.partial(
        _conv_bwd, dxbc_c, dsilu, dxr, ddt, xbc_raw, proj5, dt_raw, p["conv_ssm_w"], p["conv_lru_w"], dt_bias))
    gw3 = _matmul(d3, u, name="dw_in_lru", ta=True, tm=1024, tn=1024, tk=TK_GRAD)
    gwz = _matmul(dz, u, name="dw_in_z", ta=True, tm=1024, tn=1024, tk=TK_GRAD)
    gwxl = _matmul(dxl, u, name="dw_in_xl", ta=True, tm=1024, tn=1024, tk=TK_GRAD)
    gwxbc = _matmul(dxbc, u, name="dw_in_xbc", ta=True, tm=XBC, tn=1024, tk=TK_GRAD)
    gwdt = _matmul(ddtr, u, name="dw_in_dt", ta=True, tm=128, tn=1024, tk=TK_GRAD)
    g_w_in_t = jnp.concatenate([gwz, gwxbc, gwdt[:NH], gw3[:D], gwxl, gw3[D:2 * D], gw3[2 * D:]], axis=0)
    grads = {
        "w_in_t": g_w_in_t, "conv_ssm_w": dcws, "conv_ssm_b": dcbs, "dt_bias": ddtb[:, :NH],
        "a_log": dalog[:, :NH], "d_skip": ddsk.reshape(NH, HP).sum(axis=1)[None, :], "ssm_norm": dng,
        "conv_lru_w": dcwl, "conv_lru_b": dcbl, "lru_wa": g_wa, "lru_ba": dba.reshape(NH, HP), "lru_wx": g_wx,
        "lru_bx": dbx.reshape(NH, HP), "lru_lambda": dlam, "w_out": g_w_out, "norm_mix_post": dg2,
        "norm_mlp_pre": dg3, "w_up": g_w_up, "w_down": g_w_down, "norm_mlp_post": dg4,
    }
    if dist is not None:
        dist.late_grads(grads, loss[0, 0])
    grad_x, grads["norm_mix_pre"] = hosted("du_norm", functools.partial(
        _du_norm, d3, dz, dxl, dxbc, ddtr, w5, wxbc, wdt, x, dh1, p["norm_mix_pre"]))
    return loss[0, 0], grad_x, grads


def _split_w_in_t(w_in_t):
    z, xbc, dtc, g, xl, ga, gb = jnp.split(w_in_t, [D, D + XBC, D + XBC + NH, 2 * D + XBC + NH,
                                                    3 * D + XBC + NH, 4 * D + XBC + NH], axis=0)
    return jnp.concatenate([g, ga, gb, z, xl], axis=0), xbc, jnp.pad(dtc, ((0, 128 - NH), (0, 0)))


COMM = BF16


def _place():
    x, y, c = lax.axis_index("x"), lax.axis_index("y"), lax.axis_index("c")
    chips = [(1 - x, y), (x, 1 - y), (1 - x, 1 - y)]
    return x, y, c, chips


def _remote(src, dst, send_sem, recv_sem, to):
    return pltpu.make_async_remote_copy(src_ref=src, dst_ref=dst, send_sem=send_sem, recv_sem=recv_sem, device_id=to,
                                        device_id_type=MESH)


def _gather_plan(big, small=(), axes=None):
    nb = len(big)
    arrs = list(big) + list(small)
    na = len(arrs)
    axes = list(axes or [0] * nb)

    def half(ref, a, k, which):
        h = arrs[a].shape[axes[a]] // 2
        cut = (pl.ds(which * h, h),) if axes[a] == 0 else (slice(None), pl.ds(which * h, h))
        return ref.at[cut] if k is None else ref.at[(k,) + cut]

    def direct(ins, outs, send, recv):
        x, y, c, chips = _place()
        k = 2 * x + y
        cps = []
        for a in range(na):
            src, dst = (half(ins[a], a, None, c), half(outs[a], a, k, c)) if a < nb else (ins[a], outs[a].at[k])
            cps += [_remote(src, dst, send.at[a, j], recv.at[a, j], (cx, cy, c)) for j, (cx, cy) in enumerate(chips)]
        return cps

    def passed(outs, send, recv):
        x, y, c, chips = _place()
        cps = []
        for j, (cx, cy) in enumerate(chips):
            for a in range(nb):
                got = half(outs[a], a, 2 * cx + cy, c)
                cps.append(_remote(got, got, send.at[a, 3 + j], recv.at[a, 3 + j], (x, y, 1 - c)))
        return cps

    def start(ins, outs, sems):
        for cp in direct(ins, outs, *sems[0]):
            cp.start()

    def mid(ins, outs, sems):
        send, recv = sems[0]
        _, _, c, chips = _place()
        fwd = passed(outs, send, recv)
        for j, (cx, cy) in enumerate(chips):
            kj = 2 * cx + cy
            for a in range(na):
                got = half(outs[a], a, kj, c) if a < nb else outs[a].at[kj]
                _remote(got, got, send.at[a, j], recv.at[a, j], (cx, cy, c)).wait_recv()
                if a < nb:
                    fwd[j * nb + a].start()

    def finish(ins, outs, sems):
        send, recv = sems[0]
        x, y, c, chips = _place()
        for j, (cx, cy) in enumerate(chips):
            for a in range(nb):
                got = half(outs[a], a, 2 * cx + cy, 1 - c)
                _remote(got, got, send.at[a, 3 + j], recv.at[a, 3 + j], (x, y, 1 - c)).wait_recv()
        for cp in direct(ins, outs, send, recv) + passed(outs, send, recv):
            cp.wait_send()

    return _Plan(arrs, [_sds((NCHIP,) + a.shape, a.dtype) for a in arrs], [(na, 6)], start, finish, mid)


def _own_shards(gathered, shards):
    kchip = 2 * lax.axis_index("x") + lax.axis_index("y")
    return [lax.dynamic_update_index_in_dim(o, a, kchip, 0) for o, a in zip(gathered, shards)]


def _swap_plan(ins, outs, sems, copies):
    def start(i, o, s):
        for cp in copies(i, o, *s[0]):
            cp.start()

    def finish(i, o, s):
        for cp in copies(i, o, *s[0]):
            cp.wait()

    return _Plan(ins, outs, [sems], start, finish)


def _half_shape(shape, axis):
    return tuple(d // 2 if i == axis else d for i, d in enumerate(shape))


def _pair_exchange_plan(gs, axis=1):
    def copies(ins, outs, send, recv):
        x, y, c, _ = _place()
        cps = []
        for a in range(len(gs)):
            h = ins[a].shape[axis] // 2
            theirs = pl.ds((1 - c) * h, h)
            src = ins[a].at[:, theirs] if axis == 1 else ins[a].at[:, :, theirs]
            cps.append(_remote(src, outs[a], send.at[a], recv.at[a], (x, y, 1 - c)))
        return cps

    return _swap_plan(gs, [_sds(_half_shape(g.shape, axis), g.dtype) for g in gs], (len(gs),), copies)


def _pair_add(g, got, cidx, name, axis=1):
    half = _half_shape(g.shape, axis)
    blk, nt, _, part = _half_blocks(g.shape[1:], axis - 1)

    def body(c_ref, g_ref, o_ref, p_ref, pc_ref):
        sm = g_ref[...] + o_ref[...]
        p_ref[...] = sm
        pc_ref[...] = sm.astype(pc_ref.dtype)

    def mine(k, i, c_ref):
        j = c_ref[0] * nt + i
        return (k, j, 0) if axis == 1 else (k, 0, j)

    spec = pl.BlockSpec((1,) + blk, lambda k, i, c_ref: (k,) + part(i))
    return pl.pallas_call(
        body, name=name,
        grid_spec=pltpu.PrefetchScalarGridSpec(
            num_scalar_prefetch=1, grid=(g.shape[0], nt),
            in_specs=[pl.BlockSpec((1,) + blk, mine), spec], out_specs=[spec, spec]),
        out_shape=[_sds(half), _sds(half, COMM)],
        compiler_params=_cp("parallel", "parallel"),
    )(cidx, g, got)


def _chip_exchange_plan(ps):
    def copies(ins, outs, send, recv):
        _, _, c, chips = _place()
        return [_remote(ins[a].at[2 * cx + cy], outs[a].at[j], send.at[a, j], recv.at[a, j], (cx, cy, c))
                for a in range(len(ps)) for j, (cx, cy) in enumerate(chips)]

    return _swap_plan(ps, [_sds((NCHIP - 1,) + p.shape[1:], p.dtype) for p in ps], (len(ps), 3), copies)


def _shard_sum(p, got, kidx, name, axis=1):
    full = tuple(2 * d if i == axis - 1 else d for i, d in enumerate(p.shape[1:]))
    blk, nt, _, part = _half_blocks(full, axis - 1)

    def body(k_ref, p_ref, g_ref, o_ref):
        sm = p_ref[0]
        for j in range(NCHIP - 1):
            sm = sm + g_ref[j].astype(F32)
        o_ref[...] = sm

    return pl.pallas_call(
        body, name=name,
        grid_spec=pltpu.PrefetchScalarGridSpec(
            num_scalar_prefetch=1, grid=(nt,),
            in_specs=[pl.BlockSpec((1,) + blk, lambda i, k_ref: (k_ref[0],) + part(i)),
                      pl.BlockSpec((NCHIP - 1,) + blk, lambda i, k_ref: (0,) + part(i))],
            out_specs=pl.BlockSpec(blk, lambda i, k_ref: part(i))),
        out_shape=_sds(p.shape[1:]),
        compiler_params=_cp("parallel"),
    )(kidx, p, got)


def _pair_swap_plan(rs):
    def copies(ins, outs, send, recv):
        x, y, c, _ = _place()
        return [_remote(ins[a], outs[a], send.at[a], recv.at[a], (x, y, 1 - c)) for a in range(len(rs))]

    return _swap_plan(rs, [_sds(r.shape, r.dtype) for r in rs], (len(rs),), copies)


def _allgather8_plan(v):
    def pieces(ins, outs, send, recv):
        x, y, c, chips = _place()
        me, sibling = (x, y, c), (x, y, 1 - c)

        def copy(k, block, to, src=None):
            px, py, pc = block
            slot = outs[0].at[4 * px + 2 * py + pc]
            return _remote(slot if src is None else src, slot, send.at[k], recv.at[k], to)

        first = [copy(0, me, sibling, src=ins[0])] + [copy(1 + j, me, (*chip, c), src=ins[0])
                                                      for j, chip in enumerate(chips)]
        passed = [copy(4 + j, (*chip, c), sibling) for j, chip in enumerate(chips)]
        arrivals = [copy(1 + j, (*chip, c), me) for j, chip in enumerate(chips)]
        late = [copy(0, sibling, me)] + [copy(4 + j, (*chip, 1 - c), me) for j, chip in enumerate(chips)]
        return first, passed, arrivals, late

    def start(ins, outs, sems):
        for cp in pieces(ins, outs, *sems[0])[0]:
            cp.start()

    def mid(ins, outs, sems):
        _, passed, arrivals, _ = pieces(ins, outs, *sems[0])
        for got, fwd in zip(arrivals, passed):
            got.wait_recv()
            fwd.start()

    def finish(ins, outs, sems):
        first, passed, _, late = pieces(ins, outs, *sems[0])
        for got in late:
            got.wait_recv()
        for cp in first + passed:
            cp.wait_send()

    return _Plan([v], [_sds((8,) + v.shape, v.dtype)], [(7,)], start, finish, mid)


def _own_block(gathered, v):
    me = 4 * lax.axis_index("x") + 2 * lax.axis_index("y") + lax.axis_index("c")
    return lax.dynamic_update_index_in_dim(gathered, v, me, 0)


def _sum_devices(allv, name):
    _, r, _ = allv.shape

    def body(a_ref, o_ref):
        sm = a_ref[0]
        for d in range(1, 8):
            sm = sm + a_ref[d]
        o_ref[...] = sm

    return pl.pallas_call(
        body, name=name, grid=(1,), in_specs=[_full_spec((8, r, 128))], out_specs=_full_spec((r, 128)),
        out_shape=_sds((r, 128)), compiler_params=_cp("arbitrary"),
    )(allv)


def _pack(arrs):
    flat = jnp.concatenate([a.reshape(-1) for a in arrs])
    return jnp.pad(flat, (0, (-flat.shape[0]) % 1024)).reshape(-1, 128)


def _unpack(packed, shapes):
    flat, outs, off = packed.reshape(-1), [], 0
    for shp in shapes:
        n = math.prod(shp)
        outs.append(flat[off:off + n].reshape(shp))
        off += n
    return outs


BIG = ("w_in", "w_out", "w_up", "w_down")
CONV = ("conv_ssm_w", "conv_lru_w")
WEIGHTS = ("norm_mix_pre", "w_in", "conv_ssm_w", "conv_ssm_b", "dt_bias", "a_log", "d_skip", "ssm_norm", "conv_lru_w",
           "conv_lru_b", "lru_wa", "lru_ba", "lru_wx", "lru_bx", "lru_lambda", "w_out", "norm_mix_post",
           "norm_mlp_pre", "w_up", "w_down", "norm_mlp_post")
SMALL = tuple(n for n in WEIGHTS if n not in BIG and n not in CONV)
EARLY = ("w_down", "w_up", "w_out")


def _cat_cols(g):
    return jnp.concatenate([g[k] for k in range(NCHIP)], axis=1)


class _Dist:
    def __init__(self, shards, first, cidx, kidx):
        self.shards, self.first, self.cidx, self.kidx = shards, first, cidx, kidx
        self.halves = {}

    def early_grads(self, w_down, w_up, w_out):
        self.shard_major = [w_down.reshape(NCHIP, D, D), w_up, w_out.reshape(NCHIP, D // NCHIP, D)]

    def late_grads(self, grads, loss):
        g_in = grads["w_in_t"][None]
        got, = _run_plan(_pair_exchange_plan([g_in], axis=2), "grad_pair_exchange_w_in")
        p_all, pc_all = _pair_add(g_in, got, self.cidx, "grad_pair_add_w_in", axis=2)
        self.p_in = lax.dynamic_slice_in_dim(p_all[0], self.kidx[0] * W_IN_SHARD, W_IN_SHARD, axis=0)[None]
        self.pc_in = pc_all.reshape(NCHIP, W_IN_SHARD, D // 2)
        self.small_names = [n for n in SMALL + CONV if n != "norm_mix_pre"]
        self.small_shapes = [grads[n].shape for n in self.small_names] + [(1,)]
        self.packed_small = _pack([grads[n] for n in self.small_names] + [loss.reshape(1)])

    def plan(self, key):
        if key == "norm_u":
            return _gather_plan(self.first[:1], self.first[1:], axes=[1])
        if key == "proj5":
            return _gather_plan([self.shards["w_out"], self.shards["w_up"]])
        if key == "lru_fwd":
            return _gather_plan([self.shards["w_down"]])
        if key == "lru_bwd":
            return _pair_exchange_plan(self.shard_major)
        if key == "ssd_bwd":
            return _chip_exchange_plan([pc for _, pc in self.pair])
        if key == "conv_bwd":
            return _pair_swap_plan(self.mine)
        if key == "du_norm":
            return _merge_plans(_allgather8_plan(self.packed_small), _chip_exchange_plan([self.pc_in]))
        return None

    def done(self, key, got, p):
        if key == "norm_u":
            g_in, g_cs, g_cl = _own_shards(got, self.first)
            w5, wxbc, wdt = _split_w_in_t(g_in.reshape(W_IN_COLS, D))
            p.update(w5=w5, wxbc=wxbc, wdt=wdt, conv_ssm_w=_cat_cols(g_cs), conv_lru_w=_cat_cols(g_cl))
        elif key == "proj5":
            g_out, g_up = _own_shards(got, [self.shards["w_out"], self.shards["w_up"]])
            p.update(w_out=g_out.reshape(D, D), w_up=_cat_cols(g_up))
        elif key == "lru_fwd":
            g_down, = _own_shards(got, [self.shards["w_down"]])
            p.update(w_down=g_down.reshape(DFF, D))
        elif key == "lru_bwd":
            self.pair = [_pair_add(gs, o, self.cidx, f"grad_pair_add_{n}")
                         for gs, o, n in zip(self.shard_major, got, EARLY)]
        elif key == "ssd_bwd":
            self.mine = [_shard_sum(pf, o, self.kidx, f"grad_shard_sum_{n}")
                         for (pf, _), o, n in zip(self.pair, got, EARLY)]
        elif key == "conv_bwd":
            self.halves = {n: (mine, other) for n, mine, other in zip(EARLY, self.mine, got)}
        elif key == "du_norm":
            self.all_small, self.from_chips_in = got


def kernel(x, norm_mix_pre, w_in, conv_ssm_w, conv_ssm_b, dt_bias, a_log, d_skip, ssm_norm, conv_lru_w, conv_lru_b, lru_wa, lru_ba, lru_wx, lru_bx, lru_lambda, w_out, norm_mix_post, norm_mlp_pre, w_up, w_down, norm_mlp_post, loss_target, m_norm_mix_pre, m_w_in, m_conv_ssm_w, m_conv_ssm_b, m_dt_bias, m_a_log, m_d_skip, m_ssm_norm, m_conv_lru_w, m_conv_lru_b, m_lru_wa, m_lru_ba, m_lru_wx, m_lru_bx, m_lru_lambda, m_w_out, m_norm_mix_post, m_norm_mlp_pre, m_w_up, m_w_down, m_norm_mlp_post, v_norm_mix_pre, v_w_in, v_conv_ssm_w, v_conv_ssm_b, v_dt_bias, v_a_log, v_d_skip, v_ssm_norm, v_conv_lru_w, v_conv_lru_b, v_lru_wa, v_lru_ba, v_lru_wx, v_lru_bx, v_lru_lambda, v_w_out, v_norm_mix_post, v_norm_mlp_pre, v_w_up, v_w_down, v_norm_mlp_post):
    args = locals()
    w = {n: args[n][0] for n in WEIGHTS}
    m = {n: args["m_" + n][0] for n in WEIGHTS}
    v = {n: args["v_" + n][0] for n in WEIGHTS}
    cidx = lax.axis_index("c").astype(jnp.int32).reshape(1)
    kchip = 2 * lax.axis_index("x") + lax.axis_index("y")
    to_t = lambda a: jnp.transpose(a, (2, 0, 1)).reshape(W_IN_SHARD, D)
    from_t = lambda a: jnp.transpose(a.reshape(W_IN_SHARD, 1, D), (1, 2, 0))
    shards = {n: (to_t(w_in) if n == "w_in" else w[n]).astype(MXU) for n in BIG}
    dist = _Dist(shards, [shards["w_in"], w["conv_ssm_w"], w["conv_lru_w"]], cidx, kchip.astype(jnp.int32).reshape(1))
    p = {n: (w[n].reshape(1, -1) if w[n].ndim == 1 else w[n]) for n in SMALL}

    _, grad_x, g = _local_step(x[0], loss_target[0], p, dist)

    half_in = _shard_sum(dist.p_in, dist.from_chips_in, jnp.zeros((1,), jnp.int32), "grad_shard_sum_w_in", axis=2)
    packed_g1 = _pack([g["norm_mix_pre"]])
    all_g1, other_in = _run_plan(_merge_plans(_allgather8_plan(packed_g1), _pair_swap_plan([half_in])),
                                 "grad_pair_swap_w_in")
    halves = dist.halves

    reduced = {}
    *summed, loss = _unpack(_sum_devices(_own_block(dist.all_small, dist.packed_small), "small_sum"),
                            dist.small_shapes)
    loss = loss.reshape(())
    g1, = _unpack(_sum_devices(_own_block(all_g1, packed_g1), "small_sum_norm_mix_pre"), [g["norm_mix_pre"].shape])
    for n, s in zip(dist.small_names + ["norm_mix_pre"], summed + [g1]):
        if n in CONV:
            width = w[n].shape[1]
            reduced[n] = lax.dynamic_slice_in_dim(s, kchip * width, width, axis=1)
        else:
            reduced[n] = s.reshape(w[n].shape)

    delta, new_m, new_v = {}, {}, {}
    for n in EARLY:
        mine, other = halves[n]
        reduced[n], delta[n], new_m[n], new_v[n] = _adamw_halves(w[n], mine, other, m[n], v[n], cidx, f"adamw_{n}")
    outs_t = _adamw_halves(to_t(w_in), half_in, other_in, to_t(m_w_in), to_t(v_w_in), cidx, "adamw_w_in", axis=1)
    for d, o in zip((reduced, delta, new_m, new_v), outs_t):
        d["w_in"] = from_t(o)[0]
    for n in CONV:
        delta[n], new_m[n], new_v[n] = _adamw(w[n], reduced[n], m[n], v[n], f"adamw_{n}")
    shapes = [w[n].shape for n in SMALL]
    packed = [_pack([d[n] for n in SMALL]) for d in (w, reduced, m, v)]
    for d, out in zip((delta, new_m, new_v), _adamw(*packed, "adamw_small")):
        d.update(zip(SMALL, _unpack(out, shapes)))

    lead = lambda d: [d[n][None] for n in WEIGHTS]
    return (loss, grad_x[None], *lead(reduced), *lead(delta), *lead(new_m), *lead(new_v))
```

```python
import functools
import math

import jax
import jax.numpy as jnp
from jax import lax
from jax.experimental import pallas as pl
from jax.experimental.pallas import tpu as pltpu

F32 = jnp.float32
BF16 = jnp.bfloat16
MXU = BF16

D = 1024
DFF = 4096
NH = 16
HP = 64
NG = 2
NS = 128
CH = 128
XBC = D + 2 * NG * NS
GW = D // NG
LRU_C = 8.0
EPS = 1e-6
NCHIP = 4
W_IN_COLS = 6672
W_IN_SHARD = W_IN_COLS // NCHIP

ADAM_LR = 0.001
ADAM_B1 = 0.9
ADAM_B2 = 0.999
ADAM_EPS = 1e-08
ADAM_WD = 0.01
ADAM_STEP = 10

VMEM_LIMIT = 56 * 1024 * 1024
TK_GRAD = 2048
MID_AT = 0.7
ROWS_FUSED = 512
ROWS_CONV = 512
COL_G, COL_GA, COL_GB, COL_Z, COL_XL = range(5)
MESH = pl.DeviceIdType.MESH


def _cp(*sem):
    return pltpu.CompilerParams(dimension_semantics=sem, vmem_limit_bytes=VMEM_LIMIT)


def _dot(a, b, ca=1, cb=0, prec=None):
    return lax.dot_general(a, b, (((ca,), (cb,)), ((), ())), precision=prec, preferred_element_type=F32)


def _mdot(a, b, ca=1, cb=0):
    return _dot(a.astype(MXU), b.astype(MXU), ca, cb)


def _bf16_parts(v, n):
    parts = []
    for i in range(n):
        p = v.astype(BF16)
        parts.append(p)
        if i < n - 1:
            v = v - p.astype(F32)
    return parts


def _xdot(a, b, passes, split_b=False):
    if split_b:
        a16 = a.astype(BF16)
        terms = [_dot(a16, p) for p in _bf16_parts(b, passes)]
    else:
        b16 = b.astype(BF16)
        terms = [_dot(p, b16) for p in _bf16_parts(a, passes)]
    return functools.reduce(lambda u, v: u + v, terms)


def _sig(x):
    return 0.5 * jnp.tanh(0.5 * x) + 0.5


def _silu(x):
    return x * _sig(x)


def _dsilu(x):
    s = _sig(x)
    return s * (1.0 + x * (1.0 - s))


def _softplus(x):
    e = jnp.exp(-jnp.abs(x))
    return jnp.maximum(x, 0.0) + jnp.where(e < 1e-4, e * (1.0 - 0.5 * e), jnp.log(1.0 + e))


_GELU_C = math.sqrt(2.0 / math.pi)


def _gelu(x):
    t = jnp.tanh(_GELU_C * (x + 0.044715 * x * x * x))
    return 0.5 * x * (1.0 + t)


def _gelu_and_grad(x):
    x2 = x * x
    t = jnp.tanh(_GELU_C * (x + 0.044715 * x * x2))
    half = 0.5 * (1.0 + t)
    return x * half, half + 0.5 * x * (1.0 - t * t) * _GELU_C * (1.0 + 3.0 * 0.044715 * x2)


def _one_minus_sq(a, la):
    x = 2.0 * la
    series = -x * (1.0 + x * (0.5 + x * (1.0 / 6.0)))
    return jnp.where(x > -0.01, series, 1.0 - a * a)


def _rms(x):
    return lax.rsqrt(jnp.mean(x * x, axis=-1, keepdims=True) + EPS)


def _rms_bwd(x, r, g, dy):
    xn = x * r
    dxh = dy * g
    m = jnp.mean(dxh * xn, axis=-1, keepdims=True)
    return r * (dxh - xn * m), jnp.sum(dy * xn, axis=0, keepdims=True)


def _row_spec(t, c, col=0):
    return pl.BlockSpec((t, c), lambda i: (i, col))


def _rev_spec(t, c, n, col=0):
    return pl.BlockSpec((t, c), lambda i: (n - 1 - i, col))


def _full_spec(shape, once=False):
    nd = len(shape)
    if once:
        return pl.BlockSpec(shape, lambda *_: (0,) * nd, pipeline_mode=pl.Buffered(1))
    return pl.BlockSpec(shape, lambda *_: (0,) * nd)


def _sds(shape, dtype=F32):
    return jax.ShapeDtypeStruct(shape, dtype)


ANY = pl.BlockSpec(memory_space=pl.ANY)


class _Plan:
    def __init__(self, ins, outs, sems, start, finish, mid=None):
        self.ins, self.outs, self.sems = list(ins), list(outs), list(sems)
        self.start, self.finish, self.mid = start, finish, mid or (lambda i, o, s: None)


def _merge_plans(*plans):
    def each(fn_name, ins, outs, sems):
        i = o = s = 0
        for p in plans:
            getattr(p, fn_name)(ins[i:i + len(p.ins)], outs[o:o + len(p.outs)], sems[s:s + len(p.sems)])
            i, o, s = i + len(p.ins), o + len(p.outs), s + len(p.sems)

    return _Plan([a for p in plans for a in p.ins], [a for p in plans for a in p.outs],
                 [a for p in plans for a in p.sems], functools.partial(each, "start"),
                 functools.partial(each, "finish"), functools.partial(each, "mid"))


def _pcall(body, args, *, name, grid, in_specs, out_specs, out_shape, sem, scratch_shapes=(), plan=None):
    single = not isinstance(out_shape, (list, tuple))
    out_specs = [out_specs] if single else list(out_specs)
    out_shape = [out_shape] if single else list(out_shape)
    if plan is None:
        outs = pl.pallas_call(body, name=name, grid=grid, in_specs=list(in_specs), out_specs=out_specs,
                              out_shape=out_shape, scratch_shapes=list(scratch_shapes),
                              compiler_params=_cp(*sem))(*args)
        return outs[0] if single else outs
    n_in, n_out, n_sc, ni, no = len(in_specs), len(out_shape), len(scratch_shapes), len(plan.ins), len(plan.outs)

    def hosted(*refs):
        b0 = n_in + ni
        b1 = b0 + n_out + no
        sem_refs = refs[b1 + n_sc:]
        sems = [(sem_refs[2 * q], sem_refs[2 * q + 1]) for q in range(len(plan.sems))]
        step = functools.reduce(lambda lin, ig: lin * ig[1] + ig[0],
                                [(pl.program_id(d), g) for d, g in enumerate(grid)], 0)
        total = math.prod(grid)

        @pl.when(step == 0)
        def _():
            plan.start(refs[n_in:b0], refs[b0 + n_out:b1], sems)

        body(*refs[:n_in], *refs[b0:b0 + n_out], *refs[b1:b1 + n_sc])

        @pl.when(step == min(int(MID_AT * total), total - 1))
        def _():
            plan.mid(refs[n_in:b0], refs[b0 + n_out:b1], sems)

        @pl.when(step == total - 1)
        def _():
            plan.finish(refs[n_in:b0], refs[b0 + n_out:b1], sems)

    dma = [pltpu.SemaphoreType.DMA(shape) for shape in plan.sems for _ in range(2)]
    outs = pl.pallas_call(hosted, name=name, grid=grid, in_specs=list(in_specs) + [ANY] * ni,
                          out_specs=out_specs + [ANY] * no, out_shape=out_shape + plan.outs,
                          scratch_shapes=list(scratch_shapes) + dma,
                          compiler_params=_cp(*("arbitrary",) * len(grid)))(*args, *plan.ins)
    return (outs[0] if single else outs[:n_out]), outs[n_out:]


def _run_plan(plan, name):
    ni, no = len(plan.ins), len(plan.outs)

    def body(*refs):
        sem_refs = refs[ni + no:]
        sems = [(sem_refs[2 * q], sem_refs[2 * q + 1]) for q in range(len(plan.sems))]
        plan.start(refs[:ni], refs[ni:ni + no], sems)
        plan.mid(refs[:ni], refs[ni:ni + no], sems)
        plan.finish(refs[:ni], refs[ni:ni + no], sems)

    return pl.pallas_call(
        body, name=name, in_specs=[ANY] * ni, out_specs=[ANY] * no, out_shape=plan.outs,
        scratch_shapes=[pltpu.SemaphoreType.DMA(shape) for shape in plan.sems for _ in range(2)],
    )(*plan.ins)


def _matmul(a, b, *, name, ta=False, tb=False, tm=512, tn=1024, tk=1024, out_dtype=F32, a_fn=None, epi=None,
            epi_args=(), plan=None, col_blocks_major=False):
    m, k = (a.shape[1], a.shape[0]) if ta else a.shape
    n = b.shape[0] if tb else b.shape[1]
    tm, tn, tk = min(tm, m), min(tn, n), min(tk, k)
    nk = k // tk
    a_spec = pl.BlockSpec((tk, tm), lambda i, j, kk: (kk, i)) if ta else pl.BlockSpec((tm, tk), lambda i, j, kk: (i, kk))
    b_spec = pl.BlockSpec((tn, tk), lambda i, j, kk: (j, kk)) if tb else pl.BlockSpec((tk, tn), lambda i, j, kk: (kk, j))
    e_specs = [pl.BlockSpec((tm, tn), lambda i, j, kk: (i, j)) for _ in epi_args]
    ne = len(epi_args)

    def body(a_ref, b_ref, *rest):
        e_refs, o_ref = rest[:ne], rest[ne]
        av = a_ref[...]
        if a_fn is not None:
            av = a_fn(av)
        part = _mdot(av, b_ref[...], 0 if ta else 1, 1 if tb else 0)

        def finish(r):
            if epi is not None:
                r = epi(r, *[e[...] for e in e_refs])
            o_ref[...] = r.astype(o_ref.dtype)

        if nk == 1:
            finish(part)
            return
        acc_ref = rest[ne + 1]
        kk = pl.program_id(2)

        @pl.when(kk == 0)
        def _():
            acc_ref[...] = part

        @pl.when(jnp.logical_and(kk > 0, kk < nk - 1))
        def _():
            acc_ref[...] += part

        @pl.when(kk == nk - 1)
        def _():
            finish(acc_ref[...] + part)

    if col_blocks_major:
        out_spec = pl.BlockSpec((None, tm, tn), lambda i, j, kk: (j, i, 0))
        out_shape = _sds((n // tn, m, tn), out_dtype)
    else:
        out_spec = pl.BlockSpec((tm, tn), lambda i, j, kk: (i, j))
        out_shape = _sds((m, n), out_dtype)
    return _pcall(
        body, (a, b, *epi_args), name=name, grid=(m // tm, n // tn, nk),
        in_specs=[a_spec, b_spec] + e_specs, out_specs=out_spec, out_shape=out_shape,
        scratch_shapes=[pltpu.VMEM((tm, tn), F32)] if nk > 1 else [],
        sem=("parallel", "parallel", "arbitrary"), plan=plan)


def _relu2(p):
    p = jnp.maximum(p, jnp.zeros((), p.dtype))
    return p * p


def _norm_cast(x, g, name, plan=None):
    s = x.shape[0]
    t = min(512, s)

    def body(x_ref, g_ref, o_ref):
        xv = x_ref[...]
        o_ref[...] = (xv * _rms(xv) * g_ref[...]).astype(o_ref.dtype)

    return _pcall(body, (x, g), name=name, grid=(s // t,), in_specs=[_row_spec(t, D), _full_spec((1, D))],
                  out_specs=_row_spec(t, D), out_shape=_sds((s, D), MXU), sem=("parallel",), plan=plan)


def _conv_fwd(xbc_raw, proj5, dt_raw, cw_s, cb_s, cw_l, cb_l, dt_bias):
    s = xbc_raw.shape[0]
    t = min(ROWS_CONV, s)

    def body(xs_ref, xl_ref, dtr_ref, cws_ref, cbs_ref, cwl_ref, cbl_ref, dtb_ref, xc_ref, dsl_ref, xr_ref, dt_ref,
             bs_ref, bl_ref):
        @pl.when(pl.program_id(0) == 0)
        def _():
            bs_ref[0:8, :] = jnp.zeros((8, XBC), F32)
            bl_ref[0:8, :] = jnp.zeros((8, D), F32)

        bs_ref[8:t + 8, :] = xs_ref[...]
        bl_ref[8:t + 8, :] = xl_ref[...].astype(F32)

        def conv(buf, w_ref, b_ref):
            acc = b_ref[...] + w_ref[3:4, :] * buf[8:t + 8, :]
            for k in (1, 2, 3):
                acc = acc + w_ref[3 - k:4 - k, :] * buf[8 - k:t + 8 - k, :]
            return acc

        pre = conv(bs_ref, cws_ref, cbs_ref)
        sg = _sig(pre)
        xc_ref[...] = pre * sg
        dsl_ref[...] = (sg * (1.0 + pre * (1.0 - sg))).astype(dsl_ref.dtype)
        xr_ref[...] = conv(bl_ref, cwl_ref, cbl_ref)
        dt_ref[...] = _softplus(dtr_ref[...] + dtb_ref[...])
        bs_ref[0:8, :] = bs_ref[t:t + 8, :]
        bl_ref[0:8, :] = bl_ref[t:t + 8, :]

    return pl.pallas_call(
        body, name="conv_fwd", grid=(s // t,),
        in_specs=[_row_spec(t, XBC), _row_spec(t, D, COL_XL), _row_spec(t, 128), _full_spec((4, XBC)),
                  _full_spec((1, XBC)), _full_spec((4, D)), _full_spec((1, D)), _full_spec((1, 128))],
        out_specs=[_row_spec(t, XBC), _row_spec(t, XBC), _row_spec(t, D), _row_spec(t, 128)],
        out_shape=[_sds((s, XBC)), _sds((s, XBC), BF16), _sds((s, D)), _sds((s, 128))],
        scratch_shapes=[pltpu.VMEM((t + 8, XBC), F32), pltpu.VMEM((t + 8, D), F32)],
        compiler_params=_cp("arbitrary"),
    )(xbc_raw, proj5, dt_raw, cw_s, cb_s, cw_l, cb_l, dt_bias)


def _ssd_chunk_setup(dt_ref, alog_ref, e_ref, at_ref, dtt_ref):
    lane = lax.broadcasted_iota(jnp.int32, (CH, 128), 1)
    row = lax.broadcasted_iota(jnp.int32, (CH, 128), 0)
    lane1 = lax.broadcasted_iota(jnp.int32, (1, 128), 1)
    a = jnp.where(lane1 < NH, -jnp.exp(alog_ref[...]), 0.0)
    dtv = dt_ref[...]
    adt = dtv * a
    tril = row >= lane
    acum = _xdot(tril.astype(F32), adt, 3, split_b=True)
    alast = jnp.sum(adt, axis=0, keepdims=True)
    at_ref[...] = acum.T
    dtt_ref[...] = dtv.T
    e = e_ref[...]
    ea_x = _xdot(jnp.exp(acum), e, 2)
    ws = jnp.exp(alast - acum) * dtv
    ws_x = _xdot(ws, e, 2)
    eal = jnp.exp(alast)
    eal_x = jnp.max(_xdot(jnp.broadcast_to(eal, (8, 128)), e, 3), axis=0, keepdims=True)
    return dict(lane=lane, row=row, tril=tril, a=a, dtv=dtv, acum=acum, alast=alast, ea_x=ea_x, ws=ws, ws_x=ws_x,
                eal=eal, eal_x=eal_x)


def _head_decay(cs, at_ref, dtt_ref, h):
    col = jnp.sum(jnp.where(cs["lane"] == h, cs["acum"], 0.0), axis=1, keepdims=True)
    ld = jnp.where(cs["tril"], jnp.exp(jnp.minimum(col - at_ref[h:h + 1, :], 0.0)), 0.0)
    return ld, dtt_ref[h:h + 1, :]


def _ssd_fwd(xbc_c, dt, proj5, a_log, dskip_x, ssm_norm, expand):
    s = xbc_c.shape[0]
    nc = s // CH

    def body(xc_ref, dt_ref, z_ref, alog_ref, dsk_ref, ng_ref, e_ref, y_ref, ya_ref, st_ref, h_ref, at_ref, dtt_ref,
             yd_ref):
        @pl.when(pl.program_id(0) == 0)
        def _():
            h_ref[...] = jnp.zeros_like(h_ref)

        cs = _ssd_chunk_setup(dt_ref, alog_ref, e_ref, at_ref, dtt_ref)
        lane = cs["lane"]
        for g in range(NG):
            gs = slice(GW * g, GW * (g + 1))
            bg = xc_ref[:, D + NS * g:D + NS * (g + 1)]
            cg = xc_ref[:, D + NG * NS + NS * g:D + NG * NS + NS * (g + 1)]
            cb = _mdot(cg, bg, 1, 1)
            for j in range(4 * g, 4 * g + 4):
                ps = slice(128 * j, 128 * (j + 1))
                xp = xc_ref[:, ps]
                acc = jnp.zeros((CH, 128), F32)
                for hf in range(2):
                    ld, rowdt = _head_decay(cs, at_ref, dtt_ref, 2 * j + hf)
                    hm = (lane >= HP) if hf else (lane < HP)
                    acc = acc + _mdot(cb * ld * rowdt, jnp.where(hm, xp, 0.0))
                yd_ref[:, ps] = acc
            hg = h_ref[:, gs]
            yd_ref[:, gs] += _mdot(cg, hg) * cs["ea_x"][:, gs]
            st = _mdot(bg, xc_ref[:, gs] * cs["ws_x"][:, gs], 0, 0)
            st_ref[0, :, gs] = hg
            h_ref[:, gs] = cs["eal_x"][:, gs] * hg + st
        y = yd_ref[...] + dsk_ref[...] * xc_ref[:, 0:D]
        y_ref[...] = y
        yg = y * _silu(z_ref[...].astype(F32))
        for g in range(NG):
            gs = slice(GW * g, GW * (g + 1))
            seg = yg[:, gs]
            ya_ref[:, gs] = seg * _rms(seg) * ng_ref[:, gs]

    return pl.pallas_call(
        body, name="ssd_fwd", grid=(nc,),
        in_specs=[_row_spec(CH, XBC), _row_spec(CH, 128), _row_spec(CH, D, COL_Z), _full_spec((1, 128)),
                  _full_spec((1, D)), _full_spec((1, D)), _full_spec((128, D))],
        out_specs=[_row_spec(CH, D), _row_spec(CH, D), pl.BlockSpec((1, NS, D), lambda i: (i, 0, 0))],
        out_shape=[_sds((s, D)), _sds((s, D)), _sds((nc, NS, D))],
        scratch_shapes=[pltpu.VMEM((NS, D), F32), pltpu.VMEM((128, 128), F32), pltpu.VMEM((128, 128), F32),
                        pltpu.VMEM((CH, D), F32)],
        compiler_params=_cp("arbitrary"),
    )(xbc_c, dt, proj5, a_log, dskip_x, ssm_norm, expand)


def _lru_gates(xr, wab_ref, ba_ref, bx_ref, lam_ref):
    pre = _mdot(xr, wab_ref[...])
    gr = _sig(pre[:, 0:D] + ba_ref[...])
    gi = _sig(pre[:, D:2 * D] + bx_ref[...])
    sp = _softplus(-lam_ref[...])
    la = -LRU_C * gr * sp
    a = jnp.exp(la)
    oms = _one_minus_sq(a, la)
    inv_mult = lax.rsqrt(oms)
    return gr, gi, sp, a, oms * inv_mult, inv_mult


def _blocked_scan(a, u, carry_ref, a_ref, u_ref, c_ref, out_ref, reverse):
    t = a.shape[0]
    ns = t // 8

    def combine(av, uv, idx, n, sh):
        m = (idx < n - sh) if reverse else (idx >= sh)
        by = n - sh if reverse else sh
        return jnp.where(m, av * pltpu.roll(av, by, 0), av), jnp.where(m, uv + av * pltpu.roll(uv, by, 0), uv)

    row = lax.broadcasted_iota(jnp.int32, (t, D), 0)
    rin = jnp.bitwise_and(row, 7)
    for sh in (1, 2, 4):
        m = (rin < 8 - sh) if reverse else (rin >= sh)
        by = t - sh if reverse else sh
        a, u = jnp.where(m, a * pltpu.roll(a, by, 0), a), jnp.where(m, u + a * pltpu.roll(u, by, 0), u)
    a_ref[...] = a
    u_ref[...] = u
    edge = 0 if reverse else 7
    for j in range(ns):
        c_ref[j:j + 1, :] = a_ref[8 * j + edge:8 * j + edge + 1, :]
    at = c_ref[...]
    for j in range(ns):
        c_ref[j:j + 1, :] = u_ref[8 * j + edge:8 * j + edge + 1, :]
    ut = c_ref[...]
    srow = lax.broadcasted_iota(jnp.int32, (ns, D), 0)
    sh = 1
    while sh < ns:
        at, ut = combine(at, ut, srow, ns, sh)
        sh *= 2
    cv = carry_ref[0:1, :]
    ends = ut + at * cv
    last = 0 if reverse else ns - 1
    first = ns - 1 if reverse else 0
    c_ref[...] = jnp.where(srow == first, cv, pltpu.roll(ends, first if reverse else 1, 0))
    carry_ref[0:1, :] = jnp.sum(jnp.where(srow == last, ends, 0.0), axis=0, keepdims=True)
    for j in range(ns):
        sl = slice(8 * j, 8 * j + 8)
        out_ref[sl, :] = u_ref[sl, :] + a_ref[sl, :] * c_ref[j:j + 1, :]


def _lru_fwd(xr, proj5, ya, wab, ba, bx, lam, plan=None):
    s = xr.shape[0]
    t = min(256, s)

    def body(xr_ref, g_ref, ga_ref, gb_ref, ya_ref, wab_ref, ba_ref, bx_ref, lam_ref, h_ref, mg_ref, gr_ref,
             gi_ref, ao_ref, mo_ref, hc_ref, a_ref, u_ref, c_ref):
        @pl.when(pl.program_id(0) == 0)
        def _():
            hc_ref[...] = jnp.zeros_like(hc_ref)

        xrv = xr_ref[...]
        gr, gi, _, a, mult, _ = _lru_gates(xrv, wab_ref, ba_ref, bx_ref, lam_ref)
        gr_ref[...], gi_ref[...], ao_ref[...], mo_ref[...] = gr, gi, a, mult
        _blocked_scan(a, mult * gi * xrv, hc_ref, a_ref, u_ref, c_ref, h_ref, reverse=False)
        yb = h_ref[...] * _gelu(g_ref[...].astype(F32))
        mg_ref[...] = (_sig(ga_ref[...].astype(F32)) * ya_ref[...]
                       + _sig(gb_ref[...].astype(F32)) * yb).astype(mg_ref.dtype)

    return _pcall(
        body, (xr, proj5, proj5, proj5, ya, wab, ba, bx, lam), name="lru_fwd", grid=(s // t,),
        in_specs=[_row_spec(t, D), _row_spec(t, D, COL_G), _row_spec(t, D, COL_GA), _row_spec(t, D, COL_GB),
                  _row_spec(t, D), _full_spec((D, 2 * D), once=True), _full_spec((1, D)), _full_spec((1, D)),
                  _full_spec((1, D))],
        out_specs=[_row_spec(t, D)] * 6,
        out_shape=[_sds((s, D)), _sds((s, D), MXU)] + [_sds((s, D))] * 4,
        scratch_shapes=[pltpu.VMEM((8, D), F32), pltpu.VMEM((t, D), F32), pltpu.VMEM((t, D), F32),
                        pltpu.VMEM((t // 8, D), F32)],
        sem=("arbitrary",), plan=plan)


def _out_up_proj(merged, w_out, x, g2, g3, w_up):
    s = x.shape[0]
    t = min(ROWS_FUSED, s)

    def body(mg_ref, w_ref, x_ref, g2_ref, g3_ref, wu_ref, mix_ref, h1_ref, v_ref, pre_ref):
        mix = _mdot(mg_ref[...], w_ref[...])
        mix_ref[...] = mix
        h1 = x_ref[...] + mix * _rms(mix) * g2_ref[...]
        h1_ref[...] = h1
        v = (h1 * _rms(h1) * g3_ref[...]).astype(v_ref.dtype)
        v_ref[...] = v
        pre_ref[...] = _mdot(v, wu_ref[...]).astype(pre_ref.dtype)

    return pl.pallas_call(
        body, name="out_up_proj", grid=(s // t,),
        in_specs=[_row_spec(t, D), _full_spec((D, D), once=True), _row_spec(t, D), _full_spec((1, D)),
                  _full_spec((1, D)), _full_spec((D, DFF), once=True)],
        out_specs=[_row_spec(t, D), _row_spec(t, D), _row_spec(t, D), _row_spec(t, DFF)],
        out_shape=[_sds((s, D)), _sds((s, D)), _sds((s, D), MXU), _sds((s, DFF), MXU)],
        compiler_params=_cp("parallel"),
    )(merged, w_out, x, g2, g3, w_up)


def _down_loss(pre, w_down, h1, target, g4):
    s = pre.shape[0]
    t = min(ROWS_FUSED, s)

    def body(pre_ref, w_ref, h1_ref, tg_ref, g4_ref, dout_ref, dff_ref, loss_ref, dg4_ref):
        @pl.when(pl.program_id(0) == 0)
        def _():
            loss_ref[...] = jnp.zeros_like(loss_ref)
            dg4_ref[...] = jnp.zeros_like(dg4_ref)

        ff = _mdot(_relu2(pre_ref[...]), w_ref[...])
        r4 = _rms(ff)
        g4v = g4_ref[...]
        diff = h1_ref[...] + ff * r4 * g4v - tg_ref[...]
        sq = jnp.sum(jnp.sum(diff * diff, axis=1, keepdims=True), axis=0, keepdims=True)
        loss_ref[...] += (0.5 / D) * sq
        dout = diff * (1.0 / D)
        dout_ref[...] = dout
        dff, dg = _rms_bwd(ff, r4, g4v, dout)
        dff_ref[...] = dff.astype(dff_ref.dtype)
        dg4_ref[...] += dg

    return pl.pallas_call(
        body, name="down_loss", grid=(s // t,),
        in_specs=[_row_spec(t, DFF), _full_spec((DFF, D), once=True), _row_spec(t, D), _row_spec(t, D),
                  _full_spec((1, D))],
        out_specs=[_row_spec(t, D), _row_spec(t, D), _full_spec((1, 128)), _full_spec((1, D))],
        out_shape=[_sds((s, D)), _sds((s, D), MXU), _sds((1, 128)), _sds((1, D))],
        compiler_params=_cp("arbitrary"),
    )(pre, w_down, h1, target, g4)


def _dv_norms(dpre, w_up, h1, mix, dout, g3, g2, w_out):
    s = h1.shape[0]
    t = min(ROWS_FUSED, s)

    def body(dp_ref, w_ref, h1_ref, mix_ref, dout_ref, g3_ref, g2_ref, wo_ref, dh1_ref, dmix_ref, dmg_ref, dg3_ref,
             dg2_ref):
        @pl.when(pl.program_id(0) == 0)
        def _():
            dg3_ref[...] = jnp.zeros_like(dg3_ref)
            dg2_ref[...] = jnp.zeros_like(dg2_ref)

        dv = _mdot(dp_ref[...], w_ref[...], 1, 1)
        h1 = h1_ref[...]
        dh1n, dg3 = _rms_bwd(h1, _rms(h1), g3_ref[...], dv)
        dh1 = dout_ref[...] + dh1n
        dh1_ref[...] = dh1
        mix = mix_ref[...]
        dmix, dg2 = _rms_bwd(mix, _rms(mix), g2_ref[...], dh1)
        dmix = dmix.astype(dmix_ref.dtype)
        dmix_ref[...] = dmix
        dmg_ref[...] = _mdot(dmix, wo_ref[...], 1, 1)
        dg3_ref[...] += dg3
        dg2_ref[...] += dg2

    return pl.pallas_call(
        body, name="dv_norms", grid=(s // t,),
        in_specs=[_row_spec(t, DFF), _full_spec((D, DFF), once=True), _row_spec(t, D), _row_spec(t, D),
                  _row_spec(t, D), _full_spec((1, D)), _full_spec((1, D)), _full_spec((D, D), once=True)],
        out_specs=[_row_spec(t, D), _row_spec(t, D), _row_spec(t, D), _full_spec((1, D)), _full_spec((1, D))],
        out_shape=[_sds((s, D)), _sds((s, D), MXU), _sds((s, D)), _sds((1, D)), _sds((1, D))],
        compiler_params=_cp("arbitrary"),
    )(dpre, w_up, h1, mix, dout, g3, g2, w_out)


def _lru_bwd(dmerged, ya, xr, h, proj5, gates, wab_t, lam, plan=None):
    s = xr.shape[0]
    t = min(128, s)
    n = s // t
    rs = functools.partial(_rev_spec, t, D, n)

    def body(dm_ref, ya_ref, xr_ref, h_ref, hp_ref, g_ref, ga_ref, gb_ref, gr_ref, gi_ref, a_ref, m_ref, wab_ref,
             lam_ref, dya_ref, d3_ref, dxr_ref, dp2_ref, dlam_ref, dba_ref, dbx_ref, gc_ref, af_ref, an_ref, us_ref,
             c_ref, gs_ref):
        i = pl.program_id(0)

        @pl.when(i == 0)
        def _():
            gc_ref[...] = jnp.zeros_like(gc_ref)
            af_ref[...] = jnp.zeros_like(af_ref)
            dlam_ref[...] = jnp.zeros_like(dlam_ref)
            dba_ref[...] = jnp.zeros_like(dba_ref)
            dbx_ref[...] = jnp.zeros_like(dbx_ref)

        xrv = xr_ref[...]
        gr, gi, a, mult = gr_ref[...], gi_ref[...], a_ref[...], m_ref[...]
        sp = _softplus(-lam_ref[...])
        inv_mult = 1.0 / mult
        hv = h_ref[...]
        dm = dm_ref[...]
        sa = _sig(ga_ref[...].astype(F32))
        sb = _sig(gb_ref[...].astype(F32))
        gel, dgel = _gelu_and_grad(g_ref[...].astype(F32))
        dya = dm * sa
        dya_ref[...] = dya
        dyb = dm * sb
        dybh = dyb * hv
        d3_ref[:, 0:D] = (dybh * dgel).astype(d3_ref.dtype)
        d3_ref[:, D:2 * D] = (dya * ya_ref[...] * (1.0 - sa)).astype(d3_ref.dtype)
        d3_ref[:, 2 * D:3 * D] = (dybh * gel * (1.0 - sb)).astype(d3_ref.dtype)
        row = lax.broadcasted_iota(jnp.int32, (t, D), 0)
        an = jnp.where(row == t - 1, af_ref[0:1, :], pltpu.roll(a, t - 1, 0))
        _blocked_scan(an, dyb * gel, gc_ref, an_ref, us_ref, c_ref, gs_ref, reverse=True)
        gfull = gs_ref[...]
        af_ref[0:1, :] = jnp.sum(jnp.where(row == 0, a, 0.0), axis=0, keepdims=True)
        hlast = jnp.where(i == n - 1, 0.0, hp_ref[7:8, :])
        hprev = jnp.where(row == 0, hlast, pltpu.roll(hv, 1, 0))
        gx = gfull * xrv
        dgi = gx * mult
        dla = a * (gfull * hprev - gx * gi * a * inv_mult)
        dgr = dla * (-LRU_C * sp)
        dsp = jnp.sum(dla * (-LRU_C * gr), axis=0, keepdims=True)
        dlam_ref[...] += dsp * (-_sig(-lam_ref[...]))
        dpr = dgr * gr * (1.0 - gr)
        dpi = dgi * gi * (1.0 - gi)
        dp2_ref[:, 0:D] = dpr.astype(dp2_ref.dtype)
        dp2_ref[:, D:2 * D] = dpi.astype(dp2_ref.dtype)
        dba_ref[...] += jnp.sum(dpr, axis=0, keepdims=True)
        dbx_ref[...] += jnp.sum(dpi, axis=0, keepdims=True)
        dxr_ref[...] = gfull * mult * gi + _mdot(dp2_ref[...], wab_ref[...])

    hp_spec = pl.BlockSpec((8, D), lambda i: (jnp.maximum((n - 1 - i) * (t // 8) - 1, 0), 0))
    wide = lambda c: pl.BlockSpec((t, c), lambda i: (n - 1 - i, 0))
    return _pcall(
        body, (dmerged, ya, xr, h, h, proj5, proj5, proj5, *gates, wab_t, lam), name="lru_bwd", grid=(n,),
        in_specs=[rs(), rs(), rs(), rs(), hp_spec, rs(COL_G), rs(COL_GA), rs(COL_GB), rs(), rs(), rs(), rs(),
                  _full_spec((2 * D, D), once=True), _full_spec((1, D))],
        out_specs=[rs(), wide(3 * D), rs(), wide(2 * D), _full_spec((1, D)), _full_spec((1, D)), _full_spec((1, D))],
        out_shape=[_sds((s, D)), _sds((s, 3 * D), MXU), _sds((s, D)), _sds((s, 2 * D), MXU), _sds((1, D)),
                   _sds((1, D)), _sds((1, D))],
        scratch_shapes=[pltpu.VMEM((8, D), F32), pltpu.VMEM((8, D), F32), pltpu.VMEM((t, D), F32),
                        pltpu.VMEM((t, D), F32), pltpu.VMEM((t // 8, D), F32), pltpu.VMEM((t, D), F32)],
        sem=("arbitrary",), plan=plan)


def _ssd_bwd(dya, y, proj5, xbc_c, dt, states, a_log, dskip_x, ssm_norm, expand, reduce_, plan=None):
    s = xbc_c.shape[0]
    nc = s // CH
    rv = functools.partial(_rev_spec, CH)

    def body(dya_ref, y_ref, z_ref, xc_ref, dt_ref, st_ref, alog_ref, dsk_ref, ng_ref, e_ref, et_ref, dz_ref,
             dxc_ref, ddt_ref, dng_ref, ddsk_ref, dalog_ref, dh_ref, at_ref, dtt_ref, dat_ref, ddtt_ref, dy_ref,
             yoffdy_ref, xbds_ref):
        @pl.when(pl.program_id(0) == 0)
        def _():
            dh_ref[...] = jnp.zeros_like(dh_ref)
            dng_ref[...] = jnp.zeros_like(dng_ref)
            ddsk_ref[...] = jnp.zeros_like(ddsk_ref)
            dalog_ref[...] = jnp.zeros_like(dalog_ref)

        cs = _ssd_chunk_setup(dt_ref, alog_ref, e_ref, at_ref, dtt_ref)
        lane, row = cs["lane"], cs["row"]
        et = et_ref[...]
        for g in range(NG):
            gs = slice(GW * g, GW * (g + 1))
            yv = y_ref[:, gs]
            zv = z_ref[:, gs].astype(F32)
            sz = _silu(zv)
            yg = yv * sz
            dyav = dya_ref[:, gs]
            dyg, dng = _rms_bwd(yg, _rms(yg), ng_ref[:, gs], dyav)
            dng_ref[:, gs] += dng
            dy_ref[:, gs] = dyg * sz
            dz_ref[:, gs] = (dyg * yv * _dsilu(zv)).astype(dz_ref.dtype)
        dyv = dy_ref[...]
        xs = xc_ref[:, 0:D]
        ddsk_ref[...] += jnp.sum(dyv * xs, axis=0, keepdims=True)
        dxc_ref[:, 0:D] = dyv * dsk_ref[...]
        dat_ref[...] = jnp.zeros_like(dat_ref)
        ddtt_ref[...] = jnp.zeros_like(ddtt_ref)
        hh = jnp.sum(dh_ref[...] * st_ref[0], axis=0, keepdims=True)
        deal = jnp.max(_xdot(jnp.broadcast_to(hh, (8, D)), et, 3), axis=0, keepdims=True)
        d_acum = jnp.zeros((CH, 128), F32)
        for g in range(NG):
            gs = slice(GW * g, GW * (g + 1))
            bs_ = slice(D + NS * g, D + NS * (g + 1))
            cs_ = slice(D + NG * NS + NS * g, D + NG * NS + NS * (g + 1))
            bg = xc_ref[:, bs_]
            cg = xc_ref[:, cs_]
            cb = _mdot(cg, bg, 1, 1)
            hg = st_ref[0, :, gs]
            dhg = dh_ref[:, gs]
            dyg_ = dy_ref[:, gs]
            xsg = xc_ref[:, gs]
            ea = cs["ea_x"][:, gs]
            wsx = cs["ws_x"][:, gs]
            dp = dyg_ * ea
            yoffdy_ref[:, gs] = dp * _mdot(cg, hg)
            dc = _mdot(dp, hg, 1, 1)
            dhprev = _mdot(cg, dp, 0, 0)
            bds = _mdot(bg, dhg)
            dxc_ref[:, gs] += wsx * bds
            xbds_ref[:, gs] = xsg * bds
            db = _mdot(xsg * wsx, dhg, 1, 1)
            dh_ref[:, gs] = dhprev + cs["eal_x"][:, gs] * dhg
            dcbs = jnp.zeros((CH, CH), F32)
            for j in range(4 * g, 4 * g + 4):
                ps = slice(128 * j, 128 * (j + 1))
                xp = xc_ref[:, ps]
                dyp = dy_ref[:, ps]
                dxacc = jnp.zeros((CH, 128), F32)
                for hf in range(2):
                    hd = 2 * j + hf
                    ld, rowdt = _head_decay(cs, at_ref, dtt_ref, hd)
                    hm = (lane >= HP) if hf else (lane < HP)
                    dym = jnp.where(hm, dyp, 0.0)
                    w = cb * ld * rowdt
                    dw = _mdot(dym, jnp.where(hm, xp, 0.0), 1, 1)
                    dxacc = dxacc + _mdot(w, dym, 0, 0)
                    nm = dw * w
                    ddtt_ref[hd:hd + 1, :] += jnp.sum(dw * cb * ld, axis=0, keepdims=True)
                    d_acum = d_acum + jnp.where(lane == hd, jnp.sum(nm, axis=1, keepdims=True), 0.0)
                    dat_ref[hd:hd + 1, :] -= jnp.sum(nm, axis=0, keepdims=True)
                    dcbs = dcbs + dw * ld * rowdt
                dxc_ref[:, ps] += dxacc
            dxc_ref[:, bs_] = db + _mdot(dcbs, cg, 0, 0)
            dxc_ref[:, cs_] = dc + _mdot(dcbs, bg)
        dws = _xdot(xbds_ref[...], et, 2)
        ws = cs["ws"]
        d_acum = d_acum - dws * ws + _xdot(yoffdy_ref[...], et, 2) + dat_ref[...].T
        d_alast = jnp.sum(dws * ws, axis=0, keepdims=True) + deal * cs["eal"]
        d_acum = d_acum + jnp.where(row == CH - 1, d_alast, 0.0)
        triu = row <= lane
        d_adt = _xdot(triu.astype(F32), d_acum, 3, split_b=True)
        ddt_ref[...] = dws * jnp.exp(cs["alast"] - cs["acum"]) + ddtt_ref[...].T + d_adt * cs["a"]
        dalog_ref[...] += jnp.sum(d_adt * cs["dtv"], axis=0, keepdims=True) * cs["a"]

    return _pcall(
        body, (dya, y, proj5, xbc_c, dt, states, a_log, dskip_x, ssm_norm, expand, reduce_), name="ssd_bwd",
        grid=(nc,),
        in_specs=[rv(D, nc), rv(D, nc), rv(D, nc, COL_Z), rv(XBC, nc), rv(128, nc),
                  pl.BlockSpec((1, NS, D), lambda i: (nc - 1 - i, 0, 0)), _full_spec((1, 128)), _full_spec((1, D)),
                  _full_spec((1, D)), _full_spec((128, D)), _full_spec((D, 128))],
        out_specs=[rv(D, nc), rv(XBC, nc), rv(128, nc), _full_spec((1, D)), _full_spec((1, D)),
                   _full_spec((1, 128))],
        out_shape=[_sds((s, D), MXU), _sds((s, XBC)), _sds((s, 128)), _sds((1, D)), _sds((1, D)), _sds((1, 128))],
        scratch_shapes=[pltpu.VMEM((NS, D), F32), pltpu.VMEM((128, 128), F32), pltpu.VMEM((128, 128), F32),
                        pltpu.VMEM((128, 128), F32), pltpu.VMEM((128, 128), F32), pltpu.VMEM((CH, D), F32),
                        pltpu.VMEM((CH, D), F32), pltpu.VMEM((CH, D), F32)],
        sem=("arbitrary",), plan=plan)


def _conv_bwd(dxbc_c, dsilu, dxr, ddt, xbc_raw, proj5, dt_raw, cw_s, cw_l, dt_bias, plan=None):
    s = xbc_raw.shape[0]
    t = min(ROWS_CONV // 2, s)
    n = s // t

    def body(dxc_ref, dsl_ref, dxr_ref, ddt_ref, xs_ref, xl_ref, dtr_ref, cws_ref, cwl_ref, dtb_ref, dxs_ref,
             dxl_ref, ddtr_ref, dcws_ref, dcbs_ref, dcwl_ref, dcbl_ref, ddtb_ref, ds_ref, dl_ref):
        @pl.when(pl.program_id(0) == 0)
        def _():
            ds_ref[t:t + 8, :] = jnp.zeros((8, XBC), F32)
            dl_ref[t:t + 8, :] = jnp.zeros((8, D), F32)
            for r in (dcws_ref, dcbs_ref, dcwl_ref, dcbl_ref, ddtb_ref):
                r[...] = jnp.zeros_like(r)

        ds_ref[0:t, :] = dxc_ref[...] * dsl_ref[...].astype(F32)
        dl_ref[0:t, :] = dxr_ref[...]

        def back(dbuf, x_ref, w_ref, dx_ref, dw_ref, db_ref):
            xv = x_ref[...].astype(F32)
            dpre = dbuf[0:t, :]
            dx = w_ref[3:4, :] * dpre
            dw_ref[3:4, :] += jnp.sum(dpre * xv, axis=0, keepdims=True)
            db_ref[...] += jnp.sum(dpre, axis=0, keepdims=True)
            for k in (1, 2, 3):
                ahead = dbuf[k:t + k, :]
                dx = dx + w_ref[3 - k:4 - k, :] * ahead
                dw_ref[3 - k:4 - k, :] += jnp.sum(ahead * xv, axis=0, keepdims=True)
            dx_ref[...] = dx.astype(dx_ref.dtype)
            dbuf[t:t + 8, :] = dbuf[0:8, :]

        back(ds_ref, xs_ref, cws_ref, dxs_ref, dcws_ref, dcbs_ref)
        back(dl_ref, xl_ref, cwl_ref, dxl_ref, dcwl_ref, dcbl_ref)
        ddtr = ddt_ref[...] * _sig(dtr_ref[...] + dtb_ref[...])
        ddtr_ref[...] = ddtr.astype(ddtr_ref.dtype)
        ddtb_ref[...] += jnp.sum(ddtr, axis=0, keepdims=True)

    rv = functools.partial(_rev_spec, t)
    return _pcall(
        body, (dxbc_c, dsilu, dxr, ddt, xbc_raw, proj5, dt_raw, cw_s, cw_l, dt_bias), name="conv_bwd", grid=(n,),
        in_specs=[rv(XBC, n), rv(XBC, n), rv(D, n), rv(128, n), rv(XBC, n), rv(D, n, COL_XL), rv(128, n),
                  _full_spec((4, XBC)), _full_spec((4, D)), _full_spec((1, 128))],
        out_specs=[rv(XBC, n), rv(D, n), rv(128, n), _full_spec((4, XBC)), _full_spec((1, XBC)), _full_spec((4, D)),
                   _full_spec((1, D)), _full_spec((1, 128))],
        out_shape=[_sds((s, XBC), MXU), _sds((s, D), MXU), _sds((s, 128), MXU), _sds((4, XBC)), _sds((1, XBC)),
                   _sds((4, D)), _sds((1, D)), _sds((1, 128))],
        scratch_shapes=[pltpu.VMEM((t + 8, XBC), F32), pltpu.VMEM((t + 8, D), F32)],
        sem=("arbitrary",), plan=plan)


def _du_norm(d3, dz, dxl, dxbc, ddtr, w5, wxbc, wdt, x, dh1, g1, plan=None):
    s = x.shape[0]
    t = min(ROWS_FUSED, s)

    def body(d3_ref, dz_ref, dxl_ref, dxbc_ref, ddtr_ref, w5_ref, wx_ref, wd_ref, x_ref, dh1_ref, g1_ref, dx_ref,
             dg1_ref):
        @pl.when(pl.program_id(0) == 0)
        def _():
            dg1_ref[...] = jnp.zeros_like(dg1_ref)

        du = (_mdot(dxbc_ref[...], wx_ref[...]) + _mdot(ddtr_ref[...], wd_ref[...])
              + _mdot(d3_ref[...], w5_ref[0:3 * D, :])
              + _mdot(dz_ref[...], w5_ref[COL_Z * D:(COL_Z + 1) * D, :])
              + _mdot(dxl_ref[...], w5_ref[COL_XL * D:(COL_XL + 1) * D, :]))
        xv = x_ref[...]
        dxn, dg1 = _rms_bwd(xv, _rms(xv), g1_ref[...], du)
        dx_ref[...] = dh1_ref[...] + dxn
        dg1_ref[...] += dg1

    return _pcall(
        body, (d3, dz, dxl, dxbc, ddtr, w5, wxbc, wdt, x, dh1, g1), name="du_norm", grid=(s // t,),
        in_specs=[_row_spec(t, 3 * D), _row_spec(t, D), _row_spec(t, D), _row_spec(t, XBC), _row_spec(t, 128),
                  _full_spec((5 * D, D), once=True), _full_spec((XBC, D), once=True),
                  _full_spec((128, D), once=True), _row_spec(t, D), _row_spec(t, D), _full_spec((1, D))],
        out_specs=[_row_spec(t, D), _full_spec((1, D))],
        out_shape=[_sds((s, D)), _sds((1, D))],
        sem=("arbitrary",), plan=plan)


def _adamw(w, g, m, v, name):
    r, c = w.shape
    t = r
    if r * c > 256 * 1024:
        t = next(cand for cand in (512, 256, 128, 64, 32, 16, 8) if r % cand == 0 and cand * c <= 512 * 1024)
    bc1 = 1.0 - ADAM_B1 ** ADAM_STEP
    bc2 = 1.0 - ADAM_B2 ** ADAM_STEP

    def body(w_ref, g_ref, m_ref, v_ref, d_ref, nm_ref, nv_ref):
        gv = g_ref[...]
        nm = ADAM_B1 * m_ref[...] + (1.0 - ADAM_B1) * gv
        nv = ADAM_B2 * v_ref[...] + (1.0 - ADAM_B2) * (gv * gv)
        nm_ref[...] = nm
        nv_ref[...] = nv
        d_ref[...] = -ADAM_LR * ((nm / bc1) / (jnp.sqrt(nv / bc2) + ADAM_EPS) + ADAM_WD * w_ref[...])

    spec = pl.BlockSpec((t, c), lambda i: (i, 0))
    return pl.pallas_call(
        body, name=name, grid=(r // t,), in_specs=[spec] * 4, out_specs=[spec] * 3,
        out_shape=[_sds((r, c))] * 3, compiler_params=_cp("parallel"),
    )(w, g, m, v)


def _half_blocks(shape, axis):
    r, c = shape
    if axis == 0:
        t = 256 if (r // 2) % 256 == 0 else 128
        nb = (r // 2) // t
        return (t, c), nb, (lambda i: (i, 0)), (lambda i: (i % nb, 0))
    nb = (c // 2) // 128
    return (r, 128), nb, (lambda i: (0, i)), (lambda i: (0, i % nb))


def _adamw_halves(w, g_mine, g_other, m, v, cidx, name, axis=0):
    r, c = w.shape
    blk, nb, whole, part = _half_blocks(w.shape, axis)
    bc1 = 1.0 - ADAM_B1 ** ADAM_STEP
    bc2 = 1.0 - ADAM_B2 ** ADAM_STEP

    def body(c_ref, w_ref, gm_ref, go_ref, m_ref, v_ref, g_ref, d_ref, nm_ref, nv_ref):
        mine = (pl.program_id(0) // nb) == c_ref[0]
        gv = jnp.where(mine, gm_ref[...], go_ref[...])
        g_ref[...] = gv
        nm = ADAM_B1 * m_ref[...] + (1.0 - ADAM_B1) * gv
        nv = ADAM_B2 * v_ref[...] + (1.0 - ADAM_B2) * (gv * gv)
        nm_ref[...] = nm
        nv_ref[...] = nv
        d_ref[...] = -ADAM_LR * ((nm / bc1) / (jnp.sqrt(nv / bc2) + ADAM_EPS) + ADAM_WD * w_ref[...])

    spec = pl.BlockSpec(blk, lambda i, c_ref: whole(i))
    half = pl.BlockSpec(blk, lambda i, c_ref: part(i))
    return pl.pallas_call(
        body, name=name,
        grid_spec=pltpu.PrefetchScalarGridSpec(num_scalar_prefetch=1, grid=(2 * nb,),
                                               in_specs=[spec, half, half, spec, spec], out_specs=[spec] * 4),
        out_shape=[_sds((r, c))] * 4, compiler_params=_cp("parallel"),
    )(cidx, w, g_mine, g_other, m, v)


def _block_diag(w):
    eye = jnp.eye(NH, dtype=w.dtype)
    return (w[:, :, None, :] * eye[:, None, :, None]).reshape(D, D)


def _diag_blocks(full):
    eye = jnp.eye(NH, dtype=full.dtype)
    return (full.reshape(NH, HP, NH, HP) * eye[:, None, :, None]).sum(axis=2)


def _pad_lanes(v, n=128):
    return jnp.pad(v, ((0, 0), (0, n - v.shape[1])))


def _local_step(x, target, p, dist=None):
    heads = jnp.arange(D, dtype=jnp.int32) // HP
    expand = (jnp.arange(128, dtype=jnp.int32)[:, None] == heads[None, :]).astype(F32)
    reduce_ = expand.T
    dskip_x = jnp.repeat(p["d_skip"], HP, axis=1)
    a_log = _pad_lanes(p["a_log"])
    dt_bias = _pad_lanes(p["dt_bias"])
    wab = jnp.concatenate([_block_diag(p["lru_wa"]), _block_diag(p["lru_wx"])], axis=1).astype(MXU)
    ba = p["lru_ba"].reshape(1, D)
    bx = p["lru_bx"].reshape(1, D)

    def hosted(key, fn):
        plan = dist.plan(key) if dist is not None else None
        if plan is None:
            return fn(plan=None)
        outs, got = fn(plan=plan)
        dist.done(key, got, p)
        return outs

    u = hosted("norm_u", functools.partial(_norm_cast, x, p["norm_mix_pre"], "norm_u"))
    w5, wxbc, wdt = p["w5"], p["wxbc"], p["wdt"]
    proj5 = hosted("proj5", functools.partial(_matmul, u, w5, name="proj5", tb=True, tm=1024, out_dtype=MXU))
    xbc_raw = _matmul(u, wxbc, name="proj_xbc", tb=True, tn=XBC)
    dt_raw = _matmul(u, wdt, name="proj_dt", tb=True)
    xbc_c, dsilu, xr, dt = _conv_fwd(xbc_raw, proj5, dt_raw, p["conv_ssm_w"], p["conv_ssm_b"], p["conv_lru_w"],
                                     p["conv_lru_b"], dt_bias)
    y, ya, states = _ssd_fwd(xbc_c, dt, proj5, a_log, dskip_x, p["ssm_norm"], expand)
    h, merged, *gates = hosted("lru_fwd", functools.partial(_lru_fwd, xr, proj5, ya, wab, ba, bx, p["lru_lambda"]))
    mix, h1, v, pre = _out_up_proj(merged, p["w_out"], x, p["norm_mix_post"], p["norm_mlp_pre"], p["w_up"])
    dout, dff, loss, dg4 = _down_loss(pre, p["w_down"], h1, target, p["norm_mlp_post"])

    dpre = _matmul(dff, p["w_down"], name="d_pre", tb=True, tm=1024, out_dtype=MXU,
                   epi=lambda r, pr: r * (2.0 * jnp.maximum(pr.astype(F32), 0.0)), epi_args=(pre,))
    g_w_down = _matmul(pre, dff, name="dw_down", ta=True, tm=1024, tn=1024, tk=TK_GRAD, a_fn=_relu2)
    dh1, dmix, dmerged, dg3, dg2 = _dv_norms(dpre, p["w_up"], h1, mix, dout, p["norm_mlp_pre"], p["norm_mix_post"],
                                             p["w_out"])
    g_w_up = _matmul(v, dpre, name="dw_up", ta=True, tm=1024, tn=1024, tk=TK_GRAD, col_blocks_major=True)
    g_w_out = _matmul(merged, dmix, name="dw_out", ta=True, tm=1024, tn=1024, tk=TK_GRAD)
    if dist is not None:
        dist.early_grads(w_down=g_w_down, w_up=g_w_up, w_out=g_w_out)
    dya, d3, dxr, dp2, dlam, dba, dbx = hosted("lru_bwd", functools.partial(
        _lru_bwd, dmerged, ya, xr, h, proj5, gates, wab.T, p["lru_lambda"]))
    g_wab = _matmul(xr, dp2, name="dw_lru", ta=True, tm=1024, tn=1024, tk=TK_GRAD)
    g_wa, g_wx = _diag_blocks(g_wab[:, :D]), _diag_blocks(g_wab[:, D:])
    dz, dxbc_c, ddt, dng, ddsk, dalog = hosted("ssd_bwd", functools.partial(
        _ssd_bwd, dya, y, proj5, xbc_c, dt, states, a_log, dskip_x, p["ssm_norm"], expand, reduce_))
    (dxbc, dxl, ddtr, dcws, dcbs, dcwl, dcbl, ddtb) = hosted("conv_bwd", functools.partial(
        _conv_bwd, dxbc_c, dsilu, dxr, ddt, xbc_raw, proj5, dt_raw, p["conv_ssm_w"], p["conv_lru_w"], dt_bias))
    gw3 = _matmul(d3, u, name="dw_in_lru", ta=True, tm=1024, tn=1024, tk=TK_GRAD)
    gwz = _matmul(dz, u, name="dw_in_z", ta=True, tm=1024, tn=1024, tk=TK_GRAD)
    gwxl = _matmul(dxl, u, name="dw_in_xl", ta=True, tm=1024, tn=1024, tk=TK_GRAD)
    gwxbc = _matmul(dxbc, u, name="dw_in_xbc", ta=True, tm=XBC, tn=1024, tk=TK_GRAD)
    gwdt = _matmul(ddtr, u, name="dw_in_dt", ta=True, tm=128, tn=1024, tk=TK_GRAD)
    g_w_in_t = jnp.concatenate([gwz, gwxbc, gwdt[:NH], gw3[:D], gwxl, gw3[D:2 * D], gw3[2 * D:]], axis=0)
    grads = {
        "w_in_t": g_w_in_t, "conv_ssm_w": dcws, "conv_ssm_b": dcbs, "dt_bias": ddtb[:, :NH],
        "a_log": dalog[:, :NH], "d_skip": ddsk.reshape(NH, HP).sum(axis=1)[None, :], "ssm_norm": dng,
        "conv_lru_w": dcwl, "conv_lru_b": dcbl, "lru_wa": g_wa, "lru_ba": dba.reshape(NH, HP), "lru_wx": g_wx,
        "lru_bx": dbx.reshape(NH, HP), "lru_lambda": dlam, "w_out": g_w_out, "norm_mix_post": dg2,
        "norm_mlp_pre": dg3, "w_up": g_w_up, "w_down": g_w_down, "norm_mlp_post": dg4,
    }
    if dist is not None:
        dist.late_grads(grads, loss[0, 0])
    grad_x, grads["norm_mix_pre"] = hosted("du_norm", functools.partial(
        _du_norm, d3, dz, dxl, dxbc, ddtr, w5, wxbc, wdt, x, dh1, p["norm_mix_pre"]))
    return loss[0, 0], grad_x, grads


def _split_w_in_t(w_in_t):
    z, xbc, dtc, g, xl, ga, gb = jnp.split(w_in_t, [D, D + XBC, D + XBC + NH, 2 * D + XBC + NH,
                                                    3 * D + XBC + NH, 4 * D + XBC + NH], axis=0)
    return jnp.concatenate([g, ga, gb, z, xl], axis=0), xbc, jnp.pad(dtc, ((0, 128 - NH), (0, 0)))


COMM = BF16


def _place():
    x, y, c = lax.axis_index("x"), lax.axis_index("y"), lax.axis_index("c")
    chips = [(1 - x, y), (x, 1 - y), (1 - x, 1 - y)]
    return x, y, c, chips


def _remote(src, dst, send_sem, recv_sem, to):
    return pltpu.make_async_remote_copy(src_ref=src, dst_ref=dst, send_sem=send_sem, recv_sem=recv_sem, device_id=to,
                                        device_id_type=MESH)


def _gather_plan(big, small=(), axes=None):
    nb = len(big)
    arrs = list(big) + list(small)
    na = len(arrs)
    axes = list(axes or [0] * nb)

    def half(ref, a, k, which):
        h = arrs[a].shape[axes[a]] // 2
        cut = (pl.ds(which * h, h),) if axes[a] == 0 else (slice(None), pl.ds(which * h, h))
        return ref.at[cut] if k is None else ref.at[(k,) + cut]

    def direct(ins, outs, send, recv):
        x, y, c, chips = _place()
        k = 2 * x + y
        cps = []
        for a in range(na):
            src, dst = (half(ins[a], a, None, c), half(outs[a], a, k, c)) if a < nb else (ins[a], outs[a].at[k])
            cps += [_remote(src, dst, send.at[a, j], recv.at[a, j], (cx, cy, c)) for j, (cx, cy) in enumerate(chips)]
        return cps

    def passed(outs, send, recv):
        x, y, c, chips = _place()
        cps = []
        for j, (cx, cy) in enumerate(chips):
            for a in range(nb):
                got = half(outs[a], a, 2 * cx + cy, c)
                cps.append(_remote(got, got, send.at[a, 3 + j], recv.at[a, 3 + j], (x, y, 1 - c)))
        return cps

    def start(ins, outs, sems):
        for cp in direct(ins, outs, *sems[0]):
            cp.start()

    def mid(ins, outs, sems):
        send, recv = sems[0]
        _, _, c, chips = _place()
        fwd = passed(outs, send, recv)
        for j, (cx, cy) in enumerate(chips):
            kj = 2 * cx + cy
            for a in range(na):
                got = half(outs[a], a, kj, c) if a < nb else outs[a].at[kj]
                _remote(got, got, send.at[a, j], recv.at[a, j], (cx, cy, c)).wait_recv()
                if a < nb:
                    fwd[j * nb + a].start()

    def finish(ins, outs, sems):
        send, recv = sems[0]
        x, y, c, chips = _place()
        for j, (cx, cy) in enumerate(chips):
            for a in range(nb):
                got = half(outs[a], a, 2 * cx + cy, 1 - c)
                _remote(got, got, send.at[a, 3 + j], recv.at[a, 3 + j], (x, y, 1 - c)).wait_recv()
        for cp in direct(ins, outs, send, recv) + passed(outs, send, recv):
            cp.wait_send()

    return _Plan(arrs, [_sds((NCHIP,) + a.shape, a.dtype) for a in arrs], [(na, 6)], start, finish, mid)


def _own_shards(gathered, shards):
    kchip = 2 * lax.axis_index("x") + lax.axis_index("y")
    return [lax.dynamic_update_index_in_dim(o, a, kchip, 0) for o, a in zip(gathered, shards)]


def _swap_plan(ins, outs, sems, copies):
    def start(i, o, s):
        for cp in copies(i, o, *s[0]):
            cp.start()

    def finish(i, o, s):
        for cp in copies(i, o, *s[0]):
            cp.wait()

    return _Plan(ins, outs, [sems], start, finish)


def _half_shape(shape, axis):
    return tuple(d // 2 if i == axis else d for i, d in enumerate(shape))


def _pair_exchange_plan(gs, axis=1):
    def copies(ins, outs, send, recv):
        x, y, c, _ = _place()
        cps = []
        for a in range(len(gs)):
            h = ins[a].shape[axis] // 2
            theirs = pl.ds((1 - c) * h, h)
            src = ins[a].at[:, theirs] if axis == 1 else ins[a].at[:, :, theirs]
            cps.append(_remote(src, outs[a], send.at[a], recv.at[a], (x, y, 1 - c)))
        return cps

    return _swap_plan(gs, [_sds(_half_shape(g.shape, axis), g.dtype) for g in gs], (len(gs),), copies)


def _pair_add(g, got, cidx, name, axis=1):
    half = _half_shape(g.shape, axis)
    blk, nt, _, part = _half_blocks(g.shape[1:], axis - 1)

    def body(c_ref, g_ref, o_ref, p_ref, pc_ref):
        sm = g_ref[...] + o_ref[...]
        p_ref[...] = sm
        pc_ref[...] = sm.astype(pc_ref.dtype)

    def mine(k, i, c_ref):
        j = c_ref[0] * nt + i
        return (k, j, 0) if axis == 1 else (k, 0, j)

    spec = pl.BlockSpec((1,) + blk, lambda k, i, c_ref: (k,) + part(i))
    return pl.pallas_call(
        body, name=name,
        grid_spec=pltpu.PrefetchScalarGridSpec(
            num_scalar_prefetch=1, grid=(g.shape[0], nt),
            in_specs=[pl.BlockSpec((1,) + blk, mine), spec], out_specs=[spec, spec]),
        out_shape=[_sds(half), _sds(half, COMM)],
        compiler_params=_cp("parallel", "parallel"),
    )(cidx, g, got)


def _chip_exchange_plan(ps):
    def copies(ins, outs, send, recv):
        _, _, c, chips = _place()
        return [_remote(ins[a].at[2 * cx + cy], outs[a].at[j], send.at[a, j], recv.at[a, j], (cx, cy, c))
                for a in range(len(ps)) for j, (cx, cy) in enumerate(chips)]

    return _swap_plan(ps, [_sds((NCHIP - 1,) + p.shape[1:], p.dtype) for p in ps], (len(ps), 3), copies)


def _shard_sum(p, got, kidx, name, axis=1):
    full = tuple(2 * d if i == axis - 1 else d for i, d in enumerate(p.shape[1:]))
    blk, nt, _, part = _half_blocks(full, axis - 1)

    def body(k_ref, p_ref, g_ref, o_ref):
        sm = p_ref[0]
        for j in range(NCHIP - 1):
            sm = sm + g_ref[j].astype(F32)
        o_ref[...] = sm

    return pl.pallas_call(
        body, name=name,
        grid_spec=pltpu.PrefetchScalarGridSpec(
            num_scalar_prefetch=1, grid=(nt,),
            in_specs=[pl.BlockSpec((1,) + blk, lambda i, k_ref: (k_ref[0],) + part(i)),
                      pl.BlockSpec((NCHIP - 1,) + blk, lambda i, k_ref: (0,) + part(i))],
            out_specs=pl.BlockSpec(blk, lambda i, k_ref: part(i))),
        out_shape=_sds(p.shape[1:]),
        compiler_params=_cp("parallel"),
    )(kidx, p, got)


def _pair_swap_plan(rs):
    def copies(ins, outs, send, recv):
        x, y, c, _ = _place()
        return [_remote(ins[a], outs[a], send.at[a], recv.at[a], (x, y, 1 - c)) for a in range(len(rs))]

    return _swap_plan(rs, [_sds(r.shape, r.dtype) for r in rs], (len(rs),), copies)


def _allgather8_plan(v):
    def pieces(ins, outs, send, recv):
        x, y, c, chips = _place()
        me, sibling = (x, y, c), (x, y, 1 - c)

        def copy(k, block, to, src=None):
            px, py, pc = block
            slot = outs[0].at[4 * px + 2 * py + pc]
            return _remote(slot if src is None else src, slot, send.at[k], recv.at[k], to)

        first = [copy(0, me, sibling, src=ins[0])] + [copy(1 + j, me, (*chip, c), src=ins[0])
                                                      for j, chip in enumerate(chips)]
        passed = [copy(4 + j, (*chip, c), sibling) for j, chip in enumerate(chips)]
        arrivals = [copy(1 + j, (*chip, c), me) for j, chip in enumerate(chips)]
        late = [copy(0, sibling, me)] + [copy(4 + j, (*chip, 1 - c), me) for j, chip in enumerate(chips)]
        return first, passed, arrivals, late

    def start(ins, outs, sems):
        for cp in pieces(ins, outs, *sems[0])[0]:
            cp.start()

    def mid(ins, outs, sems):
        _, passed, arrivals, _ = pieces(ins, outs, *sems[0])
        for got, fwd in zip(arrivals, passed):
            got.wait_recv()
            fwd.start()

    def finish(ins, outs, sems):
        first, passed, _, late = pieces(ins, outs, *sems[0])
        for got in late:
            got.wait_recv()
        for cp in first + passed:
            cp.wait_send()

    return _Plan([v], [_sds((8,) + v.shape, v.dtype)], [(7,)], start, finish, mid)


def _own_block(gathered, v):
    me = 4 * lax.axis_index("x") + 2 * lax.axis_index("y") + lax.axis_index("c")
    return lax.dynamic_update_index_in_dim(gathered, v, me, 0)


def _sum_devices(allv, name):
    _, r, _ = allv.shape

    def body(a_ref, o_ref):
        sm = a_ref[0]
        for d in range(1, 8):
            sm = sm + a_ref[d]
        o_ref[...] = sm

    return pl.pallas_call(
        body, name=name, grid=(1,), in_specs=[_full_spec((8, r, 128))], out_specs=_full_spec((r, 128)),
        out_shape=_sds((r, 128)), compiler_params=_cp("arbitrary"),
    )(allv)


def _pack(arrs):
    flat = jnp.concatenate([a.reshape(-1) for a in arrs])
    return jnp.pad(flat, (0, (-flat.shape[0]) % 1024)).reshape(-1, 128)


def _unpack(packed, shapes):
    flat, outs, off = packed.reshape(-1), [], 0
    for shp in shapes:
        n = math.prod(shp)
        outs.append(flat[off:off + n].reshape(shp))
        off += n
    return outs


BIG = ("w_in", "w_out", "w_up", "w_down")
CONV = ("conv_ssm_w", "conv_lru_w")
WEIGHTS = ("norm_mix_pre", "w_in", "conv_ssm_w", "conv_ssm_b", "dt_bias", "a_log", "d_skip", "ssm_norm", "conv_lru_w",
           "conv_lru_b", "lru_wa", "lru_ba", "lru_wx", "lru_bx", "lru_lambda", "w_out", "norm_mix_post",
           "norm_mlp_pre", "w_up", "w_down", "norm_mlp_post")
SMALL = tuple(n for n in WEIGHTS if n not in BIG and n not in CONV)
EARLY = ("w_down", "w_up", "w_out")


def _cat_cols(g):
    return jnp.concatenate([g[k] for k in range(NCHIP)], axis=1)


class _Dist:
    def __init__(self, shards, first, cidx, kidx):
        self.shards, self.first, self.cidx, self.kidx = shards, first, cidx, kidx
        self.halves = {}

    def early_grads(self, w_down, w_up, w_out):
        self.shard_major = [w_down.reshape(NCHIP, D, D), w_up, w_out.reshape(NCHIP, D // NCHIP, D)]

    def late_grads(self, grads, loss):
        g_in = grads["w_in_t"][None]
        got, = _run_plan(_pair_exchange_plan([g_in], axis=2), "grad_pair_exchange_w_in")
        p_all, pc_all = _pair_add(g_in, got, self.cidx, "grad_pair_add_w_in", axis=2)
        self.p_in = lax.dynamic_slice_in_dim(p_all[0], self.kidx[0] * W_IN_SHARD, W_IN_SHARD, axis=0)[None]
        self.pc_in = pc_all.reshape(NCHIP, W_IN_SHARD, D // 2)
        self.small_names = [n for n in SMALL + CONV if n != "norm_mix_pre"]
        self.small_shapes = [grads[n].shape for n in self.small_names] + [(1,)]
        self.packed_small = _pack([grads[n] for n in self.small_names] + [loss.reshape(1)])

    def plan(self, key):
        if key == "norm_u":
            return _gather_plan(self.first[:1], self.first[1:], axes=[1])
        if key == "proj5":
            return _gather_plan([self.shards["w_out"], self.shards["w_up"]])
        if key == "lru_fwd":
            return _gather_plan([self.shards["w_down"]])
        if key == "lru_bwd":
            return _pair_exchange_plan(self.shard_major)
        if key == "ssd_bwd":
            return _chip_exchange_plan([pc for _, pc in self.pair])
        if key == "conv_bwd":
            return _pair_swap_plan(self.mine)
        if key == "du_norm":
            return _merge_plans(_allgather8_plan(self.packed_small), _chip_exchange_plan([self.pc_in]))
        return None

    def done(self, key, got, p):
        if key == "norm_u":
            g_in, g_cs, g_cl = _own_shards(got, self.first)
            w5, wxbc, wdt = _split_w_in_t(g_in.reshape(W_IN_COLS, D))
            p.update(w5=w5, wxbc=wxbc, wdt=wdt, conv_ssm_w=_cat_cols(g_cs), conv_lru_w=_cat_cols(g_cl))
        elif key == "proj5":
            g_out, g_up = _own_shards(got, [self.shards["w_out"], self.shards["w_up"]])
            p.update(w_out=g_out.reshape(D, D), w_up=_cat_cols(g_up))
        elif key == "lru_fwd":
            g_down, = _own_shards(got, [self.shards["w_down"]])
            p.update(w_down=g_down.reshape(DFF, D))
        elif key == "lru_bwd":
            self.pair = [_pair_add(gs, o, self.cidx, f"grad_pair_add_{n}")
                         for gs, o, n in zip(self.shard_major, got, EARLY)]
        elif key == "ssd_bwd":
            self.mine = [_shard_sum(pf, o, self.kidx, f"grad_shard_sum_{n}")
                         for (pf, _), o, n in zip(self.pair, got, EARLY)]
        elif key == "conv_bwd":
            self.halves = {n: (mine, other) for n, mine, other in zip(EARLY, self.mine, got)}
        elif key == "du_norm":
            self.all_small, self.from_chips_in = got


def kernel(x, norm_mix_pre, w_in, conv_ssm_w, conv_ssm_b, dt_bias, a_log, d_skip, ssm_norm, conv_lru_w, conv_lru_b, lru_wa, lru_ba, lru_wx, lru_bx, lru_lambda, w_out, norm_mix_post, norm_mlp_pre, w_up, w_down, norm_mlp_post, loss_target, m_norm_mix_pre, m_w_in, m_conv_ssm_w, m_conv_ssm_b, m_dt_bias, m_a_log, m_d_skip, m_ssm_norm, m_conv_lru_w, m_conv_lru_b, m_lru_wa, m_lru_ba, m_lru_wx, m_lru_bx, m_lru_lambda, m_w_out, m_norm_mix_post, m_norm_mlp_pre, m_w_up, m_w_down, m_norm_mlp_post, v_norm_mix_pre, v_w_in, v_conv_ssm_w, v_conv_ssm_b, v_dt_bias, v_a_log, v_d_skip, v_ssm_norm, v_conv_lru_w, v_conv_lru_b, v_lru_wa, v_lru_ba, v_lru_wx, v_lru_bx, v_lru_lambda, v_w_out, v_norm_mix_post, v_norm_mlp_pre, v_w_up, v_w_down, v_norm_mlp_post):
    args = locals()
    w = {n: args[n][0] for n in WEIGHTS}
    m = {n: args["m_" + n][0] for n in WEIGHTS}
    v = {n: args["v_" + n][0] for n in WEIGHTS}
    cidx = lax.axis_index("c").astype(jnp.int32).reshape(1)
    kchip = 2 * lax.axis_index("x") + lax.axis_index("y")
    to_t = lambda a: jnp.transpose(a, (2, 0, 1)).reshape(W_IN_SHARD, D)
    from_t = lambda a: jnp.transpose(a.reshape(W_IN_SHARD, 1, D), (1, 2, 0))
    shards = {n: (to_t(w_in) if n == "w_in" else w[n]).astype(MXU) for n in BIG}
    dist = _Dist(shards, [shards["w_in"], w["conv_ssm_w"], w["conv_lru_w"]], cidx, kchip.astype(jnp.int32).reshape(1))
    p = {n: (w[n].reshape(1, -1) if w[n].ndim == 1 else w[n]) for n in SMALL}

    _, grad_x, g = _local_step(x[0], loss_target[0], p, dist)

    half_in = _shard_sum(dist.p_in, dist.from_chips_in, jnp.zeros((1,), jnp.int32), "grad_shard_sum_w_in", axis=2)
    packed_g1 = _pack([g["norm_mix_pre"]])
    all_g1, other_in = _run_plan(_merge_plans(_allgather8_plan(packed_g1), _pair_swap_plan([half_in])),
                                 "grad_pair_swap_w_in")
    halves = dist.halves

    reduced = {}
    *summed, loss = _unpack(_sum_devices(_own_block(dist.all_small, dist.packed_small), "small_sum"),
                            dist.small_shapes)
    loss = loss.reshape(())
    g1, = _unpack(_sum_devices(_own_block(all_g1, packed_g1), "small_sum_norm_mix_pre"), [g["norm_mix_pre"].shape])
    for n, s in zip(dist.small_names + ["norm_mix_pre"], summed + [g1]):
        if n in CONV:
            width = w[n].shape[1]
            reduced[n] = lax.dynamic_slice_in_dim(s, kchip * width, width, axis=1)
        else:
            reduced[n] = s.reshape(w[n].shape)

    delta, new_m, new_v = {}, {}, {}
    for n in EARLY:
        mine, other = halves[n]
        reduced[n], delta[n], new_m[n], new_v[n] = _adamw_halves(w[n], mine, other, m[n], v[n], cidx, f"adamw_{n}")
    outs_t = _adamw_halves(to_t(w_in), half_in, other_in, to_t(m_w_in), to_t(v_w_in), cidx, "adamw_w_in", axis=1)
    for d, o in zip((reduced, delta, new_m, new_v), outs_t):
        d["w_in"] = from_t(o)[0]
    for n in CONV:
        delta[n], new_m[n], new_v[n] = _adamw(w[n], reduced[n], m[n], v[n], f"adamw_{n}")
    shapes = [w[n].shape for n in SMALL]
    packed = [_pack([d[n] for n in SMALL]) for d in (w, reduced, m, v)]
    for d, out in zip((delta, new_m, new_v), _adamw(*packed, "adamw_small")):
        d.update(zip(SMALL, _unpack(out, shapes)))

    lead = lambda d: [d[n][None] for n in WEIGHTS]
    return (loss, grad_x[None], *lead(reduced), *lead(delta), *lead(new_m), *lead(new_v))
```

```python
import functools
import math

import jax
import jax.numpy as jnp
from jax import lax
from jax.experimental import pallas as pl
from jax.experimental.pallas import tpu as pltpu
from jax.experimental.pallas import tpu_sc as plsc

F32 = jnp.float32
BF16 = jnp.bfloat16
MXU = BF16

D = 1024
DFF = 4096
NH = 16
HP = 64
NG = 2
NS = 128
CH = 128
XBC = D + 2 * NG * NS
GW = D // NG
LRU_C = 8.0
EPS = 1e-6
NCHIP = 4
W_IN_COLS = 6672
W_IN_SHARD = W_IN_COLS // NCHIP

ADAM_LR = 0.001
ADAM_B1 = 0.9
ADAM_B2 = 0.999
ADAM_EPS = 1e-08
ADAM_WD = 0.01
ADAM_STEP = 10

VMEM_LIMIT = 56 * 1024 * 1024
TK_GRAD = 2048
MID_AT = 0.7
ROWS_FUSED = 512
ROWS_CONV = 512
COL_G, COL_GA, COL_GB, COL_Z, COL_XL = range(5)
MESH = pl.DeviceIdType.MESH


def _cp(*sem):
    return pltpu.CompilerParams(dimension_semantics=sem, vmem_limit_bytes=VMEM_LIMIT)


def _dot(a, b, ca=1, cb=0, prec=None):
    return lax.dot_general(a, b, (((ca,), (cb,)), ((), ())), precision=prec, preferred_element_type=F32)


def _mdot(a, b, ca=1, cb=0):
    return _dot(a.astype(MXU), b.astype(MXU), ca, cb)


def _bf16_parts(v, n):
    parts = []
    for i in range(n):
        p = v.astype(BF16)
        parts.append(p)
        if i < n - 1:
            v = v - p.astype(F32)
    return parts


def _xdot(a, b, passes, split_b=False):
    if split_b:
        a16 = a.astype(BF16)
        terms = [_dot(a16, p) for p in _bf16_parts(b, passes)]
    else:
        b16 = b.astype(BF16)
        terms = [_dot(p, b16) for p in _bf16_parts(a, passes)]
    return functools.reduce(lambda u, v: u + v, terms)


def _sig(x):
    return 0.5 * jnp.tanh(0.5 * x) + 0.5


def _silu(x):
    return x * _sig(x)


def _dsilu(x):
    s = _sig(x)
    return s * (1.0 + x * (1.0 - s))


def _softplus(x):
    e = jnp.exp(-jnp.abs(x))
    return jnp.maximum(x, 0.0) + jnp.where(e < 1e-4, e * (1.0 - 0.5 * e), jnp.log(1.0 + e))


_GELU_C = math.sqrt(2.0 / math.pi)


def _gelu(x):
    t = jnp.tanh(_GELU_C * (x + 0.044715 * x * x * x))
    return 0.5 * x * (1.0 + t)


def _gelu_and_grad(x):
    x2 = x * x
    t = jnp.tanh(_GELU_C * (x + 0.044715 * x * x2))
    half = 0.5 * (1.0 + t)
    return x * half, half + 0.5 * x * (1.0 - t * t) * _GELU_C * (1.0 + 3.0 * 0.044715 * x2)


def _one_minus_sq(a, la):
    x = 2.0 * la
    series = -x * (1.0 + x * (0.5 + x * (1.0 / 6.0)))
    return jnp.where(x > -0.01, series, 1.0 - a * a)


def _rms(x):
    return lax.rsqrt(jnp.mean(x * x, axis=-1, keepdims=True) + EPS)


def _rms_bwd(x, r, g, dy):
    xn = x * r
    dxh = dy * g
    m = jnp.mean(dxh * xn, axis=-1, keepdims=True)
    return r * (dxh - xn * m), jnp.sum(dy * xn, axis=0, keepdims=True)


def _row_spec(t, c, col=0):
    return pl.BlockSpec((t, c), lambda i: (i, col))


def _rev_spec(t, c, n, col=0):
    return pl.BlockSpec((t, c), lambda i: (n - 1 - i, col))


def _full_spec(shape, once=False):
    nd = len(shape)
    if once:
        return pl.BlockSpec(shape, lambda *_: (0,) * nd, pipeline_mode=pl.Buffered(1))
    return pl.BlockSpec(shape, lambda *_: (0,) * nd)


def _sds(shape, dtype=F32):
    return jax.ShapeDtypeStruct(shape, dtype)


ANY = pl.BlockSpec(memory_space=pl.ANY)


class _Plan:
    def __init__(self, ins, outs, sems, start, finish, mid=None):
        self.ins, self.outs, self.sems = list(ins), list(outs), list(sems)
        self.start, self.finish, self.mid = start, finish, mid or (lambda i, o, s: None)


def _merge_plans(*plans):
    def each(fn_name, ins, outs, sems):
        i = o = s = 0
        for p in plans:
            getattr(p, fn_name)(ins[i:i + len(p.ins)], outs[o:o + len(p.outs)], sems[s:s + len(p.sems)])
            i, o, s = i + len(p.ins), o + len(p.outs), s + len(p.sems)

    return _Plan([a for p in plans for a in p.ins], [a for p in plans for a in p.outs],
                 [a for p in plans for a in p.sems], functools.partial(each, "start"),
                 functools.partial(each, "finish"), functools.partial(each, "mid"))


def _pcall(body, args, *, name, grid, in_specs, out_specs, out_shape, sem, scratch_shapes=(), plan=None):
    single = not isinstance(out_shape, (list, tuple))
    out_specs = [out_specs] if single else list(out_specs)
    out_shape = [out_shape] if single else list(out_shape)
    if plan is None:
        outs = pl.pallas_call(body, name=name, grid=grid, in_specs=list(in_specs), out_specs=out_specs,
                              out_shape=out_shape, scratch_shapes=list(scratch_shapes),
                              compiler_params=_cp(*sem))(*args)
        return outs[0] if single else outs
    n_in, n_out, n_sc, ni, no = len(in_specs), len(out_shape), len(scratch_shapes), len(plan.ins), len(plan.outs)

    def hosted(*refs):
        b0 = n_in + ni
        b1 = b0 + n_out + no
        sem_refs = refs[b1 + n_sc:]
        sems = [(sem_refs[2 * q], sem_refs[2 * q + 1]) for q in range(len(plan.sems))]
        step = functools.reduce(lambda lin, ig: lin * ig[1] + ig[0],
                                [(pl.program_id(d), g) for d, g in enumerate(grid)], 0)
        total = math.prod(grid)

        @pl.when(step == 0)
        def _():
            plan.start(refs[n_in:b0], refs[b0 + n_out:b1], sems)

        body(*refs[:n_in], *refs[b0:b0 + n_out], *refs[b1:b1 + n_sc])

        @pl.when(step == min(int(MID_AT * total), total - 1))
        def _():
            plan.mid(refs[n_in:b0], refs[b0 + n_out:b1], sems)

        @pl.when(step == total - 1)
        def _():
            plan.finish(refs[n_in:b0], refs[b0 + n_out:b1], sems)

    dma = [pltpu.SemaphoreType.DMA(shape) for shape in plan.sems for _ in range(2)]
    outs = pl.pallas_call(hosted, name=name, grid=grid, in_specs=list(in_specs) + [ANY] * ni,
                          out_specs=out_specs + [ANY] * no, out_shape=out_shape + plan.outs,
                          scratch_shapes=list(scratch_shapes) + dma,
                          compiler_params=_cp(*("arbitrary",) * len(grid)))(*args, *plan.ins)
    return (outs[0] if single else outs[:n_out]), outs[n_out:]


def _run_plan(plan, name):
    ni, no = len(plan.ins), len(plan.outs)

    def body(*refs):
        sem_refs = refs[ni + no:]
        sems = [(sem_refs[2 * q], sem_refs[2 * q + 1]) for q in range(len(plan.sems))]
        plan.start(refs[:ni], refs[ni:ni + no], sems)
        plan.mid(refs[:ni], refs[ni:ni + no], sems)
        plan.finish(refs[:ni], refs[ni:ni + no], sems)

    return pl.pallas_call(
        body, name=name, in_specs=[ANY] * ni, out_specs=[ANY] * no, out_shape=plan.outs,
        scratch_shapes=[pltpu.SemaphoreType.DMA(shape) for shape in plan.sems for _ in range(2)],
    )(*plan.ins)


def _matmul(a, b, *, name, ta=False, tb=False, tm=512, tn=1024, tk=1024, out_dtype=F32, a_fn=None, epi=None,
            epi_args=(), plan=None, col_blocks_major=False):
    m, k = (a.shape[1], a.shape[0]) if ta else a.shape
    n = b.shape[0] if tb else b.shape[1]
    tm, tn, tk = min(tm, m), min(tn, n), min(tk, k)
    nk = k // tk
    a_spec = pl.BlockSpec((tk, tm), lambda i, j, kk: (kk, i)) if ta else pl.BlockSpec((tm, tk), lambda i, j, kk: (i, kk))
    b_spec = pl.BlockSpec((tn, tk), lambda i, j, kk: (j, kk)) if tb else pl.BlockSpec((tk, tn), lambda i, j, kk: (kk, j))
    e_specs = [pl.BlockSpec((tm, tn), lambda i, j, kk: (i, j)) for _ in epi_args]
    ne = len(epi_args)

    def body(a_ref, b_ref, *rest):
        e_refs, o_ref = rest[:ne], rest[ne]
        av = a_ref[...]
        if a_fn is not None:
            av = a_fn(av)
        part = _mdot(av, b_ref[...], 0 if ta else 1, 1 if tb else 0)

        def finish(r):
            if epi is not None:
                r = epi(r, *[e[...] for e in e_refs])
            o_ref[...] = r.astype(o_ref.dtype)

        if nk == 1:
            finish(part)
            return
        acc_ref = rest[ne + 1]
        kk = pl.program_id(2)

        @pl.when(kk == 0)
        def _():
            acc_ref[...] = part

        @pl.when(jnp.logical_and(kk > 0, kk < nk - 1))
        def _():
            acc_ref[...] += part

        @pl.when(kk == nk - 1)
        def _():
            finish(acc_ref[...] + part)

    if col_blocks_major:
        out_spec = pl.BlockSpec((None, tm, tn), lambda i, j, kk: (j, i, 0))
        out_shape = _sds((n // tn, m, tn), out_dtype)
    else:
        out_spec = pl.BlockSpec((tm, tn), lambda i, j, kk: (i, j))
        out_shape = _sds((m, n), out_dtype)
    return _pcall(
        body, (a, b, *epi_args), name=name, grid=(m // tm, n // tn, nk),
        in_specs=[a_spec, b_spec] + e_specs, out_specs=out_spec, out_shape=out_shape,
        scratch_shapes=[pltpu.VMEM((tm, tn), F32)] if nk > 1 else [],
        sem=("parallel", "parallel", "arbitrary"), plan=plan)


def _relu2(p):
    p = jnp.maximum(p, jnp.zeros((), p.dtype))
    return p * p


def _norm_cast(x, g, name, plan=None):
    s = x.shape[0]
    t = min(512, s)

    def body(x_ref, g_ref, o_ref):
        xv = x_ref[...]
        o_ref[...] = (xv * _rms(xv) * g_ref[...]).astype(o_ref.dtype)

    return _pcall(body, (x, g), name=name, grid=(s // t,), in_specs=[_row_spec(t, D), _full_spec((1, D))],
                  out_specs=_row_spec(t, D), out_shape=_sds((s, D), MXU), sem=("parallel",), plan=plan)


def _conv_fwd(xbc_raw, proj5, dt_raw, cw_s, cb_s, cw_l, cb_l, dt_bias):
    s = xbc_raw.shape[0]
    t = min(ROWS_CONV, s)

    def body(xs_ref, xl_ref, dtr_ref, cws_ref, cbs_ref, cwl_ref, cbl_ref, dtb_ref, xc_ref, dsl_ref, xr_ref, dt_ref,
             bs_ref, bl_ref):
        @pl.when(pl.program_id(0) == 0)
        def _():
            bs_ref[0:8, :] = jnp.zeros((8, XBC), F32)
            bl_ref[0:8, :] = jnp.zeros((8, D), F32)

        bs_ref[8:t + 8, :] = xs_ref[...]
        bl_ref[8:t + 8, :] = xl_ref[...].astype(F32)

        def conv(buf, w_ref, b_ref):
            acc = b_ref[...] + w_ref[3:4, :] * buf[8:t + 8, :]
            for k in (1, 2, 3):
                acc = acc + w_ref[3 - k:4 - k, :] * buf[8 - k:t + 8 - k, :]
            return acc

        pre = conv(bs_ref, cws_ref, cbs_ref)
        sg = _sig(pre)
        xc_ref[...] = pre * sg
        dsl_ref[...] = (sg * (1.0 + pre * (1.0 - sg))).astype(dsl_ref.dtype)
        xr_ref[...] = conv(bl_ref, cwl_ref, cbl_ref)
        dt_ref[...] = _softplus(dtr_ref[...] + dtb_ref[...])
        bs_ref[0:8, :] = bs_ref[t:t + 8, :]
        bl_ref[0:8, :] = bl_ref[t:t + 8, :]

    return pl.pallas_call(
        body, name="conv_fwd", grid=(s // t,),
        in_specs=[_row_spec(t, XBC), _row_spec(t, D, COL_XL), _row_spec(t, 128), _full_spec((4, XBC)),
                  _full_spec((1, XBC)), _full_spec((4, D)), _full_spec((1, D)), _full_spec((1, 128))],
        out_specs=[_row_spec(t, XBC), _row_spec(t, XBC), _row_spec(t, D), _row_spec(t, 128)],
        out_shape=[_sds((s, XBC)), _sds((s, XBC), BF16), _sds((s, D)), _sds((s, 128))],
        scratch_shapes=[pltpu.VMEM((t + 8, XBC), F32), pltpu.VMEM((t + 8, D), F32)],
        compiler_params=_cp("arbitrary"),
    )(xbc_raw, proj5, dt_raw, cw_s, cb_s, cw_l, cb_l, dt_bias)


def _ssd_chunk_setup(dt_ref, alog_ref, e_ref, at_ref, dtt_ref):
    lane = lax.broadcasted_iota(jnp.int32, (CH, 128), 1)
    row = lax.broadcasted_iota(jnp.int32, (CH, 128), 0)
    lane1 = lax.broadcasted_iota(jnp.int32, (1, 128), 1)
    a = jnp.where(lane1 < NH, -jnp.exp(alog_ref[...]), 0.0)
    dtv = dt_ref[...]
    adt = dtv * a
    tril = row >= lane
    acum = _xdot(tril.astype(F32), adt, 3, split_b=True)
    alast = jnp.sum(adt, axis=0, keepdims=True)
    at_ref[...] = acum.T
    dtt_ref[...] = dtv.T
    e = e_ref[...]
    ea_x = _xdot(jnp.exp(acum), e, 2)
    ws = jnp.exp(alast - acum) * dtv
    ws_x = _xdot(ws, e, 2)
    eal = jnp.exp(alast)
    eal_x = jnp.max(_xdot(jnp.broadcast_to(eal, (8, 128)), e, 3), axis=0, keepdims=True)
    return dict(lane=lane, row=row, tril=tril, a=a, dtv=dtv, acum=acum, alast=alast, ea_x=ea_x, ws=ws, ws_x=ws_x,
                eal=eal, eal_x=eal_x)


def _head_decay(cs, at_ref, dtt_ref, h):
    col = jnp.sum(jnp.where(cs["lane"] == h, cs["acum"], 0.0), axis=1, keepdims=True)
    ld = jnp.where(cs["tril"], jnp.exp(jnp.minimum(col - at_ref[h:h + 1, :], 0.0)), 0.0)
    return ld, dtt_ref[h:h + 1, :]


def _ssd_fwd(xbc_c, dt, proj5, a_log, dskip_x, ssm_norm, expand):
    s = xbc_c.shape[0]
    nc = s // CH

    def body(xc_ref, dt_ref, z_ref, alog_ref, dsk_ref, ng_ref, e_ref, y_ref, ya_ref, st_ref, h_ref, at_ref, dtt_ref,
             yd_ref):
        @pl.when(pl.program_id(0) == 0)
        def _():
            h_ref[...] = jnp.zeros_like(h_ref)

        cs = _ssd_chunk_setup(dt_ref, alog_ref, e_ref, at_ref, dtt_ref)
        lane = cs["lane"]
        for g in range(NG):
            gs = slice(GW * g, GW * (g + 1))
            bg = xc_ref[:, D + NS * g:D + NS * (g + 1)]
            cg = xc_ref[:, D + NG * NS + NS * g:D + NG * NS + NS * (g + 1)]
            cb = _mdot(cg, bg, 1, 1)
            for j in range(4 * g, 4 * g + 4):
                ps = slice(128 * j, 128 * (j + 1))
                xp = xc_ref[:, ps]
                acc = jnp.zeros((CH, 128), F32)
                for hf in range(2):
                    ld, rowdt = _head_decay(cs, at_ref, dtt_ref, 2 * j + hf)
                    hm = (lane >= HP) if hf else (lane < HP)
                    acc = acc + _mdot(cb * ld * rowdt, jnp.where(hm, xp, 0.0))
                yd_ref[:, ps] = acc
            hg = h_ref[:, gs]
            yd_ref[:, gs] += _mdot(cg, hg) * cs["ea_x"][:, gs]
            st = _mdot(bg, xc_ref[:, gs] * cs["ws_x"][:, gs], 0, 0)
            st_ref[0, :, gs] = hg
            h_ref[:, gs] = cs["eal_x"][:, gs] * hg + st
        y = yd_ref[...] + dsk_ref[...] * xc_ref[:, 0:D]
        y_ref[...] = y
        yg = y * _silu(z_ref[...].astype(F32))
        for g in range(NG):
            gs = slice(GW * g, GW * (g + 1))
            seg = yg[:, gs]
            ya_ref[:, gs] = seg * _rms(seg) * ng_ref[:, gs]

    return pl.pallas_call(
        body, name="ssd_fwd", grid=(nc,),
        in_specs=[_row_spec(CH, XBC), _row_spec(CH, 128), _row_spec(CH, D, COL_Z), _full_spec((1, 128)),
                  _full_spec((1, D)), _full_spec((1, D)), _full_spec((128, D))],
        out_specs=[_row_spec(CH, D), _row_spec(CH, D), pl.BlockSpec((1, NS, D), lambda i: (i, 0, 0))],
        out_shape=[_sds((s, D)), _sds((s, D)), _sds((nc, NS, D))],
        scratch_shapes=[pltpu.VMEM((NS, D), F32), pltpu.VMEM((128, 128), F32), pltpu.VMEM((128, 128), F32),
                        pltpu.VMEM((CH, D), F32)],
        compiler_params=_cp("arbitrary"),
    )(xbc_c, dt, proj5, a_log, dskip_x, ssm_norm, expand)


def _lru_gates(xr, wab_ref, ba_ref, bx_ref, lam_ref):
    pre = _mdot(xr, wab_ref[...])
    gr = _sig(pre[:, 0:D] + ba_ref[...])
    gi = _sig(pre[:, D:2 * D] + bx_ref[...])
    sp = _softplus(-lam_ref[...])
    la = -LRU_C * gr * sp
    a = jnp.exp(la)
    oms = _one_minus_sq(a, la)
    inv_mult = lax.rsqrt(oms)
    return gr, gi, sp, a, oms * inv_mult, inv_mult


def _blocked_scan(a, u, carry_ref, a_ref, u_ref, c_ref, out_ref, reverse):
    t = a.shape[0]
    ns = t // 8

    def combine(av, uv, idx, n, sh):
        m = (idx < n - sh) if reverse else (idx >= sh)
        by = n - sh if reverse else sh
        return jnp.where(m, av * pltpu.roll(av, by, 0), av), jnp.where(m, uv + av * pltpu.roll(uv, by, 0), uv)

    row = lax.broadcasted_iota(jnp.int32, (t, D), 0)
    rin = jnp.bitwise_and(row, 7)
    for sh in (1, 2, 4):
        m = (rin < 8 - sh) if reverse else (rin >= sh)
        by = t - sh if reverse else sh
        a, u = jnp.where(m, a * pltpu.roll(a, by, 0), a), jnp.where(m, u + a * pltpu.roll(u, by, 0), u)
    a_ref[...] = a
    u_ref[...] = u
    edge = 0 if reverse else 7
    for j in range(ns):
        c_ref[j:j + 1, :] = a_ref[8 * j + edge:8 * j + edge + 1, :]
    at = c_ref[...]
    for j in range(ns):
        c_ref[j:j + 1, :] = u_ref[8 * j + edge:8 * j + edge + 1, :]
    ut = c_ref[...]
    srow = lax.broadcasted_iota(jnp.int32, (ns, D), 0)
    sh = 1
    while sh < ns:
        at, ut = combine(at, ut, srow, ns, sh)
        sh *= 2
    cv = carry_ref[0:1, :]
    ends = ut + at * cv
    last = 0 if reverse else ns - 1
    first = ns - 1 if reverse else 0
    c_ref[...] = jnp.where(srow == first, cv, pltpu.roll(ends, first if reverse else 1, 0))
    carry_ref[0:1, :] = jnp.sum(jnp.where(srow == last, ends, 0.0), axis=0, keepdims=True)
    for j in range(ns):
        sl = slice(8 * j, 8 * j + 8)
        out_ref[sl, :] = u_ref[sl, :] + a_ref[sl, :] * c_ref[j:j + 1, :]


def _lru_fwd(xr, proj5, ya, wab, ba, bx, lam, plan=None):
    s = xr.shape[0]
    t = min(256, s)

    def body(xr_ref, g_ref, ga_ref, gb_ref, ya_ref, wab_ref, ba_ref, bx_ref, lam_ref, h_ref, mg_ref, gr_ref,
             gi_ref, ao_ref, mo_ref, hc_ref, a_ref, u_ref, c_ref):
        @pl.when(pl.program_id(0) == 0)
        def _():
            hc_ref[...] = jnp.zeros_like(hc_ref)

        xrv = xr_ref[...]
        gr, gi, _, a, mult, _ = _lru_gates(xrv, wab_ref, ba_ref, bx_ref, lam_ref)
        gr_ref[...], gi_ref[...], ao_ref[...], mo_ref[...] = gr, gi, a, mult
        _blocked_scan(a, mult * gi * xrv, hc_ref, a_ref, u_ref, c_ref, h_ref, reverse=False)
        yb = h_ref[...] * _gelu(g_ref[...].astype(F32))
        mg_ref[...] = (_sig(ga_ref[...].astype(F32)) * ya_ref[...]
                       + _sig(gb_ref[...].astype(F32)) * yb).astype(mg_ref.dtype)

    return _pcall(
        body, (xr, proj5, proj5, proj5, ya, wab, ba, bx, lam), name="lru_fwd", grid=(s // t,),
        in_specs=[_row_spec(t, D), _row_spec(t, D, COL_G), _row_spec(t, D, COL_GA), _row_spec(t, D, COL_GB),
                  _row_spec(t, D), _full_spec((D, 2 * D), once=True), _full_spec((1, D)), _full_spec((1, D)),
                  _full_spec((1, D))],
        out_specs=[_row_spec(t, D)] * 6,
        out_shape=[_sds((s, D)), _sds((s, D), MXU)] + [_sds((s, D))] * 4,
        scratch_shapes=[pltpu.VMEM((8, D), F32), pltpu.VMEM((t, D), F32), pltpu.VMEM((t, D), F32),
                        pltpu.VMEM((t // 8, D), F32)],
        sem=("arbitrary",), plan=plan)


def _out_up_proj(merged, w_out, x, g2, g3, w_up):
    s = x.shape[0]
    t = min(ROWS_FUSED, s)

    def body(mg_ref, w_ref, x_ref, g2_ref, g3_ref, wu_ref, mix_ref, h1_ref, v_ref, pre_ref):
        mix = _mdot(mg_ref[...], w_ref[...])
        mix_ref[...] = mix
        h1 = x_ref[...] + mix * _rms(mix) * g2_ref[...]
        h1_ref[...] = h1
        v = (h1 * _rms(h1) * g3_ref[...]).astype(v_ref.dtype)
        v_ref[...] = v
        pre_ref[...] = _mdot(v, wu_ref[...]).astype(pre_ref.dtype)

    return pl.pallas_call(
        body, name="out_up_proj", grid=(s // t,),
        in_specs=[_row_spec(t, D), _full_spec((D, D), once=True), _row_spec(t, D), _full_spec((1, D)),
                  _full_spec((1, D)), _full_spec((D, DFF), once=True)],
        out_specs=[_row_spec(t, D), _row_spec(t, D), _row_spec(t, D), _row_spec(t, DFF)],
        out_shape=[_sds((s, D)), _sds((s, D)), _sds((s, D), MXU), _sds((s, DFF), MXU)],
        compiler_params=_cp("parallel"),
    )(merged, w_out, x, g2, g3, w_up)


def _down_loss(pre, w_down, h1, target, g4):
    s = pre.shape[0]
    t = min(ROWS_FUSED, s)

    def body(pre_ref, w_ref, h1_ref, tg_ref, g4_ref, dout_ref, dff_ref, loss_ref, dg4_ref):
        @pl.when(pl.program_id(0) == 0)
        def _():
            loss_ref[...] = jnp.zeros_like(loss_ref)
            dg4_ref[...] = jnp.zeros_like(dg4_ref)

        ff = _mdot(_relu2(pre_ref[...]), w_ref[...])
        r4 = _rms(ff)
        g4v = g4_ref[...]
        diff = h1_ref[...] + ff * r4 * g4v - tg_ref[...]
        sq = jnp.sum(jnp.sum(diff * diff, axis=1, keepdims=True), axis=0, keepdims=True)
        loss_ref[...] += (0.5 / D) * sq
        dout = diff * (1.0 / D)
        dout_ref[...] = dout
        dff, dg = _rms_bwd(ff, r4, g4v, dout)
        dff_ref[...] = dff.astype(dff_ref.dtype)
        dg4_ref[...] += dg

    return pl.pallas_call(
        body, name="down_loss", grid=(s // t,),
        in_specs=[_row_spec(t, DFF), _full_spec((DFF, D), once=True), _row_spec(t, D), _row_spec(t, D),
                  _full_spec((1, D))],
        out_specs=[_row_spec(t, D), _row_spec(t, D), _full_spec((1, 128)), _full_spec((1, D))],
        out_shape=[_sds((s, D)), _sds((s, D), MXU), _sds((1, 128)), _sds((1, D))],
        compiler_params=_cp("arbitrary"),
    )(pre, w_down, h1, target, g4)


def _dv_norms(dpre, w_up, h1, mix, dout, g3, g2, w_out):
    s = h1.shape[0]
    t = min(ROWS_FUSED, s)

    def body(dp_ref, w_ref, h1_ref, mix_ref, dout_ref, g3_ref, g2_ref, wo_ref, dh1_ref, dmix_ref, dmg_ref, dg3_ref,
             dg2_ref):
        @pl.when(pl.program_id(0) == 0)
        def _():
            dg3_ref[...] = jnp.zeros_like(dg3_ref)
            dg2_ref[...] = jnp.zeros_like(dg2_ref)

        dv = _mdot(dp_ref[...], w_ref[...], 1, 1)
        h1 = h1_ref[...]
        dh1n, dg3 = _rms_bwd(h1, _rms(h1), g3_ref[...], dv)
        dh1 = dout_ref[...] + dh1n
        dh1_ref[...] = dh1
        mix = mix_ref[...]
        dmix, dg2 = _rms_bwd(mix, _rms(mix), g2_ref[...], dh1)
        dmix = dmix.astype(dmix_ref.dtype)
        dmix_ref[...] = dmix
        dmg_ref[...] = _mdot(dmix, wo_ref[...], 1, 1)
        dg3_ref[...] += dg3
        dg2_ref[...] += dg2

    return pl.pallas_call(
        body, name="dv_norms", grid=(s // t,),
        in_specs=[_row_spec(t, DFF), _full_spec((D, DFF), once=True), _row_spec(t, D), _row_spec(t, D),
                  _row_spec(t, D), _full_spec((1, D)), _full_spec((1, D)), _full_spec((D, D), once=True)],
        out_specs=[_row_spec(t, D), _row_spec(t, D), _row_spec(t, D), _full_spec((1, D)), _full_spec((1, D))],
        out_shape=[_sds((s, D)), _sds((s, D), MXU), _sds((s, D)), _sds((1, D)), _sds((1, D))],
        compiler_params=_cp("arbitrary"),
    )(dpre, w_up, h1, mix, dout, g3, g2, w_out)


def _lru_bwd(dmerged, ya, xr, h, proj5, gates, wab_t, lam, plan=None):
    s = xr.shape[0]
    t = min(128, s)
    n = s // t
    rs = functools.partial(_rev_spec, t, D, n)

    def body(dm_ref, ya_ref, xr_ref, h_ref, hp_ref, g_ref, ga_ref, gb_ref, gr_ref, gi_ref, a_ref, m_ref, wab_ref,
             lam_ref, dya_ref, d3_ref, dxr_ref, dp2_ref, dlam_ref, dba_ref, dbx_ref, gc_ref, af_ref, an_ref, us_ref,
             c_ref, gs_ref):
        i = pl.program_id(0)

        @pl.when(i == 0)
        def _():
            gc_ref[...] = jnp.zeros_like(gc_ref)
            af_ref[...] = jnp.zeros_like(af_ref)
            dlam_ref[...] = jnp.zeros_like(dlam_ref)
            dba_ref[...] = jnp.zeros_like(dba_ref)
            dbx_ref[...] = jnp.zeros_like(dbx_ref)

        xrv = xr_ref[...]
        gr, gi, a, mult = gr_ref[...], gi_ref[...], a_ref[...], m_ref[...]
        sp = _softplus(-lam_ref[...])
        inv_mult = 1.0 / mult
        hv = h_ref[...]
        dm = dm_ref[...]
        sa = _sig(ga_ref[...].astype(F32))
        sb = _sig(gb_ref[...].astype(F32))
        gel, dgel = _gelu_and_grad(g_ref[...].astype(F32))
        dya = dm * sa
        dya_ref[...] = dya
        dyb = dm * sb
        dybh = dyb * hv
        d3_ref[:, 0:D] = (dybh * dgel).astype(d3_ref.dtype)
        d3_ref[:, D:2 * D] = (dya * ya_ref[...] * (1.0 - sa)).astype(d3_ref.dtype)
        d3_ref[:, 2 * D:3 * D] = (dybh * gel * (1.0 - sb)).astype(d3_ref.dtype)
        row = lax.broadcasted_iota(jnp.int32, (t, D), 0)
        an = jnp.where(row == t - 1, af_ref[0:1, :], pltpu.roll(a, t - 1, 0))
        _blocked_scan(an, dyb * gel, gc_ref, an_ref, us_ref, c_ref, gs_ref, reverse=True)
        gfull = gs_ref[...]
        af_ref[0:1, :] = jnp.sum(jnp.where(row == 0, a, 0.0), axis=0, keepdims=True)
        hlast = jnp.where(i == n - 1, 0.0, hp_ref[7:8, :])
        hprev = jnp.where(row == 0, hlast, pltpu.roll(hv, 1, 0))
        gx = gfull * xrv
        dgi = gx * mult
        dla = a * (gfull * hprev - gx * gi * a * inv_mult)
        dgr = dla * (-LRU_C * sp)
        dsp = jnp.sum(dla * (-LRU_C * gr), axis=0, keepdims=True)
        dlam_ref[...] += dsp * (-_sig(-lam_ref[...]))
        dpr = dgr * gr * (1.0 - gr)
        dpi = dgi * gi * (1.0 - gi)
        dp2_ref[:, 0:D] = dpr.astype(dp2_ref.dtype)
        dp2_ref[:, D:2 * D] = dpi.astype(dp2_ref.dtype)
        dba_ref[...] += jnp.sum(dpr, axis=0, keepdims=True)
        dbx_ref[...] += jnp.sum(dpi, axis=0, keepdims=True)
        dxr_ref[...] = gfull * mult * gi + _mdot(dp2_ref[...], wab_ref[...])

    hp_spec = pl.BlockSpec((8, D), lambda i: (jnp.maximum((n - 1 - i) * (t // 8) - 1, 0), 0))
    wide = lambda c: pl.BlockSpec((t, c), lambda i: (n - 1 - i, 0))
    return _pcall(
        body, (dmerged, ya, xr, h, h, proj5, proj5, proj5, *gates, wab_t, lam), name="lru_bwd", grid=(n,),
        in_specs=[rs(), rs(), rs(), rs(), hp_spec, rs(COL_G), rs(COL_GA), rs(COL_GB), rs(), rs(), rs(), rs(),
                  _full_spec((2 * D, D), once=True), _full_spec((1, D))],
        out_specs=[rs(), wide(3 * D), rs(), wide(2 * D), _full_spec((1, D)), _full_spec((1, D)), _full_spec((1, D))],
        out_shape=[_sds((s, D)), _sds((s, 3 * D), MXU), _sds((s, D)), _sds((s, 2 * D), MXU), _sds((1, D)),
                   _sds((1, D)), _sds((1, D))],
        scratch_shapes=[pltpu.VMEM((8, D), F32), pltpu.VMEM((8, D), F32), pltpu.VMEM((t, D), F32),
                        pltpu.VMEM((t, D), F32), pltpu.VMEM((t // 8, D), F32), pltpu.VMEM((t, D), F32)],
        sem=("arbitrary",), plan=plan)


def _ssd_bwd(dya, y, proj5, xbc_c, dt, states, a_log, dskip_x, ssm_norm, expand, reduce_, plan=None):
    s = xbc_c.shape[0]
    nc = s // CH
    rv = functools.partial(_rev_spec, CH)

    def body(dya_ref, y_ref, z_ref, xc_ref, dt_ref, st_ref, alog_ref, dsk_ref, ng_ref, e_ref, et_ref, dz_ref,
             dxc_ref, ddt_ref, dng_ref, ddsk_ref, dalog_ref, dh_ref, at_ref, dtt_ref, dat_ref, ddtt_ref, dy_ref,
             yoffdy_ref, xbds_ref):
        @pl.when(pl.program_id(0) == 0)
        def _():
            dh_ref[...] = jnp.zeros_like(dh_ref)
            dng_ref[...] = jnp.zeros_like(dng_ref)
            ddsk_ref[...] = jnp.zeros_like(ddsk_ref)
            dalog_ref[...] = jnp.zeros_like(dalog_ref)

        cs = _ssd_chunk_setup(dt_ref, alog_ref, e_ref, at_ref, dtt_ref)
        lane, row = cs["lane"], cs["row"]
        et = et_ref[...]
        for g in range(NG):
            gs = slice(GW * g, GW * (g + 1))
            yv = y_ref[:, gs]
            zv = z_ref[:, gs].astype(F32)
            sz = _silu(zv)
            yg = yv * sz
            dyav = dya_ref[:, gs]
            dyg, dng = _rms_bwd(yg, _rms(yg), ng_ref[:, gs], dyav)
            dng_ref[:, gs] += dng
            dy_ref[:, gs] = dyg * sz
            dz_ref[:, gs] = (dyg * yv * _dsilu(zv)).astype(dz_ref.dtype)
        dyv = dy_ref[...]
        xs = xc_ref[:, 0:D]
        ddsk_ref[...] += jnp.sum(dyv * xs, axis=0, keepdims=True)
        dxc_ref[:, 0:D] = dyv * dsk_ref[...]
        dat_ref[...] = jnp.zeros_like(dat_ref)
        ddtt_ref[...] = jnp.zeros_like(ddtt_ref)
        hh = jnp.sum(dh_ref[...] * st_ref[0], axis=0, keepdims=True)
        deal = jnp.max(_xdot(jnp.broadcast_to(hh, (8, D)), et, 3), axis=0, keepdims=True)
        d_acum = jnp.zeros((CH, 128), F32)
        for g in range(NG):
            gs = slice(GW * g, GW * (g + 1))
            bs_ = slice(D + NS * g, D + NS * (g + 1))
            cs_ = slice(D + NG * NS + NS * g, D + NG * NS + NS * (g + 1))
            bg = xc_ref[:, bs_]
            cg = xc_ref[:, cs_]
            cb = _mdot(cg, bg, 1, 1)
            hg = st_ref[0, :, gs]
            dhg = dh_ref[:, gs]
            dyg_ = dy_ref[:, gs]
            xsg = xc_ref[:, gs]
            ea = cs["ea_x"][:, gs]
            wsx = cs["ws_x"][:, gs]
            dp = dyg_ * ea
            yoffdy_ref[:, gs] = dp * _mdot(cg, hg)
            dc = _mdot(dp, hg, 1, 1)
            dhprev = _mdot(cg, dp, 0, 0)
            bds = _mdot(bg, dhg)
            dxc_ref[:, gs] += wsx * bds
            xbds_ref[:, gs] = xsg * bds
            db = _mdot(xsg * wsx, dhg, 1, 1)
            dh_ref[:, gs] = dhprev + cs["eal_x"][:, gs] * dhg
            dcbs = jnp.zeros((CH, CH), F32)
            for j in range(4 * g, 4 * g + 4):
                ps = slice(128 * j, 128 * (j + 1))
                xp = xc_ref[:, ps]
                dyp = dy_ref[:, ps]
                dxacc = jnp.zeros((CH, 128), F32)
                for hf in range(2):
                    hd = 2 * j + hf
                    ld, rowdt = _head_decay(cs, at_ref, dtt_ref, hd)
                    hm = (lane >= HP) if hf else (lane < HP)
                    dym = jnp.where(hm, dyp, 0.0)
                    w = cb * ld * rowdt
                    dw = _mdot(dym, jnp.where(hm, xp, 0.0), 1, 1)
                    dxacc = dxacc + _mdot(w, dym, 0, 0)
                    nm = dw * w
                    ddtt_ref[hd:hd + 1, :] += jnp.sum(dw * cb * ld, axis=0, keepdims=True)
                    d_acum = d_acum + jnp.where(lane == hd, jnp.sum(nm, axis=1, keepdims=True), 0.0)
                    dat_ref[hd:hd + 1, :] -= jnp.sum(nm, axis=0, keepdims=True)
                    dcbs = dcbs + dw * ld * rowdt
                dxc_ref[:, ps] += dxacc
            dxc_ref[:, bs_] = db + _mdot(dcbs, cg, 0, 0)
            dxc_ref[:, cs_] = dc + _mdot(dcbs, bg)
        dws = _xdot(xbds_ref[...], et, 2)
        ws = cs["ws"]
        d_acum = d_acum - dws * ws + _xdot(yoffdy_ref[...], et, 2) + dat_ref[...].T
        d_alast = jnp.sum(dws * ws, axis=0, keepdims=True) + deal * cs["eal"]
        d_acum = d_acum + jnp.where(row == CH - 1, d_alast, 0.0)
        triu = row <= lane
        d_adt = _xdot(triu.astype(F32), d_acum, 3, split_b=True)
        ddt_ref[...] = dws * jnp.exp(cs["alast"] - cs["acum"]) + ddtt_ref[...].T + d_adt * cs["a"]
        dalog_ref[...] += jnp.sum(d_adt * cs["dtv"], axis=0, keepdims=True) * cs["a"]

    return _pcall(
        body, (dya, y, proj5, xbc_c, dt, states, a_log, dskip_x, ssm_norm, expand, reduce_), name="ssd_bwd",
        grid=(nc,),
        in_specs=[rv(D, nc), rv(D, nc), rv(D, nc, COL_Z), rv(XBC, nc), rv(128, nc),
                  pl.BlockSpec((1, NS, D), lambda i: (nc - 1 - i, 0, 0)), _full_spec((1, 128)), _full_spec((1, D)),
                  _full_spec((1, D)), _full_spec((128, D)), _full_spec((D, 128))],
        out_specs=[rv(D, nc), rv(XBC, nc), rv(128, nc), _full_spec((1, D)), _full_spec((1, D)),
                   _full_spec((1, 128))],
        out_shape=[_sds((s, D), MXU), _sds((s, XBC)), _sds((s, 128)), _sds((1, D)), _sds((1, D)), _sds((1, 128))],
        scratch_shapes=[pltpu.VMEM((NS, D), F32), pltpu.VMEM((128, 128), F32), pltpu.VMEM((128, 128), F32),
                        pltpu.VMEM((128, 128), F32), pltpu.VMEM((128, 128), F32), pltpu.VMEM((CH, D), F32),
                        pltpu.VMEM((CH, D), F32), pltpu.VMEM((CH, D), F32)],
        sem=("arbitrary",), plan=plan)


def _conv_bwd(dxbc_c, dsilu, dxr, ddt, xbc_raw, proj5, dt_raw, cw_s, cw_l, dt_bias, plan=None):
    s = xbc_raw.shape[0]
    t = min(ROWS_CONV // 2, s)
    n = s // t

    def body(dxc_ref, dsl_ref, dxr_ref, ddt_ref, xs_ref, xl_ref, dtr_ref, cws_ref, cwl_ref, dtb_ref, dxs_ref,
             dxl_ref, ddtr_ref, dcws_ref, dcbs_ref, dcwl_ref, dcbl_ref, ddtb_ref, ds_ref, dl_ref):
        @pl.when(pl.program_id(0) == 0)
        def _():
            ds_ref[t:t + 8, :] = jnp.zeros((8, XBC), F32)
            dl_ref[t:t + 8, :] = jnp.zeros((8, D), F32)
            for r in (dcws_ref, dcbs_ref, dcwl_ref, dcbl_ref, ddtb_ref):
                r[...] = jnp.zeros_like(r)

        ds_ref[0:t, :] = dxc_ref[...] * dsl_ref[...].astype(F32)
        dl_ref[0:t, :] = dxr_ref[...]

        def back(dbuf, x_ref, w_ref, dx_ref, dw_ref, db_ref):
            xv = x_ref[...].astype(F32)
            dpre = dbuf[0:t, :]
            dx = w_ref[3:4, :] * dpre
            dw_ref[3:4, :] += jnp.sum(dpre * xv, axis=0, keepdims=True)
            db_ref[...] += jnp.sum(dpre, axis=0, keepdims=True)
            for k in (1, 2, 3):
                ahead = dbuf[k:t + k, :]
                dx = dx + w_ref[3 - k:4 - k, :] * ahead
                dw_ref[3 - k:4 - k, :] += jnp.sum(ahead * xv, axis=0, keepdims=True)
            dx_ref[...] = dx.astype(dx_ref.dtype)
            dbuf[t:t + 8, :] = dbuf[0:8, :]

        back(ds_ref, xs_ref, cws_ref, dxs_ref, dcws_ref, dcbs_ref)
        back(dl_ref, xl_ref, cwl_ref, dxl_ref, dcwl_ref, dcbl_ref)
        ddtr = ddt_ref[...] * _sig(dtr_ref[...] + dtb_ref[...])
        ddtr_ref[...] = ddtr.astype(ddtr_ref.dtype)
        ddtb_ref[...] += jnp.sum(ddtr, axis=0, keepdims=True)

    rv = functools.partial(_rev_spec, t)
    return _pcall(
        body, (dxbc_c, dsilu, dxr, ddt, xbc_raw, proj5, dt_raw, cw_s, cw_l, dt_bias), name="conv_bwd", grid=(n,),
        in_specs=[rv(XBC, n), rv(XBC, n), rv(D, n), rv(128, n), rv(XBC, n), rv(D, n, COL_XL), rv(128, n),
                  _full_spec((4, XBC)), _full_spec((4, D)), _full_spec((1, 128))],
        out_specs=[rv(XBC, n), rv(D, n), rv(128, n), _full_spec((4, XBC)), _full_spec((1, XBC)), _full_spec((4, D)),
                   _full_spec((1, D)), _full_spec((1, 128))],
        out_shape=[_sds((s, XBC), MXU), _sds((s, D), MXU), _sds((s, 128), MXU), _sds((4, XBC)), _sds((1, XBC)),
                   _sds((4, D)), _sds((1, D)), _sds((1, 128))],
        scratch_shapes=[pltpu.VMEM((t + 8, XBC), F32), pltpu.VMEM((t + 8, D), F32)],
        sem=("arbitrary",), plan=plan)


def _du_norm(d3, dz, dxl, dxbc, ddtr, w5, wxbc, wdt, x, dh1, g1, plan=None):
    s = x.shape[0]
    t = min(ROWS_FUSED, s)

    def body(d3_ref, dz_ref, dxl_ref, dxbc_ref, ddtr_ref, w5_ref, wx_ref, wd_ref, x_ref, dh1_ref, g1_ref, dx_ref,
             dg1_ref):
        @pl.when(pl.program_id(0) == 0)
        def _():
            dg1_ref[...] = jnp.zeros_like(dg1_ref)

        du = (_mdot(dxbc_ref[...], wx_ref[...]) + _mdot(ddtr_ref[...], wd_ref[...])
              + _mdot(d3_ref[...], w5_ref[0:3 * D, :])
              + _mdot(dz_ref[...], w5_ref[COL_Z * D:(COL_Z + 1) * D, :])
              + _mdot(dxl_ref[...], w5_ref[COL_XL * D:(COL_XL + 1) * D, :]))
        xv = x_ref[...]
        dxn, dg1 = _rms_bwd(xv, _rms(xv), g1_ref[...], du)
        dx_ref[...] = dh1_ref[...] + dxn
        dg1_ref[...] += dg1

    return _pcall(
        body, (d3, dz, dxl, dxbc, ddtr, w5, wxbc, wdt, x, dh1, g1), name="du_norm", grid=(s // t,),
        in_specs=[_row_spec(t, 3 * D), _row_spec(t, D), _row_spec(t, D), _row_spec(t, XBC), _row_spec(t, 128),
                  _full_spec((5 * D, D), once=True), _full_spec((XBC, D), once=True),
                  _full_spec((128, D), once=True), _row_spec(t, D), _row_spec(t, D), _full_spec((1, D))],
        out_specs=[_row_spec(t, D), _full_spec((1, D))],
        out_shape=[_sds((s, D)), _sds((1, D))],
        sem=("arbitrary",), plan=plan)


def _adamw(w, g, m, v, name):
    r, c = w.shape
    t = r
    if r * c > 256 * 1024:
        t = next(cand for cand in (512, 256, 128, 64, 32, 16, 8) if r % cand == 0 and cand * c <= 512 * 1024)
    bc1 = 1.0 - ADAM_B1 ** ADAM_STEP
    bc2 = 1.0 - ADAM_B2 ** ADAM_STEP

    def body(w_ref, g_ref, m_ref, v_ref, d_ref, nm_ref, nv_ref):
        gv = g_ref[...]
        nm = ADAM_B1 * m_ref[...] + (1.0 - ADAM_B1) * gv
        nv = ADAM_B2 * v_ref[...] + (1.0 - ADAM_B2) * (gv * gv)
        nm_ref[...] = nm
        nv_ref[...] = nv
        d_ref[...] = -ADAM_LR * ((nm / bc1) / (jnp.sqrt(nv / bc2) + ADAM_EPS) + ADAM_WD * w_ref[...])

    spec = pl.BlockSpec((t, c), lambda i: (i, 0))
    return pl.pallas_call(
        body, name=name, grid=(r // t,), in_specs=[spec] * 4, out_specs=[spec] * 3,
        out_shape=[_sds((r, c))] * 3, compiler_params=_cp("parallel"),
    )(w, g, m, v)


def _half_blocks(shape, axis):
    r, c = shape
    if axis == 0:
        t = 256 if (r // 2) % 256 == 0 else 128
        nb = (r // 2) // t
        return (t, c), nb, (lambda i: (i, 0)), (lambda i: (i % nb, 0))
    nb = (c // 2) // 128
    return (r, 128), nb, (lambda i: (0, i)), (lambda i: (0, i % nb))


def _adamw_halves(w, g_mine, g_other, m, v, cidx, name, axis=0):
    r, c = w.shape
    blk, nb, whole, part = _half_blocks(w.shape, axis)
    bc1 = 1.0 - ADAM_B1 ** ADAM_STEP
    bc2 = 1.0 - ADAM_B2 ** ADAM_STEP

    def body(c_ref, w_ref, gm_ref, go_ref, m_ref, v_ref, g_ref, d_ref, nm_ref, nv_ref):
        mine = (pl.program_id(0) // nb) == c_ref[0]
        gv = jnp.where(mine, gm_ref[...], go_ref[...])
        g_ref[...] = gv
        nm = ADAM_B1 * m_ref[...] + (1.0 - ADAM_B1) * gv
        nv = ADAM_B2 * v_ref[...] + (1.0 - ADAM_B2) * (gv * gv)
        nm_ref[...] = nm
        nv_ref[...] = nv
        d_ref[...] = -ADAM_LR * ((nm / bc1) / (jnp.sqrt(nv / bc2) + ADAM_EPS) + ADAM_WD * w_ref[...])

    spec = pl.BlockSpec(blk, lambda i, c_ref: whole(i))
    half = pl.BlockSpec(blk, lambda i, c_ref: part(i))
    return pl.pallas_call(
        body, name=name,
        grid_spec=pltpu.PrefetchScalarGridSpec(num_scalar_prefetch=1, grid=(2 * nb,),
                                               in_specs=[spec, half, half, spec, spec], out_specs=[spec] * 4),
        out_shape=[_sds((r, c))] * 4, compiler_params=_cp("parallel"),
    )(cidx, w, g_mine, g_other, m, v)


SC_TILES = 32
SC_ROWS = 8


def _adamw_sc(w, g, m, v, name):
    r, c = w.shape
    rows = r // SC_TILES
    bc1 = 1.0 - ADAM_B1 ** ADAM_STEP
    bc2 = 1.0 - ADAM_B2 ** ADAM_STEP

    def body(w_hbm, g_hbm, m_hbm, v_hbm, d_hbm, nm_hbm, nv_hbm, wb, gb, mb, vb, db):
        tile = lax.axis_index("subcore") * 2 + lax.axis_index("core")

        @pl.loop(0, rows, step=SC_ROWS)
        def _(r0):
            sl = pl.ds(tile * rows + r0, SC_ROWS)
            pltpu.sync_copy(w_hbm.at[sl], wb)
            pltpu.sync_copy(g_hbm.at[sl], gb)
            pltpu.sync_copy(m_hbm.at[sl], mb)
            pltpu.sync_copy(v_hbm.at[sl], vb)
            for rr in range(SC_ROWS):
                @pl.loop(0, c, step=16)
                def _(i):
                    cs = pl.ds(i, 16)
                    gv = gb[rr, cs]
                    nm = ADAM_B1 * mb[rr, cs] + (1.0 - ADAM_B1) * gv
                    nv = ADAM_B2 * vb[rr, cs] + (1.0 - ADAM_B2) * (gv * gv)
                    mb[rr, cs] = nm
                    vb[rr, cs] = nv
                    db[rr, cs] = -ADAM_LR * ((nm / bc1) / (jnp.sqrt(nv / bc2) + ADAM_EPS) + ADAM_WD * wb[rr, cs])

            pltpu.sync_copy(db, d_hbm.at[sl])
            pltpu.sync_copy(mb, nm_hbm.at[sl])
            pltpu.sync_copy(vb, nv_hbm.at[sl])

    return pl.kernel(
        body, name=name, out_type=[_sds((r, c))] * 3,
        mesh=plsc.VectorSubcoreMesh(core_axis_name="core", subcore_axis_name="subcore"),
        scratch_types=[pltpu.VMEM((SC_ROWS, c), F32)] * 5,
    )(w, g, m, v)


def _block_diag(w):
    eye = jnp.eye(NH, dtype=w.dtype)
    return (w[:, :, None, :] * eye[:, None, :, None]).reshape(D, D)


def _diag_blocks(full):
    eye = jnp.eye(NH, dtype=full.dtype)
    return (full.reshape(NH, HP, NH, HP) * eye[:, None, :, None]).sum(axis=2)


def _pad_lanes(v, n=128):
    return jnp.pad(v, ((0, 0), (0, n - v.shape[1])))


def _local_step(x, target, p, dist=None):
    heads = jnp.arange(D, dtype=jnp.int32) // HP
    expand = (jnp.arange(128, dtype=jnp.int32)[:, None] == heads[None, :]).astype(F32)
    reduce_ = expand.T
    dskip_x = jnp.repeat(p["d_skip"], HP, axis=1)
    a_log = _pad_lanes(p["a_log"])
    dt_bias = _pad_lanes(p["dt_bias"])
    wab = jnp.concatenate([_block_diag(p["lru_wa"]), _block_diag(p["lru_wx"])], axis=1).astype(MXU)
    ba = p["lru_ba"].reshape(1, D)
    bx = p["lru_bx"].reshape(1, D)

    def hosted(key, fn):
        plan = dist.plan(key) if dist is not None else None
        if plan is None:
            return fn(plan=None)
        outs, got = fn(plan=plan)
        dist.done(key, got, p)
        return outs

    u = hosted("norm_u", functools.partial(_norm_cast, x, p["norm_mix_pre"], "norm_u"))
    w5, wxbc, wdt = p["w5"], p["wxbc"], p["wdt"]
    proj5 = hosted("proj5", functools.partial(_matmul, u, w5, name="proj5", tb=True, tm=1024, out_dtype=MXU))
    xbc_raw = _matmul(u, wxbc, name="proj_xbc", tb=True, tn=XBC)
    dt_raw = _matmul(u, wdt, name="proj_dt", tb=True)
    xbc_c, dsilu, xr, dt = _conv_fwd(xbc_raw, proj5, dt_raw, p["conv_ssm_w"], p["conv_ssm_b"], p["conv_lru_w"],
                                     p["conv_lru_b"], dt_bias)
    y, ya, states = _ssd_fwd(xbc_c, dt, proj5, a_log, dskip_x, p["ssm_norm"], expand)
    h, merged, *gates = hosted("lru_fwd", functools.partial(_lru_fwd, xr, proj5, ya, wab, ba, bx, p["lru_lambda"]))
    mix, h1, v, pre = _out_up_proj(merged, p["w_out"], x, p["norm_mix_post"], p["norm_mlp_pre"], p["w_up"])
    dout, dff, loss, dg4 = _down_loss(pre, p["w_down"], h1, target, p["norm_mlp_post"])

    dpre = _matmul(dff, p["w_down"], name="d_pre", tb=True, tm=1024, out_dtype=MXU,
                   epi=lambda r, pr: r * (2.0 * jnp.maximum(pr.astype(F32), 0.0)), epi_args=(pre,))
    g_w_down = _matmul(pre, dff, name="dw_down", ta=True, tm=1024, tn=1024, tk=TK_GRAD, a_fn=_relu2)
    dh1, dmix, dmerged, dg3, dg2 = _dv_norms(dpre, p["w_up"], h1, mix, dout, p["norm_mlp_pre"], p["norm_mix_post"],
                                             p["w_out"])
    g_w_up = _matmul(v, dpre, name="dw_up", ta=True, tm=1024, tn=1024, tk=TK_GRAD, col_blocks_major=True)
    g_w_out = _matmul(merged, dmix, name="dw_out", ta=True, tm=1024, tn=1024, tk=TK_GRAD)
    if dist is not None:
        dist.early_grads(w_down=g_w_down, w_up=g_w_up, w_out=g_w_out)
    dya, d3, dxr, dp2, dlam, dba, dbx = hosted("lru_bwd", functools.partial(
        _lru_bwd, dmerged, ya, xr, h, proj5, gates, wab.T, p["lru_lambda"]))
    g_wab = _matmul(xr, dp2, name="dw_lru", ta=True, tm=1024, tn=1024, tk=TK_GRAD)
    g_wa, g_wx = _diag_blocks(g_wab[:, :D]), _diag_blocks(g_wab[:, D:])
    dz, dxbc_c, ddt, dng, ddsk, dalog = hosted("ssd_bwd", functools.partial(
        _ssd_bwd, dya, y, proj5, xbc_c, dt, states, a_log, dskip_x, p["ssm_norm"], expand, reduce_))
    (dxbc, dxl, ddtr, dcws, dcbs, dcwl, dcbl, ddtb) = hosted("conv_bwd", functools.partial(
        _conv_bwd, dxbc_c, dsilu, dxr, ddt, xbc_raw, proj5, dt_raw, p["conv_ssm_w"], p["conv_lru_w"], dt_bias))
    gw3 = _matmul(d3, u, name="dw_in_lru", ta=True, tm=1024, tn=1024, tk=TK_GRAD)
    gwz = _matmul(dz, u, name="dw_in_z", ta=True, tm=1024, tn=1024, tk=TK_GRAD)
    gwxl = _matmul(dxl, u, name="dw_in_xl", ta=True, tm=1024, tn=1024, tk=TK_GRAD)
    gwxbc = _matmul(dxbc, u, name="dw_in_xbc", ta=True, tm=XBC, tn=1024, tk=TK_GRAD)
    gwdt = _matmul(ddtr, u, name="dw_in_dt", ta=True, tm=128, tn=1024, tk=TK_GRAD)
    g_w_in_t = jnp.concatenate([gwz, gwxbc, gwdt[:NH], gw3[:D], gwxl, gw3[D:2 * D], gw3[2 * D:]], axis=0)
    grads = {
        "w_in_t": g_w_in_t, "conv_ssm_w": dcws, "conv_ssm_b": dcbs, "dt_bias": ddtb[:, :NH],
        "a_log": dalog[:, :NH], "d_skip": ddsk.reshape(NH, HP).sum(axis=1)[None, :], "ssm_norm": dng,
        "conv_lru_w": dcwl, "conv_lru_b": dcbl, "lru_wa": g_wa, "lru_ba": dba.reshape(NH, HP), "lru_wx": g_wx,
        "lru_bx": dbx.reshape(NH, HP), "lru_lambda": dlam, "w_out": g_w_out, "norm_mix_post": dg2,
        "norm_mlp_pre": dg3, "w_up": g_w_up, "w_down": g_w_down, "norm_mlp_post": dg4,
    }
    if dist is not None:
        dist.late_grads(grads, loss[0, 0])
    grad_x, grads["norm_mix_pre"] = hosted("du_norm", functools.partial(
        _du_norm, d3, dz, dxl, dxbc, ddtr, w5, wxbc, wdt, x, dh1, p["norm_mix_pre"]))
    return loss[0, 0], grad_x, grads


def _split_w_in_t(w_in_t):
    z, xbc, dtc, g, xl, ga, gb = jnp.split(w_in_t, [D, D + XBC, D + XBC + NH, 2 * D + XBC + NH,
                                                    3 * D + XBC + NH, 4 * D + XBC + NH], axis=0)
    return jnp.concatenate([g, ga, gb, z, xl], axis=0), xbc, jnp.pad(dtc, ((0, 128 - NH), (0, 0)))


COMM = BF16


def _place():
    x, y, c = lax.axis_index("x"), lax.axis_index("y"), lax.axis_index("c")
    chips = [(1 - x, y), (x, 1 - y), (1 - x, 1 - y)]
    return x, y, c, chips


def _remote(src, dst, send_sem, recv_sem, to):
    return pltpu.make_async_remote_copy(src_ref=src, dst_ref=dst, send_sem=send_sem, recv_sem=recv_sem, device_id=to,
                                        device_id_type=MESH)


def _gather_plan(big, small=(), axes=None):
    nb = len(big)
    arrs = list(big) + list(small)
    na = len(arrs)
    axes = list(axes or [0] * nb)

    def half(ref, a, k, which):
        h = arrs[a].shape[axes[a]] // 2
        cut = (pl.ds(which * h, h),) if axes[a] == 0 else (slice(None), pl.ds(which * h, h))
        return ref.at[cut] if k is None else ref.at[(k,) + cut]

    def direct(ins, outs, send, recv):
        x, y, c, chips = _place()
        k = 2 * x + y
        cps = []
        for a in range(na):
            src, dst = (half(ins[a], a, None, c), half(outs[a], a, k, c)) if a < nb else (ins[a], outs[a].at[k])
            cps += [_remote(src, dst, send.at[a, j], recv.at[a, j], (cx, cy, c)) for j, (cx, cy) in enumerate(chips)]
        return cps

    def passed(outs, send, recv):
        x, y, c, chips = _place()
        cps = []
        for j, (cx, cy) in enumerate(chips):
            for a in range(nb):
                got = half(outs[a], a, 2 * cx + cy, c)
                cps.append(_remote(got, got, send.at[a, 3 + j], recv.at[a, 3 + j], (x, y, 1 - c)))
        return cps

    def start(ins, outs, sems):
        for cp in direct(ins, outs, *sems[0]):
            cp.start()

    def mid(ins, outs, sems):
        send, recv = sems[0]
        _, _, c, chips = _place()
        fwd = passed(outs, send, recv)
        for j, (cx, cy) in enumerate(chips):
            kj = 2 * cx + cy
            for a in range(na):
                got = half(outs[a], a, kj, c) if a < nb else outs[a].at[kj]
                _remote(got, got, send.at[a, j], recv.at[a, j], (cx, cy, c)).wait_recv()
                if a < nb:
                    fwd[j * nb + a].start()

    def finish(ins, outs, sems):
        send, recv = sems[0]
        x, y, c, chips = _place()
        for j, (cx, cy) in enumerate(chips):
            for a in range(nb):
                got = half(outs[a], a, 2 * cx + cy, 1 - c)
                _remote(got, got, send.at[a, 3 + j], recv.at[a, 3 + j], (x, y, 1 - c)).wait_recv()
        for cp in direct(ins, outs, send, recv) + passed(outs, send, recv):
            cp.wait_send()

    return _Plan(arrs, [_sds((NCHIP,) + a.shape, a.dtype) for a in arrs], [(na, 6)], start, finish, mid)


def _own_shards(gathered, shards):
    kchip = 2 * lax.axis_index("x") + lax.axis_index("y")
    return [lax.dynamic_update_index_in_dim(o, a, kchip, 0) for o, a in zip(gathered, shards)]


def _swap_plan(ins, outs, sems, copies):
    def start(i, o, s):
        for cp in copies(i, o, *s[0]):
            cp.start()

    def finish(i, o, s):
        for cp in copies(i, o, *s[0]):
            cp.wait()

    return _Plan(ins, outs, [sems], start, finish)


def _half_shape(shape, axis):
    return tuple(d // 2 if i == axis else d for i, d in enumerate(shape))


def _pair_exchange_plan(gs, axis=1):
    def copies(ins, outs, send, recv):
        x, y, c, _ = _place()
        cps = []
        for a in range(len(gs)):
            h = ins[a].shape[axis] // 2
            theirs = pl.ds((1 - c) * h, h)
            src = ins[a].at[:, theirs] if axis == 1 else ins[a].at[:, :, theirs]
            cps.append(_remote(src, outs[a], send.at[a], recv.at[a], (x, y, 1 - c)))
        return cps

    return _swap_plan(gs, [_sds(_half_shape(g.shape, axis), g.dtype) for g in gs], (len(gs),), copies)


def _pair_add(g, got, cidx, name, axis=1):
    half = _half_shape(g.shape, axis)
    blk, nt, _, part = _half_blocks(g.shape[1:], axis - 1)

    def body(c_ref, g_ref, o_ref, p_ref, pc_ref):
        sm = g_ref[...] + o_ref[...]
        p_ref[...] = sm
        pc_ref[...] = sm.astype(pc_ref.dtype)

    def mine(k, i, c_ref):
        j = c_ref[0] * nt + i
        return (k, j, 0) if axis == 1 else (k, 0, j)

    spec = pl.BlockSpec((1,) + blk, lambda k, i, c_ref: (k,) + part(i))
    return pl.pallas_call(
        body, name=name,
        grid_spec=pltpu.PrefetchScalarGridSpec(
            num_scalar_prefetch=1, grid=(g.shape[0], nt),
            in_specs=[pl.BlockSpec((1,) + blk, mine), spec], out_specs=[spec, spec]),
        out_shape=[_sds(half), _sds(half, COMM)],
        compiler_params=_cp("parallel", "parallel"),
    )(cidx, g, got)


def _chip_exchange_plan(ps):
    def copies(ins, outs, send, recv):
        _, _, c, chips = _place()
        return [_remote(ins[a].at[2 * cx + cy], outs[a].at[j], send.at[a, j], recv.at[a, j], (cx, cy, c))
                for a in range(len(ps)) for j, (cx, cy) in enumerate(chips)]

    return _swap_plan(ps, [_sds((NCHIP - 1,) + p.shape[1:], p.dtype) for p in ps], (len(ps), 3), copies)


def _shard_sum(p, got, kidx, name, axis=1):
    full = tuple(2 * d if i == axis - 1 else d for i, d in enumerate(p.shape[1:]))
    blk, nt, _, part = _half_blocks(full, axis - 1)

    def body(k_ref, p_ref, g_ref, o_ref):
        sm = p_ref[0]
        for j in range(NCHIP - 1):
            sm = sm + g_ref[j].astype(F32)
        o_ref[...] = sm

    return pl.pallas_call(
        body, name=name,
        grid_spec=pltpu.PrefetchScalarGridSpec(
            num_scalar_prefetch=1, grid=(nt,),
            in_specs=[pl.BlockSpec((1,) + blk, lambda i, k_ref: (k_ref[0],) + part(i)),
                      pl.BlockSpec((NCHIP - 1,) + blk, lambda i, k_ref: (0,) + part(i))],
            out_specs=pl.BlockSpec(blk, lambda i, k_ref: part(i))),
        out_shape=_sds(p.shape[1:]),
        compiler_params=_cp("parallel"),
    )(kidx, p, got)


def _pair_swap_plan(rs):
    def copies(ins, outs, send, recv):
        x, y, c, _ = _place()
        return [_remote(ins[a], outs[a], send.at[a], recv.at[a], (x, y, 1 - c)) for a in range(len(rs))]

    return _swap_plan(rs, [_sds(r.shape, r.dtype) for r in rs], (len(rs),), copies)


def _allgather8_plan(v):
    def pieces(ins, outs, send, recv):
        x, y, c, chips = _place()
        me, sibling = (x, y, c), (x, y, 1 - c)

        def copy(k, block, to, src=None):
            px, py, pc = block
            slot = outs[0].at[4 * px + 2 * py + pc]
            return _remote(slot if src is None else src, slot, send.at[k], recv.at[k], to)

        first = [copy(0, me, sibling, src=ins[0])] + [copy(1 + j, me, (*chip, c), src=ins[0])
                                                      for j, chip in enumerate(chips)]
        passed = [copy(4 + j, (*chip, c), sibling) for j, chip in enumerate(chips)]
        arrivals = [copy(1 + j, (*chip, c), me) for j, chip in enumerate(chips)]
        late = [copy(0, sibling, me)] + [copy(4 + j, (*chip, 1 - c), me) for j, chip in enumerate(chips)]
        return first, passed, arrivals, late

    def start(ins, outs, sems):
        for cp in pieces(ins, outs, *sems[0])[0]:
            cp.start()

    def mid(ins, outs, sems):
        _, passed, arrivals, _ = pieces(ins, outs, *sems[0])
        for got, fwd in zip(arrivals, passed):
            got.wait_recv()
            fwd.start()

    def finish(ins, outs, sems):
        first, passed, _, late = pieces(ins, outs, *sems[0])
        for got in late:
            got.wait_recv()
        for cp in first + passed:
            cp.wait_send()

    return _Plan([v], [_sds((8,) + v.shape, v.dtype)], [(7,)], start, finish, mid)


def _own_block(gathered, v):
    me = 4 * lax.axis_index("x") + 2 * lax.axis_index("y") + lax.axis_index("c")
    return lax.dynamic_update_index_in_dim(gathered, v, me, 0)


def _sum_devices(allv, name):
    _, r, _ = allv.shape

    def body(a_ref, o_ref):
        sm = a_ref[0]
        for d in range(1, 8):
            sm = sm + a_ref[d]
        o_ref[...] = sm

    return pl.pallas_call(
        body, name=name, grid=(1,), in_specs=[_full_spec((8, r, 128))], out_specs=_full_spec((r, 128)),
        out_shape=_sds((r, 128)), compiler_params=_cp("arbitrary"),
    )(allv)


def _pack(arrs):
    flat = jnp.concatenate([a.reshape(-1) for a in arrs])
    return jnp.pad(flat, (0, (-flat.shape[0]) % 1024)).reshape(-1, 128)


def _unpack(packed, shapes):
    flat, outs, off = packed.reshape(-1), [], 0
    for shp in shapes:
        n = math.prod(shp)
        outs.append(flat[off:off + n].reshape(shp))
        off += n
    return outs


BIG = ("w_in", "w_out", "w_up", "w_down")
CONV = ("conv_ssm_w", "conv_lru_w")
WEIGHTS = ("norm_mix_pre", "w_in", "conv_ssm_w", "conv_ssm_b", "dt_bias", "a_log", "d_skip", "ssm_norm", "conv_lru_w",
           "conv_lru_b", "lru_wa", "lru_ba", "lru_wx", "lru_bx", "lru_lambda", "w_out", "norm_mix_post",
           "norm_mlp_pre", "w_up", "w_down", "norm_mlp_post")
SMALL = tuple(n for n in WEIGHTS if n not in BIG and n not in CONV)
EARLY = ("w_down", "w_up", "w_out")


def _cat_cols(g):
    return jnp.concatenate([g[k] for k in range(NCHIP)], axis=1)


class _Dist:
    def __init__(self, shards, first, cidx, kidx, wmv):
        self.shards, self.first, self.cidx, self.kidx, self.wmv = shards, first, cidx, kidx, wmv
        self.updates = {}

    def early_grads(self, w_down, w_up, w_out):
        self.shard_major = [w_down.reshape(NCHIP, D, D), w_up, w_out.reshape(NCHIP, D // NCHIP, D)]

    def late_grads(self, grads, loss):
        g_in = grads["w_in_t"][None]
        got, = _run_plan(_pair_exchange_plan([g_in], axis=2), "grad_pair_exchange_w_in")
        p_all, pc_all = _pair_add(g_in, got, self.cidx, "grad_pair_add_w_in", axis=2)
        self.p_in = lax.dynamic_slice_in_dim(p_all[0], self.kidx[0] * W_IN_SHARD, W_IN_SHARD, axis=0)[None]
        self.pc_in = pc_all.reshape(NCHIP, W_IN_SHARD, D // 2)
        self.small_names = [n for n in SMALL + CONV if n != "norm_mix_pre"]
        self.small_shapes = [grads[n].shape for n in self.small_names] + [(1,)]
        self.packed_small = _pack([grads[n] for n in self.small_names] + [loss.reshape(1)])

    def plan(self, key):
        if key == "norm_u":
            return _gather_plan(self.first[:1], self.first[1:], axes=[1])
        if key == "proj5":
            return _gather_plan([self.shards["w_out"], self.shards["w_up"]])
        if key == "lru_fwd":
            return _gather_plan([self.shards["w_down"]])
        if key == "lru_bwd":
            return _pair_exchange_plan(self.shard_major)
        if key == "ssd_bwd":
            return _chip_exchange_plan([pc for _, pc in self.pair])
        if key == "conv_bwd":
            return _pair_swap_plan(self.mine)
        if key == "du_norm":
            return _merge_plans(_allgather8_plan(self.packed_small), _chip_exchange_plan([self.pc_in]))
        return None

    def done(self, key, got, p):
        if key == "norm_u":
            g_in, g_cs, g_cl = _own_shards(got, self.first)
            w5, wxbc, wdt = _split_w_in_t(g_in.reshape(W_IN_COLS, D))
            p.update(w5=w5, wxbc=wxbc, wdt=wdt, conv_ssm_w=_cat_cols(g_cs), conv_lru_w=_cat_cols(g_cl))
        elif key == "proj5":
            g_out, g_up = _own_shards(got, [self.shards["w_out"], self.shards["w_up"]])
            p.update(w_out=g_out.reshape(D, D), w_up=_cat_cols(g_up))
        elif key == "lru_fwd":
            g_down, = _own_shards(got, [self.shards["w_down"]])
            p.update(w_down=g_down.reshape(DFF, D))
        elif key == "lru_bwd":
            self.pair = [_pair_add(gs, o, self.cidx, f"grad_pair_add_{n}")
                         for gs, o, n in zip(self.shard_major, got, EARLY)]
        elif key == "ssd_bwd":
            self.mine = [_shard_sum(pf, o, self.kidx, f"grad_shard_sum_{n}")
                         for (pf, _), o, n in zip(self.pair, got, EARLY)]
        elif key == "conv_bwd":
            w, m, v = self.wmv
            for n, mine, other in zip(EARLY, self.mine, got):
                g_full = jnp.where(self.cidx[0] == 0, jnp.concatenate([mine, other]), jnp.concatenate([other, mine]))
                self.updates[n] = (g_full, *_adamw_sc(w[n], g_full, m[n], v[n], f"adamw_sc_{n}"))
        elif key == "du_norm":
            self.all_small, self.from_chips_in = got


def kernel(x, norm_mix_pre, w_in, conv_ssm_w, conv_ssm_b, dt_bias, a_log, d_skip, ssm_norm, conv_lru_w, conv_lru_b, lru_wa, lru_ba, lru_wx, lru_bx, lru_lambda, w_out, norm_mix_post, norm_mlp_pre, w_up, w_down, norm_mlp_post, loss_target, m_norm_mix_pre, m_w_in, m_conv_ssm_w, m_conv_ssm_b, m_dt_bias, m_a_log, m_d_skip, m_ssm_norm, m_conv_lru_w, m_conv_lru_b, m_lru_wa, m_lru_ba, m_lru_wx, m_lru_bx, m_lru_lambda, m_w_out, m_norm_mix_post, m_norm_mlp_pre, m_w_up, m_w_down, m_norm_mlp_post, v_norm_mix_pre, v_w_in, v_conv_ssm_w, v_conv_ssm_b, v_dt_bias, v_a_log, v_d_skip, v_ssm_norm, v_conv_lru_w, v_conv_lru_b, v_lru_wa, v_lru_ba, v_lru_wx, v_lru_bx, v_lru_lambda, v_w_out, v_norm_mix_post, v_norm_mlp_pre, v_w_up, v_w_down, v_norm_mlp_post):
    args = locals()
    w = {n: args[n][0] for n in WEIGHTS}
    m = {n: args["m_" + n][0] for n in WEIGHTS}
    v = {n: args["v_" + n][0] for n in WEIGHTS}
    cidx = lax.axis_index("c").astype(jnp.int32).reshape(1)
    kchip = 2 * lax.axis_index("x") + lax.axis_index("y")
    to_t = lambda a: jnp.transpose(a, (2, 0, 1)).reshape(W_IN_SHARD, D)
    from_t = lambda a: jnp.transpose(a.reshape(W_IN_SHARD, 1, D), (1, 2, 0))
    shards = {n: (to_t(w_in) if n == "w_in" else w[n]).astype(MXU) for n in BIG}
    dist = _Dist(shards, [shards["w_in"], w["conv_ssm_w"], w["conv_lru_w"]], cidx, kchip.astype(jnp.int32).reshape(1),
                 (w, m, v))
    p = {n: (w[n].reshape(1, -1) if w[n].ndim == 1 else w[n]) for n in SMALL}

    _, grad_x, g = _local_step(x[0], loss_target[0], p, dist)

    half_in = _shard_sum(dist.p_in, dist.from_chips_in, jnp.zeros((1,), jnp.int32), "grad_shard_sum_w_in", axis=2)
    packed_g1 = _pack([g["norm_mix_pre"]])
    all_g1, other_in = _run_plan(_merge_plans(_allgather8_plan(packed_g1), _pair_swap_plan([half_in])),
                                 "grad_pair_swap_w_in")

    reduced = {}
    *summed, loss = _unpack(_sum_devices(_own_block(dist.all_small, dist.packed_small), "small_sum"),
                            dist.small_shapes)
    loss = loss.reshape(())
    g1, = _unpack(_sum_devices(_own_block(all_g1, packed_g1), "small_sum_norm_mix_pre"), [g["norm_mix_pre"].shape])
    for n, s in zip(dist.small_names + ["norm_mix_pre"], summed + [g1]):
        if n in CONV:
            width = w[n].shape[1]
            reduced[n] = lax.dynamic_slice_in_dim(s, kchip * width, width, axis=1)
        else:
            reduced[n] = s.reshape(w[n].shape)

    delta, new_m, new_v = {}, {}, {}
    for n in EARLY:
        reduced[n], delta[n], new_m[n], new_v[n] = dist.updates[n]
    outs_t = _adamw_halves(to_t(w_in), half_in, other_in, to_t(m_w_in), to_t(v_w_in), cidx, "adamw_w_in", axis=1)
    for d, o in zip((reduced, delta, new_m, new_v), outs_t):
        d["w_in"] = from_t(o)[0]
    for n in CONV:
        delta[n], new_m[n], new_v[n] = _adamw(w[n], reduced[n], m[n], v[n], f"adamw_{n}")
    shapes = [w[n].shape for n in SMALL]
    packed = [_pack([d[n] for n in SMALL]) for d in (w, reduced, m, v)]
    for d, out in zip((delta, new_m, new_v), _adamw(*packed, "adamw_small")):
        d.update(zip(SMALL, _unpack(out, shapes)))

    lead = lambda d: [d[n][None] for n in WEIGHTS]
    return (loss, grad_x[None], *lead(reduced), *lead(delta), *lead(new_m), *lead(new_v))
```

```python
import functools
import math

import jax
import jax.numpy as jnp
from jax import lax
from jax.experimental import pallas as pl
from jax.experimental.pallas import tpu as pltpu
from jax.experimental.pallas import tpu_sc as plsc

F32 = jnp.float32
BF16 = jnp.bfloat16
MXU = BF16

D = 1024
DFF = 4096
NH = 16
HP = 64
NG = 2
NS = 128
CH = 128
XBC = D + 2 * NG * NS
GW = D // NG
LRU_C = 8.0
EPS = 1e-6
NCHIP = 4
W_IN_COLS = 6672
W_IN_SHARD = W_IN_COLS // NCHIP

ADAM_LR = 0.001
ADAM_B1 = 0.9
ADAM_B2 = 0.999
ADAM_EPS = 1e-08
ADAM_WD = 0.01
ADAM_STEP = 10

VMEM_LIMIT = 56 * 1024 * 1024
TK_GRAD = 2048
MID_AT = 0.7
ROWS_FUSED = 512
ROWS_CONV = 512
COL_G, COL_GA, COL_GB, COL_Z, COL_XL = range(5)
MESH = pl.DeviceIdType.MESH


def _cp(*sem):
    return pltpu.CompilerParams(dimension_semantics=sem, vmem_limit_bytes=VMEM_LIMIT)


def _dot(a, b, ca=1, cb=0, prec=None):
    return lax.dot_general(a, b, (((ca,), (cb,)), ((), ())), precision=prec, preferred_element_type=F32)


def _mdot(a, b, ca=1, cb=0):
    return _dot(a.astype(MXU), b.astype(MXU), ca, cb)


def _bf16_parts(v, n):
    parts = []
    for i in range(n):
        p = v.astype(BF16)
        parts.append(p)
        if i < n - 1:
            v = v - p.astype(F32)
    return parts


def _xdot(a, b, passes, split_b=False):
    if split_b:
        a16 = a.astype(BF16)
        terms = [_dot(a16, p) for p in _bf16_parts(b, passes)]
    else:
        b16 = b.astype(BF16)
        terms = [_dot(p, b16) for p in _bf16_parts(a, passes)]
    return functools.reduce(lambda u, v: u + v, terms)


def _sig(x):
    return 0.5 * jnp.tanh(0.5 * x) + 0.5


def _silu(x):
    return x * _sig(x)


def _dsilu(x):
    s = _sig(x)
    return s * (1.0 + x * (1.0 - s))


def _softplus(x):
    e = jnp.exp(-jnp.abs(x))
    return jnp.maximum(x, 0.0) + jnp.where(e < 1e-4, e * (1.0 - 0.5 * e), jnp.log(1.0 + e))


_GELU_C = math.sqrt(2.0 / math.pi)


def _gelu(x):
    t = jnp.tanh(_GELU_C * (x + 0.044715 * x * x * x))
    return 0.5 * x * (1.0 + t)


def _gelu_and_grad(x):
    x2 = x * x
    t = jnp.tanh(_GELU_C * (x + 0.044715 * x * x2))
    half = 0.5 * (1.0 + t)
    return x * half, half + 0.5 * x * (1.0 - t * t) * _GELU_C * (1.0 + 3.0 * 0.044715 * x2)


def _one_minus_sq(a, la):
    x = 2.0 * la
    series = -x * (1.0 + x * (0.5 + x * (1.0 / 6.0)))
    return jnp.where(x > -0.01, series, 1.0 - a * a)


def _rms(x):
    return lax.rsqrt(jnp.mean(x * x, axis=-1, keepdims=True) + EPS)


def _rms_bwd(x, r, g, dy):
    xn = x * r
    dxh = dy * g
    m = jnp.mean(dxh * xn, axis=-1, keepdims=True)
    return r * (dxh - xn * m), jnp.sum(dy * xn, axis=0, keepdims=True)


def _row_spec(t, c, col=0):
    return pl.BlockSpec((t, c), lambda i: (i, col))


def _rev_spec(t, c, n, col=0):
    return pl.BlockSpec((t, c), lambda i: (n - 1 - i, col))


def _full_spec(shape, once=False):
    nd = len(shape)
    if once:
        return pl.BlockSpec(shape, lambda *_: (0,) * nd, pipeline_mode=pl.Buffered(1))
    return pl.BlockSpec(shape, lambda *_: (0,) * nd)


def _sds(shape, dtype=F32):
    return jax.ShapeDtypeStruct(shape, dtype)


ANY = pl.BlockSpec(memory_space=pl.ANY)


class _Plan:
    def __init__(self, ins, outs, sems, start, finish, mid=None):
        self.ins, self.outs, self.sems = list(ins), list(outs), list(sems)
        self.start, self.finish, self.mid = start, finish, mid or (lambda i, o, s: None)


def _merge_plans(*plans):
    def each(fn_name, ins, outs, sems):
        i = o = s = 0
        for p in plans:
            getattr(p, fn_name)(ins[i:i + len(p.ins)], outs[o:o + len(p.outs)], sems[s:s + len(p.sems)])
            i, o, s = i + len(p.ins), o + len(p.outs), s + len(p.sems)

    return _Plan([a for p in plans for a in p.ins], [a for p in plans for a in p.outs],
                 [a for p in plans for a in p.sems], functools.partial(each, "start"),
                 functools.partial(each, "finish"), functools.partial(each, "mid"))


def _pcall(body, args, *, name, grid, in_specs, out_specs, out_shape, sem, scratch_shapes=(), plan=None):
    single = not isinstance(out_shape, (list, tuple))
    out_specs = [out_specs] if single else list(out_specs)
    out_shape = [out_shape] if single else list(out_shape)
    if plan is None:
        outs = pl.pallas_call(body, name=name, grid=grid, in_specs=list(in_specs), out_specs=out_specs,
                              out_shape=out_shape, scratch_shapes=list(scratch_shapes),
                              compiler_params=_cp(*sem))(*args)
        return outs[0] if single else outs
    n_in, n_out, n_sc, ni, no = len(in_specs), len(out_shape), len(scratch_shapes), len(plan.ins), len(plan.outs)

    def hosted(*refs):
        b0 = n_in + ni
        b1 = b0 + n_out + no
        sem_refs = refs[b1 + n_sc:]
        sems = [(sem_refs[2 * q], sem_refs[2 * q + 1]) for q in range(len(plan.sems))]
        step = functools.reduce(lambda lin, ig: lin * ig[1] + ig[0],
                                [(pl.program_id(d), g) for d, g in enumerate(grid)], 0)
        total = math.prod(grid)

        @pl.when(step == 0)
        def _():
            plan.start(refs[n_in:b0], refs[b0 + n_out:b1], sems)

        body(*refs[:n_in], *refs[b0:b0 + n_out], *refs[b1:b1 + n_sc])

        @pl.when(step == min(int(MID_AT * total), total - 1))
        def _():
            plan.mid(refs[n_in:b0], refs[b0 + n_out:b1], sems)

        @pl.when(step == total - 1)
        def _():
            plan.finish(refs[n_in:b0], refs[b0 + n_out:b1], sems)

    dma = [pltpu.SemaphoreType.DMA(shape) for shape in plan.sems for _ in range(2)]
    outs = pl.pallas_call(hosted, name=name, grid=grid, in_specs=list(in_specs) + [ANY] * ni,
                          out_specs=out_specs + [ANY] * no, out_shape=out_shape + plan.outs,
                          scratch_shapes=list(scratch_shapes) + dma,
                          compiler_params=_cp(*("arbitrary",) * len(grid)))(*args, *plan.ins)
    return (outs[0] if single else outs[:n_out]), outs[n_out:]


def _run_plan(plan, name):
    ni, no = len(plan.ins), len(plan.outs)

    def body(*refs):
        sem_refs = refs[ni + no:]
        sems = [(sem_refs[2 * q], sem_refs[2 * q + 1]) for q in range(len(plan.sems))]
        plan.start(refs[:ni], refs[ni:ni + no], sems)
        plan.mid(refs[:ni], refs[ni:ni + no], sems)
        plan.finish(refs[:ni], refs[ni:ni + no], sems)

    return pl.pallas_call(
        body, name=name, in_specs=[ANY] * ni, out_specs=[ANY] * no, out_shape=plan.outs,
        scratch_shapes=[pltpu.SemaphoreType.DMA(shape) for shape in plan.sems for _ in range(2)],
    )(*plan.ins)


def _matmul(a, b, *, name, ta=False, tb=False, tm=512, tn=1024, tk=1024, out_dtype=F32, a_fn=None, epi=None,
            epi_args=(), plan=None, col_blocks_major=False):
    m, k = (a.shape[1], a.shape[0]) if ta else a.shape
    n = b.shape[0] if tb else b.shape[1]
    tm, tn, tk = min(tm, m), min(tn, n), min(tk, k)
    nk = k // tk
    a_spec = pl.BlockSpec((tk, tm), lambda i, j, kk: (kk, i)) if ta else pl.BlockSpec((tm, tk), lambda i, j, kk: (i, kk))
    b_spec = pl.BlockSpec((tn, tk), lambda i, j, kk: (j, kk)) if tb else pl.BlockSpec((tk, tn), lambda i, j, kk: (kk, j))
    e_specs = [pl.BlockSpec((tm, tn), lambda i, j, kk: (i, j)) for _ in epi_args]
    ne = len(epi_args)

    def body(a_ref, b_ref, *rest):
        e_refs, o_ref = rest[:ne], rest[ne]
        av = a_ref[...]
        if a_fn is not None:
            av = a_fn(av)
        part = _mdot(av, b_ref[...], 0 if ta else 1, 1 if tb else 0)

        def finish(r):
            if epi is not None:
                r = epi(r, *[e[...] for e in e_refs])
            o_ref[...] = r.astype(o_ref.dtype)

        if nk == 1:
            finish(part)
            return
        acc_ref = rest[ne + 1]
        kk = pl.program_id(2)

        @pl.when(kk == 0)
        def _():
            acc_ref[...] = part

        @pl.when(jnp.logical_and(kk > 0, kk < nk - 1))
        def _():
            acc_ref[...] += part

        @pl.when(kk == nk - 1)
        def _():
            finish(acc_ref[...] + part)

    if col_blocks_major:
        out_spec = pl.BlockSpec((None, tm, tn), lambda i, j, kk: (j, i, 0))
        out_shape = _sds((n // tn, m, tn), out_dtype)
    else:
        out_spec = pl.BlockSpec((tm, tn), lambda i, j, kk: (i, j))
        out_shape = _sds((m, n), out_dtype)
    return _pcall(
        body, (a, b, *epi_args), name=name, grid=(m // tm, n // tn, nk),
        in_specs=[a_spec, b_spec] + e_specs, out_specs=out_spec, out_shape=out_shape,
        scratch_shapes=[pltpu.VMEM((tm, tn), F32)] if nk > 1 else [],
        sem=("parallel", "parallel", "arbitrary"), plan=plan)


def _relu2(p):
    p = jnp.maximum(p, jnp.zeros((), p.dtype))
    return p * p


def _norm_cast(x, g, name, plan=None):
    s = x.shape[0]
    t = min(512, s)

    def body(x_ref, g_ref, o_ref):
        xv = x_ref[...]
        o_ref[...] = (xv * _rms(xv) * g_ref[...]).astype(o_ref.dtype)

    return _pcall(body, (x, g), name=name, grid=(s // t,), in_specs=[_row_spec(t, D), _full_spec((1, D))],
                  out_specs=_row_spec(t, D), out_shape=_sds((s, D), MXU), sem=("parallel",), plan=plan)


def _conv_fwd(xbc_raw, proj5, dt_raw, cw_s, cb_s, cw_l, cb_l, dt_bias):
    s = xbc_raw.shape[0]
    t = min(ROWS_CONV, s)

    def body(xs_ref, xl_ref, dtr_ref, cws_ref, cbs_ref, cwl_ref, cbl_ref, dtb_ref, xc_ref, dsl_ref, xr_ref, dt_ref,
             bs_ref, bl_ref):
        @pl.when(pl.program_id(0) == 0)
        def _():
            bs_ref[0:8, :] = jnp.zeros((8, XBC), F32)
            bl_ref[0:8, :] = jnp.zeros((8, D), F32)

        bs_ref[8:t + 8, :] = xs_ref[...]
        bl_ref[8:t + 8, :] = xl_ref[...].astype(F32)

        def conv(buf, w_ref, b_ref):
            acc = b_ref[...] + w_ref[3:4, :] * buf[8:t + 8, :]
            for k in (1, 2, 3):
                acc = acc + w_ref[3 - k:4 - k, :] * buf[8 - k:t + 8 - k, :]
            return acc

        pre = conv(bs_ref, cws_ref, cbs_ref)
        sg = _sig(pre)
        xc_ref[...] = pre * sg
        dsl_ref[...] = (sg * (1.0 + pre * (1.0 - sg))).astype(dsl_ref.dtype)
        xr_ref[...] = conv(bl_ref, cwl_ref, cbl_ref)
        dt_ref[...] = _softplus(dtr_ref[...] + dtb_ref[...])
        bs_ref[0:8, :] = bs_ref[t:t + 8, :]
        bl_ref[0:8, :] = bl_ref[t:t + 8, :]

    return pl.pallas_call(
        body, name="conv_fwd", grid=(s // t,),
        in_specs=[_row_spec(t, XBC), _row_spec(t, D, COL_XL), _row_spec(t, 128), _full_spec((4, XBC)),
                  _full_spec((1, XBC)), _full_spec((4, D)), _full_spec((1, D)), _full_spec((1, 128))],
        out_specs=[_row_spec(t, XBC), _row_spec(t, XBC), _row_spec(t, D), _row_spec(t, 128)],
        out_shape=[_sds((s, XBC)), _sds((s, XBC), BF16), _sds((s, D)), _sds((s, 128))],
        scratch_shapes=[pltpu.VMEM((t + 8, XBC), F32), pltpu.VMEM((t + 8, D), F32)],
        compiler_params=_cp("arbitrary"),
    )(xbc_raw, proj5, dt_raw, cw_s, cb_s, cw_l, cb_l, dt_bias)


def _ssd_chunk_setup(dt_ref, alog_ref, e_ref, at_ref, dtt_ref):
    lane = lax.broadcasted_iota(jnp.int32, (CH, 128), 1)
    row = lax.broadcasted_iota(jnp.int32, (CH, 128), 0)
    lane1 = lax.broadcasted_iota(jnp.int32, (1, 128), 1)
    a = jnp.where(lane1 < NH, -jnp.exp(alog_ref[...]), 0.0)
    dtv = dt_ref[...]
    adt = dtv * a
    tril = row >= lane
    acum = _xdot(tril.astype(F32), adt, 3, split_b=True)
    alast = jnp.sum(adt, axis=0, keepdims=True)
    at_ref[...] = acum.T
    dtt_ref[...] = dtv.T
    e = e_ref[...]
    ea_x = _xdot(jnp.exp(acum), e, 2)
    ws = jnp.exp(alast - acum) * dtv
    ws_x = _xdot(ws, e, 2)
    eal = jnp.exp(alast)
    eal_x = jnp.max(_xdot(jnp.broadcast_to(eal, (8, 128)), e, 3), axis=0, keepdims=True)
    return dict(lane=lane, row=row, tril=tril, a=a, dtv=dtv, acum=acum, alast=alast, ea_x=ea_x, ws=ws, ws_x=ws_x,
                eal=eal, eal_x=eal_x)


def _head_decay(cs, at_ref, dtt_ref, h):
    col = jnp.sum(jnp.where(cs["lane"] == h, cs["acum"], 0.0), axis=1, keepdims=True)
    ld = jnp.where(cs["tril"], jnp.exp(jnp.minimum(col - at_ref[h:h + 1, :], 0.0)), 0.0)
    return ld, dtt_ref[h:h + 1, :]


def _ssd_fwd(xbc_c, dt, proj5, a_log, dskip_x, ssm_norm, expand):
    s = xbc_c.shape[0]
    nc = s // CH

    def body(xc_ref, dt_ref, z_ref, alog_ref, dsk_ref, ng_ref, e_ref, y_ref, ya_ref, st_ref, h_ref, at_ref, dtt_ref,
             yd_ref):
        @pl.when(pl.program_id(0) == 0)
        def _():
            h_ref[...] = jnp.zeros_like(h_ref)

        cs = _ssd_chunk_setup(dt_ref, alog_ref, e_ref, at_ref, dtt_ref)
        lane = cs["lane"]
        for g in range(NG):
            gs = slice(GW * g, GW * (g + 1))
            bg = xc_ref[:, D + NS * g:D + NS * (g + 1)]
            cg = xc_ref[:, D + NG * NS + NS * g:D + NG * NS + NS * (g + 1)]
            cb = _mdot(cg, bg, 1, 1)
            for j in range(4 * g, 4 * g + 4):
                ps = slice(128 * j, 128 * (j + 1))
                xp = xc_ref[:, ps]
                acc = jnp.zeros((CH, 128), F32)
                for hf in range(2):
                    ld, rowdt = _head_decay(cs, at_ref, dtt_ref, 2 * j + hf)
                    hm = (lane >= HP) if hf else (lane < HP)
                    acc = acc + _mdot(cb * ld * rowdt, jnp.where(hm, xp, 0.0))
                yd_ref[:, ps] = acc
            hg = h_ref[:, gs]
            yd_ref[:, gs] += _mdot(cg, hg) * cs["ea_x"][:, gs]
            st = _mdot(bg, xc_ref[:, gs] * cs["ws_x"][:, gs], 0, 0)
            st_ref[0, :, gs] = hg
            h_ref[:, gs] = cs["eal_x"][:, gs] * hg + st
        y = yd_ref[...] + dsk_ref[...] * xc_ref[:, 0:D]
        y_ref[...] = y
        yg = y * _silu(z_ref[...].astype(F32))
        for g in range(NG):
            gs = slice(GW * g, GW * (g + 1))
            seg = yg[:, gs]
            ya_ref[:, gs] = seg * _rms(seg) * ng_ref[:, gs]

    return pl.pallas_call(
        body, name="ssd_fwd", grid=(nc,),
        in_specs=[_row_spec(CH, XBC), _row_spec(CH, 128), _row_spec(CH, D, COL_Z), _full_spec((1, 128)),
                  _full_spec((1, D)), _full_spec((1, D)), _full_spec((128, D))],
        out_specs=[_row_spec(CH, D), _row_spec(CH, D), pl.BlockSpec((1, NS, D), lambda i: (i, 0, 0))],
        out_shape=[_sds((s, D)), _sds((s, D)), _sds((nc, NS, D))],
        scratch_shapes=[pltpu.VMEM((NS, D), F32), pltpu.VMEM((128, 128), F32), pltpu.VMEM((128, 128), F32),
                        pltpu.VMEM((CH, D), F32)],
        compiler_params=_cp("arbitrary"),
    )(xbc_c, dt, proj5, a_log, dskip_x, ssm_norm, expand)


def _lru_gates(xr, wab_ref, ba_ref, bx_ref, lam_ref):
    pre = _mdot(xr, wab_ref[...])
    gr = _sig(pre[:, 0:D] + ba_ref[...])
    gi = _sig(pre[:, D:2 * D] + bx_ref[...])
    sp = _softplus(-lam_ref[...])
    la = -LRU_C * gr * sp
    a = jnp.exp(la)
    oms = _one_minus_sq(a, la)
    inv_mult = lax.rsqrt(oms)
    return gr, gi, sp, a, oms * inv_mult, inv_mult


def _blocked_scan(a, u, carry_ref, a_ref, u_ref, c_ref, out_ref, reverse):
    t = a.shape[0]
    ns = t // 8

    def combine(av, uv, idx, n, sh):
        m = (idx < n - sh) if reverse else (idx >= sh)
        by = n - sh if reverse else sh
        return jnp.where(m, av * pltpu.roll(av, by, 0), av), jnp.where(m, uv + av * pltpu.roll(uv, by, 0), uv)

    row = lax.broadcasted_iota(jnp.int32, (t, D), 0)
    rin = jnp.bitwise_and(row, 7)
    for sh in (1, 2, 4):
        m = (rin < 8 - sh) if reverse else (rin >= sh)
        by = t - sh if reverse else sh
        a, u = jnp.where(m, a * pltpu.roll(a, by, 0), a), jnp.where(m, u + a * pltpu.roll(u, by, 0), u)
    a_ref[...] = a
    u_ref[...] = u
    edge = 0 if reverse else 7
    for j in range(ns):
        c_ref[j:j + 1, :] = a_ref[8 * j + edge:8 * j + edge + 1, :]
    at = c_ref[...]
    for j in range(ns):
        c_ref[j:j + 1, :] = u_ref[8 * j + edge:8 * j + edge + 1, :]
    ut = c_ref[...]
    srow = lax.broadcasted_iota(jnp.int32, (ns, D), 0)
    sh = 1
    while sh < ns:
        at, ut = combine(at, ut, srow, ns, sh)
        sh *= 2
    cv = carry_ref[0:1, :]
    ends = ut + at * cv
    last = 0 if reverse else ns - 1
    first = ns - 1 if reverse else 0
    c_ref[...] = jnp.where(srow == first, cv, pltpu.roll(ends, first if reverse else 1, 0))
    carry_ref[0:1, :] = jnp.sum(jnp.where(srow == last, ends, 0.0), axis=0, keepdims=True)
    for j in range(ns):
        sl = slice(8 * j, 8 * j + 8)
        out_ref[sl, :] = u_ref[sl, :] + a_ref[sl, :] * c_ref[j:j + 1, :]


def _lru_fwd(xr, proj5, ya, wab, ba, bx, lam, plan=None):
    s = xr.shape[0]
    t = min(256, s)

    def body(xr_ref, g_ref, ga_ref, gb_ref, ya_ref, wab_ref, ba_ref, bx_ref, lam_ref, h_ref, mg_ref, gr_ref,
             gi_ref, ao_ref, mo_ref, hc_ref, a_ref, u_ref, c_ref):
        @pl.when(pl.program_id(0) == 0)
        def _():
            hc_ref[...] = jnp.zeros_like(hc_ref)

        xrv = xr_ref[...]
        gr, gi, _, a, mult, _ = _lru_gates(xrv, wab_ref, ba_ref, bx_ref, lam_ref)
        gr_ref[...], gi_ref[...], ao_ref[...], mo_ref[...] = gr, gi, a, mult
        _blocked_scan(a, mult * gi * xrv, hc_ref, a_ref, u_ref, c_ref, h_ref, reverse=False)
        yb = h_ref[...] * _gelu(g_ref[...].astype(F32))
        mg_ref[...] = (_sig(ga_ref[...].astype(F32)) * ya_ref[...]
                       + _sig(gb_ref[...].astype(F32)) * yb).astype(mg_ref.dtype)

    return _pcall(
        body, (xr, proj5, proj5, proj5, ya, wab, ba, bx, lam), name="lru_fwd", grid=(s // t,),
        in_specs=[_row_spec(t, D), _row_spec(t, D, COL_G), _row_spec(t, D, COL_GA), _row_spec(t, D, COL_GB),
                  _row_spec(t, D), _full_spec((D, 2 * D), once=True), _full_spec((1, D)), _full_spec((1, D)),
                  _full_spec((1, D))],
        out_specs=[_row_spec(t, D)] * 6,
        out_shape=[_sds((s, D)), _sds((s, D), MXU)] + [_sds((s, D))] * 4,
        scratch_shapes=[pltpu.VMEM((8, D), F32), pltpu.VMEM((t, D), F32), pltpu.VMEM((t, D), F32),
                        pltpu.VMEM((t // 8, D), F32)],
        sem=("arbitrary",), plan=plan)


def _out_up_proj(merged, w_out, x, g2, g3, w_up):
    s = x.shape[0]
    t = min(ROWS_FUSED, s)

    def body(mg_ref, w_ref, x_ref, g2_ref, g3_ref, wu_ref, mix_ref, h1_ref, v_ref, pre_ref):
        mix = _mdot(mg_ref[...], w_ref[...])
        mix_ref[...] = mix
        h1 = x_ref[...] + mix * _rms(mix) * g2_ref[...]
        h1_ref[...] = h1
        v = (h1 * _rms(h1) * g3_ref[...]).astype(v_ref.dtype)
        v_ref[...] = v
        pre_ref[...] = _mdot(v, wu_ref[...]).astype(pre_ref.dtype)

    return pl.pallas_call(
        body, name="out_up_proj", grid=(s // t,),
        in_specs=[_row_spec(t, D), _full_spec((D, D), once=True), _row_spec(t, D), _full_spec((1, D)),
                  _full_spec((1, D)), _full_spec((D, DFF), once=True)],
        out_specs=[_row_spec(t, D), _row_spec(t, D), _row_spec(t, D), _row_spec(t, DFF)],
        out_shape=[_sds((s, D)), _sds((s, D)), _sds((s, D), MXU), _sds((s, DFF), MXU)],
        compiler_params=_cp("parallel"),
    )(merged, w_out, x, g2, g3, w_up)


def _down_loss(pre, w_down, h1, target, g4):
    s = pre.shape[0]
    t = min(ROWS_FUSED, s)

    def body(pre_ref, w_ref, h1_ref, tg_ref, g4_ref, dout_ref, dff_ref, loss_ref, dg4_ref):
        @pl.when(pl.program_id(0) == 0)
        def _():
            loss_ref[...] = jnp.zeros_like(loss_ref)
            dg4_ref[...] = jnp.zeros_like(dg4_ref)

        ff = _mdot(_relu2(pre_ref[...]), w_ref[...])
        r4 = _rms(ff)
        g4v = g4_ref[...]
        diff = h1_ref[...] + ff * r4 * g4v - tg_ref[...]
        sq = jnp.sum(jnp.sum(diff * diff, axis=1, keepdims=True), axis=0, keepdims=True)
        loss_ref[...] += (0.5 / D) * sq
        dout = diff * (1.0 / D)
        dout_ref[...] = dout
        dff, dg = _rms_bwd(ff, r4, g4v, dout)
        dff_ref[...] = dff.astype(dff_ref.dtype)
        dg4_ref[...] += dg

    return pl.pallas_call(
        body, name="down_loss", grid=(s // t,),
        in_specs=[_row_spec(t, DFF), _full_spec((DFF, D), once=True), _row_spec(t, D), _row_spec(t, D),
                  _full_spec((1, D))],
        out_specs=[_row_spec(t, D), _row_spec(t, D), _full_spec((1, 128)), _full_spec((1, D))],
        out_shape=[_sds((s, D)), _sds((s, D), MXU), _sds((1, 128)), _sds((1, D))],
        compiler_params=_cp("arbitrary"),
    )(pre, w_down, h1, target, g4)


def _dv_norms(dpre, w_up, h1, mix, dout, g3, g2, w_out):
    s = h1.shape[0]
    t = min(ROWS_FUSED, s)

    def body(dp_ref, w_ref, h1_ref, mix_ref, dout_ref, g3_ref, g2_ref, wo_ref, dh1_ref, dmix_ref, dmg_ref, dg3_ref,
             dg2_ref):
        @pl.when(pl.program_id(0) == 0)
        def _():
            dg3_ref[...] = jnp.zeros_like(dg3_ref)
            dg2_ref[...] = jnp.zeros_like(dg2_ref)

        dv = _mdot(dp_ref[...], w_ref[...], 1, 1)
        h1 = h1_ref[...]
        dh1n, dg3 = _rms_bwd(h1, _rms(h1), g3_ref[...], dv)
        dh1 = dout_ref[...] + dh1n
        dh1_ref[...] = dh1
        mix = mix_ref[...]
        dmix, dg2 = _rms_bwd(mix, _rms(mix), g2_ref[...], dh1)
        dmix = dmix.astype(dmix_ref.dtype)
        dmix_ref[...] = dmix
        dmg_ref[...] = _mdot(dmix, wo_ref[...], 1, 1)
        dg3_ref[...] += dg3
        dg2_ref[...] += dg2

    return pl.pallas_call(
        body, name="dv_norms", grid=(s // t,),
        in_specs=[_row_spec(t, DFF), _full_spec((D, DFF), once=True), _row_spec(t, D), _row_spec(t, D),
                  _row_spec(t, D), _full_spec((1, D)), _full_spec((1, D)), _full_spec((D, D), once=True)],
        out_specs=[_row_spec(t, D), _row_spec(t, D), _row_spec(t, D), _full_spec((1, D)), _full_spec((1, D))],
        out_shape=[_sds((s, D)), _sds((s, D), MXU), _sds((s, D)), _sds((1, D)), _sds((1, D))],
        compiler_params=_cp("arbitrary"),
    )(dpre, w_up, h1, mix, dout, g3, g2, w_out)


def _lru_bwd(dmerged, ya, xr, h, proj5, gates, wab_t, lam, plan=None):
    s = xr.shape[0]
    t = min(128, s)
    n = s // t
    rs = functools.partial(_rev_spec, t, D, n)

    def body(dm_ref, ya_ref, xr_ref, h_ref, hp_ref, g_ref, ga_ref, gb_ref, gr_ref, gi_ref, a_ref, m_ref, wab_ref,
             lam_ref, dya_ref, d3_ref, dxr_ref, dp2_ref, dlam_ref, dba_ref, dbx_ref, gc_ref, af_ref, an_ref, us_ref,
             c_ref, gs_ref):
        i = pl.program_id(0)

        @pl.when(i == 0)
        def _():
            gc_ref[...] = jnp.zeros_like(gc_ref)
            af_ref[...] = jnp.zeros_like(af_ref)
            dlam_ref[...] = jnp.zeros_like(dlam_ref)
            dba_ref[...] = jnp.zeros_like(dba_ref)
            dbx_ref[...] = jnp.zeros_like(dbx_ref)

        xrv = xr_ref[...]
        gr, gi, a, mult = gr_ref[...], gi_ref[...], a_ref[...], m_ref[...]
        sp = _softplus(-lam_ref[...])
        inv_mult = 1.0 / mult
        hv = h_ref[...]
        dm = dm_ref[...]
        sa = _sig(ga_ref[...].astype(F32))
        sb = _sig(gb_ref[...].astype(F32))
        gel, dgel = _gelu_and_grad(g_ref[...].astype(F32))
        dya = dm * sa
        dya_ref[...] = dya
        dyb = dm * sb
        dybh = dyb * hv
        d3_ref[:, 0:D] = (dybh * dgel).astype(d3_ref.dtype)
        d3_ref[:, D:2 * D] = (dya * ya_ref[...] * (1.0 - sa)).astype(d3_ref.dtype)
        d3_ref[:, 2 * D:3 * D] = (dybh * gel * (1.0 - sb)).astype(d3_ref.dtype)
        row = lax.broadcasted_iota(jnp.int32, (t, D), 0)
        an = jnp.where(row == t - 1, af_ref[0:1, :], pltpu.roll(a, t - 1, 0))
        _blocked_scan(an, dyb * gel, gc_ref, an_ref, us_ref, c_ref, gs_ref, reverse=True)
        gfull = gs_ref[...]
        af_ref[0:1, :] = jnp.sum(jnp.where(row == 0, a, 0.0), axis=0, keepdims=True)
        hlast = jnp.where(i == n - 1, 0.0, hp_ref[7:8, :])
        hprev = jnp.where(row == 0, hlast, pltpu.roll(hv, 1, 0))
        gx = gfull * xrv
        dgi = gx * mult
        dla = a * (gfull * hprev - gx * gi * a * inv_mult)
        dgr = dla * (-LRU_C * sp)
        dsp = jnp.sum(dla * (-LRU_C * gr), axis=0, keepdims=True)
        dlam_ref[...] += dsp * (-_sig(-lam_ref[...]))
        dpr = dgr * gr * (1.0 - gr)
        dpi = dgi * gi * (1.0 - gi)
        dp2_ref[:, 0:D] = dpr.astype(dp2_ref.dtype)
        dp2_ref[:, D:2 * D] = dpi.astype(dp2_ref.dtype)
        dba_ref[...] += jnp.sum(dpr, axis=0, keepdims=True)
        dbx_ref[...] += jnp.sum(dpi, axis=0, keepdims=True)
        dxr_ref[...] = gfull * mult * gi + _mdot(dp2_ref[...], wab_ref[...])

    hp_spec = pl.BlockSpec((8, D), lambda i: (jnp.maximum((n - 1 - i) * (t // 8) - 1, 0), 0))
    wide = lambda c: pl.BlockSpec((t, c), lambda i: (n - 1 - i, 0))
    return _pcall(
        body, (dmerged, ya, xr, h, h, proj5, proj5, proj5, *gates, wab_t, lam), name="lru_bwd", grid=(n,),
        in_specs=[rs(), rs(), rs(), rs(), hp_spec, rs(COL_G), rs(COL_GA), rs(COL_GB), rs(), rs(), rs(), rs(),
                  _full_spec((2 * D, D), once=True), _full_spec((1, D))],
        out_specs=[rs(), wide(3 * D), rs(), wide(2 * D), _full_spec((1, D)), _full_spec((1, D)), _full_spec((1, D))],
        out_shape=[_sds((s, D)), _sds((s, 3 * D), MXU), _sds((s, D)), _sds((s, 2 * D), MXU), _sds((1, D)),
                   _sds((1, D)), _sds((1, D))],
        scratch_shapes=[pltpu.VMEM((8, D), F32), pltpu.VMEM((8, D), F32), pltpu.VMEM((t, D), F32),
                        pltpu.VMEM((t, D), F32), pltpu.VMEM((t // 8, D), F32), pltpu.VMEM((t, D), F32)],
        sem=("arbitrary",), plan=plan)


def _ssd_bwd(dya, y, proj5, xbc_c, dt, states, a_log, dskip_x, ssm_norm, expand, reduce_, plan=None):
    s = xbc_c.shape[0]
    nc = s // CH
    rv = functools.partial(_rev_spec, CH)

    def body(dya_ref, y_ref, z_ref, xc_ref, dt_ref, st_ref, alog_ref, dsk_ref, ng_ref, e_ref, et_ref, dz_ref,
             dxc_ref, ddt_ref, dng_ref, ddsk_ref, dalog_ref, dh_ref, at_ref, dtt_ref, dat_ref, ddtt_ref, dy_ref,
             yoffdy_ref, xbds_ref):
        @pl.when(pl.program_id(0) == 0)
        def _():
            dh_ref[...] = jnp.zeros_like(dh_ref)
            dng_ref[...] = jnp.zeros_like(dng_ref)
            ddsk_ref[...] = jnp.zeros_like(ddsk_ref)
            dalog_ref[...] = jnp.zeros_like(dalog_ref)

        cs = _ssd_chunk_setup(dt_ref, alog_ref, e_ref, at_ref, dtt_ref)
        lane, row = cs["lane"], cs["row"]
        et = et_ref[...]
        for g in range(NG):
            gs = slice(GW * g, GW * (g + 1))
            yv = y_ref[:, gs]
            zv = z_ref[:, gs].astype(F32)
            sz = _silu(zv)
            yg = yv * sz
            dyav = dya_ref[:, gs]
            dyg, dng = _rms_bwd(yg, _rms(yg), ng_ref[:, gs], dyav)
            dng_ref[:, gs] += dng
            dy_ref[:, gs] = dyg * sz
            dz_ref[:, gs] = (dyg * yv * _dsilu(zv)).astype(dz_ref.dtype)
        dyv = dy_ref[...]
        xs = xc_ref[:, 0:D]
        ddsk_ref[...] += jnp.sum(dyv * xs, axis=0, keepdims=True)
        dxc_ref[:, 0:D] = dyv * dsk_ref[...]
        dat_ref[...] = jnp.zeros_like(dat_ref)
        ddtt_ref[...] = jnp.zeros_like(ddtt_ref)
        hh = jnp.sum(dh_ref[...] * st_ref[0], axis=0, keepdims=True)
        deal = jnp.max(_xdot(jnp.broadcast_to(hh, (8, D)), et, 3), axis=0, keepdims=True)
        d_acum = jnp.zeros((CH, 128), F32)
        for g in range(NG):
            gs = slice(GW * g, GW * (g + 1))
            bs_ = slice(D + NS * g, D + NS * (g + 1))
            cs_ = slice(D + NG * NS + NS * g, D + NG * NS + NS * (g + 1))
            bg = xc_ref[:, bs_]
            cg = xc_ref[:, cs_]
            cb = _mdot(cg, bg, 1, 1)
            hg = st_ref[0, :, gs]
            dhg = dh_ref[:, gs]
            dyg_ = dy_ref[:, gs]
            xsg = xc_ref[:, gs]
            ea = cs["ea_x"][:, gs]
            wsx = cs["ws_x"][:, gs]
            dp = dyg_ * ea
            yoffdy_ref[:, gs] = dp * _mdot(cg, hg)
            dc = _mdot(dp, hg, 1, 1)
            dhprev = _mdot(cg, dp, 0, 0)
            bds = _mdot(bg, dhg)
            dxc_ref[:, gs] += wsx * bds
            xbds_ref[:, gs] = xsg * bds
            db = _mdot(xsg * wsx, dhg, 1, 1)
            dh_ref[:, gs] = dhprev + cs["eal_x"][:, gs] * dhg
            dcbs = jnp.zeros((CH, CH), F32)
            for j in range(4 * g, 4 * g + 4):
                ps = slice(128 * j, 128 * (j + 1))
                xp = xc_ref[:, ps]
                dyp = dy_ref[:, ps]
                dxacc = jnp.zeros((CH, 128), F32)
                for hf in range(2):
                    hd = 2 * j + hf
                    ld, rowdt = _head_decay(cs, at_ref, dtt_ref, hd)
                    hm = (lane >= HP) if hf else (lane < HP)
                    dym = jnp.where(hm, dyp, 0.0)
                    w = cb * ld * rowdt
                    dw = _mdot(dym, jnp.where(hm, xp, 0.0), 1, 1)
                    dxacc = dxacc + _mdot(w, dym, 0, 0)
                    nm = dw * w
                    ddtt_ref[hd:hd + 1, :] += jnp.sum(dw * cb * ld, axis=0, keepdims=True)
                    d_acum = d_acum + jnp.where(lane == hd, jnp.sum(nm, axis=1, keepdims=True), 0.0)
                    dat_ref[hd:hd + 1, :] -= jnp.sum(nm, axis=0, keepdims=True)
                    dcbs = dcbs + dw * ld * rowdt
                dxc_ref[:, ps] += dxacc
            dxc_ref[:, bs_] = db + _mdot(dcbs, cg, 0, 0)
            dxc_ref[:, cs_] = dc + _mdot(dcbs, bg)
        dws = _xdot(xbds_ref[...], et, 2)
        ws = cs["ws"]
        d_acum = d_acum - dws * ws + _xdot(yoffdy_ref[...], et, 2) + dat_ref[...].T
        d_alast = jnp.sum(dws * ws, axis=0, keepdims=True) + deal * cs["eal"]
        d_acum = d_acum + jnp.where(row == CH - 1, d_alast, 0.0)
        triu = row <= lane
        d_adt = _xdot(triu.astype(F32), d_acum, 3, split_b=True)
        ddt_ref[...] = dws * jnp.exp(cs["alast"] - cs["acum"]) + ddtt_ref[...].T + d_adt * cs["a"]
        dalog_ref[...] += jnp.sum(d_adt * cs["dtv"], axis=0, keepdims=True) * cs["a"]

    return _pcall(
        body, (dya, y, proj5, xbc_c, dt, states, a_log, dskip_x, ssm_norm, expand, reduce_), name="ssd_bwd",
        grid=(nc,),
        in_specs=[rv(D, nc), rv(D, nc), rv(D, nc, COL_Z), rv(XBC, nc), rv(128, nc),
                  pl.BlockSpec((1, NS, D), lambda i: (nc - 1 - i, 0, 0)), _full_spec((1, 128)), _full_spec((1, D)),
                  _full_spec((1, D)), _full_spec((128, D)), _full_spec((D, 128))],
        out_specs=[rv(D, nc), rv(XBC, nc), rv(128, nc), _full_spec((1, D)), _full_spec((1, D)),
                   _full_spec((1, 128))],
        out_shape=[_sds((s, D), MXU), _sds((s, XBC)), _sds((s, 128)), _sds((1, D)), _sds((1, D)), _sds((1, 128))],
        scratch_shapes=[pltpu.VMEM((NS, D), F32), pltpu.VMEM((128, 128), F32), pltpu.VMEM((128, 128), F32),
                        pltpu.VMEM((128, 128), F32), pltpu.VMEM((128, 128), F32), pltpu.VMEM((CH, D), F32),
                        pltpu.VMEM((CH, D), F32), pltpu.VMEM((CH, D), F32)],
        sem=("arbitrary",), plan=plan)


def _conv_bwd(dxbc_c, dsilu, dxr, ddt, xbc_raw, proj5, dt_raw, cw_s, cw_l, dt_bias, plan=None):
    s = xbc_raw.shape[0]
    t = min(ROWS_CONV // 2, s)
    n = s // t

    def body(dxc_ref, dsl_ref, dxr_ref, ddt_ref, xs_ref, xl_ref, dtr_ref, cws_ref, cwl_ref, dtb_ref, dxs_ref,
             dxl_ref, ddtr_ref, dcws_ref, dcbs_ref, dcwl_ref, dcbl_ref, ddtb_ref, ds_ref, dl_ref):
        @pl.when(pl.program_id(0) == 0)
        def _():
            ds_ref[t:t + 8, :] = jnp.zeros((8, XBC), F32)
            dl_ref[t:t + 8, :] = jnp.zeros((8, D), F32)
            for r in (dcws_ref, dcbs_ref, dcwl_ref, dcbl_ref, ddtb_ref):
                r[...] = jnp.zeros_like(r)

        ds_ref[0:t, :] = dxc_ref[...] * dsl_ref[...].astype(F32)
        dl_ref[0:t, :] = dxr_ref[...]

        def back(dbuf, x_ref, w_ref, dx_ref, dw_ref, db_ref):
            xv = x_ref[...].astype(F32)
            dpre = dbuf[0:t, :]
            dx = w_ref[3:4, :] * dpre
            dw_ref[3:4, :] += jnp.sum(dpre * xv, axis=0, keepdims=True)
            db_ref[...] += jnp.sum(dpre, axis=0, keepdims=True)
            for k in (1, 2, 3):
                ahead = dbuf[k:t + k, :]
                dx = dx + w_ref[3 - k:4 - k, :] * ahead
                dw_ref[3 - k:4 - k, :] += jnp.sum(ahead * xv, axis=0, keepdims=True)
            dx_ref[...] = dx.astype(dx_ref.dtype)
            dbuf[t:t + 8, :] = dbuf[0:8, :]

        back(ds_ref, xs_ref, cws_ref, dxs_ref, dcws_ref, dcbs_ref)
        back(dl_ref, xl_ref, cwl_ref, dxl_ref, dcwl_ref, dcbl_ref)
        ddtr = ddt_ref[...] * _sig(dtr_ref[...] + dtb_ref[...])
        ddtr_ref[...] = ddtr.astype(ddtr_ref.dtype)
        ddtb_ref[...] += jnp.sum(ddtr, axis=0, keepdims=True)

    rv = functools.partial(_rev_spec, t)
    return _pcall(
        body, (dxbc_c, dsilu, dxr, ddt, xbc_raw, proj5, dt_raw, cw_s, cw_l, dt_bias), name="conv_bwd", grid=(n,),
        in_specs=[rv(XBC, n), rv(XBC, n), rv(D, n), rv(128, n), rv(XBC, n), rv(D, n, COL_XL), rv(128, n),
                  _full_spec((4, XBC)), _full_spec((4, D)), _full_spec((1, 128))],
        out_specs=[rv(XBC, n), rv(D, n), rv(128, n), _full_spec((4, XBC)), _full_spec((1, XBC)), _full_spec((4, D)),
                   _full_spec((1, D)), _full_spec((1, 128))],
        out_shape=[_sds((s, XBC), MXU), _sds((s, D), MXU), _sds((s, 128), MXU), _sds((4, XBC)), _sds((1, XBC)),
                   _sds((4, D)), _sds((1, D)), _sds((1, 128))],
        scratch_shapes=[pltpu.VMEM((t + 8, XBC), F32), pltpu.VMEM((t + 8, D), F32)],
        sem=("arbitrary",), plan=plan)


def _du_norm(d3, dz, dxl, dxbc, ddtr, w5, wxbc, wdt, x, dh1, g1, plan=None):
    s = x.shape[0]
    t = min(ROWS_FUSED, s)

    def body(d3_ref, dz_ref, dxl_ref, dxbc_ref, ddtr_ref, w5_ref, wx_ref, wd_ref, x_ref, dh1_ref, g1_ref, dx_ref,
             dg1_ref):
        @pl.when(pl.program_id(0) == 0)
        def _():
            dg1_ref[...] = jnp.zeros_like(dg1_ref)

        du = (_mdot(dxbc_ref[...], wx_ref[...]) + _mdot(ddtr_ref[...], wd_ref[...])
              + _mdot(d3_ref[...], w5_ref[0:3 * D, :])
              + _mdot(dz_ref[...], w5_ref[COL_Z * D:(COL_Z + 1) * D, :])
              + _mdot(dxl_ref[...], w5_ref[COL_XL * D:(COL_XL + 1) * D, :]))
        xv = x_ref[...]
        dxn, dg1 = _rms_bwd(xv, _rms(xv), g1_ref[...], du)
        dx_ref[...] = dh1_ref[...] + dxn
        dg1_ref[...] += dg1

    return _pcall(
        body, (d3, dz, dxl, dxbc, ddtr, w5, wxbc, wdt, x, dh1, g1), name="du_norm", grid=(s // t,),
        in_specs=[_row_spec(t, 3 * D), _row_spec(t, D), _row_spec(t, D), _row_spec(t, XBC), _row_spec(t, 128),
                  _full_spec((5 * D, D), once=True), _full_spec((XBC, D), once=True),
                  _full_spec((128, D), once=True), _row_spec(t, D), _row_spec(t, D), _full_spec((1, D))],
        out_specs=[_row_spec(t, D), _full_spec((1, D))],
        out_shape=[_sds((s, D)), _sds((1, D))],
        sem=("arbitrary",), plan=plan)


def _adamw(w, g, m, v, name):
    r, c = w.shape
    t = r
    if r * c > 256 * 1024:
        t = next(cand for cand in (512, 256, 128, 64, 32, 16, 8) if r % cand == 0 and cand * c <= 512 * 1024)
    bc1 = 1.0 - ADAM_B1 ** ADAM_STEP
    bc2 = 1.0 - ADAM_B2 ** ADAM_STEP

    def body(w_ref, g_ref, m_ref, v_ref, d_ref, nm_ref, nv_ref):
        gv = g_ref[...]
        nm = ADAM_B1 * m_ref[...] + (1.0 - ADAM_B1) * gv
        nv = ADAM_B2 * v_ref[...] + (1.0 - ADAM_B2) * (gv * gv)
        nm_ref[...] = nm
        nv_ref[...] = nv
        d_ref[...] = -ADAM_LR * ((nm / bc1) / (jnp.sqrt(nv / bc2) + ADAM_EPS) + ADAM_WD * w_ref[...])

    spec = pl.BlockSpec((t, c), lambda i: (i, 0))
    return pl.pallas_call(
        body, name=name, grid=(r // t,), in_specs=[spec] * 4, out_specs=[spec] * 3,
        out_shape=[_sds((r, c))] * 3, compiler_params=_cp("parallel"),
    )(w, g, m, v)


def _half_blocks(shape, axis):
    r, c = shape
    if axis == 0:
        t = 256 if (r // 2) % 256 == 0 else 128
        nb = (r // 2) // t
        return (t, c), nb, (lambda i: (i, 0)), (lambda i: (i % nb, 0))
    nb = (c // 2) // 128
    return (r, 128), nb, (lambda i: (0, i)), (lambda i: (0, i % nb))


def _adamw_halves(w, g_mine, g_other, m, v, cidx, name, axis=0):
    r, c = w.shape
    blk, nb, whole, part = _half_blocks(w.shape, axis)
    bc1 = 1.0 - ADAM_B1 ** ADAM_STEP
    bc2 = 1.0 - ADAM_B2 ** ADAM_STEP

    def body(c_ref, w_ref, gm_ref, go_ref, m_ref, v_ref, g_ref, d_ref, nm_ref, nv_ref):
        mine = (pl.program_id(0) // nb) == c_ref[0]
        gv = jnp.where(mine, gm_ref[...], go_ref[...])
        g_ref[...] = gv
        nm = ADAM_B1 * m_ref[...] + (1.0 - ADAM_B1) * gv
        nv = ADAM_B2 * v_ref[...] + (1.0 - ADAM_B2) * (gv * gv)
        nm_ref[...] = nm
        nv_ref[...] = nv
        d_ref[...] = -ADAM_LR * ((nm / bc1) / (jnp.sqrt(nv / bc2) + ADAM_EPS) + ADAM_WD * w_ref[...])

    spec = pl.BlockSpec(blk, lambda i, c_ref: whole(i))
    half = pl.BlockSpec(blk, lambda i, c_ref: part(i))
    return pl.pallas_call(
        body, name=name,
        grid_spec=pltpu.PrefetchScalarGridSpec(num_scalar_prefetch=1, grid=(2 * nb,),
                                               in_specs=[spec, half, half, spec, spec], out_specs=[spec] * 4),
        out_shape=[_sds((r, c))] * 4, compiler_params=_cp("parallel"),
    )(cidx, w, g_mine, g_other, m, v)


SC_TILES = 32
SC_ROWS = 8


def _adamw_sc(w, g_lo, g_hi, m, v, name):
    r, c = w.shape
    rows = r // SC_TILES
    bc1 = 1.0 - ADAM_B1 ** ADAM_STEP
    bc2 = 1.0 - ADAM_B2 ** ADAM_STEP

    def body(w_hbm, glo_hbm, ghi_hbm, m_hbm, v_hbm, go_hbm, d_hbm, nm_hbm, nv_hbm, wb, gb, mb, vb, db):
        tile = lax.axis_index("subcore") * 2 + lax.axis_index("core")

        @pl.loop(0, rows, step=SC_ROWS)
        def _(r0):
            row = tile * rows + r0
            sl = pl.ds(row, SC_ROWS)

            @pl.when(tile < SC_TILES // 2)
            def _():
                pltpu.sync_copy(glo_hbm.at[sl], gb)

            @pl.when(tile >= SC_TILES // 2)
            def _():
                pltpu.sync_copy(ghi_hbm.at[pl.ds(row - r // 2, SC_ROWS)], gb)

            pltpu.sync_copy(w_hbm.at[sl], wb)
            pltpu.sync_copy(m_hbm.at[sl], mb)
            pltpu.sync_copy(v_hbm.at[sl], vb)
            for rr in range(SC_ROWS):
                @pl.loop(0, c, step=16)
                def _(i):
                    cs = pl.ds(i, 16)
                    gv = gb[rr, cs]
                    nm = ADAM_B1 * mb[rr, cs] + (1.0 - ADAM_B1) * gv
                    nv = ADAM_B2 * vb[rr, cs] + (1.0 - ADAM_B2) * (gv * gv)
                    mb[rr, cs] = nm
                    vb[rr, cs] = nv
                    db[rr, cs] = -ADAM_LR * ((nm / bc1) / (jnp.sqrt(nv / bc2) + ADAM_EPS) + ADAM_WD * wb[rr, cs])

            pltpu.sync_copy(gb, go_hbm.at[sl])
            pltpu.sync_copy(db, d_hbm.at[sl])
            pltpu.sync_copy(mb, nm_hbm.at[sl])
            pltpu.sync_copy(vb, nv_hbm.at[sl])

    return pl.kernel(
        body, name=name, out_type=[_sds((r, c))] * 4,
        mesh=plsc.VectorSubcoreMesh(core_axis_name="core", subcore_axis_name="subcore"),
        scratch_types=[pltpu.VMEM((SC_ROWS, c), F32)] * 5,
    )(w, g_lo, g_hi, m, v)


def _block_diag(w):
    eye = jnp.eye(NH, dtype=w.dtype)
    return (w[:, :, None, :] * eye[:, None, :, None]).reshape(D, D)


def _diag_blocks(full):
    eye = jnp.eye(NH, dtype=full.dtype)
    return (full.reshape(NH, HP, NH, HP) * eye[:, None, :, None]).sum(axis=2)


def _pad_lanes(v, n=128):
    return jnp.pad(v, ((0, 0), (0, n - v.shape[1])))


def _local_step(x, target, p, dist=None):
    heads = jnp.arange(D, dtype=jnp.int32) // HP
    expand = (jnp.arange(128, dtype=jnp.int32)[:, None] == heads[None, :]).astype(F32)
    reduce_ = expand.T
    dskip_x = jnp.repeat(p["d_skip"], HP, axis=1)
    a_log = _pad_lanes(p["a_log"])
    dt_bias = _pad_lanes(p["dt_bias"])
    wab = jnp.concatenate([_block_diag(p["lru_wa"]), _block_diag(p["lru_wx"])], axis=1).astype(MXU)
    ba = p["lru_ba"].reshape(1, D)
    bx = p["lru_bx"].reshape(1, D)

    def hosted(key, fn):
        plan = dist.plan(key) if dist is not None else None
        if plan is None:
            return fn(plan=None)
        outs, got = fn(plan=plan)
        dist.done(key, got, p)
        return outs

    u = hosted("norm_u", functools.partial(_norm_cast, x, p["norm_mix_pre"], "norm_u"))
    w5, wxbc, wdt = p["w5"], p["wxbc"], p["wdt"]
    proj5 = hosted("proj5", functools.partial(_matmul, u, w5, name="proj5", tb=True, tm=1024, out_dtype=MXU))
    xbc_raw = _matmul(u, wxbc, name="proj_xbc", tb=True, tn=XBC)
    dt_raw = _matmul(u, wdt, name="proj_dt", tb=True)
    xbc_c, dsilu, xr, dt = _conv_fwd(xbc_raw, proj5, dt_raw, p["conv_ssm_w"], p["conv_ssm_b"], p["conv_lru_w"],
                                     p["conv_lru_b"], dt_bias)
    y, ya, states = _ssd_fwd(xbc_c, dt, proj5, a_log, dskip_x, p["ssm_norm"], expand)
    h, merged, *gates = hosted("lru_fwd", functools.partial(_lru_fwd, xr, proj5, ya, wab, ba, bx, p["lru_lambda"]))
    mix, h1, v, pre = _out_up_proj(merged, p["w_out"], x, p["norm_mix_post"], p["norm_mlp_pre"], p["w_up"])
    dout, dff, loss, dg4 = _down_loss(pre, p["w_down"], h1, target, p["norm_mlp_post"])

    dpre = _matmul(dff, p["w_down"], name="d_pre", tb=True, tm=1024, out_dtype=MXU,
                   epi=lambda r, pr: r * (2.0 * jnp.maximum(pr.astype(F32), 0.0)), epi_args=(pre,))
    g_w_down = _matmul(pre, dff, name="dw_down", ta=True, tm=1024, tn=1024, tk=TK_GRAD, a_fn=_relu2)
    dh1, dmix, dmerged, dg3, dg2 = _dv_norms(dpre, p["w_up"], h1, mix, dout, p["norm_mlp_pre"], p["norm_mix_post"],
                                             p["w_out"])
    g_w_up = _matmul(v, dpre, name="dw_up", ta=True, tm=1024, tn=1024, tk=TK_GRAD, col_blocks_major=True)
    g_w_out = _matmul(merged, dmix, name="dw_out", ta=True, tm=1024, tn=1024, tk=TK_GRAD)
    if dist is not None:
        dist.early_grads(w_down=g_w_down, w_up=g_w_up, w_out=g_w_out)
    dya, d3, dxr, dp2, dlam, dba, dbx = hosted("lru_bwd", functools.partial(
        _lru_bwd, dmerged, ya, xr, h, proj5, gates, wab.T, p["lru_lambda"]))
    g_wab = _matmul(xr, dp2, name="dw_lru", ta=True, tm=1024, tn=1024, tk=TK_GRAD)
    g_wa, g_wx = _diag_blocks(g_wab[:, :D]), _diag_blocks(g_wab[:, D:])
    dz, dxbc_c, ddt, dng, ddsk, dalog = hosted("ssd_bwd", functools.partial(
        _ssd_bwd, dya, y, proj5, xbc_c, dt, states, a_log, dskip_x, p["ssm_norm"], expand, reduce_))
    (dxbc, dxl, ddtr, dcws, dcbs, dcwl, dcbl, ddtb) = hosted("conv_bwd", functools.partial(
        _conv_bwd, dxbc_c, dsilu, dxr, ddt, xbc_raw, proj5, dt_raw, p["conv_ssm_w"], p["conv_lru_w"], dt_bias))
    gw3 = _matmul(d3, u, name="dw_in_lru", ta=True, tm=1024, tn=1024, tk=TK_GRAD)
    gwz = _matmul(dz, u, name="dw_in_z", ta=True, tm=1024, tn=1024, tk=TK_GRAD)
    gwxl = _matmul(dxl, u, name="dw_in_xl", ta=True, tm=1024, tn=1024, tk=TK_GRAD)
    gwxbc = _matmul(dxbc, u, name="dw_in_xbc", ta=True, tm=XBC, tn=1024, tk=TK_GRAD)
    gwdt = _matmul(ddtr, u, name="dw_in_dt", ta=True, tm=128, tn=1024, tk=TK_GRAD)
    g_w_in_t = jnp.concatenate([gwz, gwxbc, gwdt[:NH], gw3[:D], gwxl, gw3[D:2 * D], gw3[2 * D:]], axis=0)
    grads = {
        "w_in_t": g_w_in_t, "conv_ssm_w": dcws, "conv_ssm_b": dcbs, "dt_bias": ddtb[:, :NH],
        "a_log": dalog[:, :NH], "d_skip": ddsk.reshape(NH, HP).sum(axis=1)[None, :], "ssm_norm": dng,
        "conv_lru_w": dcwl, "conv_lru_b": dcbl, "lru_wa": g_wa, "lru_ba": dba.reshape(NH, HP), "lru_wx": g_wx,
        "lru_bx": dbx.reshape(NH, HP), "lru_lambda": dlam, "w_out": g_w_out, "norm_mix_post": dg2,
        "norm_mlp_pre": dg3, "w_up": g_w_up, "w_down": g_w_down, "norm_mlp_post": dg4,
    }
    if dist is not None:
        dist.late_grads(grads, loss[0, 0])
    grad_x, grads["norm_mix_pre"] = hosted("du_norm", functools.partial(
        _du_norm, d3, dz, dxl, dxbc, ddtr, w5, wxbc, wdt, x, dh1, p["norm_mix_pre"]))
    return loss[0, 0], grad_x, grads


def _split_w_in_t(w_in_t):
    z, xbc, dtc, g, xl, ga, gb = jnp.split(w_in_t, [D, D + XBC, D + XBC + NH, 2 * D + XBC + NH,
                                                    3 * D + XBC + NH, 4 * D + XBC + NH], axis=0)
    return jnp.concatenate([g, ga, gb, z, xl], axis=0), xbc, jnp.pad(dtc, ((0, 128 - NH), (0, 0)))


COMM = BF16


def _place():
    x, y, c = lax.axis_index("x"), lax.axis_index("y"), lax.axis_index("c")
    chips = [(1 - x, y), (x, 1 - y), (1 - x, 1 - y)]
    return x, y, c, chips


def _remote(src, dst, send_sem, recv_sem, to):
    return pltpu.make_async_remote_copy(src_ref=src, dst_ref=dst, send_sem=send_sem, recv_sem=recv_sem, device_id=to,
                                        device_id_type=MESH)


def _gather_plan(big, small=(), axes=None):
    nb = len(big)
    arrs = list(big) + list(small)
    na = len(arrs)
    axes = list(axes or [0] * nb)

    def half(ref, a, k, which):
        h = arrs[a].shape[axes[a]] // 2
        cut = (pl.ds(which * h, h),) if axes[a] == 0 else (slice(None), pl.ds(which * h, h))
        return ref.at[cut] if k is None else ref.at[(k,) + cut]

    def direct(ins, outs, send, recv):
        x, y, c, chips = _place()
        k = 2 * x + y
        cps = []
        for a in range(na):
            src, dst = (half(ins[a], a, None, c), half(outs[a], a, k, c)) if a < nb else (ins[a], outs[a].at[k])
            cps += [_remote(src, dst, send.at[a, j], recv.at[a, j], (cx, cy, c)) for j, (cx, cy) in enumerate(chips)]
        return cps

    def passed(outs, send, recv):
        x, y, c, chips = _place()
        cps = []
        for j, (cx, cy) in enumerate(chips):
            for a in range(nb):
                got = half(outs[a], a, 2 * cx + cy, c)
                cps.append(_remote(got, got, send.at[a, 3 + j], recv.at[a, 3 + j], (x, y, 1 - c)))
        return cps

    def start(ins, outs, sems):
        for cp in direct(ins, outs, *sems[0]):
            cp.start()

    def mid(ins, outs, sems):
        send, recv = sems[0]
        _, _, c, chips = _place()
        fwd = passed(outs, send, recv)
        for j, (cx, cy) in enumerate(chips):
            kj = 2 * cx + cy
            for a in range(na):
                got = half(outs[a], a, kj, c) if a < nb else outs[a].at[kj]
                _remote(got, got, send.at[a, j], recv.at[a, j], (cx, cy, c)).wait_recv()
                if a < nb:
                    fwd[j * nb + a].start()

    def finish(ins, outs, sems):
        send, recv = sems[0]
        x, y, c, chips = _place()
        for j, (cx, cy) in enumerate(chips):
            for a in range(nb):
                got = half(outs[a], a, 2 * cx + cy, 1 - c)
                _remote(got, got, send.at[a, 3 + j], recv.at[a, 3 + j], (x, y, 1 - c)).wait_recv()
        for cp in direct(ins, outs, send, recv) + passed(outs, send, recv):
            cp.wait_send()

    return _Plan(arrs, [_sds((NCHIP,) + a.shape, a.dtype) for a in arrs], [(na, 6)], start, finish, mid)


def _own_shards(gathered, shards):
    kchip = 2 * lax.axis_index("x") + lax.axis_index("y")
    return [lax.dynamic_update_index_in_dim(o, a, kchip, 0) for o, a in zip(gathered, shards)]


def _swap_plan(ins, outs, sems, copies):
    def start(i, o, s):
        for cp in copies(i, o, *s[0]):
            cp.start()

    def finish(i, o, s):
        for cp in copies(i, o, *s[0]):
            cp.wait()

    return _Plan(ins, outs, [sems], start, finish)


def _half_shape(shape, axis):
    return tuple(d // 2 if i == axis else d for i, d in enumerate(shape))


def _pair_exchange_plan(gs, axis=1):
    def copies(ins, outs, send, recv):
        x, y, c, _ = _place()
        cps = []
        for a in range(len(gs)):
            h = ins[a].shape[axis] // 2
            theirs = pl.ds((1 - c) * h, h)
            src = ins[a].at[:, theirs] if axis == 1 else ins[a].at[:, :, theirs]
            cps.append(_remote(src, outs[a], send.at[a], recv.at[a], (x, y, 1 - c)))
        return cps

    return _swap_plan(gs, [_sds(_half_shape(g.shape, axis), g.dtype) for g in gs], (len(gs),), copies)


def _pair_add(g, got, cidx, name, axis=1):
    half = _half_shape(g.shape, axis)
    blk, nt, _, part = _half_blocks(g.shape[1:], axis - 1)

    def body(c_ref, g_ref, o_ref, p_ref, pc_ref):
        sm = g_ref[...] + o_ref[...]
        p_ref[...] = sm
        pc_ref[...] = sm.astype(pc_ref.dtype)

    def mine(k, i, c_ref):
        j = c_ref[0] * nt + i
        return (k, j, 0) if axis == 1 else (k, 0, j)

    spec = pl.BlockSpec((1,) + blk, lambda k, i, c_ref: (k,) + part(i))
    return pl.pallas_call(
        body, name=name,
        grid_spec=pltpu.PrefetchScalarGridSpec(
            num_scalar_prefetch=1, grid=(g.shape[0], nt),
            in_specs=[pl.BlockSpec((1,) + blk, mine), spec], out_specs=[spec, spec]),
        out_shape=[_sds(half), _sds(half, COMM)],
        compiler_params=_cp("parallel", "parallel"),
    )(cidx, g, got)


def _chip_exchange_plan(ps):
    def copies(ins, outs, send, recv):
        _, _, c, chips = _place()
        return [_remote(ins[a].at[2 * cx + cy], outs[a].at[j], send.at[a, j], recv.at[a, j], (cx, cy, c))
                for a in range(len(ps)) for j, (cx, cy) in enumerate(chips)]

    return _swap_plan(ps, [_sds((NCHIP - 1,) + p.shape[1:], p.dtype) for p in ps], (len(ps), 3), copies)


def _shard_sum(p, got, kidx, name, axis=1):
    full = tuple(2 * d if i == axis - 1 else d for i, d in enumerate(p.shape[1:]))
    blk, nt, _, part = _half_blocks(full, axis - 1)

    def body(k_ref, p_ref, g_ref, o_ref):
        sm = p_ref[0]
        for j in range(NCHIP - 1):
            sm = sm + g_ref[j].astype(F32)
        o_ref[...] = sm

    return pl.pallas_call(
        body, name=name,
        grid_spec=pltpu.PrefetchScalarGridSpec(
            num_scalar_prefetch=1, grid=(nt,),
            in_specs=[pl.BlockSpec((1,) + blk, lambda i, k_ref: (k_ref[0],) + part(i)),
                      pl.BlockSpec((NCHIP - 1,) + blk, lambda i, k_ref: (0,) + part(i))],
            out_specs=pl.BlockSpec(blk, lambda i, k_ref: part(i))),
        out_shape=_sds(p.shape[1:]),
        compiler_params=_cp("parallel"),
    )(kidx, p, got)


def _pair_swap_plan(rs):
    def copies(ins, outs, send, recv):
        x, y, c, _ = _place()
        return [_remote(ins[a], outs[a], send.at[a], recv.at[a], (x, y, 1 - c)) for a in range(len(rs))]

    return _swap_plan(rs, [_sds(r.shape, r.dtype) for r in rs], (len(rs),), copies)


def _allgather8_plan(v):
    def pieces(ins, outs, send, recv):
        x, y, c, chips = _place()
        me, sibling = (x, y, c), (x, y, 1 - c)

        def copy(k, block, to, src=None):
            px, py, pc = block
            slot = outs[0].at[4 * px + 2 * py + pc]
            return _remote(slot if src is None else src, slot, send.at[k], recv.at[k], to)

        first = [copy(0, me, sibling, src=ins[0])] + [copy(1 + j, me, (*chip, c), src=ins[0])
                                                      for j, chip in enumerate(chips)]
        passed = [copy(4 + j, (*chip, c), sibling) for j, chip in enumerate(chips)]
        arrivals = [copy(1 + j, (*chip, c), me) for j, chip in enumerate(chips)]
        late = [copy(0, sibling, me)] + [copy(4 + j, (*chip, 1 - c), me) for j, chip in enumerate(chips)]
        return first, passed, arrivals, late

    def start(ins, outs, sems):
        for cp in pieces(ins, outs, *sems[0])[0]:
            cp.start()

    def mid(ins, outs, sems):
        _, passed, arrivals, _ = pieces(ins, outs, *sems[0])
        for got, fwd in zip(arrivals, passed):
            got.wait_recv()
            fwd.start()

    def finish(ins, outs, sems):
        first, passed, _, late = pieces(ins, outs, *sems[0])
        for got in late:
            got.wait_recv()
        for cp in first + passed:
            cp.wait_send()

    return _Plan([v], [_sds((8,) + v.shape, v.dtype)], [(7,)], start, finish, mid)


def _own_block(gathered, v):
    me = 4 * lax.axis_index("x") + 2 * lax.axis_index("y") + lax.axis_index("c")
    return lax.dynamic_update_index_in_dim(gathered, v, me, 0)


def _sum_devices(allv, name):
    _, r, _ = allv.shape

    def body(a_ref, o_ref):
        sm = a_ref[0]
        for d in range(1, 8):
            sm = sm + a_ref[d]
        o_ref[...] = sm

    return pl.pallas_call(
        body, name=name, grid=(1,), in_specs=[_full_spec((8, r, 128))], out_specs=_full_spec((r, 128)),
        out_shape=_sds((r, 128)), compiler_params=_cp("arbitrary"),
    )(allv)


def _pack(arrs):
    flat = jnp.concatenate([a.reshape(-1) for a in arrs])
    return jnp.pad(flat, (0, (-flat.shape[0]) % 1024)).reshape(-1, 128)


def _unpack(packed, shapes):
    flat, outs, off = packed.reshape(-1), [], 0
    for shp in shapes:
        n = math.prod(shp)
        outs.append(flat[off:off + n].reshape(shp))
        off += n
    return outs


BIG = ("w_in", "w_out", "w_up", "w_down")
CONV = ("conv_ssm_w", "conv_lru_w")
WEIGHTS = ("norm_mix_pre", "w_in", "conv_ssm_w", "conv_ssm_b", "dt_bias", "a_log", "d_skip", "ssm_norm", "conv_lru_w",
           "conv_lru_b", "lru_wa", "lru_ba", "lru_wx", "lru_bx", "lru_lambda", "w_out", "norm_mix_post",
           "norm_mlp_pre", "w_up", "w_down", "norm_mlp_post")
SMALL = tuple(n for n in WEIGHTS if n not in BIG and n not in CONV)
EARLY = ("w_down", "w_up", "w_out")


def _cat_cols(g):
    return jnp.concatenate([g[k] for k in range(NCHIP)], axis=1)


class _Dist:
    def __init__(self, shards, first, cidx, kidx, wmv):
        self.shards, self.first, self.cidx, self.kidx, self.wmv = shards, first, cidx, kidx, wmv
        self.updates = {}

    def early_grads(self, w_down, w_up, w_out):
        self.shard_major = [w_down.reshape(NCHIP, D, D), w_up, w_out.reshape(NCHIP, D // NCHIP, D)]

    def late_grads(self, grads, loss):
        g_in = grads["w_in_t"][None]
        got, = _run_plan(_pair_exchange_plan([g_in], axis=2), "grad_pair_exchange_w_in")
        p_all, pc_all = _pair_add(g_in, got, self.cidx, "grad_pair_add_w_in", axis=2)
        self.p_in = lax.dynamic_slice_in_dim(p_all[0], self.kidx[0] * W_IN_SHARD, W_IN_SHARD, axis=0)[None]
        self.pc_in = pc_all.reshape(NCHIP, W_IN_SHARD, D // 2)
        self.small_names = [n for n in SMALL + CONV if n != "norm_mix_pre"]
        self.small_shapes = [grads[n].shape for n in self.small_names] + [(1,)]
        self.packed_small = _pack([grads[n] for n in self.small_names] + [loss.reshape(1)])

    def plan(self, key):
        if key == "norm_u":
            return _gather_plan(self.first[:1], self.first[1:], axes=[1])
        if key == "proj5":
            return _gather_plan([self.shards["w_out"], self.shards["w_up"]])
        if key == "lru_fwd":
            return _gather_plan([self.shards["w_down"]])
        if key == "lru_bwd":
            return _pair_exchange_plan(self.shard_major)
        if key == "ssd_bwd":
            return _chip_exchange_plan([pc for _, pc in self.pair])
        if key == "conv_bwd":
            return _pair_swap_plan(self.mine)
        if key == "du_norm":
            return _merge_plans(_allgather8_plan(self.packed_small), _chip_exchange_plan([self.pc_in]))
        return None

    def done(self, key, got, p):
        if key == "norm_u":
            g_in, g_cs, g_cl = _own_shards(got, self.first)
            w5, wxbc, wdt = _split_w_in_t(g_in.reshape(W_IN_COLS, D))
            p.update(w5=w5, wxbc=wxbc, wdt=wdt, conv_ssm_w=_cat_cols(g_cs), conv_lru_w=_cat_cols(g_cl))
        elif key == "proj5":
            g_out, g_up = _own_shards(got, [self.shards["w_out"], self.shards["w_up"]])
            p.update(w_out=g_out.reshape(D, D), w_up=_cat_cols(g_up))
        elif key == "lru_fwd":
            g_down, = _own_shards(got, [self.shards["w_down"]])
            p.update(w_down=g_down.reshape(DFF, D))
        elif key == "lru_bwd":
            self.pair = [_pair_add(gs, o, self.cidx, f"grad_pair_add_{n}")
                         for gs, o, n in zip(self.shard_major, got, EARLY)]
        elif key == "ssd_bwd":
            self.mine = [_shard_sum(pf, o, self.kidx, f"grad_shard_sum_{n}")
                         for (pf, _), o, n in zip(self.pair, got, EARLY)]
        elif key == "conv_bwd":
            w, m, v = self.wmv
            first = self.cidx[0] == 0
            for n, mine, other in zip(EARLY, self.mine, got):
                g_lo, g_hi = jnp.where(first, mine, other), jnp.where(first, other, mine)
                self.updates[n] = _adamw_sc(w[n], g_lo, g_hi, m[n], v[n], f"adamw_sc_{n}")
        elif key == "du_norm":
            self.all_small, self.from_chips_in = got


def kernel(x, norm_mix_pre, w_in, conv_ssm_w, conv_ssm_b, dt_bias, a_log, d_skip, ssm_norm, conv_lru_w, conv_lru_b, lru_wa, lru_ba, lru_wx, lru_bx, lru_lambda, w_out, norm_mix_post, norm_mlp_pre, w_up, w_down, norm_mlp_post, loss_target, m_norm_mix_pre, m_w_in, m_conv_ssm_w, m_conv_ssm_b, m_dt_bias, m_a_log, m_d_skip, m_ssm_norm, m_conv_lru_w, m_conv_lru_b, m_lru_wa, m_lru_ba, m_lru_wx, m_lru_bx, m_lru_lambda, m_w_out, m_norm_mix_post, m_norm_mlp_pre, m_w_up, m_w_down, m_norm_mlp_post, v_norm_mix_pre, v_w_in, v_conv_ssm_w, v_conv_ssm_b, v_dt_bias, v_a_log, v_d_skip, v_ssm_norm, v_conv_lru_w, v_conv_lru_b, v_lru_wa, v_lru_ba, v_lru_wx, v_lru_bx, v_lru_lambda, v_w_out, v_norm_mix_post, v_norm_mlp_pre, v_w_up, v_w_down, v_norm_mlp_post):
    args = locals()
    w = {n: args[n][0] for n in WEIGHTS}
    m = {n: args["m_" + n][0] for n in WEIGHTS}
    v = {n: args["v_" + n][0] for n in WEIGHTS}
    cidx = lax.axis_index("c").astype(jnp.int32).reshape(1)
    kchip = 2 * lax.axis_index("x") + lax.axis_index("y")
    to_t = lambda a: jnp.transpose(a, (2, 0, 1)).reshape(W_IN_SHARD, D)
    from_t = lambda a: jnp.transpose(a.reshape(W_IN_SHARD, 1, D), (1, 2, 0))
    shards = {n: (to_t(w_in) if n == "w_in" else w[n]).astype(MXU) for n in BIG}
    dist = _Dist(shards, [shards["w_in"], w["conv_ssm_w"], w["conv_lru_w"]], cidx, kchip.astype(jnp.int32).reshape(1),
                 (w, m, v))
    p = {n: (w[n].reshape(1, -1) if w[n].ndim == 1 else w[n]) for n in SMALL}

    _, grad_x, g = _local_step(x[0], loss_target[0], p, dist)

    half_in = _shard_sum(dist.p_in, dist.from_chips_in, jnp.zeros((1,), jnp.int32), "grad_shard_sum_w_in", axis=2)
    packed_g1 = _pack([g["norm_mix_pre"]])
    all_g1, other_in = _run_plan(_merge_plans(_allgather8_plan(packed_g1), _pair_swap_plan([half_in])),
                                 "grad_pair_swap_w_in")

    reduced = {}
    *summed, loss = _unpack(_sum_devices(_own_block(dist.all_small, dist.packed_small), "small_sum"),
                            dist.small_shapes)
    loss = loss.reshape(())
    g1, = _unpack(_sum_devices(_own_block(all_g1, packed_g1), "small_sum_norm_mix_pre"), [g["norm_mix_pre"].shape])
    for n, s in zip(dist.small_names + ["norm_mix_pre"], summed + [g1]):
        if n in CONV:
            width = w[n].shape[1]
            reduced[n] = lax.dynamic_slice_in_dim(s, kchip * width, width, axis=1)
        else:
            reduced[n] = s.reshape(w[n].shape)

    delta, new_m, new_v = {}, {}, {}
    for n in EARLY:
        reduced[n], delta[n], new_m[n], new_v[n] = dist.updates[n]
    outs_t = _adamw_halves(to_t(w_in), half_in, other_in, to_t(m_w_in), to_t(v_w_in), cidx, "adamw_w_in", axis=1)
    for d, o in zip((reduced, delta, new_m, new_v), outs_t):
        d["w_in"] = from_t(o)[0]
    for n in CONV:
        delta[n], new_m[n], new_v[n] = _adamw(w[n], reduced[n], m[n], v[n], f"adamw_{n}")
    shapes = [w[n].shape for n in SMALL]
    packed = [_pack([d[n] for n in SMALL]) for d in (w, reduced, m, v)]
    for d, out in zip((delta, new_m, new_v), _adamw(*packed, "adamw_small")):
        d.update(zip(SMALL, _unpack(out, shapes)))

    lead = lambda d: [d[n][None] for n in WEIGHTS]
    return (loss, grad_x[None], *lead(reduced), *lead(delta), *lead(new_m), *lead(new_v))
```
